```python
import jax, jax.numpy as jnp
from jax import lax
import numpy as np

D_MODEL = 1024
BATCH = 8
SEQ = 2048
DEPTH = 4

HEAD_DIM = 64
CONV_HEADS = 4
ATTN_HEADS = 8
SGU_HEADS = 4
CONV_W = CONV_HEADS * HEAD_DIM
ATTN_W = ATTN_HEADS * HEAD_DIM
SGU_W = SGU_HEADS * HEAD_DIM
D_MIX = CONV_W + ATTN_W + SGU_W
D_IN_PROJ = 3 * CONV_W + 3 * ATTN_W + 2 * SGU_W
CONV_WIDTH = 3
Q_BLOCK = 128
CHUNK = 128
D_FF = 4 * D_MODEL
PLE_DIM = 256
EPS = 1e-6

kernel_name = "hybrid_conv_stickbreak_sgu_block"


def rms_norm(x, g):
    xf = x.astype(jnp.float32)
    y = xf * lax.rsqrt(jnp.mean(xf * xf, axis=-1, keepdims=True) + EPS)
    return (y * g.astype(jnp.float32)).astype(x.dtype)


def short_conv(x, w):
    s = x.shape[1]
    xp = jnp.pad(x, ((0, 0), (CONV_WIDTH - 1, 0), (0, 0)))
    out = w[0] * xp[:, 0:s]
    for j in range(1, CONV_WIDTH):
        out = out + w[j] * xp[:, j:j + s]
    return out


def stick_breaking_attention(q, k, v):
    s_len = q.shape[1]
    scale = HEAD_DIM ** -0.5
    outs = []
    for qb in range(s_len // Q_BLOCK):
        start = qb * Q_BLOCK
        end = start + Q_BLOCK
        qi = q[:, start:end].astype(jnp.float32)
        kj = k[:, :end].astype(jnp.float32)
        vj = v[:, :end].astype(jnp.float32)
        z = jnp.einsum('bqhd,bkhd->bhqk', qi, kj) * scale
        t_pos = start + jnp.arange(Q_BLOCK)[:, None]
        s_pos = jnp.arange(end)[None, :]
        causal = s_pos < t_pos
        log_beta = jax.nn.log_sigmoid(z)
        log_rem = jnp.where(causal, jax.nn.log_sigmoid(-z), 0.0)
        suffix = lax.cumsum(log_rem, axis=3, reverse=True) - log_rem
        weights = jnp.where(causal, jnp.exp(log_beta + suffix), 0.0)
        o = jnp.einsum('bhqk,bkhd->bqhd', weights, vj)
        outs.append(o.astype(v.dtype))
    return jnp.concatenate(outs, axis=1)


def spatial_gating(u, v, g_v, w_s, b_s):
    bsz, s_len, _ = u.shape
    v = v.reshape(bsz, s_len, SGU_HEADS, HEAD_DIM)
    v = rms_norm(v, g_v.reshape(SGU_HEADS, HEAD_DIM))
    v = v.reshape(bsz, s_len // CHUNK, CHUNK, SGU_HEADS, HEAD_DIM)
    mask = jnp.tril(jnp.ones((CHUNK, CHUNK), dtype=w_s.dtype))
    w = w_s * mask
    sv = jnp.einsum('gts,bcsge->bctge', w, v) + b_s.T[:, :, None]
    return u * sv.reshape(bsz, s_len, SGU_W)


def _fwd_setup_inputs(seed: int = 0) -> dict:
    key = jax.random.key(seed)
    ks = jax.random.split(key, 17)

    def nrm(k, shape, scale):
        return jax.random.normal(k, shape, jnp.float32) * scale

    def gain(k, shape):
        return 1.0 + 0.05 * jax.random.normal(k, shape, jnp.float32)

    return {
        "x": nrm(ks[0], (BATCH, SEQ, D_MODEL), 1.0),
        "p": nrm(ks[1], (DEPTH, BATCH, SEQ, PLE_DIM), 1.0),
        "norm1_g": gain(ks[2], (DEPTH, D_MODEL)),
        "w_in": nrm(ks[3], (DEPTH, D_MODEL, D_IN_PROJ), D_MODEL ** -0.5),
        "conv_w": nrm(ks[4], (DEPTH, CONV_WIDTH, CONV_W), CONV_WIDTH ** -0.5),
        "q_norm_g": gain(ks[5], (DEPTH, HEAD_DIM)),
        "k_norm_g": gain(ks[6], (DEPTH, HEAD_DIM)),
        "sgu_norm_g": gain(ks[7], (DEPTH, SGU_W)),
        "sgu_w": nrm(ks[8], (DEPTH, SGU_HEADS, CHUNK, CHUNK), CHUNK ** -0.5),
        "sgu_b": gain(ks[9], (DEPTH, SGU_HEADS, CHUNK)),
        "w_out": nrm(ks[10], (DEPTH, D_MIX, D_MODEL), D_MIX ** -0.5),
        "norm2_g": gain(ks[11], (DEPTH, D_MODEL)),
        "w_ff1": nrm(ks[12], (DEPTH, D_MODEL, D_FF), D_MODEL ** -0.5),
        "w_ff2": nrm(ks[13], (DEPTH, D_FF, D_MODEL), D_FF ** -0.5),
        "norm3_g": gain(ks[14], (DEPTH, D_MODEL)),
        "w_ple_gate": nrm(ks[15], (DEPTH, D_MODEL, D_MODEL), D_MODEL ** -0.5),
        "w_ple_proj": nrm(ks[16], (DEPTH, PLE_DIM, D_MODEL), PLE_DIM ** -0.5),
    }


def _fwd_reference(x, p, norm1_g, w_in, conv_w, q_norm_g, k_norm_g, sgu_norm_g, sgu_w, sgu_b,
              w_out, norm2_g, w_ff1, w_ff2, norm3_g, w_ple_gate, w_ple_proj):
    bsz, s_len, _ = x.shape
    split_idx = list(np.cumsum([CONV_W, CONV_W, CONV_W, ATTN_W, ATTN_W, ATTN_W, SGU_W]))
    h = x
    for i in range(DEPTH):
        hn = rms_norm(h, norm1_g[i])
        proj = hn @ w_in[i]
        a_b, a_c, a_h, q, k, v, c_u, c_v = jnp.split(proj, split_idx, axis=-1)
        y_a = a_b * short_conv(a_c * a_h, conv_w[i])
        q = rms_norm(q.reshape(bsz, s_len, ATTN_HEADS, HEAD_DIM), q_norm_g[i])
        k = rms_norm(k.reshape(bsz, s_len, ATTN_HEADS, HEAD_DIM), k_norm_g[i])
        v = v.reshape(bsz, s_len, ATTN_HEADS, HEAD_DIM)
        y_b = stick_breaking_attention(q, k, v).reshape(bsz, s_len, ATTN_W)
        y_c = spatial_gating(jax.nn.gelu(c_u, approximate=False),
                             jax.nn.gelu(c_v, approximate=False),
                             sgu_norm_g[i], sgu_w[i], sgu_b[i])
        h = h + jnp.concatenate([y_a, y_b, y_c], axis=-1) @ w_out[i]
        f = jnp.square(jax.nn.relu(rms_norm(h, norm2_g[i]) @ w_ff1[i]))
        h = h + f @ w_ff2[i]
        gate = jax.nn.sigmoid(rms_norm(h, norm3_g[i]) @ w_ple_gate[i])
        h = h + gate * (p[i] @ w_ple_proj[i])
    return h


import jax as _jax
import jax.numpy as _jnp

TWIN_FORMAT = 'train_step'
FWD_PARAMS = ['x', 'p', 'norm1_g', 'w_in', 'conv_w', 'q_norm_g', 'k_norm_g', 'sgu_norm_g', 'sgu_w', 'sgu_b', 'w_out', 'norm2_g', 'w_ff1', 'w_ff2', 'norm3_g', 'w_ple_gate', 'w_ple_proj']
TWIN_WEIGHTS = ['norm1_g', 'w_in', 'conv_w', 'q_norm_g', 'k_norm_g', 'sgu_norm_g', 'sgu_w', 'sgu_b', 'w_out', 'norm2_g', 'w_ff1', 'w_ff2', 'norm3_g', 'w_ple_gate', 'w_ple_proj']
TWIN_DIFF_INPUT = 'x'
TWIN_INPUTS = ['x', 'p', 'norm1_g', 'w_in', 'conv_w', 'q_norm_g', 'k_norm_g', 'sgu_norm_g', 'sgu_w', 'sgu_b', 'w_out', 'norm2_g', 'w_ff1', 'w_ff2', 'norm3_g', 'w_ple_gate', 'w_ple_proj', 'loss_target', 'm_norm1_g', 'm_w_in', 'm_conv_w', 'm_q_norm_g', 'm_k_norm_g', 'm_sgu_norm_g', 'm_sgu_w', 'm_sgu_b', 'm_w_out', 'm_norm2_g', 'm_w_ff1', 'm_w_ff2', 'm_norm3_g', 'm_w_ple_gate', 'm_w_ple_proj', 'v_norm1_g', 'v_w_in', 'v_conv_w', 'v_q_norm_g', 'v_k_norm_g', 'v_sgu_norm_g', 'v_sgu_w', 'v_sgu_b', 'v_w_out', 'v_norm2_g', 'v_w_ff1', 'v_w_ff2', 'v_norm3_g', 'v_w_ple_gate', 'v_w_ple_proj']
TWIN_OUTPUTS = ['loss', 'grad_x', 'grad_norm1_g', 'grad_w_in', 'grad_conv_w', 'grad_q_norm_g', 'grad_k_norm_g', 'grad_sgu_norm_g', 'grad_sgu_w', 'grad_sgu_b', 'grad_w_out', 'grad_norm2_g', 'grad_w_ff1', 'grad_w_ff2', 'grad_norm3_g', 'grad_w_ple_gate', 'grad_w_ple_proj', 'delta_norm1_g', 'delta_w_in', 'delta_conv_w', 'delta_q_norm_g', 'delta_k_norm_g', 'delta_sgu_norm_g', 'delta_sgu_w', 'delta_sgu_b', 'delta_w_out', 'delta_norm2_g', 'delta_w_ff1', 'delta_w_ff2', 'delta_norm3_g', 'delta_w_ple_gate', 'delta_w_ple_proj', 'new_m_norm1_g', 'new_m_w_in', 'new_m_conv_w', 'new_m_q_norm_g', 'new_m_k_norm_g', 'new_m_sgu_norm_g', 'new_m_sgu_w', 'new_m_sgu_b', 'new_m_w_out', 'new_m_norm2_g', 'new_m_w_ff1', 'new_m_w_ff2', 'new_m_norm3_g', 'new_m_w_ple_gate', 'new_m_w_ple_proj', 'new_v_norm1_g', 'new_v_w_in', 'new_v_conv_w', 'new_v_q_norm_g', 'new_v_k_norm_g', 'new_v_sgu_norm_g', 'new_v_sgu_w', 'new_v_sgu_b', 'new_v_w_out', 'new_v_norm2_g', 'new_v_w_ff1', 'new_v_w_ff2', 'new_v_norm3_g', 'new_v_w_ple_gate', 'new_v_w_ple_proj']
TWIN_LEAF_KINDS = {'loss': 'loss', 'grad_x': 'grad_x', 'grad_norm1_g': 'grad_w', 'grad_w_in': 'grad_w', 'grad_conv_w': 'grad_w', 'grad_q_norm_g': 'grad_w', 'grad_k_norm_g': 'grad_w', 'grad_sgu_norm_g': 'grad_w', 'grad_sgu_w': 'grad_w', 'grad_sgu_b': 'grad_w', 'grad_w_out': 'grad_w', 'grad_norm2_g': 'grad_w', 'grad_w_ff1': 'grad_w', 'grad_w_ff2': 'grad_w', 'grad_norm3_g': 'grad_w', 'grad_w_ple_gate': 'grad_w', 'grad_w_ple_proj': 'grad_w', 'delta_norm1_g': 'delta_w', 'delta_w_in': 'delta_w', 'delta_conv_w': 'delta_w', 'delta_q_norm_g': 'delta_w', 'delta_k_norm_g': 'delta_w', 'delta_sgu_norm_g': 'delta_w', 'delta_sgu_w': 'delta_w', 'delta_sgu_b': 'delta_w', 'delta_w_out': 'delta_w', 'delta_norm2_g': 'delta_w', 'delta_w_ff1': 'delta_w', 'delta_w_ff2': 'delta_w', 'delta_norm3_g': 'delta_w', 'delta_w_ple_gate': 'delta_w', 'delta_w_ple_proj': 'delta_w', 'new_m_norm1_g': 'new_m', 'new_m_w_in': 'new_m', 'new_m_conv_w': 'new_m', 'new_m_q_norm_g': 'new_m', 'new_m_k_norm_g': 'new_m', 'new_m_sgu_norm_g': 'new_m', 'new_m_sgu_w': 'new_m', 'new_m_sgu_b': 'new_m', 'new_m_w_out': 'new_m', 'new_m_norm2_g': 'new_m', 'new_m_w_ff1': 'new_m', 'new_m_w_ff2': 'new_m', 'new_m_norm3_g': 'new_m', 'new_m_w_ple_gate': 'new_m', 'new_m_w_ple_proj': 'new_m', 'new_v_norm1_g': 'new_v', 'new_v_w_in': 'new_v', 'new_v_conv_w': 'new_v', 'new_v_q_norm_g': 'new_v', 'new_v_k_norm_g': 'new_v', 'new_v_sgu_norm_g': 'new_v', 'new_v_sgu_w': 'new_v', 'new_v_sgu_b': 'new_v', 'new_v_w_out': 'new_v', 'new_v_norm2_g': 'new_v', 'new_v_w_ff1': 'new_v', 'new_v_w_ff2': 'new_v', 'new_v_norm3_g': 'new_v', 'new_v_w_ple_gate': 'new_v', 'new_v_w_ple_proj': 'new_v'}


def _forward(args):
    return _fwd_reference(*[args[k] for k in FWD_PARAMS])


def _output_shape():
    out = _jax.eval_shape(lambda: _forward(_fwd_setup_inputs(0)))
    return out.shape, out.dtype

N_MICROBATCH = 1
ADAM_LR = 0.001
ADAM_B1 = 0.9
ADAM_B2 = 0.999
ADAM_EPS = 1e-08
ADAM_WD = 0.01
ADAM_STEP = 10
PER_EXAMPLE_BATCH_AXIS = {'x': 0, 'p': 1, 'loss_target': 0}
SHARED_INPUTS = []
_WEIGHT_DTYPES = {'norm1_g': _jnp.float32, 'w_in': _jnp.float32, 'conv_w': _jnp.float32, 'q_norm_g': _jnp.float32, 'k_norm_g': _jnp.float32, 'sgu_norm_g': _jnp.float32, 'sgu_w': _jnp.float32, 'sgu_b': _jnp.float32, 'w_out': _jnp.float32, 'norm2_g': _jnp.float32, 'w_ff1': _jnp.float32, 'w_ff2': _jnp.float32, 'norm3_g': _jnp.float32, 'w_ple_gate': _jnp.float32, 'w_ple_proj': _jnp.float32}
MOMENT_SCALE = {'norm1_g': 1.942504e+01, 'w_in': 3.103410e+00, 'conv_w': 9.076067e+00, 'q_norm_g': 6.784872e+00, 'k_norm_g': 6.744747e+00, 'sgu_norm_g': 4.379555e+00, 'sgu_w': 9.764505e-01, 'sgu_b': 3.105969e+00, 'w_out': 6.231215e+00, 'norm2_g': 4.876011e+01, 'w_ff1': 3.309685e+00, 'w_ff2': 1.331359e+01, 'norm3_g': 6.828633e-01, 'w_ple_gate': 4.943412e-01, 'w_ple_proj': 4.800615e-01}


def _to_microbatches(a, axis):
    t = _jnp.moveaxis(a, axis, 0)
    t = t.reshape((N_MICROBATCH, t.shape[0] // N_MICROBATCH) + t.shape[1:])
    return _jnp.moveaxis(t, 1, axis + 1)


def setup_inputs(seed: int = 0) -> dict:
    inp = _fwd_setup_inputs(seed)
    key = _jax.random.fold_in(_jax.random.key(seed), 7919)
    shape, _ = _output_shape()
    out = dict(inp)
    out["loss_target"] = _jax.random.normal(_jax.random.fold_in(key, 0), shape, _jnp.float32)
    for i, name in enumerate(TWIN_WEIGHTS):
        w = inp[name].astype(_jnp.float32)
        if MOMENT_SCALE is None:
            s = _jnp.sqrt(_jnp.mean(_jnp.square(w)) + 1e-30)
        else:
            s = MOMENT_SCALE[name]
        km, kv = _jax.random.split(_jax.random.fold_in(key, i + 1))
        out[name] = w
        out["m_" + name] = s * _jax.random.normal(km, w.shape, _jnp.float32)
        out["v_" + name] = (s * s) * _jax.random.uniform(kv, w.shape, _jnp.float32, 0.5, 1.5)
    if N_MICROBATCH > 1:
        for name, axis in PER_EXAMPLE_BATCH_AXIS.items():
            out[name] = _to_microbatches(out[name], axis)
    return {'x': out['x'], 'p': out['p'], 'norm1_g': out['norm1_g'], 'w_in': out['w_in'], 'conv_w': out['conv_w'], 'q_norm_g': out['q_norm_g'], 'k_norm_g': out['k_norm_g'], 'sgu_norm_g': out['sgu_norm_g'], 'sgu_w': out['sgu_w'], 'sgu_b': out['sgu_b'], 'w_out': out['w_out'], 'norm2_g': out['norm2_g'], 'w_ff1': out['w_ff1'], 'w_ff2': out['w_ff2'], 'norm3_g': out['norm3_g'], 'w_ple_gate': out['w_ple_gate'], 'w_ple_proj': out['w_ple_proj'], 'loss_target': out['loss_target'], 'm_norm1_g': out['m_norm1_g'], 'm_w_in': out['m_w_in'], 'm_conv_w': out['m_conv_w'], 'm_q_norm_g': out['m_q_norm_g'], 'm_k_norm_g': out['m_k_norm_g'], 'm_sgu_norm_g': out['m_sgu_norm_g'], 'm_sgu_w': out['m_sgu_w'], 'm_sgu_b': out['m_sgu_b'], 'm_w_out': out['m_w_out'], 'm_norm2_g': out['m_norm2_g'], 'm_w_ff1': out['m_w_ff1'], 'm_w_ff2': out['m_w_ff2'], 'm_norm3_g': out['m_norm3_g'], 'm_w_ple_gate': out['m_w_ple_gate'], 'm_w_ple_proj': out['m_w_ple_proj'], 'v_norm1_g': out['v_norm1_g'], 'v_w_in': out['v_w_in'], 'v_conv_w': out['v_conv_w'], 'v_q_norm_g': out['v_q_norm_g'], 'v_k_norm_g': out['v_k_norm_g'], 'v_sgu_norm_g': out['v_sgu_norm_g'], 'v_sgu_w': out['v_sgu_w'], 'v_sgu_b': out['v_sgu_b'], 'v_w_out': out['v_w_out'], 'v_norm2_g': out['v_norm2_g'], 'v_w_ff1': out['v_w_ff1'], 'v_w_ff2': out['v_w_ff2'], 'v_norm3_g': out['v_norm3_g'], 'v_w_ple_gate': out['v_w_ple_gate'], 'v_w_ple_proj': out['v_w_ple_proj']}


def _loss(weights, diff, rest, loss_target):
    with _jax.named_scope("forward"):
        args = {**rest, TWIN_DIFF_INPUT: diff, **{k: w.astype(_WEIGHT_DTYPES[k]) for k, w in weights.items()}}
        y = _forward(args)
    with _jax.named_scope("loss_head"):
        err = _jnp.square(y.astype(_jnp.float32) - loss_target)
        return 0.5 * _jnp.sum(_jnp.mean(err, axis=-1)) if err.ndim else 0.5 * err


def _adamw(w, g, m, v):
    m = ADAM_B1 * m + (1.0 - ADAM_B1) * g
    v = ADAM_B2 * v + (1.0 - ADAM_B2) * _jnp.square(g)
    m_hat = m / (1.0 - ADAM_B1 ** ADAM_STEP)
    v_hat = v / (1.0 - ADAM_B2 ** ADAM_STEP)
    delta = -ADAM_LR * (m_hat / (_jnp.sqrt(v_hat) + ADAM_EPS) + ADAM_WD * w)
    return delta, m, v


def reference(x, p, norm1_g, w_in, conv_w, q_norm_g, k_norm_g, sgu_norm_g, sgu_w, sgu_b, w_out, norm2_g, w_ff1, w_ff2, norm3_g, w_ple_gate, w_ple_proj, loss_target, m_norm1_g, m_w_in, m_conv_w, m_q_norm_g, m_k_norm_g, m_sgu_norm_g, m_sgu_w, m_sgu_b, m_w_out, m_norm2_g, m_w_ff1, m_w_ff2, m_norm3_g, m_w_ple_gate, m_w_ple_proj, v_norm1_g, v_w_in, v_conv_w, v_q_norm_g, v_k_norm_g, v_sgu_norm_g, v_sgu_w, v_sgu_b, v_w_out, v_norm2_g, v_w_ff1, v_w_ff2, v_norm3_g, v_w_ple_gate, v_w_ple_proj):
    given = dict(x=x, p=p, norm1_g=norm1_g, w_in=w_in, conv_w=conv_w, q_norm_g=q_norm_g, k_norm_g=k_norm_g, sgu_norm_g=sgu_norm_g, sgu_w=sgu_w, sgu_b=sgu_b, w_out=w_out, norm2_g=norm2_g, w_ff1=w_ff1, w_ff2=w_ff2, norm3_g=norm3_g, w_ple_gate=w_ple_gate, w_ple_proj=w_ple_proj, loss_target=loss_target, m_norm1_g=m_norm1_g, m_w_in=m_w_in, m_conv_w=m_conv_w, m_q_norm_g=m_q_norm_g, m_k_norm_g=m_k_norm_g, m_sgu_norm_g=m_sgu_norm_g, m_sgu_w=m_sgu_w, m_sgu_b=m_sgu_b, m_w_out=m_w_out, m_norm2_g=m_norm2_g, m_w_ff1=m_w_ff1, m_w_ff2=m_w_ff2, m_norm3_g=m_norm3_g, m_w_ple_gate=m_w_ple_gate, m_w_ple_proj=m_w_ple_proj, v_norm1_g=v_norm1_g, v_w_in=v_w_in, v_conv_w=v_conv_w, v_q_norm_g=v_q_norm_g, v_k_norm_g=v_k_norm_g, v_sgu_norm_g=v_sgu_norm_g, v_sgu_w=v_sgu_w, v_sgu_b=v_sgu_b, v_w_out=v_w_out, v_norm2_g=v_norm2_g, v_w_ff1=v_w_ff1, v_w_ff2=v_w_ff2, v_norm3_g=v_norm3_g, v_w_ple_gate=v_w_ple_gate, v_w_ple_proj=v_w_ple_proj)
    weights = {n: given[n] for n in TWIN_WEIGHTS}
    shared = {n: given[n] for n in SHARED_INPUTS}
    per_example = {n: given[n] for n in ['x', 'p']}
    grad_fn = _jax.value_and_grad(_loss, argnums=(0, 1))

    def one_microbatch(ex, loss_target):
        ex = dict(ex)
        diff = ex.pop(TWIN_DIFF_INPUT)
        return grad_fn(weights, diff, {**shared, **ex}, loss_target)

    if N_MICROBATCH == 1:
        loss, (grad_w, grad_x) = one_microbatch(per_example, given["loss_target"])
    else:
        def body(carry, xs):
            loss_sum, grad_sum = carry
            l_k, (gw_k, gx_k) = one_microbatch(xs[0], xs[1])
            with _jax.named_scope("update"):
                return (loss_sum + l_k, _jax.tree.map(_jnp.add, grad_sum, gw_k)), gx_k

        init = (_jnp.zeros((), _jnp.float32), _jax.tree.map(_jnp.zeros_like, weights))
        (loss, grad_w), grad_x = _jax.lax.scan(body, init, (per_example, given["loss_target"]))
    with _jax.named_scope("update"):
        delta_w, new_m, new_v = {}, {}, {}
        for n in TWIN_WEIGHTS:
            delta_w[n], new_m[n], new_v[n] = _adamw(weights[n], grad_w[n], given["m_" + n], given["v_" + n])
    return (loss, grad_x, *[grad_w[n] for n in TWIN_WEIGHTS], *[delta_w[n] for n in TWIN_WEIGHTS],
            *[new_m[n] for n in TWIN_WEIGHTS], *[new_v[n] for n in TWIN_WEIGHTS])
```

```python
import functools
import math

import jax
import jax.numpy as jnp
from jax import lax
from jax.experimental import pallas as pl
from jax.experimental.pallas import tpu as pltpu

F32 = jnp.float32
BF16 = jnp.bfloat16

N_DEV = 8
HEAD_DIM = 64
CONV_W = 256
ATTN_W = 512
SGU_W = 256
SGU_HEADS = 4
CHUNK = 128
CONV_TAPS = 3
EPS = 1e-6
QK_SCALE = HEAD_DIM ** -0.5

ADAM_LR = 0.001
ADAM_B1 = 0.9
ADAM_B2 = 0.999
ADAM_EPS = 1e-08
ADAM_WD = 0.01
ADAM_STEP = 10

LANES = 128
BF16_TILE_ROWS = 16
VMEM_LIMIT_BYTES = 56 * 1024 * 1024
MESH = pl.DeviceIdType.MESH


def _params(*sem):
    return pltpu.CompilerParams(dimension_semantics=sem, vmem_limit_bytes=VMEM_LIMIT_BYTES)


def _row_block(rows, cap):
    if rows <= cap:
        return rows
    return max(b for b in range(BF16_TILE_ROWS, cap + 1, BF16_TILE_ROWS) if rows % b == 0)


def _matmul(a, b, *, name, ta=False, tb=False, bm=512, bn=512, out_dtypes=(F32,), epilogue=None, extras=()):
    m = a.shape[1] if ta else a.shape[0]
    k = a.shape[0] if ta else a.shape[1]
    n = b.shape[0] if tb else b.shape[1]
    assert k == (b.shape[1] if tb else b.shape[0])
    bm, bn = min(bm, m), min(bn, n)
    assert m % bm == 0 and n % bn == 0
    a_spec = pl.BlockSpec((k, bm), lambda i, j: (0, i)) if ta else pl.BlockSpec((bm, k), lambda i, j: (i, 0))
    b_spec = pl.BlockSpec((bn, k), lambda i, j: (j, 0)) if tb else pl.BlockSpec((k, bn), lambda i, j: (0, j))
    dims = (((0 if ta else 1,), (1 if tb else 0,)), ((), ()))
    n_ex = len(extras)
    for e in extras:
        assert e.shape == (m, n), (e.shape, m, n)

    def body(a_ref, b_ref, *rest):
        outs = rest[n_ex:]
        acc = lax.dot_general(a_ref[...], b_ref[...], dims, preferred_element_type=F32)
        res = (acc,) if epilogue is None else epilogue(acc, *[e[...] for e in rest[:n_ex]])
        for o_ref, r in zip(outs, res, strict=True):
            o_ref[...] = r.astype(o_ref.dtype)

    tile = pl.BlockSpec((bm, bn), lambda i, j: (i, j))
    out = pl.pallas_call(
        body,
        name=name,
        grid=(m // bm, n // bn),
        in_specs=[a_spec, b_spec] + [tile] * n_ex,
        out_specs=[tile] * len(out_dtypes),
        out_shape=[jax.ShapeDtypeStruct((m, n), d) for d in out_dtypes],
        compiler_params=_params("parallel", "parallel"),
    )(a, b, *extras)
    return out[0] if len(out_dtypes) == 1 else out


def _rms_fwd(h, g, *, name, br=512):
    t, d = h.shape
    br = min(br, t)

    def body(h_ref, g_ref, o_ref):
        x = h_ref[...]
        r = lax.rsqrt(jnp.mean(x * x, axis=-1, keepdims=True) + EPS)
        o_ref[...] = (x * r * g_ref[...]).astype(o_ref.dtype)

    return pl.pallas_call(
        body,
        name=name,
        grid=(t // br,),
        in_specs=[pl.BlockSpec((br, d), lambda i: (i, 0)), pl.BlockSpec((1, d), lambda i: (0, 0))],
        out_specs=pl.BlockSpec((br, d), lambda i: (i, 0)),
        out_shape=jax.ShapeDtypeStruct((t, d), BF16),
        compiler_params=_params("parallel"),
    )(h, g)


def _rms_bwd(dy, h, g, dres, *, name, br=512):
    t, d = h.shape
    br = min(br, t)

    def body(dy_ref, h_ref, g_ref, dres_ref, dh_ref, dg_ref):
        x = h_ref[...]
        dyv = dy_ref[...]
        r = lax.rsqrt(jnp.mean(x * x, axis=-1, keepdims=True) + EPS)
        xhat = x * r
        dxhat = dyv * g_ref[...]
        dh = r * (dxhat - xhat * jnp.mean(dxhat * xhat, axis=-1, keepdims=True))
        dh_ref[...] = dres_ref[...] + dh

        @pl.when(pl.program_id(0) == 0)
        def _():
            dg_ref[...] = jnp.zeros_like(dg_ref)

        dg_ref[...] += jnp.sum(dyv * xhat, axis=0, keepdims=True)

    row = pl.BlockSpec((br, d), lambda i: (i, 0))
    vec = pl.BlockSpec((1, d), lambda i: (0, 0))
    return pl.pallas_call(
        body,
        name=name,
        grid=(t // br,),
        in_specs=[row, row, vec, row],
        out_specs=[row, vec],
        out_shape=[jax.ShapeDtypeStruct((t, d), F32), jax.ShapeDtypeStruct((1, d), F32)],
        compiler_params=_params("arbitrary"),
    )(dy, h, g, dres)


def _group_mean(x, width):
    grp = lax.broadcasted_iota(jnp.int32, x.shape, 1) // HEAD_DIM
    out = jnp.zeros_like(x)
    for gi in range(width // HEAD_DIM):
        m = grp == gi
        s = jnp.sum(jnp.where(m, x, 0.0), axis=1, keepdims=True)
        out = jnp.where(m, s, out)
    return out * (1.0 / HEAD_DIM)


def _gelu(x):
    return 0.5 * x * (1.0 + lax.erf(x * (2.0 ** -0.5)))


def _gelu_grad(x):
    cdf = 0.5 * (1.0 + lax.erf(x * (2.0 ** -0.5)))
    pdf = jnp.exp(-0.5 * x * x) * (1.0 / math.sqrt(2.0 * math.pi))
    return cdf + x * pdf


def _shift_down(z, s, row):
    return jnp.where(row >= s, pltpu.roll(z, s, 0), 0.0)


def _shift_up(z, s, row, t):
    return jnp.where(row < t - s, pltpu.roll(z, t - s, 0), 0.0)


def _conv_fwd(proj, conv_w, *, name):
    t = proj.shape[0]
    nb = CONV_W // LANES

    def body(b_ref, c_ref, h_ref, w_ref, o_ref):
        row = lax.broadcasted_iota(jnp.int32, (t, LANES), 0)
        z = c_ref[...] * h_ref[...]
        w = w_ref[...]
        conv = w[2:3, :] * z + w[1:2, :] * _shift_down(z, 1, row) + w[0:1, :] * _shift_down(z, 2, row)
        o_ref[...] = (b_ref[...] * conv).astype(o_ref.dtype)

    return pl.pallas_call(
        body,
        name=name,
        grid=(nb,),
        in_specs=[
            pl.BlockSpec((t, LANES), lambda j: (0, j)),
            pl.BlockSpec((t, LANES), lambda j: (0, nb + j)),
            pl.BlockSpec((t, LANES), lambda j: (0, 2 * nb + j)),
            pl.BlockSpec((CONV_TAPS, LANES), lambda j: (0, j)),
        ],
        out_specs=pl.BlockSpec((t, LANES), lambda j: (0, j)),
        out_shape=jax.ShapeDtypeStruct((t, CONV_W), BF16),
        compiler_params=_params("parallel"),
    )(proj, proj, proj, conv_w)


def _conv_bwd(dmix, proj, conv_w, *, name):
    t = proj.shape[0]
    nb = CONV_W // LANES

    def body(dy_ref, b_ref, c_ref, h_ref, w_ref, db_ref, dc_ref, dh_ref, dw_ref):
        row = lax.broadcasted_iota(jnp.int32, (t, LANES), 0)
        ac, ah = c_ref[...], h_ref[...]
        z = ac * ah
        w = w_ref[...]
        z1 = _shift_down(z, 1, row)
        z2 = _shift_down(z, 2, row)
        conv = w[2:3, :] * z + w[1:2, :] * z1 + w[0:1, :] * z2
        dy = dy_ref[...]
        db_ref[...] = (dy * conv).astype(db_ref.dtype)
        dconv = dy * b_ref[...]
        dz = w[2:3, :] * dconv + w[1:2, :] * _shift_up(dconv, 1, row, t) + w[0:1, :] * _shift_up(dconv, 2, row, t)
        dc_ref[...] = (dz * ah).astype(dc_ref.dtype)
        dh_ref[...] = (dz * ac).astype(dh_ref.dtype)
        dw_ref[...] = jnp.zeros_like(dw_ref)
        dw_ref[0:1, :] = jnp.sum(dconv * z2, axis=0, keepdims=True)
        dw_ref[1:2, :] = jnp.sum(dconv * z1, axis=0, keepdims=True)
        dw_ref[2:3, :] = jnp.sum(dconv * z, axis=0, keepdims=True)

    col = lambda off: pl.BlockSpec((t, LANES), lambda j: (0, off + j))
    return pl.pallas_call(
        body,
        name=name,
        grid=(nb,),
        in_specs=[col(0), col(0), col(nb), col(2 * nb), pl.BlockSpec((CONV_TAPS, LANES), lambda j: (0, j))],
        out_specs=[col(0), col(0), col(0), pl.BlockSpec((8, LANES), lambda j: (0, j))],
        out_shape=[jax.ShapeDtypeStruct((t, CONV_W), BF16)] * 3 + [jax.ShapeDtypeStruct((8, CONV_W), F32)],
        compiler_params=_params("parallel"),
    )(dmix, proj, proj, proj, conv_w)


_QK_BLOCK = 256


def _qk_prep(proj, gq, gk, *, name, br=512):
    t = proj.shape[0]
    br = min(br, t)
    nb = ATTN_W // _QK_BLOCK
    q0 = (3 * CONV_W) // _QK_BLOCK

    def body(q_ref, k_ref, v_ref, gq_ref, gk_ref, qo_ref, ko_ref, vo_ref):
        q = q_ref[...]
        k = k_ref[...]
        rq = lax.rsqrt(_group_mean(q * q, _QK_BLOCK) + EPS)
        rk = lax.rsqrt(_group_mean(k * k, _QK_BLOCK) + EPS)
        qo_ref[...] = ((q * rq * gq_ref[...]).astype(BF16) * QK_SCALE).astype(qo_ref.dtype)
        ko_ref[...] = (k * rk * gk_ref[...]).astype(ko_ref.dtype)
        vo_ref[...] = v_ref[...].astype(vo_ref.dtype)

    col = lambda off: pl.BlockSpec((br, _QK_BLOCK), lambda i, j: (i, off + j))
    vec = pl.BlockSpec((1, _QK_BLOCK), lambda i, j: (0, 0))
    return pl.pallas_call(
        body,
        name=name,
        grid=(t // br, nb),
        in_specs=[col(q0), col(q0 + nb), col(q0 + 2 * nb), vec, vec],
        out_specs=[col(0)] * 3,
        out_shape=[jax.ShapeDtypeStruct((t, ATTN_W), BF16)] * 3,
        compiler_params=_params("parallel", "parallel"),
    )(proj, proj, proj, gq, gk)


def _qk_prep_bwd(dqs, dkn, dv, proj, gq, gk, *, name, br=512):
    t = proj.shape[0]
    br = min(br, t)
    nb = ATTN_W // _QK_BLOCK
    q0 = (3 * CONV_W) // _QK_BLOCK

    def norm_bwd(dy, x, g):
        r = lax.rsqrt(_group_mean(x * x, _QK_BLOCK) + EPS)
        xhat = x * r
        dxhat = dy * g
        dx = r * (dxhat - xhat * _group_mean(dxhat * xhat, _QK_BLOCK))
        return dx, jnp.sum(dy * xhat, axis=0, keepdims=True)

    def body(dq_ref, dk_ref, dv_ref, q_ref, k_ref, gq_ref, gk_ref, oq_ref, ok_ref, ov_ref, dgq_ref, dgk_ref):
        dq, dgq = norm_bwd(dq_ref[...] * QK_SCALE, q_ref[...], gq_ref[...])
        dk, dgk = norm_bwd(dk_ref[...], k_ref[...], gk_ref[...])
        oq_ref[...] = dq.astype(oq_ref.dtype)
        ok_ref[...] = dk.astype(ok_ref.dtype)
        ov_ref[...] = dv_ref[...].astype(ov_ref.dtype)

        @pl.when((pl.program_id(0) == 0) & (pl.program_id(1) == 0))
        def _():
            dgq_ref[...] = jnp.zeros_like(dgq_ref)
            dgk_ref[...] = jnp.zeros_like(dgk_ref)

        dgq_ref[...] += dgq
        dgk_ref[...] += dgk

    col = lambda off: pl.BlockSpec((br, _QK_BLOCK), lambda i, j: (i, off + j))
    vec = pl.BlockSpec((1, _QK_BLOCK), lambda i, j: (0, 0))
    return pl.pallas_call(
        body,
        name=name,
        grid=(t // br, nb),
        in_specs=[col(0), col(0), col(0), col(q0), col(q0 + nb), vec, vec],
        out_specs=[col(0), col(0), col(0), vec, vec],
        out_shape=[jax.ShapeDtypeStruct((t, ATTN_W), BF16)] * 3 + [jax.ShapeDtypeStruct((1, _QK_BLOCK), F32)] * 2,
        compiler_params=_params("arbitrary", "arbitrary"),
    )(dqs, dkn, dv, proj, proj, gq, gk)


def _split_bf16(x):
    hi = x.astype(BF16)
    lo = (x - hi.astype(F32)).astype(BF16)
    return hi, lo


def _dot_exact01(x, m01):
    hi, lo = _split_bf16(x)
    return jnp.dot(hi, m01, preferred_element_type=F32) + jnp.dot(lo, m01, preferred_element_type=F32)


def _log_sigmoids(z):
    sp = jnp.log(1.0 + jnp.exp(-jnp.abs(z)))
    return jnp.minimum(z, 0.0) - sp, jnp.minimum(-z, 0.0) - sp


_NT = (((1,), (1,)), ((), ()))
_TN = (((0,), (0,)), ((), ()))


def _attn_fwd(qs, kn, v, *, name, tq=256, tk=256):
    t = qs.shape[0]
    tq, tk = min(tq, t), min(tk, t)
    assert tq % tk == 0 and t % tq == 0
    n_pairs = ATTN_W // LANES

    def body(q_ref, k_ref, v_ref, o_ref, lt_ref, acc_ref, carry_ref):
        qb = pl.program_id(1)
        half = lax.broadcasted_iota(jnp.int32, (1, LANES), 1) // HEAD_DIM
        q = q_ref[...]
        qh = [jnp.where(half == h, q, jnp.zeros_like(q)) for h in range(2)]
        row = qb * tq + lax.broadcasted_iota(jnp.int32, (tq, tk), 0)
        col = lax.broadcasted_iota(jnp.int32, (tq, tk), 1)
        jj = lax.broadcasted_iota(jnp.int32, (tk, tk), 0)
        ss = lax.broadcasted_iota(jnp.int32, (tk, tk), 1)
        later = (jj > ss).astype(BF16)
        acc_ref[...] = jnp.zeros_like(acc_ref)
        carry_ref[...] = jnp.zeros_like(carry_ref)
        n_kb = (qb + 1) * (tq // tk)

        def step(i, _):
            kb = n_kb - 1 - i
            start = pl.multiple_of(kb * tk, tk)
            kblk = k_ref[pl.ds(start, tk), :]
            vblk = v_ref[pl.ds(start, tk), :]
            causal = (start + col) < row
            for h in range(2):
                z = lax.dot_general(qh[h], kblk, _NT, preferred_element_type=F32)
                lb, lr = _log_sigmoids(z)
                lr = jnp.where(causal, lr, 0.0)
                carry = carry_ref[h]
                suffix = _dot_exact01(lr, later) + carry[:, 0:1]
                w = jnp.where(causal, jnp.exp(lb + suffix), 0.0)
                vh = jnp.where(half == h, vblk, jnp.zeros_like(vblk))
                acc_ref[...] += jnp.dot(w.astype(BF16), vh, preferred_element_type=F32)
                carry_ref[h] = carry + jnp.sum(lr, axis=1, keepdims=True)
            return 0

        lax.fori_loop(0, n_kb, step, 0)
        o_ref[...] = acc_ref[...].astype(o_ref.dtype)
        lt_ref[...] = jnp.where(half == 0, carry_ref[0], carry_ref[1])

    return pl.pallas_call(
        body,
        name=name,
        grid=(n_pairs, t // tq),
        in_specs=[
            pl.BlockSpec((tq, LANES), lambda p, i: (i, p)),
            pl.BlockSpec((t, LANES), lambda p, i: (0, p)),
            pl.BlockSpec((t, LANES), lambda p, i: (0, p)),
        ],
        out_specs=[pl.BlockSpec((tq, LANES), lambda p, i: (i, p))] * 2,
        out_shape=[jax.ShapeDtypeStruct((t, ATTN_W), BF16), jax.ShapeDtypeStruct((t, ATTN_W), F32)],
        scratch_shapes=[pltpu.VMEM((tq, LANES), F32), pltpu.VMEM((2, tq, LANES), F32)],
        compiler_params=_params("parallel", "parallel"),
    )(qs, kn, v)


def _attn_bwd(dmix, qs, kn, v, lt, *, name, tq=256, tk=256):
    t = qs.shape[0]
    tq, tk = min(tq, t), min(tk, t)
    assert tq % tk == 0 and t % tq == 0
    n_pairs = ATTN_W // LANES
    dy0 = CONV_W // LANES

    def body(do_ref, q_ref, k_ref, v_ref, lt_ref, dq_ref, dk_ref, dv_ref, cc_ref, cg_ref):
        qb = pl.program_id(1)
        half = lax.broadcasted_iota(jnp.int32, (1, LANES), 1) // HEAD_DIM
        q = q_ref[...]
        do = do_ref[...].astype(BF16)
        lt = lt_ref[...]
        row = qb * tq + lax.broadcasted_iota(jnp.int32, (tq, tk), 0)
        col = lax.broadcasted_iota(jnp.int32, (tq, tk), 1)
        jj = lax.broadcasted_iota(jnp.int32, (tk, tk), 0)
        ss = lax.broadcasted_iota(jnp.int32, (tk, tk), 1)
        upto = (jj <= ss).astype(BF16)
        before = (jj < ss).astype(BF16)
        qh, doh, lth = [], [], []
        for h in range(2):
            qh.append(jnp.where(half == h, q, jnp.zeros_like(q)))
            doh.append(jnp.where(half == h, do, jnp.zeros_like(do)))
            lth.append(jnp.sum(jnp.where(lax.broadcasted_iota(jnp.int32, (tq, LANES), 1) == h * HEAD_DIM, lt, 0.0),
                               axis=1, keepdims=True))

        @pl.when(qb == 0)
        def _():
            dk_ref[...] = jnp.zeros_like(dk_ref)
            dv_ref[...] = jnp.zeros_like(dv_ref)

        dq_ref[...] = jnp.zeros_like(dq_ref)
        cc_ref[...] = jnp.zeros_like(cc_ref)
        cg_ref[...] = jnp.zeros_like(cg_ref)
        n_kb = (qb + 1) * (tq // tk)

        def step(kb, _):
            start = pl.multiple_of(kb * tk, tk)
            kblk = k_ref[pl.ds(start, tk), :]
            vblk = v_ref[pl.ds(start, tk), :]
            causal = (start + col) < row
            for h in range(2):
                z = lax.dot_general(qh[h], kblk, _NT, preferred_element_type=F32)
                lb, lr = _log_sigmoids(z)
                lr = jnp.where(causal, lr, 0.0)
                cc = cc_ref[h]
                prefix = _dot_exact01(lr, upto) + cc[:, 0:1]
                a = jnp.where(causal, jnp.exp(lb + (lth[h] - prefix)), 0.0)
                vh = jnp.where(half == h, vblk, jnp.zeros_like(vblk))
                da = lax.dot_general(doh[h], vh, _NT, preferred_element_type=F32)
                g = da * a
                cg = cg_ref[h]
                p = _dot_exact01(g, before) + cg[:, 0:1]
                beta = jnp.exp(lb)
                dz = jnp.where(causal, g * (1.0 - beta) - p * beta, 0.0).astype(BF16)
                kh = jnp.where(half == h, kblk, jnp.zeros_like(kblk))
                dq_ref[...] += jnp.dot(dz, kh, preferred_element_type=F32)
                dk_ref[pl.ds(start, tk), :] += lax.dot_general(dz, qh[h], _TN, preferred_element_type=F32)
                dv_ref[pl.ds(start, tk), :] += lax.dot_general(a.astype(BF16), doh[h], _TN, preferred_element_type=F32)
                cc_ref[h] = cc + jnp.sum(lr, axis=1, keepdims=True)
                cg_ref[h] = cg + jnp.sum(g, axis=1, keepdims=True)
            return 0

        lax.fori_loop(0, n_kb, step, 0)

    qblk = pl.BlockSpec((tq, LANES), lambda p, i: (i, p))
    whole = pl.BlockSpec((t, LANES), lambda p, i: (0, p))
    return pl.pallas_call(
        body,
        name=name,
        grid=(n_pairs, t // tq),
        in_specs=[pl.BlockSpec((tq, LANES), lambda p, i: (i, dy0 + p)), qblk, whole, whole, qblk],
        out_specs=[qblk, whole, whole],
        out_shape=[jax.ShapeDtypeStruct((t, ATTN_W), F32)] * 3,
        scratch_shapes=[pltpu.VMEM((2, tq, LANES), F32), pltpu.VMEM((2, tq, LANES), F32)],
        compiler_params=_params("parallel", "arbitrary"),
    )(dmix, qs, kn, v, lt)


def _sgu_weights(w_ref):
    tt = lax.broadcasted_iota(jnp.int32, (CHUNK, CHUNK), 0)
    ss = lax.broadcasted_iota(jnp.int32, (CHUNK, CHUNK), 1)
    tril = ss <= tt
    return [jnp.where(tril, w_ref[gi], 0.0).astype(BF16) for gi in range(SGU_HEADS)], tril


def _sgu_fwd(proj, g_v, w_s, b_exp, *, name):
    t = proj.shape[0]
    u0 = (3 * CONV_W + 3 * ATTN_W) // SGU_W

    def body(u_ref, v_ref, g_ref, w_ref, b_ref, o_ref):
        grp = lax.broadcasted_iota(jnp.int32, (1, SGU_W), 1) // HEAD_DIM
        u = _gelu(u_ref[...])
        vv = _gelu(v_ref[...])
        vn = (vv * lax.rsqrt(_group_mean(vv * vv, SGU_W) + EPS) * g_ref[...]).astype(BF16)
        wm, _ = _sgu_weights(w_ref)
        sv = b_ref[...]
        for gi in range(SGU_HEADS):
            sv = sv + jnp.dot(wm[gi], jnp.where(grp == gi, vn, jnp.zeros_like(vn)), preferred_element_type=F32)
        o_ref[...] = (u * sv).astype(o_ref.dtype)

    return pl.pallas_call(
        body,
        name=name,
        grid=(t // CHUNK,),
        in_specs=[
            pl.BlockSpec((CHUNK, SGU_W), lambda i: (i, u0)),
            pl.BlockSpec((CHUNK, SGU_W), lambda i: (i, u0 + 1)),
            pl.BlockSpec((1, SGU_W), lambda i: (0, 0)),
            pl.BlockSpec((SGU_HEADS, CHUNK, CHUNK), lambda i: (0, 0, 0)),
            pl.BlockSpec((CHUNK, SGU_W), lambda i: (0, 0)),
        ],
        out_specs=pl.BlockSpec((CHUNK, SGU_W), lambda i: (i, 0)),
        out_shape=jax.ShapeDtypeStruct((t, SGU_W), BF16),
        compiler_params=_params("parallel"),
    )(proj, proj, g_v, w_s, b_exp)


def _sgu_bwd(dmix, proj, g_v, w_s, b_exp, *, name):
    t = proj.shape[0]
    u0 = (3 * CONV_W + 3 * ATTN_W) // SGU_W
    dy0 = (CONV_W + ATTN_W) // SGU_W

    def body(dy_ref, u_ref, v_ref, g_ref, w_ref, b_ref, du_ref, dv_ref, dg_ref, dw_ref, db_ref):
        grp = lax.broadcasted_iota(jnp.int32, (1, SGU_W), 1) // HEAD_DIM
        cu, cv = u_ref[...], v_ref[...]
        u = _gelu(cu)
        vv = _gelu(cv)
        r = lax.rsqrt(_group_mean(vv * vv, SGU_W) + EPS)
        xhat = vv * r
        gain = g_ref[...]
        vn = (xhat * gain).astype(BF16)
        wm, tril = _sgu_weights(w_ref)
        vng = [jnp.where(grp == gi, vn, jnp.zeros_like(vn)) for gi in range(SGU_HEADS)]
        sv = b_ref[...]
        for gi in range(SGU_HEADS):
            sv = sv + jnp.dot(wm[gi], vng[gi], preferred_element_type=F32)
        dy = dy_ref[...]
        du_ref[...] = (dy * sv * _gelu_grad(cu)).astype(du_ref.dtype)
        dsv = dy * u
        dsv16 = dsv.astype(BF16)

        @pl.when(pl.program_id(0) == 0)
        def _():
            dg_ref[...] = jnp.zeros_like(dg_ref)
            dw_ref[...] = jnp.zeros_like(dw_ref)
            db_ref[...] = jnp.zeros_like(db_ref)

        db_ref[...] += dsv
        dvn = jnp.zeros_like(dsv)
        for gi in range(SGU_HEADS):
            dw = lax.dot_general(dsv16, vng[gi], _NT, preferred_element_type=F32)
            dw_ref[gi] += jnp.where(tril, dw, 0.0)
            dvn_g = lax.dot_general(wm[gi], dsv16, _TN, preferred_element_type=F32)
            dvn = jnp.where(grp == gi, dvn_g, dvn)
        dg_ref[...] += jnp.sum(dvn * xhat, axis=0, keepdims=True)
        dxhat = dvn * gain
        dvv = r * (dxhat - xhat * _group_mean(dxhat * xhat, SGU_W))
        dv_ref[...] = (dvv * _gelu_grad(cv)).astype(dv_ref.dtype)

    return pl.pallas_call(
        body,
        name=name,
        grid=(t // CHUNK,),
        in_specs=[
            pl.BlockSpec((CHUNK, SGU_W), lambda i: (i, dy0)),
            pl.BlockSpec((CHUNK, SGU_W), lambda i: (i, u0)),
            pl.BlockSpec((CHUNK, SGU_W), lambda i: (i, u0 + 1)),
            pl.BlockSpec((1, SGU_W), lambda i: (0, 0)),
            pl.BlockSpec((SGU_HEADS, CHUNK, CHUNK), lambda i: (0, 0, 0)),
            pl.BlockSpec((CHUNK, SGU_W), lambda i: (0, 0)),
        ],
        out_specs=[
            pl.BlockSpec((CHUNK, SGU_W), lambda i: (i, 0)),
            pl.BlockSpec((CHUNK, SGU_W), lambda i: (i, 0)),
            pl.BlockSpec((1, SGU_W), lambda i: (0, 0)),
            pl.BlockSpec((SGU_HEADS, CHUNK, CHUNK), lambda i: (0, 0, 0)),
            pl.BlockSpec((CHUNK, SGU_W), lambda i: (0, 0)),
        ],
        out_shape=[
            jax.ShapeDtypeStruct((t, SGU_W), BF16),
            jax.ShapeDtypeStruct((t, SGU_W), BF16),
            jax.ShapeDtypeStruct((1, SGU_W), F32),
            jax.ShapeDtypeStruct((SGU_HEADS, CHUNK, CHUNK), F32),
            jax.ShapeDtypeStruct((CHUNK, SGU_W), F32),
        ],
        compiler_params=_params("arbitrary"),
    )(dmix, proj, proj, g_v, w_s, b_exp)


def _ple_bwd(dh, gate, pp, *, name, br=512):
    t, d = dh.shape
    br = min(br, t)

    def body(dh_ref, g_ref, p_ref, dpre_ref, dpp_ref):
        dhv, g = dh_ref[...], g_ref[...]
        dpre_ref[...] = (dhv * p_ref[...] * g * (1.0 - g)).astype(dpre_ref.dtype)
        dpp_ref[...] = (dhv * g).astype(dpp_ref.dtype)

    row = pl.BlockSpec((br, d), lambda i: (i, 0))
    return pl.pallas_call(
        body,
        name=name,
        grid=(t // br,),
        in_specs=[row] * 3,
        out_specs=[row] * 2,
        out_shape=[jax.ShapeDtypeStruct((t, d), BF16)] * 2,
        compiler_params=_params("parallel"),
    )(dh, gate, pp)


def _loss_head(y, target, *, name, br=512):
    t, d = y.shape
    br = min(br, t)

    def body(y_ref, t_ref, dy_ref, loss_ref):
        err = y_ref[...] - t_ref[...]
        dy_ref[...] = err * (1.0 / d)

        @pl.when(pl.program_id(0) == 0)
        def _():
            loss_ref[...] = jnp.zeros_like(loss_ref)

        loss_ref[...] += 0.5 * jnp.sum(jnp.sum(err * err, axis=1, keepdims=True) * (1.0 / d), axis=0, keepdims=True)

    row = pl.BlockSpec((br, d), lambda i: (i, 0))
    return pl.pallas_call(
        body,
        name=name,
        grid=(t // br,),
        in_specs=[row, row],
        out_specs=[row, pl.BlockSpec((8, LANES), lambda i: (0, 0))],
        out_shape=[jax.ShapeDtypeStruct((t, d), F32), jax.ShapeDtypeStruct((8, LANES), F32)],
        compiler_params=_params("arbitrary"),
    )(y, target)


def _adamw(w, g, m, v, *, name, br=512):
    r, c = w.shape
    br = _row_block(r, br)
    c1 = 1.0 - ADAM_B1 ** ADAM_STEP
    c2 = 1.0 - ADAM_B2 ** ADAM_STEP

    def body(w_ref, g_ref, m_ref, v_ref, d_ref, nm_ref, nv_ref):
        gv = g_ref[...]
        nm = ADAM_B1 * m_ref[...] + (1.0 - ADAM_B1) * gv
        nv = ADAM_B2 * v_ref[...] + (1.0 - ADAM_B2) * (gv * gv)
        d_ref[...] = -ADAM_LR * ((nm / c1) / (jnp.sqrt(nv / c2) + ADAM_EPS) + ADAM_WD * w_ref[...])
        nm_ref[...] = nm
        nv_ref[...] = nv

    row = pl.BlockSpec((br, c), lambda i: (i, 0))
    return pl.pallas_call(
        body,
        name=name,
        grid=(r // br,),
        in_specs=[row] * 4,
        out_specs=[row] * 3,
        out_shape=[jax.ShapeDtypeStruct((r, c), F32)] * 3,
        compiler_params=_params("parallel"),
    )(w, g, m, v)


def _sum_slots(x, *, name, br=512):
    n, r, c = x.shape
    br = _row_block(r, br)

    def body(x_ref, o_ref):
        acc = x_ref[0].astype(F32)
        for j in range(1, n):
            acc = acc + x_ref[j].astype(F32)
        o_ref[...] = acc

    return pl.pallas_call(
        body,
        name=name,
        grid=(r // br,),
        in_specs=[pl.BlockSpec((n, br, c), lambda i: (0, i, 0))],
        out_specs=pl.BlockSpec((br, c), lambda i: (i, 0)),
        out_shape=jax.ShapeDtypeStruct((r, c), F32),
        compiler_params=_params("parallel"),
    )(x)


def _my_place():
    return lax.axis_index("x"), lax.axis_index("y"), lax.axis_index("c")


def _flip(v, bit):
    return 1 - v if bit else v


def _slot_of(px, py, pc):
    return 4 * px + 2 * py + pc


_ANY = pl.BlockSpec(memory_space=pl.ANY)


def _all_gather(x, *, name):
    r, c = x.shape

    def body(x_ref, out_ref, send_sems, recv_sems, local_sem):
        mx, my, mc = _my_place()
        me, sibling = (mx, my, mc), (mx, my, 1 - mc)
        chips = [(1 - mx, my), (mx, 1 - my), (1 - mx, 1 - my)]

        def copy(k, block, to, src=None):
            return pltpu.make_async_remote_copy(
                src_ref=out_ref.at[_slot_of(*block)] if src is None else src,
                dst_ref=out_ref.at[_slot_of(*block)],
                send_sem=send_sems.at[k],
                recv_sem=recv_sems.at[k],
                device_id=to,
                device_id_type=MESH,
            )

        mine = pltpu.make_async_copy(x_ref, out_ref.at[_slot_of(*me)], local_sem)
        mine.start()
        first = [copy(0, me, sibling, src=x_ref)]
        first += [copy(1 + j, me, (*chip, mc), src=x_ref) for j, chip in enumerate(chips)]
        for cp in first:
            cp.start()
        passed = [copy(4 + j, (*chip, mc), sibling) for j, chip in enumerate(chips)]
        for j, chip in enumerate(chips):
            copy(1 + j, (*chip, mc), me).wait_recv()
            passed[j].start()
        copy(0, sibling, me).wait_recv()
        for j, chip in enumerate(chips):
            copy(4 + j, (*chip, 1 - mc), me).wait_recv()
        for cp in first + passed:
            cp.wait_send()
        mine.wait()

    return pl.pallas_call(
        body,
        name=name,
        in_specs=[_ANY],
        out_specs=_ANY,
        out_shape=jax.ShapeDtypeStruct((N_DEV, r, c), x.dtype),
        scratch_shapes=[pltpu.SemaphoreType.DMA((7,)), pltpu.SemaphoreType.DMA((7,)), pltpu.SemaphoreType.DMA(())],
        compiler_params=pltpu.CompilerParams(has_side_effects=True),
    )(x)


def _all_to_all(x, *, name):
    n, r, c = x.shape

    def body(x_ref, out_ref, send_sems, recv_sems, local_sem):
        mx, my, mc = _my_place()
        mine = _slot_of(mx, my, mc)
        local = pltpu.make_async_copy(x_ref.at[mine], out_ref.at[mine], local_sem)
        local.start()
        copies = []
        for k in range(1, n):
            peer = (_flip(mx, k & 4), _flip(my, k & 2), _flip(mc, k & 1))
            theirs = _slot_of(*peer)
            copies.append((
                pltpu.make_async_remote_copy(
                    src_ref=x_ref.at[theirs], dst_ref=out_ref.at[mine], send_sem=send_sems.at[k - 1],
                    recv_sem=recv_sems.at[k - 1], device_id=peer, device_id_type=MESH),
                pltpu.make_async_remote_copy(
                    src_ref=x_ref.at[theirs], dst_ref=out_ref.at[theirs], send_sem=send_sems.at[k - 1],
                    recv_sem=recv_sems.at[k - 1], device_id=peer, device_id_type=MESH),
            ))
        for send, _ in copies:
            send.start()
        for send, arrival in copies:
            arrival.wait_recv()
            send.wait_send()
        local.wait()

    return pl.pallas_call(
        body,
        name=name,
        in_specs=[_ANY],
        out_specs=_ANY,
        out_shape=jax.ShapeDtypeStruct((n, r, c), x.dtype),
        scratch_shapes=[pltpu.SemaphoreType.DMA((n - 1,)), pltpu.SemaphoreType.DMA((n - 1,)), pltpu.SemaphoreType.DMA(())],
        compiler_params=pltpu.CompilerParams(has_side_effects=True),
    )(x)


_PACK_ROWS = 256


def _pack(arrays, dtype):
    flat = [a.astype(dtype).reshape(-1) for a in arrays]
    total = sum(f.shape[0] for f in flat)
    padded = -(-total // (LANES * _PACK_ROWS)) * (LANES * _PACK_ROWS)
    if padded > total:
        flat.append(jnp.zeros((padded - total,), dtype))
    return jnp.concatenate(flat).reshape(-1, LANES)


def _unpack(packed, shapes, lead=()):
    flat = packed.reshape(*lead, -1)
    out, off = [], 0
    for s in shapes:
        size = math.prod(s)
        out.append(flat[..., off:off + size].reshape(*lead, *s))
        off += size
    return out


def _gather_columns(g):
    return jnp.moveaxis(g, 0, 1).reshape(g.shape[1], -1)


def _split_columns(w):
    return jnp.moveaxis(w.reshape(w.shape[0], N_DEV, -1), 1, 0)


def _split_rows(w):
    return w.reshape(N_DEV, w.shape[0] // N_DEV, w.shape[1])


_BIG = ("w_in", "w_out", "w_ff1", "w_ff2", "w_ple_gate", "w_ple_proj")
_COLUMN_SHARDED = ("w_in", "w_ff1", "w_ple_proj")
_SMALL = ("norm1_g", "q_norm_g", "k_norm_g", "sgu_norm_g", "sgu_w", "sgu_b", "norm2_g", "norm3_g")
_ORDER = ("norm1_g", "w_in", "conv_w", "q_norm_g", "k_norm_g", "sgu_norm_g", "sgu_w", "sgu_b", "w_out", "norm2_g",
          "w_ff1", "w_ff2", "norm3_g", "w_ple_gate", "w_ple_proj")


def _bn_for(n):
    bn = 512
    while n % bn:
        bn //= 2
    return bn


def _layer_forward(h0, p16, w, s, li):
    nm = lambda k: f"{k}_l{li}"
    t = h0.shape[0]
    hn1 = _rms_fwd(h0, s["norm1_g"], name=nm("rms1"))
    proj = _matmul(hn1, w["w_in"], name=nm("proj"), bm=t, bn=256)
    y_a = _conv_fwd(proj, s["conv_w"], name=nm("conv"))
    qs, kn, v = _qk_prep(proj, s["gq"], s["gk"], name=nm("qkprep"))
    y_b, lt = _attn_fwd(qs, kn, v, name=nm("attn"))
    y_c = _sgu_fwd(proj, s["sgu_norm_g"], s["sgu_w"], s["b_exp"], name=nm("sgu"))
    mix = jnp.concatenate([y_a, y_b, y_c], axis=1)
    h1 = _matmul(mix, w["w_out"], name=nm("out"), bm=t, bn=256, extras=(h0,), epilogue=lambda acc, r: (r + acc,))
    hn2 = _rms_fwd(h1, s["norm2_g"], name=nm("rms2"))
    u, f = _matmul(hn2, w["w_ff1"], name=nm("ff1"), bm=t, bn=512, out_dtypes=(F32, BF16),
                   epilogue=lambda acc: (acc, jnp.square(jnp.maximum(acc, 0.0))))
    h2 = _matmul(f, w["w_ff2"], name=nm("ff2"), bm=512, bn=512, extras=(h1,), epilogue=lambda acc, r: (r + acc,))
    hn3 = _rms_fwd(h2, s["norm3_g"], name=nm("rms3"))
    pp = _matmul(p16, w["w_ple_proj"], name=nm("pleproj"), bm=t, bn=512)

    def gate_epilogue(acc, pp_blk, h_blk):
        gate = jax.nn.sigmoid(acc)
        return h_blk + gate * pp_blk, gate

    h3, gate = _matmul(hn3, w["w_ple_gate"], name=nm("plegate"), bm=t, bn=256, out_dtypes=(F32, F32),
                       extras=(pp, h2), epilogue=gate_epilogue)
    saved = dict(h0=h0, hn1=hn1, proj=proj, qs=qs, kn=kn, v=v, lt=lt, mix=mix, h1=h1, hn2=hn2, u=u, f=f, h2=h2,
                 hn3=hn3, pp=pp, gate=gate, p16=p16)
    return h3, saved


def _layer_backward(dh3, a, w, s, li):
    nm = lambda k: f"{k}_bwd_l{li}"
    t = dh3.shape[0]
    d = dh3.shape[1]
    dpre, dpp = _ple_bwd(dh3, a["gate"], a["pp"], name=nm("ple"))
    g_gate = _matmul(a["hn3"], dpre, name=nm("dwgate"), ta=True, bm=512, bn=512, out_dtypes=(BF16,))
    g_proj = _matmul(a["p16"], dpp, name=nm("dwproj"), ta=True, bm=256, bn=512, out_dtypes=(BF16,))
    dhn3 = _matmul(dpre, w["w_ple_gate"], name=nm("dhn3"), tb=True, bm=t, bn=256)
    dh2, g_n3 = _rms_bwd(dhn3, a["h2"], s["norm3_g"], dh3, name=nm("rms3"))
    dh2_16 = dh2.astype(BF16)
    du = _matmul(dh2_16, w["w_ff2"], name=nm("du"), tb=True, bm=t, bn=512, out_dtypes=(BF16,), extras=(a["u"],),
                 epilogue=lambda acc, u: (acc * (2.0 * jnp.maximum(u, 0.0)),))
    g_ff2 = _matmul(a["f"], dh2_16, name=nm("dwff2"), ta=True, bm=512, bn=512, out_dtypes=(BF16,))
    g_ff1 = _matmul(a["hn2"], du, name=nm("dwff1"), ta=True, bm=512, bn=512, out_dtypes=(BF16,))
    dhn2 = _matmul(du, w["w_ff1"], name=nm("dhn2"), tb=True, bm=512, bn=512)
    dh1, g_n2 = _rms_bwd(dhn2, a["h1"], s["norm2_g"], dh2, name=nm("rms2"))
    dh1_16 = dh1.astype(BF16)
    dmix = _matmul(dh1_16, w["w_out"], name=nm("dmix"), tb=True, bm=t, bn=256)
    g_out = _matmul(a["mix"], dh1_16, name=nm("dwout"), ta=True, bm=512, bn=512, out_dtypes=(BF16,))
    d_b, d_c, d_h, g_conv = _conv_bwd(dmix, a["proj"], s["conv_w"], name=nm("conv"))
    dqs, dkn, dv = _attn_bwd(dmix, a["qs"], a["kn"], a["v"], a["lt"], name=nm("attn"))
    d_q, d_k, d_v, g_q, g_k = _qk_prep_bwd(dqs, dkn, dv, a["proj"], s["gq"], s["gk"], name=nm("qkprep"))
    d_cu, d_cv, g_sn, g_sw, g_sb = _sgu_bwd(dmix, a["proj"], s["sgu_norm_g"], s["sgu_w"], s["b_exp"], name=nm("sgu"))
    dproj = jnp.concatenate([d_b, d_c, d_h, d_q, d_k, d_v, d_cu, d_cv], axis=1)
    g_in = _matmul(a["hn1"], dproj, name=nm("dwin"), ta=True, bm=512, bn=256, out_dtypes=(BF16,))
    dhn1 = _matmul(dproj, w["w_in"], name=nm("dhn1"), tb=True, bm=t, bn=256)
    dh0, g_n1 = _rms_bwd(dhn1, a["h0"], s["norm1_g"], dh1, name=nm("rms1"))
    big = dict(w_in=g_in, w_out=g_out, w_ff1=g_ff1, w_ff2=g_ff2, w_ple_gate=g_gate, w_ple_proj=g_proj)
    n_tiles = g_q.shape[1] // HEAD_DIM
    small = dict(
        norm1_g=g_n1[0], norm2_g=g_n2[0], norm3_g=g_n3[0],
        q_norm_g=g_q.reshape(n_tiles, HEAD_DIM).sum(0), k_norm_g=g_k.reshape(n_tiles, HEAD_DIM).sum(0),
        sgu_norm_g=g_sn[0], sgu_w=g_sw, sgu_b=g_sb.reshape(CHUNK, SGU_HEADS, HEAD_DIM).sum(-1).T,
        conv_w=g_conv[:CONV_TAPS],
    )
    del d
    return dh0, big, small


def kernel(x, p, norm1_g, w_in, conv_w, q_norm_g, k_norm_g, sgu_norm_g, sgu_w, sgu_b, w_out, norm2_g, w_ff1, w_ff2, norm3_g, w_ple_gate, w_ple_proj, loss_target, m_norm1_g, m_w_in, m_conv_w, m_q_norm_g, m_k_norm_g, m_sgu_norm_g, m_sgu_w, m_sgu_b, m_w_out, m_norm2_g, m_w_ff1, m_w_ff2, m_norm3_g, m_w_ple_gate, m_w_ple_proj, v_norm1_g, v_w_in, v_conv_w, v_q_norm_g, v_k_norm_g, v_sgu_norm_g, v_sgu_w, v_sgu_b, v_w_out, v_norm2_g, v_w_ff1, v_w_ff2, v_norm3_g, v_w_ple_gate, v_w_ple_proj):
    weights = dict(norm1_g=norm1_g, w_in=w_in, conv_w=conv_w, q_norm_g=q_norm_g, k_norm_g=k_norm_g,
                   sgu_norm_g=sgu_norm_g, sgu_w=sgu_w, sgu_b=sgu_b, w_out=w_out, norm2_g=norm2_g, w_ff1=w_ff1,
                   w_ff2=w_ff2, norm3_g=norm3_g, w_ple_gate=w_ple_gate, w_ple_proj=w_ple_proj)
    mom = dict(norm1_g=m_norm1_g, w_in=m_w_in, conv_w=m_conv_w, q_norm_g=m_q_norm_g, k_norm_g=m_k_norm_g,
               sgu_norm_g=m_sgu_norm_g, sgu_w=m_sgu_w, sgu_b=m_sgu_b, w_out=m_w_out, norm2_g=m_norm2_g, w_ff1=m_w_ff1,
               w_ff2=m_w_ff2, norm3_g=m_norm3_g, w_ple_gate=m_w_ple_gate, w_ple_proj=m_w_ple_proj)
    var = dict(norm1_g=v_norm1_g, w_in=v_w_in, conv_w=v_conv_w, q_norm_g=v_q_norm_g, k_norm_g=v_k_norm_g,
               sgu_norm_g=v_sgu_norm_g, sgu_w=v_sgu_w, sgu_b=v_sgu_b, w_out=v_w_out, norm2_g=v_norm2_g, w_ff1=v_w_ff1,
               w_ff2=v_w_ff2, norm3_g=v_norm3_g, w_ple_gate=v_w_ple_gate, w_ple_proj=v_w_ple_proj)
    depth = norm1_g.shape[0]
    mx, my, mc = _my_place()
    me = _slot_of(mx, my, mc)

    shard_shapes = [weights[k].shape[1:] for k in _BIG]
    full = []
    for li in range(depth):
        packed = _pack([weights[k][li] for k in _BIG], BF16)
        gathered = _all_gather(packed, name=f"gather_weights_l{li}")
        parts = dict(zip(_BIG, _unpack(gathered, shard_shapes, lead=(N_DEV,))))
        full.append({k: _gather_columns(g) if k in _COLUMN_SHARDED else g.reshape(-1, g.shape[-1])
                     for k, g in parts.items()})
    conv_all = _all_gather(_pack([conv_w], F32), name="gather_conv")
    conv_full = _gather_columns(_unpack(conv_all, [(depth * CONV_TAPS, conv_w.shape[2])], lead=(N_DEV,))[0])
    conv_full = conv_full.reshape(depth, CONV_TAPS, -1)

    small = []
    for li in range(depth):
        small.append(dict(
            norm1_g=norm1_g[li][None], norm2_g=norm2_g[li][None], norm3_g=norm3_g[li][None],
            conv_w=conv_full[li],
            gq=jnp.tile(q_norm_g[li], _QK_BLOCK // HEAD_DIM)[None], gk=jnp.tile(k_norm_g[li], _QK_BLOCK // HEAD_DIM)[None],
            sgu_norm_g=sgu_norm_g[li][None], sgu_w=sgu_w[li], b_exp=jnp.repeat(sgu_b[li].T, HEAD_DIM, axis=1),
        ))

    h = x[0]
    saved = []
    for li in range(depth):
        h, acts = _layer_forward(h, p[li, 0].astype(BF16), full[li], small[li], li)
        saved.append(acts)
    dh, loss_tile = _loss_head(h, loss_target[0], name="loss_head")
    loss = lax.psum(loss_tile[0, 0], ("x", "y", "c"))

    small_grads = [None] * depth
    big_sums = [None] * depth
    for li in reversed(range(depth)):
        dh, big, small_grads[li] = _layer_backward(dh, saved[li], full[li], small[li], li)
        shards = [_split_columns(big[k]) if k in _COLUMN_SHARDED else _split_rows(big[k]) for k in _BIG]
        rows = _pack([s[0] for s in shards], BF16).shape[0]
        outgoing = jnp.stack([_pack([s[j] for s in shards], BF16) for j in range(N_DEV)]).reshape(N_DEV, rows, LANES)
        arrived = _all_to_all(outgoing, name=f"scatter_grads_l{li}")
        big_sums[li] = _unpack(_sum_slots(arrived, name=f"sum_grads_l{li}"), shard_shapes)
    grad_x = dh[None]

    small_names = _SMALL + ("conv_w",)
    small_shapes = [(depth,) + small_grads[0][k].shape for k in small_names]
    partial = _pack([jnp.stack([small_grads[li][k] for li in range(depth)]) for k in small_names], F32)
    total = _sum_slots(_all_gather(partial, name="gather_small_grads"), name="sum_small_grads")
    grads = dict(zip(small_names, _unpack(total, small_shapes)))
    n_conv = conv_w.shape[2]
    grads["conv_w"] = lax.dynamic_slice_in_dim(grads["conv_w"], me * n_conv, n_conv, axis=2)
    for i, k in enumerate(_BIG):
        grads[k] = jnp.stack([big_sums[li][i] for li in range(depth)])

    delta, new_m, new_v = {}, {}, {}
    for k in _BIG:
        two_d = lambda a: a.reshape(-1, a.shape[-1])
        dl, nm_, nv_ = _adamw(two_d(weights[k]), two_d(grads[k]), two_d(mom[k]), two_d(var[k]), name=f"adamw_{k}")
        delta[k], new_m[k], new_v[k] = (o.reshape(weights[k].shape) for o in (dl, nm_, nv_))
    rest = _SMALL + ("conv_w",)
    rest_shapes = [weights[k].shape for k in rest]
    packs = [_pack([src[k] for k in rest], F32) for src in (weights, grads, mom, var)]
    for out, packed in zip((delta, new_m, new_v), _adamw(*packs, name="adamw_small")):
        out.update(zip(rest, _unpack(packed, rest_shapes)))

    return (loss, grad_x, *[grads[k] for k in _ORDER], *[delta[k] for k in _ORDER],
            *[new_m[k] for k in _ORDER], *[new_v[k] for k in _ORDER])
```

```python
import functools
import math

import jax
import jax.numpy as jnp
from jax import lax
from jax.experimental import pallas as pl
from jax.experimental.pallas import tpu as pltpu

F32 = jnp.float32
BF16 = jnp.bfloat16

N_DEV = 8
HEAD_DIM = 64
CONV_W = 256
ATTN_W = 512
SGU_W = 256
SGU_HEADS = 4
CHUNK = 128
CONV_TAPS = 3
EPS = 1e-6
QK_SCALE = HEAD_DIM ** -0.5

ADAM_LR = 0.001
ADAM_B1 = 0.9
ADAM_B2 = 0.999
ADAM_EPS = 1e-08
ADAM_WD = 0.01
ADAM_STEP = 10

LANES = 128
BF16_TILE_ROWS = 16
VMEM_LIMIT_BYTES = 56 * 1024 * 1024
MESH = pl.DeviceIdType.MESH


def _params(*sem):
    return pltpu.CompilerParams(dimension_semantics=sem, vmem_limit_bytes=VMEM_LIMIT_BYTES)


def _row_block(rows, cap):
    if rows <= cap:
        return rows
    return max(b for b in range(BF16_TILE_ROWS, cap + 1, BF16_TILE_ROWS) if rows % b == 0)


def _matmul(a, b, *, name, ta=False, tb=False, bm=512, bn=512, out_dtypes=(F32,), epilogue=None, extras=(),
            column_shards=False):
    m = a.shape[1] if ta else a.shape[0]
    k = a.shape[0] if ta else a.shape[1]
    n = b.shape[0] if tb else b.shape[1]
    assert k == (b.shape[1] if tb else b.shape[0])
    bm, bn = min(bm, m), min(bn, n)
    if column_shards:
        bn = n // N_DEV
    assert m % bm == 0 and n % bn == 0
    a_spec = pl.BlockSpec((k, bm), lambda i, j: (0, i)) if ta else pl.BlockSpec((bm, k), lambda i, j: (i, 0))
    b_spec = pl.BlockSpec((bn, k), lambda i, j: (j, 0)) if tb else pl.BlockSpec((k, bn), lambda i, j: (0, j))
    dims = (((0 if ta else 1,), (1 if tb else 0,)), ((), ()))
    n_ex = len(extras)
    for e in extras:
        assert e.shape == (m, n), (e.shape, m, n)

    def body(a_ref, b_ref, *rest):
        outs = rest[n_ex:]
        acc = lax.dot_general(a_ref[...], b_ref[...], dims, preferred_element_type=F32)
        res = (acc,) if epilogue is None else epilogue(acc, *[e[...] for e in rest[:n_ex]])
        for o_ref, r in zip(outs, res, strict=True):
            o_ref[...] = r.astype(o_ref.dtype)

    tile = pl.BlockSpec((bm, bn), lambda i, j: (i, j))
    out_tile, out_dims = tile, (m, n)
    if column_shards:
        out_tile, out_dims = pl.BlockSpec((None, bm, bn), lambda i, j: (j, i, 0)), (N_DEV, m, bn)
    out = pl.pallas_call(
        body,
        name=name,
        grid=(m // bm, n // bn),
        in_specs=[a_spec, b_spec] + [tile] * n_ex,
        out_specs=[out_tile] * len(out_dtypes),
        out_shape=[jax.ShapeDtypeStruct(out_dims, d) for d in out_dtypes],
        compiler_params=_params("parallel", "parallel"),
    )(a, b, *extras)
    return out[0] if len(out_dtypes) == 1 else out


def _rms_fwd(h, g, *, name, br=512):
    t, d = h.shape
    br = min(br, t)

    def body(h_ref, g_ref, o_ref):
        x = h_ref[...]
        r = lax.rsqrt(jnp.mean(x * x, axis=-1, keepdims=True) + EPS)
        o_ref[...] = (x * r * g_ref[...]).astype(o_ref.dtype)

    return pl.pallas_call(
        body,
        name=name,
        grid=(t // br,),
        in_specs=[pl.BlockSpec((br, d), lambda i: (i, 0)), pl.BlockSpec((1, d), lambda i: (0, 0))],
        out_specs=pl.BlockSpec((br, d), lambda i: (i, 0)),
        out_shape=jax.ShapeDtypeStruct((t, d), BF16),
        compiler_params=_params("parallel"),
    )(h, g)


def _rms_bwd(dy, h, g, dres, *, name, br=512):
    t, d = h.shape
    br = min(br, t)

    def body(dy_ref, h_ref, g_ref, dres_ref, dh_ref, dh16_ref, dg_ref):
        x = h_ref[...]
        dyv = dy_ref[...]
        r = lax.rsqrt(jnp.mean(x * x, axis=-1, keepdims=True) + EPS)
        xhat = x * r
        dxhat = dyv * g_ref[...]
        dh = dres_ref[...] + r * (dxhat - xhat * jnp.mean(dxhat * xhat, axis=-1, keepdims=True))
        dh_ref[...] = dh
        dh16_ref[...] = dh.astype(dh16_ref.dtype)

        @pl.when(pl.program_id(0) == 0)
        def _():
            dg_ref[...] = jnp.zeros_like(dg_ref)

        dg_ref[...] += jnp.sum(dyv * xhat, axis=0, keepdims=True)

    row = pl.BlockSpec((br, d), lambda i: (i, 0))
    vec = pl.BlockSpec((1, d), lambda i: (0, 0))
    return pl.pallas_call(
        body,
        name=name,
        grid=(t // br,),
        in_specs=[row, row, vec, row],
        out_specs=[row, row, vec],
        out_shape=[jax.ShapeDtypeStruct((t, d), F32), jax.ShapeDtypeStruct((t, d), BF16),
                   jax.ShapeDtypeStruct((1, d), F32)],
        compiler_params=_params("arbitrary"),
    )(dy, h, g, dres)


def _group_mean(x, width):
    grp = lax.broadcasted_iota(jnp.int32, x.shape, 1) // HEAD_DIM
    out = jnp.zeros_like(x)
    for gi in range(width // HEAD_DIM):
        m = grp == gi
        s = jnp.sum(jnp.where(m, x, 0.0), axis=1, keepdims=True)
        out = jnp.where(m, s, out)
    return out * (1.0 / HEAD_DIM)


def _gelu(x):
    return 0.5 * x * (1.0 + lax.erf(x * (2.0 ** -0.5)))


def _gelu_grad(x):
    cdf = 0.5 * (1.0 + lax.erf(x * (2.0 ** -0.5)))
    pdf = jnp.exp(-0.5 * x * x) * (1.0 / math.sqrt(2.0 * math.pi))
    return cdf + x * pdf


def _shift_down(z, s, row):
    return jnp.where(row >= s, pltpu.roll(z, s, 0), 0.0)


def _shift_up(z, s, row, t):
    return jnp.where(row < t - s, pltpu.roll(z, t - s, 0), 0.0)


def _conv_fwd(proj, conv_w, *, name):
    t = proj.shape[0]
    nb = CONV_W // LANES

    def body(b_ref, c_ref, h_ref, w_ref, o_ref):
        row = lax.broadcasted_iota(jnp.int32, (t, LANES), 0)
        z = c_ref[...] * h_ref[...]
        w = w_ref[...]
        conv = w[2:3, :] * z + w[1:2, :] * _shift_down(z, 1, row) + w[0:1, :] * _shift_down(z, 2, row)
        o_ref[...] = (b_ref[...] * conv).astype(o_ref.dtype)

    return pl.pallas_call(
        body,
        name=name,
        grid=(nb,),
        in_specs=[
            pl.BlockSpec((t, LANES), lambda j: (0, j)),
            pl.BlockSpec((t, LANES), lambda j: (0, nb + j)),
            pl.BlockSpec((t, LANES), lambda j: (0, 2 * nb + j)),
            pl.BlockSpec((CONV_TAPS, LANES), lambda j: (0, j)),
        ],
        out_specs=pl.BlockSpec((t, LANES), lambda j: (0, j)),
        out_shape=jax.ShapeDtypeStruct((t, CONV_W), BF16),
        compiler_params=_params("parallel"),
    )(proj, proj, proj, conv_w)


def _conv_bwd(dmix, proj, conv_w, *, name):
    t = proj.shape[0]
    nb = CONV_W // LANES

    def body(dy_ref, b_ref, c_ref, h_ref, w_ref, db_ref, dc_ref, dh_ref, dw_ref):
        row = lax.broadcasted_iota(jnp.int32, (t, LANES), 0)
        ac, ah = c_ref[...], h_ref[...]
        z = ac * ah
        w = w_ref[...]
        z1 = _shift_down(z, 1, row)
        z2 = _shift_down(z, 2, row)
        conv = w[2:3, :] * z + w[1:2, :] * z1 + w[0:1, :] * z2
        dy = dy_ref[...]
        db_ref[...] = (dy * conv).astype(db_ref.dtype)
        dconv = dy * b_ref[...]
        dz = w[2:3, :] * dconv + w[1:2, :] * _shift_up(dconv, 1, row, t) + w[0:1, :] * _shift_up(dconv, 2, row, t)
        dc_ref[...] = (dz * ah).astype(dc_ref.dtype)
        dh_ref[...] = (dz * ac).astype(dh_ref.dtype)
        dw_ref[...] = jnp.zeros_like(dw_ref)
        dw_ref[0:1, :] = jnp.sum(dconv * z2, axis=0, keepdims=True)
        dw_ref[1:2, :] = jnp.sum(dconv * z1, axis=0, keepdims=True)
        dw_ref[2:3, :] = jnp.sum(dconv * z, axis=0, keepdims=True)

    col = lambda off: pl.BlockSpec((t, LANES), lambda j: (0, off + j))
    return pl.pallas_call(
        body,
        name=name,
        grid=(nb,),
        in_specs=[col(0), col(0), col(nb), col(2 * nb), pl.BlockSpec((CONV_TAPS, LANES), lambda j: (0, j))],
        out_specs=[col(0), col(0), col(0), pl.BlockSpec((8, LANES), lambda j: (0, j))],
        out_shape=[jax.ShapeDtypeStruct((t, CONV_W), BF16)] * 3 + [jax.ShapeDtypeStruct((8, CONV_W), F32)],
        compiler_params=_params("parallel"),
    )(dmix, proj, proj, proj, conv_w)


_QK_BLOCK = 256


def _qk_prep(proj, gq, gk, *, name, br=512):
    t = proj.shape[0]
    br = min(br, t)
    nb = ATTN_W // _QK_BLOCK
    q0 = (3 * CONV_W) // _QK_BLOCK

    def body(q_ref, k_ref, v_ref, gq_ref, gk_ref, qo_ref, ko_ref, vo_ref):
        q = q_ref[...]
        k = k_ref[...]
        rq = lax.rsqrt(_group_mean(q * q, _QK_BLOCK) + EPS)
        rk = lax.rsqrt(_group_mean(k * k, _QK_BLOCK) + EPS)
        qo_ref[...] = ((q * rq * gq_ref[...]).astype(BF16) * QK_SCALE).astype(qo_ref.dtype)
        ko_ref[...] = (k * rk * gk_ref[...]).astype(ko_ref.dtype)
        vo_ref[...] = v_ref[...].astype(vo_ref.dtype)

    col = lambda off: pl.BlockSpec((br, _QK_BLOCK), lambda i, j: (i, off + j))
    vec = pl.BlockSpec((1, _QK_BLOCK), lambda i, j: (0, 0))
    return pl.pallas_call(
        body,
        name=name,
        grid=(t // br, nb),
        in_specs=[col(q0), col(q0 + nb), col(q0 + 2 * nb), vec, vec],
        out_specs=[col(0)] * 3,
        out_shape=[jax.ShapeDtypeStruct((t, ATTN_W), BF16)] * 3,
        compiler_params=_params("parallel", "parallel"),
    )(proj, proj, proj, gq, gk)


def _qk_prep_bwd(dqs, dkn, dv, proj, gq, gk, *, name, br=512):
    t = proj.shape[0]
    br = min(br, t)
    nb = ATTN_W // _QK_BLOCK
    q0 = (3 * CONV_W) // _QK_BLOCK

    def norm_bwd(dy, x, g):
        r = lax.rsqrt(_group_mean(x * x, _QK_BLOCK) + EPS)
        xhat = x * r
        dxhat = dy * g
        dx = r * (dxhat - xhat * _group_mean(dxhat * xhat, _QK_BLOCK))
        return dx, jnp.sum(dy * xhat, axis=0, keepdims=True)

    def body(dq_ref, dk_ref, dv_ref, q_ref, k_ref, gq_ref, gk_ref, oq_ref, ok_ref, ov_ref, dgq_ref, dgk_ref):
        dq, dgq = norm_bwd(dq_ref[...] * QK_SCALE, q_ref[...], gq_ref[...])
        dk, dgk = norm_bwd(dk_ref[...], k_ref[...], gk_ref[...])
        oq_ref[...] = dq.astype(oq_ref.dtype)
        ok_ref[...] = dk.astype(ok_ref.dtype)
        ov_ref[...] = dv_ref[...].astype(ov_ref.dtype)

        @pl.when((pl.program_id(0) == 0) & (pl.program_id(1) == 0))
        def _():
            dgq_ref[...] = jnp.zeros_like(dgq_ref)
            dgk_ref[...] = jnp.zeros_like(dgk_ref)

        dgq_ref[...] += dgq
        dgk_ref[...] += dgk

    col = lambda off: pl.BlockSpec((br, _QK_BLOCK), lambda i, j: (i, off + j))
    vec = pl.BlockSpec((1, _QK_BLOCK), lambda i, j: (0, 0))
    return pl.pallas_call(
        body,
        name=name,
        grid=(t // br, nb),
        in_specs=[col(0), col(0), col(0), col(q0), col(q0 + nb), vec, vec],
        out_specs=[col(0), col(0), col(0), vec, vec],
        out_shape=[jax.ShapeDtypeStruct((t, ATTN_W), BF16)] * 3 + [jax.ShapeDtypeStruct((1, _QK_BLOCK), F32)] * 2,
        compiler_params=_params("arbitrary", "arbitrary"),
    )(dqs, dkn, dv, proj, proj, gq, gk)


def _split_bf16(x):
    hi = x.astype(BF16)
    lo = (x - hi.astype(F32)).astype(BF16)
    return hi, lo


def _dot_exact01(x, m01):
    hi, lo = _split_bf16(x)
    return jnp.dot(hi, m01, preferred_element_type=F32) + jnp.dot(lo, m01, preferred_element_type=F32)


def _log_sigmoids(z):
    sp = jnp.log(1.0 + jnp.exp(-jnp.abs(z)))
    return jnp.minimum(z, 0.0) - sp, jnp.minimum(-z, 0.0) - sp


_NT = (((1,), (1,)), ((), ()))
_TN = (((0,), (0,)), ((), ()))


def _attn_fwd(qs, kn, v, *, name, tq=256, tk=256):
    t = qs.shape[0]
    tq, tk = min(tq, t), min(tk, t)
    assert tq % tk == 0 and t % tq == 0
    n_pairs = ATTN_W // LANES

    def body(q_ref, k_ref, v_ref, o_ref, lt_ref, acc_ref, carry_ref):
        qb = pl.program_id(1)
        half = lax.broadcasted_iota(jnp.int32, (1, LANES), 1) // HEAD_DIM
        q = q_ref[...]
        qh = [jnp.where(half == h, q, jnp.zeros_like(q)) for h in range(2)]
        row = qb * tq + lax.broadcasted_iota(jnp.int32, (tq, tk), 0)
        col = lax.broadcasted_iota(jnp.int32, (tq, tk), 1)
        jj = lax.broadcasted_iota(jnp.int32, (tk, tk), 0)
        ss = lax.broadcasted_iota(jnp.int32, (tk, tk), 1)
        later = (jj > ss).astype(BF16)
        acc_ref[...] = jnp.zeros_like(acc_ref)
        carry_ref[...] = jnp.zeros_like(carry_ref)
        n_kb = (qb + 1) * (tq // tk)

        def step(i, _):
            kb = n_kb - 1 - i
            start = pl.multiple_of(kb * tk, tk)
            kblk = k_ref[pl.ds(start, tk), :]
            vblk = v_ref[pl.ds(start, tk), :]
            causal = (start + col) < row
            for h in range(2):
                z = lax.dot_general(qh[h], kblk, _NT, preferred_element_type=F32)
                lb, lr = _log_sigmoids(z)
                lr = jnp.where(causal, lr, 0.0)
                carry = carry_ref[h]
                suffix = _dot_exact01(lr, later) + carry[:, 0:1]
                w = jnp.where(causal, jnp.exp(lb + suffix), 0.0)
                vh = jnp.where(half == h, vblk, jnp.zeros_like(vblk))
                acc_ref[...] += jnp.dot(w.astype(BF16), vh, preferred_element_type=F32)
                carry_ref[h] = carry + jnp.sum(lr, axis=1, keepdims=True)
            return 0

        lax.fori_loop(0, n_kb, step, 0)
        o_ref[...] = acc_ref[...].astype(o_ref.dtype)
        lt_ref[...] = jnp.where(half == 0, carry_ref[0], carry_ref[1])

    return pl.pallas_call(
        body,
        name=name,
        grid=(n_pairs, t // tq),
        in_specs=[
            pl.BlockSpec((tq, LANES), lambda p, i: (i, p)),
            pl.BlockSpec((t, LANES), lambda p, i: (0, p)),
            pl.BlockSpec((t, LANES), lambda p, i: (0, p)),
        ],
        out_specs=[pl.BlockSpec((tq, LANES), lambda p, i: (i, p))] * 2,
        out_shape=[jax.ShapeDtypeStruct((t, ATTN_W), BF16), jax.ShapeDtypeStruct((t, ATTN_W), F32)],
        scratch_shapes=[pltpu.VMEM((tq, LANES), F32), pltpu.VMEM((2, tq, LANES), F32)],
        compiler_params=_params("parallel", "parallel"),
    )(qs, kn, v)


def _attn_bwd(dmix, qs, kn, v, lt, *, name, tq=256, tk=256):
    t = qs.shape[0]
    tq, tk = min(tq, t), min(tk, t)
    assert tq % tk == 0 and t % tq == 0
    n_pairs = ATTN_W // LANES
    dy0 = CONV_W // LANES

    def body(do_ref, q_ref, k_ref, v_ref, lt_ref, dq_ref, dk_ref, dv_ref, cc_ref, cg_ref):
        qb = pl.program_id(1)
        half = lax.broadcasted_iota(jnp.int32, (1, LANES), 1) // HEAD_DIM
        q = q_ref[...]
        do = do_ref[...].astype(BF16)
        lt = lt_ref[...]
        row = qb * tq + lax.broadcasted_iota(jnp.int32, (tq, tk), 0)
        col = lax.broadcasted_iota(jnp.int32, (tq, tk), 1)
        jj = lax.broadcasted_iota(jnp.int32, (tk, tk), 0)
        ss = lax.broadcasted_iota(jnp.int32, (tk, tk), 1)
        upto = (jj <= ss).astype(BF16)
        before = (jj < ss).astype(BF16)
        qh, doh, lth = [], [], []
        for h in range(2):
            qh.append(jnp.where(half == h, q, jnp.zeros_like(q)))
            doh.append(jnp.where(half == h, do, jnp.zeros_like(do)))
            lth.append(jnp.sum(jnp.where(lax.broadcasted_iota(jnp.int32, (tq, LANES), 1) == h * HEAD_DIM, lt, 0.0),
                               axis=1, keepdims=True))

        @pl.when(qb == 0)
        def _():
            dk_ref[...] = jnp.zeros_like(dk_ref)
            dv_ref[...] = jnp.zeros_like(dv_ref)

        dq_ref[...] = jnp.zeros_like(dq_ref)
        cc_ref[...] = jnp.zeros_like(cc_ref)
        cg_ref[...] = jnp.zeros_like(cg_ref)
        n_kb = (qb + 1) * (tq // tk)

        def step(kb, _):
            start = pl.multiple_of(kb * tk, tk)
            kblk = k_ref[pl.ds(start, tk), :]
            vblk = v_ref[pl.ds(start, tk), :]
            causal = (start + col) < row
            for h in range(2):
                z = lax.dot_general(qh[h], kblk, _NT, preferred_element_type=F32)
                lb, lr = _log_sigmoids(z)
                lr = jnp.where(causal, lr, 0.0)
                cc = cc_ref[h]
                prefix = _dot_exact01(lr, upto) + cc[:, 0:1]
                a = jnp.where(causal, jnp.exp(lb + (lth[h] - prefix)), 0.0)
                vh = jnp.where(half == h, vblk, jnp.zeros_like(vblk))
                da = lax.dot_general(doh[h], vh, _NT, preferred_element_type=F32)
                g = da * a
                cg = cg_ref[h]
                p = _dot_exact01(g, before) + cg[:, 0:1]
                beta = jnp.exp(lb)
                dz = jnp.where(causal, g * (1.0 - beta) - p * beta, 0.0).astype(BF16)
                kh = jnp.where(half == h, kblk, jnp.zeros_like(kblk))
                dq_ref[...] += jnp.dot(dz, kh, preferred_element_type=F32)
                dk_ref[pl.ds(start, tk), :] += lax.dot_general(dz, qh[h], _TN, preferred_element_type=F32)
                dv_ref[pl.ds(start, tk), :] += lax.dot_general(a.astype(BF16), doh[h], _TN, preferred_element_type=F32)
                cc_ref[h] = cc + jnp.sum(lr, axis=1, keepdims=True)
                cg_ref[h] = cg + jnp.sum(g, axis=1, keepdims=True)
            return 0

        lax.fori_loop(0, n_kb, step, 0)

    qblk = pl.BlockSpec((tq, LANES), lambda p, i: (i, p))
    whole = pl.BlockSpec((t, LANES), lambda p, i: (0, p))
    return pl.pallas_call(
        body,
        name=name,
        grid=(n_pairs, t // tq),
        in_specs=[pl.BlockSpec((tq, LANES), lambda p, i: (i, dy0 + p)), qblk, whole, whole, qblk],
        out_specs=[qblk, whole, whole],
        out_shape=[jax.ShapeDtypeStruct((t, ATTN_W), F32)] * 3,
        scratch_shapes=[pltpu.VMEM((2, tq, LANES), F32), pltpu.VMEM((2, tq, LANES), F32)],
        compiler_params=_params("parallel", "arbitrary"),
    )(dmix, qs, kn, v, lt)


def _sgu_weights(w_ref):
    tt = lax.broadcasted_iota(jnp.int32, (CHUNK, CHUNK), 0)
    ss = lax.broadcasted_iota(jnp.int32, (CHUNK, CHUNK), 1)
    tril = ss <= tt
    return [jnp.where(tril, w_ref[gi], 0.0).astype(BF16) for gi in range(SGU_HEADS)], tril


def _sgu_fwd(proj, g_v, w_s, b_exp, *, name):
    t = proj.shape[0]
    u0 = (3 * CONV_W + 3 * ATTN_W) // SGU_W

    def body(u_ref, v_ref, g_ref, w_ref, b_ref, o_ref):
        grp = lax.broadcasted_iota(jnp.int32, (1, SGU_W), 1) // HEAD_DIM
        u = _gelu(u_ref[...])
        vv = _gelu(v_ref[...])
        vn = (vv * lax.rsqrt(_group_mean(vv * vv, SGU_W) + EPS) * g_ref[...]).astype(BF16)
        wm, _ = _sgu_weights(w_ref)
        sv = b_ref[...]
        for gi in range(SGU_HEADS):
            sv = sv + jnp.dot(wm[gi], jnp.where(grp == gi, vn, jnp.zeros_like(vn)), preferred_element_type=F32)
        o_ref[...] = (u * sv).astype(o_ref.dtype)

    return pl.pallas_call(
        body,
        name=name,
        grid=(t // CHUNK,),
        in_specs=[
            pl.BlockSpec((CHUNK, SGU_W), lambda i: (i, u0)),
            pl.BlockSpec((CHUNK, SGU_W), lambda i: (i, u0 + 1)),
            pl.BlockSpec((1, SGU_W), lambda i: (0, 0)),
            pl.BlockSpec((SGU_HEADS, CHUNK, CHUNK), lambda i: (0, 0, 0)),
            pl.BlockSpec((CHUNK, SGU_W), lambda i: (0, 0)),
        ],
        out_specs=pl.BlockSpec((CHUNK, SGU_W), lambda i: (i, 0)),
        out_shape=jax.ShapeDtypeStruct((t, SGU_W), BF16),
        compiler_params=_params("parallel"),
    )(proj, proj, g_v, w_s, b_exp)


def _sgu_bwd(dmix, proj, g_v, w_s, b_exp, *, name):
    t = proj.shape[0]
    u0 = (3 * CONV_W + 3 * ATTN_W) // SGU_W
    dy0 = (CONV_W + ATTN_W) // SGU_W

    def body(dy_ref, u_ref, v_ref, g_ref, w_ref, b_ref, du_ref, dv_ref, dg_ref, dw_ref, db_ref):
        grp = lax.broadcasted_iota(jnp.int32, (1, SGU_W), 1) // HEAD_DIM
        cu, cv = u_ref[...], v_ref[...]
        u = _gelu(cu)
        vv = _gelu(cv)
        r = lax.rsqrt(_group_mean(vv * vv, SGU_W) + EPS)
        xhat = vv * r
        gain = g_ref[...]
        vn = (xhat * gain).astype(BF16)
        wm, tril = _sgu_weights(w_ref)
        vng = [jnp.where(grp == gi, vn, jnp.zeros_like(vn)) for gi in range(SGU_HEADS)]
        sv = b_ref[...]
        for gi in range(SGU_HEADS):
            sv = sv + jnp.dot(wm[gi], vng[gi], preferred_element_type=F32)
        dy = dy_ref[...]
        du_ref[...] = (dy * sv * _gelu_grad(cu)).astype(du_ref.dtype)
        dsv = dy * u
        dsv16 = dsv.astype(BF16)

        @pl.when(pl.program_id(0) == 0)
        def _():
            dg_ref[...] = jnp.zeros_like(dg_ref)
            dw_ref[...] = jnp.zeros_like(dw_ref)
            db_ref[...] = jnp.zeros_like(db_ref)

        db_ref[...] += dsv
        dvn = jnp.zeros_like(dsv)
        for gi in range(SGU_HEADS):
            dw = lax.dot_general(dsv16, vng[gi], _NT, preferred_element_type=F32)
            dw_ref[gi] += jnp.where(tril, dw, 0.0)
            dvn_g = lax.dot_general(wm[gi], dsv16, _TN, preferred_element_type=F32)
            dvn = jnp.where(grp == gi, dvn_g, dvn)
        dg_ref[...] += jnp.sum(dvn * xhat, axis=0, keepdims=True)
        dxhat = dvn * gain
        dvv = r * (dxhat - xhat * _group_mean(dxhat * xhat, SGU_W))
        dv_ref[...] = (dvv * _gelu_grad(cv)).astype(dv_ref.dtype)

    return pl.pallas_call(
        body,
        name=name,
        grid=(t // CHUNK,),
        in_specs=[
            pl.BlockSpec((CHUNK, SGU_W), lambda i: (i, dy0)),
            pl.BlockSpec((CHUNK, SGU_W), lambda i: (i, u0)),
            pl.BlockSpec((CHUNK, SGU_W), lambda i: (i, u0 + 1)),
            pl.BlockSpec((1, SGU_W), lambda i: (0, 0)),
            pl.BlockSpec((SGU_HEADS, CHUNK, CHUNK), lambda i: (0, 0, 0)),
            pl.BlockSpec((CHUNK, SGU_W), lambda i: (0, 0)),
        ],
        out_specs=[
            pl.BlockSpec((CHUNK, SGU_W), lambda i: (i, 0)),
            pl.BlockSpec((CHUNK, SGU_W), lambda i: (i, 0)),
            pl.BlockSpec((1, SGU_W), lambda i: (0, 0)),
            pl.BlockSpec((SGU_HEADS, CHUNK, CHUNK), lambda i: (0, 0, 0)),
            pl.BlockSpec((CHUNK, SGU_W), lambda i: (0, 0)),
        ],
        out_shape=[
            jax.ShapeDtypeStruct((t, SGU_W), BF16),
            jax.ShapeDtypeStruct((t, SGU_W), BF16),
            jax.ShapeDtypeStruct((1, SGU_W), F32),
            jax.ShapeDtypeStruct((SGU_HEADS, CHUNK, CHUNK), F32),
            jax.ShapeDtypeStruct((CHUNK, SGU_W), F32),
        ],
        compiler_params=_params("arbitrary"),
    )(dmix, proj, proj, g_v, w_s, b_exp)


def _ple_bwd(dh, gate, pp, order_after, *, name, br=512):
    t, d = dh.shape
    br = min(br, t)

    def body(dh_ref, g_ref, p_ref, order_ref, dpre_ref, dpp_ref):
        dhv, g = dh_ref[...], g_ref[...]
        dpre_ref[...] = (dhv * p_ref[...] * g * (1.0 - g)).astype(dpre_ref.dtype)
        dpp_ref[...] = (dhv * g).astype(dpp_ref.dtype)

    row = pl.BlockSpec((br, d), lambda i: (i, 0))
    return pl.pallas_call(
        body,
        name=name,
        grid=(t // br,),
        in_specs=[row] * 3 + [pl.BlockSpec(order_after.shape, lambda i: (0, 0))],
        out_specs=[row] * 2,
        out_shape=[jax.ShapeDtypeStruct((t, d), BF16)] * 2,
        compiler_params=_params("parallel"),
    )(dh, gate, pp, order_after)


def _loss_head(y, target, *, name, br=512):
    t, d = y.shape
    br = min(br, t)

    def body(y_ref, t_ref, dy_ref, loss_ref):
        err = y_ref[...] - t_ref[...]
        dy_ref[...] = err * (1.0 / d)

        @pl.when(pl.program_id(0) == 0)
        def _():
            loss_ref[...] = jnp.zeros_like(loss_ref)

        loss_ref[...] += 0.5 * jnp.sum(jnp.sum(err * err, axis=1, keepdims=True) * (1.0 / d), axis=0, keepdims=True)

    row = pl.BlockSpec((br, d), lambda i: (i, 0))
    return pl.pallas_call(
        body,
        name=name,
        grid=(t // br,),
        in_specs=[row, row],
        out_specs=[row, pl.BlockSpec((8, LANES), lambda i: (0, 0))],
        out_shape=[jax.ShapeDtypeStruct((t, d), F32), jax.ShapeDtypeStruct((8, LANES), F32)],
        compiler_params=_params("arbitrary"),
    )(y, target)


def _adamw_update(w, g, m, v):
    nm = ADAM_B1 * m + (1.0 - ADAM_B1) * g
    nv = ADAM_B2 * v + (1.0 - ADAM_B2) * (g * g)
    m_hat = nm / (1.0 - ADAM_B1 ** ADAM_STEP)
    v_hat = nv / (1.0 - ADAM_B2 ** ADAM_STEP)
    return -ADAM_LR * (m_hat / (jnp.sqrt(v_hat) + ADAM_EPS) + ADAM_WD * w), nm, nv


def _adamw(w, g, m, v, *, name, br=512):
    r, c = w.shape
    br = _row_block(r, br)

    def body(w_ref, g_ref, m_ref, v_ref, d_ref, nm_ref, nv_ref):
        d_ref[...], nm_ref[...], nv_ref[...] = _adamw_update(w_ref[...], g_ref[...], m_ref[...], v_ref[...])

    row = pl.BlockSpec((br, c), lambda i: (i, 0))
    return pl.pallas_call(
        body,
        name=name,
        grid=(r // br,),
        in_specs=[row] * 4,
        out_specs=[row] * 3,
        out_shape=[jax.ShapeDtypeStruct((r, c), F32)] * 3,
        compiler_params=_params("parallel"),
    )(w, g, m, v)


def _sum_slots(x, *, name, br=512):
    n, r, c = x.shape
    br = _row_block(r, br)

    def body(x_ref, o_ref):
        acc = x_ref[0].astype(F32)
        for j in range(1, n):
            acc = acc + x_ref[j].astype(F32)
        o_ref[...] = acc

    return pl.pallas_call(
        body,
        name=name,
        grid=(r // br,),
        in_specs=[pl.BlockSpec((n, br, c), lambda i: (0, i, 0))],
        out_specs=pl.BlockSpec((br, c), lambda i: (i, 0)),
        out_shape=jax.ShapeDtypeStruct((r, c), F32),
        compiler_params=_params("parallel"),
    )(x)


_ADAMW_BLOCK_ELEMS = 128 * 1024


def _adamw_reduce(w, arrived, m, v, *, name):
    depth, r, c = w.shape
    br = _row_block(r, max(BF16_TILE_ROWS, _ADAMW_BLOCK_ELEMS // (-(-c // LANES) * LANES)))

    def body(w_ref, m_ref, v_ref, *rest):
        parts, (g_ref, d_ref, nm_ref, nv_ref) = rest[:depth], rest[depth:]
        for li in range(depth):

            @pl.when(pl.program_id(0) == li)
            def _(li=li):
                g = parts[li][0].astype(F32)
                for j in range(1, N_DEV):
                    g = g + parts[li][j].astype(F32)
                g_ref[...] = g
                d_ref[...], nm_ref[...], nv_ref[...] = _adamw_update(w_ref[...], g, m_ref[...], v_ref[...])

    cur = pl.BlockSpec((None, br, c), lambda l, i: (l, i, 0))
    slots = [pl.BlockSpec((N_DEV, br, c), lambda l, i, li=li: (0, jnp.where(l == li, i, 0), 0)) for li in range(depth)]
    return pl.pallas_call(
        body,
        name=name,
        grid=(depth, r // br),
        in_specs=[cur, cur, cur] + slots,
        out_specs=[cur] * 4,
        out_shape=[jax.ShapeDtypeStruct((depth, r, c), F32)] * 4,
        compiler_params=_params("arbitrary", "arbitrary"),
    )(w, m, v, *arrived)


def _my_place():
    return lax.axis_index("x"), lax.axis_index("y"), lax.axis_index("c")


def _flip(v, bit):
    return 1 - v if bit else v


def _slot_of(px, py, pc):
    return 4 * px + 2 * py + pc


_ANY = pl.BlockSpec(memory_space=pl.ANY)


def _all_gather(x, *, name):
    r, c = x.shape

    def body(x_ref, out_ref, send_sems, recv_sems, local_sem):
        mx, my, mc = _my_place()
        me, sibling = (mx, my, mc), (mx, my, 1 - mc)
        chips = [(1 - mx, my), (mx, 1 - my), (1 - mx, 1 - my)]

        def copy(k, block, to, src=None):
            return pltpu.make_async_remote_copy(
                src_ref=out_ref.at[_slot_of(*block)] if src is None else src,
                dst_ref=out_ref.at[_slot_of(*block)],
                send_sem=send_sems.at[k],
                recv_sem=recv_sems.at[k],
                device_id=to,
                device_id_type=MESH,
            )

        mine = pltpu.make_async_copy(x_ref, out_ref.at[_slot_of(*me)], local_sem)
        mine.start()
        first = [copy(0, me, sibling, src=x_ref)]
        first += [copy(1 + j, me, (*chip, mc), src=x_ref) for j, chip in enumerate(chips)]
        for cp in first:
            cp.start()
        passed = [copy(4 + j, (*chip, mc), sibling) for j, chip in enumerate(chips)]
        for j, chip in enumerate(chips):
            copy(1 + j, (*chip, mc), me).wait_recv()
            passed[j].start()
        copy(0, sibling, me).wait_recv()
        for j, chip in enumerate(chips):
            copy(4 + j, (*chip, 1 - mc), me).wait_recv()
        for cp in first + passed:
            cp.wait_send()
        mine.wait()

    return pl.pallas_call(
        body,
        name=name,
        in_specs=[_ANY],
        out_specs=_ANY,
        out_shape=jax.ShapeDtypeStruct((N_DEV, r, c), x.dtype),
        scratch_shapes=[pltpu.SemaphoreType.DMA((7,)), pltpu.SemaphoreType.DMA((7,)), pltpu.SemaphoreType.DMA(())],
        compiler_params=pltpu.CompilerParams(has_side_effects=True),
    )(x)


_HBM = pl.BlockSpec(memory_space=pltpu.HBM)
_SEM = pl.BlockSpec(memory_space=pltpu.SEMAPHORE)
_DATAFLOW = pltpu.SideEffectType.DATAFLOW_SIDE_EFFECTING


def _exchange_copies(src_refs, land_refs, send_sem, recv_sem, scatter):
    mx, my, mc = _my_place()
    mine = _slot_of(mx, my, mc)
    copies = []
    for src, land in zip(src_refs, land_refs, strict=True):
        for k in range(1, N_DEV):
            peer = (_flip(mx, k & 4), _flip(my, k & 2), _flip(mc, k & 1))
            copies.append(pltpu.make_async_remote_copy(
                src_ref=src.at[_slot_of(*peer)] if scatter else src, dst_ref=land.at[mine],
                send_sem=send_sem, recv_sem=recv_sem, device_id=peer, device_id_type=MESH))
    return copies


def _exchange_start(srcs, *, scatter, name):
    n = len(srcs)
    lands = [lax.empty(s.shape if scatter else (N_DEV, *s.shape), s.dtype) for s in srcs]

    def body(*refs):
        send_sem, recv_sem = refs[2 * n], refs[2 * n + 1]
        for cp in _exchange_copies(refs[:n], refs[n:2 * n], send_sem, recv_sem, scatter):
            cp.start()
        refs[-1][...] = jnp.zeros_like(refs[-1])

    thru = [pltpu.HBM(a.shape, a.dtype) for a in (*srcs, *lands)]
    out = pl.pallas_call(
        body,
        name=name,
        in_specs=[_HBM] * (2 * n),
        out_specs=(_SEM, _SEM, *[_HBM] * (2 * n), pl.BlockSpec(memory_space=pltpu.VMEM)),
        out_shape=(pltpu.SemaphoreType.DMA(()), pltpu.SemaphoreType.DMA(()), *thru,
                   jax.ShapeDtypeStruct((8, LANES), F32)),
        input_output_aliases={i: 2 + i for i in range(2 * n)},
        compiler_params=pltpu.CompilerParams(has_side_effects=_DATAFLOW),
    )(*[pltpu.with_memory_space_constraint(a, pltpu.HBM) for a in (*srcs, *lands)])
    return out[:-1], out[-1]


def _exchange_wait(started, after, *, scatter, name):
    send_sem, recv_sem, *thru = started
    n = len(thru) // 2

    def body(*refs):
        for cp in _exchange_copies(refs[:n], refs[n:2 * n], refs[2 * n], refs[2 * n + 1], scatter):
            cp.wait_send()
            cp.wait_recv()

    out = pl.pallas_call(
        body,
        name=name,
        in_specs=[_HBM] * (2 * n) + [_SEM, _SEM, _ANY],
        out_specs=[_HBM] * (2 * n),
        out_shape=[pltpu.HBM(a.shape, a.dtype) for a in thru],
        input_output_aliases={i: i for i in range(2 * n)},
        compiler_params=pltpu.CompilerParams(has_side_effects=_DATAFLOW),
    )(*thru, send_sem, recv_sem, after)
    return out[:n], out[n:]


def _with_own_slot(landed, own, me):
    return lax.dynamic_update_slice(landed, own[None], (me,) + (0,) * own.ndim)


_PACK_ROWS = 256


def _pack(arrays, dtype):
    flat = [a.astype(dtype).reshape(-1) for a in arrays]
    total = sum(f.shape[0] for f in flat)
    padded = -(-total // (LANES * _PACK_ROWS)) * (LANES * _PACK_ROWS)
    if padded > total:
        flat.append(jnp.zeros((padded - total,), dtype))
    return jnp.concatenate(flat).reshape(-1, LANES)


def _unpack(packed, shapes, lead=()):
    flat = packed.reshape(*lead, -1)
    out, off = [], 0
    for s in shapes:
        size = math.prod(s)
        out.append(flat[..., off:off + size].reshape(*lead, *s))
        off += size
    return out


def _gather_columns(g):
    return jnp.moveaxis(g, 0, 1).reshape(g.shape[1], -1)


def _split_columns(w):
    return jnp.moveaxis(w.reshape(w.shape[0], N_DEV, -1), 1, 0)


def _split_rows(w):
    return w.reshape(N_DEV, w.shape[0] // N_DEV, w.shape[1])


_BIG = ("w_in", "w_out", "w_ff1", "w_ff2", "w_ple_gate", "w_ple_proj")
_COLUMN_SHARDED = ("w_in", "w_ff1", "w_ple_proj")
_SMALL = ("norm1_g", "q_norm_g", "k_norm_g", "sgu_norm_g", "sgu_w", "sgu_b", "norm2_g", "norm3_g")
_ORDER = ("norm1_g", "w_in", "conv_w", "q_norm_g", "k_norm_g", "sgu_norm_g", "sgu_w", "sgu_b", "w_out", "norm2_g",
          "w_ff1", "w_ff2", "norm3_g", "w_ple_gate", "w_ple_proj")


def _bn_for(n):
    bn = 512
    while n % bn:
        bn //= 2
    return bn


def _layer_forward(h0, p16, w, s, li):
    nm = lambda k: f"{k}_l{li}"
    t = h0.shape[0]
    hn1 = _rms_fwd(h0, s["norm1_g"], name=nm("rms1"))
    proj = _matmul(hn1, w["w_in"], name=nm("proj"), bm=t, bn=256)
    y_a = _conv_fwd(proj, s["conv_w"], name=nm("conv"))
    qs, kn, v = _qk_prep(proj, s["gq"], s["gk"], name=nm("qkprep"))
    y_b, lt = _attn_fwd(qs, kn, v, name=nm("attn"))
    y_c = _sgu_fwd(proj, s["sgu_norm_g"], s["sgu_w"], s["b_exp"], name=nm("sgu"))
    mix = jnp.concatenate([y_a, y_b, y_c], axis=1)
    h1 = _matmul(mix, w["w_out"], name=nm("out"), bm=t, bn=256, extras=(h0,), epilogue=lambda acc, r: (r + acc,))
    hn2 = _rms_fwd(h1, s["norm2_g"], name=nm("rms2"))
    u, f = _matmul(hn2, w["w_ff1"], name=nm("ff1"), bm=t, bn=512, out_dtypes=(F32, BF16),
                   epilogue=lambda acc: (acc, jnp.square(jnp.maximum(acc, 0.0))))
    h2 = _matmul(f, w["w_ff2"], name=nm("ff2"), bm=512, bn=512, extras=(h1,), epilogue=lambda acc, r: (r + acc,))
    hn3 = _rms_fwd(h2, s["norm3_g"], name=nm("rms3"))
    pp = _matmul(p16, w["w_ple_proj"], name=nm("pleproj"), bm=t, bn=512)

    def gate_epilogue(acc, pp_blk, h_blk):
        gate = jax.nn.sigmoid(acc)
        return h_blk + gate * pp_blk, gate

    h3, gate = _matmul(hn3, w["w_ple_gate"], name=nm("plegate"), bm=t, bn=256, out_dtypes=(F32, F32),
                       extras=(pp, h2), epilogue=gate_epilogue)
    saved = dict(h0=h0, hn1=hn1, proj=proj, qs=qs, kn=kn, v=v, lt=lt, mix=mix, h1=h1, hn2=hn2, u=u, f=f, h2=h2,
                 hn3=hn3, pp=pp, gate=gate, p16=p16)
    return h3, saved


def _layer_backward(dh3, a, w, s, li, order_after):
    nm = lambda k: f"{k}_bwd_l{li}"
    t = dh3.shape[0]
    dpre, dpp = _ple_bwd(dh3, a["gate"], a["pp"], order_after, name=nm("ple"))
    g_gate = _matmul(a["hn3"], dpre, name=nm("dwgate"), ta=True, bm=512, bn=512, out_dtypes=(BF16,))
    g_proj = _matmul(a["p16"], dpp, name=nm("dwproj"), ta=True, bm=256, out_dtypes=(BF16,), column_shards=True)
    dhn3 = _matmul(dpre, w["w_ple_gate"], name=nm("dhn3"), tb=True, bm=t, bn=256)
    dh2, dh2_16, g_n3 = _rms_bwd(dhn3, a["h2"], s["norm3_g"], dh3, name=nm("rms3"))
    du = _matmul(dh2_16, w["w_ff2"], name=nm("du"), tb=True, bm=t, bn=512, out_dtypes=(BF16,), extras=(a["u"],),
                 epilogue=lambda acc, u: (acc * (2.0 * jnp.maximum(u, 0.0)),))
    g_ff2 = _matmul(a["f"], dh2_16, name=nm("dwff2"), ta=True, bm=512, bn=512, out_dtypes=(BF16,))
    g_ff1 = _matmul(a["hn2"], du, name=nm("dwff1"), ta=True, bm=512, out_dtypes=(BF16,), column_shards=True)
    dhn2 = _matmul(du, w["w_ff1"], name=nm("dhn2"), tb=True, bm=512, bn=512)
    dh1, dh1_16, g_n2 = _rms_bwd(dhn2, a["h1"], s["norm2_g"], dh2, name=nm("rms2"))
    dmix = _matmul(dh1_16, w["w_out"], name=nm("dmix"), tb=True, bm=t, bn=256)
    g_out = _matmul(a["mix"], dh1_16, name=nm("dwout"), ta=True, bm=512, bn=512, out_dtypes=(BF16,))
    d_b, d_c, d_h, g_conv = _conv_bwd(dmix, a["proj"], s["conv_w"], name=nm("conv"))
    dqs, dkn, dv = _attn_bwd(dmix, a["qs"], a["kn"], a["v"], a["lt"], name=nm("attn"))
    d_q, d_k, d_v, g_q, g_k = _qk_prep_bwd(dqs, dkn, dv, a["proj"], s["gq"], s["gk"], name=nm("qkprep"))
    d_cu, d_cv, g_sn, g_sw, g_sb = _sgu_bwd(dmix, a["proj"], s["sgu_norm_g"], s["sgu_w"], s["b_exp"], name=nm("sgu"))
    dproj = jnp.concatenate([d_b, d_c, d_h, d_q, d_k, d_v, d_cu, d_cv], axis=1)
    g_in = _matmul(a["hn1"], dproj, name=nm("dwin"), ta=True, bm=512, bn=256, out_dtypes=(BF16,))
    dhn1 = _matmul(dproj, w["w_in"], name=nm("dhn1"), tb=True, bm=t, bn=256)
    dh0, _, g_n1 = _rms_bwd(dhn1, a["h0"], s["norm1_g"], dh1, name=nm("rms1"))
    big = dict(w_in=_split_columns(g_in), w_out=_split_rows(g_out), w_ff1=g_ff1, w_ff2=_split_rows(g_ff2),
               w_ple_gate=_split_rows(g_gate), w_ple_proj=g_proj)
    n_tiles = g_q.shape[1] // HEAD_DIM
    small = dict(
        norm1_g=g_n1[0], norm2_g=g_n2[0], norm3_g=g_n3[0],
        q_norm_g=g_q.reshape(n_tiles, HEAD_DIM).sum(0), k_norm_g=g_k.reshape(n_tiles, HEAD_DIM).sum(0),
        sgu_norm_g=g_sn[0], sgu_w=g_sw, sgu_b=g_sb.reshape(CHUNK, SGU_HEADS, HEAD_DIM).sum(-1).T,
        conv_w=g_conv[:CONV_TAPS],
    )
    return dh0, big, small


def kernel(x, p, norm1_g, w_in, conv_w, q_norm_g, k_norm_g, sgu_norm_g, sgu_w, sgu_b, w_out, norm2_g, w_ff1, w_ff2, norm3_g, w_ple_gate, w_ple_proj, loss_target, m_norm1_g, m_w_in, m_conv_w, m_q_norm_g, m_k_norm_g, m_sgu_norm_g, m_sgu_w, m_sgu_b, m_w_out, m_norm2_g, m_w_ff1, m_w_ff2, m_norm3_g, m_w_ple_gate, m_w_ple_proj, v_norm1_g, v_w_in, v_conv_w, v_q_norm_g, v_k_norm_g, v_sgu_norm_g, v_sgu_w, v_sgu_b, v_w_out, v_norm2_g, v_w_ff1, v_w_ff2, v_norm3_g, v_w_ple_gate, v_w_ple_proj):
    weights = dict(norm1_g=norm1_g, w_in=w_in, conv_w=conv_w, q_norm_g=q_norm_g, k_norm_g=k_norm_g,
                   sgu_norm_g=sgu_norm_g, sgu_w=sgu_w, sgu_b=sgu_b, w_out=w_out, norm2_g=norm2_g, w_ff1=w_ff1,
                   w_ff2=w_ff2, norm3_g=norm3_g, w_ple_gate=w_ple_gate, w_ple_proj=w_ple_proj)
    mom = dict(norm1_g=m_norm1_g, w_in=m_w_in, conv_w=m_conv_w, q_norm_g=m_q_norm_g, k_norm_g=m_k_norm_g,
               sgu_norm_g=m_sgu_norm_g, sgu_w=m_sgu_w, sgu_b=m_sgu_b, w_out=m_w_out, norm2_g=m_norm2_g, w_ff1=m_w_ff1,
               w_ff2=m_w_ff2, norm3_g=m_norm3_g, w_ple_gate=m_w_ple_gate, w_ple_proj=m_w_ple_proj)
    var = dict(norm1_g=v_norm1_g, w_in=v_w_in, conv_w=v_conv_w, q_norm_g=v_q_norm_g, k_norm_g=v_k_norm_g,
               sgu_norm_g=v_sgu_norm_g, sgu_w=v_sgu_w, sgu_b=v_sgu_b, w_out=v_w_out, norm2_g=v_norm2_g, w_ff1=v_w_ff1,
               w_ff2=v_w_ff2, norm3_g=v_norm3_g, w_ple_gate=v_w_ple_gate, w_ple_proj=v_w_ple_proj)
    depth = norm1_g.shape[0]
    mx, my, mc = _my_place()
    me = _slot_of(mx, my, mc)

    gathers, tokens = [], []
    for li in range(depth):
        started, token = _exchange_start([weights[k][li].astype(BF16) for k in _BIG], scatter=False,
                                         name=f"gather_weights_start_l{li}")
        gathers.append(started)
        tokens.append(token)
    conv_all = _all_gather(_pack([conv_w], F32), name="gather_conv")
    conv_full = _gather_columns(_unpack(conv_all, [(depth * CONV_TAPS, conv_w.shape[2])], lead=(N_DEV,))[0])
    conv_full = conv_full.reshape(depth, CONV_TAPS, -1)

    small = []
    for li in range(depth):
        small.append(dict(
            norm1_g=norm1_g[li][None], norm2_g=norm2_g[li][None], norm3_g=norm3_g[li][None],
            conv_w=conv_full[li],
            gq=jnp.tile(q_norm_g[li], _QK_BLOCK // HEAD_DIM)[None], gk=jnp.tile(k_norm_g[li], _QK_BLOCK // HEAD_DIM)[None],
            sgu_norm_g=sgu_norm_g[li][None], sgu_w=sgu_w[li], b_exp=jnp.repeat(sgu_b[li].T, HEAD_DIM, axis=1),
        ))
    small[0]["norm1_g"] = small[0]["norm1_g"] + sum(tk[0, 0] for tk in tokens)

    h = x[0]
    saved, full = [], []
    for li in range(depth):
        shards, landed = _exchange_wait(gathers[li], tokens[0] if li == 0 else h, scatter=False,
                                        name=f"gather_weights_wait_l{li}")
        parts = {k: _with_own_slot(g, own, me) for k, g, own in zip(_BIG, landed, shards)}
        full.append({k: _gather_columns(g) if k in _COLUMN_SHARDED else g.reshape(-1, g.shape[-1])
                     for k, g in parts.items()})
        h, acts = _layer_forward(h, p[li, 0].astype(BF16), full[li], small[li], li)
        saved.append(acts)
    dh, loss_tile = _loss_head(h, loss_target[0], name="loss_head")
    loss = lax.psum(loss_tile[0, 0], ("x", "y", "c"))

    small_grads = [None] * depth
    scatters = [None] * depth
    order_after = loss_tile
    for li in reversed(range(depth)):
        dh, big, small_grads[li] = _layer_backward(dh, saved[li], full[li], small[li], li, order_after)
        scatters[li], order_after = _exchange_start([big[k] for k in _BIG], scatter=True, name=f"scatter_grads_start_l{li}")
    grad_x = dh[None]

    small_names = _SMALL + ("conv_w",)
    small_shapes = [(depth,) + small_grads[0][k].shape for k in small_names]
    partial = _pack([jnp.stack([small_grads[li][k] for li in range(depth)]) for k in small_names], F32)
    total = _sum_slots(_all_gather(partial, name="gather_small_grads"), name="sum_small_grads")
    grads = dict(zip(small_names, _unpack(total, small_shapes)))
    n_conv = conv_w.shape[2]
    grads["conv_w"] = lax.dynamic_slice_in_dim(grads["conv_w"], me * n_conv, n_conv, axis=2)

    arrived = {k: [None] * depth for k in _BIG}
    for li in reversed(range(depth)):
        sent, landed = _exchange_wait(scatters[li], total, scatter=True, name=f"scatter_grads_wait_l{li}")
        for k, g, src in zip(_BIG, landed, sent):
            arrived[k][li] = _with_own_slot(g, lax.dynamic_index_in_dim(src, me, 0, keepdims=False), me)
    delta, new_m, new_v = {}, {}, {}
    for k in _BIG:
        grads[k], delta[k], new_m[k], new_v[k] = _adamw_reduce(weights[k], arrived[k], mom[k], var[k], name=f"adamw_{k}")
    rest = _SMALL + ("conv_w",)
    rest_shapes = [weights[k].shape for k in rest]
    packs = [_pack([src[k] for k in rest], F32) for src in (weights, grads, mom, var)]
    for out, packed in zip((delta, new_m, new_v), _adamw(*packs, name="adamw_small")):
        out.update(zip(rest, _unpack(packed, rest_shapes)))

    return (loss, grad_x, *[grads[k] for k in _ORDER], *[delta[k] for k in _ORDER],
            *[new_m[k] for k in _ORDER], *[new_v[k] for k in _ORDER])
```

```python
import math

import jax
import jax.numpy as jnp
from jax import lax
from jax.experimental import pallas as pl
from jax.experimental.pallas import tpu as pltpu

F32 = jnp.float32
BF16 = jnp.bfloat16

N_DEV = 8
HEAD_DIM = 64
CONV_W = 256
ATTN_W = 512
SGU_W = 256
SGU_HEADS = 4
CHUNK = 128
CONV_TAPS = 3
EPS = 1e-6
QK_SCALE = HEAD_DIM ** -0.5

ADAM_LR = 0.001
ADAM_B1 = 0.9
ADAM_B2 = 0.999
ADAM_EPS = 1e-08
ADAM_WD = 0.01
ADAM_STEP = 10

LANES = 128
BF16_TILE_ROWS = 16
VMEM_LIMIT_BYTES = 56 * 1024 * 1024
MESH = pl.DeviceIdType.MESH


def _params(*sem):
    return pltpu.CompilerParams(dimension_semantics=sem, vmem_limit_bytes=VMEM_LIMIT_BYTES)


def _row_block(rows, cap):
    if rows <= cap:
        return rows
    return max(b for b in range(BF16_TILE_ROWS, cap + 1, BF16_TILE_ROWS) if rows % b == 0)


def _matmul(a, b, *, name, ta=False, tb=False, bm=512, bn=512, out_dtypes=(F32,), epilogue=None, extras=(),
            column_shards=False):
    m = a.shape[1] if ta else a.shape[0]
    k = a.shape[0] if ta else a.shape[1]
    n = b.shape[0] if tb else b.shape[1]
    assert k == (b.shape[1] if tb else b.shape[0])
    bm, bn = min(bm, m), min(bn, n)
    if column_shards:
        bn = n // N_DEV
    assert m % bm == 0 and n % bn == 0
    a_spec = pl.BlockSpec((k, bm), lambda i, j: (0, i)) if ta else pl.BlockSpec((bm, k), lambda i, j: (i, 0))
    b_spec = pl.BlockSpec((bn, k), lambda i, j: (j, 0)) if tb else pl.BlockSpec((k, bn), lambda i, j: (0, j))
    dims = (((0 if ta else 1,), (1 if tb else 0,)), ((), ()))
    n_ex = len(extras)
    for e in extras:
        assert e.shape == (m, n), (e.shape, m, n)

    def body(a_ref, b_ref, *rest):
        outs = rest[n_ex:]
        acc = lax.dot_general(a_ref[...], b_ref[...], dims, preferred_element_type=F32)
        res = (acc,) if epilogue is None else epilogue(acc, *[e[...] for e in rest[:n_ex]])
        for o_ref, r in zip(outs, res, strict=True):
            o_ref[...] = r.astype(o_ref.dtype)

    tile = pl.BlockSpec((bm, bn), lambda i, j: (i, j))
    out_tile, out_dims = tile, (m, n)
    if column_shards:
        out_tile, out_dims = pl.BlockSpec((None, bm, bn), lambda i, j: (j, i, 0)), (N_DEV, m, bn)
    out = pl.pallas_call(
        body,
        name=name,
        grid=(m // bm, n // bn),
        in_specs=[a_spec, b_spec] + [tile] * n_ex,
        out_specs=[out_tile] * len(out_dtypes),
        out_shape=[jax.ShapeDtypeStruct(out_dims, d) for d in out_dtypes],
        compiler_params=_params("parallel", "parallel"),
    )(a, b, *extras)
    return out[0] if len(out_dtypes) == 1 else out


def _rms_fwd(h, g, *, name, br=512):
    t, d = h.shape
    br = min(br, t)

    def body(h_ref, g_ref, o_ref):
        x = h_ref[...]
        r = lax.rsqrt(jnp.mean(x * x, axis=-1, keepdims=True) + EPS)
        o_ref[...] = (x * r * g_ref[...]).astype(o_ref.dtype)

    return pl.pallas_call(
        body,
        name=name,
        grid=(t // br,),
        in_specs=[pl.BlockSpec((br, d), lambda i: (i, 0)), pl.BlockSpec((1, d), lambda i: (0, 0))],
        out_specs=pl.BlockSpec((br, d), lambda i: (i, 0)),
        out_shape=jax.ShapeDtypeStruct((t, d), BF16),
        compiler_params=_params("parallel"),
    )(h, g)


def _rms_bwd(dy, h, g, dres, *, name, br=512):
    t, d = h.shape
    br = min(br, t)

    def body(dy_ref, h_ref, g_ref, dres_ref, dh_ref, dh16_ref, dg_ref):
        x = h_ref[...]
        dyv = dy_ref[...]
        r = lax.rsqrt(jnp.mean(x * x, axis=-1, keepdims=True) + EPS)
        xhat = x * r
        dxhat = dyv * g_ref[...]
        dh = dres_ref[...] + r * (dxhat - xhat * jnp.mean(dxhat * xhat, axis=-1, keepdims=True))
        dh_ref[...] = dh
        dh16_ref[...] = dh.astype(dh16_ref.dtype)

        @pl.when(pl.program_id(0) == 0)
        def _():
            dg_ref[...] = jnp.zeros_like(dg_ref)

        dg_ref[...] += jnp.sum(dyv * xhat, axis=0, keepdims=True)

    row = pl.BlockSpec((br, d), lambda i: (i, 0))
    vec = pl.BlockSpec((1, d), lambda i: (0, 0))
    return pl.pallas_call(
        body,
        name=name,
        grid=(t // br,),
        in_specs=[row, row, vec, row],
        out_specs=[row, row, vec],
        out_shape=[jax.ShapeDtypeStruct((t, d), F32), jax.ShapeDtypeStruct((t, d), BF16),
                   jax.ShapeDtypeStruct((1, d), F32)],
        compiler_params=_params("arbitrary"),
    )(dy, h, g, dres)


def _group_mean(x, width):
    grp = lax.broadcasted_iota(jnp.int32, x.shape, 1) // HEAD_DIM
    out = jnp.zeros_like(x)
    for gi in range(width // HEAD_DIM):
        m = grp == gi
        s = jnp.sum(jnp.where(m, x, 0.0), axis=1, keepdims=True)
        out = jnp.where(m, s, out)
    return out * (1.0 / HEAD_DIM)


def _gelu(x):
    return 0.5 * x * (1.0 + lax.erf(x * (2.0 ** -0.5)))


def _gelu_grad(x):
    cdf = 0.5 * (1.0 + lax.erf(x * (2.0 ** -0.5)))
    pdf = jnp.exp(-0.5 * x * x) * (1.0 / math.sqrt(2.0 * math.pi))
    return cdf + x * pdf


def _shift_down(z, s, row):
    return jnp.where(row >= s, pltpu.roll(z, s, 0), 0.0)


def _shift_up(z, s, row, t):
    return jnp.where(row < t - s, pltpu.roll(z, t - s, 0), 0.0)


def _conv_fwd(proj, conv_w, *, name):
    t = proj.shape[0]
    nb = CONV_W // LANES

    def body(b_ref, c_ref, h_ref, w_ref, o_ref):
        row = lax.broadcasted_iota(jnp.int32, (t, LANES), 0)
        z = c_ref[...] * h_ref[...]
        w = w_ref[...]
        conv = w[2:3, :] * z + w[1:2, :] * _shift_down(z, 1, row) + w[0:1, :] * _shift_down(z, 2, row)
        o_ref[...] = (b_ref[...] * conv).astype(o_ref.dtype)

    return pl.pallas_call(
        body,
        name=name,
        grid=(nb,),
        in_specs=[
            pl.BlockSpec((t, LANES), lambda j: (0, j)),
            pl.BlockSpec((t, LANES), lambda j: (0, nb + j)),
            pl.BlockSpec((t, LANES), lambda j: (0, 2 * nb + j)),
            pl.BlockSpec((CONV_TAPS, LANES), lambda j: (0, j)),
        ],
        out_specs=pl.BlockSpec((t, LANES), lambda j: (0, j)),
        out_shape=jax.ShapeDtypeStruct((t, CONV_W), BF16),
        compiler_params=_params("parallel"),
    )(proj, proj, proj, conv_w)


def _conv_bwd(dmix, proj, conv_w, *, name):
    t = proj.shape[0]
    nb = CONV_W // LANES

    def body(dy_ref, b_ref, c_ref, h_ref, w_ref, db_ref, dc_ref, dh_ref, dw_ref):
        row = lax.broadcasted_iota(jnp.int32, (t, LANES), 0)
        ac, ah = c_ref[...], h_ref[...]
        z = ac * ah
        w = w_ref[...]
        z1 = _shift_down(z, 1, row)
        z2 = _shift_down(z, 2, row)
        conv = w[2:3, :] * z + w[1:2, :] * z1 + w[0:1, :] * z2
        dy = dy_ref[...]
        db_ref[...] = (dy * conv).astype(db_ref.dtype)
        dconv = dy * b_ref[...]
        dz = w[2:3, :] * dconv + w[1:2, :] * _shift_up(dconv, 1, row, t) + w[0:1, :] * _shift_up(dconv, 2, row, t)
        dc_ref[...] = (dz * ah).astype(dc_ref.dtype)
        dh_ref[...] = (dz * ac).astype(dh_ref.dtype)
        dw_ref[...] = jnp.zeros_like(dw_ref)
        dw_ref[0:1, :] = jnp.sum(dconv * z2, axis=0, keepdims=True)
        dw_ref[1:2, :] = jnp.sum(dconv * z1, axis=0, keepdims=True)
        dw_ref[2:3, :] = jnp.sum(dconv * z, axis=0, keepdims=True)

    col = lambda off: pl.BlockSpec((t, LANES), lambda j: (0, off + j))
    return pl.pallas_call(
        body,
        name=name,
        grid=(nb,),
        in_specs=[col(0), col(0), col(nb), col(2 * nb), pl.BlockSpec((CONV_TAPS, LANES), lambda j: (0, j))],
        out_specs=[col(0), col(0), col(0), pl.BlockSpec((8, LANES), lambda j: (0, j))],
        out_shape=[jax.ShapeDtypeStruct((t, CONV_W), BF16)] * 3 + [jax.ShapeDtypeStruct((8, CONV_W), F32)],
        compiler_params=_params("parallel"),
    )(dmix, proj, proj, proj, conv_w)


_QK_BLOCK = 256


def _qk_prep(proj, gq, gk, *, name, br=512):
    t = proj.shape[0]
    br = min(br, t)
    nb = ATTN_W // _QK_BLOCK
    q0 = (3 * CONV_W) // _QK_BLOCK

    def body(q_ref, k_ref, v_ref, gq_ref, gk_ref, qo_ref, ko_ref, vo_ref):
        q = q_ref[...]
        k = k_ref[...]
        rq = lax.rsqrt(_group_mean(q * q, _QK_BLOCK) + EPS)
        rk = lax.rsqrt(_group_mean(k * k, _QK_BLOCK) + EPS)
        qo_ref[...] = ((q * rq * gq_ref[...]).astype(BF16) * QK_SCALE).astype(qo_ref.dtype)
        ko_ref[...] = (k * rk * gk_ref[...]).astype(ko_ref.dtype)
        vo_ref[...] = v_ref[...].astype(vo_ref.dtype)

    col = lambda off: pl.BlockSpec((br, _QK_BLOCK), lambda i, j: (i, off + j))
    vec = pl.BlockSpec((1, _QK_BLOCK), lambda i, j: (0, 0))
    return pl.pallas_call(
        body,
        name=name,
        grid=(t // br, nb),
        in_specs=[col(q0), col(q0 + nb), col(q0 + 2 * nb), vec, vec],
        out_specs=[col(0)] * 3,
        out_shape=[jax.ShapeDtypeStruct((t, ATTN_W), BF16)] * 3,
        compiler_params=_params("parallel", "parallel"),
    )(proj, proj, proj, gq, gk)


def _qk_prep_bwd(dqs, dkn, dv, proj, gq, gk, *, name, br=512):
    t = proj.shape[0]
    br = min(br, t)
    nb = ATTN_W // _QK_BLOCK
    q0 = (3 * CONV_W) // _QK_BLOCK

    def norm_bwd(dy, x, g):
        r = lax.rsqrt(_group_mean(x * x, _QK_BLOCK) + EPS)
        xhat = x * r
        dxhat = dy * g
        dx = r * (dxhat - xhat * _group_mean(dxhat * xhat, _QK_BLOCK))
        return dx, jnp.sum(dy * xhat, axis=0, keepdims=True)

    def body(dq_ref, dk_ref, dv_ref, q_ref, k_ref, gq_ref, gk_ref, oq_ref, ok_ref, ov_ref, dgq_ref, dgk_ref):
        dq, dgq = norm_bwd(dq_ref[...] * QK_SCALE, q_ref[...], gq_ref[...])
        dk, dgk = norm_bwd(dk_ref[...], k_ref[...], gk_ref[...])
        oq_ref[...] = dq.astype(oq_ref.dtype)
        ok_ref[...] = dk.astype(ok_ref.dtype)
        ov_ref[...] = dv_ref[...].astype(ov_ref.dtype)

        @pl.when((pl.program_id(0) == 0) & (pl.program_id(1) == 0))
        def _():
            dgq_ref[...] = jnp.zeros_like(dgq_ref)
            dgk_ref[...] = jnp.zeros_like(dgk_ref)

        dgq_ref[...] += dgq
        dgk_ref[...] += dgk

    col = lambda off: pl.BlockSpec((br, _QK_BLOCK), lambda i, j: (i, off + j))
    vec = pl.BlockSpec((1, _QK_BLOCK), lambda i, j: (0, 0))
    return pl.pallas_call(
        body,
        name=name,
        grid=(t // br, nb),
        in_specs=[col(0), col(0), col(0), col(q0), col(q0 + nb), vec, vec],
        out_specs=[col(0), col(0), col(0), vec, vec],
        out_shape=[jax.ShapeDtypeStruct((t, ATTN_W), BF16)] * 3 + [jax.ShapeDtypeStruct((1, _QK_BLOCK), F32)] * 2,
        compiler_params=_params("arbitrary", "arbitrary"),
    )(dqs, dkn, dv, proj, proj, gq, gk)


def _split_bf16(x):
    hi = x.astype(BF16)
    lo = (x - hi.astype(F32)).astype(BF16)
    return hi, lo


def _dot_exact01(x, m01):
    hi, lo = _split_bf16(x)
    return jnp.dot(hi, m01, preferred_element_type=F32) + jnp.dot(lo, m01, preferred_element_type=F32)


def _log_sigmoids(z):
    sp = jnp.log(1.0 + jnp.exp(-jnp.abs(z)))
    return jnp.minimum(z, 0.0) - sp, jnp.minimum(-z, 0.0) - sp


_NT = (((1,), (1,)), ((), ()))
_TN = (((0,), (0,)), ((), ()))


def _attn_fwd(qs, kn, v, *, name, tq=256, tk=256):
    t = qs.shape[0]
    tq, tk = min(tq, t), min(tk, t)
    assert tq % tk == 0 and t % tq == 0
    n_pairs = ATTN_W // LANES

    def body(q_ref, k_ref, v_ref, o_ref, lt_ref, acc_ref, carry_ref):
        qb = pl.program_id(1)
        half = lax.broadcasted_iota(jnp.int32, (1, LANES), 1) // HEAD_DIM
        q = q_ref[...]
        qh = [jnp.where(half == h, q, jnp.zeros_like(q)) for h in range(2)]
        row = qb * tq + lax.broadcasted_iota(jnp.int32, (tq, tk), 0)
        col = lax.broadcasted_iota(jnp.int32, (tq, tk), 1)
        jj = lax.broadcasted_iota(jnp.int32, (tk, tk), 0)
        ss = lax.broadcasted_iota(jnp.int32, (tk, tk), 1)
        later = (jj > ss).astype(BF16)
        acc_ref[...] = jnp.zeros_like(acc_ref)
        carry_ref[...] = jnp.zeros_like(carry_ref)
        n_kb = (qb + 1) * (tq // tk)

        def step(i, _):
            kb = n_kb - 1 - i
            start = pl.multiple_of(kb * tk, tk)
            kblk = k_ref[pl.ds(start, tk), :]
            vblk = v_ref[pl.ds(start, tk), :]
            causal = (start + col) < row
            for h in range(2):
                z = lax.dot_general(qh[h], kblk, _NT, preferred_element_type=F32)
                lb, lr = _log_sigmoids(z)
                lr = jnp.where(causal, lr, 0.0)
                carry = carry_ref[h]
                suffix = _dot_exact01(lr, later) + carry[:, 0:1]
                w = jnp.where(causal, jnp.exp(lb + suffix), 0.0)
                vh = jnp.where(half == h, vblk, jnp.zeros_like(vblk))
                acc_ref[...] += jnp.dot(w.astype(BF16), vh, preferred_element_type=F32)
                carry_ref[h] = carry + jnp.sum(lr, axis=1, keepdims=True)
            return 0

        lax.fori_loop(0, n_kb, step, 0)
        o_ref[...] = acc_ref[...].astype(o_ref.dtype)
        lt_ref[...] = jnp.where(half == 0, carry_ref[0], carry_ref[1])

    return pl.pallas_call(
        body,
        name=name,
        grid=(n_pairs, t // tq),
        in_specs=[
            pl.BlockSpec((tq, LANES), lambda p, i: (i, p)),
            pl.BlockSpec((t, LANES), lambda p, i: (0, p)),
            pl.BlockSpec((t, LANES), lambda p, i: (0, p)),
        ],
        out_specs=[pl.BlockSpec((tq, LANES), lambda p, i: (i, p))] * 2,
        out_shape=[jax.ShapeDtypeStruct((t, ATTN_W), BF16), jax.ShapeDtypeStruct((t, ATTN_W), F32)],
        scratch_shapes=[pltpu.VMEM((tq, LANES), F32), pltpu.VMEM((2, tq, LANES), F32)],
        compiler_params=_params("parallel", "parallel"),
    )(qs, kn, v)


def _attn_bwd(dmix, qs, kn, v, lt, order_after, *, name, tq=256, tk=256):
    t = qs.shape[0]
    tq, tk = min(tq, t), min(tk, t)
    assert tq % tk == 0 and t % tq == 0
    n_pairs = ATTN_W // LANES
    dy0 = CONV_W // LANES

    def body(do_ref, q_ref, k_ref, v_ref, lt_ref, order_ref, dq_ref, dk_ref, dv_ref, cc_ref, cg_ref):
        qb = pl.program_id(1)
        half = lax.broadcasted_iota(jnp.int32, (1, LANES), 1) // HEAD_DIM
        q = q_ref[...]
        do = do_ref[...].astype(BF16)
        lt = lt_ref[...]
        row = qb * tq + lax.broadcasted_iota(jnp.int32, (tq, tk), 0)
        col = lax.broadcasted_iota(jnp.int32, (tq, tk), 1)
        jj = lax.broadcasted_iota(jnp.int32, (tk, tk), 0)
        ss = lax.broadcasted_iota(jnp.int32, (tk, tk), 1)
        upto = (jj <= ss).astype(BF16)
        before = (jj < ss).astype(BF16)
        qh, doh, lth = [], [], []
        for h in range(2):
            qh.append(jnp.where(half == h, q, jnp.zeros_like(q)))
            doh.append(jnp.where(half == h, do, jnp.zeros_like(do)))
            lth.append(jnp.sum(jnp.where(lax.broadcasted_iota(jnp.int32, (tq, LANES), 1) == h * HEAD_DIM, lt, 0.0),
                               axis=1, keepdims=True))

        @pl.when(qb == 0)
        def _():
            dk_ref[...] = jnp.zeros_like(dk_ref)
            dv_ref[...] = jnp.zeros_like(dv_ref)

        dq_ref[...] = jnp.zeros_like(dq_ref)
        cc_ref[...] = jnp.zeros_like(cc_ref)
        cg_ref[...] = jnp.zeros_like(cg_ref)
        n_kb = (qb + 1) * (tq // tk)

        def step(kb, _):
            start = pl.multiple_of(kb * tk, tk)
            kblk = k_ref[pl.ds(start, tk), :]
            vblk = v_ref[pl.ds(start, tk), :]
            causal = (start + col) < row
            for h in range(2):
                z = lax.dot_general(qh[h], kblk, _NT, preferred_element_type=F32)
                lb, lr = _log_sigmoids(z)
                lr = jnp.where(causal, lr, 0.0)
                cc = cc_ref[h]
                prefix = _dot_exact01(lr, upto) + cc[:, 0:1]
                a = jnp.where(causal, jnp.exp(lb + (lth[h] - prefix)), 0.0)
                vh = jnp.where(half == h, vblk, jnp.zeros_like(vblk))
                da = lax.dot_general(doh[h], vh, _NT, preferred_element_type=F32)
                g = da * a
                cg = cg_ref[h]
                p = _dot_exact01(g, before) + cg[:, 0:1]
                beta = jnp.exp(lb)
                dz = jnp.where(causal, g * (1.0 - beta) - p * beta, 0.0).astype(BF16)
                kh = jnp.where(half == h, kblk, jnp.zeros_like(kblk))
                dq_ref[...] += jnp.dot(dz, kh, preferred_element_type=F32)
                dk_ref[pl.ds(start, tk), :] += lax.dot_general(dz, qh[h], _TN, preferred_element_type=F32)
                dv_ref[pl.ds(start, tk), :] += lax.dot_general(a.astype(BF16), doh[h], _TN, preferred_element_type=F32)
                cc_ref[h] = cc + jnp.sum(lr, axis=1, keepdims=True)
                cg_ref[h] = cg + jnp.sum(g, axis=1, keepdims=True)
            return 0

        lax.fori_loop(0, n_kb, step, 0)

    qblk = pl.BlockSpec((tq, LANES), lambda p, i: (i, p))
    whole = pl.BlockSpec((t, LANES), lambda p, i: (0, p))
    return pl.pallas_call(
        body,
        name=name,
        grid=(n_pairs, t // tq),
        in_specs=[pl.BlockSpec((tq, LANES), lambda p, i: (i, dy0 + p)), qblk, whole, whole, qblk,
                  pl.BlockSpec(order_after.shape, lambda p, i: (0, 0))],
        out_specs=[qblk, whole, whole],
        out_shape=[jax.ShapeDtypeStruct((t, ATTN_W), F32)] * 3,
        scratch_shapes=[pltpu.VMEM((2, tq, LANES), F32), pltpu.VMEM((2, tq, LANES), F32)],
        compiler_params=_params("parallel", "arbitrary"),
    )(dmix, qs, kn, v, lt, order_after)


def _sgu_weights(w_ref):
    tt = lax.broadcasted_iota(jnp.int32, (CHUNK, CHUNK), 0)
    ss = lax.broadcasted_iota(jnp.int32, (CHUNK, CHUNK), 1)
    tril = ss <= tt
    return [jnp.where(tril, w_ref[gi], 0.0).astype(BF16) for gi in range(SGU_HEADS)], tril


def _sgu_fwd(proj, g_v, w_s, b_exp, *, name):
    t = proj.shape[0]
    u0 = (3 * CONV_W + 3 * ATTN_W) // SGU_W

    def body(u_ref, v_ref, g_ref, w_ref, b_ref, o_ref):
        grp = lax.broadcasted_iota(jnp.int32, (1, SGU_W), 1) // HEAD_DIM
        u = _gelu(u_ref[...])
        vv = _gelu(v_ref[...])
        vn = (vv * lax.rsqrt(_group_mean(vv * vv, SGU_W) + EPS) * g_ref[...]).astype(BF16)
        wm, _ = _sgu_weights(w_ref)
        sv = b_ref[...]
        for gi in range(SGU_HEADS):
            sv = sv + jnp.dot(wm[gi], jnp.where(grp == gi, vn, jnp.zeros_like(vn)), preferred_element_type=F32)
        o_ref[...] = (u * sv).astype(o_ref.dtype)

    return pl.pallas_call(
        body,
        name=name,
        grid=(t // CHUNK,),
        in_specs=[
            pl.BlockSpec((CHUNK, SGU_W), lambda i: (i, u0)),
            pl.BlockSpec((CHUNK, SGU_W), lambda i: (i, u0 + 1)),
            pl.BlockSpec((1, SGU_W), lambda i: (0, 0)),
            pl.BlockSpec((SGU_HEADS, CHUNK, CHUNK), lambda i: (0, 0, 0)),
            pl.BlockSpec((CHUNK, SGU_W), lambda i: (0, 0)),
        ],
        out_specs=pl.BlockSpec((CHUNK, SGU_W), lambda i: (i, 0)),
        out_shape=jax.ShapeDtypeStruct((t, SGU_W), BF16),
        compiler_params=_params("parallel"),
    )(proj, proj, g_v, w_s, b_exp)


def _sgu_bwd(dmix, proj, g_v, w_s, b_exp, *, name):
    t = proj.shape[0]
    u0 = (3 * CONV_W + 3 * ATTN_W) // SGU_W
    dy0 = (CONV_W + ATTN_W) // SGU_W

    def body(dy_ref, u_ref, v_ref, g_ref, w_ref, b_ref, du_ref, dv_ref, dg_ref, dw_ref, db_ref):
        grp = lax.broadcasted_iota(jnp.int32, (1, SGU_W), 1) // HEAD_DIM
        cu, cv = u_ref[...], v_ref[...]
        u = _gelu(cu)
        vv = _gelu(cv)
        r = lax.rsqrt(_group_mean(vv * vv, SGU_W) + EPS)
        xhat = vv * r
        gain = g_ref[...]
        vn = (xhat * gain).astype(BF16)
        wm, tril = _sgu_weights(w_ref)
        vng = [jnp.where(grp == gi, vn, jnp.zeros_like(vn)) for gi in range(SGU_HEADS)]
        sv = b_ref[...]
        for gi in range(SGU_HEADS):
            sv = sv + jnp.dot(wm[gi], vng[gi], preferred_element_type=F32)
        dy = dy_ref[...]
        du_ref[...] = (dy * sv * _gelu_grad(cu)).astype(du_ref.dtype)
        dsv = dy * u
        dsv16 = dsv.astype(BF16)

        @pl.when(pl.program_id(0) == 0)
        def _():
            dg_ref[...] = jnp.zeros_like(dg_ref)
            dw_ref[...] = jnp.zeros_like(dw_ref)
            db_ref[...] = jnp.zeros_like(db_ref)

        db_ref[...] += dsv
        dvn = jnp.zeros_like(dsv)
        for gi in range(SGU_HEADS):
            dw = lax.dot_general(dsv16, vng[gi], _NT, preferred_element_type=F32)
            dw_ref[gi] += jnp.where(tril, dw, 0.0)
            dvn_g = lax.dot_general(wm[gi], dsv16, _TN, preferred_element_type=F32)
            dvn = jnp.where(grp == gi, dvn_g, dvn)
        dg_ref[...] += jnp.sum(dvn * xhat, axis=0, keepdims=True)
        dxhat = dvn * gain
        dvv = r * (dxhat - xhat * _group_mean(dxhat * xhat, SGU_W))
        dv_ref[...] = (dvv * _gelu_grad(cv)).astype(dv_ref.dtype)

    return pl.pallas_call(
        body,
        name=name,
        grid=(t // CHUNK,),
        in_specs=[
            pl.BlockSpec((CHUNK, SGU_W), lambda i: (i, dy0)),
            pl.BlockSpec((CHUNK, SGU_W), lambda i: (i, u0)),
            pl.BlockSpec((CHUNK, SGU_W), lambda i: (i, u0 + 1)),
            pl.BlockSpec((1, SGU_W), lambda i: (0, 0)),
            pl.BlockSpec((SGU_HEADS, CHUNK, CHUNK), lambda i: (0, 0, 0)),
            pl.BlockSpec((CHUNK, SGU_W), lambda i: (0, 0)),
        ],
        out_specs=[
            pl.BlockSpec((CHUNK, SGU_W), lambda i: (i, 0)),
            pl.BlockSpec((CHUNK, SGU_W), lambda i: (i, 0)),
            pl.BlockSpec((1, SGU_W), lambda i: (0, 0)),
            pl.BlockSpec((SGU_HEADS, CHUNK, CHUNK), lambda i: (0, 0, 0)),
            pl.BlockSpec((CHUNK, SGU_W), lambda i: (0, 0)),
        ],
        out_shape=[
            jax.ShapeDtypeStruct((t, SGU_W), BF16),
            jax.ShapeDtypeStruct((t, SGU_W), BF16),
            jax.ShapeDtypeStruct((1, SGU_W), F32),
            jax.ShapeDtypeStruct((SGU_HEADS, CHUNK, CHUNK), F32),
            jax.ShapeDtypeStruct((CHUNK, SGU_W), F32),
        ],
        compiler_params=_params("arbitrary"),
    )(dmix, proj, proj, g_v, w_s, b_exp)


def _ple_bwd(dh, gate, pp, order_after, *, name, br=512):
    t, d = dh.shape
    br = min(br, t)

    def body(dh_ref, g_ref, p_ref, order_ref, dpre_ref, dpp_ref):
        dhv, g = dh_ref[...], g_ref[...]
        dpre_ref[...] = (dhv * p_ref[...] * g * (1.0 - g)).astype(dpre_ref.dtype)
        dpp_ref[...] = (dhv * g).astype(dpp_ref.dtype)

    row = pl.BlockSpec((br, d), lambda i: (i, 0))
    return pl.pallas_call(
        body,
        name=name,
        grid=(t // br,),
        in_specs=[row] * 3 + [pl.BlockSpec(order_after.shape, lambda i: (0, 0))],
        out_specs=[row] * 2,
        out_shape=[jax.ShapeDtypeStruct((t, d), BF16)] * 2,
        compiler_params=_params("parallel"),
    )(dh, gate, pp, order_after)


def _loss_head(y, target, *, name, br=512):
    t, d = y.shape
    br = min(br, t)

    def body(y_ref, t_ref, dy_ref, loss_ref):
        err = y_ref[...] - t_ref[...]
        dy_ref[...] = err * (1.0 / d)

        @pl.when(pl.program_id(0) == 0)
        def _():
            loss_ref[...] = jnp.zeros_like(loss_ref)

        loss_ref[...] += 0.5 * jnp.sum(jnp.sum(err * err, axis=1, keepdims=True) * (1.0 / d), axis=0, keepdims=True)

    row = pl.BlockSpec((br, d), lambda i: (i, 0))
    return pl.pallas_call(
        body,
        name=name,
        grid=(t // br,),
        in_specs=[row, row],
        out_specs=[row, pl.BlockSpec((8, LANES), lambda i: (0, 0))],
        out_shape=[jax.ShapeDtypeStruct((t, d), F32), jax.ShapeDtypeStruct((8, LANES), F32)],
        compiler_params=_params("arbitrary"),
    )(y, target)


def _adamw_update(w, g, m, v):
    nm = ADAM_B1 * m + (1.0 - ADAM_B1) * g
    nv = ADAM_B2 * v + (1.0 - ADAM_B2) * (g * g)
    m_hat = nm / (1.0 - ADAM_B1 ** ADAM_STEP)
    v_hat = nv / (1.0 - ADAM_B2 ** ADAM_STEP)
    return -ADAM_LR * (m_hat / (jnp.sqrt(v_hat) + ADAM_EPS) + ADAM_WD * w), nm, nv


def _adamw(w, g, m, v, *, name, br=512):
    r, c = w.shape
    br = _row_block(r, br)

    def body(w_ref, g_ref, m_ref, v_ref, d_ref, nm_ref, nv_ref):
        d_ref[...], nm_ref[...], nv_ref[...] = _adamw_update(w_ref[...], g_ref[...], m_ref[...], v_ref[...])

    row = pl.BlockSpec((br, c), lambda i: (i, 0))
    return pl.pallas_call(
        body,
        name=name,
        grid=(r // br,),
        in_specs=[row] * 4,
        out_specs=[row] * 3,
        out_shape=[jax.ShapeDtypeStruct((r, c), F32)] * 3,
        compiler_params=_params("parallel"),
    )(w, g, m, v)


def _sum_slots(x, *, name, br=512):
    n, r, c = x.shape
    br = _row_block(r, br)

    def body(x_ref, o_ref):
        acc = x_ref[0].astype(F32)
        for j in range(1, n):
            acc = acc + x_ref[j].astype(F32)
        o_ref[...] = acc

    return pl.pallas_call(
        body,
        name=name,
        grid=(r // br,),
        in_specs=[pl.BlockSpec((n, br, c), lambda i: (0, i, 0))],
        out_specs=pl.BlockSpec((br, c), lambda i: (i, 0)),
        out_shape=jax.ShapeDtypeStruct((r, c), F32),
        compiler_params=_params("parallel"),
    )(x)


_ADAMW_BLOCK_ELEMS = 128 * 1024


def _adamw_reduce(w, arrived, m, v, *, name):
    depth, r, c = w.shape
    br = _row_block(r, max(BF16_TILE_ROWS, _ADAMW_BLOCK_ELEMS // (-(-c // LANES) * LANES)))

    def body(w_ref, m_ref, v_ref, *rest):
        parts, (g_ref, d_ref, nm_ref, nv_ref) = rest[:depth], rest[depth:]
        for li in range(depth):

            @pl.when(pl.program_id(0) == li)
            def _(li=li):
                g = parts[li][0].astype(F32)
                for j in range(1, N_DEV):
                    g = g + parts[li][j].astype(F32)
                g_ref[...] = g
                d_ref[...], nm_ref[...], nv_ref[...] = _adamw_update(w_ref[...], g, m_ref[...], v_ref[...])

    cur = pl.BlockSpec((None, br, c), lambda l, i: (l, i, 0))
    slots = [pl.BlockSpec((N_DEV, br, c), lambda l, i, li=li: (0, jnp.where(l == li, i, 0), 0)) for li in range(depth)]
    return pl.pallas_call(
        body,
        name=name,
        grid=(depth, r // br),
        in_specs=[cur, cur, cur] + slots,
        out_specs=[cur] * 4,
        out_shape=[jax.ShapeDtypeStruct((depth, r, c), F32)] * 4,
        compiler_params=_params("arbitrary", "arbitrary"),
    )(w, m, v, *arrived)


def _my_place():
    return lax.axis_index("x"), lax.axis_index("y"), lax.axis_index("c")


def _flip(v, bit):
    return 1 - v if bit else v


def _slot_of(px, py, pc):
    return 4 * px + 2 * py + pc


_ANY = pl.BlockSpec(memory_space=pl.ANY)


def _all_gather(x, after, *, name):
    r, c = x.shape

    def body(x_ref, after_ref, out_ref, send_sems, recv_sems, local_sem):
        mx, my, mc = _my_place()
        me, sibling = (mx, my, mc), (mx, my, 1 - mc)
        chips = [(1 - mx, my), (mx, 1 - my), (1 - mx, 1 - my)]

        def copy(k, block, to, src=None):
            return pltpu.make_async_remote_copy(
                src_ref=out_ref.at[_slot_of(*block)] if src is None else src,
                dst_ref=out_ref.at[_slot_of(*block)],
                send_sem=send_sems.at[k],
                recv_sem=recv_sems.at[k],
                device_id=to,
                device_id_type=MESH,
            )

        mine = pltpu.make_async_copy(x_ref, out_ref.at[_slot_of(*me)], local_sem)
        mine.start()
        first = [copy(0, me, sibling, src=x_ref)]
        first += [copy(1 + j, me, (*chip, mc), src=x_ref) for j, chip in enumerate(chips)]
        for cp in first:
            cp.start()
        passed = [copy(4 + j, (*chip, mc), sibling) for j, chip in enumerate(chips)]
        for j, chip in enumerate(chips):
            copy(1 + j, (*chip, mc), me).wait_recv()
            passed[j].start()
        copy(0, sibling, me).wait_recv()
        for j, chip in enumerate(chips):
            copy(4 + j, (*chip, 1 - mc), me).wait_recv()
        for cp in first + passed:
            cp.wait_send()
        mine.wait()

    return pl.pallas_call(
        body,
        name=name,
        in_specs=[_ANY, _ANY],
        out_specs=_ANY,
        out_shape=jax.ShapeDtypeStruct((N_DEV, r, c), x.dtype),
        scratch_shapes=[pltpu.SemaphoreType.DMA((7,)), pltpu.SemaphoreType.DMA((7,)), pltpu.SemaphoreType.DMA(())],
        compiler_params=pltpu.CompilerParams(has_side_effects=True),
    )(x, after)


_HBM = pl.BlockSpec(memory_space=pltpu.HBM)
_SEM = pl.BlockSpec(memory_space=pltpu.SEMAPHORE)
_DATAFLOW = pltpu.SideEffectType.DATAFLOW_SIDE_EFFECTING


def _exchange_copies(src_refs, land_refs, send_sem, recv_sem, scatter):
    mx, my, mc = _my_place()
    mine = _slot_of(mx, my, mc)
    copies = []
    for src, land in zip(src_refs, land_refs, strict=True):
        for k in range(1, N_DEV):
            peer = (_flip(mx, k & 4), _flip(my, k & 2), _flip(mc, k & 1))
            copies.append(pltpu.make_async_remote_copy(
                src_ref=src.at[_slot_of(*peer)] if scatter else src, dst_ref=land.at[mine],
                send_sem=send_sem, recv_sem=recv_sem, device_id=peer, device_id_type=MESH))
    return copies


def _exchange_start(srcs, after, *, scatter, name):
    n = len(srcs)
    lands = [lax.empty(s.shape if scatter else (N_DEV, *s.shape), s.dtype) for s in srcs]

    def body(*refs):
        send_sem, recv_sem = refs[2 * n + 1], refs[2 * n + 2]
        for cp in _exchange_copies(refs[:n], refs[n:2 * n], send_sem, recv_sem, scatter):
            cp.start()
        refs[-1][...] = jnp.zeros_like(refs[-1])

    thru = [pltpu.HBM(a.shape, a.dtype) for a in (*srcs, *lands)]
    out = pl.pallas_call(
        body,
        name=name,
        in_specs=[_HBM] * (2 * n) + [_ANY],
        out_specs=(_SEM, _SEM, *[_HBM] * (2 * n), pl.BlockSpec(memory_space=pltpu.VMEM)),
        out_shape=(pltpu.SemaphoreType.DMA(()), pltpu.SemaphoreType.DMA(()), *thru,
                   jax.ShapeDtypeStruct((8, LANES), F32)),
        input_output_aliases={i: 2 + i for i in range(2 * n)},
        compiler_params=pltpu.CompilerParams(has_side_effects=_DATAFLOW),
    )(*[pltpu.with_memory_space_constraint(a, pltpu.HBM) for a in (*srcs, *lands)], after)
    return out[:-1], out[-1]


def _exchange_wait(started, after, *, scatter, name):
    send_sem, recv_sem, *thru = started
    n = len(thru) // 2

    def body(*refs):
        for cp in _exchange_copies(refs[:n], refs[n:2 * n], refs[2 * n], refs[2 * n + 1], scatter):
            cp.wait_send()
            cp.wait_recv()

    out = pl.pallas_call(
        body,
        name=name,
        in_specs=[_HBM] * (2 * n) + [_SEM, _SEM, _ANY],
        out_specs=[_HBM] * (2 * n),
        out_shape=[pltpu.HBM(a.shape, a.dtype) for a in thru],
        input_output_aliases={i: i for i in range(2 * n)},
        compiler_params=pltpu.CompilerParams(has_side_effects=_DATAFLOW),
    )(*thru, send_sem, recv_sem, after)
    return out[:n], out[n:]


def _with_own_slot(landed, own, me):
    return lax.dynamic_update_slice(landed, own[None], (me,) + (0,) * own.ndim)


_PACK_ROWS = 256


def _pack(arrays, dtype):
    flat = [a.astype(dtype).reshape(-1) for a in arrays]
    total = sum(f.shape[0] for f in flat)
    padded = -(-total // (LANES * _PACK_ROWS)) * (LANES * _PACK_ROWS)
    if padded > total:
        flat.append(jnp.zeros((padded - total,), dtype))
    return jnp.concatenate(flat).reshape(-1, LANES)


def _unpack(packed, shapes, lead=()):
    flat = packed.reshape(*lead, -1)
    out, off = [], 0
    for s in shapes:
        size = math.prod(s)
        out.append(flat[..., off:off + size].reshape(*lead, *s))
        off += size
    return out


def _gather_columns(g):
    return jnp.moveaxis(g, 0, 1).reshape(g.shape[1], -1)


def _split_columns(w):
    return jnp.moveaxis(w.reshape(w.shape[0], N_DEV, -1), 1, 0)


def _split_rows(w):
    return w.reshape(N_DEV, w.shape[0] // N_DEV, w.shape[1])


_FIRST = ("w_in",)
_REST = ("w_out", "w_ff1", "w_ff2", "w_ple_gate", "w_ple_proj")
_BIG = _FIRST + _REST
_COLUMN_SHARDED = ("w_in", "w_ff1", "w_ple_proj", "conv_w")
_SMALL = ("norm1_g", "q_norm_g", "k_norm_g", "sgu_norm_g", "sgu_w", "sgu_b", "norm2_g", "norm3_g")
_ORDER = ("norm1_g", "w_in", "conv_w", "q_norm_g", "k_norm_g", "sgu_norm_g", "sgu_w", "sgu_b", "w_out", "norm2_g",
          "w_ff1", "w_ff2", "norm3_g", "w_ple_gate", "w_ple_proj")


def _whole_matrices(names, landed, own, me):
    out = {}
    for k, g, mine in zip(names, landed, own, strict=True):
        g = _with_own_slot(g, mine, me)
        out[k] = _gather_columns(g) if k in _COLUMN_SHARDED else g.reshape(-1, g.shape[-1])
    return out


def _layer_forward(h0, p16, w, s, li, wait_rest):
    nm = lambda k: f"{k}_l{li}"
    t = h0.shape[0]
    hn1 = _rms_fwd(h0, s["norm1_g"], name=nm("rms1"))
    proj = _matmul(hn1, w["w_in"], name=nm("proj"), bm=t, bn=256)
    y_a = _conv_fwd(proj, w["conv_w"], name=nm("conv"))
    qs, kn, v = _qk_prep(proj, s["gq"], s["gk"], name=nm("qkprep"))
    y_b, lt = _attn_fwd(qs, kn, v, name=nm("attn"))
    y_c = _sgu_fwd(proj, s["sgu_norm_g"], s["sgu_w"], s["b_exp"], name=nm("sgu"))
    mix = jnp.concatenate([y_a, y_b, y_c], axis=1)
    w = dict(w, **wait_rest(y_b))
    h1 = _matmul(mix, w["w_out"], name=nm("out"), bm=t, bn=256, extras=(h0,), epilogue=lambda acc, r: (r + acc,))
    hn2 = _rms_fwd(h1, s["norm2_g"], name=nm("rms2"))
    u, f = _matmul(hn2, w["w_ff1"], name=nm("ff1"), bm=t, bn=512, out_dtypes=(F32, BF16),
                   epilogue=lambda acc: (acc, jnp.square(jnp.maximum(acc, 0.0))))
    h2 = _matmul(f, w["w_ff2"], name=nm("ff2"), bm=512, bn=512, extras=(h1,), epilogue=lambda acc, r: (r + acc,))
    hn3 = _rms_fwd(h2, s["norm3_g"], name=nm("rms3"))
    pp = _matmul(p16, w["w_ple_proj"], name=nm("pleproj"), bm=t, bn=512)

    def gate_epilogue(acc, pp_blk, h_blk):
        gate = jax.nn.sigmoid(acc)
        return h_blk + gate * pp_blk, gate

    h3, gate = _matmul(hn3, w["w_ple_gate"], name=nm("plegate"), bm=t, bn=256, out_dtypes=(F32, F32),
                       extras=(pp, h2), epilogue=gate_epilogue)
    saved = dict(h0=h0, hn1=hn1, proj=proj, qs=qs, kn=kn, v=v, lt=lt, mix=mix, h1=h1, hn2=hn2, u=u, f=f, h2=h2,
                 hn3=hn3, pp=pp, gate=gate, p16=p16)
    return h3, w, saved


def _layer_backward(dh3, a, w, s, li, order_after, start_rest):
    nm = lambda k: f"{k}_bwd_l{li}"
    t = dh3.shape[0]
    dpre, dpp = _ple_bwd(dh3, a["gate"], a["pp"], order_after, name=nm("ple"))
    g_gate = _matmul(a["hn3"], dpre, name=nm("dwgate"), ta=True, bm=512, bn=512, out_dtypes=(BF16,))
    g_proj = _matmul(a["p16"], dpp, name=nm("dwproj"), ta=True, bm=256, out_dtypes=(BF16,), column_shards=True)
    dhn3 = _matmul(dpre, w["w_ple_gate"], name=nm("dhn3"), tb=True, bm=t, bn=256)
    dh2, dh2_16, g_n3 = _rms_bwd(dhn3, a["h2"], s["norm3_g"], dh3, name=nm("rms3"))
    du = _matmul(dh2_16, w["w_ff2"], name=nm("du"), tb=True, bm=t, bn=512, out_dtypes=(BF16,), extras=(a["u"],),
                 epilogue=lambda acc, u: (acc * (2.0 * jnp.maximum(u, 0.0)),))
    g_ff2 = _matmul(a["f"], dh2_16, name=nm("dwff2"), ta=True, bm=512, bn=512, out_dtypes=(BF16,))
    g_ff1 = _matmul(a["hn2"], du, name=nm("dwff1"), ta=True, bm=512, out_dtypes=(BF16,), column_shards=True)
    dhn2 = _matmul(du, w["w_ff1"], name=nm("dhn2"), tb=True, bm=512, bn=512)
    dh1, dh1_16, g_n2 = _rms_bwd(dhn2, a["h1"], s["norm2_g"], dh2, name=nm("rms2"))
    dmix = _matmul(dh1_16, w["w_out"], name=nm("dmix"), tb=True, bm=t, bn=256)
    g_out = _matmul(a["mix"], dh1_16, name=nm("dwout"), ta=True, bm=512, bn=512, out_dtypes=(BF16,))
    started = start_rest(dict(w_out=_split_rows(g_out), w_ff1=g_ff1, w_ff2=_split_rows(g_ff2),
                              w_ple_gate=_split_rows(g_gate), w_ple_proj=g_proj), dmix)
    d_b, d_c, d_h, g_conv = _conv_bwd(dmix, a["proj"], w["conv_w"], name=nm("conv"))
    dqs, dkn, dv = _attn_bwd(dmix, a["qs"], a["kn"], a["v"], a["lt"], started, name=nm("attn"))
    d_q, d_k, d_v, g_q, g_k = _qk_prep_bwd(dqs, dkn, dv, a["proj"], s["gq"], s["gk"], name=nm("qkprep"))
    d_cu, d_cv, g_sn, g_sw, g_sb = _sgu_bwd(dmix, a["proj"], s["sgu_norm_g"], s["sgu_w"], s["b_exp"], name=nm("sgu"))
    dproj = jnp.concatenate([d_b, d_c, d_h, d_q, d_k, d_v, d_cu, d_cv], axis=1)
    g_in = _matmul(a["hn1"], dproj, name=nm("dwin"), ta=True, bm=512, bn=256, out_dtypes=(BF16,))
    dhn1 = _matmul(dproj, w["w_in"], name=nm("dhn1"), tb=True, bm=t, bn=256)
    dh0, _, g_n1 = _rms_bwd(dhn1, a["h0"], s["norm1_g"], dh1, name=nm("rms1"))
    n_tiles = g_q.shape[1] // HEAD_DIM
    small = dict(
        norm1_g=g_n1[0], norm2_g=g_n2[0], norm3_g=g_n3[0],
        q_norm_g=g_q.reshape(n_tiles, HEAD_DIM).sum(0), k_norm_g=g_k.reshape(n_tiles, HEAD_DIM).sum(0),
        sgu_norm_g=g_sn[0], sgu_w=g_sw, sgu_b=g_sb.reshape(CHUNK, SGU_HEADS, HEAD_DIM).sum(-1).T,
        conv_w=g_conv[:CONV_TAPS],
    )
    return dh0, _split_columns(g_in), small


def kernel(x, p, norm1_g, w_in, conv_w, q_norm_g, k_norm_g, sgu_norm_g, sgu_w, sgu_b, w_out, norm2_g, w_ff1, w_ff2, norm3_g, w_ple_gate, w_ple_proj, loss_target, m_norm1_g, m_w_in, m_conv_w, m_q_norm_g, m_k_norm_g, m_sgu_norm_g, m_sgu_w, m_sgu_b, m_w_out, m_norm2_g, m_w_ff1, m_w_ff2, m_norm3_g, m_w_ple_gate, m_w_ple_proj, v_norm1_g, v_w_in, v_conv_w, v_q_norm_g, v_k_norm_g, v_sgu_norm_g, v_sgu_w, v_sgu_b, v_w_out, v_norm2_g, v_w_ff1, v_w_ff2, v_norm3_g, v_w_ple_gate, v_w_ple_proj):
    weights = dict(norm1_g=norm1_g, w_in=w_in, conv_w=conv_w, q_norm_g=q_norm_g, k_norm_g=k_norm_g,
                   sgu_norm_g=sgu_norm_g, sgu_w=sgu_w, sgu_b=sgu_b, w_out=w_out, norm2_g=norm2_g, w_ff1=w_ff1,
                   w_ff2=w_ff2, norm3_g=norm3_g, w_ple_gate=w_ple_gate, w_ple_proj=w_ple_proj)
    mom = dict(norm1_g=m_norm1_g, w_in=m_w_in, conv_w=m_conv_w, q_norm_g=m_q_norm_g, k_norm_g=m_k_norm_g,
               sgu_norm_g=m_sgu_norm_g, sgu_w=m_sgu_w, sgu_b=m_sgu_b, w_out=m_w_out, norm2_g=m_norm2_g, w_ff1=m_w_ff1,
               w_ff2=m_w_ff2, norm3_g=m_norm3_g, w_ple_gate=m_w_ple_gate, w_ple_proj=m_w_ple_proj)
    var = dict(norm1_g=v_norm1_g, w_in=v_w_in, conv_w=v_conv_w, q_norm_g=v_q_norm_g, k_norm_g=v_k_norm_g,
               sgu_norm_g=v_sgu_norm_g, sgu_w=v_sgu_w, sgu_b=v_sgu_b, w_out=v_w_out, norm2_g=v_norm2_g, w_ff1=v_w_ff1,
               w_ff2=v_w_ff2, norm3_g=v_norm3_g, w_ple_gate=v_w_ple_gate, w_ple_proj=v_w_ple_proj)
    depth = norm1_g.shape[0]
    mx, my, mc = _my_place()
    me = _slot_of(mx, my, mc)

    gather_first, gather_rest = [], []
    token = x[0, :8, :LANES]
    for li in range(depth):
        started, token = _exchange_start([w_in[li].astype(BF16), conv_w[li]], token, scatter=False,
                                         name=f"gather_first_start_l{li}")
        gather_first.append(started)
        started, token = _exchange_start([weights[k][li].astype(BF16) for k in _REST], token, scatter=False,
                                         name=f"gather_rest_start_l{li}")
        gather_rest.append(started)

    small = []
    for li in range(depth):
        small.append(dict(
            norm1_g=norm1_g[li][None], norm2_g=norm2_g[li][None], norm3_g=norm3_g[li][None],
            gq=jnp.tile(q_norm_g[li], _QK_BLOCK // HEAD_DIM)[None], gk=jnp.tile(k_norm_g[li], _QK_BLOCK // HEAD_DIM)[None],
            sgu_norm_g=sgu_norm_g[li][None], sgu_w=sgu_w[li], b_exp=jnp.repeat(sgu_b[li].T, HEAD_DIM, axis=1),
        ))

    h = x[0]
    saved, full = [], []
    for li in range(depth):
        own, landed = _exchange_wait(gather_first[li], token if li == 0 else h, scatter=False,
                                     name=f"gather_first_wait_l{li}")
        first = _whole_matrices(("w_in", "conv_w"), landed, own, me)

        def wait_rest(after, li=li):
            own, landed = _exchange_wait(gather_rest[li], after, scatter=False, name=f"gather_rest_wait_l{li}")
            return _whole_matrices(_REST, landed, own, me)

        h, w, acts = _layer_forward(h, p[li, 0].astype(BF16), first, small[li], li, wait_rest)
        full.append(w)
        saved.append(acts)
    dh, loss_tile = _loss_head(h, loss_target[0], name="loss_head")
    loss = lax.psum(loss_tile[0, 0], ("x", "y", "c"))

    small_grads = [None] * depth
    scatter_first, scatter_rest = [None] * depth, [None] * depth
    token = loss_tile
    for li in reversed(range(depth)):

        def start_rest(parts, after, li=li):
            scatter_rest[li], started = _exchange_start([parts[k] for k in _REST], after, scatter=True,
                                                        name=f"scatter_rest_start_l{li}")
            return started

        dh, g_in, small_grads[li] = _layer_backward(dh, saved[li], full[li], small[li], li, token, start_rest)
        scatter_first[li], token = _exchange_start([g_in], dh, scatter=True, name=f"scatter_first_start_l{li}")
    grad_x = dh[None]

    small_names = _SMALL + ("conv_w",)
    small_shapes = [(depth,) + small_grads[0][k].shape for k in small_names]
    partial = _pack([jnp.stack([small_grads[li][k] for li in range(depth)]) for k in small_names], F32)

    grads, delta, new_m, new_v = {}, {}, {}, {}

    def reduce_and_update(names, scatters, after):
        arrived = {k: [None] * depth for k in names}
        for li in reversed(range(depth)):
            sent, landed = _exchange_wait(scatters[li], after, scatter=True, name=f"scatter_{names[0]}_wait_l{li}")
            for k, g, src in zip(names, landed, sent, strict=True):
                arrived[k][li] = _with_own_slot(g, lax.dynamic_index_in_dim(src, me, 0, keepdims=False), me)
        for k in names:
            grads[k], delta[k], new_m[k], new_v[k] = _adamw_reduce(weights[k], arrived[k], mom[k], var[k],
                                                                   name=f"adamw_{k}")

    reduce_and_update(_REST, scatter_rest, token)
    total = _sum_slots(_all_gather(partial, delta[_REST[-1]], name="gather_small_grads"), name="sum_small_grads")
    reduce_and_update(_FIRST, scatter_first, total)
    grads.update(zip(small_names, _unpack(total, small_shapes)))
    n_conv = conv_w.shape[2]
    grads["conv_w"] = lax.dynamic_slice_in_dim(grads["conv_w"], me * n_conv, n_conv, axis=2)
    rest = _SMALL + ("conv_w",)
    rest_shapes = [weights[k].shape for k in rest]
    packs = [_pack([src[k] for k in rest], F32) for src in (weights, grads, mom, var)]
    for out, packed in zip((delta, new_m, new_v), _adamw(*packs, name="adamw_small")):
        out.update(zip(rest, _unpack(packed, rest_shapes)))

    return (loss, grad_x, *[grads[k] for k in _ORDER], *[delta[k] for k in _ORDER],
            *[new_m[k] for k in _ORDER], *[new_v[k] for k in _ORDER])
```

```python
import math

import jax
import jax.numpy as jnp
from jax import lax
from jax.experimental import pallas as pl
from jax.experimental.pallas import tpu as pltpu

F32 = jnp.float32
BF16 = jnp.bfloat16

N_DEV = 8
HEAD_DIM = 64
CONV_W = 256
ATTN_W = 512
SGU_W = 256
SGU_HEADS = 4
CHUNK = 128
CONV_TAPS = 3
EPS = 1e-6
QK_SCALE = HEAD_DIM ** -0.5

ADAM_LR = 0.001
ADAM_B1 = 0.9
ADAM_B2 = 0.999
ADAM_EPS = 1e-08
ADAM_WD = 0.01
ADAM_STEP = 10

LANES = 128
BF16_TILE_ROWS = 16
VMEM_LIMIT_BYTES = 56 * 1024 * 1024
MESH = pl.DeviceIdType.MESH


def _params(*sem):
    return pltpu.CompilerParams(dimension_semantics=sem, vmem_limit_bytes=VMEM_LIMIT_BYTES)


def _row_block(rows, cap):
    if rows <= cap:
        return rows
    return max(b for b in range(BF16_TILE_ROWS, cap + 1, BF16_TILE_ROWS) if rows % b == 0)


def _matmul(a, b, *, name, ta=False, tb=False, bm=512, bn=512, out_dtypes=(F32,), epilogue=None, extras=(),
            column_shards=False):
    m = a.shape[1] if ta else a.shape[0]
    k = a.shape[0] if ta else a.shape[1]
    n = b.shape[0] if tb else b.shape[1]
    assert k == (b.shape[1] if tb else b.shape[0])
    bm, bn = min(bm, m), min(bn, n)
    if column_shards:
        bn = n // N_DEV
    assert m % bm == 0 and n % bn == 0
    a_spec = pl.BlockSpec((k, bm), lambda i, j: (0, i)) if ta else pl.BlockSpec((bm, k), lambda i, j: (i, 0))
    b_spec = pl.BlockSpec((bn, k), lambda i, j: (j, 0)) if tb else pl.BlockSpec((k, bn), lambda i, j: (0, j))
    dims = (((0 if ta else 1,), (1 if tb else 0,)), ((), ()))
    n_ex = len(extras)
    for e in extras:
        assert e.shape == (m, n), (e.shape, m, n)

    def body(a_ref, b_ref, *rest):
        outs = rest[n_ex:]
        acc = lax.dot_general(a_ref[...], b_ref[...], dims, preferred_element_type=F32)
        res = (acc,) if epilogue is None else epilogue(acc, *[e[...] for e in rest[:n_ex]])
        for o_ref, r in zip(outs, res, strict=True):
            o_ref[...] = r.astype(o_ref.dtype)

    tile = pl.BlockSpec((bm, bn), lambda i, j: (i, j))
    out_tile, out_dims = tile, (m, n)
    if column_shards:
        out_tile, out_dims = pl.BlockSpec((None, bm, bn), lambda i, j: (j, i, 0)), (N_DEV, m, bn)
    out = pl.pallas_call(
        body,
        name=name,
        grid=(m // bm, n // bn),
        in_specs=[a_spec, b_spec] + [tile] * n_ex,
        out_specs=[out_tile] * len(out_dtypes),
        out_shape=[jax.ShapeDtypeStruct(out_dims, d) for d in out_dtypes],
        compiler_params=_params("parallel", "parallel"),
    )(a, b, *extras)
    return out[0] if len(out_dtypes) == 1 else out


def _rms_fwd(h, g, *, name, br=512):
    t, d = h.shape
    br = min(br, t)

    def body(h_ref, g_ref, o_ref):
        x = h_ref[...]
        r = lax.rsqrt(jnp.mean(x * x, axis=-1, keepdims=True) + EPS)
        o_ref[...] = (x * r * g_ref[...]).astype(o_ref.dtype)

    return pl.pallas_call(
        body,
        name=name,
        grid=(t // br,),
        in_specs=[pl.BlockSpec((br, d), lambda i: (i, 0)), pl.BlockSpec((1, d), lambda i: (0, 0))],
        out_specs=pl.BlockSpec((br, d), lambda i: (i, 0)),
        out_shape=jax.ShapeDtypeStruct((t, d), BF16),
        compiler_params=_params("parallel"),
    )(h, g)


def _rms_bwd(dy, h, g, dres, *, name, br=512):
    t, d = h.shape
    br = min(br, t)

    def body(dy_ref, h_ref, g_ref, dres_ref, dh_ref, dh16_ref, dg_ref):
        x = h_ref[...]
        dyv = dy_ref[...]
        r = lax.rsqrt(jnp.mean(x * x, axis=-1, keepdims=True) + EPS)
        xhat = x * r
        dxhat = dyv * g_ref[...]
        dh = dres_ref[...] + r * (dxhat - xhat * jnp.mean(dxhat * xhat, axis=-1, keepdims=True))
        dh_ref[...] = dh
        dh16_ref[...] = dh.astype(dh16_ref.dtype)

        @pl.when(pl.program_id(0) == 0)
        def _():
            dg_ref[...] = jnp.zeros_like(dg_ref)

        dg_ref[...] += jnp.sum(dyv * xhat, axis=0, keepdims=True)

    row = pl.BlockSpec((br, d), lambda i: (i, 0))
    vec = pl.BlockSpec((1, d), lambda i: (0, 0))
    return pl.pallas_call(
        body,
        name=name,
        grid=(t // br,),
        in_specs=[row, row, vec, row],
        out_specs=[row, row, vec],
        out_shape=[jax.ShapeDtypeStruct((t, d), F32), jax.ShapeDtypeStruct((t, d), BF16),
                   jax.ShapeDtypeStruct((1, d), F32)],
        compiler_params=_params("arbitrary"),
    )(dy, h, g, dres)


def _group_mean(x, width):
    grp = lax.broadcasted_iota(jnp.int32, x.shape, 1) // HEAD_DIM
    out = jnp.zeros_like(x)
    for gi in range(width // HEAD_DIM):
        m = grp == gi
        s = jnp.sum(jnp.where(m, x, 0.0), axis=1, keepdims=True)
        out = jnp.where(m, s, out)
    return out * (1.0 / HEAD_DIM)


def _gelu(x):
    return 0.5 * x * (1.0 + lax.erf(x * (2.0 ** -0.5)))


def _gelu_grad(x):
    cdf = 0.5 * (1.0 + lax.erf(x * (2.0 ** -0.5)))
    pdf = jnp.exp(-0.5 * x * x) * (1.0 / math.sqrt(2.0 * math.pi))
    return cdf + x * pdf


def _shift_down(z, s, row):
    return jnp.where(row >= s, pltpu.roll(z, s, 0), 0.0)


def _shift_up(z, s, row, t):
    return jnp.where(row < t - s, pltpu.roll(z, t - s, 0), 0.0)


def _conv_fwd(proj, conv_w, *, name):
    t = proj.shape[0]
    nb = CONV_W // LANES

    def body(b_ref, c_ref, h_ref, w_ref, o_ref):
        row = lax.broadcasted_iota(jnp.int32, (t, LANES), 0)
        z = c_ref[...] * h_ref[...]
        w = w_ref[...]
        conv = w[2:3, :] * z + w[1:2, :] * _shift_down(z, 1, row) + w[0:1, :] * _shift_down(z, 2, row)
        o_ref[...] = (b_ref[...] * conv).astype(o_ref.dtype)

    return pl.pallas_call(
        body,
        name=name,
        grid=(nb,),
        in_specs=[
            pl.BlockSpec((t, LANES), lambda j: (0, j)),
            pl.BlockSpec((t, LANES), lambda j: (0, nb + j)),
            pl.BlockSpec((t, LANES), lambda j: (0, 2 * nb + j)),
            pl.BlockSpec((CONV_TAPS, LANES), lambda j: (0, j)),
        ],
        out_specs=pl.BlockSpec((t, LANES), lambda j: (0, j)),
        out_shape=jax.ShapeDtypeStruct((t, CONV_W), BF16),
        compiler_params=_params("parallel"),
    )(proj, proj, proj, conv_w)


def _conv_bwd(dmix, proj, conv_w, *, name):
    t = proj.shape[0]
    nb = CONV_W // LANES

    def body(dy_ref, b_ref, c_ref, h_ref, w_ref, db_ref, dc_ref, dh_ref, dw_ref):
        row = lax.broadcasted_iota(jnp.int32, (t, LANES), 0)
        ac, ah = c_ref[...], h_ref[...]
        z = ac * ah
        w = w_ref[...]
        z1 = _shift_down(z, 1, row)
        z2 = _shift_down(z, 2, row)
        conv = w[2:3, :] * z + w[1:2, :] * z1 + w[0:1, :] * z2
        dy = dy_ref[...]
        db_ref[...] = (dy * conv).astype(db_ref.dtype)
        dconv = dy * b_ref[...]
        dz = w[2:3, :] * dconv + w[1:2, :] * _shift_up(dconv, 1, row, t) + w[0:1, :] * _shift_up(dconv, 2, row, t)
        dc_ref[...] = (dz * ah).astype(dc_ref.dtype)
        dh_ref[...] = (dz * ac).astype(dh_ref.dtype)
        dw_ref[...] = jnp.zeros_like(dw_ref)
        dw_ref[0:1, :] = jnp.sum(dconv * z2, axis=0, keepdims=True)
        dw_ref[1:2, :] = jnp.sum(dconv * z1, axis=0, keepdims=True)
        dw_ref[2:3, :] = jnp.sum(dconv * z, axis=0, keepdims=True)

    col = lambda off: pl.BlockSpec((t, LANES), lambda j: (0, off + j))
    return pl.pallas_call(
        body,
        name=name,
        grid=(nb,),
        in_specs=[col(0), col(0), col(nb), col(2 * nb), pl.BlockSpec((CONV_TAPS, LANES), lambda j: (0, j))],
        out_specs=[col(0), col(0), col(0), pl.BlockSpec((8, LANES), lambda j: (0, j))],
        out_shape=[jax.ShapeDtypeStruct((t, CONV_W), BF16)] * 3 + [jax.ShapeDtypeStruct((8, CONV_W), F32)],
        compiler_params=_params("parallel"),
    )(dmix, proj, proj, proj, conv_w)


_QK_BLOCK = 256


def _qk_prep(proj, gq, gk, *, name, br=512):
    t = proj.shape[0]
    br = min(br, t)
    nb = ATTN_W // _QK_BLOCK
    q0 = (3 * CONV_W) // _QK_BLOCK

    def body(q_ref, k_ref, v_ref, gq_ref, gk_ref, qo_ref, ko_ref, vo_ref):
        q = q_ref[...]
        k = k_ref[...]
        rq = lax.rsqrt(_group_mean(q * q, _QK_BLOCK) + EPS)
        rk = lax.rsqrt(_group_mean(k * k, _QK_BLOCK) + EPS)
        qo_ref[...] = ((q * rq * gq_ref[...]).astype(BF16) * QK_SCALE).astype(qo_ref.dtype)
        ko_ref[...] = (k * rk * gk_ref[...]).astype(ko_ref.dtype)
        vo_ref[...] = v_ref[...].astype(vo_ref.dtype)

    col = lambda off: pl.BlockSpec((br, _QK_BLOCK), lambda i, j: (i, off + j))
    vec = pl.BlockSpec((1, _QK_BLOCK), lambda i, j: (0, 0))
    return pl.pallas_call(
        body,
        name=name,
        grid=(t // br, nb),
        in_specs=[col(q0), col(q0 + nb), col(q0 + 2 * nb), vec, vec],
        out_specs=[col(0)] * 3,
        out_shape=[jax.ShapeDtypeStruct((t, ATTN_W), BF16)] * 3,
        compiler_params=_params("parallel", "parallel"),
    )(proj, proj, proj, gq, gk)


def _qk_prep_bwd(dqs, dkn, dv, proj, gq, gk, *, name, br=512):
    t = proj.shape[0]
    br = min(br, t)
    nb = ATTN_W // _QK_BLOCK
    q0 = (3 * CONV_W) // _QK_BLOCK

    def norm_bwd(dy, x, g):
        r = lax.rsqrt(_group_mean(x * x, _QK_BLOCK) + EPS)
        xhat = x * r
        dxhat = dy * g
        dx = r * (dxhat - xhat * _group_mean(dxhat * xhat, _QK_BLOCK))
        return dx, jnp.sum(dy * xhat, axis=0, keepdims=True)

    def body(dq_ref, dk_ref, dv_ref, q_ref, k_ref, gq_ref, gk_ref, oq_ref, ok_ref, ov_ref, dgq_ref, dgk_ref):
        dq, dgq = norm_bwd(dq_ref[...] * QK_SCALE, q_ref[...], gq_ref[...])
        dk, dgk = norm_bwd(dk_ref[...], k_ref[...], gk_ref[...])
        oq_ref[...] = dq.astype(oq_ref.dtype)
        ok_ref[...] = dk.astype(ok_ref.dtype)
        ov_ref[...] = dv_ref[...].astype(ov_ref.dtype)

        @pl.when((pl.program_id(0) == 0) & (pl.program_id(1) == 0))
        def _():
            dgq_ref[...] = jnp.zeros_like(dgq_ref)
            dgk_ref[...] = jnp.zeros_like(dgk_ref)

        dgq_ref[...] += dgq
        dgk_ref[...] += dgk

    col = lambda off: pl.BlockSpec((br, _QK_BLOCK), lambda i, j: (i, off + j))
    vec = pl.BlockSpec((1, _QK_BLOCK), lambda i, j: (0, 0))
    return pl.pallas_call(
        body,
        name=name,
        grid=(t // br, nb),
        in_specs=[col(0), col(0), col(0), col(q0), col(q0 + nb), vec, vec],
        out_specs=[col(0), col(0), col(0), vec, vec],
        out_shape=[jax.ShapeDtypeStruct((t, ATTN_W), BF16)] * 3 + [jax.ShapeDtypeStruct((1, _QK_BLOCK), F32)] * 2,
        compiler_params=_params("arbitrary", "arbitrary"),
    )(dqs, dkn, dv, proj, proj, gq, gk)


def _split_bf16(x):
    hi = x.astype(BF16)
    return jnp.concatenate([hi, (x - hi.astype(F32)).astype(BF16)], axis=1)


def _key_order_matrix(tb, relation):
    jj = lax.broadcasted_iota(jnp.int32, (tb, tb), 0)
    ss = lax.broadcasted_iota(jnp.int32, (tb, tb), 1)
    m = relation(jj, ss).astype(BF16)
    return jnp.concatenate([m, m], axis=0)


def _log_sigmoids(z):
    lb = jnp.minimum(z, 0.0) - jnp.log(1.0 + jnp.exp(-jnp.abs(z)))
    return lb, lb - z


def _below_diagonal(tb):
    return lax.broadcasted_iota(jnp.int32, (tb, tb), 1) < lax.broadcasted_iota(jnp.int32, (tb, tb), 0)


_NT = (((1,), (1,)), ((), ()))
_TN = (((0,), (0,)), ((), ()))
_ATTN_BLOCK = 256
_ATTN_UNROLL = 2


def _attn_fwd(qs, kn, v, *, name, tb=_ATTN_BLOCK, unroll=_ATTN_UNROLL):
    t = qs.shape[0]
    tb = min(tb, t)
    assert t % tb == 0
    n_pairs = ATTN_W // LANES

    def body(q_ref, k_ref, v_ref, o_ref, lt_ref, acc_ref, carry_ref):
        qb = pl.program_id(1)
        half = lax.broadcasted_iota(jnp.int32, (1, LANES), 1) // HEAD_DIM
        later = _key_order_matrix(tb, lambda j, s: j > s)
        acc_ref[...] = jnp.zeros_like(acc_ref)
        carry_ref[...] = jnp.zeros_like(carry_ref)
        q = q_ref[...]
        qh = [jnp.where(half == h, q, jnp.zeros_like(q)) for h in range(2)]

        def tiles(kbs, diagonal):
            blk = []
            for kb in kbs:
                start = pl.multiple_of(kb * tb, tb)
                blk.append((k_ref[pl.ds(start, tb), :], v_ref[pl.ds(start, tb), :]))
            chains = [(h, j) for j in range(len(kbs)) for h in range(2)]
            z = [lax.dot_general(qh[h], blk[j][0], _NT, preferred_element_type=F32) for h, j in chains]
            causal = _below_diagonal(tb) if diagonal else None
            lb, lr = [], []
            for zi in z:
                b, r = _log_sigmoids(zi)
                lb.append(b)
                lr.append(jnp.where(causal, r, 0.0) if diagonal else r)
            suffix = [jnp.dot(_split_bf16(r), later, preferred_element_type=F32) for r in lr]
            carry = [carry_ref[0], carry_ref[1]]
            w = []
            for i, (h, j) in enumerate(chains):
                wi = jnp.exp(lb[i] + (suffix[i] + carry[h][:, 0:1]))
                w.append((jnp.where(causal, wi, 0.0) if diagonal else wi).astype(BF16))
                carry[h] = carry[h] + jnp.sum(lr[i], axis=1, keepdims=True)
            for i, (h, j) in enumerate(chains):
                vh = jnp.where(half == h, blk[j][1], jnp.zeros_like(blk[j][1]))
                acc_ref[h] += jnp.dot(w[i], vh, preferred_element_type=F32)
            carry_ref[0] = carry[0]
            carry_ref[1] = carry[1]

        tiles([qb], True)

        def step(i, _):
            kb = qb - 1 - unroll * i
            tiles([kb - u for u in range(unroll)], False)
            return 0

        lax.fori_loop(0, qb // unroll, step, 0)
        for left in range(1, unroll):

            @pl.when(qb % unroll == left)
            def _(left=left):
                tiles([left - 1 - u for u in range(left)], False)

        o_ref[...] = (acc_ref[0] + acc_ref[1]).astype(o_ref.dtype)
        lt_ref[...] = jnp.where(half == 0, carry_ref[0], carry_ref[1])

    return pl.pallas_call(
        body,
        name=name,
        grid=(n_pairs, t // tb),
        in_specs=[
            pl.BlockSpec((tb, LANES), lambda p, i: (i, p)),
            pl.BlockSpec((t, LANES), lambda p, i: (0, p)),
            pl.BlockSpec((t, LANES), lambda p, i: (0, p)),
        ],
        out_specs=[pl.BlockSpec((tb, LANES), lambda p, i: (i, p))] * 2,
        out_shape=[jax.ShapeDtypeStruct((t, ATTN_W), BF16), jax.ShapeDtypeStruct((t, ATTN_W), F32)],
        scratch_shapes=[pltpu.VMEM((2, tb, LANES), F32), pltpu.VMEM((2, tb, LANES), F32)],
        compiler_params=_params("parallel", "parallel"),
    )(qs, kn, v)


def _attn_bwd(dmix, qs, kn, v, lt, order_after, *, name, tb=_ATTN_BLOCK, unroll=_ATTN_UNROLL):
    t = qs.shape[0]
    tb = min(tb, t)
    assert t % tb == 0
    n_pairs = ATTN_W // LANES
    dy0 = CONV_W // LANES

    def body(do_ref, q_ref, k_ref, v_ref, lt_ref, order_ref, dq_ref, dk_ref, dv_ref, dqacc_ref, cc_ref, cg_ref):
        qb = pl.program_id(1)
        half = lax.broadcasted_iota(jnp.int32, (1, LANES), 1) // HEAD_DIM
        lane = lax.broadcasted_iota(jnp.int32, (tb, LANES), 1)
        upto = _key_order_matrix(tb, lambda j, s: j <= s)
        before = _key_order_matrix(tb, lambda j, s: j < s)
        q = q_ref[...]
        do = do_ref[...].astype(BF16)
        lt = lt_ref[...]
        qh = [jnp.where(half == h, q, jnp.zeros_like(q)) for h in range(2)]
        doh = [jnp.where(half == h, do, jnp.zeros_like(do)) for h in range(2)]
        lth = [jnp.sum(jnp.where(lane == h * HEAD_DIM, lt, 0.0), axis=1, keepdims=True) for h in range(2)]

        @pl.when(qb == 0)
        def _():
            dk_ref[...] = jnp.zeros_like(dk_ref)
            dv_ref[...] = jnp.zeros_like(dv_ref)

        dqacc_ref[...] = jnp.zeros_like(dqacc_ref)
        cc_ref[...] = jnp.zeros_like(cc_ref)
        cg_ref[...] = jnp.zeros_like(cg_ref)

        def tiles(kbs, diagonal):
            starts = [pl.multiple_of(kb * tb, tb) for kb in kbs]
            blk = [(k_ref[pl.ds(s, tb), :], v_ref[pl.ds(s, tb), :]) for s in starts]
            chains = [(h, j) for j in range(len(kbs)) for h in range(2)]
            z = [lax.dot_general(qh[h], blk[j][0], _NT, preferred_element_type=F32) for h, j in chains]
            da = [lax.dot_general(doh[h], jnp.where(half == h, blk[j][1], jnp.zeros_like(blk[j][1])), _NT,
                                  preferred_element_type=F32) for h, j in chains]
            causal = _below_diagonal(tb) if diagonal else None
            lb, lr = [], []
            for zi in z:
                b, r = _log_sigmoids(zi)
                lb.append(b)
                lr.append(jnp.where(causal, r, 0.0) if diagonal else r)
            prefix = [jnp.dot(_split_bf16(r), upto, preferred_element_type=F32) for r in lr]
            cc = [cc_ref[0], cc_ref[1]]
            cg = [cg_ref[0], cg_ref[1]]
            a16, g = [], []
            for i, (h, j) in enumerate(chains):
                a = jnp.exp(lb[i] - prefix[i] + (lth[h] - cc[h][:, 0:1]))
                if diagonal:
                    a = jnp.where(causal, a, 0.0)
                a16.append(a.astype(BF16))
                g.append(da[i] * a)
                cc[h] = cc[h] + jnp.sum(lr[i], axis=1, keepdims=True)
            g_before = [jnp.dot(_split_bf16(gi), before, preferred_element_type=F32) for gi in g]
            dz = []
            for i, (h, j) in enumerate(chains):
                dzi = g[i] - jnp.exp(lb[i]) * (g[i] + (g_before[i] + cg[h][:, 0:1]))
                dz.append((jnp.where(causal, dzi, 0.0) if diagonal else dzi).astype(BF16))
                cg[h] = cg[h] + jnp.sum(g[i], axis=1, keepdims=True)
            for i, (h, j) in enumerate(chains):
                kh = jnp.where(half == h, blk[j][0], jnp.zeros_like(blk[j][0]))
                dqacc_ref[h] += jnp.dot(dz[i], kh, preferred_element_type=F32)
                dk_ref[pl.ds(starts[j], tb), :] += lax.dot_general(dz[i], qh[h], _TN, preferred_element_type=F32)
                dv_ref[pl.ds(starts[j], tb), :] += lax.dot_general(a16[i], doh[h], _TN, preferred_element_type=F32)
            for h in range(2):
                cc_ref[h] = cc[h]
                cg_ref[h] = cg[h]

        def step(i, _):
            kb = unroll * i
            tiles([kb + u for u in range(unroll)], False)
            return 0

        lax.fori_loop(0, qb // unroll, step, 0)
        for left in range(1, unroll):

            @pl.when(qb % unroll == left)
            def _(left=left):
                tiles([qb - left + u for u in range(left)], False)

        tiles([qb], True)
        dq_ref[...] = dqacc_ref[0] + dqacc_ref[1]

    qblk = pl.BlockSpec((tb, LANES), lambda p, i: (i, p))
    whole = pl.BlockSpec((t, LANES), lambda p, i: (0, p))
    return pl.pallas_call(
        body,
        name=name,
        grid=(n_pairs, t // tb),
        in_specs=[pl.BlockSpec((tb, LANES), lambda p, i: (i, dy0 + p)), qblk, whole, whole, qblk,
                  pl.BlockSpec(order_after.shape, lambda p, i: (0, 0))],
        out_specs=[qblk, whole, whole],
        out_shape=[jax.ShapeDtypeStruct((t, ATTN_W), F32)] * 3,
        scratch_shapes=[pltpu.VMEM((2, tb, LANES), F32)] * 3,
        compiler_params=_params("parallel", "arbitrary"),
    )(dmix, qs, kn, v, lt, order_after)


def _sgu_weights(w_ref):
    tt = lax.broadcasted_iota(jnp.int32, (CHUNK, CHUNK), 0)
    ss = lax.broadcasted_iota(jnp.int32, (CHUNK, CHUNK), 1)
    tril = ss <= tt
    return [jnp.where(tril, w_ref[gi], 0.0).astype(BF16) for gi in range(SGU_HEADS)], tril


def _sgu_fwd(proj, g_v, w_s, b_exp, *, name):
    t = proj.shape[0]
    u0 = (3 * CONV_W + 3 * ATTN_W) // SGU_W

    def body(u_ref, v_ref, g_ref, w_ref, b_ref, o_ref):
        grp = lax.broadcasted_iota(jnp.int32, (1, SGU_W), 1) // HEAD_DIM
        u = _gelu(u_ref[...])
        vv = _gelu(v_ref[...])
        vn = (vv * lax.rsqrt(_group_mean(vv * vv, SGU_W) + EPS) * g_ref[...]).astype(BF16)
        wm, _ = _sgu_weights(w_ref)
        sv = b_ref[...]
        for gi in range(SGU_HEADS):
            sv = sv + jnp.dot(wm[gi], jnp.where(grp == gi, vn, jnp.zeros_like(vn)), preferred_element_type=F32)
        o_ref[...] = (u * sv).astype(o_ref.dtype)

    return pl.pallas_call(
        body,
        name=name,
        grid=(t // CHUNK,),
        in_specs=[
            pl.BlockSpec((CHUNK, SGU_W), lambda i: (i, u0)),
            pl.BlockSpec((CHUNK, SGU_W), lambda i: (i, u0 + 1)),
            pl.BlockSpec((1, SGU_W), lambda i: (0, 0)),
            pl.BlockSpec((SGU_HEADS, CHUNK, CHUNK), lambda i: (0, 0, 0)),
            pl.BlockSpec((CHUNK, SGU_W), lambda i: (0, 0)),
        ],
        out_specs=pl.BlockSpec((CHUNK, SGU_W), lambda i: (i, 0)),
        out_shape=jax.ShapeDtypeStruct((t, SGU_W), BF16),
        compiler_params=_params("parallel"),
    )(proj, proj, g_v, w_s, b_exp)


def _sgu_bwd(dmix, proj, g_v, w_s, b_exp, *, name):
    t = proj.shape[0]
    u0 = (3 * CONV_W + 3 * ATTN_W) // SGU_W
    dy0 = (CONV_W + ATTN_W) // SGU_W

    def body(dy_ref, u_ref, v_ref, g_ref, w_ref, b_ref, du_ref, dv_ref, dg_ref, dw_ref, db_ref):
        grp = lax.broadcasted_iota(jnp.int32, (1, SGU_W), 1) // HEAD_DIM
        cu, cv = u_ref[...], v_ref[...]
        u = _gelu(cu)
        vv = _gelu(cv)
        r = lax.rsqrt(_group_mean(vv * vv, SGU_W) + EPS)
        xhat = vv * r
        gain = g_ref[...]
        vn = (xhat * gain).astype(BF16)
        wm, tril = _sgu_weights(w_ref)
        vng = [jnp.where(grp == gi, vn, jnp.zeros_like(vn)) for gi in range(SGU_HEADS)]
        sv = b_ref[...]
        for gi in range(SGU_HEADS):
            sv = sv + jnp.dot(wm[gi], vng[gi], preferred_element_type=F32)
        dy = dy_ref[...]
        du_ref[...] = (dy * sv * _gelu_grad(cu)).astype(du_ref.dtype)
        dsv = dy * u
        dsv16 = dsv.astype(BF16)

        @pl.when(pl.program_id(0) == 0)
        def _():
            dg_ref[...] = jnp.zeros_like(dg_ref)
            dw_ref[...] = jnp.zeros_like(dw_ref)
            db_ref[...] = jnp.zeros_like(db_ref)

        db_ref[...] += dsv
        dvn = jnp.zeros_like(dsv)
        for gi in range(SGU_HEADS):
            dw = lax.dot_general(dsv16, vng[gi], _NT, preferred_element_type=F32)
            dw_ref[gi] += jnp.where(tril, dw, 0.0)
            dvn_g = lax.dot_general(wm[gi], dsv16, _TN, preferred_element_type=F32)
            dvn = jnp.where(grp == gi, dvn_g, dvn)
        dg_ref[...] += jnp.sum(dvn * xhat, axis=0, keepdims=True)
        dxhat = dvn * gain
        dvv = r * (dxhat - xhat * _group_mean(dxhat * xhat, SGU_W))
        dv_ref[...] = (dvv * _gelu_grad(cv)).astype(dv_ref.dtype)

    return pl.pallas_call(
        body,
        name=name,
        grid=(t // CHUNK,),
        in_specs=[
            pl.BlockSpec((CHUNK, SGU_W), lambda i: (i, dy0)),
            pl.BlockSpec((CHUNK, SGU_W), lambda i: (i, u0)),
            pl.BlockSpec((CHUNK, SGU_W), lambda i: (i, u0 + 1)),
            pl.BlockSpec((1, SGU_W), lambda i: (0, 0)),
            pl.BlockSpec((SGU_HEADS, CHUNK, CHUNK), lambda i: (0, 0, 0)),
            pl.BlockSpec((CHUNK, SGU_W), lambda i: (0, 0)),
        ],
        out_specs=[
            pl.BlockSpec((CHUNK, SGU_W), lambda i: (i, 0)),
            pl.BlockSpec((CHUNK, SGU_W), lambda i: (i, 0)),
            pl.BlockSpec((1, SGU_W), lambda i: (0, 0)),
            pl.BlockSpec((SGU_HEADS, CHUNK, CHUNK), lambda i: (0, 0, 0)),
            pl.BlockSpec((CHUNK, SGU_W), lambda i: (0, 0)),
        ],
        out_shape=[
            jax.ShapeDtypeStruct((t, SGU_W), BF16),
            jax.ShapeDtypeStruct((t, SGU_W), BF16),
            jax.ShapeDtypeStruct((1, SGU_W), F32),
            jax.ShapeDtypeStruct((SGU_HEADS, CHUNK, CHUNK), F32),
            jax.ShapeDtypeStruct((CHUNK, SGU_W), F32),
        ],
        compiler_params=_params("arbitrary"),
    )(dmix, proj, proj, g_v, w_s, b_exp)


def _ple_bwd(dh, gate, pp, order_after, *, name, br=512):
    t, d = dh.shape
    br = min(br, t)

    def body(dh_ref, g_ref, p_ref, order_ref, dpre_ref, dpp_ref):
        dhv, g = dh_ref[...], g_ref[...]
        dpre_ref[...] = (dhv * p_ref[...] * g * (1.0 - g)).astype(dpre_ref.dtype)
        dpp_ref[...] = (dhv * g).astype(dpp_ref.dtype)

    row = pl.BlockSpec((br, d), lambda i: (i, 0))
    return pl.pallas_call(
        body,
        name=name,
        grid=(t // br,),
        in_specs=[row] * 3 + [pl.BlockSpec(order_after.shape, lambda i: (0, 0))],
        out_specs=[row] * 2,
        out_shape=[jax.ShapeDtypeStruct((t, d), BF16)] * 2,
        compiler_params=_params("parallel"),
    )(dh, gate, pp, order_after)


def _loss_head(y, target, *, name, br=512):
    t, d = y.shape
    br = min(br, t)

    def body(y_ref, t_ref, dy_ref, loss_ref):
        err = y_ref[...] - t_ref[...]
        dy_ref[...] = err * (1.0 / d)

        @pl.when(pl.program_id(0) == 0)
        def _():
            loss_ref[...] = jnp.zeros_like(loss_ref)

        loss_ref[...] += 0.5 * jnp.sum(jnp.sum(err * err, axis=1, keepdims=True) * (1.0 / d), axis=0, keepdims=True)

    row = pl.BlockSpec((br, d), lambda i: (i, 0))
    return pl.pallas_call(
        body,
        name=name,
        grid=(t // br,),
        in_specs=[row, row],
        out_specs=[row, pl.BlockSpec((8, LANES), lambda i: (0, 0))],
        out_shape=[jax.ShapeDtypeStruct((t, d), F32), jax.ShapeDtypeStruct((8, LANES), F32)],
        compiler_params=_params("arbitrary"),
    )(y, target)


def _adamw_update(w, g, m, v):
    nm = ADAM_B1 * m + (1.0 - ADAM_B1) * g
    nv = ADAM_B2 * v + (1.0 - ADAM_B2) * (g * g)
    m_hat = nm / (1.0 - ADAM_B1 ** ADAM_STEP)
    v_hat = nv / (1.0 - ADAM_B2 ** ADAM_STEP)
    return -ADAM_LR * (m_hat / (jnp.sqrt(v_hat) + ADAM_EPS) + ADAM_WD * w), nm, nv


def _adamw(w, g, m, v, *, name, br=512):
    r, c = w.shape
    br = _row_block(r, br)

    def body(w_ref, g_ref, m_ref, v_ref, d_ref, nm_ref, nv_ref):
        d_ref[...], nm_ref[...], nv_ref[...] = _adamw_update(w_ref[...], g_ref[...], m_ref[...], v_ref[...])

    row = pl.BlockSpec((br, c), lambda i: (i, 0))
    return pl.pallas_call(
        body,
        name=name,
        grid=(r // br,),
        in_specs=[row] * 4,
        out_specs=[row] * 3,
        out_shape=[jax.ShapeDtypeStruct((r, c), F32)] * 3,
        compiler_params=_params("parallel"),
    )(w, g, m, v)


def _sum_slots(x, *, name, br=512):
    n, r, c = x.shape
    br = _row_block(r, br)

    def body(x_ref, o_ref):
        acc = x_ref[0].astype(F32)
        for j in range(1, n):
            acc = acc + x_ref[j].astype(F32)
        o_ref[...] = acc

    return pl.pallas_call(
        body,
        name=name,
        grid=(r // br,),
        in_specs=[pl.BlockSpec((n, br, c), lambda i: (0, i, 0))],
        out_specs=pl.BlockSpec((br, c), lambda i: (i, 0)),
        out_shape=jax.ShapeDtypeStruct((r, c), F32),
        compiler_params=_params("parallel"),
    )(x)


_ADAMW_BLOCK_ELEMS = 128 * 1024


def _adamw_reduce(w, arrived, m, v, *, name):
    depth, r, c = w.shape
    br = _row_block(r, max(BF16_TILE_ROWS, _ADAMW_BLOCK_ELEMS // (-(-c // LANES) * LANES)))

    def body(w_ref, m_ref, v_ref, *rest):
        parts, (g_ref, d_ref, nm_ref, nv_ref) = rest[:depth], rest[depth:]
        for li in range(depth):

            @pl.when(pl.program_id(0) == li)
            def _(li=li):
                g = parts[li][0].astype(F32)
                for j in range(1, N_DEV):
                    g = g + parts[li][j].astype(F32)
                g_ref[...] = g
                d_ref[...], nm_ref[...], nv_ref[...] = _adamw_update(w_ref[...], g, m_ref[...], v_ref[...])

    cur = pl.BlockSpec((None, br, c), lambda l, i: (l, i, 0))
    slots = [pl.BlockSpec((N_DEV, br, c), lambda l, i, li=li: (0, jnp.where(l == li, i, 0), 0)) for li in range(depth)]
    return pl.pallas_call(
        body,
        name=name,
        grid=(depth, r // br),
        in_specs=[cur, cur, cur] + slots,
        out_specs=[cur] * 4,
        out_shape=[jax.ShapeDtypeStruct((depth, r, c), F32)] * 4,
        compiler_params=_params("arbitrary", "arbitrary"),
    )(w, m, v, *arrived)


def _my_place():
    return lax.axis_index("x"), lax.axis_index("y"), lax.axis_index("c")


def _flip(v, bit):
    return 1 - v if bit else v


def _slot_of(px, py, pc):
    return 4 * px + 2 * py + pc


_ANY = pl.BlockSpec(memory_space=pl.ANY)


def _all_gather(x, after, *, name):
    r, c = x.shape

    def body(x_ref, after_ref, out_ref, send_sems, recv_sems, local_sem):
        mx, my, mc = _my_place()
        me, sibling = (mx, my, mc), (mx, my, 1 - mc)
        chips = [(1 - mx, my), (mx, 1 - my), (1 - mx, 1 - my)]

        def copy(k, block, to, src=None):
            return pltpu.make_async_remote_copy(
                src_ref=out_ref.at[_slot_of(*block)] if src is None else src,
                dst_ref=out_ref.at[_slot_of(*block)],
                send_sem=send_sems.at[k],
                recv_sem=recv_sems.at[k],
                device_id=to,
                device_id_type=MESH,
            )

        mine = pltpu.make_async_copy(x_ref, out_ref.at[_slot_of(*me)], local_sem)
        mine.start()
        first = [copy(0, me, sibling, src=x_ref)]
        first += [copy(1 + j, me, (*chip, mc), src=x_ref) for j, chip in enumerate(chips)]
        for cp in first:
            cp.start()
        passed = [copy(4 + j, (*chip, mc), sibling) for j, chip in enumerate(chips)]
        for j, chip in enumerate(chips):
            copy(1 + j, (*chip, mc), me).wait_recv()
            passed[j].start()
        copy(0, sibling, me).wait_recv()
        for j, chip in enumerate(chips):
            copy(4 + j, (*chip, 1 - mc), me).wait_recv()
        for cp in first + passed:
            cp.wait_send()
        mine.wait()

    return pl.pallas_call(
        body,
        name=name,
        in_specs=[_ANY, _ANY],
        out_specs=_ANY,
        out_shape=jax.ShapeDtypeStruct((N_DEV, r, c), x.dtype),
        scratch_shapes=[pltpu.SemaphoreType.DMA((7,)), pltpu.SemaphoreType.DMA((7,)), pltpu.SemaphoreType.DMA(())],
        compiler_params=pltpu.CompilerParams(has_side_effects=True),
    )(x, after)


_HBM = pl.BlockSpec(memory_space=pltpu.HBM)
_SEM = pl.BlockSpec(memory_space=pltpu.SEMAPHORE)
_DATAFLOW = pltpu.SideEffectType.DATAFLOW_SIDE_EFFECTING


def _exchange_copies(src_refs, land_refs, send_sem, recv_sem, scatter):
    mx, my, mc = _my_place()
    mine = _slot_of(mx, my, mc)
    copies = []
    for src, land in zip(src_refs, land_refs, strict=True):
        for k in range(1, N_DEV):
            peer = (_flip(mx, k & 4), _flip(my, k & 2), _flip(mc, k & 1))
            copies.append(pltpu.make_async_remote_copy(
                src_ref=src.at[_slot_of(*peer)] if scatter else src, dst_ref=land.at[mine],
                send_sem=send_sem, recv_sem=recv_sem, device_id=peer, device_id_type=MESH))
    return copies


def _exchange_start(srcs, after, *, scatter, name):
    n = len(srcs)
    lands = [lax.empty(s.shape if scatter else (N_DEV, *s.shape), s.dtype) for s in srcs]

    def body(*refs):
        send_sem, recv_sem = refs[2 * n + 1], refs[2 * n + 2]
        for cp in _exchange_copies(refs[:n], refs[n:2 * n], send_sem, recv_sem, scatter):
            cp.start()
        refs[-1][...] = jnp.zeros_like(refs[-1])

    thru = [pltpu.HBM(a.shape, a.dtype) for a in (*srcs, *lands)]
    out = pl.pallas_call(
        body,
        name=name,
        in_specs=[_HBM] * (2 * n) + [_ANY],
        out_specs=(_SEM, _SEM, *[_HBM] * (2 * n), pl.BlockSpec(memory_space=pltpu.VMEM)),
        out_shape=(pltpu.SemaphoreType.DMA(()), pltpu.SemaphoreType.DMA(()), *thru,
                   jax.ShapeDtypeStruct((8, LANES), F32)),
        input_output_aliases={i: 2 + i for i in range(2 * n)},
        compiler_params=pltpu.CompilerParams(has_side_effects=_DATAFLOW),
    )(*[pltpu.with_memory_space_constraint(a, pltpu.HBM) for a in (*srcs, *lands)], after)
    return out[:-1], out[-1]


def _exchange_wait(started, after, *, scatter, name):
    send_sem, recv_sem, *thru = started
    n = len(thru) // 2

    def body(*refs):
        for cp in _exchange_copies(refs[:n], refs[n:2 * n], refs[2 * n], refs[2 * n + 1], scatter):
            cp.wait_send()
            cp.wait_recv()

    out = pl.pallas_call(
        body,
        name=name,
        in_specs=[_HBM] * (2 * n) + [_SEM, _SEM, _ANY],
        out_specs=[_HBM] * (2 * n),
        out_shape=[pltpu.HBM(a.shape, a.dtype) for a in thru],
        input_output_aliases={i: i for i in range(2 * n)},
        compiler_params=pltpu.CompilerParams(has_side_effects=_DATAFLOW),
    )(*thru, send_sem, recv_sem, after)
    return out[:n], out[n:]


def _with_own_slot(landed, own, me):
    return lax.dynamic_update_slice(landed, own[None], (me,) + (0,) * own.ndim)


_PACK_ROWS = 256


def _pack(arrays, dtype):
    flat = [a.astype(dtype).reshape(-1) for a in arrays]
    total = sum(f.shape[0] for f in flat)
    padded = -(-total // (LANES * _PACK_ROWS)) * (LANES * _PACK_ROWS)
    if padded > total:
        flat.append(jnp.zeros((padded - total,), dtype))
    return jnp.concatenate(flat).reshape(-1, LANES)


def _unpack(packed, shapes, lead=()):
    flat = packed.reshape(*lead, -1)
    out, off = [], 0
    for s in shapes:
        size = math.prod(s)
        out.append(flat[..., off:off + size].reshape(*lead, *s))
        off += size
    return out


def _gather_columns(g):
    return jnp.moveaxis(g, 0, 1).reshape(g.shape[1], -1)


def _split_columns(w):
    return jnp.moveaxis(w.reshape(w.shape[0], N_DEV, -1), 1, 0)


def _split_rows(w):
    return w.reshape(N_DEV, w.shape[0] // N_DEV, w.shape[1])


_FIRST = ("w_in",)
_REST = ("w_out", "w_ff1", "w_ff2", "w_ple_gate", "w_ple_proj")
_BIG = _FIRST + _REST
_COLUMN_SHARDED = ("w_in", "w_ff1", "w_ple_proj", "conv_w")
_SMALL = ("norm1_g", "q_norm_g", "k_norm_g", "sgu_norm_g", "sgu_w", "sgu_b", "norm2_g", "norm3_g")
_ORDER = ("norm1_g", "w_in", "conv_w", "q_norm_g", "k_norm_g", "sgu_norm_g", "sgu_w", "sgu_b", "w_out", "norm2_g",
          "w_ff1", "w_ff2", "norm3_g", "w_ple_gate", "w_ple_proj")


def _whole_matrices(names, landed, own, me):
    out = {}
    for k, g, mine in zip(names, landed, own, strict=True):
        g = _with_own_slot(g, mine, me)
        out[k] = _gather_columns(g) if k in _COLUMN_SHARDED else g.reshape(-1, g.shape[-1])
    return out


def _layer_forward(h0, p16, w, s, li, wait_rest):
    nm = lambda k: f"{k}_l{li}"
    t = h0.shape[0]
    hn1 = _rms_fwd(h0, s["norm1_g"], name=nm("rms1"))
    proj = _matmul(hn1, w["w_in"], name=nm("proj"), bm=t, bn=256)
    y_a = _conv_fwd(proj, w["conv_w"], name=nm("conv"))
    qs, kn, v = _qk_prep(proj, s["gq"], s["gk"], name=nm("qkprep"))
    y_b, lt = _attn_fwd(qs, kn, v, name=nm("attn"))
    y_c = _sgu_fwd(proj, s["sgu_norm_g"], s["sgu_w"], s["b_exp"], name=nm("sgu"))
    mix = jnp.concatenate([y_a, y_b, y_c], axis=1)
    w = dict(w, **wait_rest(y_b))
    h1 = _matmul(mix, w["w_out"], name=nm("out"), bm=t, bn=256, extras=(h0,), epilogue=lambda acc, r: (r + acc,))
    hn2 = _rms_fwd(h1, s["norm2_g"], name=nm("rms2"))
    u, f = _matmul(hn2, w["w_ff1"], name=nm("ff1"), bm=t, bn=512, out_dtypes=(F32, BF16),
                   epilogue=lambda acc: (acc, jnp.square(jnp.maximum(acc, 0.0))))
    h2 = _matmul(f, w["w_ff2"], name=nm("ff2"), bm=512, bn=512, extras=(h1,), epilogue=lambda acc, r: (r + acc,))
    hn3 = _rms_fwd(h2, s["norm3_g"], name=nm("rms3"))
    pp = _matmul(p16, w["w_ple_proj"], name=nm("pleproj"), bm=t, bn=512)

    def gate_epilogue(acc, pp_blk, h_blk):
        gate = jax.nn.sigmoid(acc)
        return h_blk + gate * pp_blk, gate

    h3, gate = _matmul(hn3, w["w_ple_gate"], name=nm("plegate"), bm=t, bn=256, out_dtypes=(F32, F32),
                       extras=(pp, h2), epilogue=gate_epilogue)
    saved = dict(h0=h0, hn1=hn1, proj=proj, qs=qs, kn=kn, v=v, lt=lt, mix=mix, h1=h1, hn2=hn2, u=u, f=f, h2=h2,
                 hn3=hn3, pp=pp, gate=gate, p16=p16)
    return h3, w, saved


def _layer_backward(dh3, a, w, s, li, order_after, start_rest):
    nm = lambda k: f"{k}_bwd_l{li}"
    t = dh3.shape[0]
    dpre, dpp = _ple_bwd(dh3, a["gate"], a["pp"], order_after, name=nm("ple"))
    g_gate = _matmul(a["hn3"], dpre, name=nm("dwgate"), ta=True, bm=512, bn=512, out_dtypes=(BF16,))
    g_proj = _matmul(a["p16"], dpp, name=nm("dwproj"), ta=True, bm=256, out_dtypes=(BF16,), column_shards=True)
    dhn3 = _matmul(dpre, w["w_ple_gate"], name=nm("dhn3"), tb=True, bm=t, bn=256)
    dh2, dh2_16, g_n3 = _rms_bwd(dhn3, a["h2"], s["norm3_g"], dh3, name=nm("rms3"))
    du = _matmul(dh2_16, w["w_ff2"], name=nm("du"), tb=True, bm=t, bn=512, out_dtypes=(BF16,), extras=(a["u"],),
                 epilogue=lambda acc, u: (acc * (2.0 * jnp.maximum(u, 0.0)),))
    g_ff2 = _matmul(a["f"], dh2_16, name=nm("dwff2"), ta=True, bm=512, bn=512, out_dtypes=(BF16,))
    g_ff1 = _matmul(a["hn2"], du, name=nm("dwff1"), ta=True, bm=512, out_dtypes=(BF16,), column_shards=True)
    dhn2 = _matmul(du, w["w_ff1"], name=nm("dhn2"), tb=True, bm=512, bn=512)
    dh1, dh1_16, g_n2 = _rms_bwd(dhn2, a["h1"], s["norm2_g"], dh2, name=nm("rms2"))
    dmix = _matmul(dh1_16, w["w_out"], name=nm("dmix"), tb=True, bm=t, bn=256)
    g_out = _matmul(a["mix"], dh1_16, name=nm("dwout"), ta=True, bm=512, bn=512, out_dtypes=(BF16,))
    started = start_rest(dict(w_out=_split_rows(g_out), w_ff1=g_ff1, w_ff2=_split_rows(g_ff2),
                              w_ple_gate=_split_rows(g_gate), w_ple_proj=g_proj), dmix)
    d_b, d_c, d_h, g_conv = _conv_bwd(dmix, a["proj"], w["conv_w"], name=nm("conv"))
    dqs, dkn, dv = _attn_bwd(dmix, a["qs"], a["kn"], a["v"], a["lt"], started, name=nm("attn"))
    d_q, d_k, d_v, g_q, g_k = _qk_prep_bwd(dqs, dkn, dv, a["proj"], s["gq"], s["gk"], name=nm("qkprep"))
    d_cu, d_cv, g_sn, g_sw, g_sb = _sgu_bwd(dmix, a["proj"], s["sgu_norm_g"], s["sgu_w"], s["b_exp"], name=nm("sgu"))
    dproj = jnp.concatenate([d_b, d_c, d_h, d_q, d_k, d_v, d_cu, d_cv], axis=1)
    g_in = _matmul(a["hn1"], dproj, name=nm("dwin"), ta=True, bm=512, bn=256, out_dtypes=(BF16,))
    dhn1 = _matmul(dproj, w["w_in"], name=nm("dhn1"), tb=True, bm=t, bn=256)
    dh0, _, g_n1 = _rms_bwd(dhn1, a["h0"], s["norm1_g"], dh1, name=nm("rms1"))
    n_tiles = g_q.shape[1] // HEAD_DIM
    small = dict(
        norm1_g=g_n1[0], norm2_g=g_n2[0], norm3_g=g_n3[0],
        q_norm_g=g_q.reshape(n_tiles, HEAD_DIM).sum(0), k_norm_g=g_k.reshape(n_tiles, HEAD_DIM).sum(0),
        sgu_norm_g=g_sn[0], sgu_w=g_sw, sgu_b=g_sb.reshape(CHUNK, SGU_HEADS, HEAD_DIM).sum(-1).T,
        conv_w=g_conv[:CONV_TAPS],
    )
    return dh0, _split_columns(g_in), small


def kernel(x, p, norm1_g, w_in, conv_w, q_norm_g, k_norm_g, sgu_norm_g, sgu_w, sgu_b, w_out, norm2_g, w_ff1, w_ff2, norm3_g, w_ple_gate, w_ple_proj, loss_target, m_norm1_g, m_w_in, m_conv_w, m_q_norm_g, m_k_norm_g, m_sgu_norm_g, m_sgu_w, m_sgu_b, m_w_out, m_norm2_g, m_w_ff1, m_w_ff2, m_norm3_g, m_w_ple_gate, m_w_ple_proj, v_norm1_g, v_w_in, v_conv_w, v_q_norm_g, v_k_norm_g, v_sgu_norm_g, v_sgu_w, v_sgu_b, v_w_out, v_norm2_g, v_w_ff1, v_w_ff2, v_norm3_g, v_w_ple_gate, v_w_ple_proj):
    weights = dict(norm1_g=norm1_g, w_in=w_in, conv_w=conv_w, q_norm_g=q_norm_g, k_norm_g=k_norm_g,
                   sgu_norm_g=sgu_norm_g, sgu_w=sgu_w, sgu_b=sgu_b, w_out=w_out, norm2_g=norm2_g, w_ff1=w_ff1,
                   w_ff2=w_ff2, norm3_g=norm3_g, w_ple_gate=w_ple_gate, w_ple_proj=w_ple_proj)
    mom = dict(norm1_g=m_norm1_g, w_in=m_w_in, conv_w=m_conv_w, q_norm_g=m_q_norm_g, k_norm_g=m_k_norm_g,
               sgu_norm_g=m_sgu_norm_g, sgu_w=m_sgu_w, sgu_b=m_sgu_b, w_out=m_w_out, norm2_g=m_norm2_g, w_ff1=m_w_ff1,
               w_ff2=m_w_ff2, norm3_g=m_norm3_g, w_ple_gate=m_w_ple_gate, w_ple_proj=m_w_ple_proj)
    var = dict(norm1_g=v_norm1_g, w_in=v_w_in, conv_w=v_conv_w, q_norm_g=v_q_norm_g, k_norm_g=v_k_norm_g,
               sgu_norm_g=v_sgu_norm_g, sgu_w=v_sgu_w, sgu_b=v_sgu_b, w_out=v_w_out, norm2_g=v_norm2_g, w_ff1=v_w_ff1,
               w_ff2=v_w_ff2, norm3_g=v_norm3_g, w_ple_gate=v_w_ple_gate, w_ple_proj=v_w_ple_proj)
    depth = norm1_g.shape[0]
    mx, my, mc = _my_place()
    me = _slot_of(mx, my, mc)

    gather_first, gather_rest = [], []
    token = x[0, :8, :LANES]
    for li in range(depth):
        started, token = _exchange_start([w_in[li].astype(BF16), conv_w[li]], token, scatter=False,
                                         name=f"gather_first_start_l{li}")
        gather_first.append(started)
        started, token = _exchange_start([weights[k][li].astype(BF16) for k in _REST], token, scatter=False,
                                         name=f"gather_rest_start_l{li}")
        gather_rest.append(started)

    small = []
    for li in range(depth):
        small.append(dict(
            norm1_g=norm1_g[li][None], norm2_g=norm2_g[li][None], norm3_g=norm3_g[li][None],
            gq=jnp.tile(q_norm_g[li], _QK_BLOCK // HEAD_DIM)[None], gk=jnp.tile(k_norm_g[li], _QK_BLOCK // HEAD_DIM)[None],
            sgu_norm_g=sgu_norm_g[li][None], sgu_w=sgu_w[li], b_exp=jnp.repeat(sgu_b[li].T, HEAD_DIM, axis=1),
        ))

    h = x[0]
    saved, full = [], []
    for li in range(depth):
        own, landed = _exchange_wait(gather_first[li], token if li == 0 else h, scatter=False,
                                     name=f"gather_first_wait_l{li}")
        first = _whole_matrices(("w_in", "conv_w"), landed, own, me)

        def wait_rest(after, li=li):
            own, landed = _exchange_wait(gather_rest[li], after, scatter=False, name=f"gather_rest_wait_l{li}")
            return _whole_matrices(_REST, landed, own, me)

        h, w, acts = _layer_forward(h, p[li, 0].astype(BF16), first, small[li], li, wait_rest)
        full.append(w)
        saved.append(acts)
    dh, loss_tile = _loss_head(h, loss_target[0], name="loss_head")
    loss = lax.psum(loss_tile[0, 0], ("x", "y", "c"))

    small_grads = [None] * depth
    scatter_first, scatter_rest = [None] * depth, [None] * depth
    token = loss_tile
    for li in reversed(range(depth)):

        def start_rest(parts, after, li=li):
            scatter_rest[li], started = _exchange_start([parts[k] for k in _REST], after, scatter=True,
                                                        name=f"scatter_rest_start_l{li}")
            return started

        dh, g_in, small_grads[li] = _layer_backward(dh, saved[li], full[li], small[li], li, token, start_rest)
        scatter_first[li], token = _exchange_start([g_in], dh, scatter=True, name=f"scatter_first_start_l{li}")
    grad_x = dh[None]

    small_names = _SMALL + ("conv_w",)
    small_shapes = [(depth,) + small_grads[0][k].shape for k in small_names]
    partial = _pack([jnp.stack([small_grads[li][k] for li in range(depth)]) for k in small_names], F32)

    grads, delta, new_m, new_v = {}, {}, {}, {}

    def reduce_and_update(names, scatters, after):
        arrived = {k: [None] * depth for k in names}
        for li in reversed(range(depth)):
            sent, landed = _exchange_wait(scatters[li], after, scatter=True, name=f"scatter_{names[0]}_wait_l{li}")
            for k, g, src in zip(names, landed, sent, strict=True):
                arrived[k][li] = _with_own_slot(g, lax.dynamic_index_in_dim(src, me, 0, keepdims=False), me)
        for k in names:
            grads[k], delta[k], new_m[k], new_v[k] = _adamw_reduce(weights[k], arrived[k], mom[k], var[k],
                                                                   name=f"adamw_{k}")

    reduce_and_update(_REST, scatter_rest, token)
    total = _sum_slots(_all_gather(partial, delta[_REST[-1]], name="gather_small_grads"), name="sum_small_grads")
    reduce_and_update(_FIRST, scatter_first, total)
    grads.update(zip(small_names, _unpack(total, small_shapes)))
    n_conv = conv_w.shape[2]
    grads["conv_w"] = lax.dynamic_slice_in_dim(grads["conv_w"], me * n_conv, n_conv, axis=2)
    rest = _SMALL + ("conv_w",)
    rest_shapes = [weights[k].shape for k in rest]
    packs = [_pack([src[k] for k in rest], F32) for src in (weights, grads, mom, var)]
    for out, packed in zip((delta, new_m, new_v), _adamw(*packs, name="adamw_small")):
        out.update(zip(rest, _unpack(packed, rest_shapes)))

    return (loss, grad_x, *[grads[k] for k in _ORDER], *[delta[k] for k in _ORDER],
            *[new_m[k] for k in _ORDER], *[new_v[k] for k in _ORDER])
```

```python
import math

import jax
import jax.numpy as jnp
from jax import lax
from jax.experimental import pallas as pl
from jax.experimental.pallas import tpu as pltpu

F32 = jnp.float32
BF16 = jnp.bfloat16

N_DEV = 8
HEAD_DIM = 64
CONV_W = 256
ATTN_W = 512
SGU_W = 256
SGU_HEADS = 4
CHUNK = 128
CONV_TAPS = 3
EPS = 1e-6
QK_SCALE = HEAD_DIM ** -0.5

ADAM_LR = 0.001
ADAM_B1 = 0.9
ADAM_B2 = 0.999
ADAM_EPS = 1e-08
ADAM_WD = 0.01
ADAM_STEP = 10

LANES = 128
BF16_TILE_ROWS = 16
VMEM_LIMIT_BYTES = 56 * 1024 * 1024
MESH = pl.DeviceIdType.MESH


def _params(*sem):
    return pltpu.CompilerParams(dimension_semantics=sem, vmem_limit_bytes=VMEM_LIMIT_BYTES)


def _row_block(rows, cap):
    if rows <= cap:
        return rows
    return max(b for b in range(BF16_TILE_ROWS, cap + 1, BF16_TILE_ROWS) if rows % b == 0)


def _matmul(a, b, *, name, tb=False, bm=512, bn=512, out_dtypes=(F32,), epilogue=None, extras=()):
    m, k = a.shape
    n = b.shape[0] if tb else b.shape[1]
    assert k == (b.shape[1] if tb else b.shape[0])
    bm, bn = min(bm, m), min(bn, n)
    assert m % bm == 0 and n % bn == 0
    a_spec = pl.BlockSpec((bm, k), lambda i, j: (i, 0))
    b_spec = pl.BlockSpec((bn, k), lambda i, j: (j, 0)) if tb else pl.BlockSpec((k, bn), lambda i, j: (0, j))
    dims = (((1,), (1 if tb else 0,)), ((), ()))
    n_ex = len(extras)
    for e in extras:
        assert e.shape == (m, n), (e.shape, m, n)

    def body(a_ref, b_ref, *rest):
        outs = rest[n_ex:]
        acc = lax.dot_general(a_ref[...], b_ref[...], dims, preferred_element_type=F32)
        res = (acc,) if epilogue is None else epilogue(acc, *[e[...] for e in rest[:n_ex]])
        for o_ref, r in zip(outs, res, strict=True):
            o_ref[...] = r.astype(o_ref.dtype)

    tile = pl.BlockSpec((bm, bn), lambda i, j: (i, j))
    out = pl.pallas_call(
        body,
        name=name,
        grid=(m // bm, n // bn),
        in_specs=[a_spec, b_spec] + [tile] * n_ex,
        out_specs=[tile] * len(out_dtypes),
        out_shape=[jax.ShapeDtypeStruct((m, n), d) for d in out_dtypes],
        compiler_params=_params("parallel", "parallel"),
    )(a, b, *extras)
    return out[0] if len(out_dtypes) == 1 else out


_WEIGHT_GRAD_ACC_ELEMS = 1024 * 1024


def _weight_grad(x, dy, *, name, column_shards=False):
    t, m = x.shape
    n = dy.shape[1]
    bm = m if m <= 2 * LANES else min(m // 2, max(LANES, _WEIGHT_GRAD_ACC_ELEMS // n // LANES * LANES))
    assert m % bm == 0
    ns = n // N_DEV

    def body(x_ref, dy_ref, o_ref):
        acc = lax.dot_general(x_ref[...], dy_ref[...], _TN, preferred_element_type=F32)
        if column_shards:
            for s in range(N_DEV):
                o_ref[s] = acc[:, s * ns:(s + 1) * ns].astype(o_ref.dtype)
        else:
            o_ref[...] = acc.astype(o_ref.dtype)

    if column_shards:
        out_spec, out_dims = pl.BlockSpec((N_DEV, bm, ns), lambda i: (0, i, 0)), (N_DEV, m, ns)
    else:
        out_spec, out_dims = pl.BlockSpec((bm, n), lambda i: (i, 0)), (m, n)
    return pl.pallas_call(
        body,
        name=name,
        grid=(m // bm,),
        in_specs=[pl.BlockSpec((t, bm), lambda i: (0, i)), pl.BlockSpec((t, n), lambda i: (0, 0))],
        out_specs=out_spec,
        out_shape=jax.ShapeDtypeStruct(out_dims, BF16),
        compiler_params=_params("parallel"),
    )(x, dy)


def _rms_fwd(h, g, *, name, br=512):
    t, d = h.shape
    br = min(br, t)

    def body(h_ref, g_ref, o_ref):
        x = h_ref[...]
        r = lax.rsqrt(jnp.mean(x * x, axis=-1, keepdims=True) + EPS)
        o_ref[...] = (x * r * g_ref[...]).astype(o_ref.dtype)

    return pl.pallas_call(
        body,
        name=name,
        grid=(t // br,),
        in_specs=[pl.BlockSpec((br, d), lambda i: (i, 0)), pl.BlockSpec((1, d), lambda i: (0, 0))],
        out_specs=pl.BlockSpec((br, d), lambda i: (i, 0)),
        out_shape=jax.ShapeDtypeStruct((t, d), BF16),
        compiler_params=_params("parallel"),
    )(h, g)


def _matmul_rms_bwd(dz, w, h, g, dres, *, name):
    t, d = h.shape
    k = dz.shape[1]
    br = min(t, 512 if k <= d else 256)

    def body(dz_ref, w_ref, h_ref, g_ref, dres_ref, dh_ref, dh16_ref, dg_ref):
        x = h_ref[...]
        dyv = lax.dot_general(dz_ref[...], w_ref[...], _NT, preferred_element_type=F32)
        r = lax.rsqrt(jnp.mean(x * x, axis=-1, keepdims=True) + EPS)
        xhat = x * r
        dxhat = dyv * g_ref[...]
        dh = dres_ref[...] + r * (dxhat - xhat * jnp.mean(dxhat * xhat, axis=-1, keepdims=True))
        dh_ref[...] = dh
        dh16_ref[...] = dh.astype(dh16_ref.dtype)

        @pl.when(pl.program_id(0) == 0)
        def _():
            dg_ref[...] = jnp.zeros_like(dg_ref)

        dg_ref[...] += jnp.sum(dyv * xhat, axis=0, keepdims=True)

    row = pl.BlockSpec((br, d), lambda i: (i, 0))
    vec = pl.BlockSpec((1, d), lambda i: (0, 0))
    return pl.pallas_call(
        body,
        name=name,
        grid=(t // br,),
        in_specs=[pl.BlockSpec((br, k), lambda i: (i, 0)), pl.BlockSpec((d, k), lambda i: (0, 0)), row, vec, row],
        out_specs=[row, row, vec],
        out_shape=[jax.ShapeDtypeStruct((t, d), F32), jax.ShapeDtypeStruct((t, d), BF16),
                   jax.ShapeDtypeStruct((1, d), F32)],
        compiler_params=_params("arbitrary"),
    )(dz, w, h, g, dres)


def _group_mean(x, width):
    grp = lax.broadcasted_iota(jnp.int32, x.shape, 1) // HEAD_DIM
    out = jnp.zeros_like(x)
    for gi in range(width // HEAD_DIM):
        m = grp == gi
        s = jnp.sum(jnp.where(m, x, 0.0), axis=1, keepdims=True)
        out = jnp.where(m, s, out)
    return out * (1.0 / HEAD_DIM)


def _gelu(x):
    return 0.5 * x * (1.0 + lax.erf(x * (2.0 ** -0.5)))


def _gelu_grad(x):
    cdf = 0.5 * (1.0 + lax.erf(x * (2.0 ** -0.5)))
    pdf = jnp.exp(-0.5 * x * x) * (1.0 / math.sqrt(2.0 * math.pi))
    return cdf + x * pdf


def _shift_down(z, s, row):
    return jnp.where(row >= s, pltpu.roll(z, s, 0), 0.0)


def _shift_up(z, s, row, t):
    return jnp.where(row < t - s, pltpu.roll(z, t - s, 0), 0.0)


def _conv_fwd(proj, conv_w, *, name):
    t = proj.shape[0]
    nb = CONV_W // LANES

    def body(b_ref, c_ref, h_ref, w_ref, o_ref):
        row = lax.broadcasted_iota(jnp.int32, (t, LANES), 0)
        z = c_ref[...] * h_ref[...]
        w = w_ref[...]
        conv = w[2:3, :] * z + w[1:2, :] * _shift_down(z, 1, row) + w[0:1, :] * _shift_down(z, 2, row)
        o_ref[...] = (b_ref[...] * conv).astype(o_ref.dtype)

    return pl.pallas_call(
        body,
        name=name,
        grid=(nb,),
        in_specs=[
            pl.BlockSpec((t, LANES), lambda j: (0, j)),
            pl.BlockSpec((t, LANES), lambda j: (0, nb + j)),
            pl.BlockSpec((t, LANES), lambda j: (0, 2 * nb + j)),
            pl.BlockSpec((CONV_TAPS, LANES), lambda j: (0, j)),
        ],
        out_specs=pl.BlockSpec((t, LANES), lambda j: (0, j)),
        out_shape=jax.ShapeDtypeStruct((t, CONV_W), BF16),
        compiler_params=_params("parallel"),
    )(proj, proj, proj, conv_w)


def _conv_bwd(dmix, proj, conv_w, *, name):
    t = proj.shape[0]
    nb = CONV_W // LANES

    def body(dy_ref, b_ref, c_ref, h_ref, w_ref, db_ref, dc_ref, dh_ref, dw_ref):
        row = lax.broadcasted_iota(jnp.int32, (t, LANES), 0)
        ac, ah = c_ref[...], h_ref[...]
        z = ac * ah
        w = w_ref[...]
        z1 = _shift_down(z, 1, row)
        z2 = _shift_down(z, 2, row)
        conv = w[2:3, :] * z + w[1:2, :] * z1 + w[0:1, :] * z2
        dy = dy_ref[...]
        db_ref[...] = (dy * conv).astype(db_ref.dtype)
        dconv = dy * b_ref[...]
        dz = w[2:3, :] * dconv + w[1:2, :] * _shift_up(dconv, 1, row, t) + w[0:1, :] * _shift_up(dconv, 2, row, t)
        dc_ref[...] = (dz * ah).astype(dc_ref.dtype)
        dh_ref[...] = (dz * ac).astype(dh_ref.dtype)
        dw_ref[...] = jnp.zeros_like(dw_ref)
        dw_ref[0:1, :] = jnp.sum(dconv * z2, axis=0, keepdims=True)
        dw_ref[1:2, :] = jnp.sum(dconv * z1, axis=0, keepdims=True)
        dw_ref[2:3, :] = jnp.sum(dconv * z, axis=0, keepdims=True)

    col = lambda off: pl.BlockSpec((t, LANES), lambda j: (0, off + j))
    return pl.pallas_call(
        body,
        name=name,
        grid=(nb,),
        in_specs=[col(0), col(0), col(nb), col(2 * nb), pl.BlockSpec((CONV_TAPS, LANES), lambda j: (0, j))],
        out_specs=[col(0), col(0), col(0), pl.BlockSpec((8, LANES), lambda j: (0, j))],
        out_shape=[jax.ShapeDtypeStruct((t, CONV_W), BF16)] * 3 + [jax.ShapeDtypeStruct((8, CONV_W), F32)],
        compiler_params=_params("parallel"),
    )(dmix, proj, proj, proj, conv_w)


_QK_BLOCK = 256


def _qk_prep(proj, gq, gk, *, name, br=512):
    t = proj.shape[0]
    br = min(br, t)
    nb = ATTN_W // _QK_BLOCK
    q0 = (3 * CONV_W) // _QK_BLOCK

    def body(q_ref, k_ref, v_ref, gq_ref, gk_ref, qo_ref, ko_ref, vo_ref):
        q = q_ref[...]
        k = k_ref[...]
        rq = lax.rsqrt(_group_mean(q * q, _QK_BLOCK) + EPS)
        rk = lax.rsqrt(_group_mean(k * k, _QK_BLOCK) + EPS)
        qo_ref[...] = ((q * rq * gq_ref[...]).astype(BF16) * QK_SCALE).astype(qo_ref.dtype)
        ko_ref[...] = (k * rk * gk_ref[...]).astype(ko_ref.dtype)
        vo_ref[...] = v_ref[...].astype(vo_ref.dtype)

    col = lambda off: pl.BlockSpec((br, _QK_BLOCK), lambda i, j: (i, off + j))
    vec = pl.BlockSpec((1, _QK_BLOCK), lambda i, j: (0, 0))
    return pl.pallas_call(
        body,
        name=name,
        grid=(t // br, nb),
        in_specs=[col(q0), col(q0 + nb), col(q0 + 2 * nb), vec, vec],
        out_specs=[col(0)] * 3,
        out_shape=[jax.ShapeDtypeStruct((t, ATTN_W), BF16)] * 3,
        compiler_params=_params("parallel", "parallel"),
    )(proj, proj, proj, gq, gk)


def _qk_prep_bwd(dqs, dkn, dv, proj, gq, gk, *, name, br=512):
    t = proj.shape[0]
    br = min(br, t)
    nb = ATTN_W // _QK_BLOCK
    q0 = (3 * CONV_W) // _QK_BLOCK

    def norm_bwd(dy, x, g):
        r = lax.rsqrt(_group_mean(x * x, _QK_BLOCK) + EPS)
        xhat = x * r
        dxhat = dy * g
        dx = r * (dxhat - xhat * _group_mean(dxhat * xhat, _QK_BLOCK))
        return dx, jnp.sum(dy * xhat, axis=0, keepdims=True)

    def body(dq_ref, dk_ref, dv_ref, q_ref, k_ref, gq_ref, gk_ref, oq_ref, ok_ref, ov_ref, dgq_ref, dgk_ref):
        dq, dgq = norm_bwd(dq_ref[...] * QK_SCALE, q_ref[...], gq_ref[...])
        dk, dgk = norm_bwd(dk_ref[...], k_ref[...], gk_ref[...])
        oq_ref[...] = dq.astype(oq_ref.dtype)
        ok_ref[...] = dk.astype(ok_ref.dtype)
        ov_ref[...] = dv_ref[...].astype(ov_ref.dtype)

        @pl.when((pl.program_id(0) == 0) & (pl.program_id(1) == 0))
        def _():
            dgq_ref[...] = jnp.zeros_like(dgq_ref)
            dgk_ref[...] = jnp.zeros_like(dgk_ref)

        dgq_ref[...] += dgq
        dgk_ref[...] += dgk

    col = lambda off: pl.BlockSpec((br, _QK_BLOCK), lambda i, j: (i, off + j))
    vec = pl.BlockSpec((1, _QK_BLOCK), lambda i, j: (0, 0))
    return pl.pallas_call(
        body,
        name=name,
        grid=(t // br, nb),
        in_specs=[col(0), col(0), col(0), col(q0), col(q0 + nb), vec, vec],
        out_specs=[col(0), col(0), col(0), vec, vec],
        out_shape=[jax.ShapeDtypeStruct((t, ATTN_W), BF16)] * 3 + [jax.ShapeDtypeStruct((1, _QK_BLOCK), F32)] * 2,
        compiler_params=_params("arbitrary", "arbitrary"),
    )(dqs, dkn, dv, proj, proj, gq, gk)


def _split_bf16(x):
    hi = x.astype(BF16)
    return jnp.concatenate([hi, (x - hi.astype(F32)).astype(BF16)], axis=1)


def _key_order_matrix(tb, relation):
    jj = lax.broadcasted_iota(jnp.int32, (tb, tb), 0)
    ss = lax.broadcasted_iota(jnp.int32, (tb, tb), 1)
    m = relation(jj, ss).astype(BF16)
    return jnp.concatenate([m, m], axis=0)


def _log_sigmoids(z):
    lb = jnp.minimum(z, 0.0) - jnp.log(1.0 + jnp.exp(-jnp.abs(z)))
    return lb, lb - z


def _below_diagonal(tb):
    return lax.broadcasted_iota(jnp.int32, (tb, tb), 1) < lax.broadcasted_iota(jnp.int32, (tb, tb), 0)


_NT = (((1,), (1,)), ((), ()))
_TN = (((0,), (0,)), ((), ()))
_ATTN_BLOCK = 256
_ATTN_UNROLL = 2


def _attn_fwd(qs, kn, v, *, name, tb=_ATTN_BLOCK, unroll=_ATTN_UNROLL):
    t = qs.shape[0]
    tb = min(tb, t)
    assert t % tb == 0
    n_pairs = ATTN_W // LANES

    def body(q_ref, k_ref, v_ref, o_ref, lt_ref, acc_ref, carry_ref):
        qb = pl.program_id(1)
        half = lax.broadcasted_iota(jnp.int32, (1, LANES), 1) // HEAD_DIM
        later = _key_order_matrix(tb, lambda j, s: j > s)
        acc_ref[...] = jnp.zeros_like(acc_ref)
        carry_ref[...] = jnp.zeros_like(carry_ref)
        q = q_ref[...]
        qh = [jnp.where(half == h, q, jnp.zeros_like(q)) for h in range(2)]

        def tiles(kbs, diagonal):
            blk = []
            for kb in kbs:
                start = pl.multiple_of(kb * tb, tb)
                blk.append((k_ref[pl.ds(start, tb), :], v_ref[pl.ds(start, tb), :]))
            chains = [(h, j) for j in range(len(kbs)) for h in range(2)]
            z = [lax.dot_general(qh[h], blk[j][0], _NT, preferred_element_type=F32) for h, j in chains]
            causal = _below_diagonal(tb) if diagonal else None
            lb, lr = [], []
            for zi in z:
                b, r = _log_sigmoids(zi)
                lb.append(b)
                lr.append(jnp.where(causal, r, 0.0) if diagonal else r)
            suffix = [jnp.dot(_split_bf16(r), later, preferred_element_type=F32) for r in lr]
            carry = [carry_ref[0], carry_ref[1]]
            w = []
            for i, (h, j) in enumerate(chains):
                wi = jnp.exp(lb[i] + (suffix[i] + carry[h][:, 0:1]))
                w.append((jnp.where(causal, wi, 0.0) if diagonal else wi).astype(BF16))
                carry[h] = carry[h] + jnp.sum(lr[i], axis=1, keepdims=True)
            for i, (h, j) in enumerate(chains):
                vh = jnp.where(half == h, blk[j][1], jnp.zeros_like(blk[j][1]))
                acc_ref[h] += jnp.dot(w[i], vh, preferred_element_type=F32)
            carry_ref[0] = carry[0]
            carry_ref[1] = carry[1]

        tiles([qb], True)

        def step(i, _):
            kb = qb - 1 - unroll * i
            tiles([kb - u for u in range(unroll)], False)
            return 0

        lax.fori_loop(0, qb // unroll, step, 0)
        for left in range(1, unroll):

            @pl.when(qb % unroll == left)
            def _(left=left):
                tiles([left - 1 - u for u in range(left)], False)

        o_ref[...] = (acc_ref[0] + acc_ref[1]).astype(o_ref.dtype)
        lt_ref[...] = jnp.where(half == 0, carry_ref[0], carry_ref[1])

    return pl.pallas_call(
        body,
        name=name,
        grid=(n_pairs, t // tb),
        in_specs=[
            pl.BlockSpec((tb, LANES), lambda p, i: (i, p)),
            pl.BlockSpec((t, LANES), lambda p, i: (0, p)),
            pl.BlockSpec((t, LANES), lambda p, i: (0, p)),
        ],
        out_specs=[pl.BlockSpec((tb, LANES), lambda p, i: (i, p))] * 2,
        out_shape=[jax.ShapeDtypeStruct((t, ATTN_W), BF16), jax.ShapeDtypeStruct((t, ATTN_W), F32)],
        scratch_shapes=[pltpu.VMEM((2, tb, LANES), F32), pltpu.VMEM((2, tb, LANES), F32)],
        compiler_params=_params("parallel", "parallel"),
    )(qs, kn, v)


def _attn_bwd(dmix, qs, kn, v, lt, order_after, *, name, tb=_ATTN_BLOCK, unroll=_ATTN_UNROLL):
    t = qs.shape[0]
    tb = min(tb, t)
    assert t % tb == 0
    n_pairs = ATTN_W // LANES
    dy0 = CONV_W // LANES

    def body(do_ref, q_ref, k_ref, v_ref, lt_ref, order_ref, dq_ref, dk_ref, dv_ref, dqacc_ref, cc_ref, cg_ref):
        qb = pl.program_id(1)
        half = lax.broadcasted_iota(jnp.int32, (1, LANES), 1) // HEAD_DIM
        lane = lax.broadcasted_iota(jnp.int32, (tb, LANES), 1)
        upto = _key_order_matrix(tb, lambda j, s: j <= s)
        before = _key_order_matrix(tb, lambda j, s: j < s)
        q = q_ref[...]
        do = do_ref[...].astype(BF16)
        lt = lt_ref[...]
        qh = [jnp.where(half == h, q, jnp.zeros_like(q)) for h in range(2)]
        doh = [jnp.where(half == h, do, jnp.zeros_like(do)) for h in range(2)]
        lth = [jnp.sum(jnp.where(lane == h * HEAD_DIM, lt, 0.0), axis=1, keepdims=True) for h in range(2)]

        @pl.when(qb == 0)
        def _():
            dk_ref[...] = jnp.zeros_like(dk_ref)
            dv_ref[...] = jnp.zeros_like(dv_ref)

        dqacc_ref[...] = jnp.zeros_like(dqacc_ref)
        cc_ref[...] = jnp.zeros_like(cc_ref)
        cg_ref[...] = jnp.zeros_like(cg_ref)

        def tiles(kbs, diagonal):
            starts = [pl.multiple_of(kb * tb, tb) for kb in kbs]
            blk = [(k_ref[pl.ds(s, tb), :], v_ref[pl.ds(s, tb), :]) for s in starts]
            chains = [(h, j) for j in range(len(kbs)) for h in range(2)]
            z = [lax.dot_general(qh[h], blk[j][0], _NT, preferred_element_type=F32) for h, j in chains]
            da = [lax.dot_general(doh[h], jnp.where(half == h, blk[j][1], jnp.zeros_like(blk[j][1])), _NT,
                                  preferred_element_type=F32) for h, j in chains]
            causal = _below_diagonal(tb) if diagonal else None
            lb, lr = [], []
            for zi in z:
                b, r = _log_sigmoids(zi)
                lb.append(b)
                lr.append(jnp.where(causal, r, 0.0) if diagonal else r)
            prefix = [jnp.dot(_split_bf16(r), upto, preferred_element_type=F32) for r in lr]
            cc = [cc_ref[0], cc_ref[1]]
            cg = [cg_ref[0], cg_ref[1]]
            a16, g = [], []
            for i, (h, j) in enumerate(chains):
                a = jnp.exp(lb[i] - prefix[i] + (lth[h] - cc[h][:, 0:1]))
                if diagonal:
                    a = jnp.where(causal, a, 0.0)
                a16.append(a.astype(BF16))
                g.append(da[i] * a)
                cc[h] = cc[h] + jnp.sum(lr[i], axis=1, keepdims=True)
            g_before = [jnp.dot(_split_bf16(gi), before, preferred_element_type=F32) for gi in g]
            dz = []
            for i, (h, j) in enumerate(chains):
                dzi = g[i] - jnp.exp(lb[i]) * (g[i] + (g_before[i] + cg[h][:, 0:1]))
                dz.append((jnp.where(causal, dzi, 0.0) if diagonal else dzi).astype(BF16))
                cg[h] = cg[h] + jnp.sum(g[i], axis=1, keepdims=True)
            for i, (h, j) in enumerate(chains):
                kh = jnp.where(half == h, blk[j][0], jnp.zeros_like(blk[j][0]))
                dqacc_ref[h] += jnp.dot(dz[i], kh, preferred_element_type=F32)
                dk_ref[pl.ds(starts[j], tb), :] += lax.dot_general(dz[i], qh[h], _TN, preferred_element_type=F32)
                dv_ref[pl.ds(starts[j], tb), :] += lax.dot_general(a16[i], doh[h], _TN, preferred_element_type=F32)
            for h in range(2):
                cc_ref[h] = cc[h]
                cg_ref[h] = cg[h]

        def step(i, _):
            kb = unroll * i
            tiles([kb + u for u in range(unroll)], False)
            return 0

        lax.fori_loop(0, qb // unroll, step, 0)
        for left in range(1, unroll):

            @pl.when(qb % unroll == left)
            def _(left=left):
                tiles([qb - left + u for u in range(left)], False)

        tiles([qb], True)
        dq_ref[...] = dqacc_ref[0] + dqacc_ref[1]

    qblk = pl.BlockSpec((tb, LANES), lambda p, i: (i, p))
    whole = pl.BlockSpec((t, LANES), lambda p, i: (0, p))
    return pl.pallas_call(
        body,
        name=name,
        grid=(n_pairs, t // tb),
        in_specs=[pl.BlockSpec((tb, LANES), lambda p, i: (i, dy0 + p)), qblk, whole, whole, qblk,
                  pl.BlockSpec(order_after.shape, lambda p, i: (0, 0))],
        out_specs=[qblk, whole, whole],
        out_shape=[jax.ShapeDtypeStruct((t, ATTN_W), F32)] * 3,
        scratch_shapes=[pltpu.VMEM((2, tb, LANES), F32)] * 3,
        compiler_params=_params("parallel", "arbitrary"),
    )(dmix, qs, kn, v, lt, order_after)


def _sgu_weights(w_ref):
    tt = lax.broadcasted_iota(jnp.int32, (CHUNK, CHUNK), 0)
    ss = lax.broadcasted_iota(jnp.int32, (CHUNK, CHUNK), 1)
    tril = ss <= tt
    return [jnp.where(tril, w_ref[gi], 0.0).astype(BF16) for gi in range(SGU_HEADS)], tril


def _sgu_fwd(proj, g_v, w_s, b_exp, *, name):
    t = proj.shape[0]
    u0 = (3 * CONV_W + 3 * ATTN_W) // SGU_W

    def body(u_ref, v_ref, g_ref, w_ref, b_ref, o_ref):
        grp = lax.broadcasted_iota(jnp.int32, (1, SGU_W), 1) // HEAD_DIM
        u = _gelu(u_ref[...])
        vv = _gelu(v_ref[...])
        vn = (vv * lax.rsqrt(_group_mean(vv * vv, SGU_W) + EPS) * g_ref[...]).astype(BF16)
        wm, _ = _sgu_weights(w_ref)
        sv = b_ref[...]
        for gi in range(SGU_HEADS):
            sv = sv + jnp.dot(wm[gi], jnp.where(grp == gi, vn, jnp.zeros_like(vn)), preferred_element_type=F32)
        o_ref[...] = (u * sv).astype(o_ref.dtype)

    return pl.pallas_call(
        body,
        name=name,
        grid=(t // CHUNK,),
        in_specs=[
            pl.BlockSpec((CHUNK, SGU_W), lambda i: (i, u0)),
            pl.BlockSpec((CHUNK, SGU_W), lambda i: (i, u0 + 1)),
            pl.BlockSpec((1, SGU_W), lambda i: (0, 0)),
            pl.BlockSpec((SGU_HEADS, CHUNK, CHUNK), lambda i: (0, 0, 0)),
            pl.BlockSpec((CHUNK, SGU_W), lambda i: (0, 0)),
        ],
        out_specs=pl.BlockSpec((CHUNK, SGU_W), lambda i: (i, 0)),
        out_shape=jax.ShapeDtypeStruct((t, SGU_W), BF16),
        compiler_params=_params("parallel"),
    )(proj, proj, g_v, w_s, b_exp)


def _sgu_bwd(dmix, proj, g_v, w_s, b_exp, *, name):
    t = proj.shape[0]
    u0 = (3 * CONV_W + 3 * ATTN_W) // SGU_W
    dy0 = (CONV_W + ATTN_W) // SGU_W

    def body(dy_ref, u_ref, v_ref, g_ref, w_ref, b_ref, du_ref, dv_ref, dg_ref, dw_ref, db_ref):
        grp = lax.broadcasted_iota(jnp.int32, (1, SGU_W), 1) // HEAD_DIM
        cu, cv = u_ref[...], v_ref[...]
        u = _gelu(cu)
        vv = _gelu(cv)
        r = lax.rsqrt(_group_mean(vv * vv, SGU_W) + EPS)
        xhat = vv * r
        gain = g_ref[...]
        vn = (xhat * gain).astype(BF16)
        wm, tril = _sgu_weights(w_ref)
        vng = [jnp.where(grp == gi, vn, jnp.zeros_like(vn)) for gi in range(SGU_HEADS)]
        sv = b_ref[...]
        for gi in range(SGU_HEADS):
            sv = sv + jnp.dot(wm[gi], vng[gi], preferred_element_type=F32)
        dy = dy_ref[...]
        du_ref[...] = (dy * sv * _gelu_grad(cu)).astype(du_ref.dtype)
        dsv = dy * u
        dsv16 = dsv.astype(BF16)

        @pl.when(pl.program_id(0) == 0)
        def _():
            dg_ref[...] = jnp.zeros_like(dg_ref)
            dw_ref[...] = jnp.zeros_like(dw_ref)
            db_ref[...] = jnp.zeros_like(db_ref)

        db_ref[...] += dsv
        dvn = jnp.zeros_like(dsv)
        for gi in range(SGU_HEADS):
            dw = lax.dot_general(dsv16, vng[gi], _NT, preferred_element_type=F32)
            dw_ref[gi] += jnp.where(tril, dw, 0.0)
            dvn_g = lax.dot_general(wm[gi], dsv16, _TN, preferred_element_type=F32)
            dvn = jnp.where(grp == gi, dvn_g, dvn)
        dg_ref[...] += jnp.sum(dvn * xhat, axis=0, keepdims=True)
        dxhat = dvn * gain
        dvv = r * (dxhat - xhat * _group_mean(dxhat * xhat, SGU_W))
        dv_ref[...] = (dvv * _gelu_grad(cv)).astype(dv_ref.dtype)

    return pl.pallas_call(
        body,
        name=name,
        grid=(t // CHUNK,),
        in_specs=[
            pl.BlockSpec((CHUNK, SGU_W), lambda i: (i, dy0)),
            pl.BlockSpec((CHUNK, SGU_W), lambda i: (i, u0)),
            pl.BlockSpec((CHUNK, SGU_W), lambda i: (i, u0 + 1)),
            pl.BlockSpec((1, SGU_W), lambda i: (0, 0)),
            pl.BlockSpec((SGU_HEADS, CHUNK, CHUNK), lambda i: (0, 0, 0)),
            pl.BlockSpec((CHUNK, SGU_W), lambda i: (0, 0)),
        ],
        out_specs=[
            pl.BlockSpec((CHUNK, SGU_W), lambda i: (i, 0)),
            pl.BlockSpec((CHUNK, SGU_W), lambda i: (i, 0)),
            pl.BlockSpec((1, SGU_W), lambda i: (0, 0)),
            pl.BlockSpec((SGU_HEADS, CHUNK, CHUNK), lambda i: (0, 0, 0)),
            pl.BlockSpec((CHUNK, SGU_W), lambda i: (0, 0)),
        ],
        out_shape=[
            jax.ShapeDtypeStruct((t, SGU_W), BF16),
            jax.ShapeDtypeStruct((t, SGU_W), BF16),
            jax.ShapeDtypeStruct((1, SGU_W), F32),
            jax.ShapeDtypeStruct((SGU_HEADS, CHUNK, CHUNK), F32),
            jax.ShapeDtypeStruct((CHUNK, SGU_W), F32),
        ],
        compiler_params=_params("arbitrary"),
    )(dmix, proj, proj, g_v, w_s, b_exp)


def _ple_bwd(dh, gate, pp, order_after, *, name, br=512):
    t, d = dh.shape
    br = min(br, t)

    def body(dh_ref, g_ref, p_ref, order_ref, dpre_ref, dpp_ref):
        dhv, g = dh_ref[...], g_ref[...]
        dpre_ref[...] = (dhv * p_ref[...] * g * (1.0 - g)).astype(dpre_ref.dtype)
        dpp_ref[...] = (dhv * g).astype(dpp_ref.dtype)

    row = pl.BlockSpec((br, d), lambda i: (i, 0))
    return pl.pallas_call(
        body,
        name=name,
        grid=(t // br,),
        in_specs=[row] * 3 + [pl.BlockSpec(order_after.shape, lambda i: (0, 0))],
        out_specs=[row] * 2,
        out_shape=[jax.ShapeDtypeStruct((t, d), BF16)] * 2,
        compiler_params=_params("parallel"),
    )(dh, gate, pp, order_after)


def _loss_head(y, target, *, name, br=512):
    t, d = y.shape
    br = min(br, t)

    def body(y_ref, t_ref, dy_ref, loss_ref):
        err = y_ref[...] - t_ref[...]
        dy_ref[...] = err * (1.0 / d)

        @pl.when(pl.program_id(0) == 0)
        def _():
            loss_ref[...] = jnp.zeros_like(loss_ref)

        loss_ref[...] += 0.5 * jnp.sum(jnp.sum(err * err, axis=1, keepdims=True) * (1.0 / d), axis=0, keepdims=True)

    row = pl.BlockSpec((br, d), lambda i: (i, 0))
    return pl.pallas_call(
        body,
        name=name,
        grid=(t // br,),
        in_specs=[row, row],
        out_specs=[row, pl.BlockSpec((8, LANES), lambda i: (0, 0))],
        out_shape=[jax.ShapeDtypeStruct((t, d), F32), jax.ShapeDtypeStruct((8, LANES), F32)],
        compiler_params=_params("arbitrary"),
    )(y, target)


def _adamw_update(w, g, m, v):
    nm = ADAM_B1 * m + (1.0 - ADAM_B1) * g
    nv = ADAM_B2 * v + (1.0 - ADAM_B2) * (g * g)
    m_hat = nm / (1.0 - ADAM_B1 ** ADAM_STEP)
    v_hat = nv / (1.0 - ADAM_B2 ** ADAM_STEP)
    return -ADAM_LR * (m_hat / (jnp.sqrt(v_hat) + ADAM_EPS) + ADAM_WD * w), nm, nv


def _adamw(w, g, m, v, *, name, br=512):
    r, c = w.shape
    br = _row_block(r, br)

    def body(w_ref, g_ref, m_ref, v_ref, d_ref, nm_ref, nv_ref):
        d_ref[...], nm_ref[...], nv_ref[...] = _adamw_update(w_ref[...], g_ref[...], m_ref[...], v_ref[...])

    row = pl.BlockSpec((br, c), lambda i: (i, 0))
    return pl.pallas_call(
        body,
        name=name,
        grid=(r // br,),
        in_specs=[row] * 4,
        out_specs=[row] * 3,
        out_shape=[jax.ShapeDtypeStruct((r, c), F32)] * 3,
        compiler_params=_params("parallel"),
    )(w, g, m, v)


def _sum_slots(x, *, name, br=512):
    n, r, c = x.shape
    br = _row_block(r, br)

    def body(x_ref, o_ref):
        acc = x_ref[0].astype(F32)
        for j in range(1, n):
            acc = acc + x_ref[j].astype(F32)
        o_ref[...] = acc

    return pl.pallas_call(
        body,
        name=name,
        grid=(r // br,),
        in_specs=[pl.BlockSpec((n, br, c), lambda i: (0, i, 0))],
        out_specs=pl.BlockSpec((br, c), lambda i: (i, 0)),
        out_shape=jax.ShapeDtypeStruct((r, c), F32),
        compiler_params=_params("parallel"),
    )(x)


_ADAMW_BLOCK_ELEMS = 128 * 1024


def _adamw_reduce(w, arrived, m, v, *, name):
    depth, r, c = w.shape
    br = _row_block(r, max(BF16_TILE_ROWS, _ADAMW_BLOCK_ELEMS // (-(-c // LANES) * LANES)))

    def body(w_ref, m_ref, v_ref, *rest):
        parts, (g_ref, d_ref, nm_ref, nv_ref) = rest[:depth], rest[depth:]
        for li in range(depth):

            @pl.when(pl.program_id(0) == li)
            def _(li=li):
                g = parts[li][0].astype(F32)
                for j in range(1, N_DEV):
                    g = g + parts[li][j].astype(F32)
                g_ref[...] = g
                d_ref[...], nm_ref[...], nv_ref[...] = _adamw_update(w_ref[...], g, m_ref[...], v_ref[...])

    cur = pl.BlockSpec((None, br, c), lambda l, i: (l, i, 0))
    slots = [pl.BlockSpec((N_DEV, br, c), lambda l, i, li=li: (0, jnp.where(l == li, i, 0), 0)) for li in range(depth)]
    return pl.pallas_call(
        body,
        name=name,
        grid=(depth, r // br),
        in_specs=[cur, cur, cur] + slots,
        out_specs=[cur] * 4,
        out_shape=[jax.ShapeDtypeStruct((depth, r, c), F32)] * 4,
        compiler_params=_params("arbitrary", "arbitrary"),
    )(w, m, v, *arrived)


def _my_place():
    return lax.axis_index("x"), lax.axis_index("y"), lax.axis_index("c")


def _flip(v, bit):
    return 1 - v if bit else v


def _slot_of(px, py, pc):
    return 4 * px + 2 * py + pc


_ANY = pl.BlockSpec(memory_space=pl.ANY)


_HBM = pl.BlockSpec(memory_space=pltpu.HBM)
_SEM = pl.BlockSpec(memory_space=pltpu.SEMAPHORE)
_DATAFLOW = pltpu.SideEffectType.DATAFLOW_SIDE_EFFECTING


def _exchange_copies(src_refs, land_refs, send_sem, recv_sem, scatter):
    mx, my, mc = _my_place()
    mine = _slot_of(mx, my, mc)
    copies = []
    for src, land, split in zip(src_refs, land_refs, scatter, strict=True):
        for k in range(1, N_DEV):
            peer = (_flip(mx, k & 4), _flip(my, k & 2), _flip(mc, k & 1))
            copies.append(pltpu.make_async_remote_copy(
                src_ref=src.at[_slot_of(*peer)] if split else src, dst_ref=land.at[mine],
                send_sem=send_sem, recv_sem=recv_sem, device_id=peer, device_id_type=MESH))
    return copies


def _exchange_start(srcs, after, *, scatter, name):
    n = len(srcs)
    lands = [lax.empty(s.shape if split else (N_DEV, *s.shape), s.dtype) for s, split in zip(srcs, scatter, strict=True)]

    def body(*refs):
        send_sem, recv_sem = refs[2 * n + 1], refs[2 * n + 2]
        for cp in _exchange_copies(refs[:n], refs[n:2 * n], send_sem, recv_sem, scatter):
            cp.start()
        refs[-1][...] = jnp.zeros_like(refs[-1])

    thru = [pltpu.HBM(a.shape, a.dtype) for a in (*srcs, *lands)]
    out = pl.pallas_call(
        body,
        name=name,
        in_specs=[_HBM] * (2 * n) + [_ANY],
        out_specs=(_SEM, _SEM, *[_HBM] * (2 * n), pl.BlockSpec(memory_space=pltpu.VMEM)),
        out_shape=(pltpu.SemaphoreType.DMA(()), pltpu.SemaphoreType.DMA(()), *thru,
                   jax.ShapeDtypeStruct((8, LANES), F32)),
        input_output_aliases={i: 2 + i for i in range(2 * n)},
        compiler_params=pltpu.CompilerParams(has_side_effects=_DATAFLOW),
    )(*[pltpu.with_memory_space_constraint(a, pltpu.HBM) for a in (*srcs, *lands)], after)
    return out[:-1], out[-1]


def _exchange_wait(started, after, *, scatter, name):
    send_sem, recv_sem, *thru = started
    n = len(thru) // 2

    def body(*refs):
        for cp in _exchange_copies(refs[:n], refs[n:2 * n], refs[2 * n], refs[2 * n + 1], scatter):
            cp.wait_send()
            cp.wait_recv()

    out = pl.pallas_call(
        body,
        name=name,
        in_specs=[_HBM] * (2 * n) + [_SEM, _SEM, _ANY],
        out_specs=[_HBM] * (2 * n),
        out_shape=[pltpu.HBM(a.shape, a.dtype) for a in thru],
        input_output_aliases={i: i for i in range(2 * n)},
        compiler_params=pltpu.CompilerParams(has_side_effects=_DATAFLOW),
    )(*thru, send_sem, recv_sem, after)
    return out[:n], out[n:]


def _with_own_slot(landed, own, me):
    return lax.dynamic_update_slice(landed, own[None], (me,) + (0,) * own.ndim)


_PACK_ROWS = BF16_TILE_ROWS


def _pack(arrays, dtype):
    flat = [a.astype(dtype).reshape(-1) for a in arrays]
    total = sum(f.shape[0] for f in flat)
    padded = -(-total // (LANES * _PACK_ROWS)) * (LANES * _PACK_ROWS)
    if padded > total:
        flat.append(jnp.zeros((padded - total,), dtype))
    return jnp.concatenate(flat).reshape(-1, LANES)


def _unpack(packed, shapes, lead=()):
    flat = packed.reshape(*lead, -1)
    out, off = [], 0
    for s in shapes:
        size = math.prod(s)
        out.append(flat[..., off:off + size].reshape(*lead, *s))
        off += size
    return out


def _gather_columns(g):
    return jnp.moveaxis(g, 0, 1).reshape(g.shape[1], -1)


def _split_columns(w):
    return jnp.moveaxis(w.reshape(w.shape[0], N_DEV, -1), 1, 0)


def _split_rows(w):
    return w.reshape(N_DEV, w.shape[0] // N_DEV, w.shape[1])


_FIRST = ("w_in",)
_REST = ("w_out", "w_ff1", "w_ff2", "w_ple_gate", "w_ple_proj")
_BIG = _FIRST + _REST
_COLUMN_SHARDED = ("w_in", "w_ff1", "w_ple_proj", "conv_w")
_SMALL = ("norm1_g", "q_norm_g", "k_norm_g", "sgu_norm_g", "sgu_w", "sgu_b", "norm2_g", "norm3_g")
_ORDER = ("norm1_g", "w_in", "conv_w", "q_norm_g", "k_norm_g", "sgu_norm_g", "sgu_w", "sgu_b", "w_out", "norm2_g",
          "w_ff1", "w_ff2", "norm3_g", "w_ple_gate", "w_ple_proj")


def _whole_matrices(names, landed, own, me):
    out = {}
    for k, g, mine in zip(names, landed, own, strict=True):
        g = _with_own_slot(g, mine, me)
        out[k] = _gather_columns(g) if k in _COLUMN_SHARDED else g.reshape(-1, g.shape[-1])
    return out


def _layer_forward(h0, p16, w, s, li, wait_rest):
    nm = lambda k: f"{k}_l{li}"
    t = h0.shape[0]
    hn1 = _rms_fwd(h0, s["norm1_g"], name=nm("rms1"))
    proj = _matmul(hn1, w["w_in"], name=nm("proj"), bm=t, bn=256)
    y_a = _conv_fwd(proj, w["conv_w"], name=nm("conv"))
    qs, kn, v = _qk_prep(proj, s["gq"], s["gk"], name=nm("qkprep"))
    y_b, lt = _attn_fwd(qs, kn, v, name=nm("attn"))
    y_c = _sgu_fwd(proj, s["sgu_norm_g"], s["sgu_w"], s["b_exp"], name=nm("sgu"))
    mix = jnp.concatenate([y_a, y_b, y_c], axis=1)
    w = dict(w, **wait_rest(y_b))
    h1 = _matmul(mix, w["w_out"], name=nm("out"), bm=t, bn=256, extras=(h0,), epilogue=lambda acc, r: (r + acc,))
    hn2 = _rms_fwd(h1, s["norm2_g"], name=nm("rms2"))
    u, f = _matmul(hn2, w["w_ff1"], name=nm("ff1"), bm=t, bn=512, out_dtypes=(F32, BF16),
                   epilogue=lambda acc: (acc, jnp.square(jnp.maximum(acc, 0.0))))
    h2 = _matmul(f, w["w_ff2"], name=nm("ff2"), bm=512, bn=512, extras=(h1,), epilogue=lambda acc, r: (r + acc,))
    hn3 = _rms_fwd(h2, s["norm3_g"], name=nm("rms3"))
    pp = _matmul(p16, w["w_ple_proj"], name=nm("pleproj"), bm=t, bn=512)

    def gate_epilogue(acc, pp_blk, h_blk):
        gate = jax.nn.sigmoid(acc)
        return h_blk + gate * pp_blk, gate

    h3, gate = _matmul(hn3, w["w_ple_gate"], name=nm("plegate"), bm=t, bn=256, out_dtypes=(F32, F32),
                       extras=(pp, h2), epilogue=gate_epilogue)
    saved = dict(h0=h0, hn1=hn1, proj=proj, qs=qs, kn=kn, v=v, lt=lt, mix=mix, h1=h1, hn2=hn2, u=u, f=f, h2=h2,
                 hn3=hn3, pp=pp, gate=gate, p16=p16)
    return h3, w, saved


def _layer_backward(dh3, a, w, s, li, order_after, start_rest):
    nm = lambda k: f"{k}_bwd_l{li}"
    t = dh3.shape[0]
    dpre, dpp = _ple_bwd(dh3, a["gate"], a["pp"], order_after, name=nm("ple"))
    g_gate = _weight_grad(a["hn3"], dpre, name=nm("dwgate"))
    g_proj = _weight_grad(a["p16"], dpp, name=nm("dwproj"), column_shards=True)
    dh2, dh2_16, g_n3 = _matmul_rms_bwd(dpre, w["w_ple_gate"], a["h2"], s["norm3_g"], dh3, name=nm("dh2"))
    du = _matmul(dh2_16, w["w_ff2"], name=nm("du"), tb=True, bm=t, bn=512, out_dtypes=(BF16,), extras=(a["u"],),
                 epilogue=lambda acc, u: (acc * (2.0 * jnp.maximum(u, 0.0)),))
    g_ff2 = _weight_grad(a["f"], dh2_16, name=nm("dwff2"))
    g_ff1 = _weight_grad(a["hn2"], du, name=nm("dwff1"), column_shards=True)
    dh1, dh1_16, g_n2 = _matmul_rms_bwd(du, w["w_ff1"], a["h1"], s["norm2_g"], dh2, name=nm("dh1"))
    dmix = _matmul(dh1_16, w["w_out"], name=nm("dmix"), tb=True, bm=t, bn=256)
    g_out = _weight_grad(a["mix"], dh1_16, name=nm("dwout"))
    started = start_rest(dict(w_out=_split_rows(g_out), w_ff1=g_ff1, w_ff2=_split_rows(g_ff2),
                              w_ple_gate=_split_rows(g_gate), w_ple_proj=g_proj), dmix)
    d_b, d_c, d_h, g_conv = _conv_bwd(dmix, a["proj"], w["conv_w"], name=nm("conv"))
    dqs, dkn, dv = _attn_bwd(dmix, a["qs"], a["kn"], a["v"], a["lt"], started, name=nm("attn"))
    d_q, d_k, d_v, g_q, g_k = _qk_prep_bwd(dqs, dkn, dv, a["proj"], s["gq"], s["gk"], name=nm("qkprep"))
    d_cu, d_cv, g_sn, g_sw, g_sb = _sgu_bwd(dmix, a["proj"], s["sgu_norm_g"], s["sgu_w"], s["b_exp"], name=nm("sgu"))
    dproj = jnp.concatenate([d_b, d_c, d_h, d_q, d_k, d_v, d_cu, d_cv], axis=1)
    g_in = _weight_grad(a["hn1"], dproj, name=nm("dwin"))
    dh0, _, g_n1 = _matmul_rms_bwd(dproj, w["w_in"], a["h0"], s["norm1_g"], dh1, name=nm("dh0"))
    n_tiles = g_q.shape[1] // HEAD_DIM
    small = dict(
        norm1_g=g_n1[0], norm2_g=g_n2[0], norm3_g=g_n3[0],
        q_norm_g=g_q.reshape(n_tiles, HEAD_DIM).sum(0), k_norm_g=g_k.reshape(n_tiles, HEAD_DIM).sum(0),
        sgu_norm_g=g_sn[0], sgu_w=g_sw, sgu_b=g_sb.reshape(CHUNK, SGU_HEADS, HEAD_DIM).sum(-1).T,
        conv_w=g_conv[:CONV_TAPS],
    )
    return dh0, _split_columns(g_in), small


def kernel(x, p, norm1_g, w_in, conv_w, q_norm_g, k_norm_g, sgu_norm_g, sgu_w, sgu_b, w_out, norm2_g, w_ff1, w_ff2, norm3_g, w_ple_gate, w_ple_proj, loss_target, m_norm1_g, m_w_in, m_conv_w, m_q_norm_g, m_k_norm_g, m_sgu_norm_g, m_sgu_w, m_sgu_b, m_w_out, m_norm2_g, m_w_ff1, m_w_ff2, m_norm3_g, m_w_ple_gate, m_w_ple_proj, v_norm1_g, v_w_in, v_conv_w, v_q_norm_g, v_k_norm_g, v_sgu_norm_g, v_sgu_w, v_sgu_b, v_w_out, v_norm2_g, v_w_ff1, v_w_ff2, v_norm3_g, v_w_ple_gate, v_w_ple_proj):
    weights = dict(norm1_g=norm1_g, w_in=w_in, conv_w=conv_w, q_norm_g=q_norm_g, k_norm_g=k_norm_g,
                   sgu_norm_g=sgu_norm_g, sgu_w=sgu_w, sgu_b=sgu_b, w_out=w_out, norm2_g=norm2_g, w_ff1=w_ff1,
                   w_ff2=w_ff2, norm3_g=norm3_g, w_ple_gate=w_ple_gate, w_ple_proj=w_ple_proj)
    mom = dict(norm1_g=m_norm1_g, w_in=m_w_in, conv_w=m_conv_w, q_norm_g=m_q_norm_g, k_norm_g=m_k_norm_g,
               sgu_norm_g=m_sgu_norm_g, sgu_w=m_sgu_w, sgu_b=m_sgu_b, w_out=m_w_out, norm2_g=m_norm2_g, w_ff1=m_w_ff1,
               w_ff2=m_w_ff2, norm3_g=m_norm3_g, w_ple_gate=m_w_ple_gate, w_ple_proj=m_w_ple_proj)
    var = dict(norm1_g=v_norm1_g, w_in=v_w_in, conv_w=v_conv_w, q_norm_g=v_q_norm_g, k_norm_g=v_k_norm_g,
               sgu_norm_g=v_sgu_norm_g, sgu_w=v_sgu_w, sgu_b=v_sgu_b, w_out=v_w_out, norm2_g=v_norm2_g, w_ff1=v_w_ff1,
               w_ff2=v_w_ff2, norm3_g=v_norm3_g, w_ple_gate=v_w_ple_gate, w_ple_proj=v_w_ple_proj)
    depth = norm1_g.shape[0]
    mx, my, mc = _my_place()
    me = _slot_of(mx, my, mc)

    gather_first, gather_rest = [], []
    token = x[0, :8, :LANES]
    for li in range(depth):
        started, token = _exchange_start([w_in[li].astype(BF16), conv_w[li]], token, scatter=(False, False),
                                         name=f"gather_first_start_l{li}")
        gather_first.append(started)
        started, token = _exchange_start([weights[k][li].astype(BF16) for k in _REST], token,
                                         scatter=(False,) * len(_REST), name=f"gather_rest_start_l{li}")
        gather_rest.append(started)

    small = []
    for li in range(depth):
        small.append(dict(
            norm1_g=norm1_g[li][None], norm2_g=norm2_g[li][None], norm3_g=norm3_g[li][None],
            gq=jnp.tile(q_norm_g[li], _QK_BLOCK // HEAD_DIM)[None], gk=jnp.tile(k_norm_g[li], _QK_BLOCK // HEAD_DIM)[None],
            sgu_norm_g=sgu_norm_g[li][None], sgu_w=sgu_w[li], b_exp=jnp.repeat(sgu_b[li].T, HEAD_DIM, axis=1),
        ))

    h = x[0]
    saved, full = [], []
    for li in range(depth):
        own, landed = _exchange_wait(gather_first[li], token if li == 0 else h, scatter=(False, False),
                                     name=f"gather_first_wait_l{li}")
        first = _whole_matrices(("w_in", "conv_w"), landed, own, me)

        def wait_rest(after, li=li):
            own, landed = _exchange_wait(gather_rest[li], after, scatter=(False,) * len(_REST),
                                         name=f"gather_rest_wait_l{li}")
            return _whole_matrices(_REST, landed, own, me)

        h, w, acts = _layer_forward(h, p[li, 0].astype(BF16), first, small[li], li, wait_rest)
        full.append(w)
        saved.append(acts)
    dh, loss_tile = _loss_head(h, loss_target[0], name="loss_head")
    loss = lax.psum(loss_tile[0, 0], ("x", "y", "c"))

    small_names = _SMALL + ("conv_w",)
    scatter_first, scatter_rest = [None] * depth, [None] * depth
    first_modes = (True, False)
    token = loss_tile
    for li in reversed(range(depth)):

        def start_rest(parts, after, li=li):
            scatter_rest[li], started = _exchange_start([parts[k] for k in _REST], after, scatter=(True,) * len(_REST),
                                                        name=f"scatter_rest_start_l{li}")
            return started

        dh, g_in, small_grads = _layer_backward(dh, saved[li], full[li], small[li], li, token, start_rest)
        small_shapes = [small_grads[k].shape for k in small_names]
        scatter_first[li], token = _exchange_start([g_in, _pack([small_grads[k] for k in small_names], F32)], dh,
                                                   scatter=first_modes, name=f"scatter_first_start_l{li}")
    grad_x = dh[None]

    grads, delta, new_m, new_v = {}, {}, {}, {}
    arrived = {k: [None] * depth for k in _BIG}
    for li in reversed(range(depth)):
        sent, landed = _exchange_wait(scatter_rest[li], token, scatter=(True,) * len(_REST),
                                      name=f"scatter_rest_wait_l{li}")
        for k, g, src in zip(_REST, landed, sent, strict=True):
            arrived[k][li] = _with_own_slot(g, lax.dynamic_index_in_dim(src, me, 0, keepdims=False), me)
    for k in _REST:
        grads[k], delta[k], new_m[k], new_v[k] = _adamw_reduce(weights[k], arrived[k], mom[k], var[k], name=f"adamw_{k}")
    small_sums = [None] * depth
    for li in reversed(range(depth)):
        sent, landed = _exchange_wait(scatter_first[li], delta[_REST[-1]], scatter=first_modes,
                                      name=f"scatter_first_wait_l{li}")
        arrived["w_in"][li] = _with_own_slot(landed[0], lax.dynamic_index_in_dim(sent[0], me, 0, keepdims=False), me)
        small_sums[li] = _unpack(_sum_slots(_with_own_slot(landed[1], sent[1], me), name=f"sum_small_grads_l{li}"),
                                 small_shapes)
    grads["w_in"], delta["w_in"], new_m["w_in"], new_v["w_in"] = _adamw_reduce(
        w_in, arrived["w_in"], mom["w_in"], var["w_in"], name="adamw_w_in")
    for i, k in enumerate(small_names):
        grads[k] = jnp.stack([small_sums[li][i] for li in range(depth)])
    n_conv = conv_w.shape[2]
    grads["conv_w"] = lax.dynamic_slice_in_dim(grads["conv_w"], me * n_conv, n_conv, axis=2)
    rest = _SMALL + ("conv_w",)
    rest_shapes = [weights[k].shape for k in rest]
    packs = [_pack([src[k] for k in rest], F32) for src in (weights, grads, mom, var)]
    for out, packed in zip((delta, new_m, new_v), _adamw(*packs, name="adamw_small")):
        out.update(zip(rest, _unpack(packed, rest_shapes)))

    return (loss, grad_x, *[grads[k] for k in _ORDER], *[delta[k] for k in _ORDER],
            *[new_m[k] for k in _ORDER], *[new_v[k] for k in _ORDER])
```

```python
import math

import jax
import jax.numpy as jnp
from jax import lax
from jax.experimental import pallas as pl
from jax.experimental.pallas import tpu as pltpu

F32 = jnp.float32
BF16 = jnp.bfloat16

N_DEV = 8
HEAD_DIM = 64
CONV_W = 256
ATTN_W = 512
SGU_W = 256
SGU_HEADS = 4
CHUNK = 128
CONV_TAPS = 3
EPS = 1e-6
QK_SCALE = HEAD_DIM ** -0.5

ADAM_LR = 0.001
ADAM_B1 = 0.9
ADAM_B2 = 0.999
ADAM_EPS = 1e-08
ADAM_WD = 0.01
ADAM_STEP = 10

LANES = 128
BF16_TILE_ROWS = 16
VMEM_LIMIT_BYTES = 56 * 1024 * 1024
MESH = pl.DeviceIdType.MESH


def _params(*sem):
    return pltpu.CompilerParams(dimension_semantics=sem, vmem_limit_bytes=VMEM_LIMIT_BYTES)


def _row_block(rows, cap):
    if rows <= cap:
        return rows
    return max(b for b in range(BF16_TILE_ROWS, cap + 1, BF16_TILE_ROWS) if rows % b == 0)


def _matmul(a, b, *, name, tb=False, bm=512, bn=512, out_dtypes=(F32,), epilogue=None, extras=()):
    m, k = a.shape
    n = b.shape[0] if tb else b.shape[1]
    assert k == (b.shape[1] if tb else b.shape[0])
    bm, bn = min(bm, m), min(bn, n)
    assert m % bm == 0 and n % bn == 0
    a_spec = pl.BlockSpec((bm, k), lambda i, j: (i, 0))
    b_spec = pl.BlockSpec((bn, k), lambda i, j: (j, 0)) if tb else pl.BlockSpec((k, bn), lambda i, j: (0, j))
    dims = (((1,), (1 if tb else 0,)), ((), ()))
    n_ex = len(extras)
    for e in extras:
        assert e.shape == (m, n), (e.shape, m, n)

    def body(a_ref, b_ref, *rest):
        outs = rest[n_ex:]
        acc = lax.dot_general(a_ref[...], b_ref[...], dims, preferred_element_type=F32)
        res = (acc,) if epilogue is None else epilogue(acc, *[e[...] for e in rest[:n_ex]])
        for o_ref, r in zip(outs, res, strict=True):
            o_ref[...] = r.astype(o_ref.dtype)

    tile = pl.BlockSpec((bm, bn), lambda i, j: (i, j))
    out = pl.pallas_call(
        body,
        name=name,
        grid=(m // bm, n // bn),
        in_specs=[a_spec, b_spec] + [tile] * n_ex,
        out_specs=[tile] * len(out_dtypes),
        out_shape=[jax.ShapeDtypeStruct((m, n), d) for d in out_dtypes],
        compiler_params=_params("parallel", "parallel"),
    )(a, b, *extras)
    return out[0] if len(out_dtypes) == 1 else out


_WEIGHT_GRAD_ACC_ELEMS = 1024 * 1024


def _weight_grad(x, dy, *, name, column_shards=False):
    t, m = x.shape
    n = dy.shape[1]
    bm = m if m <= 2 * LANES else min(m // 2, max(LANES, _WEIGHT_GRAD_ACC_ELEMS // n // LANES * LANES))
    assert m % bm == 0
    ns = n // N_DEV

    def body(x_ref, dy_ref, o_ref):
        acc = lax.dot_general(x_ref[...], dy_ref[...], _TN, preferred_element_type=F32)
        if column_shards:
            for s in range(N_DEV):
                o_ref[s] = acc[:, s * ns:(s + 1) * ns].astype(o_ref.dtype)
        else:
            o_ref[...] = acc.astype(o_ref.dtype)

    if column_shards:
        out_spec, out_dims = pl.BlockSpec((N_DEV, bm, ns), lambda i: (0, i, 0)), (N_DEV, m, ns)
    else:
        out_spec, out_dims = pl.BlockSpec((bm, n), lambda i: (i, 0)), (m, n)
    return pl.pallas_call(
        body,
        name=name,
        grid=(m // bm,),
        in_specs=[pl.BlockSpec((t, bm), lambda i: (0, i)), pl.BlockSpec((t, n), lambda i: (0, 0))],
        out_specs=out_spec,
        out_shape=jax.ShapeDtypeStruct(out_dims, BF16),
        compiler_params=_params("parallel"),
    )(x, dy)


def _rms_fwd(h, g, *, name, br=512):
    t, d = h.shape
    br = min(br, t)

    def body(h_ref, g_ref, o_ref):
        x = h_ref[...]
        r = lax.rsqrt(jnp.mean(x * x, axis=-1, keepdims=True) + EPS)
        o_ref[...] = (x * r * g_ref[...]).astype(o_ref.dtype)

    return pl.pallas_call(
        body,
        name=name,
        grid=(t // br,),
        in_specs=[pl.BlockSpec((br, d), lambda i: (i, 0)), pl.BlockSpec((1, d), lambda i: (0, 0))],
        out_specs=pl.BlockSpec((br, d), lambda i: (i, 0)),
        out_shape=jax.ShapeDtypeStruct((t, d), BF16),
        compiler_params=_params("parallel"),
    )(h, g)


def _matmul_rms_bwd(dz, w, h, g, dres, *, name):
    t, d = h.shape
    k = dz.shape[1]
    br = min(t, 512 if k <= d else 256)

    def body(dz_ref, w_ref, h_ref, g_ref, dres_ref, dh_ref, dh16_ref, dg_ref):
        x = h_ref[...]
        dyv = lax.dot_general(dz_ref[...], w_ref[...], _NT, preferred_element_type=F32)
        r = lax.rsqrt(jnp.mean(x * x, axis=-1, keepdims=True) + EPS)
        xhat = x * r
        dxhat = dyv * g_ref[...]
        dh = dres_ref[...] + r * (dxhat - xhat * jnp.mean(dxhat * xhat, axis=-1, keepdims=True))
        dh_ref[...] = dh
        dh16_ref[...] = dh.astype(dh16_ref.dtype)

        @pl.when(pl.program_id(0) == 0)
        def _():
            dg_ref[...] = jnp.zeros_like(dg_ref)

        dg_ref[...] += jnp.sum(dyv * xhat, axis=0, keepdims=True)

    row = pl.BlockSpec((br, d), lambda i: (i, 0))
    vec = pl.BlockSpec((1, d), lambda i: (0, 0))
    return pl.pallas_call(
        body,
        name=name,
        grid=(t // br,),
        in_specs=[pl.BlockSpec((br, k), lambda i: (i, 0)), pl.BlockSpec((d, k), lambda i: (0, 0)), row, vec, row],
        out_specs=[row, row, vec],
        out_shape=[jax.ShapeDtypeStruct((t, d), F32), jax.ShapeDtypeStruct((t, d), BF16),
                   jax.ShapeDtypeStruct((1, d), F32)],
        compiler_params=_params("arbitrary"),
    )(dz, w, h, g, dres)


def _group_mean(x, width):
    grp = lax.broadcasted_iota(jnp.int32, x.shape, 1) // HEAD_DIM
    out = jnp.zeros_like(x)
    for gi in range(width // HEAD_DIM):
        m = grp == gi
        s = jnp.sum(jnp.where(m, x, 0.0), axis=1, keepdims=True)
        out = jnp.where(m, s, out)
    return out * (1.0 / HEAD_DIM)


def _gelu(x):
    return 0.5 * x * (1.0 + lax.erf(x * (2.0 ** -0.5)))


def _gelu_grad(x):
    cdf = 0.5 * (1.0 + lax.erf(x * (2.0 ** -0.5)))
    pdf = jnp.exp(-0.5 * x * x) * (1.0 / math.sqrt(2.0 * math.pi))
    return cdf + x * pdf


def _shift_down(z, s, row):
    return jnp.where(row >= s, pltpu.roll(z, s, 0), 0.0)


def _shift_up(z, s, row, t):
    return jnp.where(row < t - s, pltpu.roll(z, t - s, 0), 0.0)


def _conv_fwd(proj, conv_w, *, name):
    t = proj.shape[0]
    nb = CONV_W // LANES

    def body(b_ref, c_ref, h_ref, w_ref, o_ref):
        row = lax.broadcasted_iota(jnp.int32, (t, LANES), 0)
        z = c_ref[...] * h_ref[...]
        w = w_ref[...]
        conv = w[2:3, :] * z + w[1:2, :] * _shift_down(z, 1, row) + w[0:1, :] * _shift_down(z, 2, row)
        o_ref[...] = (b_ref[...] * conv).astype(o_ref.dtype)

    return pl.pallas_call(
        body,
        name=name,
        grid=(nb,),
        in_specs=[
            pl.BlockSpec((t, LANES), lambda j: (0, j)),
            pl.BlockSpec((t, LANES), lambda j: (0, nb + j)),
            pl.BlockSpec((t, LANES), lambda j: (0, 2 * nb + j)),
            pl.BlockSpec((CONV_TAPS, LANES), lambda j: (0, j)),
        ],
        out_specs=pl.BlockSpec((t, LANES), lambda j: (0, j)),
        out_shape=jax.ShapeDtypeStruct((t, CONV_W), BF16),
        compiler_params=_params("parallel"),
    )(proj, proj, proj, conv_w)


def _conv_bwd(dmix, proj, conv_w, *, name):
    t = proj.shape[0]
    nb = CONV_W // LANES

    def body(dy_ref, b_ref, c_ref, h_ref, w_ref, db_ref, dc_ref, dh_ref, dw_ref):
        row = lax.broadcasted_iota(jnp.int32, (t, LANES), 0)
        ac, ah = c_ref[...], h_ref[...]
        z = ac * ah
        w = w_ref[...]
        z1 = _shift_down(z, 1, row)
        z2 = _shift_down(z, 2, row)
        conv = w[2:3, :] * z + w[1:2, :] * z1 + w[0:1, :] * z2
        dy = dy_ref[...]
        db_ref[...] = (dy * conv).astype(db_ref.dtype)
        dconv = dy * b_ref[...]
        dz = w[2:3, :] * dconv + w[1:2, :] * _shift_up(dconv, 1, row, t) + w[0:1, :] * _shift_up(dconv, 2, row, t)
        dc_ref[...] = (dz * ah).astype(dc_ref.dtype)
        dh_ref[...] = (dz * ac).astype(dh_ref.dtype)
        dw_ref[...] = jnp.zeros_like(dw_ref)
        dw_ref[0:1, :] = jnp.sum(dconv * z2, axis=0, keepdims=True)
        dw_ref[1:2, :] = jnp.sum(dconv * z1, axis=0, keepdims=True)
        dw_ref[2:3, :] = jnp.sum(dconv * z, axis=0, keepdims=True)

    col = lambda off: pl.BlockSpec((t, LANES), lambda j: (0, off + j))
    return pl.pallas_call(
        body,
        name=name,
        grid=(nb,),
        in_specs=[col(0), col(0), col(nb), col(2 * nb), pl.BlockSpec((CONV_TAPS, LANES), lambda j: (0, j))],
        out_specs=[col(0), col(0), col(0), pl.BlockSpec((8, LANES), lambda j: (0, j))],
        out_shape=[jax.ShapeDtypeStruct((t, CONV_W), BF16)] * 3 + [jax.ShapeDtypeStruct((8, CONV_W), F32)],
        compiler_params=_params("parallel"),
    )(dmix, proj, proj, proj, conv_w)


_QK_BLOCK = 256


def _qk_prep(proj, gq, gk, *, name, br=512):
    t = proj.shape[0]
    br = min(br, t)
    nb = ATTN_W // _QK_BLOCK
    q0 = (3 * CONV_W) // _QK_BLOCK

    def body(q_ref, k_ref, v_ref, gq_ref, gk_ref, qo_ref, ko_ref, vo_ref):
        q = q_ref[...]
        k = k_ref[...]
        rq = lax.rsqrt(_group_mean(q * q, _QK_BLOCK) + EPS)
        rk = lax.rsqrt(_group_mean(k * k, _QK_BLOCK) + EPS)
        qo_ref[...] = ((q * rq * gq_ref[...]).astype(BF16) * QK_SCALE).astype(qo_ref.dtype)
        ko_ref[...] = (k * rk * gk_ref[...]).astype(ko_ref.dtype)
        vo_ref[...] = v_ref[...].astype(vo_ref.dtype)

    col = lambda off: pl.BlockSpec((br, _QK_BLOCK), lambda i, j: (i, off + j))
    vec = pl.BlockSpec((1, _QK_BLOCK), lambda i, j: (0, 0))
    return pl.pallas_call(
        body,
        name=name,
        grid=(t // br, nb),
        in_specs=[col(q0), col(q0 + nb), col(q0 + 2 * nb), vec, vec],
        out_specs=[col(0)] * 3,
        out_shape=[jax.ShapeDtypeStruct((t, ATTN_W), BF16)] * 3,
        compiler_params=_params("parallel", "parallel"),
    )(proj, proj, proj, gq, gk)


def _qk_prep_bwd(dqs, dkn, dv, proj, gq, gk, *, name, br=512):
    t = proj.shape[0]
    br = min(br, t)
    nb = ATTN_W // _QK_BLOCK
    q0 = (3 * CONV_W) // _QK_BLOCK

    def norm_bwd(dy, x, g):
        r = lax.rsqrt(_group_mean(x * x, _QK_BLOCK) + EPS)
        xhat = x * r
        dxhat = dy * g
        dx = r * (dxhat - xhat * _group_mean(dxhat * xhat, _QK_BLOCK))
        return dx, jnp.sum(dy * xhat, axis=0, keepdims=True)

    def body(dq_ref, dk_ref, dv_ref, q_ref, k_ref, gq_ref, gk_ref, oq_ref, ok_ref, ov_ref, dgq_ref, dgk_ref):
        dq, dgq = norm_bwd(dq_ref[...] * QK_SCALE, q_ref[...], gq_ref[...])
        dk, dgk = norm_bwd(dk_ref[...], k_ref[...], gk_ref[...])
        oq_ref[...] = dq.astype(oq_ref.dtype)
        ok_ref[...] = dk.astype(ok_ref.dtype)
        ov_ref[...] = dv_ref[...].astype(ov_ref.dtype)

        @pl.when((pl.program_id(0) == 0) & (pl.program_id(1) == 0))
        def _():
            dgq_ref[...] = jnp.zeros_like(dgq_ref)
            dgk_ref[...] = jnp.zeros_like(dgk_ref)

        dgq_ref[...] += dgq
        dgk_ref[...] += dgk

    col = lambda off: pl.BlockSpec((br, _QK_BLOCK), lambda i, j: (i, off + j))
    vec = pl.BlockSpec((1, _QK_BLOCK), lambda i, j: (0, 0))
    return pl.pallas_call(
        body,
        name=name,
        grid=(t // br, nb),
        in_specs=[col(0), col(0), col(0), col(q0), col(q0 + nb), vec, vec],
        out_specs=[col(0), col(0), col(0), vec, vec],
        out_shape=[jax.ShapeDtypeStruct((t, ATTN_W), BF16)] * 3 + [jax.ShapeDtypeStruct((1, _QK_BLOCK), F32)] * 2,
        compiler_params=_params("arbitrary", "arbitrary"),
    )(dqs, dkn, dv, proj, proj, gq, gk)


def _split_bf16(x):
    hi = x.astype(BF16)
    return jnp.concatenate([hi, (x - hi.astype(F32)).astype(BF16)], axis=1)


def _key_order_matrix(tb, relation):
    jj = lax.broadcasted_iota(jnp.int32, (tb, tb), 0)
    ss = lax.broadcasted_iota(jnp.int32, (tb, tb), 1)
    m = relation(jj, ss).astype(BF16)
    return jnp.concatenate([m, m], axis=0)


def _log_sigmoids(z):
    lb = jnp.minimum(z, 0.0) - jnp.log(1.0 + jnp.exp(-jnp.abs(z)))
    return lb, lb - z


def _below_diagonal(tb):
    return lax.broadcasted_iota(jnp.int32, (tb, tb), 1) < lax.broadcasted_iota(jnp.int32, (tb, tb), 0)


_NT = (((1,), (1,)), ((), ()))
_TN = (((0,), (0,)), ((), ()))
_ATTN_BLOCK = 256
_ATTN_UNROLL = 2


def _attn_fwd(qs, kn, v, *, name, tb=_ATTN_BLOCK, unroll=_ATTN_UNROLL):
    t = qs.shape[0]
    tb = min(tb, t)
    assert t % tb == 0
    n_pairs = ATTN_W // LANES

    def body(q_ref, k_ref, v_ref, o_ref, lt_ref, acc_ref, carry_ref):
        qb = pl.program_id(1)
        half = lax.broadcasted_iota(jnp.int32, (1, LANES), 1) // HEAD_DIM
        later = _key_order_matrix(tb, lambda j, s: j > s)
        acc_ref[...] = jnp.zeros_like(acc_ref)
        carry_ref[...] = jnp.zeros_like(carry_ref)
        q = q_ref[...]
        qh = [jnp.where(half == h, q, jnp.zeros_like(q)) for h in range(2)]

        def tiles(kbs, diagonal):
            blk = []
            for kb in kbs:
                start = pl.multiple_of(kb * tb, tb)
                blk.append((k_ref[pl.ds(start, tb), :], v_ref[pl.ds(start, tb), :]))
            chains = [(h, j) for j in range(len(kbs)) for h in range(2)]
            z = [lax.dot_general(qh[h], blk[j][0], _NT, preferred_element_type=F32) for h, j in chains]
            causal = _below_diagonal(tb) if diagonal else None
            lb, lr = [], []
            for zi in z:
                b, r = _log_sigmoids(zi)
                lb.append(b)
                lr.append(jnp.where(causal, r, 0.0) if diagonal else r)
            suffix = [jnp.dot(_split_bf16(r), later, preferred_element_type=F32) for r in lr]
            carry = [carry_ref[0], carry_ref[1]]
            w = []
            for i, (h, j) in enumerate(chains):
                wi = jnp.exp(lb[i] + (suffix[i] + carry[h][:, 0:1]))
                w.append((jnp.where(causal, wi, 0.0) if diagonal else wi).astype(BF16))
                carry[h] = carry[h] + jnp.sum(lr[i], axis=1, keepdims=True)
            for i, (h, j) in enumerate(chains):
                vh = jnp.where(half == h, blk[j][1], jnp.zeros_like(blk[j][1]))
                acc_ref[h] += jnp.dot(w[i], vh, preferred_element_type=F32)
            carry_ref[0] = carry[0]
            carry_ref[1] = carry[1]

        tiles([qb], True)

        def step(i, _):
            kb = qb - 1 - unroll * i
            tiles([kb - u for u in range(unroll)], False)
            return 0

        lax.fori_loop(0, qb // unroll, step, 0)
        for left in range(1, unroll):

            @pl.when(qb % unroll == left)
            def _(left=left):
                tiles([left - 1 - u for u in range(left)], False)

        o_ref[...] = (acc_ref[0] + acc_ref[1]).astype(o_ref.dtype)
        lt_ref[...] = jnp.where(half == 0, carry_ref[0], carry_ref[1])

    return pl.pallas_call(
        body,
        name=name,
        grid=(n_pairs, t // tb),
        in_specs=[
            pl.BlockSpec((tb, LANES), lambda p, i: (i, p)),
            pl.BlockSpec((t, LANES), lambda p, i: (0, p)),
            pl.BlockSpec((t, LANES), lambda p, i: (0, p)),
        ],
        out_specs=[pl.BlockSpec((tb, LANES), lambda p, i: (i, p))] * 2,
        out_shape=[jax.ShapeDtypeStruct((t, ATTN_W), BF16), jax.ShapeDtypeStruct((t, ATTN_W), F32)],
        scratch_shapes=[pltpu.VMEM((2, tb, LANES), F32), pltpu.VMEM((2, tb, LANES), F32)],
        compiler_params=_params("parallel", "parallel"),
    )(qs, kn, v)


def _attn_bwd(dmix, qs, kn, v, lt, order_after, *, name, tb=_ATTN_BLOCK, unroll=_ATTN_UNROLL):
    t = qs.shape[0]
    tb = min(tb, t)
    assert t % tb == 0
    n_pairs = ATTN_W // LANES
    dy0 = CONV_W // LANES

    def body(do_ref, q_ref, k_ref, v_ref, lt_ref, order_ref, dq_ref, dk_ref, dv_ref, dqacc_ref, cc_ref, cg_ref):
        qb = pl.program_id(1)
        half = lax.broadcasted_iota(jnp.int32, (1, LANES), 1) // HEAD_DIM
        lane = lax.broadcasted_iota(jnp.int32, (tb, LANES), 1)
        upto = _key_order_matrix(tb, lambda j, s: j <= s)
        before = _key_order_matrix(tb, lambda j, s: j < s)
        q = q_ref[...]
        do = do_ref[...].astype(BF16)
        lt = lt_ref[...]
        qh = [jnp.where(half == h, q, jnp.zeros_like(q)) for h in range(2)]
        doh = [jnp.where(half == h, do, jnp.zeros_like(do)) for h in range(2)]
        lth = [jnp.sum(jnp.where(lane == h * HEAD_DIM, lt, 0.0), axis=1, keepdims=True) for h in range(2)]

        @pl.when(qb == 0)
        def _():
            dk_ref[...] = jnp.zeros_like(dk_ref)
            dv_ref[...] = jnp.zeros_like(dv_ref)

        dqacc_ref[...] = jnp.zeros_like(dqacc_ref)
        cc_ref[...] = jnp.zeros_like(cc_ref)
        cg_ref[...] = jnp.zeros_like(cg_ref)

        def tiles(kbs, diagonal):
            starts = [pl.multiple_of(kb * tb, tb) for kb in kbs]
            blk = [(k_ref[pl.ds(s, tb), :], v_ref[pl.ds(s, tb), :]) for s in starts]
            chains = [(h, j) for j in range(len(kbs)) for h in range(2)]
            z = [lax.dot_general(qh[h], blk[j][0], _NT, preferred_element_type=F32) for h, j in chains]
            da = [lax.dot_general(doh[h], jnp.where(half == h, blk[j][1], jnp.zeros_like(blk[j][1])), _NT,
                                  preferred_element_type=F32) for h, j in chains]
            causal = _below_diagonal(tb) if diagonal else None
            lb, lr = [], []
            for zi in z:
                b, r = _log_sigmoids(zi)
                lb.append(b)
                lr.append(jnp.where(causal, r, 0.0) if diagonal else r)
            prefix = [jnp.dot(_split_bf16(r), upto, preferred_element_type=F32) for r in lr]
            cc = [cc_ref[0], cc_ref[1]]
            cg = [cg_ref[0], cg_ref[1]]
            a16, g = [], []
            for i, (h, j) in enumerate(chains):
                a = jnp.exp(lb[i] - prefix[i] + (lth[h] - cc[h][:, 0:1]))
                if diagonal:
                    a = jnp.where(causal, a, 0.0)
                a16.append(a.astype(BF16))
                g.append(da[i] * a)
                cc[h] = cc[h] + jnp.sum(lr[i], axis=1, keepdims=True)
            g_before = [jnp.dot(_split_bf16(gi), before, preferred_element_type=F32) for gi in g]
            dz = []
            for i, (h, j) in enumerate(chains):
                dzi = g[i] - jnp.exp(lb[i]) * (g[i] + (g_before[i] + cg[h][:, 0:1]))
                dz.append((jnp.where(causal, dzi, 0.0) if diagonal else dzi).astype(BF16))
                cg[h] = cg[h] + jnp.sum(g[i], axis=1, keepdims=True)
            for i, (h, j) in enumerate(chains):
                kh = jnp.where(half == h, blk[j][0], jnp.zeros_like(blk[j][0]))
                dqacc_ref[h] += jnp.dot(dz[i], kh, preferred_element_type=F32)
                dk_ref[pl.ds(starts[j], tb), :] += lax.dot_general(dz[i], qh[h], _TN, preferred_element_type=F32)
                dv_ref[pl.ds(starts[j], tb), :] += lax.dot_general(a16[i], doh[h], _TN, preferred_element_type=F32)
            for h in range(2):
                cc_ref[h] = cc[h]
                cg_ref[h] = cg[h]

        def step(i, _):
            kb = unroll * i
            tiles([kb + u for u in range(unroll)], False)
            return 0

        lax.fori_loop(0, qb // unroll, step, 0)
        for left in range(1, unroll):

            @pl.when(qb % unroll == left)
            def _(left=left):
                tiles([qb - left + u for u in range(left)], False)

        tiles([qb], True)
        dq_ref[...] = dqacc_ref[0] + dqacc_ref[1]

    qblk = pl.BlockSpec((tb, LANES), lambda p, i: (i, p))
    whole = pl.BlockSpec((t, LANES), lambda p, i: (0, p))
    return pl.pallas_call(
        body,
        name=name,
        grid=(n_pairs, t // tb),
        in_specs=[pl.BlockSpec((tb, LANES), lambda p, i: (i, dy0 + p)), qblk, whole, whole, qblk,
                  pl.BlockSpec(order_after.shape, lambda p, i: (0, 0))],
        out_specs=[qblk, whole, whole],
        out_shape=[jax.ShapeDtypeStruct((t, ATTN_W), F32)] * 3,
        scratch_shapes=[pltpu.VMEM((2, tb, LANES), F32)] * 3,
        compiler_params=_params("parallel", "arbitrary"),
    )(dmix, qs, kn, v, lt, order_after)


def _sgu_weights(w_ref):
    tt = lax.broadcasted_iota(jnp.int32, (CHUNK, CHUNK), 0)
    ss = lax.broadcasted_iota(jnp.int32, (CHUNK, CHUNK), 1)
    tril = ss <= tt
    return [jnp.where(tril, w_ref[gi], 0.0).astype(BF16) for gi in range(SGU_HEADS)], tril


def _sgu_fwd(proj, g_v, w_s, b_exp, *, name):
    t = proj.shape[0]
    u0 = (3 * CONV_W + 3 * ATTN_W) // SGU_W

    def body(u_ref, v_ref, g_ref, w_ref, b_ref, o_ref):
        grp = lax.broadcasted_iota(jnp.int32, (1, SGU_W), 1) // HEAD_DIM
        u = _gelu(u_ref[...])
        vv = _gelu(v_ref[...])
        vn = (vv * lax.rsqrt(_group_mean(vv * vv, SGU_W) + EPS) * g_ref[...]).astype(BF16)
        wm, _ = _sgu_weights(w_ref)
        sv = b_ref[...]
        for gi in range(SGU_HEADS):
            sv = sv + jnp.dot(wm[gi], jnp.where(grp == gi, vn, jnp.zeros_like(vn)), preferred_element_type=F32)
        o_ref[...] = (u * sv).astype(o_ref.dtype)

    return pl.pallas_call(
        body,
        name=name,
        grid=(t // CHUNK,),
        in_specs=[
            pl.BlockSpec((CHUNK, SGU_W), lambda i: (i, u0)),
            pl.BlockSpec((CHUNK, SGU_W), lambda i: (i, u0 + 1)),
            pl.BlockSpec((1, SGU_W), lambda i: (0, 0)),
            pl.BlockSpec((SGU_HEADS, CHUNK, CHUNK), lambda i: (0, 0, 0)),
            pl.BlockSpec((CHUNK, SGU_W), lambda i: (0, 0)),
        ],
        out_specs=pl.BlockSpec((CHUNK, SGU_W), lambda i: (i, 0)),
        out_shape=jax.ShapeDtypeStruct((t, SGU_W), BF16),
        compiler_params=_params("parallel"),
    )(proj, proj, g_v, w_s, b_exp)


def _sgu_bwd(dmix, proj, g_v, w_s, b_exp, *, name):
    t = proj.shape[0]
    u0 = (3 * CONV_W + 3 * ATTN_W) // SGU_W
    dy0 = (CONV_W + ATTN_W) // SGU_W

    def body(dy_ref, u_ref, v_ref, g_ref, w_ref, b_ref, du_ref, dv_ref, dg_ref, dw_ref, db_ref):
        grp = lax.broadcasted_iota(jnp.int32, (1, SGU_W), 1) // HEAD_DIM
        cu, cv = u_ref[...], v_ref[...]
        u = _gelu(cu)
        vv = _gelu(cv)
        r = lax.rsqrt(_group_mean(vv * vv, SGU_W) + EPS)
        xhat = vv * r
        gain = g_ref[...]
        vn = (xhat * gain).astype(BF16)
        wm, tril = _sgu_weights(w_ref)
        vng = [jnp.where(grp == gi, vn, jnp.zeros_like(vn)) for gi in range(SGU_HEADS)]
        sv = b_ref[...]
        for gi in range(SGU_HEADS):
            sv = sv + jnp.dot(wm[gi], vng[gi], preferred_element_type=F32)
        dy = dy_ref[...]
        du_ref[...] = (dy * sv * _gelu_grad(cu)).astype(du_ref.dtype)
        dsv = dy * u
        dsv16 = dsv.astype(BF16)

        @pl.when(pl.program_id(0) == 0)
        def _():
            dg_ref[...] = jnp.zeros_like(dg_ref)
            dw_ref[...] = jnp.zeros_like(dw_ref)
            db_ref[...] = jnp.zeros_like(db_ref)

        db_ref[...] += dsv
        dvn = jnp.zeros_like(dsv)
        for gi in range(SGU_HEADS):
            dw = lax.dot_general(dsv16, vng[gi], _NT, preferred_element_type=F32)
            dw_ref[gi] += jnp.where(tril, dw, 0.0)
            dvn_g = lax.dot_general(wm[gi], dsv16, _TN, preferred_element_type=F32)
            dvn = jnp.where(grp == gi, dvn_g, dvn)
        dg_ref[...] += jnp.sum(dvn * xhat, axis=0, keepdims=True)
        dxhat = dvn * gain
        dvv = r * (dxhat - xhat * _group_mean(dxhat * xhat, SGU_W))
        dv_ref[...] = (dvv * _gelu_grad(cv)).astype(dv_ref.dtype)

    return pl.pallas_call(
        body,
        name=name,
        grid=(t // CHUNK,),
        in_specs=[
            pl.BlockSpec((CHUNK, SGU_W), lambda i: (i, dy0)),
            pl.BlockSpec((CHUNK, SGU_W), lambda i: (i, u0)),
            pl.BlockSpec((CHUNK, SGU_W), lambda i: (i, u0 + 1)),
            pl.BlockSpec((1, SGU_W), lambda i: (0, 0)),
            pl.BlockSpec((SGU_HEADS, CHUNK, CHUNK), lambda i: (0, 0, 0)),
            pl.BlockSpec((CHUNK, SGU_W), lambda i: (0, 0)),
        ],
        out_specs=[
            pl.BlockSpec((CHUNK, SGU_W), lambda i: (i, 0)),
            pl.BlockSpec((CHUNK, SGU_W), lambda i: (i, 0)),
            pl.BlockSpec((1, SGU_W), lambda i: (0, 0)),
            pl.BlockSpec((SGU_HEADS, CHUNK, CHUNK), lambda i: (0, 0, 0)),
            pl.BlockSpec((CHUNK, SGU_W), lambda i: (0, 0)),
        ],
        out_shape=[
            jax.ShapeDtypeStruct((t, SGU_W), BF16),
            jax.ShapeDtypeStruct((t, SGU_W), BF16),
            jax.ShapeDtypeStruct((1, SGU_W), F32),
            jax.ShapeDtypeStruct((SGU_HEADS, CHUNK, CHUNK), F32),
            jax.ShapeDtypeStruct((CHUNK, SGU_W), F32),
        ],
        compiler_params=_params("arbitrary"),
    )(dmix, proj, proj, g_v, w_s, b_exp)


def _ple_bwd(dh, gate, pp, order_after, *, name, br=512):
    t, d = dh.shape
    br = min(br, t)

    def body(dh_ref, g_ref, p_ref, order_ref, dpre_ref, dpp_ref):
        dhv, g = dh_ref[...], g_ref[...]
        dpre_ref[...] = (dhv * p_ref[...] * g * (1.0 - g)).astype(dpre_ref.dtype)
        dpp_ref[...] = (dhv * g).astype(dpp_ref.dtype)

    row = pl.BlockSpec((br, d), lambda i: (i, 0))
    return pl.pallas_call(
        body,
        name=name,
        grid=(t // br,),
        in_specs=[row] * 3 + [pl.BlockSpec(order_after.shape, lambda i: (0, 0))],
        out_specs=[row] * 2,
        out_shape=[jax.ShapeDtypeStruct((t, d), BF16)] * 2,
        compiler_params=_params("parallel"),
    )(dh, gate, pp, order_after)


def _loss_head(y, target, *, name, br=512):
    t, d = y.shape
    br = min(br, t)

    def body(y_ref, t_ref, dy_ref, loss_ref):
        err = y_ref[...] - t_ref[...]
        dy_ref[...] = err * (1.0 / d)

        @pl.when(pl.program_id(0) == 0)
        def _():
            loss_ref[...] = jnp.zeros_like(loss_ref)

        loss_ref[...] += 0.5 * jnp.sum(jnp.sum(err * err, axis=1, keepdims=True) * (1.0 / d), axis=0, keepdims=True)

    row = pl.BlockSpec((br, d), lambda i: (i, 0))
    return pl.pallas_call(
        body,
        name=name,
        grid=(t // br,),
        in_specs=[row, row],
        out_specs=[row, pl.BlockSpec((8, LANES), lambda i: (0, 0))],
        out_shape=[jax.ShapeDtypeStruct((t, d), F32), jax.ShapeDtypeStruct((8, LANES), F32)],
        compiler_params=_params("arbitrary"),
    )(y, target)


def _adamw_update(w, g, m, v):
    nm = ADAM_B1 * m + (1.0 - ADAM_B1) * g
    nv = ADAM_B2 * v + (1.0 - ADAM_B2) * (g * g)
    m_hat = nm / (1.0 - ADAM_B1 ** ADAM_STEP)
    v_hat = nv / (1.0 - ADAM_B2 ** ADAM_STEP)
    return -ADAM_LR * (m_hat / (jnp.sqrt(v_hat) + ADAM_EPS) + ADAM_WD * w), nm, nv


def _adamw(w, g, m, v, *, name, br=512):
    r, c = w.shape
    br = _row_block(r, br)

    def body(w_ref, g_ref, m_ref, v_ref, d_ref, nm_ref, nv_ref):
        d_ref[...], nm_ref[...], nv_ref[...] = _adamw_update(w_ref[...], g_ref[...], m_ref[...], v_ref[...])

    row = pl.BlockSpec((br, c), lambda i: (i, 0))
    return pl.pallas_call(
        body,
        name=name,
        grid=(r // br,),
        in_specs=[row] * 4,
        out_specs=[row] * 3,
        out_shape=[jax.ShapeDtypeStruct((r, c), F32)] * 3,
        compiler_params=_params("parallel"),
    )(w, g, m, v)


def _sum_slots(x, *, name, br=512):
    n, r, c = x.shape
    br = _row_block(r, br)

    def body(x_ref, o_ref):
        acc = x_ref[0].astype(F32)
        for j in range(1, n):
            acc = acc + x_ref[j].astype(F32)
        o_ref[...] = acc

    return pl.pallas_call(
        body,
        name=name,
        grid=(r // br,),
        in_specs=[pl.BlockSpec((n, br, c), lambda i: (0, i, 0))],
        out_specs=pl.BlockSpec((br, c), lambda i: (i, 0)),
        out_shape=jax.ShapeDtypeStruct((r, c), F32),
        compiler_params=_params("parallel"),
    )(x)


_ADAMW_BLOCK_ELEMS = 128 * 1024


def _adamw_reduce(w, arrived, m, v, *, name):
    depth, r, c = w.shape
    br = _row_block(r, max(BF16_TILE_ROWS, _ADAMW_BLOCK_ELEMS // (-(-c // LANES) * LANES)))

    def body(w_ref, m_ref, v_ref, *rest):
        parts, (g_ref, d_ref, nm_ref, nv_ref) = rest[:depth], rest[depth:]
        for li in range(depth):

            @pl.when(pl.program_id(0) == li)
            def _(li=li):
                g = parts[li][0].astype(F32)
                for j in range(1, N_DEV):
                    g = g + parts[li][j].astype(F32)
                g_ref[...] = g
                d_ref[...], nm_ref[...], nv_ref[...] = _adamw_update(w_ref[...], g, m_ref[...], v_ref[...])

    cur = pl.BlockSpec((None, br, c), lambda l, i: (l, i, 0))
    slots = [pl.BlockSpec((N_DEV, br, c), lambda l, i, li=li: (0, jnp.where(l == li, i, 0), 0)) for li in range(depth)]
    return pl.pallas_call(
        body,
        name=name,
        grid=(depth, r // br),
        in_specs=[cur, cur, cur] + slots,
        out_specs=[cur] * 4,
        out_shape=[jax.ShapeDtypeStruct((depth, r, c), F32)] * 4,
        compiler_params=_params("arbitrary", "arbitrary"),
    )(w, m, v, *arrived)


def _my_place():
    return lax.axis_index("x"), lax.axis_index("y"), lax.axis_index("c")


def _flip(v, bit):
    return 1 - v if bit else v


def _slot_of(px, py, pc):
    return 4 * px + 2 * py + pc


_ANY = pl.BlockSpec(memory_space=pl.ANY)


_HBM = pl.BlockSpec(memory_space=pltpu.HBM)
_SEM = pl.BlockSpec(memory_space=pltpu.SEMAPHORE)
_DATAFLOW = pltpu.SideEffectType.DATAFLOW_SIDE_EFFECTING


def _exchange_copies(src_refs, land_refs, send_sem, recv_sem, scatter):
    mx, my, mc = _my_place()
    mine = _slot_of(mx, my, mc)
    copies = []
    for src, land, split in zip(src_refs, land_refs, scatter, strict=True):
        for k in range(1, N_DEV):
            peer = (_flip(mx, k & 4), _flip(my, k & 2), _flip(mc, k & 1))
            copies.append(pltpu.make_async_remote_copy(
                src_ref=src.at[_slot_of(*peer)] if split else src, dst_ref=land.at[mine],
                send_sem=send_sem, recv_sem=recv_sem, device_id=peer, device_id_type=MESH))
    return copies


def _exchange_start(groups, after, *, name):
    sizes = [len(srcs) for srcs, _ in groups]
    n, n_sems = sum(sizes), 2 * len(groups)
    srcs = [a for arrays, _ in groups for a in arrays]
    lands = [lax.empty(a.shape if split else (N_DEV, *a.shape), a.dtype)
             for arrays, flags in groups for a, split in zip(arrays, flags, strict=True)]
    offsets = [sum(sizes[:g]) for g in range(len(groups))]

    def body(*refs):
        sems = refs[2 * n + 1:2 * n + 1 + n_sems]
        for g, (off, size, (_, flags)) in enumerate(zip(offsets, sizes, groups)):
            for cp in _exchange_copies(refs[off:off + size], refs[n + off:n + off + size], sems[2 * g], sems[2 * g + 1],
                                       flags):
                cp.start()
        refs[-1][...] = jnp.zeros_like(refs[-1])

    thru = [pltpu.HBM(a.shape, a.dtype) for a in (*srcs, *lands)]
    out = pl.pallas_call(
        body,
        name=name,
        in_specs=[_HBM] * (2 * n) + [_ANY],
        out_specs=(*[_SEM] * n_sems, *[_HBM] * (2 * n), pl.BlockSpec(memory_space=pltpu.VMEM)),
        out_shape=(*[pltpu.SemaphoreType.DMA(())] * n_sems, *thru, jax.ShapeDtypeStruct((8, LANES), F32)),
        input_output_aliases={i: n_sems + i for i in range(2 * n)},
        compiler_params=pltpu.CompilerParams(has_side_effects=_DATAFLOW),
    )(*[pltpu.with_memory_space_constraint(a, pltpu.HBM) for a in (*srcs, *lands)], after)
    sems, arrays = out[:n_sems], out[n_sems:-1]
    started = [(sems[2 * g], sems[2 * g + 1], *arrays[off:off + size], *arrays[n + off:n + off + size])
               for g, (off, size) in enumerate(zip(offsets, sizes))]
    return started, out[-1]


def _exchange_wait(started, after, *, scatter, name):
    send_sem, recv_sem, *thru = started
    n = len(thru) // 2

    def body(*refs):
        for cp in _exchange_copies(refs[:n], refs[n:2 * n], refs[2 * n], refs[2 * n + 1], scatter):
            cp.wait_send()
            cp.wait_recv()

    out = pl.pallas_call(
        body,
        name=name,
        in_specs=[_HBM] * (2 * n) + [_SEM, _SEM, _ANY],
        out_specs=[_HBM] * (2 * n),
        out_shape=[pltpu.HBM(a.shape, a.dtype) for a in thru],
        input_output_aliases={i: i for i in range(2 * n)},
        compiler_params=pltpu.CompilerParams(has_side_effects=_DATAFLOW),
    )(*thru, send_sem, recv_sem, after)
    return out[:n], out[n:]


def _with_own_slot(landed, own, me):
    return lax.dynamic_update_slice(landed, own[None], (me,) + (0,) * own.ndim)


_PACK_ROWS = BF16_TILE_ROWS


def _pack(arrays, dtype):
    flat = [a.astype(dtype).reshape(-1) for a in arrays]
    total = sum(f.shape[0] for f in flat)
    padded = -(-total // (LANES * _PACK_ROWS)) * (LANES * _PACK_ROWS)
    if padded > total:
        flat.append(jnp.zeros((padded - total,), dtype))
    return jnp.concatenate(flat).reshape(-1, LANES)


def _unpack(packed, shapes, lead=()):
    flat = packed.reshape(*lead, -1)
    out, off = [], 0
    for s in shapes:
        size = math.prod(s)
        out.append(flat[..., off:off + size].reshape(*lead, *s))
        off += size
    return out


def _gather_columns(g):
    return jnp.moveaxis(g, 0, 1).reshape(g.shape[1], -1)


def _split_columns(w):
    return jnp.moveaxis(w.reshape(w.shape[0], N_DEV, -1), 1, 0)


def _split_rows(w):
    return w.reshape(N_DEV, w.shape[0] // N_DEV, w.shape[1])


_GATHER_GROUPS = (("w_in", "conv_w"), ("w_out",), ("w_ff1",), ("w_ff2",), ("w_ple_gate", "w_ple_proj"))
_REST = ("w_out", "w_ff1", "w_ff2", "w_ple_gate", "w_ple_proj")
_BIG = ("w_in",) + _REST
_COLUMN_SHARDED = ("w_in", "w_ff1", "w_ple_proj", "conv_w")
_SMALL = ("norm1_g", "q_norm_g", "k_norm_g", "sgu_norm_g", "sgu_w", "sgu_b", "norm2_g", "norm3_g")
_ORDER = ("norm1_g", "w_in", "conv_w", "q_norm_g", "k_norm_g", "sgu_norm_g", "sgu_w", "sgu_b", "w_out", "norm2_g",
          "w_ff1", "w_ff2", "norm3_g", "w_ple_gate", "w_ple_proj")


def _whole_matrices(names, landed, own, me):
    out = {}
    for k, g, mine in zip(names, landed, own, strict=True):
        g = _with_own_slot(g, mine, me)
        out[k] = _gather_columns(g) if k in _COLUMN_SHARDED else g.reshape(-1, g.shape[-1])
    return out


def _layer_forward(h0, p16, s, li, fetch):
    nm = lambda k: f"{k}_l{li}"
    t = h0.shape[0]
    hn1 = _rms_fwd(h0, s["norm1_g"], name=nm("rms1"))
    w = fetch(0, hn1)
    proj = _matmul(hn1, w["w_in"], name=nm("proj"), bm=t, bn=256)
    y_a = _conv_fwd(proj, w["conv_w"], name=nm("conv"))
    qs, kn, v = _qk_prep(proj, s["gq"], s["gk"], name=nm("qkprep"))
    y_b, lt = _attn_fwd(qs, kn, v, name=nm("attn"))
    y_c = _sgu_fwd(proj, s["sgu_norm_g"], s["sgu_w"], s["b_exp"], name=nm("sgu"))
    mix = jnp.concatenate([y_a, y_b, y_c], axis=1)
    w.update(fetch(1, y_b))
    h1 = _matmul(mix, w["w_out"], name=nm("out"), bm=t, bn=256, extras=(h0,), epilogue=lambda acc, r: (r + acc,))
    hn2 = _rms_fwd(h1, s["norm2_g"], name=nm("rms2"))
    w.update(fetch(2, hn2))
    u, f = _matmul(hn2, w["w_ff1"], name=nm("ff1"), bm=t, bn=512, out_dtypes=(F32, BF16),
                   epilogue=lambda acc: (acc, jnp.square(jnp.maximum(acc, 0.0))))
    w.update(fetch(3, f))
    h2 = _matmul(f, w["w_ff2"], name=nm("ff2"), bm=512, bn=512, extras=(h1,), epilogue=lambda acc, r: (r + acc,))
    hn3 = _rms_fwd(h2, s["norm3_g"], name=nm("rms3"))
    w.update(fetch(4, hn3))
    pp = _matmul(p16, w["w_ple_proj"], name=nm("pleproj"), bm=t, bn=512)

    def gate_epilogue(acc, pp_blk, h_blk):
        gate = jax.nn.sigmoid(acc)
        return h_blk + gate * pp_blk, gate

    h3, gate = _matmul(hn3, w["w_ple_gate"], name=nm("plegate"), bm=t, bn=256, out_dtypes=(F32, F32),
                       extras=(pp, h2), epilogue=gate_epilogue)
    saved = dict(h0=h0, hn1=hn1, proj=proj, qs=qs, kn=kn, v=v, lt=lt, mix=mix, h1=h1, hn2=hn2, u=u, f=f, h2=h2,
                 hn3=hn3, pp=pp, gate=gate, p16=p16)
    return h3, w, saved


def _layer_backward(dh3, a, w, s, li, order_after, start_rest):
    nm = lambda k: f"{k}_bwd_l{li}"
    t = dh3.shape[0]
    dpre, dpp = _ple_bwd(dh3, a["gate"], a["pp"], order_after, name=nm("ple"))
    g_gate = _weight_grad(a["hn3"], dpre, name=nm("dwgate"))
    g_proj = _weight_grad(a["p16"], dpp, name=nm("dwproj"), column_shards=True)
    dh2, dh2_16, g_n3 = _matmul_rms_bwd(dpre, w["w_ple_gate"], a["h2"], s["norm3_g"], dh3, name=nm("dh2"))
    du = _matmul(dh2_16, w["w_ff2"], name=nm("du"), tb=True, bm=t, bn=512, out_dtypes=(BF16,), extras=(a["u"],),
                 epilogue=lambda acc, u: (acc * (2.0 * jnp.maximum(u, 0.0)),))
    g_ff2 = _weight_grad(a["f"], dh2_16, name=nm("dwff2"))
    g_ff1 = _weight_grad(a["hn2"], du, name=nm("dwff1"), column_shards=True)
    dh1, dh1_16, g_n2 = _matmul_rms_bwd(du, w["w_ff1"], a["h1"], s["norm2_g"], dh2, name=nm("dh1"))
    dmix = _matmul(dh1_16, w["w_out"], name=nm("dmix"), tb=True, bm=t, bn=256)
    g_out = _weight_grad(a["mix"], dh1_16, name=nm("dwout"))
    started = start_rest(dict(w_out=_split_rows(g_out), w_ff1=g_ff1, w_ff2=_split_rows(g_ff2),
                              w_ple_gate=_split_rows(g_gate), w_ple_proj=g_proj), dmix)
    d_b, d_c, d_h, g_conv = _conv_bwd(dmix, a["proj"], w["conv_w"], name=nm("conv"))
    dqs, dkn, dv = _attn_bwd(dmix, a["qs"], a["kn"], a["v"], a["lt"], started, name=nm("attn"))
    d_q, d_k, d_v, g_q, g_k = _qk_prep_bwd(dqs, dkn, dv, a["proj"], s["gq"], s["gk"], name=nm("qkprep"))
    d_cu, d_cv, g_sn, g_sw, g_sb = _sgu_bwd(dmix, a["proj"], s["sgu_norm_g"], s["sgu_w"], s["b_exp"], name=nm("sgu"))
    dproj = jnp.concatenate([d_b, d_c, d_h, d_q, d_k, d_v, d_cu, d_cv], axis=1)
    g_in = _weight_grad(a["hn1"], dproj, name=nm("dwin"))
    dh0, _, g_n1 = _matmul_rms_bwd(dproj, w["w_in"], a["h0"], s["norm1_g"], dh1, name=nm("dh0"))
    n_tiles = g_q.shape[1] // HEAD_DIM
    small = dict(
        norm1_g=g_n1[0], norm2_g=g_n2[0], norm3_g=g_n3[0],
        q_norm_g=g_q.reshape(n_tiles, HEAD_DIM).sum(0), k_norm_g=g_k.reshape(n_tiles, HEAD_DIM).sum(0),
        sgu_norm_g=g_sn[0], sgu_w=g_sw, sgu_b=g_sb.reshape(CHUNK, SGU_HEADS, HEAD_DIM).sum(-1).T,
        conv_w=g_conv[:CONV_TAPS],
    )
    return dh0, _split_columns(g_in), small


def kernel(x, p, norm1_g, w_in, conv_w, q_norm_g, k_norm_g, sgu_norm_g, sgu_w, sgu_b, w_out, norm2_g, w_ff1, w_ff2, norm3_g, w_ple_gate, w_ple_proj, loss_target, m_norm1_g, m_w_in, m_conv_w, m_q_norm_g, m_k_norm_g, m_sgu_norm_g, m_sgu_w, m_sgu_b, m_w_out, m_norm2_g, m_w_ff1, m_w_ff2, m_norm3_g, m_w_ple_gate, m_w_ple_proj, v_norm1_g, v_w_in, v_conv_w, v_q_norm_g, v_k_norm_g, v_sgu_norm_g, v_sgu_w, v_sgu_b, v_w_out, v_norm2_g, v_w_ff1, v_w_ff2, v_norm3_g, v_w_ple_gate, v_w_ple_proj):
    weights = dict(norm1_g=norm1_g, w_in=w_in, conv_w=conv_w, q_norm_g=q_norm_g, k_norm_g=k_norm_g,
                   sgu_norm_g=sgu_norm_g, sgu_w=sgu_w, sgu_b=sgu_b, w_out=w_out, norm2_g=norm2_g, w_ff1=w_ff1,
                   w_ff2=w_ff2, norm3_g=norm3_g, w_ple_gate=w_ple_gate, w_ple_proj=w_ple_proj)
    mom = dict(norm1_g=m_norm1_g, w_in=m_w_in, conv_w=m_conv_w, q_norm_g=m_q_norm_g, k_norm_g=m_k_norm_g,
               sgu_norm_g=m_sgu_norm_g, sgu_w=m_sgu_w, sgu_b=m_sgu_b, w_out=m_w_out, norm2_g=m_norm2_g, w_ff1=m_w_ff1,
               w_ff2=m_w_ff2, norm3_g=m_norm3_g, w_ple_gate=m_w_ple_gate, w_ple_proj=m_w_ple_proj)
    var = dict(norm1_g=v_norm1_g, w_in=v_w_in, conv_w=v_conv_w, q_norm_g=v_q_norm_g, k_norm_g=v_k_norm_g,
               sgu_norm_g=v_sgu_norm_g, sgu_w=v_sgu_w, sgu_b=v_sgu_b, w_out=v_w_out, norm2_g=v_norm2_g, w_ff1=v_w_ff1,
               w_ff2=v_w_ff2, norm3_g=v_norm3_g, w_ple_gate=v_w_ple_gate, w_ple_proj=v_w_ple_proj)
    depth = norm1_g.shape[0]
    mx, my, mc = _my_place()
    me = _slot_of(mx, my, mc)

    gathers = []
    token = x[0, :8, :LANES]
    for li in range(depth):
        groups = [([weights[k][li] if k == "conv_w" else weights[k][li].astype(BF16) for k in names],
                   (False,) * len(names)) for names in _GATHER_GROUPS]
        started, token = _exchange_start(groups, token, name=f"gather_start_l{li}")
        gathers.append(started)

    small = []
    for li in range(depth):
        small.append(dict(
            norm1_g=norm1_g[li][None], norm2_g=norm2_g[li][None], norm3_g=norm3_g[li][None],
            gq=jnp.tile(q_norm_g[li], _QK_BLOCK // HEAD_DIM)[None], gk=jnp.tile(k_norm_g[li], _QK_BLOCK // HEAD_DIM)[None],
            sgu_norm_g=sgu_norm_g[li][None], sgu_w=sgu_w[li], b_exp=jnp.repeat(sgu_b[li].T, HEAD_DIM, axis=1),
        ))
    small[0]["norm1_g"] = small[0]["norm1_g"] + token[0, 0]

    h = x[0]
    saved, full = [], []
    for li in range(depth):

        def fetch(g, after, li=li):
            names = _GATHER_GROUPS[g]
            own, landed = _exchange_wait(gathers[li][g], after, scatter=(False,) * len(names),
                                         name=f"gather_{names[0]}_wait_l{li}")
            return _whole_matrices(names, landed, own, me)

        h, w, acts = _layer_forward(h, p[li, 0].astype(BF16), small[li], li, fetch)
        full.append(w)
        saved.append(acts)
    dh, loss_tile = _loss_head(h, loss_target[0], name="loss_head")
    loss = lax.psum(loss_tile[0, 0], ("x", "y", "c"))

    small_names = _SMALL + ("conv_w",)
    scatter_first, scatter_rest = [None] * depth, [None] * depth
    first_modes = (True, False)
    token = loss_tile
    for li in reversed(range(depth)):

        def start_rest(parts, after, li=li):
            (scatter_rest[li],), started = _exchange_start([([parts[k] for k in _REST], (True,) * len(_REST))], after,
                                                           name=f"scatter_rest_start_l{li}")
            return started

        dh, g_in, small_grads = _layer_backward(dh, saved[li], full[li], small[li], li, token, start_rest)
        small_shapes = [small_grads[k].shape for k in small_names]
        (scatter_first[li],), token = _exchange_start(
            [([g_in, _pack([small_grads[k] for k in small_names], F32)], first_modes)], dh,
            name=f"scatter_first_start_l{li}")
    grad_x = dh[None]

    grads, delta, new_m, new_v = {}, {}, {}, {}
    arrived = {k: [None] * depth for k in _BIG}
    for li in reversed(range(depth)):
        sent, landed = _exchange_wait(scatter_rest[li], token, scatter=(True,) * len(_REST),
                                      name=f"scatter_rest_wait_l{li}")
        for k, g, src in zip(_REST, landed, sent, strict=True):
            arrived[k][li] = _with_own_slot(g, lax.dynamic_index_in_dim(src, me, 0, keepdims=False), me)
    for k in _REST:
        grads[k], delta[k], new_m[k], new_v[k] = _adamw_reduce(weights[k], arrived[k], mom[k], var[k], name=f"adamw_{k}")
    small_sums = [None] * depth
    updated = jnp.stack([delta[k][0, 0, :1] for k in _REST])
    for li in reversed(range(depth)):
        sent, landed = _exchange_wait(scatter_first[li], updated, scatter=first_modes,
                                      name=f"scatter_first_wait_l{li}")
        arrived["w_in"][li] = _with_own_slot(landed[0], lax.dynamic_index_in_dim(sent[0], me, 0, keepdims=False), me)
        small_sums[li] = _unpack(_sum_slots(_with_own_slot(landed[1], sent[1], me), name=f"sum_small_grads_l{li}"),
                                 small_shapes)
    grads["w_in"], delta["w_in"], new_m["w_in"], new_v["w_in"] = _adamw_reduce(
        w_in, arrived["w_in"], mom["w_in"], var["w_in"], name="adamw_w_in")
    for i, k in enumerate(small_names):
        grads[k] = jnp.stack([small_sums[li][i] for li in range(depth)])
    n_conv = conv_w.shape[2]
    grads["conv_w"] = lax.dynamic_slice_in_dim(grads["conv_w"], me * n_conv, n_conv, axis=2)
    for k in small_names:
        as_rows = lambda a: a.reshape(-1, a.shape[-1])
        outs = _adamw(as_rows(weights[k]), as_rows(grads[k]), as_rows(mom[k]), as_rows(var[k]), name=f"adamw_{k}")
        delta[k], new_m[k], new_v[k] = (o.reshape(weights[k].shape) for o in outs)

    return (loss, grad_x, *[grads[k] for k in _ORDER], *[delta[k] for k in _ORDER],
            *[new_m[k] for k in _ORDER], *[new_v[k] for k in _ORDER])
```

```python
import math

import jax
import jax.numpy as jnp
from jax import lax
from jax.experimental import pallas as pl
from jax.experimental.pallas import tpu as pltpu

F32 = jnp.float32
BF16 = jnp.bfloat16

N_DEV = 8
HEAD_DIM = 64
CONV_W = 256
ATTN_W = 512
SGU_W = 256
SGU_HEADS = 4
CHUNK = 128
CONV_TAPS = 3
EPS = 1e-6
QK_SCALE = HEAD_DIM ** -0.5

ADAM_LR = 0.001
ADAM_B1 = 0.9
ADAM_B2 = 0.999
ADAM_EPS = 1e-08
ADAM_WD = 0.01
ADAM_STEP = 10

LANES = 128
BF16_TILE_ROWS = 16
VMEM_LIMIT_BYTES = 56 * 1024 * 1024
MESH = pl.DeviceIdType.MESH


def _params(*sem):
    return pltpu.CompilerParams(dimension_semantics=sem, vmem_limit_bytes=VMEM_LIMIT_BYTES)


def _row_block(rows, cap):
    if rows <= cap:
        return rows
    return max(b for b in range(BF16_TILE_ROWS, cap + 1, BF16_TILE_ROWS) if rows % b == 0)


def _matmul(a, b, *, name, tb=False, bm=512, bn=512, out_dtypes=(F32,), epilogue=None, extras=()):
    m, k = a.shape
    n = b.shape[0] if tb else b.shape[1]
    assert k == (b.shape[1] if tb else b.shape[0])
    bm, bn = min(bm, m), min(bn, n)
    assert m % bm == 0 and n % bn == 0
    a_spec = pl.BlockSpec((bm, k), lambda i, j: (i, 0))
    b_spec = pl.BlockSpec((bn, k), lambda i, j: (j, 0)) if tb else pl.BlockSpec((k, bn), lambda i, j: (0, j))
    dims = (((1,), (1 if tb else 0,)), ((), ()))
    n_ex = len(extras)
    for e in extras:
        assert e.shape == (m, n), (e.shape, m, n)

    def body(a_ref, b_ref, *rest):
        outs = rest[n_ex:]
        acc = lax.dot_general(a_ref[...], b_ref[...], dims, preferred_element_type=F32)
        res = (acc,) if epilogue is None else epilogue(acc, *[e[...] for e in rest[:n_ex]])
        for o_ref, r in zip(outs, res, strict=True):
            o_ref[...] = r.astype(o_ref.dtype)

    tile = pl.BlockSpec((bm, bn), lambda i, j: (i, j))
    out = pl.pallas_call(
        body,
        name=name,
        grid=(m // bm, n // bn),
        in_specs=[a_spec, b_spec] + [tile] * n_ex,
        out_specs=[tile] * len(out_dtypes),
        out_shape=[jax.ShapeDtypeStruct((m, n), d) for d in out_dtypes],
        compiler_params=_params("parallel", "parallel"),
    )(a, b, *extras)
    return out[0] if len(out_dtypes) == 1 else out


_WEIGHT_GRAD_ACC_ELEMS = 1024 * 1024


def _weight_grad(x, dy, *, name, column_shards=False):
    t, m = x.shape
    n = dy.shape[1]
    bm = m if m <= 2 * LANES else min(m // 2, max(LANES, _WEIGHT_GRAD_ACC_ELEMS // n // LANES * LANES))
    assert m % bm == 0
    ns = n // N_DEV

    def body(x_ref, dy_ref, o_ref):
        acc = lax.dot_general(x_ref[...], dy_ref[...], _TN, preferred_element_type=F32)
        if column_shards:
            for s in range(N_DEV):
                o_ref[s] = acc[:, s * ns:(s + 1) * ns].astype(o_ref.dtype)
        else:
            o_ref[...] = acc.astype(o_ref.dtype)

    if column_shards:
        out_spec, out_dims = pl.BlockSpec((N_DEV, bm, ns), lambda i: (0, i, 0)), (N_DEV, m, ns)
    else:
        out_spec, out_dims = pl.BlockSpec((bm, n), lambda i: (i, 0)), (m, n)
    return pl.pallas_call(
        body,
        name=name,
        grid=(m // bm,),
        in_specs=[pl.BlockSpec((t, bm), lambda i: (0, i)), pl.BlockSpec((t, n), lambda i: (0, 0))],
        out_specs=out_spec,
        out_shape=jax.ShapeDtypeStruct(out_dims, BF16),
        compiler_params=_params("parallel"),
    )(x, dy)


def _rms_fwd(h, g, *, name, br=512):
    t, d = h.shape
    br = min(br, t)

    def body(h_ref, g_ref, o_ref):
        x = h_ref[...]
        r = lax.rsqrt(jnp.mean(x * x, axis=-1, keepdims=True) + EPS)
        o_ref[...] = (x * r * g_ref[...]).astype(o_ref.dtype)

    return pl.pallas_call(
        body,
        name=name,
        grid=(t // br,),
        in_specs=[pl.BlockSpec((br, d), lambda i: (i, 0)), pl.BlockSpec((1, d), lambda i: (0, 0))],
        out_specs=pl.BlockSpec((br, d), lambda i: (i, 0)),
        out_shape=jax.ShapeDtypeStruct((t, d), BF16),
        compiler_params=_params("parallel"),
    )(h, g)


def _matmul_rms_bwd(dz, w, h, g, dres, *, name):
    t, d = h.shape
    k = dz.shape[1]
    br = min(t, 512 if k <= d else 256)

    def body(dz_ref, w_ref, h_ref, g_ref, dres_ref, dh_ref, dh16_ref, dg_ref):
        x = h_ref[...]
        dyv = lax.dot_general(dz_ref[...], w_ref[...], _NT, preferred_element_type=F32)
        r = lax.rsqrt(jnp.mean(x * x, axis=-1, keepdims=True) + EPS)
        xhat = x * r
        dxhat = dyv * g_ref[...]
        dh = dres_ref[...] + r * (dxhat - xhat * jnp.mean(dxhat * xhat, axis=-1, keepdims=True))
        dh_ref[...] = dh
        dh16_ref[...] = dh.astype(dh16_ref.dtype)

        @pl.when(pl.program_id(0) == 0)
        def _():
            dg_ref[...] = jnp.zeros_like(dg_ref)

        dg_ref[...] += jnp.sum(dyv * xhat, axis=0, keepdims=True)

    row = pl.BlockSpec((br, d), lambda i: (i, 0))
    vec = pl.BlockSpec((1, d), lambda i: (0, 0))
    return pl.pallas_call(
        body,
        name=name,
        grid=(t // br,),
        in_specs=[pl.BlockSpec((br, k), lambda i: (i, 0)), pl.BlockSpec((d, k), lambda i: (0, 0)), row, vec, row],
        out_specs=[row, row, vec],
        out_shape=[jax.ShapeDtypeStruct((t, d), F32), jax.ShapeDtypeStruct((t, d), BF16),
                   jax.ShapeDtypeStruct((1, d), F32)],
        compiler_params=_params("arbitrary"),
    )(dz, w, h, g, dres)


def _group_mean(x, width):
    grp = lax.broadcasted_iota(jnp.int32, x.shape, 1) // HEAD_DIM
    out = jnp.zeros_like(x)
    for gi in range(width // HEAD_DIM):
        m = grp == gi
        s = jnp.sum(jnp.where(m, x, 0.0), axis=1, keepdims=True)
        out = jnp.where(m, s, out)
    return out * (1.0 / HEAD_DIM)


def _gelu(x):
    return 0.5 * x * (1.0 + lax.erf(x * (2.0 ** -0.5)))


def _gelu_grad(x):
    cdf = 0.5 * (1.0 + lax.erf(x * (2.0 ** -0.5)))
    pdf = jnp.exp(-0.5 * x * x) * (1.0 / math.sqrt(2.0 * math.pi))
    return cdf + x * pdf


def _shift_down(z, s, row):
    return jnp.where(row >= s, pltpu.roll(z, s, 0), 0.0)


def _shift_up(z, s, row, t):
    return jnp.where(row < t - s, pltpu.roll(z, t - s, 0), 0.0)


def _conv_fwd(proj, conv_w, *, name):
    t = proj.shape[0]
    nb = CONV_W // LANES

    def body(b_ref, c_ref, h_ref, w_ref, o_ref):
        row = lax.broadcasted_iota(jnp.int32, (t, LANES), 0)
        z = c_ref[...] * h_ref[...]
        w = w_ref[...]
        conv = w[2:3, :] * z + w[1:2, :] * _shift_down(z, 1, row) + w[0:1, :] * _shift_down(z, 2, row)
        o_ref[...] = (b_ref[...] * conv).astype(o_ref.dtype)

    return pl.pallas_call(
        body,
        name=name,
        grid=(nb,),
        in_specs=[
            pl.BlockSpec((t, LANES), lambda j: (0, j)),
            pl.BlockSpec((t, LANES), lambda j: (0, nb + j)),
            pl.BlockSpec((t, LANES), lambda j: (0, 2 * nb + j)),
            pl.BlockSpec((CONV_TAPS, LANES), lambda j: (0, j)),
        ],
        out_specs=pl.BlockSpec((t, LANES), lambda j: (0, j)),
        out_shape=jax.ShapeDtypeStruct((t, CONV_W), BF16),
        compiler_params=_params("parallel"),
    )(proj, proj, proj, conv_w)


def _conv_bwd(dmix, proj, conv_w, *, name):
    t = proj.shape[0]
    nb = CONV_W // LANES

    def body(dy_ref, b_ref, c_ref, h_ref, w_ref, db_ref, dc_ref, dh_ref, dw_ref):
        row = lax.broadcasted_iota(jnp.int32, (t, LANES), 0)
        ac, ah = c_ref[...], h_ref[...]
        z = ac * ah
        w = w_ref[...]
        z1 = _shift_down(z, 1, row)
        z2 = _shift_down(z, 2, row)
        conv = w[2:3, :] * z + w[1:2, :] * z1 + w[0:1, :] * z2
        dy = dy_ref[...]
        db_ref[...] = (dy * conv).astype(db_ref.dtype)
        dconv = dy * b_ref[...]
        dz = w[2:3, :] * dconv + w[1:2, :] * _shift_up(dconv, 1, row, t) + w[0:1, :] * _shift_up(dconv, 2, row, t)
        dc_ref[...] = (dz * ah).astype(dc_ref.dtype)
        dh_ref[...] = (dz * ac).astype(dh_ref.dtype)
        dw_ref[...] = jnp.zeros_like(dw_ref)
        dw_ref[0:1, :] = jnp.sum(dconv * z2, axis=0, keepdims=True)
        dw_ref[1:2, :] = jnp.sum(dconv * z1, axis=0, keepdims=True)
        dw_ref[2:3, :] = jnp.sum(dconv * z, axis=0, keepdims=True)

    col = lambda off: pl.BlockSpec((t, LANES), lambda j: (0, off + j))
    return pl.pallas_call(
        body,
        name=name,
        grid=(nb,),
        in_specs=[col(0), col(0), col(nb), col(2 * nb), pl.BlockSpec((CONV_TAPS, LANES), lambda j: (0, j))],
        out_specs=[col(0), col(0), col(0), pl.BlockSpec((8, LANES), lambda j: (0, j))],
        out_shape=[jax.ShapeDtypeStruct((t, CONV_W), BF16)] * 3 + [jax.ShapeDtypeStruct((8, CONV_W), F32)],
        compiler_params=_params("parallel"),
    )(dmix, proj, proj, proj, conv_w)


_QK_BLOCK = 256


def _qk_prep(proj, gq, gk, *, name, br=512):
    t = proj.shape[0]
    br = min(br, t)
    nb = ATTN_W // _QK_BLOCK
    q0 = (3 * CONV_W) // _QK_BLOCK

    def body(q_ref, k_ref, v_ref, gq_ref, gk_ref, qo_ref, ko_ref, vo_ref):
        q = q_ref[...]
        k = k_ref[...]
        rq = lax.rsqrt(_group_mean(q * q, _QK_BLOCK) + EPS)
        rk = lax.rsqrt(_group_mean(k * k, _QK_BLOCK) + EPS)
        qo_ref[...] = ((q * rq * gq_ref[...]).astype(BF16) * QK_SCALE).astype(qo_ref.dtype)
        ko_ref[...] = (k * rk * gk_ref[...]).astype(ko_ref.dtype)
        vo_ref[...] = v_ref[...].astype(vo_ref.dtype)

    col = lambda off: pl.BlockSpec((br, _QK_BLOCK), lambda i, j: (i, off + j))
    vec = pl.BlockSpec((1, _QK_BLOCK), lambda i, j: (0, 0))
    return pl.pallas_call(
        body,
        name=name,
        grid=(t // br, nb),
        in_specs=[col(q0), col(q0 + nb), col(q0 + 2 * nb), vec, vec],
        out_specs=[col(0)] * 3,
        out_shape=[jax.ShapeDtypeStruct((t, ATTN_W), BF16)] * 3,
        compiler_params=_params("parallel", "parallel"),
    )(proj, proj, proj, gq, gk)


def _qk_prep_bwd(dqs, dkn, dv, proj, gq, gk, *, name, br=512):
    t = proj.shape[0]
    br = min(br, t)
    nb = ATTN_W // _QK_BLOCK
    q0 = (3 * CONV_W) // _QK_BLOCK

    def norm_bwd(dy, x, g):
        r = lax.rsqrt(_group_mean(x * x, _QK_BLOCK) + EPS)
        xhat = x * r
        dxhat = dy * g
        dx = r * (dxhat - xhat * _group_mean(dxhat * xhat, _QK_BLOCK))
        return dx, jnp.sum(dy * xhat, axis=0, keepdims=True)

    def body(dq_ref, dk_ref, dv_ref, q_ref, k_ref, gq_ref, gk_ref, oq_ref, ok_ref, ov_ref, dgq_ref, dgk_ref):
        dq, dgq = norm_bwd(dq_ref[...] * QK_SCALE, q_ref[...], gq_ref[...])
        dk, dgk = norm_bwd(dk_ref[...], k_ref[...], gk_ref[...])
        oq_ref[...] = dq.astype(oq_ref.dtype)
        ok_ref[...] = dk.astype(ok_ref.dtype)
        ov_ref[...] = dv_ref[...].astype(ov_ref.dtype)

        @pl.when((pl.program_id(0) == 0) & (pl.program_id(1) == 0))
        def _():
            dgq_ref[...] = jnp.zeros_like(dgq_ref)
            dgk_ref[...] = jnp.zeros_like(dgk_ref)

        dgq_ref[...] += dgq
        dgk_ref[...] += dgk

    col = lambda off: pl.BlockSpec((br, _QK_BLOCK), lambda i, j: (i, off + j))
    vec = pl.BlockSpec((1, _QK_BLOCK), lambda i, j: (0, 0))
    return pl.pallas_call(
        body,
        name=name,
        grid=(t // br, nb),
        in_specs=[col(0), col(0), col(0), col(q0), col(q0 + nb), vec, vec],
        out_specs=[col(0), col(0), col(0), vec, vec],
        out_shape=[jax.ShapeDtypeStruct((t, ATTN_W), BF16)] * 3 + [jax.ShapeDtypeStruct((1, _QK_BLOCK), F32)] * 2,
        compiler_params=_params("arbitrary", "arbitrary"),
    )(dqs, dkn, dv, proj, proj, gq, gk)


def _split_bf16(x):
    hi = x.astype(BF16)
    return jnp.concatenate([hi, (x - hi.astype(F32)).astype(BF16)], axis=1)


def _key_order_matrix(tb, relation):
    jj = lax.broadcasted_iota(jnp.int32, (tb, tb), 0)
    ss = lax.broadcasted_iota(jnp.int32, (tb, tb), 1)
    m = relation(jj, ss).astype(BF16)
    return jnp.concatenate([m, m], axis=0)


def _log_sigmoids(z):
    lb = jnp.minimum(z, 0.0) - jnp.log(1.0 + jnp.exp(-jnp.abs(z)))
    return lb, lb - z


def _below_diagonal(tb):
    return lax.broadcasted_iota(jnp.int32, (tb, tb), 1) < lax.broadcasted_iota(jnp.int32, (tb, tb), 0)


_NT = (((1,), (1,)), ((), ()))
_TN = (((0,), (0,)), ((), ()))
_ATTN_BLOCK = 256
_ATTN_UNROLL = 2


def _attn_fwd(qs, kn, v, *, name, tb=_ATTN_BLOCK, unroll=_ATTN_UNROLL):
    t = qs.shape[0]
    tb = min(tb, t)
    assert t % tb == 0
    n_pairs = ATTN_W // LANES

    def body(q_ref, k_ref, v_ref, o_ref, lt_ref, acc_ref, carry_ref):
        qb = pl.program_id(1)
        half = lax.broadcasted_iota(jnp.int32, (1, LANES), 1) // HEAD_DIM
        later = _key_order_matrix(tb, lambda j, s: j > s)
        acc_ref[...] = jnp.zeros_like(acc_ref)
        carry_ref[...] = jnp.zeros_like(carry_ref)
        q = q_ref[...]
        qh = [jnp.where(half == h, q, jnp.zeros_like(q)) for h in range(2)]

        def tiles(kbs, diagonal):
            blk = []
            for kb in kbs:
                start = pl.multiple_of(kb * tb, tb)
                blk.append((k_ref[pl.ds(start, tb), :], v_ref[pl.ds(start, tb), :]))
            chains = [(h, j) for j in range(len(kbs)) for h in range(2)]
            z = [lax.dot_general(qh[h], blk[j][0], _NT, preferred_element_type=F32) for h, j in chains]
            causal = _below_diagonal(tb) if diagonal else None
            lb, lr = [], []
            for zi in z:
                b, r = _log_sigmoids(zi)
                lb.append(b)
                lr.append(jnp.where(causal, r, 0.0) if diagonal else r)
            suffix = [jnp.dot(_split_bf16(r), later, preferred_element_type=F32) for r in lr]
            carry = [carry_ref[0], carry_ref[1]]
            w = []
            for i, (h, j) in enumerate(chains):
                wi = jnp.exp(lb[i] + (suffix[i] + carry[h][:, 0:1]))
                w.append((jnp.where(causal, wi, 0.0) if diagonal else wi).astype(BF16))
                carry[h] = carry[h] + jnp.sum(lr[i], axis=1, keepdims=True)
            for i, (h, j) in enumerate(chains):
                vh = jnp.where(half == h, blk[j][1], jnp.zeros_like(blk[j][1]))
                acc_ref[h] += jnp.dot(w[i], vh, preferred_element_type=F32)
            carry_ref[0] = carry[0]
            carry_ref[1] = carry[1]

        tiles([qb], True)

        def step(i, _):
            kb = qb - 1 - unroll * i
            tiles([kb - u for u in range(unroll)], False)
            return 0

        lax.fori_loop(0, qb // unroll, step, 0)
        for left in range(1, unroll):

            @pl.when(qb % unroll == left)
            def _(left=left):
                tiles([left - 1 - u for u in range(left)], False)

        o_ref[...] = (acc_ref[0] + acc_ref[1]).astype(o_ref.dtype)
        lt_ref[...] = jnp.where(half == 0, carry_ref[0], carry_ref[1])

    return pl.pallas_call(
        body,
        name=name,
        grid=(n_pairs, t // tb),
        in_specs=[
            pl.BlockSpec((tb, LANES), lambda p, i: (i, p)),
            pl.BlockSpec((t, LANES), lambda p, i: (0, p)),
            pl.BlockSpec((t, LANES), lambda p, i: (0, p)),
        ],
        out_specs=[pl.BlockSpec((tb, LANES), lambda p, i: (i, p))] * 2,
        out_shape=[jax.ShapeDtypeStruct((t, ATTN_W), BF16), jax.ShapeDtypeStruct((t, ATTN_W), F32)],
        scratch_shapes=[pltpu.VMEM((2, tb, LANES), F32), pltpu.VMEM((2, tb, LANES), F32)],
        compiler_params=_params("parallel", "parallel"),
    )(qs, kn, v)


def _attn_bwd(dmix, qs, kn, v, lt, order_after, *, name, tb=_ATTN_BLOCK, unroll=_ATTN_UNROLL):
    t = qs.shape[0]
    tb = min(tb, t)
    assert t % tb == 0
    n_pairs = ATTN_W // LANES
    dy0 = CONV_W // LANES

    def body(do_ref, q_ref, k_ref, v_ref, lt_ref, order_ref, dq_ref, dk_ref, dv_ref, dqacc_ref, cc_ref, cg_ref):
        qb = pl.program_id(1)
        half = lax.broadcasted_iota(jnp.int32, (1, LANES), 1) // HEAD_DIM
        lane = lax.broadcasted_iota(jnp.int32, (tb, LANES), 1)
        upto = _key_order_matrix(tb, lambda j, s: j <= s)
        before = _key_order_matrix(tb, lambda j, s: j < s)
        q = q_ref[...]
        do = do_ref[...].astype(BF16)
        lt = lt_ref[...]
        qh = [jnp.where(half == h, q, jnp.zeros_like(q)) for h in range(2)]
        doh = [jnp.where(half == h, do, jnp.zeros_like(do)) for h in range(2)]
        lth = [jnp.sum(jnp.where(lane == h * HEAD_DIM, lt, 0.0), axis=1, keepdims=True) for h in range(2)]

        @pl.when(qb == 0)
        def _():
            dk_ref[...] = jnp.zeros_like(dk_ref)
            dv_ref[...] = jnp.zeros_like(dv_ref)

        dqacc_ref[...] = jnp.zeros_like(dqacc_ref)
        cc_ref[...] = jnp.zeros_like(cc_ref)
        cg_ref[...] = jnp.zeros_like(cg_ref)

        def tiles(kbs, diagonal):
            starts = [pl.multiple_of(kb * tb, tb) for kb in kbs]
            blk = [(k_ref[pl.ds(s, tb), :], v_ref[pl.ds(s, tb), :]) for s in starts]
            chains = [(h, j) for j in range(len(kbs)) for h in range(2)]
            z = [lax.dot_general(qh[h], blk[j][0], _NT, preferred_element_type=F32) for h, j in chains]
            da = [lax.dot_general(doh[h], jnp.where(half == h, blk[j][1], jnp.zeros_like(blk[j][1])), _NT,
                                  preferred_element_type=F32) for h, j in chains]
            causal = _below_diagonal(tb) if diagonal else None
            lb, lr = [], []
            for zi in z:
                b, r = _log_sigmoids(zi)
                lb.append(b)
                lr.append(jnp.where(causal, r, 0.0) if diagonal else r)
            prefix = [jnp.dot(_split_bf16(r), upto, preferred_element_type=F32) for r in lr]
            cc = [cc_ref[0], cc_ref[1]]
            cg = [cg_ref[0], cg_ref[1]]
            a16, g = [], []
            for i, (h, j) in enumerate(chains):
                a = jnp.exp(lb[i] - prefix[i] + (lth[h] - cc[h][:, 0:1]))
                if diagonal:
                    a = jnp.where(causal, a, 0.0)
                a16.append(a.astype(BF16))
                g.append(da[i] * a)
                cc[h] = cc[h] + jnp.sum(lr[i], axis=1, keepdims=True)
            g_before = [jnp.dot(_split_bf16(gi), before, preferred_element_type=F32) for gi in g]
            dz = []
            for i, (h, j) in enumerate(chains):
                dzi = g[i] - jnp.exp(lb[i]) * (g[i] + (g_before[i] + cg[h][:, 0:1]))
                dz.append((jnp.where(causal, dzi, 0.0) if diagonal else dzi).astype(BF16))
                cg[h] = cg[h] + jnp.sum(g[i], axis=1, keepdims=True)
            for i, (h, j) in enumerate(chains):
                kh = jnp.where(half == h, blk[j][0], jnp.zeros_like(blk[j][0]))
                dqacc_ref[h] += jnp.dot(dz[i], kh, preferred_element_type=F32)
                dk_ref[pl.ds(starts[j], tb), :] += lax.dot_general(dz[i], qh[h], _TN, preferred_element_type=F32)
                dv_ref[pl.ds(starts[j], tb), :] += lax.dot_general(a16[i], doh[h], _TN, preferred_element_type=F32)
            for h in range(2):
                cc_ref[h] = cc[h]
                cg_ref[h] = cg[h]

        def step(i, _):
            kb = unroll * i
            tiles([kb + u for u in range(unroll)], False)
            return 0

        lax.fori_loop(0, qb // unroll, step, 0)
        for left in range(1, unroll):

            @pl.when(qb % unroll == left)
            def _(left=left):
                tiles([qb - left + u for u in range(left)], False)

        tiles([qb], True)
        dq_ref[...] = dqacc_ref[0] + dqacc_ref[1]

    qblk = pl.BlockSpec((tb, LANES), lambda p, i: (i, p))
    whole = pl.BlockSpec((t, LANES), lambda p, i: (0, p))
    return pl.pallas_call(
        body,
        name=name,
        grid=(n_pairs, t // tb),
        in_specs=[pl.BlockSpec((tb, LANES), lambda p, i: (i, dy0 + p)), qblk, whole, whole, qblk,
                  pl.BlockSpec(order_after.shape, lambda p, i: (0, 0))],
        out_specs=[qblk, whole, whole],
        out_shape=[jax.ShapeDtypeStruct((t, ATTN_W), F32)] * 3,
        scratch_shapes=[pltpu.VMEM((2, tb, LANES), F32)] * 3,
        compiler_params=_params("parallel", "arbitrary"),
    )(dmix, qs, kn, v, lt, order_after)


def _sgu_weights(w_ref):
    tt = lax.broadcasted_iota(jnp.int32, (CHUNK, CHUNK), 0)
    ss = lax.broadcasted_iota(jnp.int32, (CHUNK, CHUNK), 1)
    tril = ss <= tt
    return [jnp.where(tril, w_ref[gi], 0.0).astype(BF16) for gi in range(SGU_HEADS)], tril


def _sgu_fwd(proj, g_v, w_s, b_exp, *, name):
    t = proj.shape[0]
    u0 = (3 * CONV_W + 3 * ATTN_W) // SGU_W

    def body(u_ref, v_ref, g_ref, w_ref, b_ref, o_ref):
        grp = lax.broadcasted_iota(jnp.int32, (1, SGU_W), 1) // HEAD_DIM
        u = _gelu(u_ref[...])
        vv = _gelu(v_ref[...])
        vn = (vv * lax.rsqrt(_group_mean(vv * vv, SGU_W) + EPS) * g_ref[...]).astype(BF16)
        wm, _ = _sgu_weights(w_ref)
        sv = b_ref[...]
        for gi in range(SGU_HEADS):
            sv = sv + jnp.dot(wm[gi], jnp.where(grp == gi, vn, jnp.zeros_like(vn)), preferred_element_type=F32)
        o_ref[...] = (u * sv).astype(o_ref.dtype)

    return pl.pallas_call(
        body,
        name=name,
        grid=(t // CHUNK,),
        in_specs=[
            pl.BlockSpec((CHUNK, SGU_W), lambda i: (i, u0)),
            pl.BlockSpec((CHUNK, SGU_W), lambda i: (i, u0 + 1)),
            pl.BlockSpec((1, SGU_W), lambda i: (0, 0)),
            pl.BlockSpec((SGU_HEADS, CHUNK, CHUNK), lambda i: (0, 0, 0)),
            pl.BlockSpec((CHUNK, SGU_W), lambda i: (0, 0)),
        ],
        out_specs=pl.BlockSpec((CHUNK, SGU_W), lambda i: (i, 0)),
        out_shape=jax.ShapeDtypeStruct((t, SGU_W), BF16),
        compiler_params=_params("parallel"),
    )(proj, proj, g_v, w_s, b_exp)


def _sgu_bwd(dmix, proj, g_v, w_s, b_exp, *, name):
    t = proj.shape[0]
    u0 = (3 * CONV_W + 3 * ATTN_W) // SGU_W
    dy0 = (CONV_W + ATTN_W) // SGU_W

    def body(dy_ref, u_ref, v_ref, g_ref, w_ref, b_ref, du_ref, dv_ref, dg_ref, dw_ref, db_ref):
        grp = lax.broadcasted_iota(jnp.int32, (1, SGU_W), 1) // HEAD_DIM
        cu, cv = u_ref[...], v_ref[...]
        u = _gelu(cu)
        vv = _gelu(cv)
        r = lax.rsqrt(_group_mean(vv * vv, SGU_W) + EPS)
        xhat = vv * r
        gain = g_ref[...]
        vn = (xhat * gain).astype(BF16)
        wm, tril = _sgu_weights(w_ref)
        vng = [jnp.where(grp == gi, vn, jnp.zeros_like(vn)) for gi in range(SGU_HEADS)]
        sv = b_ref[...]
        for gi in range(SGU_HEADS):
            sv = sv + jnp.dot(wm[gi], vng[gi], preferred_element_type=F32)
        dy = dy_ref[...]
        du_ref[...] = (dy * sv * _gelu_grad(cu)).astype(du_ref.dtype)
        dsv = dy * u
        dsv16 = dsv.astype(BF16)

        @pl.when(pl.program_id(0) == 0)
        def _():
            dg_ref[...] = jnp.zeros_like(dg_ref)
            dw_ref[...] = jnp.zeros_like(dw_ref)
            db_ref[...] = jnp.zeros_like(db_ref)

        db_ref[...] += dsv
        dvn = jnp.zeros_like(dsv)
        for gi in range(SGU_HEADS):
            dw = lax.dot_general(dsv16, vng[gi], _NT, preferred_element_type=F32)
            dw_ref[gi] += jnp.where(tril, dw, 0.0)
            dvn_g = lax.dot_general(wm[gi], dsv16, _TN, preferred_element_type=F32)
            dvn = jnp.where(grp == gi, dvn_g, dvn)
        dg_ref[...] += jnp.sum(dvn * xhat, axis=0, keepdims=True)
        dxhat = dvn * gain
        dvv = r * (dxhat - xhat * _group_mean(dxhat * xhat, SGU_W))
        dv_ref[...] = (dvv * _gelu_grad(cv)).astype(dv_ref.dtype)

    return pl.pallas_call(
        body,
        name=name,
        grid=(t // CHUNK,),
        in_specs=[
            pl.BlockSpec((CHUNK, SGU_W), lambda i: (i, dy0)),
            pl.BlockSpec((CHUNK, SGU_W), lambda i: (i, u0)),
            pl.BlockSpec((CHUNK, SGU_W), lambda i: (i, u0 + 1)),
            pl.BlockSpec((1, SGU_W), lambda i: (0, 0)),
            pl.BlockSpec((SGU_HEADS, CHUNK, CHUNK), lambda i: (0, 0, 0)),
            pl.BlockSpec((CHUNK, SGU_W), lambda i: (0, 0)),
        ],
        out_specs=[
            pl.BlockSpec((CHUNK, SGU_W), lambda i: (i, 0)),
            pl.BlockSpec((CHUNK, SGU_W), lambda i: (i, 0)),
            pl.BlockSpec((1, SGU_W), lambda i: (0, 0)),
            pl.BlockSpec((SGU_HEADS, CHUNK, CHUNK), lambda i: (0, 0, 0)),
            pl.BlockSpec((CHUNK, SGU_W), lambda i: (0, 0)),
        ],
        out_shape=[
            jax.ShapeDtypeStruct((t, SGU_W), BF16),
            jax.ShapeDtypeStruct((t, SGU_W), BF16),
            jax.ShapeDtypeStruct((1, SGU_W), F32),
            jax.ShapeDtypeStruct((SGU_HEADS, CHUNK, CHUNK), F32),
            jax.ShapeDtypeStruct((CHUNK, SGU_W), F32),
        ],
        compiler_params=_params("arbitrary"),
    )(dmix, proj, proj, g_v, w_s, b_exp)


def _ple_bwd(dh, gate, pp, order_after, *, name, br=512):
    t, d = dh.shape
    br = min(br, t)

    def body(dh_ref, g_ref, p_ref, order_ref, dpre_ref, dpp_ref):
        dhv, g = dh_ref[...], g_ref[...]
        dpre_ref[...] = (dhv * p_ref[...] * g * (1.0 - g)).astype(dpre_ref.dtype)
        dpp_ref[...] = (dhv * g).astype(dpp_ref.dtype)

    row = pl.BlockSpec((br, d), lambda i: (i, 0))
    return pl.pallas_call(
        body,
        name=name,
        grid=(t // br,),
        in_specs=[row] * 3 + [pl.BlockSpec(order_after.shape, lambda i: (0, 0))],
        out_specs=[row] * 2,
        out_shape=[jax.ShapeDtypeStruct((t, d), BF16)] * 2,
        compiler_params=_params("parallel"),
    )(dh, gate, pp, order_after)


def _loss_head(y, target, *, name, br=512):
    t, d = y.shape
    br = min(br, t)

    def body(y_ref, t_ref, dy_ref, loss_ref):
        err = y_ref[...] - t_ref[...]
        dy_ref[...] = err * (1.0 / d)

        @pl.when(pl.program_id(0) == 0)
        def _():
            loss_ref[...] = jnp.zeros_like(loss_ref)

        loss_ref[...] += 0.5 * jnp.sum(jnp.sum(err * err, axis=1, keepdims=True) * (1.0 / d), axis=0, keepdims=True)

    row = pl.BlockSpec((br, d), lambda i: (i, 0))
    return pl.pallas_call(
        body,
        name=name,
        grid=(t // br,),
        in_specs=[row, row],
        out_specs=[row, pl.BlockSpec((8, LANES), lambda i: (0, 0))],
        out_shape=[jax.ShapeDtypeStruct((t, d), F32), jax.ShapeDtypeStruct((8, LANES), F32)],
        compiler_params=_params("arbitrary"),
    )(y, target)


def _adamw_update(w, g, m, v):
    nm = ADAM_B1 * m + (1.0 - ADAM_B1) * g
    nv = ADAM_B2 * v + (1.0 - ADAM_B2) * (g * g)
    m_hat = nm / (1.0 - ADAM_B1 ** ADAM_STEP)
    v_hat = nv / (1.0 - ADAM_B2 ** ADAM_STEP)
    return -ADAM_LR * (m_hat / (jnp.sqrt(v_hat) + ADAM_EPS) + ADAM_WD * w), nm, nv


def _adamw(w, g, m, v, *, name, br=512):
    r, c = w.shape
    br = _row_block(r, br)

    def body(w_ref, g_ref, m_ref, v_ref, d_ref, nm_ref, nv_ref):
        d_ref[...], nm_ref[...], nv_ref[...] = _adamw_update(w_ref[...], g_ref[...], m_ref[...], v_ref[...])

    row = pl.BlockSpec((br, c), lambda i: (i, 0))
    return pl.pallas_call(
        body,
        name=name,
        grid=(r // br,),
        in_specs=[row] * 4,
        out_specs=[row] * 3,
        out_shape=[jax.ShapeDtypeStruct((r, c), F32)] * 3,
        compiler_params=_params("parallel"),
    )(w, g, m, v)


def _sum_slots(x, *, name, br=512):
    n, r, c = x.shape
    br = _row_block(r, br)

    def body(x_ref, o_ref):
        acc = x_ref[0].astype(F32)
        for j in range(1, n):
            acc = acc + x_ref[j].astype(F32)
        o_ref[...] = acc

    return pl.pallas_call(
        body,
        name=name,
        grid=(r // br,),
        in_specs=[pl.BlockSpec((n, br, c), lambda i: (0, i, 0))],
        out_specs=pl.BlockSpec((br, c), lambda i: (i, 0)),
        out_shape=jax.ShapeDtypeStruct((r, c), F32),
        compiler_params=_params("parallel"),
    )(x)


_ADAMW_BLOCK_ELEMS = 128 * 1024


def _adamw_reduce(w, arrived, m, v, *, name):
    depth, r, c = w.shape
    br = _row_block(r, max(BF16_TILE_ROWS, _ADAMW_BLOCK_ELEMS // (-(-c // LANES) * LANES)))

    def body(w_ref, m_ref, v_ref, *rest):
        parts, (g_ref, d_ref, nm_ref, nv_ref) = rest[:depth], rest[depth:]
        for li in range(depth):

            @pl.when(pl.program_id(0) == li)
            def _(li=li):
                g = parts[li][0].astype(F32)
                for j in range(1, N_DEV):
                    g = g + parts[li][j].astype(F32)
                g_ref[...] = g
                d_ref[...], nm_ref[...], nv_ref[...] = _adamw_update(w_ref[...], g, m_ref[...], v_ref[...])

    cur = pl.BlockSpec((None, br, c), lambda l, i: (l, i, 0))
    slots = [pl.BlockSpec((N_DEV, br, c), lambda l, i, li=li: (0, jnp.where(l == li, i, 0), 0)) for li in range(depth)]
    return pl.pallas_call(
        body,
        name=name,
        grid=(depth, r // br),
        in_specs=[cur, cur, cur] + slots,
        out_specs=[cur] * 4,
        out_shape=[jax.ShapeDtypeStruct((depth, r, c), F32)] * 4,
        compiler_params=_params("arbitrary", "arbitrary"),
    )(w, m, v, *arrived)


def _my_place():
    return lax.axis_index("x"), lax.axis_index("y"), lax.axis_index("c")


def _flip(v, bit):
    return 1 - v if bit else v


def _slot_of(px, py, pc):
    return 4 * px + 2 * py + pc


_ANY = pl.BlockSpec(memory_space=pl.ANY)


_HBM = pl.BlockSpec(memory_space=pltpu.HBM)
_SEM = pl.BlockSpec(memory_space=pltpu.SEMAPHORE)
_DATAFLOW = pltpu.SideEffectType.DATAFLOW_SIDE_EFFECTING


_GATHER, _GATHER_COLUMNS, _SCATTER = "gather", "gather_columns", "scatter"


def _landing_shape(a, mode):
    if mode == _SCATTER:
        return a.shape
    if mode == _GATHER_COLUMNS:
        return (a.shape[0], N_DEV * a.shape[1])
    return (N_DEV, *a.shape)


def _exchange_copies(src_refs, land_refs, send_sem, recv_sem, modes):
    mx, my, mc = _my_place()
    mine = _slot_of(mx, my, mc)
    remote, local = [], []
    for src, land, mode in zip(src_refs, land_refs, modes, strict=True):
        if mode == _GATHER_COLUMNS:
            n = land.shape[1] // N_DEV
            dst = land.at[:, pl.ds(pl.multiple_of(mine * n, LANES), n)]
        else:
            dst = land.at[mine]
        for k in range(1, N_DEV):
            peer = (_flip(mx, k & 4), _flip(my, k & 2), _flip(mc, k & 1))
            remote.append(pltpu.make_async_remote_copy(
                src_ref=src.at[_slot_of(*peer)] if mode == _SCATTER else src, dst_ref=dst,
                send_sem=send_sem, recv_sem=recv_sem, device_id=peer, device_id_type=MESH))
        local.append(pltpu.make_async_copy(src.at[mine] if mode == _SCATTER else src, dst, recv_sem))
    return remote, local


def _exchange_start(groups, after, *, name):
    sizes = [len(srcs) for srcs, _ in groups]
    n, n_sems = sum(sizes), 2 * len(groups)
    srcs = [a for arrays, _ in groups for a in arrays]
    lands = [lax.empty(_landing_shape(a, mode), a.dtype)
             for arrays, modes in groups for a, mode in zip(arrays, modes, strict=True)]
    offsets = [sum(sizes[:g]) for g in range(len(groups))]

    def body(*refs):
        sems = refs[2 * n + 1:2 * n + 1 + n_sems]
        for g, (off, size, (_, modes)) in enumerate(zip(offsets, sizes, groups)):
            remote, local = _exchange_copies(refs[off:off + size], refs[n + off:n + off + size], sems[2 * g],
                                             sems[2 * g + 1], modes)
            for cp in remote + local:
                cp.start()
        refs[-1][...] = jnp.zeros_like(refs[-1])

    thru = [pltpu.HBM(a.shape, a.dtype) for a in (*srcs, *lands)]
    out = pl.pallas_call(
        body,
        name=name,
        in_specs=[_HBM] * (2 * n) + [_ANY],
        out_specs=(*[_SEM] * n_sems, *[_HBM] * (2 * n), pl.BlockSpec(memory_space=pltpu.VMEM)),
        out_shape=(*[pltpu.SemaphoreType.DMA(())] * n_sems, *thru, jax.ShapeDtypeStruct((8, LANES), F32)),
        input_output_aliases={i: n_sems + i for i in range(2 * n)},
        compiler_params=pltpu.CompilerParams(has_side_effects=_DATAFLOW),
    )(*[pltpu.with_memory_space_constraint(a, pltpu.HBM) for a in (*srcs, *lands)], after)
    sems, arrays = out[:n_sems], out[n_sems:-1]
    started = [(sems[2 * g], sems[2 * g + 1], *arrays[off:off + size], *arrays[n + off:n + off + size])
               for g, (off, size) in enumerate(zip(offsets, sizes))]
    return started, out[-1]


def _exchange_wait(started, after, *, modes, name):
    send_sem, recv_sem, *thru = started
    n = len(thru) // 2

    def body(*refs):
        remote, local = _exchange_copies(refs[:n], refs[n:2 * n], refs[2 * n], refs[2 * n + 1], modes)
        for cp in remote:
            cp.wait_send()
            cp.wait_recv()
        for cp in local:
            cp.wait()

    out = pl.pallas_call(
        body,
        name=name,
        in_specs=[_HBM] * (2 * n) + [_SEM, _SEM, _ANY],
        out_specs=[_HBM] * (2 * n),
        out_shape=[pltpu.HBM(a.shape, a.dtype) for a in thru],
        input_output_aliases={i: i for i in range(2 * n)},
        compiler_params=pltpu.CompilerParams(has_side_effects=_DATAFLOW),
    )(*thru, send_sem, recv_sem, after)
    return out[n:]


_PACK_ROWS = BF16_TILE_ROWS


def _pack(arrays, dtype):
    flat = [a.astype(dtype).reshape(-1) for a in arrays]
    total = sum(f.shape[0] for f in flat)
    padded = -(-total // (LANES * _PACK_ROWS)) * (LANES * _PACK_ROWS)
    if padded > total:
        flat.append(jnp.zeros((padded - total,), dtype))
    return jnp.concatenate(flat).reshape(-1, LANES)


def _unpack(packed, shapes, lead=()):
    flat = packed.reshape(*lead, -1)
    out, off = [], 0
    for s in shapes:
        size = math.prod(s)
        out.append(flat[..., off:off + size].reshape(*lead, *s))
        off += size
    return out


def _gather_columns(g):
    return jnp.moveaxis(g, 0, 1).reshape(g.shape[1], -1)


def _split_columns(w):
    return jnp.moveaxis(w.reshape(w.shape[0], N_DEV, -1), 1, 0)


def _split_rows(w):
    return w.reshape(N_DEV, w.shape[0] // N_DEV, w.shape[1])


_GATHER_GROUPS = (("w_in", "conv_w"), ("w_out",), ("w_ff1",), ("w_ff2",), ("w_ple_gate", "w_ple_proj"))
_REST = ("w_out", "w_ff1", "w_ff2", "w_ple_gate", "w_ple_proj")
_BIG = ("w_in",) + _REST
_GATHER_MODE = dict(w_in=_GATHER, conv_w=_GATHER, w_out=_GATHER, w_ff1=_GATHER_COLUMNS, w_ff2=_GATHER,
                    w_ple_gate=_GATHER, w_ple_proj=_GATHER_COLUMNS)
_RELAYOUT_AFTER_GATHER = ("w_in", "conv_w")
_SMALL = ("norm1_g", "q_norm_g", "k_norm_g", "sgu_norm_g", "sgu_w", "sgu_b", "norm2_g", "norm3_g")
_ORDER = ("norm1_g", "w_in", "conv_w", "q_norm_g", "k_norm_g", "sgu_norm_g", "sgu_w", "sgu_b", "w_out", "norm2_g",
          "w_ff1", "w_ff2", "norm3_g", "w_ple_gate", "w_ple_proj")


def _whole_matrices(names, landed):
    return {k: _gather_columns(g) if k in _RELAYOUT_AFTER_GATHER else g.reshape(-1, g.shape[-1])
            for k, g in zip(names, landed, strict=True)}


def _layer_forward(h0, p16, s, li, fetch, w_first, fetch_next):
    nm = lambda k: f"{k}_l{li}"
    t = h0.shape[0]
    hn1 = _rms_fwd(h0, s["norm1_g"], name=nm("rms1"))
    w = dict(w_first) if w_first is not None else fetch(0, hn1)
    proj = _matmul(hn1, w["w_in"], name=nm("proj"), bm=t, bn=256)
    y_a = _conv_fwd(proj, w["conv_w"], name=nm("conv"))
    qs, kn, v = _qk_prep(proj, s["gq"], s["gk"], name=nm("qkprep"))
    y_b, lt = _attn_fwd(qs, kn, v, name=nm("attn"))
    y_c = _sgu_fwd(proj, s["sgu_norm_g"], s["sgu_w"], s["b_exp"], name=nm("sgu"))
    mix = jnp.concatenate([y_a, y_b, y_c], axis=1)
    w.update(fetch(1, y_b))
    h1 = _matmul(mix, w["w_out"], name=nm("out"), bm=t, bn=256, extras=(h0,), epilogue=lambda acc, r: (r + acc,))
    hn2 = _rms_fwd(h1, s["norm2_g"], name=nm("rms2"))
    w.update(fetch(2, hn2))
    u, f = _matmul(hn2, w["w_ff1"], name=nm("ff1"), bm=t, bn=512, out_dtypes=(F32, BF16),
                   epilogue=lambda acc: (acc, jnp.square(jnp.maximum(acc, 0.0))))
    w.update(fetch(3, f))
    h2 = _matmul(f, w["w_ff2"], name=nm("ff2"), bm=512, bn=512, extras=(h1,), epilogue=lambda acc, r: (r + acc,))
    hn3 = _rms_fwd(h2, s["norm3_g"], name=nm("rms3"))
    w.update(fetch(4, hn3))
    w_next = fetch_next(hn3)
    pp = _matmul(p16, w["w_ple_proj"], name=nm("pleproj"), bm=t, bn=512)

    def gate_epilogue(acc, pp_blk, h_blk):
        gate = jax.nn.sigmoid(acc)
        return h_blk + gate * pp_blk, gate

    h3, gate = _matmul(hn3, w["w_ple_gate"], name=nm("plegate"), bm=t, bn=256, out_dtypes=(F32, F32),
                       extras=(pp, h2), epilogue=gate_epilogue)
    saved = dict(h0=h0, hn1=hn1, proj=proj, qs=qs, kn=kn, v=v, lt=lt, mix=mix, h1=h1, hn2=hn2, u=u, f=f, h2=h2,
                 hn3=hn3, pp=pp, gate=gate, p16=p16)
    return h3, w, w_next, saved


def _layer_backward(dh3, a, w, s, li, order_after, start_rest):
    nm = lambda k: f"{k}_bwd_l{li}"
    t = dh3.shape[0]
    dpre, dpp = _ple_bwd(dh3, a["gate"], a["pp"], order_after, name=nm("ple"))
    g_gate = _weight_grad(a["hn3"], dpre, name=nm("dwgate"))
    g_proj = _weight_grad(a["p16"], dpp, name=nm("dwproj"), column_shards=True)
    dh2, dh2_16, g_n3 = _matmul_rms_bwd(dpre, w["w_ple_gate"], a["h2"], s["norm3_g"], dh3, name=nm("dh2"))
    du = _matmul(dh2_16, w["w_ff2"], name=nm("du"), tb=True, bm=t, bn=512, out_dtypes=(BF16,), extras=(a["u"],),
                 epilogue=lambda acc, u: (acc * (2.0 * jnp.maximum(u, 0.0)),))
    g_ff2 = _weight_grad(a["f"], dh2_16, name=nm("dwff2"))
    g_ff1 = _weight_grad(a["hn2"], du, name=nm("dwff1"), column_shards=True)
    dh1, dh1_16, g_n2 = _matmul_rms_bwd(du, w["w_ff1"], a["h1"], s["norm2_g"], dh2, name=nm("dh1"))
    dmix = _matmul(dh1_16, w["w_out"], name=nm("dmix"), tb=True, bm=t, bn=256)
    g_out = _weight_grad(a["mix"], dh1_16, name=nm("dwout"))
    started = start_rest(dict(w_out=_split_rows(g_out), w_ff1=g_ff1, w_ff2=_split_rows(g_ff2),
                              w_ple_gate=_split_rows(g_gate), w_ple_proj=g_proj), dmix)
    d_b, d_c, d_h, g_conv = _conv_bwd(dmix, a["proj"], w["conv_w"], name=nm("conv"))
    dqs, dkn, dv = _attn_bwd(dmix, a["qs"], a["kn"], a["v"], a["lt"], started, name=nm("attn"))
    d_q, d_k, d_v, g_q, g_k = _qk_prep_bwd(dqs, dkn, dv, a["proj"], s["gq"], s["gk"], name=nm("qkprep"))
    d_cu, d_cv, g_sn, g_sw, g_sb = _sgu_bwd(dmix, a["proj"], s["sgu_norm_g"], s["sgu_w"], s["b_exp"], name=nm("sgu"))
    dproj = jnp.concatenate([d_b, d_c, d_h, d_q, d_k, d_v, d_cu, d_cv], axis=1)
    g_in = _weight_grad(a["hn1"], dproj, name=nm("dwin"))
    dh0, _, g_n1 = _matmul_rms_bwd(dproj, w["w_in"], a["h0"], s["norm1_g"], dh1, name=nm("dh0"))
    n_tiles = g_q.shape[1] // HEAD_DIM
    small = dict(
        norm1_g=g_n1[0], norm2_g=g_n2[0], norm3_g=g_n3[0],
        q_norm_g=g_q.reshape(n_tiles, HEAD_DIM).sum(0), k_norm_g=g_k.reshape(n_tiles, HEAD_DIM).sum(0),
        sgu_norm_g=g_sn[0], sgu_w=g_sw, sgu_b=g_sb.reshape(CHUNK, SGU_HEADS, HEAD_DIM).sum(-1).T,
        conv_w=g_conv[:CONV_TAPS],
    )
    return dh0, _split_columns(g_in), small


def kernel(x, p, norm1_g, w_in, conv_w, q_norm_g, k_norm_g, sgu_norm_g, sgu_w, sgu_b, w_out, norm2_g, w_ff1, w_ff2, norm3_g, w_ple_gate, w_ple_proj, loss_target, m_norm1_g, m_w_in, m_conv_w, m_q_norm_g, m_k_norm_g, m_sgu_norm_g, m_sgu_w, m_sgu_b, m_w_out, m_norm2_g, m_w_ff1, m_w_ff2, m_norm3_g, m_w_ple_gate, m_w_ple_proj, v_norm1_g, v_w_in, v_conv_w, v_q_norm_g, v_k_norm_g, v_sgu_norm_g, v_sgu_w, v_sgu_b, v_w_out, v_norm2_g, v_w_ff1, v_w_ff2, v_norm3_g, v_w_ple_gate, v_w_ple_proj):
    weights = dict(norm1_g=norm1_g, w_in=w_in, conv_w=conv_w, q_norm_g=q_norm_g, k_norm_g=k_norm_g,
                   sgu_norm_g=sgu_norm_g, sgu_w=sgu_w, sgu_b=sgu_b, w_out=w_out, norm2_g=norm2_g, w_ff1=w_ff1,
                   w_ff2=w_ff2, norm3_g=norm3_g, w_ple_gate=w_ple_gate, w_ple_proj=w_ple_proj)
    mom = dict(norm1_g=m_norm1_g, w_in=m_w_in, conv_w=m_conv_w, q_norm_g=m_q_norm_g, k_norm_g=m_k_norm_g,
               sgu_norm_g=m_sgu_norm_g, sgu_w=m_sgu_w, sgu_b=m_sgu_b, w_out=m_w_out, norm2_g=m_norm2_g, w_ff1=m_w_ff1,
               w_ff2=m_w_ff2, norm3_g=m_norm3_g, w_ple_gate=m_w_ple_gate, w_ple_proj=m_w_ple_proj)
    var = dict(norm1_g=v_norm1_g, w_in=v_w_in, conv_w=v_conv_w, q_norm_g=v_q_norm_g, k_norm_g=v_k_norm_g,
               sgu_norm_g=v_sgu_norm_g, sgu_w=v_sgu_w, sgu_b=v_sgu_b, w_out=v_w_out, norm2_g=v_norm2_g, w_ff1=v_w_ff1,
               w_ff2=v_w_ff2, norm3_g=v_norm3_g, w_ple_gate=v_w_ple_gate, w_ple_proj=v_w_ple_proj)
    depth = norm1_g.shape[0]
    mx, my, mc = _my_place()
    me = _slot_of(mx, my, mc)

    gathers = []
    token = x[0, :8, :LANES]
    for li in range(depth):
        groups = [([weights[k][li] if k == "conv_w" else weights[k][li].astype(BF16) for k in names],
                   tuple(_GATHER_MODE[k] for k in names)) for names in _GATHER_GROUPS]
        started, token = _exchange_start(groups, token, name=f"gather_start_l{li}")
        gathers.append(started)

    small = []
    for li in range(depth):
        small.append(dict(
            norm1_g=norm1_g[li][None], norm2_g=norm2_g[li][None], norm3_g=norm3_g[li][None],
            gq=jnp.tile(q_norm_g[li], _QK_BLOCK // HEAD_DIM)[None], gk=jnp.tile(k_norm_g[li], _QK_BLOCK // HEAD_DIM)[None],
            sgu_norm_g=sgu_norm_g[li][None], sgu_w=sgu_w[li], b_exp=jnp.repeat(sgu_b[li].T, HEAD_DIM, axis=1),
        ))
    small[0]["norm1_g"] = small[0]["norm1_g"] + token[0, 0]

    h = x[0]
    saved, full = [], []
    w_first = None
    for li in range(depth):

        def fetch(g, after, li=li):
            names = _GATHER_GROUPS[g]
            landed = _exchange_wait(gathers[li][g], after, modes=tuple(_GATHER_MODE[k] for k in names),
                                    name=f"gather_{names[0]}_wait_l{li}")
            return _whole_matrices(names, landed)

        def fetch_next(after, li=li):
            return fetch(0, after, li + 1) if li + 1 < depth else None

        h, w, w_first, acts = _layer_forward(h, p[li, 0].astype(BF16), small[li], li, fetch, w_first, fetch_next)
        full.append(w)
        saved.append(acts)
    dh, loss_tile = _loss_head(h, loss_target[0], name="loss_head")
    loss = lax.psum(loss_tile[0, 0], ("x", "y", "c"))

    small_names = _SMALL + ("conv_w",)
    scatter_first, scatter_rest = [None] * depth, [None] * depth
    first_modes, rest_modes = (_SCATTER, _GATHER), (_SCATTER,) * len(_REST)
    token = loss_tile
    for li in reversed(range(depth)):

        def start_rest(parts, after, li=li):
            (scatter_rest[li],), started = _exchange_start([([parts[k] for k in _REST], rest_modes)], after,
                                                           name=f"scatter_rest_start_l{li}")
            return started

        dh, g_in, small_grads = _layer_backward(dh, saved[li], full[li], small[li], li, token, start_rest)
        small_shapes = [small_grads[k].shape for k in small_names]
        (scatter_first[li],), token = _exchange_start(
            [([g_in, _pack([small_grads[k] for k in small_names], F32)], first_modes)], dh,
            name=f"scatter_first_start_l{li}")
    grad_x = dh[None]

    grads, delta, new_m, new_v = {}, {}, {}, {}
    arrived = {k: [None] * depth for k in _BIG}
    for li in reversed(range(depth)):
        landed = _exchange_wait(scatter_rest[li], token, modes=rest_modes, name=f"scatter_rest_wait_l{li}")
        for k, g in zip(_REST, landed, strict=True):
            arrived[k][li] = g
    for k in _REST:
        grads[k], delta[k], new_m[k], new_v[k] = _adamw_reduce(weights[k], arrived[k], mom[k], var[k], name=f"adamw_{k}")
    small_sums = [None] * depth
    updated = jnp.stack([delta[k][0, 0, :1] for k in _REST])
    for li in reversed(range(depth)):
        arrived["w_in"][li], small_parts = _exchange_wait(scatter_first[li], updated, modes=first_modes,
                                                          name=f"scatter_first_wait_l{li}")
        small_sums[li] = _unpack(_sum_slots(small_parts, name=f"sum_small_grads_l{li}"), small_shapes)
    grads["w_in"], delta["w_in"], new_m["w_in"], new_v["w_in"] = _adamw_reduce(
        w_in, arrived["w_in"], mom["w_in"], var["w_in"], name="adamw_w_in")
    for i, k in enumerate(small_names):
        grads[k] = jnp.stack([small_sums[li][i] for li in range(depth)])
    n_conv = conv_w.shape[2]
    grads["conv_w"] = lax.dynamic_slice_in_dim(grads["conv_w"], me * n_conv, n_conv, axis=2)
    for k in small_names:
        as_rows = lambda a: a.reshape(-1, a.shape[-1])
        outs = _adamw(as_rows(weights[k]), as_rows(grads[k]), as_rows(mom[k]), as_rows(var[k]), name=f"adamw_{k}")
        delta[k], new_m[k], new_v[k] = (o.reshape(weights[k].shape) for o in outs)

    return (loss, grad_x, *[grads[k] for k in _ORDER], *[delta[k] for k in _ORDER],
            *[new_m[k] for k in _ORDER], *[new_v[k] for k in _ORDER])
```

```python
import math

import jax
import jax.numpy as jnp
from jax import lax
from jax.experimental import pallas as pl
from jax.experimental.pallas import tpu as pltpu

F32 = jnp.float32
BF16 = jnp.bfloat16

N_DEV = 8
HEAD_DIM = 64
CONV_W = 256
ATTN_W = 512
SGU_W = 256
SGU_HEADS = 4
CHUNK = 128
CONV_TAPS = 3
EPS = 1e-6
QK_SCALE = HEAD_DIM ** -0.5

ADAM_LR = 0.001
ADAM_B1 = 0.9
ADAM_B2 = 0.999
ADAM_EPS = 1e-08
ADAM_WD = 0.01
ADAM_STEP = 10

LANES = 128
BF16_TILE_ROWS = 16
VMEM_LIMIT_BYTES = 56 * 1024 * 1024
MESH = pl.DeviceIdType.MESH


def _params(*sem):
    return pltpu.CompilerParams(dimension_semantics=sem, vmem_limit_bytes=VMEM_LIMIT_BYTES)


def _row_block(rows, cap):
    if rows <= cap:
        return rows
    return max(b for b in range(BF16_TILE_ROWS, cap + 1, BF16_TILE_ROWS) if rows % b == 0)


def _matmul(a, b, *, name, tb=False, bm=512, bn=512, out_dtypes=(F32,), epilogue=None, extras=()):
    m, k = a.shape
    n = b.shape[0] if tb else b.shape[1]
    assert k == (b.shape[1] if tb else b.shape[0])
    bm, bn = min(bm, m), min(bn, n)
    assert m % bm == 0 and n % bn == 0
    a_spec = pl.BlockSpec((bm, k), lambda i, j: (i, 0))
    b_spec = pl.BlockSpec((bn, k), lambda i, j: (j, 0)) if tb else pl.BlockSpec((k, bn), lambda i, j: (0, j))
    dims = (((1,), (1 if tb else 0,)), ((), ()))
    n_ex = len(extras)
    for e in extras:
        assert e.shape == (m, n), (e.shape, m, n)

    def body(a_ref, b_ref, *rest):
        outs = rest[n_ex:]
        acc = lax.dot_general(a_ref[...], b_ref[...], dims, preferred_element_type=F32)
        res = (acc,) if epilogue is None else epilogue(acc, *[e[...] for e in rest[:n_ex]])
        for o_ref, r in zip(outs, res, strict=True):
            o_ref[...] = r.astype(o_ref.dtype)

    tile = pl.BlockSpec((bm, bn), lambda i, j: (i, j))
    out = pl.pallas_call(
        body,
        name=name,
        grid=(m // bm, n // bn),
        in_specs=[a_spec, b_spec] + [tile] * n_ex,
        out_specs=[tile] * len(out_dtypes),
        out_shape=[jax.ShapeDtypeStruct((m, n), d) for d in out_dtypes],
        compiler_params=_params("parallel", "parallel"),
    )(a, b, *extras)
    return out[0] if len(out_dtypes) == 1 else out


_WEIGHT_GRAD_ACC_ELEMS = 1024 * 1024


def _weight_grad(x, dy, *, name, column_shards=False):
    t, m = x.shape
    n = dy.shape[1]
    bm = m if m <= 2 * LANES else min(m // 2, max(LANES, _WEIGHT_GRAD_ACC_ELEMS // n // LANES * LANES))
    assert m % bm == 0
    ns = n // N_DEV

    def body(x_ref, dy_ref, o_ref):
        acc = lax.dot_general(x_ref[...], dy_ref[...], _TN, preferred_element_type=F32)
        if column_shards:
            for s in range(N_DEV):
                o_ref[s] = acc[:, s * ns:(s + 1) * ns].astype(o_ref.dtype)
        else:
            o_ref[...] = acc.astype(o_ref.dtype)

    if column_shards:
        out_spec, out_dims = pl.BlockSpec((N_DEV, bm, ns), lambda i: (0, i, 0)), (N_DEV, m, ns)
    else:
        out_spec, out_dims = pl.BlockSpec((bm, n), lambda i: (i, 0)), (m, n)
    return pl.pallas_call(
        body,
        name=name,
        grid=(m // bm,),
        in_specs=[pl.BlockSpec((t, bm), lambda i: (0, i)), pl.BlockSpec((t, n), lambda i: (0, 0))],
        out_specs=out_spec,
        out_shape=jax.ShapeDtypeStruct(out_dims, BF16),
        compiler_params=_params("parallel"),
    )(x, dy)


def _rms_fwd(h, g, *, name, br=512):
    t, d = h.shape
    br = min(br, t)

    def body(h_ref, g_ref, o_ref):
        x = h_ref[...]
        r = lax.rsqrt(jnp.mean(x * x, axis=-1, keepdims=True) + EPS)
        o_ref[...] = (x * r * g_ref[...]).astype(o_ref.dtype)

    return pl.pallas_call(
        body,
        name=name,
        grid=(t // br,),
        in_specs=[pl.BlockSpec((br, d), lambda i: (i, 0)), pl.BlockSpec((1, d), lambda i: (0, 0))],
        out_specs=pl.BlockSpec((br, d), lambda i: (i, 0)),
        out_shape=jax.ShapeDtypeStruct((t, d), BF16),
        compiler_params=_params("parallel"),
    )(h, g)


def _matmul_rms_bwd(dz, w, h, g, dres, *, name):
    t, d = h.shape
    k = dz.shape[1]
    br = min(t, 512 if k <= d else 256)

    def body(dz_ref, w_ref, h_ref, g_ref, dres_ref, dh_ref, dh16_ref, dg_ref):
        x = h_ref[...]
        dyv = lax.dot_general(dz_ref[...], w_ref[...], _NT, preferred_element_type=F32)
        r = lax.rsqrt(jnp.mean(x * x, axis=-1, keepdims=True) + EPS)
        xhat = x * r
        dxhat = dyv * g_ref[...]
        dh = dres_ref[...] + r * (dxhat - xhat * jnp.mean(dxhat * xhat, axis=-1, keepdims=True))
        dh_ref[...] = dh
        dh16_ref[...] = dh.astype(dh16_ref.dtype)

        @pl.when(pl.program_id(0) == 0)
        def _():
            dg_ref[...] = jnp.zeros_like(dg_ref)

        dg_ref[...] += jnp.sum(dyv * xhat, axis=0, keepdims=True)

    row = pl.BlockSpec((br, d), lambda i: (i, 0))
    vec = pl.BlockSpec((1, d), lambda i: (0, 0))
    return pl.pallas_call(
        body,
        name=name,
        grid=(t // br,),
        in_specs=[pl.BlockSpec((br, k), lambda i: (i, 0)), pl.BlockSpec((d, k), lambda i: (0, 0)), row, vec, row],
        out_specs=[row, row, vec],
        out_shape=[jax.ShapeDtypeStruct((t, d), F32), jax.ShapeDtypeStruct((t, d), BF16),
                   jax.ShapeDtypeStruct((1, d), F32)],
        compiler_params=_params("arbitrary"),
    )(dz, w, h, g, dres)


def _group_mean(x, width):
    grp = lax.broadcasted_iota(jnp.int32, x.shape, 1) // HEAD_DIM
    out = jnp.zeros_like(x)
    for gi in range(width // HEAD_DIM):
        m = grp == gi
        s = jnp.sum(jnp.where(m, x, 0.0), axis=1, keepdims=True)
        out = jnp.where(m, s, out)
    return out * (1.0 / HEAD_DIM)


def _gelu(x):
    return 0.5 * x * (1.0 + lax.erf(x * (2.0 ** -0.5)))


def _gelu_grad(x):
    cdf = 0.5 * (1.0 + lax.erf(x * (2.0 ** -0.5)))
    pdf = jnp.exp(-0.5 * x * x) * (1.0 / math.sqrt(2.0 * math.pi))
    return cdf + x * pdf


def _shift_down(z, s, row):
    return jnp.where(row >= s, pltpu.roll(z, s, 0), 0.0)


def _shift_up(z, s, row, t):
    return jnp.where(row < t - s, pltpu.roll(z, t - s, 0), 0.0)


def _conv_fwd(proj, conv_w, *, name):
    t = proj.shape[0]
    nb = CONV_W // LANES

    def body(b_ref, c_ref, h_ref, w_ref, o_ref):
        row = lax.broadcasted_iota(jnp.int32, (t, LANES), 0)
        z = c_ref[...] * h_ref[...]
        w = w_ref[...]
        conv = w[2:3, :] * z + w[1:2, :] * _shift_down(z, 1, row) + w[0:1, :] * _shift_down(z, 2, row)
        o_ref[...] = (b_ref[...] * conv).astype(o_ref.dtype)

    return pl.pallas_call(
        body,
        name=name,
        grid=(nb,),
        in_specs=[
            pl.BlockSpec((t, LANES), lambda j: (0, j)),
            pl.BlockSpec((t, LANES), lambda j: (0, nb + j)),
            pl.BlockSpec((t, LANES), lambda j: (0, 2 * nb + j)),
            pl.BlockSpec((CONV_TAPS, LANES), lambda j: (0, j)),
        ],
        out_specs=pl.BlockSpec((t, LANES), lambda j: (0, j)),
        out_shape=jax.ShapeDtypeStruct((t, CONV_W), BF16),
        compiler_params=_params("parallel"),
    )(proj, proj, proj, conv_w)


def _conv_bwd(dmix, proj, conv_w, *, name):
    t = proj.shape[0]
    nb = CONV_W // LANES

    def body(dy_ref, b_ref, c_ref, h_ref, w_ref, db_ref, dc_ref, dh_ref, dw_ref):
        row = lax.broadcasted_iota(jnp.int32, (t, LANES), 0)
        ac, ah = c_ref[...], h_ref[...]
        z = ac * ah
        w = w_ref[...]
        z1 = _shift_down(z, 1, row)
        z2 = _shift_down(z, 2, row)
        conv = w[2:3, :] * z + w[1:2, :] * z1 + w[0:1, :] * z2
        dy = dy_ref[...]
        db_ref[...] = (dy * conv).astype(db_ref.dtype)
        dconv = dy * b_ref[...]
        dz = w[2:3, :] * dconv + w[1:2, :] * _shift_up(dconv, 1, row, t) + w[0:1, :] * _shift_up(dconv, 2, row, t)
        dc_ref[...] = (dz * ah).astype(dc_ref.dtype)
        dh_ref[...] = (dz * ac).astype(dh_ref.dtype)
        dw_ref[...] = jnp.zeros_like(dw_ref)
        dw_ref[0:1, :] = jnp.sum(dconv * z2, axis=0, keepdims=True)
        dw_ref[1:2, :] = jnp.sum(dconv * z1, axis=0, keepdims=True)
        dw_ref[2:3, :] = jnp.sum(dconv * z, axis=0, keepdims=True)

    col = lambda off: pl.BlockSpec((t, LANES), lambda j: (0, off + j))
    return pl.pallas_call(
        body,
        name=name,
        grid=(nb,),
        in_specs=[col(0), col(0), col(nb), col(2 * nb), pl.BlockSpec((CONV_TAPS, LANES), lambda j: (0, j))],
        out_specs=[col(0), col(0), col(0), pl.BlockSpec((8, LANES), lambda j: (0, j))],
        out_shape=[jax.ShapeDtypeStruct((t, CONV_W), BF16)] * 3 + [jax.ShapeDtypeStruct((8, CONV_W), F32)],
        compiler_params=_params("parallel"),
    )(dmix, proj, proj, proj, conv_w)


_QK_BLOCK = 256


def _qk_prep(proj, gq, gk, *, name, br=512):
    t = proj.shape[0]
    br = min(br, t)
    nb = ATTN_W // _QK_BLOCK
    q0 = (3 * CONV_W) // _QK_BLOCK

    def body(q_ref, k_ref, v_ref, gq_ref, gk_ref, qo_ref, ko_ref, vo_ref):
        q = q_ref[...]
        k = k_ref[...]
        rq = lax.rsqrt(_group_mean(q * q, _QK_BLOCK) + EPS)
        rk = lax.rsqrt(_group_mean(k * k, _QK_BLOCK) + EPS)
        qo_ref[...] = ((q * rq * gq_ref[...]).astype(BF16) * QK_SCALE).astype(qo_ref.dtype)
        ko_ref[...] = (k * rk * gk_ref[...]).astype(ko_ref.dtype)
        vo_ref[...] = v_ref[...].astype(vo_ref.dtype)

    col = lambda off: pl.BlockSpec((br, _QK_BLOCK), lambda i, j: (i, off + j))
    vec = pl.BlockSpec((1, _QK_BLOCK), lambda i, j: (0, 0))
    return pl.pallas_call(
        body,
        name=name,
        grid=(t // br, nb),
        in_specs=[col(q0), col(q0 + nb), col(q0 + 2 * nb), vec, vec],
        out_specs=[col(0)] * 3,
        out_shape=[jax.ShapeDtypeStruct((t, ATTN_W), BF16)] * 3,
        compiler_params=_params("parallel", "parallel"),
    )(proj, proj, proj, gq, gk)


def _qk_prep_bwd(dqs, dkn, dv, proj, gq, gk, *, name, br=512):
    t = proj.shape[0]
    br = min(br, t)
    nb = ATTN_W // _QK_BLOCK
    q0 = (3 * CONV_W) // _QK_BLOCK

    def norm_bwd(dy, x, g):
        r = lax.rsqrt(_group_mean(x * x, _QK_BLOCK) + EPS)
        xhat = x * r
        dxhat = dy * g
        dx = r * (dxhat - xhat * _group_mean(dxhat * xhat, _QK_BLOCK))
        return dx, jnp.sum(dy * xhat, axis=0, keepdims=True)

    def body(dq_ref, dk_ref, dv_ref, q_ref, k_ref, gq_ref, gk_ref, oq_ref, ok_ref, ov_ref, dgq_ref, dgk_ref):
        dq, dgq = norm_bwd(dq_ref[...] * QK_SCALE, q_ref[...], gq_ref[...])
        dk, dgk = norm_bwd(dk_ref[...], k_ref[...], gk_ref[...])
        oq_ref[...] = dq.astype(oq_ref.dtype)
        ok_ref[...] = dk.astype(ok_ref.dtype)
        ov_ref[...] = dv_ref[...].astype(ov_ref.dtype)

        @pl.when((pl.program_id(0) == 0) & (pl.program_id(1) == 0))
        def _():
            dgq_ref[...] = jnp.zeros_like(dgq_ref)
            dgk_ref[...] = jnp.zeros_like(dgk_ref)

        dgq_ref[...] += dgq
        dgk_ref[...] += dgk

    col = lambda off: pl.BlockSpec((br, _QK_BLOCK), lambda i, j: (i, off + j))
    vec = pl.BlockSpec((1, _QK_BLOCK), lambda i, j: (0, 0))
    return pl.pallas_call(
        body,
        name=name,
        grid=(t // br, nb),
        in_specs=[col(0), col(0), col(0), col(q0), col(q0 + nb), vec, vec],
        out_specs=[col(0), col(0), col(0), vec, vec],
        out_shape=[jax.ShapeDtypeStruct((t, ATTN_W), BF16)] * 3 + [jax.ShapeDtypeStruct((1, _QK_BLOCK), F32)] * 2,
        compiler_params=_params("arbitrary", "arbitrary"),
    )(dqs, dkn, dv, proj, proj, gq, gk)


def _key_order_matrix(tb, relation):
    jj = lax.broadcasted_iota(jnp.int32, (tb, tb), 0)
    ss = lax.broadcasted_iota(jnp.int32, (tb, tb), 1)
    return relation(jj, ss).astype(BF16)


def _log_sigmoids(z):
    lb = jnp.minimum(z, 0.0) - jnp.log(1.0 + jnp.exp(-jnp.abs(z)))
    return lb, lb - z


def _below_diagonal(tb):
    return lax.broadcasted_iota(jnp.int32, (tb, tb), 1) < lax.broadcasted_iota(jnp.int32, (tb, tb), 0)


_NT = (((1,), (1,)), ((), ()))
_TN = (((0,), (0,)), ((), ()))
_ATTN_BLOCK = 256
_ATTN_FWD_UNROLL = 2
_ATTN_BWD_UNROLL = 3


def _attn_fwd(qs, kn, v, *, name, tb=_ATTN_BLOCK, unroll=_ATTN_FWD_UNROLL):
    t = qs.shape[0]
    tb = min(tb, t)
    assert t % tb == 0
    n_pairs = ATTN_W // LANES

    def body(q_ref, k_ref, v_ref, o_ref, lt_ref, acc_ref, carry_ref):
        qb = pl.program_id(1)
        half = lax.broadcasted_iota(jnp.int32, (1, LANES), 1) // HEAD_DIM
        later = _key_order_matrix(tb, lambda j, s: j > s)
        acc_ref[...] = jnp.zeros_like(acc_ref)
        carry_ref[...] = jnp.zeros_like(carry_ref)
        q = q_ref[...]
        qh = [jnp.where(half == h, q, jnp.zeros_like(q)) for h in range(2)]

        def tiles(kbs, diagonal):
            blk = []
            for kb in kbs:
                start = pl.multiple_of(kb * tb, tb)
                blk.append((k_ref[pl.ds(start, tb), :], v_ref[pl.ds(start, tb), :]))
            chains = [(h, j) for j in range(len(kbs)) for h in range(2)]
            z = [lax.dot_general(qh[h], blk[j][0], _NT, preferred_element_type=F32) for h, j in chains]
            causal = _below_diagonal(tb) if diagonal else None
            lb, lr = [], []
            for zi in z:
                b, r = _log_sigmoids(zi)
                lb.append(b)
                lr.append(jnp.where(causal, r, 0.0) if diagonal else r)
            suffix = [jnp.dot(r.astype(BF16), later, preferred_element_type=F32) for r in lr]
            carry = [carry_ref[0], carry_ref[1]]
            w = []
            for i, (h, j) in enumerate(chains):
                wi = jnp.exp(lb[i] + (suffix[i] + carry[h][:, 0:1]))
                w.append((jnp.where(causal, wi, 0.0) if diagonal else wi).astype(BF16))
                carry[h] = carry[h] + jnp.sum(lr[i], axis=1, keepdims=True)
            for i, (h, j) in enumerate(chains):
                vh = jnp.where(half == h, blk[j][1], jnp.zeros_like(blk[j][1]))
                acc_ref[h] += jnp.dot(w[i], vh, preferred_element_type=F32)
            carry_ref[0] = carry[0]
            carry_ref[1] = carry[1]

        tiles([qb], True)

        def step(i, _):
            kb = qb - 1 - unroll * i
            tiles([kb - u for u in range(unroll)], False)
            return 0

        lax.fori_loop(0, qb // unroll, step, 0)
        for left in range(1, unroll):

            @pl.when(qb % unroll == left)
            def _(left=left):
                tiles([left - 1 - u for u in range(left)], False)

        o_ref[...] = (acc_ref[0] + acc_ref[1]).astype(o_ref.dtype)
        lt_ref[...] = jnp.where(half == 0, carry_ref[0], carry_ref[1])

    return pl.pallas_call(
        body,
        name=name,
        grid=(n_pairs, t // tb),
        in_specs=[
            pl.BlockSpec((tb, LANES), lambda p, i: (i, p)),
            pl.BlockSpec((t, LANES), lambda p, i: (0, p)),
            pl.BlockSpec((t, LANES), lambda p, i: (0, p)),
        ],
        out_specs=[pl.BlockSpec((tb, LANES), lambda p, i: (i, p))] * 2,
        out_shape=[jax.ShapeDtypeStruct((t, ATTN_W), BF16), jax.ShapeDtypeStruct((t, ATTN_W), F32)],
        scratch_shapes=[pltpu.VMEM((2, tb, LANES), F32), pltpu.VMEM((2, tb, LANES), F32)],
        compiler_params=_params("parallel", "parallel"),
    )(qs, kn, v)


def _attn_bwd(dmix, qs, kn, v, lt, order_after, *, name, tb=_ATTN_BLOCK, unroll=_ATTN_BWD_UNROLL):
    t = qs.shape[0]
    tb = min(tb, t)
    assert t % tb == 0
    n_pairs = ATTN_W // LANES
    dy0 = CONV_W // LANES

    def body(do_ref, q_ref, k_ref, v_ref, lt_ref, order_ref, dq_ref, dk_ref, dv_ref, dqacc_ref, cc_ref, cg_ref):
        qb = pl.program_id(1)
        half = lax.broadcasted_iota(jnp.int32, (1, LANES), 1) // HEAD_DIM
        lane = lax.broadcasted_iota(jnp.int32, (tb, LANES), 1)
        later = _key_order_matrix(tb, lambda j, s: j > s)
        before = _key_order_matrix(tb, lambda j, s: j < s)
        q = q_ref[...]
        do = do_ref[...].astype(BF16)
        lt = lt_ref[...]
        qh = [jnp.where(half == h, q, jnp.zeros_like(q)) for h in range(2)]
        doh = [jnp.where(half == h, do, jnp.zeros_like(do)) for h in range(2)]
        lth = [jnp.sum(jnp.where(lane == h * HEAD_DIM, lt, 0.0), axis=1, keepdims=True) for h in range(2)]

        @pl.when(qb == 0)
        def _():
            dk_ref[...] = jnp.zeros_like(dk_ref)
            dv_ref[...] = jnp.zeros_like(dv_ref)

        dqacc_ref[...] = jnp.zeros_like(dqacc_ref)
        cc_ref[...] = jnp.zeros_like(cc_ref)
        cg_ref[...] = jnp.zeros_like(cg_ref)

        def tiles(kbs, diagonal):
            starts = [pl.multiple_of(kb * tb, tb) for kb in kbs]
            blk = [(k_ref[pl.ds(s, tb), :], v_ref[pl.ds(s, tb), :]) for s in starts]
            chains = [(h, j) for j in range(len(kbs)) for h in range(2)]
            z = [lax.dot_general(qh[h], blk[j][0], _NT, preferred_element_type=F32) for h, j in chains]
            da = [lax.dot_general(doh[h], jnp.where(half == h, blk[j][1], jnp.zeros_like(blk[j][1])), _NT,
                                  preferred_element_type=F32) for h, j in chains]
            causal = _below_diagonal(tb) if diagonal else None
            lb, lr = [], []
            for zi in z:
                b, r = _log_sigmoids(zi)
                lb.append(b)
                lr.append(jnp.where(causal, r, 0.0) if diagonal else r)
            suffix = [jnp.dot(r.astype(BF16), later, preferred_element_type=F32) for r in lr]
            cc = [cc_ref[0], cc_ref[1]]
            cg = [cg_ref[0], cg_ref[1]]
            a16, g = [], []
            for i, (h, j) in enumerate(chains):
                cc[h] = cc[h] + jnp.sum(lr[i], axis=1, keepdims=True)
                a = jnp.exp(lb[i] + suffix[i] + (lth[h] - cc[h][:, 0:1]))
                if diagonal:
                    a = jnp.where(causal, a, 0.0)
                a16.append(a.astype(BF16))
                g.append(da[i] * a)
            g_before = [jnp.dot(gi.astype(BF16), before, preferred_element_type=F32) for gi in g]
            dz = []
            for i, (h, j) in enumerate(chains):
                dzi = g[i] - jnp.exp(lb[i]) * (g[i] + (g_before[i] + cg[h][:, 0:1]))
                dz.append((jnp.where(causal, dzi, 0.0) if diagonal else dzi).astype(BF16))
                cg[h] = cg[h] + jnp.sum(g[i], axis=1, keepdims=True)
            for i, (h, j) in enumerate(chains):
                kh = jnp.where(half == h, blk[j][0], jnp.zeros_like(blk[j][0]))
                dqacc_ref[h] += jnp.dot(dz[i], kh, preferred_element_type=F32)
                dk_ref[pl.ds(starts[j], tb), :] += lax.dot_general(dz[i], qh[h], _TN, preferred_element_type=F32)
                dv_ref[pl.ds(starts[j], tb), :] += lax.dot_general(a16[i], doh[h], _TN, preferred_element_type=F32)
            for h in range(2):
                cc_ref[h] = cc[h]
                cg_ref[h] = cg[h]

        def step(i, _):
            kb = unroll * i
            tiles([kb + u for u in range(unroll)], False)
            return 0

        lax.fori_loop(0, qb // unroll, step, 0)
        for left in range(1, unroll):

            @pl.when(qb % unroll == left)
            def _(left=left):
                tiles([qb - left + u for u in range(left)], False)

        tiles([qb], True)
        dq_ref[...] = dqacc_ref[0] + dqacc_ref[1]

    qblk = pl.BlockSpec((tb, LANES), lambda p, i: (i, p))
    whole = pl.BlockSpec((t, LANES), lambda p, i: (0, p))
    return pl.pallas_call(
        body,
        name=name,
        grid=(n_pairs, t // tb),
        in_specs=[pl.BlockSpec((tb, LANES), lambda p, i: (i, dy0 + p)), qblk, whole, whole, qblk,
                  pl.BlockSpec(order_after.shape, lambda p, i: (0, 0))],
        out_specs=[qblk, whole, whole],
        out_shape=[jax.ShapeDtypeStruct((t, ATTN_W), F32)] * 3,
        scratch_shapes=[pltpu.VMEM((2, tb, LANES), F32)] * 3,
        compiler_params=_params("parallel", "arbitrary"),
    )(dmix, qs, kn, v, lt, order_after)


def _sgu_weights(w_ref):
    tt = lax.broadcasted_iota(jnp.int32, (CHUNK, CHUNK), 0)
    ss = lax.broadcasted_iota(jnp.int32, (CHUNK, CHUNK), 1)
    tril = ss <= tt
    return [jnp.where(tril, w_ref[gi], 0.0).astype(BF16) for gi in range(SGU_HEADS)], tril


def _sgu_fwd(proj, g_v, w_s, b_exp, *, name):
    t = proj.shape[0]
    u0 = (3 * CONV_W + 3 * ATTN_W) // SGU_W

    def body(u_ref, v_ref, g_ref, w_ref, b_ref, o_ref):
        grp = lax.broadcasted_iota(jnp.int32, (1, SGU_W), 1) // HEAD_DIM
        u = _gelu(u_ref[...])
        vv = _gelu(v_ref[...])
        vn = (vv * lax.rsqrt(_group_mean(vv * vv, SGU_W) + EPS) * g_ref[...]).astype(BF16)
        wm, _ = _sgu_weights(w_ref)
        sv = b_ref[...]
        for gi in range(SGU_HEADS):
            sv = sv + jnp.dot(wm[gi], jnp.where(grp == gi, vn, jnp.zeros_like(vn)), preferred_element_type=F32)
        o_ref[...] = (u * sv).astype(o_ref.dtype)

    return pl.pallas_call(
        body,
        name=name,
        grid=(t // CHUNK,),
        in_specs=[
            pl.BlockSpec((CHUNK, SGU_W), lambda i: (i, u0)),
            pl.BlockSpec((CHUNK, SGU_W), lambda i: (i, u0 + 1)),
            pl.BlockSpec((1, SGU_W), lambda i: (0, 0)),
            pl.BlockSpec((SGU_HEADS, CHUNK, CHUNK), lambda i: (0, 0, 0)),
            pl.BlockSpec((CHUNK, SGU_W), lambda i: (0, 0)),
        ],
        out_specs=pl.BlockSpec((CHUNK, SGU_W), lambda i: (i, 0)),
        out_shape=jax.ShapeDtypeStruct((t, SGU_W), BF16),
        compiler_params=_params("parallel"),
    )(proj, proj, g_v, w_s, b_exp)


def _sgu_bwd(dmix, proj, g_v, w_s, b_exp, *, name):
    t = proj.shape[0]
    u0 = (3 * CONV_W + 3 * ATTN_W) // SGU_W
    dy0 = (CONV_W + ATTN_W) // SGU_W

    def body(dy_ref, u_ref, v_ref, g_ref, w_ref, b_ref, du_ref, dv_ref, dg_ref, dw_ref, db_ref):
        grp = lax.broadcasted_iota(jnp.int32, (1, SGU_W), 1) // HEAD_DIM
        cu, cv = u_ref[...], v_ref[...]
        u = _gelu(cu)
        vv = _gelu(cv)
        r = lax.rsqrt(_group_mean(vv * vv, SGU_W) + EPS)
        xhat = vv * r
        gain = g_ref[...]
        vn = (xhat * gain).astype(BF16)
        wm, tril = _sgu_weights(w_ref)
        vng = [jnp.where(grp == gi, vn, jnp.zeros_like(vn)) for gi in range(SGU_HEADS)]
        sv = b_ref[...]
        for gi in range(SGU_HEADS):
            sv = sv + jnp.dot(wm[gi], vng[gi], preferred_element_type=F32)
        dy = dy_ref[...]
        du_ref[...] = (dy * sv * _gelu_grad(cu)).astype(du_ref.dtype)
        dsv = dy * u
        dsv16 = dsv.astype(BF16)

        @pl.when(pl.program_id(0) == 0)
        def _():
            dg_ref[...] = jnp.zeros_like(dg_ref)
            dw_ref[...] = jnp.zeros_like(dw_ref)
            db_ref[...] = jnp.zeros_like(db_ref)

        db_ref[...] += dsv
        dvn = jnp.zeros_like(dsv)
        for gi in range(SGU_HEADS):
            dw = lax.dot_general(dsv16, vng[gi], _NT, preferred_element_type=F32)
            dw_ref[gi] += jnp.where(tril, dw, 0.0)
            dvn_g = lax.dot_general(wm[gi], dsv16, _TN, preferred_element_type=F32)
            dvn = jnp.where(grp == gi, dvn_g, dvn)
        dg_ref[...] += jnp.sum(dvn * xhat, axis=0, keepdims=True)
        dxhat = dvn * gain
        dvv = r * (dxhat - xhat * _group_mean(dxhat * xhat, SGU_W))
        dv_ref[...] = (dvv * _gelu_grad(cv)).astype(dv_ref.dtype)

    return pl.pallas_call(
        body,
        name=name,
        grid=(t // CHUNK,),
        in_specs=[
            pl.BlockSpec((CHUNK, SGU_W), lambda i: (i, dy0)),
            pl.BlockSpec((CHUNK, SGU_W), lambda i: (i, u0)),
            pl.BlockSpec((CHUNK, SGU_W), lambda i: (i, u0 + 1)),
            pl.BlockSpec((1, SGU_W), lambda i: (0, 0)),
            pl.BlockSpec((SGU_HEADS, CHUNK, CHUNK), lambda i: (0, 0, 0)),
            pl.BlockSpec((CHUNK, SGU_W), lambda i: (0, 0)),
        ],
        out_specs=[
            pl.BlockSpec((CHUNK, SGU_W), lambda i: (i, 0)),
            pl.BlockSpec((CHUNK, SGU_W), lambda i: (i, 0)),
            pl.BlockSpec((1, SGU_W), lambda i: (0, 0)),
            pl.BlockSpec((SGU_HEADS, CHUNK, CHUNK), lambda i: (0, 0, 0)),
            pl.BlockSpec((CHUNK, SGU_W), lambda i: (0, 0)),
        ],
        out_shape=[
            jax.ShapeDtypeStruct((t, SGU_W), BF16),
            jax.ShapeDtypeStruct((t, SGU_W), BF16),
            jax.ShapeDtypeStruct((1, SGU_W), F32),
            jax.ShapeDtypeStruct((SGU_HEADS, CHUNK, CHUNK), F32),
            jax.ShapeDtypeStruct((CHUNK, SGU_W), F32),
        ],
        compiler_params=_params("arbitrary"),
    )(dmix, proj, proj, g_v, w_s, b_exp)


def _ple_bwd(dh, gate, pp, order_after, *, name, br=512):
    t, d = dh.shape
    br = min(br, t)

    def body(dh_ref, g_ref, p_ref, order_ref, dpre_ref, dpp_ref):
        dhv, g = dh_ref[...], g_ref[...]
        dpre_ref[...] = (dhv * p_ref[...] * g * (1.0 - g)).astype(dpre_ref.dtype)
        dpp_ref[...] = (dhv * g).astype(dpp_ref.dtype)

    row = pl.BlockSpec((br, d), lambda i: (i, 0))
    return pl.pallas_call(
        body,
        name=name,
        grid=(t // br,),
        in_specs=[row] * 3 + [pl.BlockSpec(order_after.shape, lambda i: (0, 0))],
        out_specs=[row] * 2,
        out_shape=[jax.ShapeDtypeStruct((t, d), BF16)] * 2,
        compiler_params=_params("parallel"),
    )(dh, gate, pp, order_after)


def _loss_head(y, target, *, name, br=512):
    t, d = y.shape
    br = min(br, t)

    def body(y_ref, t_ref, dy_ref, loss_ref):
        err = y_ref[...] - t_ref[...]
        dy_ref[...] = err * (1.0 / d)

        @pl.when(pl.program_id(0) == 0)
        def _():
            loss_ref[...] = jnp.zeros_like(loss_ref)

        loss_ref[...] += 0.5 * jnp.sum(jnp.sum(err * err, axis=1, keepdims=True) * (1.0 / d), axis=0, keepdims=True)

    row = pl.BlockSpec((br, d), lambda i: (i, 0))
    return pl.pallas_call(
        body,
        name=name,
        grid=(t // br,),
        in_specs=[row, row],
        out_specs=[row, pl.BlockSpec((8, LANES), lambda i: (0, 0))],
        out_shape=[jax.ShapeDtypeStruct((t, d), F32), jax.ShapeDtypeStruct((8, LANES), F32)],
        compiler_params=_params("arbitrary"),
    )(y, target)


def _adamw_update(w, g, m, v):
    nm = ADAM_B1 * m + (1.0 - ADAM_B1) * g
    nv = ADAM_B2 * v + (1.0 - ADAM_B2) * (g * g)
    m_hat = nm / (1.0 - ADAM_B1 ** ADAM_STEP)
    v_hat = nv / (1.0 - ADAM_B2 ** ADAM_STEP)
    return -ADAM_LR * (m_hat / (jnp.sqrt(v_hat) + ADAM_EPS) + ADAM_WD * w), nm, nv


def _adamw(w, g, m, v, *, name, br=512):
    r, c = w.shape
    br = _row_block(r, br)

    def body(w_ref, g_ref, m_ref, v_ref, d_ref, nm_ref, nv_ref):
        d_ref[...], nm_ref[...], nv_ref[...] = _adamw_update(w_ref[...], g_ref[...], m_ref[...], v_ref[...])

    row = pl.BlockSpec((br, c), lambda i: (i, 0))
    return pl.pallas_call(
        body,
        name=name,
        grid=(r // br,),
        in_specs=[row] * 4,
        out_specs=[row] * 3,
        out_shape=[jax.ShapeDtypeStruct((r, c), F32)] * 3,
        compiler_params=_params("parallel"),
    )(w, g, m, v)


def _sum_slots(x, *, name, br=512):
    n, r, c = x.shape
    br = _row_block(r, br)

    def body(x_ref, o_ref):
        acc = x_ref[0].astype(F32)
        for j in range(1, n):
            acc = acc + x_ref[j].astype(F32)
        o_ref[...] = acc

    return pl.pallas_call(
        body,
        name=name,
        grid=(r // br,),
        in_specs=[pl.BlockSpec((n, br, c), lambda i: (0, i, 0))],
        out_specs=pl.BlockSpec((br, c), lambda i: (i, 0)),
        out_shape=jax.ShapeDtypeStruct((r, c), F32),
        compiler_params=_params("parallel"),
    )(x)


_ADAMW_BLOCK_ELEMS = 128 * 1024


def _adamw_reduce(w, arrived, m, v, *, name):
    depth, r, c = w.shape
    br = _row_block(r, max(BF16_TILE_ROWS, _ADAMW_BLOCK_ELEMS // (-(-c // LANES) * LANES)))

    def body(w_ref, m_ref, v_ref, *rest):
        parts, (g_ref, d_ref, nm_ref, nv_ref) = rest[:depth], rest[depth:]
        for li in range(depth):

            @pl.when(pl.program_id(0) == li)
            def _(li=li):
                g = parts[li][0].astype(F32)
                for j in range(1, N_DEV):
                    g = g + parts[li][j].astype(F32)
                g_ref[...] = g
                d_ref[...], nm_ref[...], nv_ref[...] = _adamw_update(w_ref[...], g, m_ref[...], v_ref[...])

    cur = pl.BlockSpec((None, br, c), lambda l, i: (l, i, 0))
    slots = [pl.BlockSpec((N_DEV, br, c), lambda l, i, li=li: (0, jnp.where(l == li, i, 0), 0)) for li in range(depth)]
    return pl.pallas_call(
        body,
        name=name,
        grid=(depth, r // br),
        in_specs=[cur, cur, cur] + slots,
        out_specs=[cur] * 4,
        out_shape=[jax.ShapeDtypeStruct((depth, r, c), F32)] * 4,
        compiler_params=_params("arbitrary", "arbitrary"),
    )(w, m, v, *arrived)


def _my_place():
    return lax.axis_index("x"), lax.axis_index("y"), lax.axis_index("c")


def _flip(v, bit):
    return 1 - v if bit else v


def _slot_of(px, py, pc):
    return 4 * px + 2 * py + pc


_ANY = pl.BlockSpec(memory_space=pl.ANY)


_HBM = pl.BlockSpec(memory_space=pltpu.HBM)
_SEM = pl.BlockSpec(memory_space=pltpu.SEMAPHORE)
_DATAFLOW = pltpu.SideEffectType.DATAFLOW_SIDE_EFFECTING


_GATHER, _GATHER_COLUMNS, _SCATTER = "gather", "gather_columns", "scatter"


def _landing_shape(a, mode):
    if mode == _SCATTER:
        return a.shape
    if mode == _GATHER_COLUMNS:
        return (a.shape[0], N_DEV * a.shape[1])
    return (N_DEV, *a.shape)


def _exchange_copies(src_refs, land_refs, send_sem, recv_sem, modes):
    mx, my, mc = _my_place()
    mine = _slot_of(mx, my, mc)
    remote, local = [], []
    for src, land, mode in zip(src_refs, land_refs, modes, strict=True):
        if mode == _GATHER_COLUMNS:
            n = land.shape[1] // N_DEV
            dst = land.at[:, pl.ds(pl.multiple_of(mine * n, LANES), n)]
        else:
            dst = land.at[mine]
        for k in range(1, N_DEV):
            peer = (_flip(mx, k & 4), _flip(my, k & 2), _flip(mc, k & 1))
            remote.append(pltpu.make_async_remote_copy(
                src_ref=src.at[_slot_of(*peer)] if mode == _SCATTER else src, dst_ref=dst,
                send_sem=send_sem, recv_sem=recv_sem, device_id=peer, device_id_type=MESH))
        local.append(pltpu.make_async_copy(src.at[mine] if mode == _SCATTER else src, dst, recv_sem))
    return remote, local


def _exchange_start(groups, after, *, name):
    sizes = [len(srcs) for srcs, _ in groups]
    n, n_sems = sum(sizes), 2 * len(groups)
    srcs = [a for arrays, _ in groups for a in arrays]
    lands = [lax.empty(_landing_shape(a, mode), a.dtype)
             for arrays, modes in groups for a, mode in zip(arrays, modes, strict=True)]
    offsets = [sum(sizes[:g]) for g in range(len(groups))]

    def body(*refs):
        sems = refs[2 * n + 1:2 * n + 1 + n_sems]
        for g, (off, size, (_, modes)) in enumerate(zip(offsets, sizes, groups)):
            remote, local = _exchange_copies(refs[off:off + size], refs[n + off:n + off + size], sems[2 * g],
                                             sems[2 * g + 1], modes)
            for cp in remote + local:
                cp.start()
        refs[-1][...] = jnp.zeros_like(refs[-1])

    thru = [pltpu.HBM(a.shape, a.dtype) for a in (*srcs, *lands)]
    out = pl.pallas_call(
        body,
        name=name,
        in_specs=[_HBM] * (2 * n) + [_ANY],
        out_specs=(*[_SEM] * n_sems, *[_HBM] * (2 * n), pl.BlockSpec(memory_space=pltpu.VMEM)),
        out_shape=(*[pltpu.SemaphoreType.DMA(())] * n_sems, *thru, jax.ShapeDtypeStruct((8, LANES), F32)),
        input_output_aliases={i: n_sems + i for i in range(2 * n)},
        compiler_params=pltpu.CompilerParams(has_side_effects=_DATAFLOW),
    )(*[pltpu.with_memory_space_constraint(a, pltpu.HBM) for a in (*srcs, *lands)], after)
    sems, arrays = out[:n_sems], out[n_sems:-1]
    started = [(sems[2 * g], sems[2 * g + 1], *arrays[off:off + size], *arrays[n + off:n + off + size])
               for g, (off, size) in enumerate(zip(offsets, sizes))]
    return started, out[-1]


def _exchange_wait(started, after, *, modes, name):
    send_sem, recv_sem, *thru = started
    n = len(thru) // 2

    def body(*refs):
        remote, local = _exchange_copies(refs[:n], refs[n:2 * n], refs[2 * n], refs[2 * n + 1], modes)
        for cp in remote:
            cp.wait_send()
            cp.wait_recv()
        for cp in local:
            cp.wait()

    out = pl.pallas_call(
        body,
        name=name,
        in_specs=[_HBM] * (2 * n) + [_SEM, _SEM, _ANY],
        out_specs=[_HBM] * (2 * n),
        out_shape=[pltpu.HBM(a.shape, a.dtype) for a in thru],
        input_output_aliases={i: i for i in range(2 * n)},
        compiler_params=pltpu.CompilerParams(has_side_effects=_DATAFLOW),
    )(*thru, send_sem, recv_sem, after)
    return out[n:]


_PACK_ROWS = BF16_TILE_ROWS


def _pack(arrays, dtype):
    flat = [a.astype(dtype).reshape(-1) for a in arrays]
    total = sum(f.shape[0] for f in flat)
    padded = -(-total // (LANES * _PACK_ROWS)) * (LANES * _PACK_ROWS)
    if padded > total:
        flat.append(jnp.zeros((padded - total,), dtype))
    return jnp.concatenate(flat).reshape(-1, LANES)


def _unpack(packed, shapes, lead=()):
    flat = packed.reshape(*lead, -1)
    out, off = [], 0
    for s in shapes:
        size = math.prod(s)
        out.append(flat[..., off:off + size].reshape(*lead, *s))
        off += size
    return out


def _gather_columns(g):
    return jnp.moveaxis(g, 0, 1).reshape(g.shape[1], -1)


def _split_rows(w):
    return w.reshape(N_DEV, w.shape[0] // N_DEV, w.shape[1])


_GATHER_GROUPS = (("w_in", "conv_w"), ("w_out",), ("w_ff1",), ("w_ff2",), ("w_ple_gate", "w_ple_proj"))
_REST = ("w_out", "w_ff1", "w_ff2", "w_ple_gate", "w_ple_proj")
_BIG = ("w_in",) + _REST
_GATHER_MODE = dict(w_in=_GATHER, conv_w=_GATHER, w_out=_GATHER, w_ff1=_GATHER_COLUMNS, w_ff2=_GATHER,
                    w_ple_gate=_GATHER, w_ple_proj=_GATHER_COLUMNS)
_RELAYOUT_AFTER_GATHER = ("w_in", "conv_w")
_SMALL = ("norm1_g", "q_norm_g", "k_norm_g", "sgu_norm_g", "sgu_w", "sgu_b", "norm2_g", "norm3_g")
_ORDER = ("norm1_g", "w_in", "conv_w", "q_norm_g", "k_norm_g", "sgu_norm_g", "sgu_w", "sgu_b", "w_out", "norm2_g",
          "w_ff1", "w_ff2", "norm3_g", "w_ple_gate", "w_ple_proj")


def _whole_matrices(names, landed):
    return {k: _gather_columns(g) if k in _RELAYOUT_AFTER_GATHER else g.reshape(-1, g.shape[-1])
            for k, g in zip(names, landed, strict=True)}


def _layer_forward(h0, p16, s, li, fetch, w_first, fetch_next):
    nm = lambda k: f"{k}_l{li}"
    t = h0.shape[0]
    hn1 = _rms_fwd(h0, s["norm1_g"], name=nm("rms1"))
    w = dict(w_first) if w_first is not None else fetch(0, hn1)
    proj = _matmul(hn1, w["w_in"], name=nm("proj"), bm=t, bn=256)
    y_a = _conv_fwd(proj, w["conv_w"], name=nm("conv"))
    qs, kn, v = _qk_prep(proj, s["gq"], s["gk"], name=nm("qkprep"))
    y_b, lt = _attn_fwd(qs, kn, v, name=nm("attn"))
    y_c = _sgu_fwd(proj, s["sgu_norm_g"], s["sgu_w"], s["b_exp"], name=nm("sgu"))
    mix = jnp.concatenate([y_a, y_b, y_c], axis=1)
    w.update(fetch(1, y_b))
    h1 = _matmul(mix, w["w_out"], name=nm("out"), bm=t, bn=256, extras=(h0,), epilogue=lambda acc, r: (r + acc,))
    hn2 = _rms_fwd(h1, s["norm2_g"], name=nm("rms2"))
    w.update(fetch(2, hn2))
    u, f = _matmul(hn2, w["w_ff1"], name=nm("ff1"), bm=t, bn=512, out_dtypes=(F32, BF16),
                   epilogue=lambda acc: (acc, jnp.square(jnp.maximum(acc, 0.0))))
    w.update(fetch(3, f))
    h2 = _matmul(f, w["w_ff2"], name=nm("ff2"), bm=512, bn=512, extras=(h1,), epilogue=lambda acc, r: (r + acc,))
    hn3 = _rms_fwd(h2, s["norm3_g"], name=nm("rms3"))
    w.update(fetch(4, hn3))
    w_next = fetch_next(hn3)
    pp = _matmul(p16, w["w_ple_proj"], name=nm("pleproj"), bm=t, bn=512)

    def gate_epilogue(acc, pp_blk, h_blk):
        gate = jax.nn.sigmoid(acc)
        return h_blk + gate * pp_blk, gate

    h3, gate = _matmul(hn3, w["w_ple_gate"], name=nm("plegate"), bm=t, bn=256, out_dtypes=(F32, F32),
                       extras=(pp, h2), epilogue=gate_epilogue)
    saved = dict(h0=h0, hn1=hn1, proj=proj, qs=qs, kn=kn, v=v, lt=lt, mix=mix, h1=h1, hn2=hn2, u=u, f=f, h2=h2,
                 hn3=hn3, pp=pp, gate=gate, p16=p16)
    return h3, w, w_next, saved


def _layer_backward(dh3, a, w, s, li, order_after, start_rest):
    nm = lambda k: f"{k}_bwd_l{li}"
    t = dh3.shape[0]
    dpre, dpp = _ple_bwd(dh3, a["gate"], a["pp"], order_after, name=nm("ple"))
    g_gate = _weight_grad(a["hn3"], dpre, name=nm("dwgate"))
    g_proj = _weight_grad(a["p16"], dpp, name=nm("dwproj"), column_shards=True)
    dh2, dh2_16, g_n3 = _matmul_rms_bwd(dpre, w["w_ple_gate"], a["h2"], s["norm3_g"], dh3, name=nm("dh2"))
    du = _matmul(dh2_16, w["w_ff2"], name=nm("du"), tb=True, bm=t, bn=512, out_dtypes=(BF16,), extras=(a["u"],),
                 epilogue=lambda acc, u: (acc * (2.0 * jnp.maximum(u, 0.0)),))
    g_ff2 = _weight_grad(a["f"], dh2_16, name=nm("dwff2"))
    g_ff1 = _weight_grad(a["hn2"], du, name=nm("dwff1"), column_shards=True)
    dh1, dh1_16, g_n2 = _matmul_rms_bwd(du, w["w_ff1"], a["h1"], s["norm2_g"], dh2, name=nm("dh1"))
    dmix = _matmul(dh1_16, w["w_out"], name=nm("dmix"), tb=True, bm=t, bn=256)
    g_out = _weight_grad(a["mix"], dh1_16, name=nm("dwout"))
    started = start_rest(dict(w_out=_split_rows(g_out), w_ff1=g_ff1, w_ff2=_split_rows(g_ff2),
                              w_ple_gate=_split_rows(g_gate), w_ple_proj=g_proj), dmix)
    d_b, d_c, d_h, g_conv = _conv_bwd(dmix, a["proj"], w["conv_w"], name=nm("conv"))
    dqs, dkn, dv = _attn_bwd(dmix, a["qs"], a["kn"], a["v"], a["lt"], started, name=nm("attn"))
    d_q, d_k, d_v, g_q, g_k = _qk_prep_bwd(dqs, dkn, dv, a["proj"], s["gq"], s["gk"], name=nm("qkprep"))
    d_cu, d_cv, g_sn, g_sw, g_sb = _sgu_bwd(dmix, a["proj"], s["sgu_norm_g"], s["sgu_w"], s["b_exp"], name=nm("sgu"))
    dproj = jnp.concatenate([d_b, d_c, d_h, d_q, d_k, d_v, d_cu, d_cv], axis=1)
    g_in = _weight_grad(a["hn1"], dproj, name=nm("dwin"), column_shards=True)
    dh0, _, g_n1 = _matmul_rms_bwd(dproj, w["w_in"], a["h0"], s["norm1_g"], dh1, name=nm("dh0"))
    n_tiles = g_q.shape[1] // HEAD_DIM
    small = dict(
        norm1_g=g_n1[0], norm2_g=g_n2[0], norm3_g=g_n3[0],
        q_norm_g=g_q.reshape(n_tiles, HEAD_DIM).sum(0), k_norm_g=g_k.reshape(n_tiles, HEAD_DIM).sum(0),
        sgu_norm_g=g_sn[0], sgu_w=g_sw, sgu_b=g_sb.reshape(CHUNK, SGU_HEADS, HEAD_DIM).sum(-1).T,
        conv_w=g_conv[:CONV_TAPS],
    )
    return dh0, g_in, small


def kernel(x, p, norm1_g, w_in, conv_w, q_norm_g, k_norm_g, sgu_norm_g, sgu_w, sgu_b, w_out, norm2_g, w_ff1, w_ff2, norm3_g, w_ple_gate, w_ple_proj, loss_target, m_norm1_g, m_w_in, m_conv_w, m_q_norm_g, m_k_norm_g, m_sgu_norm_g, m_sgu_w, m_sgu_b, m_w_out, m_norm2_g, m_w_ff1, m_w_ff2, m_norm3_g, m_w_ple_gate, m_w_ple_proj, v_norm1_g, v_w_in, v_conv_w, v_q_norm_g, v_k_norm_g, v_sgu_norm_g, v_sgu_w, v_sgu_b, v_w_out, v_norm2_g, v_w_ff1, v_w_ff2, v_norm3_g, v_w_ple_gate, v_w_ple_proj):
    weights = dict(norm1_g=norm1_g, w_in=w_in, conv_w=conv_w, q_norm_g=q_norm_g, k_norm_g=k_norm_g,
                   sgu_norm_g=sgu_norm_g, sgu_w=sgu_w, sgu_b=sgu_b, w_out=w_out, norm2_g=norm2_g, w_ff1=w_ff1,
                   w_ff2=w_ff2, norm3_g=norm3_g, w_ple_gate=w_ple_gate, w_ple_proj=w_ple_proj)
    mom = dict(norm1_g=m_norm1_g, w_in=m_w_in, conv_w=m_conv_w, q_norm_g=m_q_norm_g, k_norm_g=m_k_norm_g,
               sgu_norm_g=m_sgu_norm_g, sgu_w=m_sgu_w, sgu_b=m_sgu_b, w_out=m_w_out, norm2_g=m_norm2_g, w_ff1=m_w_ff1,
               w_ff2=m_w_ff2, norm3_g=m_norm3_g, w_ple_gate=m_w_ple_gate, w_ple_proj=m_w_ple_proj)
    var = dict(norm1_g=v_norm1_g, w_in=v_w_in, conv_w=v_conv_w, q_norm_g=v_q_norm_g, k_norm_g=v_k_norm_g,
               sgu_norm_g=v_sgu_norm_g, sgu_w=v_sgu_w, sgu_b=v_sgu_b, w_out=v_w_out, norm2_g=v_norm2_g, w_ff1=v_w_ff1,
               w_ff2=v_w_ff2, norm3_g=v_norm3_g, w_ple_gate=v_w_ple_gate, w_ple_proj=v_w_ple_proj)
    depth = norm1_g.shape[0]
    mx, my, mc = _my_place()
    me = _slot_of(mx, my, mc)

    gathers = []
    token = x[0, :8, :LANES]
    for li in range(depth):
        groups = [([weights[k][li] if k == "conv_w" else weights[k][li].astype(BF16) for k in names],
                   tuple(_GATHER_MODE[k] for k in names)) for names in _GATHER_GROUPS]
        started, token = _exchange_start(groups, token, name=f"gather_start_l{li}")
        gathers.append(started)

    small = []
    for li in range(depth):
        small.append(dict(
            norm1_g=norm1_g[li][None], norm2_g=norm2_g[li][None], norm3_g=norm3_g[li][None],
            gq=jnp.tile(q_norm_g[li], _QK_BLOCK // HEAD_DIM)[None], gk=jnp.tile(k_norm_g[li], _QK_BLOCK // HEAD_DIM)[None],
            sgu_norm_g=sgu_norm_g[li][None], sgu_w=sgu_w[li], b_exp=jnp.repeat(sgu_b[li].T, HEAD_DIM, axis=1),
        ))
    small[0]["norm1_g"] = small[0]["norm1_g"] + token[0, 0]

    h = x[0]
    saved, full = [], []
    w_first = None
    for li in range(depth):

        def fetch(g, after, li=li):
            names = _GATHER_GROUPS[g]
            landed = _exchange_wait(gathers[li][g], after, modes=tuple(_GATHER_MODE[k] for k in names),
                                    name=f"gather_{names[0]}_wait_l{li}")
            return _whole_matrices(names, landed)

        def fetch_next(after, li=li):
            return fetch(0, after, li + 1) if li + 1 < depth else None

        h, w, w_first, acts = _layer_forward(h, p[li, 0].astype(BF16), small[li], li, fetch, w_first, fetch_next)
        full.append(w)
        saved.append(acts)
    dh, loss_tile = _loss_head(h, loss_target[0], name="loss_head")
    loss = lax.psum(loss_tile[0, 0], ("x", "y", "c"))

    small_names = _SMALL + ("conv_w",)
    scatter_first, scatter_rest = [None] * depth, [None] * depth
    first_modes, rest_modes = (_SCATTER, _GATHER), (_SCATTER,) * len(_REST)
    token = loss_tile
    for li in reversed(range(depth)):

        def start_rest(parts, after, li=li):
            (scatter_rest[li],), started = _exchange_start([([parts[k] for k in _REST], rest_modes)], after,
                                                           name=f"scatter_rest_start_l{li}")
            return started

        dh, g_in, small_grads = _layer_backward(dh, saved[li], full[li], small[li], li, token, start_rest)
        small_shapes = [small_grads[k].shape for k in small_names]
        (scatter_first[li],), token = _exchange_start(
            [([g_in, _pack([small_grads[k] for k in small_names], F32)], first_modes)], dh,
            name=f"scatter_first_start_l{li}")
    grad_x = dh[None]

    grads, delta, new_m, new_v = {}, {}, {}, {}
    arrived = {k: [None] * depth for k in _BIG}
    for li in reversed(range(depth)):
        landed = _exchange_wait(scatter_rest[li], token, modes=rest_modes, name=f"scatter_rest_wait_l{li}")
        for k, g in zip(_REST, landed, strict=True):
            arrived[k][li] = g
    for k in _REST:
        grads[k], delta[k], new_m[k], new_v[k] = _adamw_reduce(weights[k], arrived[k], mom[k], var[k], name=f"adamw_{k}")
    small_sums = [None] * depth
    updated = jnp.stack([delta[k][0, 0, :1] for k in _REST])
    for li in reversed(range(depth)):
        arrived["w_in"][li], small_parts = _exchange_wait(scatter_first[li], updated, modes=first_modes,
                                                          name=f"scatter_first_wait_l{li}")
        small_sums[li] = _unpack(_sum_slots(small_parts, name=f"sum_small_grads_l{li}"), small_shapes)
    grads["w_in"], delta["w_in"], new_m["w_in"], new_v["w_in"] = _adamw_reduce(
        w_in, arrived["w_in"], mom["w_in"], var["w_in"], name="adamw_w_in")
    for i, k in enumerate(small_names):
        grads[k] = jnp.stack([small_sums[li][i] for li in range(depth)])
    n_conv = conv_w.shape[2]
    grads["conv_w"] = lax.dynamic_slice_in_dim(grads["conv_w"], me * n_conv, n_conv, axis=2)
    for k in small_names:
        as_rows = lambda a: a.reshape(-1, a.shape[-1])
        outs = _adamw(as_rows(weights[k]), as_rows(grads[k]), as_rows(mom[k]), as_rows(var[k]), name=f"adamw_{k}")
        delta[k], new_m[k], new_v[k] = (o.reshape(weights[k].shape) for o in outs)

    return (loss, grad_x, *[grads[k] for k in _ORDER], *[delta[k] for k in _ORDER],
            *[new_m[k] for k in _ORDER], *[new_v[k] for k in _ORDER])
```

```python
import functools
import math

import jax
import jax.numpy as jnp
from jax import lax
from jax.experimental import pallas as pl
from jax.experimental.pallas import tpu as pltpu

F32 = jnp.float32
BF16 = jnp.bfloat16

N_DEV = 8
HEAD_DIM = 64
CONV_W = 256
ATTN_W = 512
SGU_W = 256
SGU_HEADS = 4
CHUNK = 128
CONV_TAPS = 3
EPS = 1e-6
QK_SCALE = HEAD_DIM ** -0.5

ADAM_LR = 0.001
ADAM_B1 = 0.9
ADAM_B2 = 0.999
ADAM_EPS = 1e-08
ADAM_WD = 0.01
ADAM_STEP = 10

LANES = 128
BF16_TILE_ROWS = 16
VMEM_LIMIT_BYTES = 56 * 1024 * 1024
MESH = pl.DeviceIdType.MESH


def _params(*sem):
    return pltpu.CompilerParams(dimension_semantics=sem, vmem_limit_bytes=VMEM_LIMIT_BYTES)


def _row_block(rows, cap):
    if rows <= cap:
        return rows
    return max(b for b in range(BF16_TILE_ROWS, cap + 1, BF16_TILE_ROWS) if rows % b == 0)


def _matmul(a, b, *, name, tb=False, bm=512, bn=512, out_dtypes=(F32,), epilogue=None, extras=()):
    m, k = a.shape
    n = b.shape[0] if tb else b.shape[1]
    assert k == (b.shape[1] if tb else b.shape[0])
    bm, bn = min(bm, m), min(bn, n)
    assert m % bm == 0 and n % bn == 0
    a_spec = pl.BlockSpec((bm, k), lambda i, j: (i, 0))
    b_spec = pl.BlockSpec((bn, k), lambda i, j: (j, 0)) if tb else pl.BlockSpec((k, bn), lambda i, j: (0, j))
    dims = (((1,), (1 if tb else 0,)), ((), ()))
    n_ex = len(extras)
    for e in extras:
        assert e.shape == (m, n), (e.shape, m, n)

    def body(a_ref, b_ref, *rest):
        outs = rest[n_ex:]
        acc = lax.dot_general(a_ref[...], b_ref[...], dims, preferred_element_type=F32)
        res = (acc,) if epilogue is None else epilogue(acc, *[e[...] for e in rest[:n_ex]])
        for o_ref, r in zip(outs, res, strict=True):
            o_ref[...] = r.astype(o_ref.dtype)

    tile = pl.BlockSpec((bm, bn), lambda i, j: (i, j))
    out = pl.pallas_call(
        body,
        name=name,
        grid=(m // bm, n // bn),
        in_specs=[a_spec, b_spec] + [tile] * n_ex,
        out_specs=[tile] * len(out_dtypes),
        out_shape=[jax.ShapeDtypeStruct((m, n), d) for d in out_dtypes],
        compiler_params=_params("parallel", "parallel"),
    )(a, b, *extras)
    return out[0] if len(out_dtypes) == 1 else out


_WEIGHT_GRAD_ACC_ELEMS = 1024 * 1024


def _weight_grad(x, dy, *, name, column_shards=False):
    t, m = x.shape
    n = dy.shape[1]
    bm = m if m <= 2 * LANES else min(m // 2, max(LANES, _WEIGHT_GRAD_ACC_ELEMS // n // LANES * LANES))
    assert m % bm == 0
    ns = n // N_DEV

    def body(x_ref, dy_ref, o_ref):
        acc = lax.dot_general(x_ref[...], dy_ref[...], _TN, preferred_element_type=F32)
        if column_shards:
            for s in range(N_DEV):
                o_ref[s] = acc[:, s * ns:(s + 1) * ns].astype(o_ref.dtype)
        else:
            o_ref[...] = acc.astype(o_ref.dtype)

    if column_shards:
        out_spec, out_dims = pl.BlockSpec((N_DEV, bm, ns), lambda i: (0, i, 0)), (N_DEV, m, ns)
    else:
        out_spec, out_dims = pl.BlockSpec((bm, n), lambda i: (i, 0)), (m, n)
    return pl.pallas_call(
        body,
        name=name,
        grid=(m // bm,),
        in_specs=[pl.BlockSpec((t, bm), lambda i: (0, i)), pl.BlockSpec((t, n), lambda i: (0, 0))],
        out_specs=out_spec,
        out_shape=jax.ShapeDtypeStruct(out_dims, BF16),
        compiler_params=_params("parallel"),
    )(x, dy)


def _rms_fwd(h, g, *, name, br=512):
    t, d = h.shape
    br = min(br, t)

    def body(h_ref, g_ref, o_ref):
        x = h_ref[...]
        r = lax.rsqrt(jnp.mean(x * x, axis=-1, keepdims=True) + EPS)
        o_ref[...] = (x * r * g_ref[...]).astype(o_ref.dtype)

    return pl.pallas_call(
        body,
        name=name,
        grid=(t // br,),
        in_specs=[pl.BlockSpec((br, d), lambda i: (i, 0)), pl.BlockSpec((1, d), lambda i: (0, 0))],
        out_specs=pl.BlockSpec((br, d), lambda i: (i, 0)),
        out_shape=jax.ShapeDtypeStruct((t, d), BF16),
        compiler_params=_params("parallel"),
    )(h, g)


def _matmul_rms_bwd(dz, w, h, g, dres, *, name):
    t, d = h.shape
    k = dz.shape[1]
    br = min(t, 512 if k <= d else 256)

    def body(dz_ref, w_ref, h_ref, g_ref, dres_ref, dh_ref, dh16_ref, dg_ref):
        x = h_ref[...]
        dyv = lax.dot_general(dz_ref[...], w_ref[...], _NT, preferred_element_type=F32)
        r = lax.rsqrt(jnp.mean(x * x, axis=-1, keepdims=True) + EPS)
        xhat = x * r
        dxhat = dyv * g_ref[...]
        dh = dres_ref[...] + r * (dxhat - xhat * jnp.mean(dxhat * xhat, axis=-1, keepdims=True))
        dh_ref[...] = dh
        dh16_ref[...] = dh.astype(dh16_ref.dtype)

        @pl.when(pl.program_id(0) == 0)
        def _():
            dg_ref[...] = jnp.zeros_like(dg_ref)

        dg_ref[...] += jnp.sum(dyv * xhat, axis=0, keepdims=True)

    row = pl.BlockSpec((br, d), lambda i: (i, 0))
    vec = pl.BlockSpec((1, d), lambda i: (0, 0))
    return pl.pallas_call(
        body,
        name=name,
        grid=(t // br,),
        in_specs=[pl.BlockSpec((br, k), lambda i: (i, 0)), pl.BlockSpec((d, k), lambda i: (0, 0)), row, vec, row],
        out_specs=[row, row, vec],
        out_shape=[jax.ShapeDtypeStruct((t, d), F32), jax.ShapeDtypeStruct((t, d), BF16),
                   jax.ShapeDtypeStruct((1, d), F32)],
        compiler_params=_params("arbitrary"),
    )(dz, w, h, g, dres)


def _group_mean(x, width):
    grp = lax.broadcasted_iota(jnp.int32, x.shape, 1) // HEAD_DIM
    out = jnp.zeros_like(x)
    for gi in range(width // HEAD_DIM):
        m = grp == gi
        s = jnp.sum(jnp.where(m, x, 0.0), axis=1, keepdims=True)
        out = jnp.where(m, s, out)
    return out * (1.0 / HEAD_DIM)


def _gelu(x):
    return 0.5 * x * (1.0 + lax.erf(x * (2.0 ** -0.5)))


def _gelu_grad(x):
    cdf = 0.5 * (1.0 + lax.erf(x * (2.0 ** -0.5)))
    pdf = jnp.exp(-0.5 * x * x) * (1.0 / math.sqrt(2.0 * math.pi))
    return cdf + x * pdf


def _shift_down(z, s, row):
    return jnp.where(row >= s, pltpu.roll(z, s, 0), 0.0)


def _shift_up(z, s, row, t):
    return jnp.where(row < t - s, pltpu.roll(z, t - s, 0), 0.0)


def _conv_fwd(proj, conv_w, *, name):
    t = proj.shape[0]
    nb = CONV_W // LANES

    def body(b_ref, c_ref, h_ref, w_ref, o_ref):
        row = lax.broadcasted_iota(jnp.int32, (t, LANES), 0)
        z = c_ref[...] * h_ref[...]
        w = w_ref[...]
        conv = w[2:3, :] * z + w[1:2, :] * _shift_down(z, 1, row) + w[0:1, :] * _shift_down(z, 2, row)
        o_ref[...] = (b_ref[...] * conv).astype(o_ref.dtype)

    return pl.pallas_call(
        body,
        name=name,
        grid=(nb,),
        in_specs=[
            pl.BlockSpec((t, LANES), lambda j: (0, j)),
            pl.BlockSpec((t, LANES), lambda j: (0, nb + j)),
            pl.BlockSpec((t, LANES), lambda j: (0, 2 * nb + j)),
            pl.BlockSpec((CONV_TAPS, LANES), lambda j: (0, j)),
        ],
        out_specs=pl.BlockSpec((t, LANES), lambda j: (0, j)),
        out_shape=jax.ShapeDtypeStruct((t, CONV_W), BF16),
        compiler_params=_params("parallel"),
    )(proj, proj, proj, conv_w)


def _conv_bwd(dmix, proj, conv_w, *, name):
    t = proj.shape[0]
    nb = CONV_W // LANES

    def body(dy_ref, b_ref, c_ref, h_ref, w_ref, db_ref, dc_ref, dh_ref, dw_ref):
        row = lax.broadcasted_iota(jnp.int32, (t, LANES), 0)
        ac, ah = c_ref[...], h_ref[...]
        z = ac * ah
        w = w_ref[...]
        z1 = _shift_down(z, 1, row)
        z2 = _shift_down(z, 2, row)
        conv = w[2:3, :] * z + w[1:2, :] * z1 + w[0:1, :] * z2
        dy = dy_ref[...]
        db_ref[...] = (dy * conv).astype(db_ref.dtype)
        dconv = dy * b_ref[...]
        dz = w[2:3, :] * dconv + w[1:2, :] * _shift_up(dconv, 1, row, t) + w[0:1, :] * _shift_up(dconv, 2, row, t)
        dc_ref[...] = (dz * ah).astype(dc_ref.dtype)
        dh_ref[...] = (dz * ac).astype(dh_ref.dtype)
        dw_ref[...] = jnp.zeros_like(dw_ref)
        dw_ref[0:1, :] = jnp.sum(dconv * z2, axis=0, keepdims=True)
        dw_ref[1:2, :] = jnp.sum(dconv * z1, axis=0, keepdims=True)
        dw_ref[2:3, :] = jnp.sum(dconv * z, axis=0, keepdims=True)

    col = lambda off: pl.BlockSpec((t, LANES), lambda j: (0, off + j))
    return pl.pallas_call(
        body,
        name=name,
        grid=(nb,),
        in_specs=[col(0), col(0), col(nb), col(2 * nb), pl.BlockSpec((CONV_TAPS, LANES), lambda j: (0, j))],
        out_specs=[col(0), col(0), col(0), pl.BlockSpec((8, LANES), lambda j: (0, j))],
        out_shape=[jax.ShapeDtypeStruct((t, CONV_W), BF16)] * 3 + [jax.ShapeDtypeStruct((8, CONV_W), F32)],
        compiler_params=_params("parallel"),
    )(dmix, proj, proj, proj, conv_w)


_QK_BLOCK = 256


def _qk_prep(proj, gq, gk, *, name, br=512):
    t = proj.shape[0]
    br = min(br, t)
    nb = ATTN_W // _QK_BLOCK
    q0 = (3 * CONV_W) // _QK_BLOCK

    def body(q_ref, k_ref, v_ref, gq_ref, gk_ref, qo_ref, ko_ref, vo_ref):
        q = q_ref[...]
        k = k_ref[...]
        rq = lax.rsqrt(_group_mean(q * q, _QK_BLOCK) + EPS)
        rk = lax.rsqrt(_group_mean(k * k, _QK_BLOCK) + EPS)
        qo_ref[...] = ((q * rq * gq_ref[...]).astype(BF16) * QK_SCALE).astype(qo_ref.dtype)
        ko_ref[...] = (k * rk * gk_ref[...]).astype(ko_ref.dtype)
        vo_ref[...] = v_ref[...].astype(vo_ref.dtype)

    col = lambda off: pl.BlockSpec((br, _QK_BLOCK), lambda i, j: (i, off + j))
    vec = pl.BlockSpec((1, _QK_BLOCK), lambda i, j: (0, 0))
    return pl.pallas_call(
        body,
        name=name,
        grid=(t // br, nb),
        in_specs=[col(q0), col(q0 + nb), col(q0 + 2 * nb), vec, vec],
        out_specs=[col(0)] * 3,
        out_shape=[jax.ShapeDtypeStruct((t, ATTN_W), BF16)] * 3,
        compiler_params=_params("parallel", "parallel"),
    )(proj, proj, proj, gq, gk)


def _qk_prep_bwd(dqs, dkn, dv, proj, gq, gk, *, name, br=512):
    t = proj.shape[0]
    br = min(br, t)
    nb = ATTN_W // _QK_BLOCK
    q0 = (3 * CONV_W) // _QK_BLOCK

    def norm_bwd(dy, x, g):
        r = lax.rsqrt(_group_mean(x * x, _QK_BLOCK) + EPS)
        xhat = x * r
        dxhat = dy * g
        dx = r * (dxhat - xhat * _group_mean(dxhat * xhat, _QK_BLOCK))
        return dx, jnp.sum(dy * xhat, axis=0, keepdims=True)

    def body(dq_ref, dk_ref, dv_ref, q_ref, k_ref, gq_ref, gk_ref, oq_ref, ok_ref, ov_ref, dgq_ref, dgk_ref):
        dq, dgq = norm_bwd(dq_ref[...] * QK_SCALE, q_ref[...], gq_ref[...])
        dk, dgk = norm_bwd(dk_ref[...], k_ref[...], gk_ref[...])
        oq_ref[...] = dq.astype(oq_ref.dtype)
        ok_ref[...] = dk.astype(ok_ref.dtype)
        ov_ref[...] = dv_ref[...].astype(ov_ref.dtype)

        @pl.when((pl.program_id(0) == 0) & (pl.program_id(1) == 0))
        def _():
            dgq_ref[...] = jnp.zeros_like(dgq_ref)
            dgk_ref[...] = jnp.zeros_like(dgk_ref)

        dgq_ref[...] += dgq
        dgk_ref[...] += dgk

    col = lambda off: pl.BlockSpec((br, _QK_BLOCK), lambda i, j: (i, off + j))
    vec = pl.BlockSpec((1, _QK_BLOCK), lambda i, j: (0, 0))
    return pl.pallas_call(
        body,
        name=name,
        grid=(t // br, nb),
        in_specs=[col(0), col(0), col(0), col(q0), col(q0 + nb), vec, vec],
        out_specs=[col(0), col(0), col(0), vec, vec],
        out_shape=[jax.ShapeDtypeStruct((t, ATTN_W), BF16)] * 3 + [jax.ShapeDtypeStruct((1, _QK_BLOCK), F32)] * 2,
        compiler_params=_params("arbitrary", "arbitrary"),
    )(dqs, dkn, dv, proj, proj, gq, gk)


def _key_order_matrix(tb, relation):
    jj = lax.broadcasted_iota(jnp.int32, (tb, tb), 0)
    ss = lax.broadcasted_iota(jnp.int32, (tb, tb), 1)
    return relation(jj, ss).astype(BF16)


def _log_sigmoids(z):
    lb = jnp.minimum(z, 0.0) - jnp.log(1.0 + jnp.exp(-jnp.abs(z)))
    return lb, lb - z


def _below_diagonal(tb):
    return lax.broadcasted_iota(jnp.int32, (tb, tb), 1) < lax.broadcasted_iota(jnp.int32, (tb, tb), 0)


_NT = (((1,), (1,)), ((), ()))
_TN = (((0,), (0,)), ((), ()))
_ATTN_BLOCK = 256
_ATTN_FWD_UNROLL = 2
_ATTN_BWD_UNROLL = 3


def _attn_fwd(qs, kn, v, *, name, tb=_ATTN_BLOCK, unroll=_ATTN_FWD_UNROLL):
    t = qs.shape[0]
    tb = min(tb, t)
    assert t % tb == 0
    n_pairs = ATTN_W // LANES

    def body(q_ref, k_ref, v_ref, o_ref, lt_ref, acc_ref, carry_ref):
        qb = pl.program_id(1)
        half = lax.broadcasted_iota(jnp.int32, (1, LANES), 1) // HEAD_DIM
        later = _key_order_matrix(tb, lambda j, s: j > s)
        acc_ref[...] = jnp.zeros_like(acc_ref)
        carry_ref[...] = jnp.zeros_like(carry_ref)
        q = q_ref[...]
        qh = [jnp.where(half == h, q, jnp.zeros_like(q)) for h in range(2)]

        def tiles(kbs, diagonal):
            blk = []
            for kb in kbs:
                start = pl.multiple_of(kb * tb, tb)
                blk.append((k_ref[pl.ds(start, tb), :], v_ref[pl.ds(start, tb), :]))
            chains = [(h, j) for j in range(len(kbs)) for h in range(2)]
            z = [lax.dot_general(qh[h], blk[j][0], _NT, preferred_element_type=F32) for h, j in chains]
            causal = _below_diagonal(tb) if diagonal else None
            lb, lr = [], []
            for zi in z:
                b, r = _log_sigmoids(zi)
                lb.append(b)
                lr.append(jnp.where(causal, r, 0.0) if diagonal else r)
            suffix = [jnp.dot(r.astype(BF16), later, preferred_element_type=F32) for r in lr]
            carry = [carry_ref[0], carry_ref[1]]
            w = []
            for i, (h, j) in enumerate(chains):
                wi = jnp.exp(lb[i] + (suffix[i] + carry[h][:, 0:1]))
                w.append((jnp.where(causal, wi, 0.0) if diagonal else wi).astype(BF16))
                carry[h] = carry[h] + jnp.sum(lr[i], axis=1, keepdims=True)
            for i, (h, j) in enumerate(chains):
                vh = jnp.where(half == h, blk[j][1], jnp.zeros_like(blk[j][1]))
                acc_ref[h] += jnp.dot(w[i], vh, preferred_element_type=F32)
            carry_ref[0] = carry[0]
            carry_ref[1] = carry[1]

        tiles([qb], True)

        def step(i, _):
            kb = qb - 1 - unroll * i
            tiles([kb - u for u in range(unroll)], False)
            return 0

        lax.fori_loop(0, qb // unroll, step, 0)
        for left in range(1, unroll):

            @pl.when(qb % unroll == left)
            def _(left=left):
                tiles([left - 1 - u for u in range(left)], False)

        o_ref[...] = (acc_ref[0] + acc_ref[1]).astype(o_ref.dtype)
        lt_ref[...] = jnp.where(half == 0, carry_ref[0], carry_ref[1])

    return pl.pallas_call(
        body,
        name=name,
        grid=(n_pairs, t // tb),
        in_specs=[
            pl.BlockSpec((tb, LANES), lambda p, i: (i, p)),
            pl.BlockSpec((t, LANES), lambda p, i: (0, p)),
            pl.BlockSpec((t, LANES), lambda p, i: (0, p)),
        ],
        out_specs=[pl.BlockSpec((tb, LANES), lambda p, i: (i, p))] * 2,
        out_shape=[jax.ShapeDtypeStruct((t, ATTN_W), BF16), jax.ShapeDtypeStruct((t, ATTN_W), F32)],
        scratch_shapes=[pltpu.VMEM((2, tb, LANES), F32), pltpu.VMEM((2, tb, LANES), F32)],
        compiler_params=_params("parallel", "parallel"),
    )(qs, kn, v)


def _attn_bwd(dmix, qs, kn, v, lt, order_after, *, name, tb=_ATTN_BLOCK, unroll=_ATTN_BWD_UNROLL):
    t = qs.shape[0]
    tb = min(tb, t)
    assert t % tb == 0
    n_pairs = ATTN_W // LANES
    dy0 = CONV_W // LANES

    def body(do_ref, q_ref, k_ref, v_ref, lt_ref, order_ref, dq_ref, dk_ref, dv_ref, dqacc_ref, cc_ref, cg_ref):
        qb = pl.program_id(1)
        half = lax.broadcasted_iota(jnp.int32, (1, LANES), 1) // HEAD_DIM
        lane = lax.broadcasted_iota(jnp.int32, (tb, LANES), 1)
        later = _key_order_matrix(tb, lambda j, s: j > s)
        before = _key_order_matrix(tb, lambda j, s: j < s)
        q = q_ref[...]
        do = do_ref[...].astype(BF16)
        lt = lt_ref[...]
        qh = [jnp.where(half == h, q, jnp.zeros_like(q)) for h in range(2)]
        doh = [jnp.where(half == h, do, jnp.zeros_like(do)) for h in range(2)]
        lth = [jnp.sum(jnp.where(lane == h * HEAD_DIM, lt, 0.0), axis=1, keepdims=True) for h in range(2)]

        @pl.when(qb == 0)
        def _():
            dk_ref[...] = jnp.zeros_like(dk_ref)
            dv_ref[...] = jnp.zeros_like(dv_ref)

        dqacc_ref[...] = jnp.zeros_like(dqacc_ref)
        cc_ref[...] = jnp.zeros_like(cc_ref)
        cg_ref[...] = jnp.zeros_like(cg_ref)

        def tiles(kbs, diagonal):
            starts = [pl.multiple_of(kb * tb, tb) for kb in kbs]
            blk = [(k_ref[pl.ds(s, tb), :], v_ref[pl.ds(s, tb), :]) for s in starts]
            chains = [(h, j) for j in range(len(kbs)) for h in range(2)]
            z = [lax.dot_general(qh[h], blk[j][0], _NT, preferred_element_type=F32) for h, j in chains]
            da = [lax.dot_general(doh[h], jnp.where(half == h, blk[j][1], jnp.zeros_like(blk[j][1])), _NT,
                                  preferred_element_type=F32) for h, j in chains]
            causal = _below_diagonal(tb) if diagonal else None
            lb, lr = [], []
            for zi in z:
                b, r = _log_sigmoids(zi)
                lb.append(b)
                lr.append(jnp.where(causal, r, 0.0) if diagonal else r)
            suffix = [jnp.dot(r.astype(BF16), later, preferred_element_type=F32) for r in lr]
            cc = [cc_ref[0], cc_ref[1]]
            cg = [cg_ref[0], cg_ref[1]]
            a16, g = [], []
            for i, (h, j) in enumerate(chains):
                cc[h] = cc[h] + jnp.sum(lr[i], axis=1, keepdims=True)
                a = jnp.exp(lb[i] + suffix[i] + (lth[h] - cc[h][:, 0:1]))
                if diagonal:
                    a = jnp.where(causal, a, 0.0)
                a16.append(a.astype(BF16))
                g.append(da[i] * a)
            g_before = [jnp.dot(gi.astype(BF16), before, preferred_element_type=F32) for gi in g]
            dz = []
            for i, (h, j) in enumerate(chains):
                dzi = g[i] - jnp.exp(lb[i]) * (g[i] + (g_before[i] + cg[h][:, 0:1]))
                dz.append((jnp.where(causal, dzi, 0.0) if diagonal else dzi).astype(BF16))
                cg[h] = cg[h] + jnp.sum(g[i], axis=1, keepdims=True)
            for i, (h, j) in enumerate(chains):
                kh = jnp.where(half == h, blk[j][0], jnp.zeros_like(blk[j][0]))
                dqacc_ref[h] += jnp.dot(dz[i], kh, preferred_element_type=F32)
                dk_ref[pl.ds(starts[j], tb), :] += lax.dot_general(dz[i], qh[h], _TN, preferred_element_type=F32)
                dv_ref[pl.ds(starts[j], tb), :] += lax.dot_general(a16[i], doh[h], _TN, preferred_element_type=F32)
            for h in range(2):
                cc_ref[h] = cc[h]
                cg_ref[h] = cg[h]

        def step(i, _):
            kb = unroll * i
            tiles([kb + u for u in range(unroll)], False)
            return 0

        lax.fori_loop(0, qb // unroll, step, 0)
        for left in range(1, unroll):

            @pl.when(qb % unroll == left)
            def _(left=left):
                tiles([qb - left + u for u in range(left)], False)

        tiles([qb], True)
        dq_ref[...] = dqacc_ref[0] + dqacc_ref[1]

    qblk = pl.BlockSpec((tb, LANES), lambda p, i: (i, p))
    whole = pl.BlockSpec((t, LANES), lambda p, i: (0, p))
    return pl.pallas_call(
        body,
        name=name,
        grid=(n_pairs, t // tb),
        in_specs=[pl.BlockSpec((tb, LANES), lambda p, i: (i, dy0 + p)), qblk, whole, whole, qblk,
                  pl.BlockSpec(order_after.shape, lambda p, i: (0, 0))],
        out_specs=[qblk, whole, whole],
        out_shape=[jax.ShapeDtypeStruct((t, ATTN_W), F32)] * 3,
        scratch_shapes=[pltpu.VMEM((2, tb, LANES), F32)] * 3,
        compiler_params=_params("parallel", "arbitrary"),
    )(dmix, qs, kn, v, lt, order_after)


def _sgu_weights(w_ref):
    tt = lax.broadcasted_iota(jnp.int32, (CHUNK, CHUNK), 0)
    ss = lax.broadcasted_iota(jnp.int32, (CHUNK, CHUNK), 1)
    tril = ss <= tt
    return [jnp.where(tril, w_ref[gi], 0.0).astype(BF16) for gi in range(SGU_HEADS)], tril


def _sgu_fwd(proj, g_v, w_s, b_exp, *, name):
    t = proj.shape[0]
    u0 = (3 * CONV_W + 3 * ATTN_W) // SGU_W

    def body(u_ref, v_ref, g_ref, w_ref, b_ref, o_ref):
        grp = lax.broadcasted_iota(jnp.int32, (1, SGU_W), 1) // HEAD_DIM
        u = _gelu(u_ref[...])
        vv = _gelu(v_ref[...])
        vn = (vv * lax.rsqrt(_group_mean(vv * vv, SGU_W) + EPS) * g_ref[...]).astype(BF16)
        wm, _ = _sgu_weights(w_ref)
        sv = b_ref[...]
        for gi in range(SGU_HEADS):
            sv = sv + jnp.dot(wm[gi], jnp.where(grp == gi, vn, jnp.zeros_like(vn)), preferred_element_type=F32)
        o_ref[...] = (u * sv).astype(o_ref.dtype)

    return pl.pallas_call(
        body,
        name=name,
        grid=(t // CHUNK,),
        in_specs=[
            pl.BlockSpec((CHUNK, SGU_W), lambda i: (i, u0)),
            pl.BlockSpec((CHUNK, SGU_W), lambda i: (i, u0 + 1)),
            pl.BlockSpec((1, SGU_W), lambda i: (0, 0)),
            pl.BlockSpec((SGU_HEADS, CHUNK, CHUNK), lambda i: (0, 0, 0)),
            pl.BlockSpec((CHUNK, SGU_W), lambda i: (0, 0)),
        ],
        out_specs=pl.BlockSpec((CHUNK, SGU_W), lambda i: (i, 0)),
        out_shape=jax.ShapeDtypeStruct((t, SGU_W), BF16),
        compiler_params=_params("parallel"),
    )(proj, proj, g_v, w_s, b_exp)


def _sgu_bwd(dmix, proj, g_v, w_s, b_exp, *, name):
    t = proj.shape[0]
    u0 = (3 * CONV_W + 3 * ATTN_W) // SGU_W
    dy0 = (CONV_W + ATTN_W) // SGU_W

    def body(dy_ref, u_ref, v_ref, g_ref, w_ref, b_ref, du_ref, dv_ref, dg_ref, dw_ref, db_ref):
        grp = lax.broadcasted_iota(jnp.int32, (1, SGU_W), 1) // HEAD_DIM
        cu, cv = u_ref[...], v_ref[...]
        u = _gelu(cu)
        vv = _gelu(cv)
        r = lax.rsqrt(_group_mean(vv * vv, SGU_W) + EPS)
        xhat = vv * r
        gain = g_ref[...]
        vn = (xhat * gain).astype(BF16)
        wm, tril = _sgu_weights(w_ref)
        vng = [jnp.where(grp == gi, vn, jnp.zeros_like(vn)) for gi in range(SGU_HEADS)]
        sv = b_ref[...]
        for gi in range(SGU_HEADS):
            sv = sv + jnp.dot(wm[gi], vng[gi], preferred_element_type=F32)
        dy = dy_ref[...]
        du_ref[...] = (dy * sv * _gelu_grad(cu)).astype(du_ref.dtype)
        dsv = dy * u
        dsv16 = dsv.astype(BF16)

        @pl.when(pl.program_id(0) == 0)
        def _():
            dg_ref[...] = jnp.zeros_like(dg_ref)
            dw_ref[...] = jnp.zeros_like(dw_ref)
            db_ref[...] = jnp.zeros_like(db_ref)

        db_ref[...] += dsv
        dvn = jnp.zeros_like(dsv)
        for gi in range(SGU_HEADS):
            dw = lax.dot_general(dsv16, vng[gi], _NT, preferred_element_type=F32)
            dw_ref[gi] += jnp.where(tril, dw, 0.0)
            dvn_g = lax.dot_general(wm[gi], dsv16, _TN, preferred_element_type=F32)
            dvn = jnp.where(grp == gi, dvn_g, dvn)
        dg_ref[...] += jnp.sum(dvn * xhat, axis=0, keepdims=True)
        dxhat = dvn * gain
        dvv = r * (dxhat - xhat * _group_mean(dxhat * xhat, SGU_W))
        dv_ref[...] = (dvv * _gelu_grad(cv)).astype(dv_ref.dtype)

    return pl.pallas_call(
        body,
        name=name,
        grid=(t // CHUNK,),
        in_specs=[
            pl.BlockSpec((CHUNK, SGU_W), lambda i: (i, dy0)),
            pl.BlockSpec((CHUNK, SGU_W), lambda i: (i, u0)),
            pl.BlockSpec((CHUNK, SGU_W), lambda i: (i, u0 + 1)),
            pl.BlockSpec((1, SGU_W), lambda i: (0, 0)),
            pl.BlockSpec((SGU_HEADS, CHUNK, CHUNK), lambda i: (0, 0, 0)),
            pl.BlockSpec((CHUNK, SGU_W), lambda i: (0, 0)),
        ],
        out_specs=[
            pl.BlockSpec((CHUNK, SGU_W), lambda i: (i, 0)),
            pl.BlockSpec((CHUNK, SGU_W), lambda i: (i, 0)),
            pl.BlockSpec((1, SGU_W), lambda i: (0, 0)),
            pl.BlockSpec((SGU_HEADS, CHUNK, CHUNK), lambda i: (0, 0, 0)),
            pl.BlockSpec((CHUNK, SGU_W), lambda i: (0, 0)),
        ],
        out_shape=[
            jax.ShapeDtypeStruct((t, SGU_W), BF16),
            jax.ShapeDtypeStruct((t, SGU_W), BF16),
            jax.ShapeDtypeStruct((1, SGU_W), F32),
            jax.ShapeDtypeStruct((SGU_HEADS, CHUNK, CHUNK), F32),
            jax.ShapeDtypeStruct((CHUNK, SGU_W), F32),
        ],
        compiler_params=_params("arbitrary"),
    )(dmix, proj, proj, g_v, w_s, b_exp)


def _ple_bwd(dh, gate, pp, order_after, *, name, br=512):
    t, d = dh.shape
    br = min(br, t)

    def body(dh_ref, g_ref, p_ref, order_ref, dpre_ref, dpp_ref):
        dhv, g = dh_ref[...], g_ref[...]
        dpre_ref[...] = (dhv * p_ref[...] * g * (1.0 - g)).astype(dpre_ref.dtype)
        dpp_ref[...] = (dhv * g).astype(dpp_ref.dtype)

    row = pl.BlockSpec((br, d), lambda i: (i, 0))
    return pl.pallas_call(
        body,
        name=name,
        grid=(t // br,),
        in_specs=[row] * 3 + [pl.BlockSpec(order_after.shape, lambda i: (0, 0))],
        out_specs=[row] * 2,
        out_shape=[jax.ShapeDtypeStruct((t, d), BF16)] * 2,
        compiler_params=_params("parallel"),
    )(dh, gate, pp, order_after)


def _loss_head(y, target, *, name, br=512):
    t, d = y.shape
    br = min(br, t)

    def body(y_ref, t_ref, dy_ref, loss_ref):
        err = y_ref[...] - t_ref[...]
        dy_ref[...] = err * (1.0 / d)

        @pl.when(pl.program_id(0) == 0)
        def _():
            loss_ref[...] = jnp.zeros_like(loss_ref)

        loss_ref[...] += 0.5 * jnp.sum(jnp.sum(err * err, axis=1, keepdims=True) * (1.0 / d), axis=0, keepdims=True)

    row = pl.BlockSpec((br, d), lambda i: (i, 0))
    return pl.pallas_call(
        body,
        name=name,
        grid=(t // br,),
        in_specs=[row, row],
        out_specs=[row, pl.BlockSpec((8, LANES), lambda i: (0, 0))],
        out_shape=[jax.ShapeDtypeStruct((t, d), F32), jax.ShapeDtypeStruct((8, LANES), F32)],
        compiler_params=_params("arbitrary"),
    )(y, target)


def _adamw_update(w, g, m, v):
    nm = ADAM_B1 * m + (1.0 - ADAM_B1) * g
    nv = ADAM_B2 * v + (1.0 - ADAM_B2) * (g * g)
    m_hat = nm / (1.0 - ADAM_B1 ** ADAM_STEP)
    v_hat = nv / (1.0 - ADAM_B2 ** ADAM_STEP)
    return -ADAM_LR * (m_hat / (jnp.sqrt(v_hat) + ADAM_EPS) + ADAM_WD * w), nm, nv


def _adamw(w, g, m, v, *, name, br=512):
    r, c = w.shape
    br = _row_block(r, br)

    def body(w_ref, g_ref, m_ref, v_ref, d_ref, nm_ref, nv_ref):
        d_ref[...], nm_ref[...], nv_ref[...] = _adamw_update(w_ref[...], g_ref[...], m_ref[...], v_ref[...])

    row = pl.BlockSpec((br, c), lambda i: (i, 0))
    return pl.pallas_call(
        body,
        name=name,
        grid=(r // br,),
        in_specs=[row] * 4,
        out_specs=[row] * 3,
        out_shape=[jax.ShapeDtypeStruct((r, c), F32)] * 3,
        compiler_params=_params("parallel"),
    )(w, g, m, v)


def _sum_slots(x, *, name, br=512):
    n, r, c = x.shape
    br = _row_block(r, br)

    def body(x_ref, o_ref):
        acc = x_ref[0].astype(F32)
        for j in range(1, n):
            acc = acc + x_ref[j].astype(F32)
        o_ref[...] = acc

    return pl.pallas_call(
        body,
        name=name,
        grid=(r // br,),
        in_specs=[pl.BlockSpec((n, br, c), lambda i: (0, i, 0))],
        out_specs=pl.BlockSpec((br, c), lambda i: (i, 0)),
        out_shape=jax.ShapeDtypeStruct((r, c), F32),
        compiler_params=_params("parallel"),
    )(x)


_ADAMW_BLOCK_ELEMS = 128 * 1024


def _adamw_reduce(w, arrived, m, v, *, name):
    depth, r, c = w.shape
    br = _row_block(r, max(BF16_TILE_ROWS, _ADAMW_BLOCK_ELEMS // (-(-c // LANES) * LANES)))

    def body(w_ref, m_ref, v_ref, *rest):
        parts, (g_ref, d_ref, nm_ref, nv_ref) = rest[:depth], rest[depth:]
        for li in range(depth):

            @pl.when(pl.program_id(0) == li)
            def _(li=li):
                g = parts[li][0].astype(F32)
                for j in range(1, N_DEV):
                    g = g + parts[li][j].astype(F32)
                g_ref[...] = g
                d_ref[...], nm_ref[...], nv_ref[...] = _adamw_update(w_ref[...], g, m_ref[...], v_ref[...])

    cur = pl.BlockSpec((None, br, c), lambda l, i: (l, i, 0))
    slots = [pl.BlockSpec((N_DEV, br, c), lambda l, i, li=li: (0, jnp.where(l == li, i, 0), 0)) for li in range(depth)]
    return pl.pallas_call(
        body,
        name=name,
        grid=(depth, r // br),
        in_specs=[cur, cur, cur] + slots,
        out_specs=[cur] * 4,
        out_shape=[jax.ShapeDtypeStruct((depth, r, c), F32)] * 4,
        compiler_params=_params("arbitrary", "arbitrary"),
    )(w, m, v, *arrived)


def _my_place():
    return lax.axis_index("x"), lax.axis_index("y"), lax.axis_index("c")


def _flip(v, bit):
    return 1 - v if bit else v


def _slot_of(px, py, pc):
    return 4 * px + 2 * py + pc


_ANY = pl.BlockSpec(memory_space=pl.ANY)


_HBM = pl.BlockSpec(memory_space=pltpu.HBM)
_SEM = pl.BlockSpec(memory_space=pltpu.SEMAPHORE)
_DATAFLOW = pltpu.SideEffectType.DATAFLOW_SIDE_EFFECTING


_GATHER, _GATHER_COLUMNS, _SCATTER = "gather", "gather_columns", "scatter"


def _landing_shape(a, mode):
    if mode == _SCATTER:
        return a.shape
    if mode == _GATHER_COLUMNS:
        return (a.shape[0], N_DEV * a.shape[1])
    return (N_DEV, *a.shape)


_DIRECT, _NEAR, _RELAY = "direct", "near", "relay"
_OTHER_CHIPS = (2, 4, 6)
_SIBLING = 1


def _exchange_copies(src_refs, land_refs, send_sem, recv_sem, modes, hops=_DIRECT):
    mx, my, mc = _my_place()
    peer_of = lambda k: (_flip(mx, k & 4), _flip(my, k & 2), _flip(mc, k & 1))
    mine = _slot_of(mx, my, mc)

    def block(land, mode, slot):
        if mode == _GATHER_COLUMNS:
            n = land.shape[1] // N_DEV
            return land.at[:, pl.ds(pl.multiple_of(slot * n, LANES), n)]
        return land.at[slot]

    def remote_copy(src, dst, to):
        return pltpu.make_async_remote_copy(src_ref=src, dst_ref=dst, send_sem=send_sem, recv_sem=recv_sem,
                                            device_id=to, device_id_type=MESH)

    remote, local = [], []
    for src, land, mode in zip(src_refs, land_refs, modes, strict=True):
        if hops == _RELAY:
            assert mode != _SCATTER
            for k in _OTHER_CHIPS:
                came = block(land, mode, _slot_of(*peer_of(k)))
                remote.append(remote_copy(came, came, peer_of(_SIBLING)))
            continue
        dst = block(land, mode, mine)
        for k in ((_SIBLING,) + _OTHER_CHIPS if hops == _NEAR else range(1, N_DEV)):
            remote.append(remote_copy(src.at[_slot_of(*peer_of(k))] if mode == _SCATTER else src, dst, peer_of(k)))
        local.append(pltpu.make_async_copy(src.at[mine] if mode == _SCATTER else src, dst, recv_sem))
    return remote, local


def _wait_copies(remote, local):
    for cp in remote:
        cp.wait_send()
        cp.wait_recv()
    for cp in local:
        cp.wait()


def _exchange_start(groups, after, *, name, hops=_DIRECT):
    sizes = [len(srcs) for srcs, _ in groups]
    n, n_sems = sum(sizes), 2 * len(groups)
    srcs = [a for arrays, _ in groups for a in arrays]
    lands = [lax.empty(_landing_shape(a, mode), a.dtype)
             for arrays, modes in groups for a, mode in zip(arrays, modes, strict=True)]
    offsets = [sum(sizes[:g]) for g in range(len(groups))]

    def body(*refs):
        sems = refs[2 * n + 1:2 * n + 1 + n_sems]
        for g, (off, size, (_, modes)) in enumerate(zip(offsets, sizes, groups)):
            remote, local = _exchange_copies(refs[off:off + size], refs[n + off:n + off + size], sems[2 * g],
                                             sems[2 * g + 1], modes, hops)
            for cp in remote + local:
                cp.start()
        refs[-1][...] = jnp.zeros_like(refs[-1])

    thru = [pltpu.HBM(a.shape, a.dtype) for a in (*srcs, *lands)]
    out = pl.pallas_call(
        body,
        name=name,
        in_specs=[_HBM] * (2 * n) + [_ANY],
        out_specs=(*[_SEM] * n_sems, *[_HBM] * (2 * n), pl.BlockSpec(memory_space=pltpu.VMEM)),
        out_shape=(*[pltpu.SemaphoreType.DMA(())] * n_sems, *thru, jax.ShapeDtypeStruct((8, LANES), F32)),
        input_output_aliases={i: n_sems + i for i in range(2 * n)},
        compiler_params=pltpu.CompilerParams(has_side_effects=_DATAFLOW),
    )(*[pltpu.with_memory_space_constraint(a, pltpu.HBM) for a in (*srcs, *lands)], after)
    sems, arrays = out[:n_sems], out[n_sems:-1]
    started = [(sems[2 * g], sems[2 * g + 1], *arrays[off:off + size], *arrays[n + off:n + off + size])
               for g, (off, size) in enumerate(zip(offsets, sizes))]
    return started, out[-1]


def _exchange_relay(started, after, *, modes, regroup, name):
    send_sem, recv_sem, *thru = started
    n, n_sems = len(thru) // 2, 2 * len(regroup)

    def body(*refs):
        srcs, lands = refs[:n], refs[n:2 * n]
        _wait_copies(*_exchange_copies(srcs, lands, refs[2 * n], refs[2 * n + 1], modes, _NEAR))
        sems = refs[2 * n + 3:2 * n + 3 + n_sems]
        for g, members in enumerate(regroup):
            remote, _ = _exchange_copies([srcs[i] for i in members], [lands[i] for i in members], sems[2 * g],
                                         sems[2 * g + 1], [modes[i] for i in members], _RELAY)
            for cp in remote:
                cp.start()
        refs[-1][...] = jnp.zeros_like(refs[-1])

    out = pl.pallas_call(
        body,
        name=name,
        in_specs=[_HBM] * (2 * n) + [_SEM, _SEM, _ANY],
        out_specs=(*[_SEM] * n_sems, *[_HBM] * (2 * n), pl.BlockSpec(memory_space=pltpu.VMEM)),
        out_shape=(*[pltpu.SemaphoreType.DMA(())] * n_sems, *[pltpu.HBM(a.shape, a.dtype) for a in thru],
                   jax.ShapeDtypeStruct((8, LANES), F32)),
        input_output_aliases={i: n_sems + i for i in range(2 * n)},
        compiler_params=pltpu.CompilerParams(has_side_effects=_DATAFLOW),
    )(*thru, send_sem, recv_sem, after)
    sems, arrays = out[:n_sems], out[n_sems:-1]
    groups = [(sems[2 * g], sems[2 * g + 1], *[arrays[i] for i in members], *[arrays[n + i] for i in members])
              for g, members in enumerate(regroup)]
    return groups, out[-1]


def _exchange_wait(started, after, *, modes, name, hops=_DIRECT):
    send_sem, recv_sem, *thru = started
    n = len(thru) // 2

    def body(*refs):
        _wait_copies(*_exchange_copies(refs[:n], refs[n:2 * n], refs[2 * n], refs[2 * n + 1], modes, hops))

    out = pl.pallas_call(
        body,
        name=name,
        in_specs=[_HBM] * (2 * n) + [_SEM, _SEM, _ANY],
        out_specs=[_HBM] * (2 * n),
        out_shape=[pltpu.HBM(a.shape, a.dtype) for a in thru],
        input_output_aliases={i: i for i in range(2 * n)},
        compiler_params=pltpu.CompilerParams(has_side_effects=_DATAFLOW),
    )(*thru, send_sem, recv_sem, after)
    return out[n:]


_PACK_ROWS = BF16_TILE_ROWS


def _pack(arrays, dtype):
    flat = [a.astype(dtype).reshape(-1) for a in arrays]
    total = sum(f.shape[0] for f in flat)
    padded = -(-total // (LANES * _PACK_ROWS)) * (LANES * _PACK_ROWS)
    if padded > total:
        flat.append(jnp.zeros((padded - total,), dtype))
    return jnp.concatenate(flat).reshape(-1, LANES)


def _unpack(packed, shapes, lead=()):
    flat = packed.reshape(*lead, -1)
    out, off = [], 0
    for s in shapes:
        size = math.prod(s)
        out.append(flat[..., off:off + size].reshape(*lead, *s))
        off += size
    return out


def _gather_columns(g):
    return jnp.moveaxis(g, 0, 1).reshape(g.shape[1], -1)


def _split_rows(w):
    return w.reshape(N_DEV, w.shape[0] // N_DEV, w.shape[1])


_FIRST = ("w_in", "conv_w")
_REST = ("w_out", "w_ff1", "w_ff2", "w_ple_gate", "w_ple_proj")
_REST_GROUPS = (("w_out",), ("w_ff1",), ("w_ff2",), ("w_ple_gate", "w_ple_proj"))
_BIG = ("w_in",) + _REST
_GATHER_MODE = dict(w_in=_GATHER, conv_w=_GATHER, w_out=_GATHER, w_ff1=_GATHER_COLUMNS, w_ff2=_GATHER,
                    w_ple_gate=_GATHER, w_ple_proj=_GATHER_COLUMNS)
_RELAYOUT_AFTER_GATHER = ("w_in", "conv_w")
_SMALL = ("norm1_g", "q_norm_g", "k_norm_g", "sgu_norm_g", "sgu_w", "sgu_b", "norm2_g", "norm3_g")
_ORDER = ("norm1_g", "w_in", "conv_w", "q_norm_g", "k_norm_g", "sgu_norm_g", "sgu_w", "sgu_b", "w_out", "norm2_g",
          "w_ff1", "w_ff2", "norm3_g", "w_ple_gate", "w_ple_proj")


def _whole_matrices(names, landed):
    return {k: _gather_columns(g) if k in _RELAYOUT_AFTER_GATHER else g.reshape(-1, g.shape[-1])
            for k, g in zip(names, landed, strict=True)}


def _layer_forward(h0, p16, s, li, w_first, gathered):
    nm = lambda k: f"{k}_l{li}"
    t = h0.shape[0]
    w = dict(w_first)
    hn1 = _rms_fwd(h0, s["norm1_g"], name=nm("rms1"))
    proj = _matmul(hn1, w["w_in"], name=nm("proj"), bm=t, bn=256)
    y_a = _conv_fwd(proj, w["conv_w"], name=nm("conv"))
    qs, kn, v = _qk_prep(proj, s["gq"], s["gk"], name=nm("qkprep"))
    y_b, lt = _attn_fwd(qs, kn, v, name=nm("attn"))
    gathered["relay_rest"](y_b)
    y_c = _sgu_fwd(proj, s["sgu_norm_g"], s["sgu_w"], s["b_exp"], name=nm("sgu"))
    mix = jnp.concatenate([y_a, y_b, y_c], axis=1)
    w.update(gathered["fetch"](0, mix))
    h1 = _matmul(mix, w["w_out"], name=nm("out"), bm=t, bn=256, extras=(h0,), epilogue=lambda acc, r: (r + acc,))
    hn2 = _rms_fwd(h1, s["norm2_g"], name=nm("rms2"))
    w.update(gathered["fetch"](1, hn2))
    u, f = _matmul(hn2, w["w_ff1"], name=nm("ff1"), bm=t, bn=512, out_dtypes=(F32, BF16),
                   epilogue=lambda acc: (acc, jnp.square(jnp.maximum(acc, 0.0))))
    w.update(gathered["fetch"](2, f))
    h2 = _matmul(f, w["w_ff2"], name=nm("ff2"), bm=512, bn=512, extras=(h1,), epilogue=lambda acc, r: (r + acc,))
    gathered["relay_next"](h2)
    hn3 = _rms_fwd(h2, s["norm3_g"], name=nm("rms3"))
    w.update(gathered["fetch"](3, hn3))
    w_next = gathered["fetch_next"](hn3)
    pp = _matmul(p16, w["w_ple_proj"], name=nm("pleproj"), bm=t, bn=512)

    def gate_epilogue(acc, pp_blk, h_blk):
        gate = jax.nn.sigmoid(acc)
        return h_blk + gate * pp_blk, gate

    h3, gate = _matmul(hn3, w["w_ple_gate"], name=nm("plegate"), bm=t, bn=256, out_dtypes=(F32, F32),
                       extras=(pp, h2), epilogue=gate_epilogue)
    saved = dict(h0=h0, hn1=hn1, proj=proj, qs=qs, kn=kn, v=v, lt=lt, mix=mix, h1=h1, hn2=hn2, u=u, f=f, h2=h2,
                 hn3=hn3, pp=pp, gate=gate, p16=p16)
    return h3, w, w_next, saved


def _layer_backward(dh3, a, w, s, li, order_after, start_rest):
    nm = lambda k: f"{k}_bwd_l{li}"
    t = dh3.shape[0]
    dpre, dpp = _ple_bwd(dh3, a["gate"], a["pp"], order_after, name=nm("ple"))
    g_gate = _weight_grad(a["hn3"], dpre, name=nm("dwgate"))
    g_proj = _weight_grad(a["p16"], dpp, name=nm("dwproj"), column_shards=True)
    dh2, dh2_16, g_n3 = _matmul_rms_bwd(dpre, w["w_ple_gate"], a["h2"], s["norm3_g"], dh3, name=nm("dh2"))
    du = _matmul(dh2_16, w["w_ff2"], name=nm("du"), tb=True, bm=t, bn=512, out_dtypes=(BF16,), extras=(a["u"],),
                 epilogue=lambda acc, u: (acc * (2.0 * jnp.maximum(u, 0.0)),))
    g_ff2 = _weight_grad(a["f"], dh2_16, name=nm("dwff2"))
    g_ff1 = _weight_grad(a["hn2"], du, name=nm("dwff1"), column_shards=True)
    dh1, dh1_16, g_n2 = _matmul_rms_bwd(du, w["w_ff1"], a["h1"], s["norm2_g"], dh2, name=nm("dh1"))
    dmix = _matmul(dh1_16, w["w_out"], name=nm("dmix"), tb=True, bm=t, bn=256)
    g_out = _weight_grad(a["mix"], dh1_16, name=nm("dwout"))
    started = start_rest(dict(w_out=_split_rows(g_out), w_ff1=g_ff1, w_ff2=_split_rows(g_ff2),
                              w_ple_gate=_split_rows(g_gate), w_ple_proj=g_proj), dmix)
    d_b, d_c, d_h, g_conv = _conv_bwd(dmix, a["proj"], w["conv_w"], name=nm("conv"))
    dqs, dkn, dv = _attn_bwd(dmix, a["qs"], a["kn"], a["v"], a["lt"], started, name=nm("attn"))
    d_q, d_k, d_v, g_q, g_k = _qk_prep_bwd(dqs, dkn, dv, a["proj"], s["gq"], s["gk"], name=nm("qkprep"))
    d_cu, d_cv, g_sn, g_sw, g_sb = _sgu_bwd(dmix, a["proj"], s["sgu_norm_g"], s["sgu_w"], s["b_exp"], name=nm("sgu"))
    dproj = jnp.concatenate([d_b, d_c, d_h, d_q, d_k, d_v, d_cu, d_cv], axis=1)
    g_in = _weight_grad(a["hn1"], dproj, name=nm("dwin"), column_shards=True)
    dh0, _, g_n1 = _matmul_rms_bwd(dproj, w["w_in"], a["h0"], s["norm1_g"], dh1, name=nm("dh0"))
    n_tiles = g_q.shape[1] // HEAD_DIM
    small = dict(
        norm1_g=g_n1[0], norm2_g=g_n2[0], norm3_g=g_n3[0],
        q_norm_g=g_q.reshape(n_tiles, HEAD_DIM).sum(0), k_norm_g=g_k.reshape(n_tiles, HEAD_DIM).sum(0),
        sgu_norm_g=g_sn[0], sgu_w=g_sw, sgu_b=g_sb.reshape(CHUNK, SGU_HEADS, HEAD_DIM).sum(-1).T,
        conv_w=g_conv[:CONV_TAPS],
    )
    return dh0, g_in, small


def kernel(x, p, norm1_g, w_in, conv_w, q_norm_g, k_norm_g, sgu_norm_g, sgu_w, sgu_b, w_out, norm2_g, w_ff1, w_ff2, norm3_g, w_ple_gate, w_ple_proj, loss_target, m_norm1_g, m_w_in, m_conv_w, m_q_norm_g, m_k_norm_g, m_sgu_norm_g, m_sgu_w, m_sgu_b, m_w_out, m_norm2_g, m_w_ff1, m_w_ff2, m_norm3_g, m_w_ple_gate, m_w_ple_proj, v_norm1_g, v_w_in, v_conv_w, v_q_norm_g, v_k_norm_g, v_sgu_norm_g, v_sgu_w, v_sgu_b, v_w_out, v_norm2_g, v_w_ff1, v_w_ff2, v_norm3_g, v_w_ple_gate, v_w_ple_proj):
    weights = dict(norm1_g=norm1_g, w_in=w_in, conv_w=conv_w, q_norm_g=q_norm_g, k_norm_g=k_norm_g,
                   sgu_norm_g=sgu_norm_g, sgu_w=sgu_w, sgu_b=sgu_b, w_out=w_out, norm2_g=norm2_g, w_ff1=w_ff1,
                   w_ff2=w_ff2, norm3_g=norm3_g, w_ple_gate=w_ple_gate, w_ple_proj=w_ple_proj)
    mom = dict(norm1_g=m_norm1_g, w_in=m_w_in, conv_w=m_conv_w, q_norm_g=m_q_norm_g, k_norm_g=m_k_norm_g,
               sgu_norm_g=m_sgu_norm_g, sgu_w=m_sgu_w, sgu_b=m_sgu_b, w_out=m_w_out, norm2_g=m_norm2_g, w_ff1=m_w_ff1,
               w_ff2=m_w_ff2, norm3_g=m_norm3_g, w_ple_gate=m_w_ple_gate, w_ple_proj=m_w_ple_proj)
    var = dict(norm1_g=v_norm1_g, w_in=v_w_in, conv_w=v_conv_w, q_norm_g=v_q_norm_g, k_norm_g=v_k_norm_g,
               sgu_norm_g=v_sgu_norm_g, sgu_w=v_sgu_w, sgu_b=v_sgu_b, w_out=v_w_out, norm2_g=v_norm2_g, w_ff1=v_w_ff1,
               w_ff2=v_w_ff2, norm3_g=v_norm3_g, w_ple_gate=v_w_ple_gate, w_ple_proj=v_w_ple_proj)
    depth = norm1_g.shape[0]
    mx, my, mc = _my_place()
    me = _slot_of(mx, my, mc)

    gathers = []
    modes_of = lambda names: tuple(_GATHER_MODE[k] for k in names)
    token = x[0, :8, :LANES]
    for li in range(depth):
        groups = [([weights[k][li] if k == "conv_w" else weights[k][li].astype(BF16) for k in names], modes_of(names))
                  for names in (_FIRST, _REST)]
        started, token = _exchange_start(groups, token, name=f"gather_start_l{li}", hops=_NEAR)
        gathers.append(started)

    small = []
    for li in range(depth):
        small.append(dict(
            norm1_g=norm1_g[li][None], norm2_g=norm2_g[li][None], norm3_g=norm3_g[li][None],
            gq=jnp.tile(q_norm_g[li], _QK_BLOCK // HEAD_DIM)[None], gk=jnp.tile(k_norm_g[li], _QK_BLOCK // HEAD_DIM)[None],
            sgu_norm_g=sgu_norm_g[li][None], sgu_w=sgu_w[li], b_exp=jnp.repeat(sgu_b[li].T, HEAD_DIM, axis=1),
        ))
    small[0]["norm1_g"] = small[0]["norm1_g"] + token[0, 0]

    h = x[0]
    saved, full = [], []
    relayed_first, relayed_rest = [None] * depth, [None] * depth
    rest_members = [[_REST.index(k) for k in names] for names in _REST_GROUPS]

    def relay_first(li, after):
        if li < depth:
            (relayed_first[li],), _ = _exchange_relay(gathers[li][0], after, modes=modes_of(_FIRST),
                                                      regroup=[list(range(len(_FIRST)))], name=f"gather_first_relay_l{li}")

    def fetch_first(li, after):
        if li == depth:
            return None
        landed = _exchange_wait(relayed_first[li], after, modes=modes_of(_FIRST), hops=_RELAY,
                                name=f"gather_first_wait_l{li}")
        return _whole_matrices(_FIRST, landed)

    relay_first(0, token)
    w_first = fetch_first(0, small[0]["norm1_g"])
    for li in range(depth):

        def relay_rest(after, li=li):
            relayed_rest[li], _ = _exchange_relay(gathers[li][1], after, modes=modes_of(_REST), regroup=rest_members,
                                                  name=f"gather_rest_relay_l{li}")

        def fetch(g, after, li=li):
            landed = _exchange_wait(relayed_rest[li][g], after, modes=modes_of(_REST_GROUPS[g]), hops=_RELAY,
                                    name=f"gather_{_REST_GROUPS[g][0]}_wait_l{li}")
            return _whole_matrices(_REST_GROUPS[g], landed)

        gathered = dict(relay_rest=relay_rest, fetch=fetch, relay_next=functools.partial(relay_first, li + 1),
                        fetch_next=functools.partial(fetch_first, li + 1))
        h, w, w_first, acts = _layer_forward(h, p[li, 0].astype(BF16), small[li], li, w_first, gathered)
        full.append(w)
        saved.append(acts)
    dh, loss_tile = _loss_head(h, loss_target[0], name="loss_head")
    loss = lax.psum(loss_tile[0, 0], ("x", "y", "c"))

    small_names = _SMALL + ("conv_w",)
    scatter_first, scatter_rest = [None] * depth, [None] * depth
    first_modes, rest_modes = (_SCATTER, _GATHER), (_SCATTER,) * len(_REST)
    token = loss_tile
    for li in reversed(range(depth)):

        def start_rest(parts, after, li=li):
            (scatter_rest[li],), started = _exchange_start([([parts[k] for k in _REST], rest_modes)], after,
                                                           name=f"scatter_rest_start_l{li}")
            return started

        dh, g_in, small_grads = _layer_backward(dh, saved[li], full[li], small[li], li, token, start_rest)
        small_shapes = [small_grads[k].shape for k in small_names]
        (scatter_first[li],), token = _exchange_start(
            [([g_in, _pack([small_grads[k] for k in small_names], F32)], first_modes)], dh,
            name=f"scatter_first_start_l{li}")
    grad_x = dh[None]

    grads, delta, new_m, new_v = {}, {}, {}, {}
    arrived = {k: [None] * depth for k in _BIG}
    for li in reversed(range(depth)):
        landed = _exchange_wait(scatter_rest[li], token, modes=rest_modes, name=f"scatter_rest_wait_l{li}")
        for k, g in zip(_REST, landed, strict=True):
            arrived[k][li] = g
    for k in _REST:
        grads[k], delta[k], new_m[k], new_v[k] = _adamw_reduce(weights[k], arrived[k], mom[k], var[k], name=f"adamw_{k}")
    small_sums = [None] * depth
    updated = jnp.stack([delta[k][0, 0, :1] for k in _REST])
    for li in reversed(range(depth)):
        arrived["w_in"][li], small_parts = _exchange_wait(scatter_first[li], updated, modes=first_modes,
                                                          name=f"scatter_first_wait_l{li}")
        small_sums[li] = _unpack(_sum_slots(small_parts, name=f"sum_small_grads_l{li}"), small_shapes)
    grads["w_in"], delta["w_in"], new_m["w_in"], new_v["w_in"] = _adamw_reduce(
        w_in, arrived["w_in"], mom["w_in"], var["w_in"], name="adamw_w_in")
    for i, k in enumerate(small_names):
        grads[k] = jnp.stack([small_sums[li][i] for li in range(depth)])
    n_conv = conv_w.shape[2]
    grads["conv_w"] = lax.dynamic_slice_in_dim(grads["conv_w"], me * n_conv, n_conv, axis=2)
    for k in small_names:
        as_rows = lambda a: a.reshape(-1, a.shape[-1])
        outs = _adamw(as_rows(weights[k]), as_rows(grads[k]), as_rows(mom[k]), as_rows(var[k]), name=f"adamw_{k}")
        delta[k], new_m[k], new_v[k] = (o.reshape(weights[k].shape) for o in outs)

    return (loss, grad_x, *[grads[k] for k in _ORDER], *[delta[k] for k in _ORDER],
            *[new_m[k] for k in _ORDER], *[new_v[k] for k in _ORDER])
```

```python
import functools
import math

import jax
import jax.numpy as jnp
from jax import lax
from jax.experimental import pallas as pl
from jax.experimental.pallas import tpu as pltpu

F32 = jnp.float32
BF16 = jnp.bfloat16

N_DEV = 8
HEAD_DIM = 64
CONV_W = 256
ATTN_W = 512
SGU_W = 256
SGU_HEADS = 4
CHUNK = 128
CONV_TAPS = 3
EPS = 1e-6
QK_SCALE = HEAD_DIM ** -0.5

ADAM_LR = 0.001
ADAM_B1 = 0.9
ADAM_B2 = 0.999
ADAM_EPS = 1e-08
ADAM_WD = 0.01
ADAM_STEP = 10

LANES = 128
BF16_TILE_ROWS = 16
VMEM_LIMIT_BYTES = 56 * 1024 * 1024
MESH = pl.DeviceIdType.MESH


def _params(*sem):
    return pltpu.CompilerParams(dimension_semantics=sem, vmem_limit_bytes=VMEM_LIMIT_BYTES)


def _row_block(rows, cap):
    if rows <= cap:
        return rows
    return max(b for b in range(BF16_TILE_ROWS, cap + 1, BF16_TILE_ROWS) if rows % b == 0)


def _matmul(a, b, *, name, tb=False, bm=512, bn=512, out_dtypes=(F32,), epilogue=None, extras=()):
    m, k = a.shape
    n = b.shape[0] if tb else b.shape[1]
    assert k == (b.shape[1] if tb else b.shape[0])
    bm, bn = min(bm, m), min(bn, n)
    assert m % bm == 0 and n % bn == 0
    a_spec = pl.BlockSpec((bm, k), lambda i, j: (i, 0))
    b_spec = pl.BlockSpec((bn, k), lambda i, j: (j, 0)) if tb else pl.BlockSpec((k, bn), lambda i, j: (0, j))
    dims = (((1,), (1 if tb else 0,)), ((), ()))
    n_ex = len(extras)
    for e in extras:
        assert e.shape == (m, n), (e.shape, m, n)

    def body(a_ref, b_ref, *rest):
        outs = rest[n_ex:]
        acc = lax.dot_general(a_ref[...], b_ref[...], dims, preferred_element_type=F32)
        res = (acc,) if epilogue is None else epilogue(acc, *[e[...] for e in rest[:n_ex]])
        for o_ref, r in zip(outs, res, strict=True):
            o_ref[...] = r.astype(o_ref.dtype)

    tile = pl.BlockSpec((bm, bn), lambda i, j: (i, j))
    out = pl.pallas_call(
        body,
        name=name,
        grid=(m // bm, n // bn),
        in_specs=[a_spec, b_spec] + [tile] * n_ex,
        out_specs=[tile] * len(out_dtypes),
        out_shape=[jax.ShapeDtypeStruct((m, n), d) for d in out_dtypes],
        compiler_params=_params("parallel", "parallel"),
    )(a, b, *extras)
    return out[0] if len(out_dtypes) == 1 else out


_WEIGHT_GRAD_ACC_ELEMS = 1024 * 1024


def _weight_grad(x, dy, *, name, column_shards=False):
    t, m = x.shape
    n = dy.shape[1]
    bm = m if m <= 2 * LANES else min(m // 2, max(LANES, _WEIGHT_GRAD_ACC_ELEMS // n // LANES * LANES))
    assert m % bm == 0
    ns = n // N_DEV

    def body(x_ref, dy_ref, o_ref):
        acc = lax.dot_general(x_ref[...], dy_ref[...], _TN, preferred_element_type=F32)
        if column_shards:
            for s in range(N_DEV):
                o_ref[s] = acc[:, s * ns:(s + 1) * ns].astype(o_ref.dtype)
        else:
            o_ref[...] = acc.astype(o_ref.dtype)

    if column_shards:
        out_spec, out_dims = pl.BlockSpec((N_DEV, bm, ns), lambda i: (0, i, 0)), (N_DEV, m, ns)
    else:
        out_spec, out_dims = pl.BlockSpec((bm, n), lambda i: (i, 0)), (m, n)
    return pl.pallas_call(
        body,
        name=name,
        grid=(m // bm,),
        in_specs=[pl.BlockSpec((t, bm), lambda i: (0, i)), pl.BlockSpec((t, n), lambda i: (0, 0))],
        out_specs=out_spec,
        out_shape=jax.ShapeDtypeStruct(out_dims, BF16),
        compiler_params=_params("parallel"),
    )(x, dy)


def _rms_fwd(h, g, *, name, br=512):
    t, d = h.shape
    br = min(br, t)

    def body(h_ref, g_ref, o_ref):
        x = h_ref[...]
        r = lax.rsqrt(jnp.mean(x * x, axis=-1, keepdims=True) + EPS)
        o_ref[...] = (x * r * g_ref[...]).astype(o_ref.dtype)

    return pl.pallas_call(
        body,
        name=name,
        grid=(t // br,),
        in_specs=[pl.BlockSpec((br, d), lambda i: (i, 0)), pl.BlockSpec((1, d), lambda i: (0, 0))],
        out_specs=pl.BlockSpec((br, d), lambda i: (i, 0)),
        out_shape=jax.ShapeDtypeStruct((t, d), BF16),
        compiler_params=_params("parallel"),
    )(h, g)


def _matmul_rms_bwd(dz, w, h, g, dres, *, name):
    t, d = h.shape
    k = dz.shape[1]
    br = min(t, 512 if k <= d else 256)

    def body(dz_ref, w_ref, h_ref, g_ref, dres_ref, dh_ref, dh16_ref, dg_ref):
        x = h_ref[...]
        dyv = lax.dot_general(dz_ref[...], w_ref[...], _NT, preferred_element_type=F32)
        r = lax.rsqrt(jnp.mean(x * x, axis=-1, keepdims=True) + EPS)
        xhat = x * r
        dxhat = dyv * g_ref[...]
        dh = dres_ref[...] + r * (dxhat - xhat * jnp.mean(dxhat * xhat, axis=-1, keepdims=True))
        dh_ref[...] = dh
        dh16_ref[...] = dh.astype(dh16_ref.dtype)

        @pl.when(pl.program_id(0) == 0)
        def _():
            dg_ref[...] = jnp.zeros_like(dg_ref)

        dg_ref[...] += jnp.sum(dyv * xhat, axis=0, keepdims=True)

    row = pl.BlockSpec((br, d), lambda i: (i, 0))
    vec = pl.BlockSpec((1, d), lambda i: (0, 0))
    return pl.pallas_call(
        body,
        name=name,
        grid=(t // br,),
        in_specs=[pl.BlockSpec((br, k), lambda i: (i, 0)), pl.BlockSpec((d, k), lambda i: (0, 0)), row, vec, row],
        out_specs=[row, row, vec],
        out_shape=[jax.ShapeDtypeStruct((t, d), F32), jax.ShapeDtypeStruct((t, d), BF16),
                   jax.ShapeDtypeStruct((1, d), F32)],
        compiler_params=_params("arbitrary"),
    )(dz, w, h, g, dres)


def _group_mean(x, width):
    grp = lax.broadcasted_iota(jnp.int32, x.shape, 1) // HEAD_DIM
    out = jnp.zeros_like(x)
    for gi in range(width // HEAD_DIM):
        m = grp == gi
        s = jnp.sum(jnp.where(m, x, 0.0), axis=1, keepdims=True)
        out = jnp.where(m, s, out)
    return out * (1.0 / HEAD_DIM)


def _gelu(x):
    return 0.5 * x * (1.0 + lax.erf(x * (2.0 ** -0.5)))


def _gelu_grad(x):
    cdf = 0.5 * (1.0 + lax.erf(x * (2.0 ** -0.5)))
    pdf = jnp.exp(-0.5 * x * x) * (1.0 / math.sqrt(2.0 * math.pi))
    return cdf + x * pdf


def _shift_down(z, s, row):
    return jnp.where(row >= s, pltpu.roll(z, s, 0), 0.0)


def _shift_up(z, s, row, t):
    return jnp.where(row < t - s, pltpu.roll(z, t - s, 0), 0.0)


def _conv_fwd(proj, conv_w, *, name):
    t = proj.shape[0]
    nb = CONV_W // LANES

    def body(b_ref, c_ref, h_ref, w_ref, o_ref):
        row = lax.broadcasted_iota(jnp.int32, (t, LANES), 0)
        z = c_ref[...] * h_ref[...]
        w = w_ref[...]
        conv = w[2:3, :] * z + w[1:2, :] * _shift_down(z, 1, row) + w[0:1, :] * _shift_down(z, 2, row)
        o_ref[...] = (b_ref[...] * conv).astype(o_ref.dtype)

    return pl.pallas_call(
        body,
        name=name,
        grid=(nb,),
        in_specs=[
            pl.BlockSpec((t, LANES), lambda j: (0, j)),
            pl.BlockSpec((t, LANES), lambda j: (0, nb + j)),
            pl.BlockSpec((t, LANES), lambda j: (0, 2 * nb + j)),
            pl.BlockSpec((CONV_TAPS, LANES), lambda j: (0, j)),
        ],
        out_specs=pl.BlockSpec((t, LANES), lambda j: (0, j)),
        out_shape=jax.ShapeDtypeStruct((t, CONV_W), BF16),
        compiler_params=_params("parallel"),
    )(proj, proj, proj, conv_w)


def _conv_bwd(dmix, proj, conv_w, *, name):
    t = proj.shape[0]
    nb = CONV_W // LANES

    def body(dy_ref, b_ref, c_ref, h_ref, w_ref, db_ref, dc_ref, dh_ref, dw_ref):
        row = lax.broadcasted_iota(jnp.int32, (t, LANES), 0)
        ac, ah = c_ref[...], h_ref[...]
        z = ac * ah
        w = w_ref[...]
        z1 = _shift_down(z, 1, row)
        z2 = _shift_down(z, 2, row)
        conv = w[2:3, :] * z + w[1:2, :] * z1 + w[0:1, :] * z2
        dy = dy_ref[...]
        db_ref[...] = (dy * conv).astype(db_ref.dtype)
        dconv = dy * b_ref[...]
        dz = w[2:3, :] * dconv + w[1:2, :] * _shift_up(dconv, 1, row, t) + w[0:1, :] * _shift_up(dconv, 2, row, t)
        dc_ref[...] = (dz * ah).astype(dc_ref.dtype)
        dh_ref[...] = (dz * ac).astype(dh_ref.dtype)
        dw_ref[...] = jnp.zeros_like(dw_ref)
        dw_ref[0:1, :] = jnp.sum(dconv * z2, axis=0, keepdims=True)
        dw_ref[1:2, :] = jnp.sum(dconv * z1, axis=0, keepdims=True)
        dw_ref[2:3, :] = jnp.sum(dconv * z, axis=0, keepdims=True)

    col = lambda off: pl.BlockSpec((t, LANES), lambda j: (0, off + j))
    return pl.pallas_call(
        body,
        name=name,
        grid=(nb,),
        in_specs=[col(0), col(0), col(nb), col(2 * nb), pl.BlockSpec((CONV_TAPS, LANES), lambda j: (0, j))],
        out_specs=[col(0), col(0), col(0), pl.BlockSpec((8, LANES), lambda j: (0, j))],
        out_shape=[jax.ShapeDtypeStruct((t, CONV_W), BF16)] * 3 + [jax.ShapeDtypeStruct((8, CONV_W), F32)],
        compiler_params=_params("parallel"),
    )(dmix, proj, proj, proj, conv_w)


_QK_BLOCK = 256


def _qk_prep(proj, gq, gk, *, name, br=512):
    t = proj.shape[0]
    br = min(br, t)
    nb = ATTN_W // _QK_BLOCK
    q0 = (3 * CONV_W) // _QK_BLOCK

    def body(q_ref, k_ref, v_ref, gq_ref, gk_ref, qo_ref, ko_ref, vo_ref):
        q = q_ref[...]
        k = k_ref[...]
        rq = lax.rsqrt(_group_mean(q * q, _QK_BLOCK) + EPS)
        rk = lax.rsqrt(_group_mean(k * k, _QK_BLOCK) + EPS)
        qo_ref[...] = ((q * rq * gq_ref[...]).astype(BF16) * QK_SCALE).astype(qo_ref.dtype)
        ko_ref[...] = (k * rk * gk_ref[...]).astype(ko_ref.dtype)
        vo_ref[...] = v_ref[...].astype(vo_ref.dtype)

    col = lambda off: pl.BlockSpec((br, _QK_BLOCK), lambda i, j: (i, off + j))
    vec = pl.BlockSpec((1, _QK_BLOCK), lambda i, j: (0, 0))
    return pl.pallas_call(
        body,
        name=name,
        grid=(t // br, nb),
        in_specs=[col(q0), col(q0 + nb), col(q0 + 2 * nb), vec, vec],
        out_specs=[col(0)] * 3,
        out_shape=[jax.ShapeDtypeStruct((t, ATTN_W), BF16)] * 3,
        compiler_params=_params("parallel", "parallel"),
    )(proj, proj, proj, gq, gk)


def _qk_prep_bwd(dqs, dkn, dv, proj, gq, gk, *, name, br=512):
    t = proj.shape[0]
    br = min(br, t)
    nb = ATTN_W // _QK_BLOCK
    q0 = (3 * CONV_W) // _QK_BLOCK

    def norm_bwd(dy, x, g):
        r = lax.rsqrt(_group_mean(x * x, _QK_BLOCK) + EPS)
        xhat = x * r
        dxhat = dy * g
        dx = r * (dxhat - xhat * _group_mean(dxhat * xhat, _QK_BLOCK))
        return dx, jnp.sum(dy * xhat, axis=0, keepdims=True)

    def body(dq_ref, dk_ref, dv_ref, q_ref, k_ref, gq_ref, gk_ref, oq_ref, ok_ref, ov_ref, dgq_ref, dgk_ref):
        dq, dgq = norm_bwd(dq_ref[...] * QK_SCALE, q_ref[...], gq_ref[...])
        dk, dgk = norm_bwd(dk_ref[...], k_ref[...], gk_ref[...])
        oq_ref[...] = dq.astype(oq_ref.dtype)
        ok_ref[...] = dk.astype(ok_ref.dtype)
        ov_ref[...] = dv_ref[...].astype(ov_ref.dtype)

        @pl.when((pl.program_id(0) == 0) & (pl.program_id(1) == 0))
        def _():
            dgq_ref[...] = jnp.zeros_like(dgq_ref)
            dgk_ref[...] = jnp.zeros_like(dgk_ref)

        dgq_ref[...] += dgq
        dgk_ref[...] += dgk

    col = lambda off: pl.BlockSpec((br, _QK_BLOCK), lambda i, j: (i, off + j))
    vec = pl.BlockSpec((1, _QK_BLOCK), lambda i, j: (0, 0))
    return pl.pallas_call(
        body,
        name=name,
        grid=(t // br, nb),
        in_specs=[col(0), col(0), col(0), col(q0), col(q0 + nb), vec, vec],
        out_specs=[col(0), col(0), col(0), vec, vec],
        out_shape=[jax.ShapeDtypeStruct((t, ATTN_W), BF16)] * 3 + [jax.ShapeDtypeStruct((1, _QK_BLOCK), F32)] * 2,
        compiler_params=_params("arbitrary", "arbitrary"),
    )(dqs, dkn, dv, proj, proj, gq, gk)


def _key_order_matrix(tb, relation):
    jj = lax.broadcasted_iota(jnp.int32, (tb, tb), 0)
    ss = lax.broadcasted_iota(jnp.int32, (tb, tb), 1)
    return relation(jj, ss).astype(BF16)


def _log_sigmoids(z):
    lb = jnp.minimum(z, 0.0) - jnp.log(1.0 + jnp.exp(-jnp.abs(z)))
    return lb, lb - z


def _below_diagonal(tb):
    return lax.broadcasted_iota(jnp.int32, (tb, tb), 1) < lax.broadcasted_iota(jnp.int32, (tb, tb), 0)


_NT = (((1,), (1,)), ((), ()))
_TN = (((0,), (0,)), ((), ()))
_ATTN_BLOCK = 256
_ATTN_FWD_UNROLL = 2
_ATTN_BWD_UNROLL = 3


def _attn_fwd(qs, kn, v, *, name, tb=_ATTN_BLOCK, unroll=_ATTN_FWD_UNROLL):
    t = qs.shape[0]
    tb = min(tb, t)
    assert t % tb == 0
    n_pairs = ATTN_W // LANES

    def body(q_ref, k_ref, v_ref, o_ref, lt_ref, acc_ref, carry_ref):
        qb = pl.program_id(1)
        half = lax.broadcasted_iota(jnp.int32, (1, LANES), 1) // HEAD_DIM
        later = _key_order_matrix(tb, lambda j, s: j > s)
        acc_ref[...] = jnp.zeros_like(acc_ref)
        carry_ref[...] = jnp.zeros_like(carry_ref)
        q = q_ref[...]
        qh = [jnp.where(half == h, q, jnp.zeros_like(q)) for h in range(2)]

        def tiles(kbs, diagonal):
            blk = []
            for kb in kbs:
                start = pl.multiple_of(kb * tb, tb)
                blk.append((k_ref[pl.ds(start, tb), :], v_ref[pl.ds(start, tb), :]))
            chains = [(h, j) for j in range(len(kbs)) for h in range(2)]
            z = [lax.dot_general(qh[h], blk[j][0], _NT, preferred_element_type=F32) for h, j in chains]
            causal = _below_diagonal(tb) if diagonal else None
            lb, lr = [], []
            for zi in z:
                b, r = _log_sigmoids(zi)
                lb.append(b)
                lr.append(jnp.where(causal, r, 0.0) if diagonal else r)
            suffix = [jnp.dot(r.astype(BF16), later, preferred_element_type=F32) for r in lr]
            carry = [carry_ref[0], carry_ref[1]]
            w = []
            for i, (h, j) in enumerate(chains):
                wi = jnp.exp(lb[i] + (suffix[i] + carry[h][:, 0:1]))
                w.append((jnp.where(causal, wi, 0.0) if diagonal else wi).astype(BF16))
                carry[h] = carry[h] + jnp.sum(lr[i], axis=1, keepdims=True)
            for i, (h, j) in enumerate(chains):
                vh = jnp.where(half == h, blk[j][1], jnp.zeros_like(blk[j][1]))
                acc_ref[h] += jnp.dot(w[i], vh, preferred_element_type=F32)
            carry_ref[0] = carry[0]
            carry_ref[1] = carry[1]

        tiles([qb], True)

        def step(i, _):
            kb = qb - 1 - unroll * i
            tiles([kb - u for u in range(unroll)], False)
            return 0

        lax.fori_loop(0, qb // unroll, step, 0)
        for left in range(1, unroll):

            @pl.when(qb % unroll == left)
            def _(left=left):
                tiles([left - 1 - u for u in range(left)], False)

        o_ref[...] = (acc_ref[0] + acc_ref[1]).astype(o_ref.dtype)
        lt_ref[...] = jnp.where(half == 0, carry_ref[0], carry_ref[1])

    return pl.pallas_call(
        body,
        name=name,
        grid=(n_pairs, t // tb),
        in_specs=[
            pl.BlockSpec((tb, LANES), lambda p, i: (i, p)),
            pl.BlockSpec((t, LANES), lambda p, i: (0, p)),
            pl.BlockSpec((t, LANES), lambda p, i: (0, p)),
        ],
        out_specs=[pl.BlockSpec((tb, LANES), lambda p, i: (i, p))] * 2,
        out_shape=[jax.ShapeDtypeStruct((t, ATTN_W), BF16), jax.ShapeDtypeStruct((t, ATTN_W), F32)],
        scratch_shapes=[pltpu.VMEM((2, tb, LANES), F32), pltpu.VMEM((2, tb, LANES), F32)],
        compiler_params=_params("parallel", "parallel"),
    )(qs, kn, v)


def _attn_bwd(dmix, qs, kn, v, lt, order_after, *, name, tb=_ATTN_BLOCK, unroll=_ATTN_BWD_UNROLL):
    t = qs.shape[0]
    tb = min(tb, t)
    assert t % tb == 0
    n_pairs = ATTN_W // LANES
    dy0 = CONV_W // LANES

    def body(do_ref, q_ref, k_ref, v_ref, lt_ref, order_ref, dq_ref, dk_ref, dv_ref, dqacc_ref, cc_ref, cg_ref):
        qb = pl.program_id(1)
        half = lax.broadcasted_iota(jnp.int32, (1, LANES), 1) // HEAD_DIM
        lane = lax.broadcasted_iota(jnp.int32, (tb, LANES), 1)
        later = _key_order_matrix(tb, lambda j, s: j > s)
        before = _key_order_matrix(tb, lambda j, s: j < s)
        q = q_ref[...]
        do = do_ref[...].astype(BF16)
        lt = lt_ref[...]
        qh = [jnp.where(half == h, q, jnp.zeros_like(q)) for h in range(2)]
        doh = [jnp.where(half == h, do, jnp.zeros_like(do)) for h in range(2)]
        lth = [jnp.sum(jnp.where(lane == h * HEAD_DIM, lt, 0.0), axis=1, keepdims=True) for h in range(2)]

        @pl.when(qb == 0)
        def _():
            dk_ref[...] = jnp.zeros_like(dk_ref)
            dv_ref[...] = jnp.zeros_like(dv_ref)

        dqacc_ref[...] = jnp.zeros_like(dqacc_ref)
        cc_ref[...] = jnp.zeros_like(cc_ref)
        cg_ref[...] = jnp.zeros_like(cg_ref)

        def tiles(kbs, diagonal):
            starts = [pl.multiple_of(kb * tb, tb) for kb in kbs]
            blk = [(k_ref[pl.ds(s, tb), :], v_ref[pl.ds(s, tb), :]) for s in starts]
            chains = [(h, j) for j in range(len(kbs)) for h in range(2)]
            z = [lax.dot_general(qh[h], blk[j][0], _NT, preferred_element_type=F32) for h, j in chains]
            da = [lax.dot_general(doh[h], jnp.where(half == h, blk[j][1], jnp.zeros_like(blk[j][1])), _NT,
                                  preferred_element_type=F32) for h, j in chains]
            causal = _below_diagonal(tb) if diagonal else None
            lb, lr = [], []
            for zi in z:
                b, r = _log_sigmoids(zi)
                lb.append(b)
                lr.append(jnp.where(causal, r, 0.0) if diagonal else r)
            suffix = [jnp.dot(r.astype(BF16), later, preferred_element_type=F32) for r in lr]
            cc = [cc_ref[0], cc_ref[1]]
            cg = [cg_ref[0], cg_ref[1]]
            a16, g = [], []
            for i, (h, j) in enumerate(chains):
                cc[h] = cc[h] + jnp.sum(lr[i], axis=1, keepdims=True)
                a = jnp.exp(lb[i] + suffix[i] + (lth[h] - cc[h][:, 0:1]))
                if diagonal:
                    a = jnp.where(causal, a, 0.0)
                a16.append(a.astype(BF16))
                g.append(da[i] * a)
            g_before = [jnp.dot(gi.astype(BF16), before, preferred_element_type=F32) for gi in g]
            dz = []
            for i, (h, j) in enumerate(chains):
                dzi = g[i] - jnp.exp(lb[i]) * (g[i] + (g_before[i] + cg[h][:, 0:1]))
                dz.append((jnp.where(causal, dzi, 0.0) if diagonal else dzi).astype(BF16))
                cg[h] = cg[h] + jnp.sum(g[i], axis=1, keepdims=True)
            for i, (h, j) in enumerate(chains):
                kh = jnp.where(half == h, blk[j][0], jnp.zeros_like(blk[j][0]))
                dqacc_ref[h] += jnp.dot(dz[i], kh, preferred_element_type=F32)
                dk_ref[pl.ds(starts[j], tb), :] += lax.dot_general(dz[i], qh[h], _TN, preferred_element_type=F32)
                dv_ref[pl.ds(starts[j], tb), :] += lax.dot_general(a16[i], doh[h], _TN, preferred_element_type=F32)
            for h in range(2):
                cc_ref[h] = cc[h]
                cg_ref[h] = cg[h]

        def step(i, _):
            kb = unroll * i
            tiles([kb + u for u in range(unroll)], False)
            return 0

        lax.fori_loop(0, qb // unroll, step, 0)
        for left in range(1, unroll):

            @pl.when(qb % unroll == left)
            def _(left=left):
                tiles([qb - left + u for u in range(left)], False)

        tiles([qb], True)
        dq_ref[...] = dqacc_ref[0] + dqacc_ref[1]

    qblk = pl.BlockSpec((tb, LANES), lambda p, i: (i, p))
    whole = pl.BlockSpec((t, LANES), lambda p, i: (0, p))
    return pl.pallas_call(
        body,
        name=name,
        grid=(n_pairs, t // tb),
        in_specs=[pl.BlockSpec((tb, LANES), lambda p, i: (i, dy0 + p)), qblk, whole, whole, qblk,
                  pl.BlockSpec(order_after.shape, lambda p, i: (0, 0))],
        out_specs=[qblk, whole, whole],
        out_shape=[jax.ShapeDtypeStruct((t, ATTN_W), F32)] * 3,
        scratch_shapes=[pltpu.VMEM((2, tb, LANES), F32)] * 3,
        compiler_params=_params("parallel", "arbitrary"),
    )(dmix, qs, kn, v, lt, order_after)


_SGU_CHUNKS_PER_STEP = 4


def _sgu_rows(t):
    return CHUNK * math.gcd(_SGU_CHUNKS_PER_STEP, t // CHUNK)


def _sgu_weights(w_ref):
    tt = lax.broadcasted_iota(jnp.int32, (CHUNK, CHUNK), 0)
    ss = lax.broadcasted_iota(jnp.int32, (CHUNK, CHUNK), 1)
    tril = ss <= tt
    return [jnp.where(tril, w_ref[gi], 0.0).astype(BF16) for gi in range(SGU_HEADS)], tril


def _sgu_fwd(proj, g_v, w_s, b_exp, *, name):
    t = proj.shape[0]
    u0 = (3 * CONV_W + 3 * ATTN_W) // SGU_W
    rows = _sgu_rows(t)

    def body(u_ref, v_ref, g_ref, w_ref, b_ref, o_ref):
        grp = lax.broadcasted_iota(jnp.int32, (1, SGU_W), 1) // HEAD_DIM
        wm, _ = _sgu_weights(w_ref)
        gain, bias = g_ref[...], b_ref[...]
        for c in range(rows // CHUNK):
            chunk = pl.ds(c * CHUNK, CHUNK)
            u = _gelu(u_ref[chunk, :])
            vv = _gelu(v_ref[chunk, :])
            vn = (vv * lax.rsqrt(_group_mean(vv * vv, SGU_W) + EPS) * gain).astype(BF16)
            sv = bias
            for gi in range(SGU_HEADS):
                sv = sv + jnp.dot(wm[gi], jnp.where(grp == gi, vn, jnp.zeros_like(vn)), preferred_element_type=F32)
            o_ref[chunk, :] = (u * sv).astype(o_ref.dtype)

    return pl.pallas_call(
        body,
        name=name,
        grid=(t // rows,),
        in_specs=[
            pl.BlockSpec((rows, SGU_W), lambda i: (i, u0)),
            pl.BlockSpec((rows, SGU_W), lambda i: (i, u0 + 1)),
            pl.BlockSpec((1, SGU_W), lambda i: (0, 0)),
            pl.BlockSpec((SGU_HEADS, CHUNK, CHUNK), lambda i: (0, 0, 0)),
            pl.BlockSpec((CHUNK, SGU_W), lambda i: (0, 0)),
        ],
        out_specs=pl.BlockSpec((rows, SGU_W), lambda i: (i, 0)),
        out_shape=jax.ShapeDtypeStruct((t, SGU_W), BF16),
        compiler_params=_params("parallel"),
    )(proj, proj, g_v, w_s, b_exp)


def _sgu_bwd(dmix, proj, g_v, w_s, b_exp, *, name):
    t = proj.shape[0]
    u0 = (3 * CONV_W + 3 * ATTN_W) // SGU_W
    dy0 = (CONV_W + ATTN_W) // SGU_W
    rows = _sgu_rows(t)

    def body(dy_ref, u_ref, v_ref, g_ref, w_ref, b_ref, du_ref, dv_ref, dg_ref, dw_ref, db_ref):
        grp = lax.broadcasted_iota(jnp.int32, (1, SGU_W), 1) // HEAD_DIM
        gain, bias = g_ref[...], b_ref[...]
        wm, tril = _sgu_weights(w_ref)

        @pl.when(pl.program_id(0) == 0)
        def _():
            dg_ref[...] = jnp.zeros_like(dg_ref)
            dw_ref[...] = jnp.zeros_like(dw_ref)
            db_ref[...] = jnp.zeros_like(db_ref)

        dg = jnp.zeros_like(gain)
        db = jnp.zeros_like(bias)
        dw = [jnp.zeros((CHUNK, CHUNK), F32) for _ in range(SGU_HEADS)]
        for c in range(rows // CHUNK):
            chunk = pl.ds(c * CHUNK, CHUNK)
            cu, cv = u_ref[chunk, :], v_ref[chunk, :]
            u = _gelu(cu)
            vv = _gelu(cv)
            r = lax.rsqrt(_group_mean(vv * vv, SGU_W) + EPS)
            xhat = vv * r
            vn = (xhat * gain).astype(BF16)
            vng = [jnp.where(grp == gi, vn, jnp.zeros_like(vn)) for gi in range(SGU_HEADS)]
            sv = bias
            for gi in range(SGU_HEADS):
                sv = sv + jnp.dot(wm[gi], vng[gi], preferred_element_type=F32)
            dy = dy_ref[chunk, :]
            du_ref[chunk, :] = (dy * sv * _gelu_grad(cu)).astype(du_ref.dtype)
            dsv = dy * u
            dsv16 = dsv.astype(BF16)
            db = db + dsv
            dvn = jnp.zeros_like(dsv)
            for gi in range(SGU_HEADS):
                dw[gi] = dw[gi] + lax.dot_general(dsv16, vng[gi], _NT, preferred_element_type=F32)
                dvn_g = lax.dot_general(wm[gi], dsv16, _TN, preferred_element_type=F32)
                dvn = jnp.where(grp == gi, dvn_g, dvn)
            dg = dg + jnp.sum(dvn * xhat, axis=0, keepdims=True)
            dxhat = dvn * gain
            dvv = r * (dxhat - xhat * _group_mean(dxhat * xhat, SGU_W))
            dv_ref[chunk, :] = (dvv * _gelu_grad(cv)).astype(dv_ref.dtype)
        dg_ref[...] += dg
        db_ref[...] += db
        for gi in range(SGU_HEADS):
            dw_ref[gi] += jnp.where(tril, dw[gi], 0.0)

    return pl.pallas_call(
        body,
        name=name,
        grid=(t // rows,),
        in_specs=[
            pl.BlockSpec((rows, SGU_W), lambda i: (i, dy0)),
            pl.BlockSpec((rows, SGU_W), lambda i: (i, u0)),
            pl.BlockSpec((rows, SGU_W), lambda i: (i, u0 + 1)),
            pl.BlockSpec((1, SGU_W), lambda i: (0, 0)),
            pl.BlockSpec((SGU_HEADS, CHUNK, CHUNK), lambda i: (0, 0, 0)),
            pl.BlockSpec((CHUNK, SGU_W), lambda i: (0, 0)),
        ],
        out_specs=[
            pl.BlockSpec((rows, SGU_W), lambda i: (i, 0)),
            pl.BlockSpec((rows, SGU_W), lambda i: (i, 0)),
            pl.BlockSpec((1, SGU_W), lambda i: (0, 0)),
            pl.BlockSpec((SGU_HEADS, CHUNK, CHUNK), lambda i: (0, 0, 0)),
            pl.BlockSpec((CHUNK, SGU_W), lambda i: (0, 0)),
        ],
        out_shape=[
            jax.ShapeDtypeStruct((t, SGU_W), BF16),
            jax.ShapeDtypeStruct((t, SGU_W), BF16),
            jax.ShapeDtypeStruct((1, SGU_W), F32),
            jax.ShapeDtypeStruct((SGU_HEADS, CHUNK, CHUNK), F32),
            jax.ShapeDtypeStruct((CHUNK, SGU_W), F32),
        ],
        compiler_params=_params("arbitrary"),
    )(dmix, proj, proj, g_v, w_s, b_exp)


def _ple_bwd(dh, gate, pp, order_after, *, name, br=512):
    t, d = dh.shape
    br = min(br, t)

    def body(dh_ref, g_ref, p_ref, order_ref, dpre_ref, dpp_ref):
        dhv, g = dh_ref[...], g_ref[...]
        dpre_ref[...] = (dhv * p_ref[...] * g * (1.0 - g)).astype(dpre_ref.dtype)
        dpp_ref[...] = (dhv * g).astype(dpp_ref.dtype)

    row = pl.BlockSpec((br, d), lambda i: (i, 0))
    return pl.pallas_call(
        body,
        name=name,
        grid=(t // br,),
        in_specs=[row] * 3 + [pl.BlockSpec(order_after.shape, lambda i: (0, 0))],
        out_specs=[row] * 2,
        out_shape=[jax.ShapeDtypeStruct((t, d), BF16)] * 2,
        compiler_params=_params("parallel"),
    )(dh, gate, pp, order_after)


def _loss_head(y, target, *, name, br=512):
    t, d = y.shape
    br = min(br, t)

    def body(y_ref, t_ref, dy_ref, loss_ref):
        err = y_ref[...] - t_ref[...]
        dy_ref[...] = err * (1.0 / d)

        @pl.when(pl.program_id(0) == 0)
        def _():
            loss_ref[...] = jnp.zeros_like(loss_ref)

        loss_ref[...] += 0.5 * jnp.sum(jnp.sum(err * err, axis=1, keepdims=True) * (1.0 / d), axis=0, keepdims=True)

    row = pl.BlockSpec((br, d), lambda i: (i, 0))
    return pl.pallas_call(
        body,
        name=name,
        grid=(t // br,),
        in_specs=[row, row],
        out_specs=[row, pl.BlockSpec((8, LANES), lambda i: (0, 0))],
        out_shape=[jax.ShapeDtypeStruct((t, d), F32), jax.ShapeDtypeStruct((8, LANES), F32)],
        compiler_params=_params("arbitrary"),
    )(y, target)


def _adamw_update(w, g, m, v):
    nm = ADAM_B1 * m + (1.0 - ADAM_B1) * g
    nv = ADAM_B2 * v + (1.0 - ADAM_B2) * (g * g)
    m_hat = nm / (1.0 - ADAM_B1 ** ADAM_STEP)
    v_hat = nv / (1.0 - ADAM_B2 ** ADAM_STEP)
    return -ADAM_LR * (m_hat / (jnp.sqrt(v_hat) + ADAM_EPS) + ADAM_WD * w), nm, nv


def _adamw(w, g, m, v, *, name, br=512):
    r, c = w.shape
    br = _row_block(r, br)

    def body(w_ref, g_ref, m_ref, v_ref, d_ref, nm_ref, nv_ref):
        d_ref[...], nm_ref[...], nv_ref[...] = _adamw_update(w_ref[...], g_ref[...], m_ref[...], v_ref[...])

    row = pl.BlockSpec((br, c), lambda i: (i, 0))
    return pl.pallas_call(
        body,
        name=name,
        grid=(r // br,),
        in_specs=[row] * 4,
        out_specs=[row] * 3,
        out_shape=[jax.ShapeDtypeStruct((r, c), F32)] * 3,
        compiler_params=_params("parallel"),
    )(w, g, m, v)


def _sum_slots(per_layer, *, name):
    counts = [len(arrays) for arrays in per_layer]
    flat = [a for arrays in per_layer for a in arrays]

    def body(*refs):
        ins, outs = refs[:len(flat)], refs[len(flat):]
        at = 0
        for o_ref, count in zip(outs, counts, strict=True):
            for li in range(count):
                acc = ins[at + li][0]
                for j in range(1, N_DEV):
                    acc = acc + ins[at + li][j]
                o_ref[li] = acc
            at += count

    return pl.pallas_call(
        body,
        name=name,
        out_shape=[jax.ShapeDtypeStruct((len(arrays), *arrays[0].shape[1:]), F32) for arrays in per_layer],
        compiler_params=pltpu.CompilerParams(vmem_limit_bytes=VMEM_LIMIT_BYTES),
    )(*flat)


_ADAMW_BLOCK_ELEMS = 128 * 1024


def _adamw_reduce(w, arrived, m, v, *, name):
    depth, r, c = w.shape
    br = _row_block(r, max(BF16_TILE_ROWS, _ADAMW_BLOCK_ELEMS // (-(-c // LANES) * LANES)))

    def body(w_ref, m_ref, v_ref, *rest):
        parts, (g_ref, d_ref, nm_ref, nv_ref) = rest[:depth], rest[depth:]
        for li in range(depth):

            @pl.when(pl.program_id(0) == li)
            def _(li=li):
                g = parts[li][0].astype(F32)
                for j in range(1, N_DEV):
                    g = g + parts[li][j].astype(F32)
                g_ref[...] = g
                d_ref[...], nm_ref[...], nv_ref[...] = _adamw_update(w_ref[...], g, m_ref[...], v_ref[...])

    cur = pl.BlockSpec((None, br, c), lambda l, i: (l, i, 0))
    slots = [pl.BlockSpec((N_DEV, br, c), lambda l, i, li=li: (0, jnp.where(l == li, i, 0), 0)) for li in range(depth)]
    return pl.pallas_call(
        body,
        name=name,
        grid=(depth, r // br),
        in_specs=[cur, cur, cur] + slots,
        out_specs=[cur] * 4,
        out_shape=[jax.ShapeDtypeStruct((depth, r, c), F32)] * 4,
        compiler_params=_params("arbitrary", "arbitrary"),
    )(w, m, v, *arrived)


def _my_place():
    return lax.axis_index("x"), lax.axis_index("y"), lax.axis_index("c")


def _flip(v, bit):
    return 1 - v if bit else v


def _slot_of(px, py, pc):
    return 4 * px + 2 * py + pc


_ANY = pl.BlockSpec(memory_space=pl.ANY)


_HBM = pl.BlockSpec(memory_space=pltpu.HBM)
_SEM = pl.BlockSpec(memory_space=pltpu.SEMAPHORE)
_DATAFLOW = pltpu.SideEffectType.DATAFLOW_SIDE_EFFECTING


_GATHER, _GATHER_COLUMNS, _SCATTER = "gather", "gather_columns", "scatter"


def _landing_shape(a, mode):
    if mode == _SCATTER:
        return a.shape
    if mode == _GATHER_COLUMNS:
        return (a.shape[0], N_DEV * a.shape[1])
    return (N_DEV, *a.shape)


_DIRECT, _NEAR, _RELAY = "direct", "near", "relay"
_OTHER_CHIPS = (2, 4, 6)
_SIBLING = 1


def _exchange_copies(src_refs, land_refs, send_sem, recv_sem, modes, hops=_DIRECT):
    mx, my, mc = _my_place()
    peer_of = lambda k: (_flip(mx, k & 4), _flip(my, k & 2), _flip(mc, k & 1))
    mine = _slot_of(mx, my, mc)

    def block(land, mode, slot):
        if mode == _GATHER_COLUMNS:
            n = land.shape[1] // N_DEV
            return land.at[:, pl.ds(pl.multiple_of(slot * n, LANES), n)]
        return land.at[slot]

    def remote_copy(src, dst, to):
        return pltpu.make_async_remote_copy(src_ref=src, dst_ref=dst, send_sem=send_sem, recv_sem=recv_sem,
                                            device_id=to, device_id_type=MESH)

    remote, local = [], []
    for src, land, mode in zip(src_refs, land_refs, modes, strict=True):
        if hops == _RELAY:
            assert mode != _SCATTER
            for k in _OTHER_CHIPS:
                came = block(land, mode, _slot_of(*peer_of(k)))
                remote.append(remote_copy(came, came, peer_of(_SIBLING)))
            continue
        dst = block(land, mode, mine)
        for k in ((_SIBLING,) + _OTHER_CHIPS if hops == _NEAR else range(1, N_DEV)):
            remote.append(remote_copy(src.at[_slot_of(*peer_of(k))] if mode == _SCATTER else src, dst, peer_of(k)))
        local.append(pltpu.make_async_copy(src.at[mine] if mode == _SCATTER else src, dst, recv_sem))
    return remote, local


def _wait_copies(remote, local):
    for cp in remote:
        cp.wait_send()
        cp.wait_recv()
    for cp in local:
        cp.wait()


def _exchange_start(groups, after, *, name, hops=_DIRECT):
    sizes = [len(srcs) for srcs, _ in groups]
    n, n_sems = sum(sizes), 2 * len(groups)
    srcs = [a for arrays, _ in groups for a in arrays]
    lands = [lax.empty(_landing_shape(a, mode), a.dtype)
             for arrays, modes in groups for a, mode in zip(arrays, modes, strict=True)]
    offsets = [sum(sizes[:g]) for g in range(len(groups))]

    def body(*refs):
        sems = refs[2 * n + 1:2 * n + 1 + n_sems]
        for g, (off, size, (_, modes)) in enumerate(zip(offsets, sizes, groups)):
            remote, local = _exchange_copies(refs[off:off + size], refs[n + off:n + off + size], sems[2 * g],
                                             sems[2 * g + 1], modes, hops)
            for cp in remote + local:
                cp.start()
        refs[-1][...] = jnp.zeros_like(refs[-1])

    thru = [pltpu.HBM(a.shape, a.dtype) for a in (*srcs, *lands)]
    out = pl.pallas_call(
        body,
        name=name,
        in_specs=[_HBM] * (2 * n) + [_ANY],
        out_specs=(*[_SEM] * n_sems, *[_HBM] * (2 * n), pl.BlockSpec(memory_space=pltpu.VMEM)),
        out_shape=(*[pltpu.SemaphoreType.DMA(())] * n_sems, *thru, jax.ShapeDtypeStruct((8, LANES), F32)),
        input_output_aliases={i: n_sems + i for i in range(2 * n)},
        compiler_params=pltpu.CompilerParams(has_side_effects=_DATAFLOW),
    )(*[pltpu.with_memory_space_constraint(a, pltpu.HBM) for a in (*srcs, *lands)], after)
    sems, arrays = out[:n_sems], out[n_sems:-1]
    started = [(sems[2 * g], sems[2 * g + 1], *arrays[off:off + size], *arrays[n + off:n + off + size])
               for g, (off, size) in enumerate(zip(offsets, sizes))]
    return started, out[-1]


def _exchange_relay(started, after, *, modes, regroup, name):
    send_sem, recv_sem, *thru = started
    n, n_sems = len(thru) // 2, 2 * len(regroup)

    def body(*refs):
        srcs, lands = refs[:n], refs[n:2 * n]
        _wait_copies(*_exchange_copies(srcs, lands, refs[2 * n], refs[2 * n + 1], modes, _NEAR))
        sems = refs[2 * n + 3:2 * n + 3 + n_sems]
        for g, members in enumerate(regroup):
            remote, _ = _exchange_copies([srcs[i] for i in members], [lands[i] for i in members], sems[2 * g],
                                         sems[2 * g + 1], [modes[i] for i in members], _RELAY)
            for cp in remote:
                cp.start()
        refs[-1][...] = jnp.zeros_like(refs[-1])

    out = pl.pallas_call(
        body,
        name=name,
        in_specs=[_HBM] * (2 * n) + [_SEM, _SEM, _ANY],
        out_specs=(*[_SEM] * n_sems, *[_HBM] * (2 * n), pl.BlockSpec(memory_space=pltpu.VMEM)),
        out_shape=(*[pltpu.SemaphoreType.DMA(())] * n_sems, *[pltpu.HBM(a.shape, a.dtype) for a in thru],
                   jax.ShapeDtypeStruct((8, LANES), F32)),
        input_output_aliases={i: n_sems + i for i in range(2 * n)},
        compiler_params=pltpu.CompilerParams(has_side_effects=_DATAFLOW),
    )(*thru, send_sem, recv_sem, after)
    sems, arrays = out[:n_sems], out[n_sems:-1]
    groups = [(sems[2 * g], sems[2 * g + 1], *[arrays[i] for i in members], *[arrays[n + i] for i in members])
              for g, members in enumerate(regroup)]
    return groups, out[-1]


def _exchange_wait(started, after, *, modes, name, hops=_DIRECT):
    send_sem, recv_sem, *thru = started
    n = len(thru) // 2

    def body(*refs):
        _wait_copies(*_exchange_copies(refs[:n], refs[n:2 * n], refs[2 * n], refs[2 * n + 1], modes, hops))

    out = pl.pallas_call(
        body,
        name=name,
        in_specs=[_HBM] * (2 * n) + [_SEM, _SEM, _ANY],
        out_specs=[_HBM] * (2 * n),
        out_shape=[pltpu.HBM(a.shape, a.dtype) for a in thru],
        input_output_aliases={i: i for i in range(2 * n)},
        compiler_params=pltpu.CompilerParams(has_side_effects=_DATAFLOW),
    )(*thru, send_sem, recv_sem, after)
    return out[n:]


def _gather_columns(g):
    return jnp.moveaxis(g, 0, 1).reshape(g.shape[1], -1)


def _split_rows(w):
    return w.reshape(N_DEV, w.shape[0] // N_DEV, w.shape[1])


_FIRST = ("w_in", "conv_w")
_REST = ("w_out", "w_ff1", "w_ff2", "w_ple_gate", "w_ple_proj")
_REST_GROUPS = (("w_out",), ("w_ff1",), ("w_ff2",), ("w_ple_gate", "w_ple_proj"))
_BIG = ("w_in",) + _REST
_GATHER_MODE = dict(w_in=_GATHER, conv_w=_GATHER, w_out=_GATHER, w_ff1=_GATHER_COLUMNS, w_ff2=_GATHER,
                    w_ple_gate=_GATHER, w_ple_proj=_GATHER_COLUMNS)
_RELAYOUT_AFTER_GATHER = ("w_in", "conv_w")
_SMALL = ("norm1_g", "q_norm_g", "k_norm_g", "sgu_norm_g", "sgu_w", "sgu_b", "norm2_g", "norm3_g")
_ORDER = ("norm1_g", "w_in", "conv_w", "q_norm_g", "k_norm_g", "sgu_norm_g", "sgu_w", "sgu_b", "w_out", "norm2_g",
          "w_ff1", "w_ff2", "norm3_g", "w_ple_gate", "w_ple_proj")


def _whole_matrices(names, landed):
    return {k: _gather_columns(g) if k in _RELAYOUT_AFTER_GATHER else g.reshape(-1, g.shape[-1])
            for k, g in zip(names, landed, strict=True)}


def _layer_forward(h0, p16, s, li, w_first, gathered):
    nm = lambda k: f"{k}_l{li}"
    t = h0.shape[0]
    w = dict(w_first)
    hn1 = _rms_fwd(h0, s["norm1_g"], name=nm("rms1"))
    proj = _matmul(hn1, w["w_in"], name=nm("proj"), bm=t, bn=256)
    y_a = _conv_fwd(proj, w["conv_w"], name=nm("conv"))
    qs, kn, v = _qk_prep(proj, s["gq"], s["gk"], name=nm("qkprep"))
    y_b, lt = _attn_fwd(qs, kn, v, name=nm("attn"))
    gathered["relay_rest"](y_b)
    y_c = _sgu_fwd(proj, s["sgu_norm_g"], s["sgu_w"], s["b_exp"], name=nm("sgu"))
    mix = jnp.concatenate([y_a, y_b, y_c], axis=1)
    w.update(gathered["fetch"](0, mix))
    h1 = _matmul(mix, w["w_out"], name=nm("out"), bm=t, bn=256, extras=(h0,), epilogue=lambda acc, r: (r + acc,))
    hn2 = _rms_fwd(h1, s["norm2_g"], name=nm("rms2"))
    w.update(gathered["fetch"](1, hn2))
    u, f = _matmul(hn2, w["w_ff1"], name=nm("ff1"), bm=t, bn=512, out_dtypes=(F32, BF16),
                   epilogue=lambda acc: (acc, jnp.square(jnp.maximum(acc, 0.0))))
    w.update(gathered["fetch"](2, f))
    h2 = _matmul(f, w["w_ff2"], name=nm("ff2"), bm=1024, bn=512, extras=(h1,), epilogue=lambda acc, r: (r + acc,))
    gathered["relay_next"](h2)
    hn3 = _rms_fwd(h2, s["norm3_g"], name=nm("rms3"))
    w.update(gathered["fetch"](3, hn3))
    w_next = gathered["fetch_next"](hn3)
    pp = _matmul(p16, w["w_ple_proj"], name=nm("pleproj"), bm=t, bn=512)

    def gate_epilogue(acc, pp_blk, h_blk):
        gate = jax.nn.sigmoid(acc)
        return h_blk + gate * pp_blk, gate

    h3, gate = _matmul(hn3, w["w_ple_gate"], name=nm("plegate"), bm=t, bn=256, out_dtypes=(F32, F32),
                       extras=(pp, h2), epilogue=gate_epilogue)
    saved = dict(h0=h0, hn1=hn1, proj=proj, qs=qs, kn=kn, v=v, lt=lt, mix=mix, h1=h1, hn2=hn2, u=u, f=f, h2=h2,
                 hn3=hn3, pp=pp, gate=gate, p16=p16)
    return h3, w, w_next, saved


def _layer_backward(dh3, a, w, s, li, order_after, start_rest):
    nm = lambda k: f"{k}_bwd_l{li}"
    t = dh3.shape[0]
    dpre, dpp = _ple_bwd(dh3, a["gate"], a["pp"], order_after, name=nm("ple"))
    g_gate = _weight_grad(a["hn3"], dpre, name=nm("dwgate"))
    g_proj = _weight_grad(a["p16"], dpp, name=nm("dwproj"), column_shards=True)
    dh2, dh2_16, g_n3 = _matmul_rms_bwd(dpre, w["w_ple_gate"], a["h2"], s["norm3_g"], dh3, name=nm("dh2"))
    du = _matmul(dh2_16, w["w_ff2"], name=nm("du"), tb=True, bm=t, bn=512, out_dtypes=(BF16,), extras=(a["u"],),
                 epilogue=lambda acc, u: (acc * (2.0 * jnp.maximum(u, 0.0)),))
    g_ff2 = _weight_grad(a["f"], dh2_16, name=nm("dwff2"))
    g_ff1 = _weight_grad(a["hn2"], du, name=nm("dwff1"), column_shards=True)
    dh1, dh1_16, g_n2 = _matmul_rms_bwd(du, w["w_ff1"], a["h1"], s["norm2_g"], dh2, name=nm("dh1"))
    dmix = _matmul(dh1_16, w["w_out"], name=nm("dmix"), tb=True, bm=t, bn=256)
    g_out = _weight_grad(a["mix"], dh1_16, name=nm("dwout"))
    started = start_rest(dict(w_out=_split_rows(g_out), w_ff1=g_ff1, w_ff2=_split_rows(g_ff2),
                              w_ple_gate=_split_rows(g_gate), w_ple_proj=g_proj), dmix)
    d_b, d_c, d_h, g_conv = _conv_bwd(dmix, a["proj"], w["conv_w"], name=nm("conv"))
    dqs, dkn, dv = _attn_bwd(dmix, a["qs"], a["kn"], a["v"], a["lt"], started, name=nm("attn"))
    d_q, d_k, d_v, g_q, g_k = _qk_prep_bwd(dqs, dkn, dv, a["proj"], s["gq"], s["gk"], name=nm("qkprep"))
    d_cu, d_cv, g_sn, g_sw, g_sb = _sgu_bwd(dmix, a["proj"], s["sgu_norm_g"], s["sgu_w"], s["b_exp"], name=nm("sgu"))
    dproj = jnp.concatenate([d_b, d_c, d_h, d_q, d_k, d_v, d_cu, d_cv], axis=1)
    g_in = _weight_grad(a["hn1"], dproj, name=nm("dwin"), column_shards=True)
    dh0, _, g_n1 = _matmul_rms_bwd(dproj, w["w_in"], a["h0"], s["norm1_g"], dh1, name=nm("dh0"))
    small = dict(norm1_g=g_n1, norm2_g=g_n2, norm3_g=g_n3, q_norm_g=g_q, k_norm_g=g_k, sgu_norm_g=g_sn, sgu_w=g_sw,
                 sgu_b=g_sb, conv_w=g_conv)
    return dh0, g_in, small


def _small_gradients(raw, depth):
    return dict(
        norm1_g=raw["norm1_g"].reshape(depth, -1), norm2_g=raw["norm2_g"].reshape(depth, -1),
        norm3_g=raw["norm3_g"].reshape(depth, -1),
        q_norm_g=raw["q_norm_g"].reshape(depth, -1, HEAD_DIM).sum(1),
        k_norm_g=raw["k_norm_g"].reshape(depth, -1, HEAD_DIM).sum(1),
        sgu_norm_g=raw["sgu_norm_g"].reshape(depth, -1), sgu_w=raw["sgu_w"],
        sgu_b=jnp.swapaxes(raw["sgu_b"].reshape(depth, CHUNK, SGU_HEADS, HEAD_DIM).sum(-1), 1, 2),
        conv_w=raw["conv_w"][:, :CONV_TAPS],
    )


def kernel(x, p, norm1_g, w_in, conv_w, q_norm_g, k_norm_g, sgu_norm_g, sgu_w, sgu_b, w_out, norm2_g, w_ff1, w_ff2, norm3_g, w_ple_gate, w_ple_proj, loss_target, m_norm1_g, m_w_in, m_conv_w, m_q_norm_g, m_k_norm_g, m_sgu_norm_g, m_sgu_w, m_sgu_b, m_w_out, m_norm2_g, m_w_ff1, m_w_ff2, m_norm3_g, m_w_ple_gate, m_w_ple_proj, v_norm1_g, v_w_in, v_conv_w, v_q_norm_g, v_k_norm_g, v_sgu_norm_g, v_sgu_w, v_sgu_b, v_w_out, v_norm2_g, v_w_ff1, v_w_ff2, v_norm3_g, v_w_ple_gate, v_w_ple_proj):
    weights = dict(norm1_g=norm1_g, w_in=w_in, conv_w=conv_w, q_norm_g=q_norm_g, k_norm_g=k_norm_g,
                   sgu_norm_g=sgu_norm_g, sgu_w=sgu_w, sgu_b=sgu_b, w_out=w_out, norm2_g=norm2_g, w_ff1=w_ff1,
                   w_ff2=w_ff2, norm3_g=norm3_g, w_ple_gate=w_ple_gate, w_ple_proj=w_ple_proj)
    mom = dict(norm1_g=m_norm1_g, w_in=m_w_in, conv_w=m_conv_w, q_norm_g=m_q_norm_g, k_norm_g=m_k_norm_g,
               sgu_norm_g=m_sgu_norm_g, sgu_w=m_sgu_w, sgu_b=m_sgu_b, w_out=m_w_out, norm2_g=m_norm2_g, w_ff1=m_w_ff1,
               w_ff2=m_w_ff2, norm3_g=m_norm3_g, w_ple_gate=m_w_ple_gate, w_ple_proj=m_w_ple_proj)
    var = dict(norm1_g=v_norm1_g, w_in=v_w_in, conv_w=v_conv_w, q_norm_g=v_q_norm_g, k_norm_g=v_k_norm_g,
               sgu_norm_g=v_sgu_norm_g, sgu_w=v_sgu_w, sgu_b=v_sgu_b, w_out=v_w_out, norm2_g=v_norm2_g, w_ff1=v_w_ff1,
               w_ff2=v_w_ff2, norm3_g=v_norm3_g, w_ple_gate=v_w_ple_gate, w_ple_proj=v_w_ple_proj)
    depth = norm1_g.shape[0]
    mx, my, mc = _my_place()
    me = _slot_of(mx, my, mc)

    gathers = []
    modes_of = lambda names: tuple(_GATHER_MODE[k] for k in names)
    token = x[0, :8, :LANES]
    for li in range(depth):
        groups = [([weights[k][li] if k == "conv_w" else weights[k][li].astype(BF16) for k in names], modes_of(names))
                  for names in (_FIRST, _REST)]
        started, token = _exchange_start(groups, token, name=f"gather_start_l{li}", hops=_NEAR)
        gathers.append(started)

    small = []
    for li in range(depth):
        small.append(dict(
            norm1_g=norm1_g[li][None], norm2_g=norm2_g[li][None], norm3_g=norm3_g[li][None],
            gq=jnp.tile(q_norm_g[li], _QK_BLOCK // HEAD_DIM)[None], gk=jnp.tile(k_norm_g[li], _QK_BLOCK // HEAD_DIM)[None],
            sgu_norm_g=sgu_norm_g[li][None], sgu_w=sgu_w[li], b_exp=jnp.repeat(sgu_b[li].T, HEAD_DIM, axis=1),
        ))
    small[0]["norm1_g"] = small[0]["norm1_g"] + token[0, 0]

    h = x[0]
    saved, full = [], []
    relayed_first, relayed_rest = [None] * depth, [None] * depth
    rest_members = [[_REST.index(k) for k in names] for names in _REST_GROUPS]

    def relay_first(li, after):
        if li < depth:
            (relayed_first[li],), _ = _exchange_relay(gathers[li][0], after, modes=modes_of(_FIRST),
                                                      regroup=[list(range(len(_FIRST)))], name=f"gather_first_relay_l{li}")

    def fetch_first(li, after):
        if li == depth:
            return None
        landed = _exchange_wait(relayed_first[li], after, modes=modes_of(_FIRST), hops=_RELAY,
                                name=f"gather_first_wait_l{li}")
        return _whole_matrices(_FIRST, landed)

    relay_first(0, token)
    w_first = fetch_first(0, small[0]["norm1_g"])
    for li in range(depth):

        def relay_rest(after, li=li):
            relayed_rest[li], _ = _exchange_relay(gathers[li][1], after, modes=modes_of(_REST), regroup=rest_members,
                                                  name=f"gather_rest_relay_l{li}")

        def fetch(g, after, li=li):
            landed = _exchange_wait(relayed_rest[li][g], after, modes=modes_of(_REST_GROUPS[g]), hops=_RELAY,
                                    name=f"gather_{_REST_GROUPS[g][0]}_wait_l{li}")
            return _whole_matrices(_REST_GROUPS[g], landed)

        gathered = dict(relay_rest=relay_rest, fetch=fetch, relay_next=functools.partial(relay_first, li + 1),
                        fetch_next=functools.partial(fetch_first, li + 1))
        h, w, w_first, acts = _layer_forward(h, p[li, 0].astype(BF16), small[li], li, w_first, gathered)
        full.append(w)
        saved.append(acts)
    dh, loss_tile = _loss_head(h, loss_target[0], name="loss_head")
    loss = lax.psum(loss_tile[0, 0], ("x", "y", "c"))

    small_names = _SMALL + ("conv_w",)
    scatter_first, scatter_rest = [None] * depth, [None] * depth
    first_modes, rest_modes = (_SCATTER,) + (_GATHER,) * len(small_names), (_SCATTER,) * len(_REST)
    token = loss_tile
    for li in reversed(range(depth)):

        def start_rest(parts, after, li=li):
            (scatter_rest[li],), started = _exchange_start([([parts[k] for k in _REST], rest_modes)], after,
                                                           name=f"scatter_rest_start_l{li}")
            return started

        dh, g_in, small_grads = _layer_backward(dh, saved[li], full[li], small[li], li, token, start_rest)
        (scatter_first[li],), token = _exchange_start(
            [([g_in] + [small_grads[k] for k in small_names], first_modes)], dh, name=f"scatter_first_start_l{li}")
    grad_x = dh[None]

    grads, delta, new_m, new_v = {}, {}, {}, {}
    arrived = {k: [None] * depth for k in _BIG}
    for li in reversed(range(depth)):
        landed = _exchange_wait(scatter_rest[li], token, modes=rest_modes, name=f"scatter_rest_wait_l{li}")
        for k, g in zip(_REST, landed, strict=True):
            arrived[k][li] = g
    for k in _REST:
        grads[k], delta[k], new_m[k], new_v[k] = _adamw_reduce(weights[k], arrived[k], mom[k], var[k], name=f"adamw_{k}")
    small_parts = {k: [None] * depth for k in small_names}
    updated = jnp.stack([delta[k][0, 0, :1] for k in _REST])
    for li in reversed(range(depth)):
        arrived["w_in"][li], *parts = _exchange_wait(scatter_first[li], updated, modes=first_modes,
                                                     name=f"scatter_first_wait_l{li}")
        for k, part in zip(small_names, parts, strict=True):
            small_parts[k][li] = part
    grads["w_in"], delta["w_in"], new_m["w_in"], new_v["w_in"] = _adamw_reduce(
        w_in, arrived["w_in"], mom["w_in"], var["w_in"], name="adamw_w_in")
    sums = _sum_slots([small_parts[k] for k in small_names], name="sum_small_grads")
    grads.update(_small_gradients(dict(zip(small_names, sums)), depth))
    n_conv = conv_w.shape[2]
    grads["conv_w"] = lax.dynamic_slice_in_dim(grads["conv_w"], me * n_conv, n_conv, axis=2)
    for k in small_names:
        as_rows = lambda a: a.reshape(-1, a.shape[-1])
        outs = _adamw(as_rows(weights[k]), as_rows(grads[k]), as_rows(mom[k]), as_rows(var[k]), name=f"adamw_{k}")
        delta[k], new_m[k], new_v[k] = (o.reshape(weights[k].shape) for o in outs)

    return (loss, grad_x, *[grads[k] for k in _ORDER], *[delta[k] for k in _ORDER],
            *[new_m[k] for k in _ORDER], *[new_v[k] for k in _ORDER])
```

```python
import functools
import math

import jax
import jax.numpy as jnp
from jax import lax
from jax.experimental import pallas as pl
from jax.experimental.pallas import tpu as pltpu

F32 = jnp.float32
BF16 = jnp.bfloat16

N_DEV = 8
HEAD_DIM = 64
CONV_W = 256
ATTN_W = 512
SGU_W = 256
SGU_HEADS = 4
CHUNK = 128
CONV_TAPS = 3
EPS = 1e-6
QK_SCALE = HEAD_DIM ** -0.5

ADAM_LR = 0.001
ADAM_B1 = 0.9
ADAM_B2 = 0.999
ADAM_EPS = 1e-08
ADAM_WD = 0.01
ADAM_STEP = 10

LANES = 128
BF16_TILE_ROWS = 16
VMEM_LIMIT_BYTES = 56 * 1024 * 1024
MESH = pl.DeviceIdType.MESH


def _params(*sem):
    return pltpu.CompilerParams(dimension_semantics=sem, vmem_limit_bytes=VMEM_LIMIT_BYTES)


def _row_block(rows, cap):
    if rows <= cap:
        return rows
    return max(b for b in range(BF16_TILE_ROWS, cap + 1, BF16_TILE_ROWS) if rows % b == 0)


def _matmul(a, b, *, name, tb=False, bm=512, bn=512, out_dtypes=(F32,), epilogue=None, extras=()):
    m, k = a.shape
    n = b.shape[0] if tb else b.shape[1]
    assert k == (b.shape[1] if tb else b.shape[0])
    bm, bn = min(bm, m), min(bn, n)
    assert m % bm == 0 and n % bn == 0
    a_spec = pl.BlockSpec((bm, k), lambda i, j: (i, 0))
    b_spec = pl.BlockSpec((bn, k), lambda i, j: (j, 0)) if tb else pl.BlockSpec((k, bn), lambda i, j: (0, j))
    dims = (((1,), (1 if tb else 0,)), ((), ()))
    n_ex = len(extras)
    for e in extras:
        assert e.shape == (m, n), (e.shape, m, n)

    def body(a_ref, b_ref, *rest):
        outs = rest[n_ex:]
        acc = lax.dot_general(a_ref[...], b_ref[...], dims, preferred_element_type=F32)
        res = (acc,) if epilogue is None else epilogue(acc, *[e[...] for e in rest[:n_ex]])
        for o_ref, r in zip(outs, res, strict=True):
            o_ref[...] = r.astype(o_ref.dtype)

    tile = pl.BlockSpec((bm, bn), lambda i, j: (i, j))
    out = pl.pallas_call(
        body,
        name=name,
        grid=(m // bm, n // bn),
        in_specs=[a_spec, b_spec] + [tile] * n_ex,
        out_specs=[tile] * len(out_dtypes),
        out_shape=[jax.ShapeDtypeStruct((m, n), d) for d in out_dtypes],
        compiler_params=_params("parallel", "parallel"),
    )(a, b, *extras)
    return out[0] if len(out_dtypes) == 1 else out


_WEIGHT_GRAD_ACC_ELEMS = 1024 * 1024


def _weight_grad(x, dys, *, name, column_shards=False):
    t, m = x.shape
    n = sum(dy.shape[1] for dy in dys)
    bm = m if m <= 2 * LANES else min(m // 2, max(LANES, _WEIGHT_GRAD_ACC_ELEMS // n // LANES * LANES))
    assert m % bm == 0
    ns = n // N_DEV

    def body(x_ref, *rest):
        o_ref = rest[-1]
        xb = x_ref[...]
        acc = jnp.concatenate([lax.dot_general(xb, dy_ref[...], _TN, preferred_element_type=F32) for dy_ref in rest[:-1]],
                              axis=1)
        if column_shards:
            for s in range(N_DEV):
                o_ref[s] = acc[:, s * ns:(s + 1) * ns].astype(o_ref.dtype)
        else:
            o_ref[...] = acc.astype(o_ref.dtype)

    if column_shards:
        out_spec, out_dims = pl.BlockSpec((N_DEV, bm, ns), lambda i: (0, i, 0)), (N_DEV, m, ns)
    else:
        out_spec, out_dims = pl.BlockSpec((bm, n), lambda i: (i, 0)), (m, n)
    return pl.pallas_call(
        body,
        name=name,
        grid=(m // bm,),
        in_specs=[pl.BlockSpec((t, bm), lambda i: (0, i))] + [pl.BlockSpec(dy.shape, lambda i: (0, 0)) for dy in dys],
        out_specs=out_spec,
        out_shape=jax.ShapeDtypeStruct(out_dims, BF16),
        compiler_params=_params("parallel"),
    )(x, *dys)


def _rms_fwd(h, g, *, name, br=512):
    t, d = h.shape
    br = min(br, t)

    def body(h_ref, g_ref, o_ref):
        x = h_ref[...]
        r = lax.rsqrt(jnp.mean(x * x, axis=-1, keepdims=True) + EPS)
        o_ref[...] = (x * r * g_ref[...]).astype(o_ref.dtype)

    return pl.pallas_call(
        body,
        name=name,
        grid=(t // br,),
        in_specs=[pl.BlockSpec((br, d), lambda i: (i, 0)), pl.BlockSpec((1, d), lambda i: (0, 0))],
        out_specs=pl.BlockSpec((br, d), lambda i: (i, 0)),
        out_shape=jax.ShapeDtypeStruct((t, d), BF16),
        compiler_params=_params("parallel"),
    )(h, g)


def _matmul_rms_bwd(dzs, w, h, g, dres, *, name):
    t, d = h.shape
    widths = [dz.shape[1] for dz in dzs]
    k = sum(widths)
    br = min(t, 512 if k <= d else 256)
    n_dz = len(dzs)

    def body(*refs):
        w_ref, h_ref, g_ref, dres_ref, dh_ref, dh16_ref, dg_ref = refs[n_dz:]
        x = h_ref[...]
        dyv, at = None, 0
        for dz_ref, width in zip(refs[:n_dz], widths):
            part = lax.dot_general(dz_ref[...], w_ref[:, at:at + width], _NT, preferred_element_type=F32)
            dyv = part if dyv is None else dyv + part
            at += width
        r = lax.rsqrt(jnp.mean(x * x, axis=-1, keepdims=True) + EPS)
        xhat = x * r
        dxhat = dyv * g_ref[...]
        dh = dres_ref[...] + r * (dxhat - xhat * jnp.mean(dxhat * xhat, axis=-1, keepdims=True))
        dh_ref[...] = dh
        dh16_ref[...] = dh.astype(dh16_ref.dtype)

        @pl.when(pl.program_id(0) == 0)
        def _():
            dg_ref[...] = jnp.zeros_like(dg_ref)

        dg_ref[...] += jnp.sum(dyv * xhat, axis=0, keepdims=True)

    row = pl.BlockSpec((br, d), lambda i: (i, 0))
    vec = pl.BlockSpec((1, d), lambda i: (0, 0))
    return pl.pallas_call(
        body,
        name=name,
        grid=(t // br,),
        in_specs=[pl.BlockSpec((br, width), lambda i: (i, 0)) for width in widths]
        + [pl.BlockSpec((d, k), lambda i: (0, 0)), row, vec, row],
        out_specs=[row, row, vec],
        out_shape=[jax.ShapeDtypeStruct((t, d), F32), jax.ShapeDtypeStruct((t, d), BF16),
                   jax.ShapeDtypeStruct((1, d), F32)],
        compiler_params=_params("arbitrary"),
    )(*dzs, w, h, g, dres)


def _group_mean(x, width):
    grp = lax.broadcasted_iota(jnp.int32, x.shape, 1) // HEAD_DIM
    out = jnp.zeros_like(x)
    for gi in range(width // HEAD_DIM):
        m = grp == gi
        s = jnp.sum(jnp.where(m, x, 0.0), axis=1, keepdims=True)
        out = jnp.where(m, s, out)
    return out * (1.0 / HEAD_DIM)


def _gelu(x):
    return 0.5 * x * (1.0 + lax.erf(x * (2.0 ** -0.5)))


def _gelu_grad(x):
    cdf = 0.5 * (1.0 + lax.erf(x * (2.0 ** -0.5)))
    pdf = jnp.exp(-0.5 * x * x) * (1.0 / math.sqrt(2.0 * math.pi))
    return cdf + x * pdf


def _shift_down(z, s, row):
    return jnp.where(row >= s, pltpu.roll(z, s, 0), 0.0)


def _shift_up(z, s, row, t):
    return jnp.where(row < t - s, pltpu.roll(z, t - s, 0), 0.0)


def _conv_fwd(proj, conv_w, *, name):
    t = proj.shape[0]
    nb = CONV_W // LANES

    def body(b_ref, c_ref, h_ref, w_ref, o_ref):
        row = lax.broadcasted_iota(jnp.int32, (t, LANES), 0)
        z = c_ref[...] * h_ref[...]
        w = w_ref[...]
        conv = w[2:3, :] * z + w[1:2, :] * _shift_down(z, 1, row) + w[0:1, :] * _shift_down(z, 2, row)
        o_ref[...] = (b_ref[...] * conv).astype(o_ref.dtype)

    return pl.pallas_call(
        body,
        name=name,
        grid=(nb,),
        in_specs=[
            pl.BlockSpec((t, LANES), lambda j: (0, j)),
            pl.BlockSpec((t, LANES), lambda j: (0, nb + j)),
            pl.BlockSpec((t, LANES), lambda j: (0, 2 * nb + j)),
            pl.BlockSpec((CONV_TAPS, LANES), lambda j: (0, j)),
        ],
        out_specs=pl.BlockSpec((t, LANES), lambda j: (0, j)),
        out_shape=jax.ShapeDtypeStruct((t, CONV_W), BF16),
        compiler_params=_params("parallel"),
    )(proj, proj, proj, conv_w)


def _conv_bwd(dmix, proj, conv_w, *, name):
    t = proj.shape[0]

    def body(dy_ref, b_ref, c_ref, h_ref, w_ref, o_ref, dw_ref):
        row = lax.broadcasted_iota(jnp.int32, (t, CONV_W), 0)
        ac, ah = c_ref[...], h_ref[...]
        z = ac * ah
        w = w_ref[...]
        z1 = _shift_down(z, 1, row)
        z2 = _shift_down(z, 2, row)
        conv = w[2:3, :] * z + w[1:2, :] * z1 + w[0:1, :] * z2
        dy = dy_ref[...]
        o_ref[:, 0:CONV_W] = (dy * conv).astype(o_ref.dtype)
        dconv = dy * b_ref[...]
        dz = w[2:3, :] * dconv + w[1:2, :] * _shift_up(dconv, 1, row, t) + w[0:1, :] * _shift_up(dconv, 2, row, t)
        o_ref[:, CONV_W:2 * CONV_W] = (dz * ah).astype(o_ref.dtype)
        o_ref[:, 2 * CONV_W:3 * CONV_W] = (dz * ac).astype(o_ref.dtype)
        dw_ref[...] = jnp.zeros_like(dw_ref)
        dw_ref[0:1, :] = jnp.sum(dconv * z2, axis=0, keepdims=True)
        dw_ref[1:2, :] = jnp.sum(dconv * z1, axis=0, keepdims=True)
        dw_ref[2:3, :] = jnp.sum(dconv * z, axis=0, keepdims=True)

    col = lambda j: pl.BlockSpec((t, CONV_W), lambda i: (0, j))
    return pl.pallas_call(
        body,
        name=name,
        grid=(1,),
        in_specs=[col(0), col(0), col(1), col(2), pl.BlockSpec((CONV_TAPS, CONV_W), lambda i: (0, 0))],
        out_specs=[pl.BlockSpec((t, 3 * CONV_W), lambda i: (0, 0)), pl.BlockSpec((8, CONV_W), lambda i: (0, 0))],
        out_shape=[jax.ShapeDtypeStruct((t, 3 * CONV_W), BF16), jax.ShapeDtypeStruct((8, CONV_W), F32)],
        compiler_params=_params("arbitrary"),
    )(dmix, proj, proj, proj, conv_w)


_QK_BLOCK = 256


def _qk_prep(proj, gq, gk, *, name, br=512):
    t = proj.shape[0]
    br = min(br, t)
    nb = ATTN_W // _QK_BLOCK
    q0 = (3 * CONV_W) // _QK_BLOCK

    def body(q_ref, k_ref, v_ref, gq_ref, gk_ref, qo_ref, ko_ref, vo_ref):
        q = q_ref[...]
        k = k_ref[...]
        rq = lax.rsqrt(_group_mean(q * q, _QK_BLOCK) + EPS)
        rk = lax.rsqrt(_group_mean(k * k, _QK_BLOCK) + EPS)
        qo_ref[...] = ((q * rq * gq_ref[...]).astype(BF16) * QK_SCALE).astype(qo_ref.dtype)
        ko_ref[...] = (k * rk * gk_ref[...]).astype(ko_ref.dtype)
        vo_ref[...] = v_ref[...].astype(vo_ref.dtype)

    col = lambda off: pl.BlockSpec((br, _QK_BLOCK), lambda i, j: (i, off + j))
    vec = pl.BlockSpec((1, _QK_BLOCK), lambda i, j: (0, 0))
    return pl.pallas_call(
        body,
        name=name,
        grid=(t // br, nb),
        in_specs=[col(q0), col(q0 + nb), col(q0 + 2 * nb), vec, vec],
        out_specs=[col(0)] * 3,
        out_shape=[jax.ShapeDtypeStruct((t, ATTN_W), BF16)] * 3,
        compiler_params=_params("parallel", "parallel"),
    )(proj, proj, proj, gq, gk)


def _qk_prep_bwd(dqs, dkn, dv, proj, gq, gk, *, name, br=256):
    t = proj.shape[0]
    br = min(br, t)
    nb = ATTN_W // _QK_BLOCK
    q0 = (3 * CONV_W) // _QK_BLOCK

    def norm_bwd(dy, x, g):
        r = lax.rsqrt(_group_mean(x * x, ATTN_W) + EPS)
        xhat = x * r
        dxhat = dy * g
        dx = r * (dxhat - xhat * _group_mean(dxhat * xhat, ATTN_W))
        return dx, jnp.sum(dy * xhat, axis=0, keepdims=True)

    def body(dq_ref, dk_ref, dv_ref, *rest):
        x_refs, (gq_ref, gk_ref, o_ref, dgq_ref, dgk_ref) = rest[:2 * nb], rest[2 * nb:]
        whole = lambda refs: jnp.concatenate([r[...] for r in refs], axis=1)
        dq, dgq = norm_bwd(dq_ref[...] * QK_SCALE, whole(x_refs[:nb]), whole([gq_ref] * nb))
        dk, dgk = norm_bwd(dk_ref[...], whole(x_refs[nb:]), whole([gk_ref] * nb))
        o_ref[:, 0:ATTN_W] = dq.astype(o_ref.dtype)
        o_ref[:, ATTN_W:2 * ATTN_W] = dk.astype(o_ref.dtype)
        o_ref[:, 2 * ATTN_W:3 * ATTN_W] = dv_ref[...].astype(o_ref.dtype)

        @pl.when(pl.program_id(0) == 0)
        def _():
            dgq_ref[...] = jnp.zeros_like(dgq_ref)
            dgk_ref[...] = jnp.zeros_like(dgk_ref)

        dgq_ref[...] += dgq
        dgk_ref[...] += dgk

    rows = pl.BlockSpec((br, ATTN_W), lambda i: (i, 0))
    col = lambda j: pl.BlockSpec((br, _QK_BLOCK), lambda i: (i, j))
    gain = pl.BlockSpec((1, _QK_BLOCK), lambda i: (0, 0))
    total = pl.BlockSpec((1, ATTN_W), lambda i: (0, 0))
    return pl.pallas_call(
        body,
        name=name,
        grid=(t // br,),
        in_specs=[rows, rows, rows] + [col(q0 + j) for j in range(2 * nb)] + [gain, gain],
        out_specs=[pl.BlockSpec((br, 3 * ATTN_W), lambda i: (i, 0)), total, total],
        out_shape=[jax.ShapeDtypeStruct((t, 3 * ATTN_W), BF16)] + [jax.ShapeDtypeStruct((1, ATTN_W), F32)] * 2,
        compiler_params=_params("arbitrary"),
    )(dqs, dkn, dv, *[proj] * (2 * nb), gq, gk)


def _key_order_matrix(tb, relation):
    jj = lax.broadcasted_iota(jnp.int32, (tb, tb), 0)
    ss = lax.broadcasted_iota(jnp.int32, (tb, tb), 1)
    return relation(jj, ss).astype(BF16)


def _log_sigmoids(z):
    lb = jnp.minimum(z, 0.0) - jnp.log(1.0 + jnp.exp(-jnp.abs(z)))
    return lb, lb - z


def _below_diagonal(tb):
    return lax.broadcasted_iota(jnp.int32, (tb, tb), 1) < lax.broadcasted_iota(jnp.int32, (tb, tb), 0)


_NT = (((1,), (1,)), ((), ()))
_TN = (((0,), (0,)), ((), ()))
_ATTN_BLOCK = 256
_ATTN_FWD_UNROLL = 2
_ATTN_BWD_UNROLL = 3


def _attn_fwd(qs, kn, v, *, name, tb=_ATTN_BLOCK, unroll=_ATTN_FWD_UNROLL):
    t = qs.shape[0]
    tb = min(tb, t)
    assert t % tb == 0
    n_pairs = ATTN_W // LANES

    def body(q_ref, k_ref, v_ref, o_ref, lt_ref, acc_ref, carry_ref):
        qb = pl.program_id(1)
        half = lax.broadcasted_iota(jnp.int32, (1, LANES), 1) // HEAD_DIM
        later = _key_order_matrix(tb, lambda j, s: j > s)
        acc_ref[...] = jnp.zeros_like(acc_ref)
        carry_ref[...] = jnp.zeros_like(carry_ref)
        q = q_ref[...]
        qh = [jnp.where(half == h, q, jnp.zeros_like(q)) for h in range(2)]

        def tiles(kbs, diagonal):
            blk = []
            for kb in kbs:
                start = pl.multiple_of(kb * tb, tb)
                blk.append((k_ref[pl.ds(start, tb), :], v_ref[pl.ds(start, tb), :]))
            chains = [(h, j) for j in range(len(kbs)) for h in range(2)]
            z = [lax.dot_general(qh[h], blk[j][0], _NT, preferred_element_type=F32) for h, j in chains]
            causal = _below_diagonal(tb) if diagonal else None
            lb, lr = [], []
            for zi in z:
                b, r = _log_sigmoids(zi)
                lb.append(b)
                lr.append(jnp.where(causal, r, 0.0) if diagonal else r)
            suffix = [jnp.dot(r.astype(BF16), later, preferred_element_type=F32) for r in lr]
            carry = [carry_ref[0], carry_ref[1]]
            w = []
            for i, (h, j) in enumerate(chains):
                wi = jnp.exp(lb[i] + (suffix[i] + carry[h][:, 0:1]))
                w.append((jnp.where(causal, wi, 0.0) if diagonal else wi).astype(BF16))
                carry[h] = carry[h] + jnp.sum(lr[i], axis=1, keepdims=True)
            for i, (h, j) in enumerate(chains):
                vh = jnp.where(half == h, blk[j][1], jnp.zeros_like(blk[j][1]))
                acc_ref[h] += jnp.dot(w[i], vh, preferred_element_type=F32)
            carry_ref[0] = carry[0]
            carry_ref[1] = carry[1]

        tiles([qb], True)

        def step(i, _):
            kb = qb - 1 - unroll * i
            tiles([kb - u for u in range(unroll)], False)
            return 0

        lax.fori_loop(0, qb // unroll, step, 0)
        for left in range(1, unroll):

            @pl.when(qb % unroll == left)
            def _(left=left):
                tiles([left - 1 - u for u in range(left)], False)

        o_ref[...] = (acc_ref[0] + acc_ref[1]).astype(o_ref.dtype)
        lt_ref[...] = jnp.where(half == 0, carry_ref[0], carry_ref[1])

    return pl.pallas_call(
        body,
        name=name,
        grid=(n_pairs, t // tb),
        in_specs=[
            pl.BlockSpec((tb, LANES), lambda p, i: (i, p)),
            pl.BlockSpec((t, LANES), lambda p, i: (0, p)),
            pl.BlockSpec((t, LANES), lambda p, i: (0, p)),
        ],
        out_specs=[pl.BlockSpec((tb, LANES), lambda p, i: (i, p))] * 2,
        out_shape=[jax.ShapeDtypeStruct((t, ATTN_W), BF16), jax.ShapeDtypeStruct((t, ATTN_W), F32)],
        scratch_shapes=[pltpu.VMEM((2, tb, LANES), F32), pltpu.VMEM((2, tb, LANES), F32)],
        compiler_params=_params("parallel", "parallel"),
    )(qs, kn, v)


def _attn_bwd(dmix, qs, kn, v, lt, order_after, *, name, tb=_ATTN_BLOCK, unroll=_ATTN_BWD_UNROLL):
    t = qs.shape[0]
    tb = min(tb, t)
    assert t % tb == 0
    n_pairs = ATTN_W // LANES
    dy0 = CONV_W // LANES

    def body(do_ref, q_ref, k_ref, v_ref, lt_ref, order_ref, dq_ref, dk_ref, dv_ref, dqacc_ref, cc_ref, cg_ref):
        qb = pl.program_id(1)
        half = lax.broadcasted_iota(jnp.int32, (1, LANES), 1) // HEAD_DIM
        lane = lax.broadcasted_iota(jnp.int32, (tb, LANES), 1)
        later = _key_order_matrix(tb, lambda j, s: j > s)
        before = _key_order_matrix(tb, lambda j, s: j < s)
        q = q_ref[...]
        do = do_ref[...].astype(BF16)
        lt = lt_ref[...]
        qh = [jnp.where(half == h, q, jnp.zeros_like(q)) for h in range(2)]
        doh = [jnp.where(half == h, do, jnp.zeros_like(do)) for h in range(2)]
        lth = [jnp.sum(jnp.where(lane == h * HEAD_DIM, lt, 0.0), axis=1, keepdims=True) for h in range(2)]

        @pl.when(qb == 0)
        def _():
            dk_ref[...] = jnp.zeros_like(dk_ref)
            dv_ref[...] = jnp.zeros_like(dv_ref)

        dqacc_ref[...] = jnp.zeros_like(dqacc_ref)
        cc_ref[...] = jnp.zeros_like(cc_ref)
        cg_ref[...] = jnp.zeros_like(cg_ref)

        def tiles(kbs, diagonal):
            starts = [pl.multiple_of(kb * tb, tb) for kb in kbs]
            blk = [(k_ref[pl.ds(s, tb), :], v_ref[pl.ds(s, tb), :]) for s in starts]
            chains = [(h, j) for j in range(len(kbs)) for h in range(2)]
            z = [lax.dot_general(qh[h], blk[j][0], _NT, preferred_element_type=F32) for h, j in chains]
            da = [lax.dot_general(doh[h], jnp.where(half == h, blk[j][1], jnp.zeros_like(blk[j][1])), _NT,
                                  preferred_element_type=F32) for h, j in chains]
            causal = _below_diagonal(tb) if diagonal else None
            lb, lr = [], []
            for zi in z:
                b, r = _log_sigmoids(zi)
                lb.append(b)
                lr.append(jnp.where(causal, r, 0.0) if diagonal else r)
            suffix = [jnp.dot(r.astype(BF16), later, preferred_element_type=F32) for r in lr]
            cc = [cc_ref[0], cc_ref[1]]
            cg = [cg_ref[0], cg_ref[1]]
            a16, g = [], []
            for i, (h, j) in enumerate(chains):
                cc[h] = cc[h] + jnp.sum(lr[i], axis=1, keepdims=True)
                a = jnp.exp(lb[i] + suffix[i] + (lth[h] - cc[h][:, 0:1]))
                if diagonal:
                    a = jnp.where(causal, a, 0.0)
                a16.append(a.astype(BF16))
                g.append(da[i] * a)
            g_before = [jnp.dot(gi.astype(BF16), before, preferred_element_type=F32) for gi in g]
            dz = []
            for i, (h, j) in enumerate(chains):
                dzi = g[i] - jnp.exp(lb[i]) * (g[i] + (g_before[i] + cg[h][:, 0:1]))
                dz.append((jnp.where(causal, dzi, 0.0) if diagonal else dzi).astype(BF16))
                cg[h] = cg[h] + jnp.sum(g[i], axis=1, keepdims=True)
            for i, (h, j) in enumerate(chains):
                kh = jnp.where(half == h, blk[j][0], jnp.zeros_like(blk[j][0]))
                dqacc_ref[h] += jnp.dot(dz[i], kh, preferred_element_type=F32)
                dk_ref[pl.ds(starts[j], tb), :] += lax.dot_general(dz[i], qh[h], _TN, preferred_element_type=F32)
                dv_ref[pl.ds(starts[j], tb), :] += lax.dot_general(a16[i], doh[h], _TN, preferred_element_type=F32)
            for h in range(2):
                cc_ref[h] = cc[h]
                cg_ref[h] = cg[h]

        def step(i, _):
            kb = unroll * i
            tiles([kb + u for u in range(unroll)], False)
            return 0

        lax.fori_loop(0, qb // unroll, step, 0)
        for left in range(1, unroll):

            @pl.when(qb % unroll == left)
            def _(left=left):
                tiles([qb - left + u for u in range(left)], False)

        tiles([qb], True)
        dq_ref[...] = dqacc_ref[0] + dqacc_ref[1]

    qblk = pl.BlockSpec((tb, LANES), lambda p, i: (i, p))
    whole = pl.BlockSpec((t, LANES), lambda p, i: (0, p))
    return pl.pallas_call(
        body,
        name=name,
        grid=(n_pairs, t // tb),
        in_specs=[pl.BlockSpec((tb, LANES), lambda p, i: (i, dy0 + p)), qblk, whole, whole, qblk,
                  pl.BlockSpec(order_after.shape, lambda p, i: (0, 0))],
        out_specs=[qblk, whole, whole],
        out_shape=[jax.ShapeDtypeStruct((t, ATTN_W), F32)] * 3,
        scratch_shapes=[pltpu.VMEM((2, tb, LANES), F32)] * 3,
        compiler_params=_params("parallel", "arbitrary"),
    )(dmix, qs, kn, v, lt, order_after)


_SGU_CHUNKS_PER_STEP = 4


def _sgu_rows(t):
    return CHUNK * math.gcd(_SGU_CHUNKS_PER_STEP, t // CHUNK)


def _sgu_weights(w_ref):
    tt = lax.broadcasted_iota(jnp.int32, (CHUNK, CHUNK), 0)
    ss = lax.broadcasted_iota(jnp.int32, (CHUNK, CHUNK), 1)
    tril = ss <= tt
    return [jnp.where(tril, w_ref[gi], 0.0).astype(BF16) for gi in range(SGU_HEADS)], tril


def _sgu_fwd(proj, g_v, w_s, b_exp, *, name):
    t = proj.shape[0]
    u0 = (3 * CONV_W + 3 * ATTN_W) // SGU_W
    rows = _sgu_rows(t)

    def body(u_ref, v_ref, g_ref, w_ref, b_ref, o_ref):
        grp = lax.broadcasted_iota(jnp.int32, (1, SGU_W), 1) // HEAD_DIM
        wm, _ = _sgu_weights(w_ref)
        gain, bias = g_ref[...], b_ref[...]
        for c in range(rows // CHUNK):
            chunk = pl.ds(c * CHUNK, CHUNK)
            u = _gelu(u_ref[chunk, :])
            vv = _gelu(v_ref[chunk, :])
            vn = (vv * lax.rsqrt(_group_mean(vv * vv, SGU_W) + EPS) * gain).astype(BF16)
            sv = bias
            for gi in range(SGU_HEADS):
                sv = sv + jnp.dot(wm[gi], jnp.where(grp == gi, vn, jnp.zeros_like(vn)), preferred_element_type=F32)
            o_ref[chunk, :] = (u * sv).astype(o_ref.dtype)

    return pl.pallas_call(
        body,
        name=name,
        grid=(t // rows,),
        in_specs=[
            pl.BlockSpec((rows, SGU_W), lambda i: (i, u0)),
            pl.BlockSpec((rows, SGU_W), lambda i: (i, u0 + 1)),
            pl.BlockSpec((1, SGU_W), lambda i: (0, 0)),
            pl.BlockSpec((SGU_HEADS, CHUNK, CHUNK), lambda i: (0, 0, 0)),
            pl.BlockSpec((CHUNK, SGU_W), lambda i: (0, 0)),
        ],
        out_specs=pl.BlockSpec((rows, SGU_W), lambda i: (i, 0)),
        out_shape=jax.ShapeDtypeStruct((t, SGU_W), BF16),
        compiler_params=_params("parallel"),
    )(proj, proj, g_v, w_s, b_exp)


def _sgu_bwd(dmix, proj, g_v, w_s, b_exp, *, name):
    t = proj.shape[0]
    u0 = (3 * CONV_W + 3 * ATTN_W) // SGU_W
    dy0 = (CONV_W + ATTN_W) // SGU_W
    rows = _sgu_rows(t)

    def body(dy_ref, u_ref, v_ref, g_ref, w_ref, b_ref, o_ref, dg_ref, dw_ref, db_ref):
        grp = lax.broadcasted_iota(jnp.int32, (1, SGU_W), 1) // HEAD_DIM
        gain, bias = g_ref[...], b_ref[...]
        wm, tril = _sgu_weights(w_ref)

        @pl.when(pl.program_id(0) == 0)
        def _():
            dg_ref[...] = jnp.zeros_like(dg_ref)
            dw_ref[...] = jnp.zeros_like(dw_ref)
            db_ref[...] = jnp.zeros_like(db_ref)

        dg = jnp.zeros_like(gain)
        db = jnp.zeros_like(bias)
        dw = [jnp.zeros((CHUNK, CHUNK), F32) for _ in range(SGU_HEADS)]
        for c in range(rows // CHUNK):
            chunk = pl.ds(c * CHUNK, CHUNK)
            cu, cv = u_ref[chunk, :], v_ref[chunk, :]
            u = _gelu(cu)
            vv = _gelu(cv)
            r = lax.rsqrt(_group_mean(vv * vv, SGU_W) + EPS)
            xhat = vv * r
            vn = (xhat * gain).astype(BF16)
            vng = [jnp.where(grp == gi, vn, jnp.zeros_like(vn)) for gi in range(SGU_HEADS)]
            sv = bias
            for gi in range(SGU_HEADS):
                sv = sv + jnp.dot(wm[gi], vng[gi], preferred_element_type=F32)
            dy = dy_ref[chunk, :]
            o_ref[chunk, 0:SGU_W] = (dy * sv * _gelu_grad(cu)).astype(o_ref.dtype)
            dsv = dy * u
            dsv16 = dsv.astype(BF16)
            db = db + dsv
            dvn = jnp.zeros_like(dsv)
            for gi in range(SGU_HEADS):
                dw[gi] = dw[gi] + lax.dot_general(dsv16, vng[gi], _NT, preferred_element_type=F32)
                dvn_g = lax.dot_general(wm[gi], dsv16, _TN, preferred_element_type=F32)
                dvn = jnp.where(grp == gi, dvn_g, dvn)
            dg = dg + jnp.sum(dvn * xhat, axis=0, keepdims=True)
            dxhat = dvn * gain
            dvv = r * (dxhat - xhat * _group_mean(dxhat * xhat, SGU_W))
            o_ref[chunk, SGU_W:2 * SGU_W] = (dvv * _gelu_grad(cv)).astype(o_ref.dtype)
        dg_ref[...] += dg
        db_ref[...] += db
        for gi in range(SGU_HEADS):
            dw_ref[gi] += jnp.where(tril, dw[gi], 0.0)

    return pl.pallas_call(
        body,
        name=name,
        grid=(t // rows,),
        in_specs=[
            pl.BlockSpec((rows, SGU_W), lambda i: (i, dy0)),
            pl.BlockSpec((rows, SGU_W), lambda i: (i, u0)),
            pl.BlockSpec((rows, SGU_W), lambda i: (i, u0 + 1)),
            pl.BlockSpec((1, SGU_W), lambda i: (0, 0)),
            pl.BlockSpec((SGU_HEADS, CHUNK, CHUNK), lambda i: (0, 0, 0)),
            pl.BlockSpec((CHUNK, SGU_W), lambda i: (0, 0)),
        ],
        out_specs=[
            pl.BlockSpec((rows, 2 * SGU_W), lambda i: (i, 0)),
            pl.BlockSpec((1, SGU_W), lambda i: (0, 0)),
            pl.BlockSpec((SGU_HEADS, CHUNK, CHUNK), lambda i: (0, 0, 0)),
            pl.BlockSpec((CHUNK, SGU_W), lambda i: (0, 0)),
        ],
        out_shape=[
            jax.ShapeDtypeStruct((t, 2 * SGU_W), BF16),
            jax.ShapeDtypeStruct((1, SGU_W), F32),
            jax.ShapeDtypeStruct((SGU_HEADS, CHUNK, CHUNK), F32),
            jax.ShapeDtypeStruct((CHUNK, SGU_W), F32),
        ],
        compiler_params=_params("arbitrary"),
    )(dmix, proj, proj, g_v, w_s, b_exp)


def _ple_bwd(dh, gate, pp, order_after, *, name, br=512):
    t, d = dh.shape
    br = min(br, t)

    def body(dh_ref, g_ref, p_ref, order_ref, dpre_ref, dpp_ref):
        dhv, g = dh_ref[...], g_ref[...]
        dpre_ref[...] = (dhv * p_ref[...] * g * (1.0 - g)).astype(dpre_ref.dtype)
        dpp_ref[...] = (dhv * g).astype(dpp_ref.dtype)

    row = pl.BlockSpec((br, d), lambda i: (i, 0))
    return pl.pallas_call(
        body,
        name=name,
        grid=(t // br,),
        in_specs=[row] * 3 + [pl.BlockSpec(order_after.shape, lambda i: (0, 0))],
        out_specs=[row] * 2,
        out_shape=[jax.ShapeDtypeStruct((t, d), BF16)] * 2,
        compiler_params=_params("parallel"),
    )(dh, gate, pp, order_after)


def _loss_head(y, target, *, name, br=512):
    t, d = y.shape
    br = min(br, t)

    def body(y_ref, t_ref, dy_ref, loss_ref):
        err = y_ref[...] - t_ref[...]
        dy_ref[...] = err * (1.0 / d)

        @pl.when(pl.program_id(0) == 0)
        def _():
            loss_ref[...] = jnp.zeros_like(loss_ref)

        loss_ref[...] += 0.5 * jnp.sum(jnp.sum(err * err, axis=1, keepdims=True) * (1.0 / d), axis=0, keepdims=True)

    row = pl.BlockSpec((br, d), lambda i: (i, 0))
    return pl.pallas_call(
        body,
        name=name,
        grid=(t // br,),
        in_specs=[row, row],
        out_specs=[row, pl.BlockSpec((8, LANES), lambda i: (0, 0))],
        out_shape=[jax.ShapeDtypeStruct((t, d), F32), jax.ShapeDtypeStruct((8, LANES), F32)],
        compiler_params=_params("arbitrary"),
    )(y, target)


def _adamw_update(w, g, m, v):
    nm = ADAM_B1 * m + (1.0 - ADAM_B1) * g
    nv = ADAM_B2 * v + (1.0 - ADAM_B2) * (g * g)
    m_hat = nm / (1.0 - ADAM_B1 ** ADAM_STEP)
    v_hat = nv / (1.0 - ADAM_B2 ** ADAM_STEP)
    return -ADAM_LR * (m_hat / (jnp.sqrt(v_hat) + ADAM_EPS) + ADAM_WD * w), nm, nv


def _adamw(w, g, m, v, *, name, br=512):
    r, c = w.shape
    br = _row_block(r, br)

    def body(w_ref, g_ref, m_ref, v_ref, d_ref, nm_ref, nv_ref):
        d_ref[...], nm_ref[...], nv_ref[...] = _adamw_update(w_ref[...], g_ref[...], m_ref[...], v_ref[...])

    row = pl.BlockSpec((br, c), lambda i: (i, 0))
    return pl.pallas_call(
        body,
        name=name,
        grid=(r // br,),
        in_specs=[row] * 4,
        out_specs=[row] * 3,
        out_shape=[jax.ShapeDtypeStruct((r, c), F32)] * 3,
        compiler_params=_params("parallel"),
    )(w, g, m, v)


def _sum_slots(per_layer, *, name):
    counts = [len(arrays) for arrays in per_layer]
    flat = [a for arrays in per_layer for a in arrays]

    def body(*refs):
        ins, outs = refs[:len(flat)], refs[len(flat):]
        at = 0
        for o_ref, count in zip(outs, counts, strict=True):
            for li in range(count):
                acc = ins[at + li][0]
                for j in range(1, N_DEV):
                    acc = acc + ins[at + li][j]
                o_ref[li] = acc
            at += count

    return pl.pallas_call(
        body,
        name=name,
        out_shape=[jax.ShapeDtypeStruct((len(arrays), *arrays[0].shape[1:]), F32) for arrays in per_layer],
        compiler_params=pltpu.CompilerParams(vmem_limit_bytes=VMEM_LIMIT_BYTES),
    )(*flat)


_ADAMW_BLOCK_ELEMS = 128 * 1024


def _adamw_reduce(w, arrived, m, v, *, name):
    depth, r, c = w.shape
    br = _row_block(r, max(BF16_TILE_ROWS, _ADAMW_BLOCK_ELEMS // (-(-c // LANES) * LANES)))

    def body(w_ref, m_ref, v_ref, *rest):
        parts, (g_ref, d_ref, nm_ref, nv_ref) = rest[:depth], rest[depth:]
        for li in range(depth):

            @pl.when(pl.program_id(0) == li)
            def _(li=li):
                g = parts[li][0].astype(F32)
                for j in range(1, N_DEV):
                    g = g + parts[li][j].astype(F32)
                g_ref[...] = g
                d_ref[...], nm_ref[...], nv_ref[...] = _adamw_update(w_ref[...], g, m_ref[...], v_ref[...])

    cur = pl.BlockSpec((None, br, c), lambda l, i: (l, i, 0))
    slots = [pl.BlockSpec((N_DEV, br, c), lambda l, i, li=li: (0, jnp.where(l == li, i, 0), 0)) for li in range(depth)]
    return pl.pallas_call(
        body,
        name=name,
        grid=(depth, r // br),
        in_specs=[cur, cur, cur] + slots,
        out_specs=[cur] * 4,
        out_shape=[jax.ShapeDtypeStruct((depth, r, c), F32)] * 4,
        compiler_params=_params("arbitrary", "arbitrary"),
    )(w, m, v, *arrived)


def _my_place():
    return lax.axis_index("x"), lax.axis_index("y"), lax.axis_index("c")


def _flip(v, bit):
    return 1 - v if bit else v


def _slot_of(px, py, pc):
    return 4 * px + 2 * py + pc


_ANY = pl.BlockSpec(memory_space=pl.ANY)


_HBM = pl.BlockSpec(memory_space=pltpu.HBM)
_SEM = pl.BlockSpec(memory_space=pltpu.SEMAPHORE)
_DATAFLOW = pltpu.SideEffectType.DATAFLOW_SIDE_EFFECTING


_GATHER, _GATHER_COLUMNS, _SCATTER = "gather", "gather_columns", "scatter"


def _landing_shape(a, mode):
    if mode == _SCATTER:
        return a.shape
    if mode == _GATHER_COLUMNS:
        return (a.shape[0], N_DEV * a.shape[1])
    return (N_DEV, *a.shape)


_DIRECT, _NEAR, _RELAY = "direct", "near", "relay"
_OTHER_CHIPS = (2, 4, 6)
_SIBLING = 1


def _exchange_copies(src_refs, land_refs, send_sem, recv_sem, modes, hops=_DIRECT):
    mx, my, mc = _my_place()
    peer_of = lambda k: (_flip(mx, k & 4), _flip(my, k & 2), _flip(mc, k & 1))
    mine = _slot_of(mx, my, mc)

    def block(land, mode, slot):
        if mode == _GATHER_COLUMNS:
            n = land.shape[1] // N_DEV
            return land.at[:, pl.ds(pl.multiple_of(slot * n, LANES), n)]
        return land.at[slot]

    def remote_copy(src, dst, to):
        return pltpu.make_async_remote_copy(src_ref=src, dst_ref=dst, send_sem=send_sem, recv_sem=recv_sem,
                                            device_id=to, device_id_type=MESH)

    remote, local = [], []
    for src, land, mode in zip(src_refs, land_refs, modes, strict=True):
        if hops == _RELAY:
            assert mode != _SCATTER
            for k in _OTHER_CHIPS:
                came = block(land, mode, _slot_of(*peer_of(k)))
                remote.append(remote_copy(came, came, peer_of(_SIBLING)))
            continue
        dst = block(land, mode, mine)
        for k in ((_SIBLING,) + _OTHER_CHIPS if hops == _NEAR else range(1, N_DEV)):
            remote.append(remote_copy(src.at[_slot_of(*peer_of(k))] if mode == _SCATTER else src, dst, peer_of(k)))
        local.append(pltpu.make_async_copy(src.at[mine] if mode == _SCATTER else src, dst, recv_sem))
    return remote, local


def _wait_copies(remote, local):
    for cp in remote:
        cp.wait_send()
        cp.wait_recv()
    for cp in local:
        cp.wait()


def _exchange_start(groups, after, *, name, hops=_DIRECT):
    sizes = [len(srcs) for srcs, _ in groups]
    n, n_sems = sum(sizes), 2 * len(groups)
    srcs = [a for arrays, _ in groups for a in arrays]
    lands = [lax.empty(_landing_shape(a, mode), a.dtype)
             for arrays, modes in groups for a, mode in zip(arrays, modes, strict=True)]
    offsets = [sum(sizes[:g]) for g in range(len(groups))]

    def body(*refs):
        sems = refs[2 * n + 1:2 * n + 1 + n_sems]
        for g, (off, size, (_, modes)) in enumerate(zip(offsets, sizes, groups)):
            remote, local = _exchange_copies(refs[off:off + size], refs[n + off:n + off + size], sems[2 * g],
                                             sems[2 * g + 1], modes, hops)
            for cp in remote + local:
                cp.start()
        refs[-1][...] = jnp.zeros_like(refs[-1])

    thru = [pltpu.HBM(a.shape, a.dtype) for a in (*srcs, *lands)]
    out = pl.pallas_call(
        body,
        name=name,
        in_specs=[_HBM] * (2 * n) + [_ANY],
        out_specs=(*[_SEM] * n_sems, *[_HBM] * (2 * n), pl.BlockSpec(memory_space=pltpu.VMEM)),
        out_shape=(*[pltpu.SemaphoreType.DMA(())] * n_sems, *thru, jax.ShapeDtypeStruct((8, LANES), F32)),
        input_output_aliases={i: n_sems + i for i in range(2 * n)},
        compiler_params=pltpu.CompilerParams(has_side_effects=_DATAFLOW),
    )(*[pltpu.with_memory_space_constraint(a, pltpu.HBM) for a in (*srcs, *lands)], after)
    sems, arrays = out[:n_sems], out[n_sems:-1]
    started = [(sems[2 * g], sems[2 * g + 1], *arrays[off:off + size], *arrays[n + off:n + off + size])
               for g, (off, size) in enumerate(zip(offsets, sizes))]
    return started, out[-1]


def _exchange_relay(started, after, *, modes, regroup, name):
    send_sem, recv_sem, *thru = started
    n, n_sems = len(thru) // 2, 2 * len(regroup)

    def body(*refs):
        srcs, lands = refs[:n], refs[n:2 * n]
        _wait_copies(*_exchange_copies(srcs, lands, refs[2 * n], refs[2 * n + 1], modes, _NEAR))
        sems = refs[2 * n + 3:2 * n + 3 + n_sems]
        for g, members in enumerate(regroup):
            remote, _ = _exchange_copies([srcs[i] for i in members], [lands[i] for i in members], sems[2 * g],
                                         sems[2 * g + 1], [modes[i] for i in members], _RELAY)
            for cp in remote:
                cp.start()
        refs[-1][...] = jnp.zeros_like(refs[-1])

    out = pl.pallas_call(
        body,
        name=name,
        in_specs=[_HBM] * (2 * n) + [_SEM, _SEM, _ANY],
        out_specs=(*[_SEM] * n_sems, *[_HBM] * (2 * n), pl.BlockSpec(memory_space=pltpu.VMEM)),
        out_shape=(*[pltpu.SemaphoreType.DMA(())] * n_sems, *[pltpu.HBM(a.shape, a.dtype) for a in thru],
                   jax.ShapeDtypeStruct((8, LANES), F32)),
        input_output_aliases={i: n_sems + i for i in range(2 * n)},
        compiler_params=pltpu.CompilerParams(has_side_effects=_DATAFLOW),
    )(*thru, send_sem, recv_sem, after)
    sems, arrays = out[:n_sems], out[n_sems:-1]
    groups = [(sems[2 * g], sems[2 * g + 1], *[arrays[i] for i in members], *[arrays[n + i] for i in members])
              for g, members in enumerate(regroup)]
    return groups, out[-1]


def _exchange_wait(started, after, *, modes, name, hops=_DIRECT):
    send_sem, recv_sem, *thru = started
    n = len(thru) // 2

    def body(*refs):
        _wait_copies(*_exchange_copies(refs[:n], refs[n:2 * n], refs[2 * n], refs[2 * n + 1], modes, hops))

    out = pl.pallas_call(
        body,
        name=name,
        in_specs=[_HBM] * (2 * n) + [_SEM, _SEM, _ANY],
        out_specs=[_HBM] * (2 * n),
        out_shape=[pltpu.HBM(a.shape, a.dtype) for a in thru],
        input_output_aliases={i: i for i in range(2 * n)},
        compiler_params=pltpu.CompilerParams(has_side_effects=_DATAFLOW),
    )(*thru, send_sem, recv_sem, after)
    return out[n:]


def _gather_columns(g):
    return jnp.moveaxis(g, 0, 1).reshape(g.shape[1], -1)


def _split_rows(w):
    return w.reshape(N_DEV, w.shape[0] // N_DEV, w.shape[1])


_FIRST = ("w_in", "conv_w")
_REST = ("w_out", "w_ff1", "w_ff2", "w_ple_gate", "w_ple_proj")
_REST_GROUPS = (("w_out",), ("w_ff1",), ("w_ff2",), ("w_ple_gate", "w_ple_proj"))
_BIG = ("w_in",) + _REST
_GATHER_MODE = dict(w_in=_GATHER, conv_w=_GATHER, w_out=_GATHER, w_ff1=_GATHER_COLUMNS, w_ff2=_GATHER,
                    w_ple_gate=_GATHER, w_ple_proj=_GATHER_COLUMNS)
_RELAYOUT_AFTER_GATHER = ("w_in", "conv_w")
_SMALL = ("norm1_g", "q_norm_g", "k_norm_g", "sgu_norm_g", "sgu_w", "sgu_b", "norm2_g", "norm3_g")
_ORDER = ("norm1_g", "w_in", "conv_w", "q_norm_g", "k_norm_g", "sgu_norm_g", "sgu_w", "sgu_b", "w_out", "norm2_g",
          "w_ff1", "w_ff2", "norm3_g", "w_ple_gate", "w_ple_proj")


def _whole_matrices(names, landed):
    return {k: _gather_columns(g) if k in _RELAYOUT_AFTER_GATHER else g.reshape(-1, g.shape[-1])
            for k, g in zip(names, landed, strict=True)}


def _layer_forward(h0, p16, s, li, w_first, gathered):
    nm = lambda k: f"{k}_l{li}"
    t = h0.shape[0]
    w = dict(w_first)
    hn1 = _rms_fwd(h0, s["norm1_g"], name=nm("rms1"))
    proj = _matmul(hn1, w["w_in"], name=nm("proj"), bm=t, bn=256)
    y_a = _conv_fwd(proj, w["conv_w"], name=nm("conv"))
    qs, kn, v = _qk_prep(proj, s["gq"], s["gk"], name=nm("qkprep"))
    y_b, lt = _attn_fwd(qs, kn, v, name=nm("attn"))
    gathered["relay_rest"](y_b)
    y_c = _sgu_fwd(proj, s["sgu_norm_g"], s["sgu_w"], s["b_exp"], name=nm("sgu"))
    mix = jnp.concatenate([y_a, y_b, y_c], axis=1)
    w.update(gathered["fetch"](0, mix))
    h1 = _matmul(mix, w["w_out"], name=nm("out"), bm=t, bn=256, extras=(h0,), epilogue=lambda acc, r: (r + acc,))
    hn2 = _rms_fwd(h1, s["norm2_g"], name=nm("rms2"))
    w.update(gathered["fetch"](1, hn2))
    f = _matmul(hn2, w["w_ff1"], name=nm("ff1"), bm=t, bn=512, out_dtypes=(BF16,),
                epilogue=lambda acc: (jnp.square(jnp.maximum(acc, 0.0)),))
    w.update(gathered["fetch"](2, f))
    h2 = _matmul(f, w["w_ff2"], name=nm("ff2"), bm=1024, bn=512, extras=(h1,), epilogue=lambda acc, r: (r + acc,))
    gathered["relay_next"](h2)
    hn3 = _rms_fwd(h2, s["norm3_g"], name=nm("rms3"))
    w.update(gathered["fetch"](3, hn3))
    w_next = gathered["fetch_next"](hn3)
    pp = _matmul(p16, w["w_ple_proj"], name=nm("pleproj"), bm=t, bn=512)

    def gate_epilogue(acc, pp_blk, h_blk):
        gate = jax.nn.sigmoid(acc)
        return h_blk + gate * pp_blk, gate

    h3, gate = _matmul(hn3, w["w_ple_gate"], name=nm("plegate"), bm=t, bn=256, out_dtypes=(F32, F32),
                       extras=(pp, h2), epilogue=gate_epilogue)
    saved = dict(h0=h0, hn1=hn1, proj=proj, qs=qs, kn=kn, v=v, lt=lt, mix=mix, h1=h1, hn2=hn2, f=f, h2=h2,
                 hn3=hn3, pp=pp, gate=gate, p16=p16)
    return h3, w, w_next, saved


def _layer_backward(dh3, a, w, s, li, order_after, start_rest):
    nm = lambda k: f"{k}_bwd_l{li}"
    t = dh3.shape[0]
    dpre, dpp = _ple_bwd(dh3, a["gate"], a["pp"], order_after, name=nm("ple"))
    g_gate = _weight_grad(a["hn3"], [dpre], name=nm("dwgate"))
    g_proj = _weight_grad(a["p16"], [dpp], name=nm("dwproj"), column_shards=True)
    dh2, dh2_16, g_n3 = _matmul_rms_bwd([dpre], w["w_ple_gate"], a["h2"], s["norm3_g"], dh3, name=nm("dh2"))
    du = _matmul(dh2_16, w["w_ff2"], name=nm("du"), tb=True, bm=t, bn=512, out_dtypes=(BF16,), extras=(a["f"],),
                 epilogue=lambda acc, f: (acc * (2.0 * jnp.sqrt(f.astype(F32))),))
    g_ff2 = _weight_grad(a["f"], [dh2_16], name=nm("dwff2"))
    g_ff1 = _weight_grad(a["hn2"], [du], name=nm("dwff1"), column_shards=True)
    dh1, dh1_16, g_n2 = _matmul_rms_bwd([du], w["w_ff1"], a["h1"], s["norm2_g"], dh2, name=nm("dh1"))
    dmix = _matmul(dh1_16, w["w_out"], name=nm("dmix"), tb=True, bm=t, bn=256)
    g_out = _weight_grad(a["mix"], [dh1_16], name=nm("dwout"))
    started = start_rest(dict(w_out=_split_rows(g_out), w_ff1=g_ff1, w_ff2=_split_rows(g_ff2),
                              w_ple_gate=_split_rows(g_gate), w_ple_proj=g_proj), dmix)
    d_conv, g_conv = _conv_bwd(dmix, a["proj"], w["conv_w"], name=nm("conv"))
    dqs, dkn, dv = _attn_bwd(dmix, a["qs"], a["kn"], a["v"], a["lt"], started, name=nm("attn"))
    d_qkv, g_q, g_k = _qk_prep_bwd(dqs, dkn, dv, a["proj"], s["gq"], s["gk"], name=nm("qkprep"))
    d_sgu, g_sn, g_sw, g_sb = _sgu_bwd(dmix, a["proj"], s["sgu_norm_g"], s["sgu_w"], s["b_exp"], name=nm("sgu"))
    dproj = [d_conv, d_qkv, d_sgu]
    g_in = _weight_grad(a["hn1"], dproj, name=nm("dwin"), column_shards=True)
    dh0, _, g_n1 = _matmul_rms_bwd(dproj, w["w_in"], a["h0"], s["norm1_g"], dh1, name=nm("dh0"))
    small = dict(norm1_g=g_n1, norm2_g=g_n2, norm3_g=g_n3, q_norm_g=g_q, k_norm_g=g_k, sgu_norm_g=g_sn, sgu_w=g_sw,
                 sgu_b=g_sb, conv_w=g_conv)
    return dh0, g_in, small


def _small_gradients(raw, depth):
    return dict(
        norm1_g=raw["norm1_g"].reshape(depth, -1), norm2_g=raw["norm2_g"].reshape(depth, -1),
        norm3_g=raw["norm3_g"].reshape(depth, -1),
        q_norm_g=raw["q_norm_g"].reshape(depth, -1, HEAD_DIM).sum(1),
        k_norm_g=raw["k_norm_g"].reshape(depth, -1, HEAD_DIM).sum(1),
        sgu_norm_g=raw["sgu_norm_g"].reshape(depth, -1), sgu_w=raw["sgu_w"],
        sgu_b=jnp.swapaxes(raw["sgu_b"].reshape(depth, CHUNK, SGU_HEADS, HEAD_DIM).sum(-1), 1, 2),
        conv_w=raw["conv_w"][:, :CONV_TAPS],
    )


def kernel(x, p, norm1_g, w_in, conv_w, q_norm_g, k_norm_g, sgu_norm_g, sgu_w, sgu_b, w_out, norm2_g, w_ff1, w_ff2, norm3_g, w_ple_gate, w_ple_proj, loss_target, m_norm1_g, m_w_in, m_conv_w, m_q_norm_g, m_k_norm_g, m_sgu_norm_g, m_sgu_w, m_sgu_b, m_w_out, m_norm2_g, m_w_ff1, m_w_ff2, m_norm3_g, m_w_ple_gate, m_w_ple_proj, v_norm1_g, v_w_in, v_conv_w, v_q_norm_g, v_k_norm_g, v_sgu_norm_g, v_sgu_w, v_sgu_b, v_w_out, v_norm2_g, v_w_ff1, v_w_ff2, v_norm3_g, v_w_ple_gate, v_w_ple_proj):
    weights = dict(norm1_g=norm1_g, w_in=w_in, conv_w=conv_w, q_norm_g=q_norm_g, k_norm_g=k_norm_g,
                   sgu_norm_g=sgu_norm_g, sgu_w=sgu_w, sgu_b=sgu_b, w_out=w_out, norm2_g=norm2_g, w_ff1=w_ff1,
                   w_ff2=w_ff2, norm3_g=norm3_g, w_ple_gate=w_ple_gate, w_ple_proj=w_ple_proj)
    mom = dict(norm1_g=m_norm1_g, w_in=m_w_in, conv_w=m_conv_w, q_norm_g=m_q_norm_g, k_norm_g=m_k_norm_g,
               sgu_norm_g=m_sgu_norm_g, sgu_w=m_sgu_w, sgu_b=m_sgu_b, w_out=m_w_out, norm2_g=m_norm2_g, w_ff1=m_w_ff1,
               w_ff2=m_w_ff2, norm3_g=m_norm3_g, w_ple_gate=m_w_ple_gate, w_ple_proj=m_w_ple_proj)
    var = dict(norm1_g=v_norm1_g, w_in=v_w_in, conv_w=v_conv_w, q_norm_g=v_q_norm_g, k_norm_g=v_k_norm_g,
               sgu_norm_g=v_sgu_norm_g, sgu_w=v_sgu_w, sgu_b=v_sgu_b, w_out=v_w_out, norm2_g=v_norm2_g, w_ff1=v_w_ff1,
               w_ff2=v_w_ff2, norm3_g=v_norm3_g, w_ple_gate=v_w_ple_gate, w_ple_proj=v_w_ple_proj)
    depth = norm1_g.shape[0]
    mx, my, mc = _my_place()
    me = _slot_of(mx, my, mc)

    gathers = []
    modes_of = lambda names: tuple(_GATHER_MODE[k] for k in names)
    token = x[0, :8, :LANES]
    for li in range(depth):
        groups = [([weights[k][li] if k == "conv_w" else weights[k][li].astype(BF16) for k in names], modes_of(names))
                  for names in (_FIRST, _REST)]
        started, token = _exchange_start(groups, token, name=f"gather_start_l{li}", hops=_NEAR)
        gathers.append(started)

    small = []
    for li in range(depth):
        small.append(dict(
            norm1_g=norm1_g[li][None], norm2_g=norm2_g[li][None], norm3_g=norm3_g[li][None],
            gq=jnp.tile(q_norm_g[li], _QK_BLOCK // HEAD_DIM)[None], gk=jnp.tile(k_norm_g[li], _QK_BLOCK // HEAD_DIM)[None],
            sgu_norm_g=sgu_norm_g[li][None], sgu_w=sgu_w[li], b_exp=jnp.repeat(sgu_b[li].T, HEAD_DIM, axis=1),
        ))
    small[0]["norm1_g"] = small[0]["norm1_g"] + token[0, 0]

    h = x[0]
    saved, full = [], []
    relayed_first, relayed_rest = [None] * depth, [None] * depth
    rest_members = [[_REST.index(k) for k in names] for names in _REST_GROUPS]

    def relay_first(li, after):
        if li < depth:
            (relayed_first[li],), _ = _exchange_relay(gathers[li][0], after, modes=modes_of(_FIRST),
                                                      regroup=[list(range(len(_FIRST)))], name=f"gather_first_relay_l{li}")

    def fetch_first(li, after):
        if li == depth:
            return None
        landed = _exchange_wait(relayed_first[li], after, modes=modes_of(_FIRST), hops=_RELAY,
                                name=f"gather_first_wait_l{li}")
        return _whole_matrices(_FIRST, landed)

    relay_first(0, token)
    w_first = fetch_first(0, small[0]["norm1_g"])
    for li in range(depth):

        def relay_rest(after, li=li):
            relayed_rest[li], _ = _exchange_relay(gathers[li][1], after, modes=modes_of(_REST), regroup=rest_members,
                                                  name=f"gather_rest_relay_l{li}")

        def fetch(g, after, li=li):
            landed = _exchange_wait(relayed_rest[li][g], after, modes=modes_of(_REST_GROUPS[g]), hops=_RELAY,
                                    name=f"gather_{_REST_GROUPS[g][0]}_wait_l{li}")
            return _whole_matrices(_REST_GROUPS[g], landed)

        gathered = dict(relay_rest=relay_rest, fetch=fetch, relay_next=functools.partial(relay_first, li + 1),
                        fetch_next=functools.partial(fetch_first, li + 1))
        h, w, w_first, acts = _layer_forward(h, p[li, 0].astype(BF16), small[li], li, w_first, gathered)
        full.append(w)
        saved.append(acts)
    dh, loss_tile = _loss_head(h, loss_target[0], name="loss_head")
    loss = lax.psum(loss_tile[0, 0], ("x", "y", "c"))

    small_names = _SMALL + ("conv_w",)
    scatter_first, scatter_rest = [None] * depth, [None] * depth
    first_modes, rest_modes = (_SCATTER,) + (_GATHER,) * len(small_names), (_SCATTER,) * len(_REST)
    token = loss_tile
    for li in reversed(range(depth)):

        def start_rest(parts, after, li=li):
            (scatter_rest[li],), started = _exchange_start([([parts[k] for k in _REST], rest_modes)], after,
                                                           name=f"scatter_rest_start_l{li}")
            return started

        dh, g_in, small_grads = _layer_backward(dh, saved[li], full[li], small[li], li, token, start_rest)
        (scatter_first[li],), token = _exchange_start(
            [([g_in] + [small_grads[k] for k in small_names], first_modes)], dh, name=f"scatter_first_start_l{li}")
    grad_x = dh[None]

    grads, delta, new_m, new_v = {}, {}, {}, {}
    arrived = {k: [None] * depth for k in _BIG}
    for li in reversed(range(depth)):
        landed = _exchange_wait(scatter_rest[li], token, modes=rest_modes, name=f"scatter_rest_wait_l{li}")
        for k, g in zip(_REST, landed, strict=True):
            arrived[k][li] = g
    for k in _REST:
        grads[k], delta[k], new_m[k], new_v[k] = _adamw_reduce(weights[k], arrived[k], mom[k], var[k], name=f"adamw_{k}")
    small_parts = {k: [None] * depth for k in small_names}
    updated = jnp.stack([delta[k][0, 0, :1] for k in _REST])
    for li in reversed(range(depth)):
        arrived["w_in"][li], *parts = _exchange_wait(scatter_first[li], updated, modes=first_modes,
                                                     name=f"scatter_first_wait_l{li}")
        for k, part in zip(small_names, parts, strict=True):
            small_parts[k][li] = part
    grads["w_in"], delta["w_in"], new_m["w_in"], new_v["w_in"] = _adamw_reduce(
        w_in, arrived["w_in"], mom["w_in"], var["w_in"], name="adamw_w_in")
    sums = _sum_slots([small_parts[k] for k in small_names], name="sum_small_grads")
    grads.update(_small_gradients(dict(zip(small_names, sums)), depth))
    n_conv = conv_w.shape[2]
    grads["conv_w"] = lax.dynamic_slice_in_dim(grads["conv_w"], me * n_conv, n_conv, axis=2)
    for k in small_names:
        as_rows = lambda a: a.reshape(-1, a.shape[-1])
        outs = _adamw(as_rows(weights[k]), as_rows(grads[k]), as_rows(mom[k]), as_rows(var[k]), name=f"adamw_{k}")
        delta[k], new_m[k], new_v[k] = (o.reshape(weights[k].shape) for o in outs)

    return (loss, grad_x, *[grads[k] for k in _ORDER], *[delta[k] for k in _ORDER],
            *[new_m[k] for k in _ORDER], *[new_v[k] for k in _ORDER])
```

```python
import functools
import math

import jax
import jax.numpy as jnp
from jax import lax
from jax.experimental import pallas as pl
from jax.experimental.pallas import tpu as pltpu

F32 = jnp.float32
BF16 = jnp.bfloat16

N_DEV = 8
HEAD_DIM = 64
CONV_W = 256
ATTN_W = 512
SGU_W = 256
SGU_HEADS = 4
CHUNK = 128
CONV_TAPS = 3
EPS = 1e-6
QK_SCALE = HEAD_DIM ** -0.5

ADAM_LR = 0.001
ADAM_B1 = 0.9
ADAM_B2 = 0.999
ADAM_EPS = 1e-08
ADAM_WD = 0.01
ADAM_STEP = 10

LANES = 128
BF16_TILE_ROWS = 16
VMEM_LIMIT_BYTES = 56 * 1024 * 1024
MESH = pl.DeviceIdType.MESH


def _params(*sem):
    return pltpu.CompilerParams(dimension_semantics=sem, vmem_limit_bytes=VMEM_LIMIT_BYTES)


def _row_block(rows, cap):
    if rows <= cap:
        return rows
    return max(b for b in range(BF16_TILE_ROWS, cap + 1, BF16_TILE_ROWS) if rows % b == 0)


def _matmul(a, b, *, name, tb=False, bm=512, bn=512, out_dtypes=(F32,), epilogue=None, extras=()):
    m, k = a.shape
    n = b.shape[0] if tb else b.shape[1]
    assert k == (b.shape[1] if tb else b.shape[0])
    bm, bn = min(bm, m), min(bn, n)
    assert m % bm == 0 and n % bn == 0
    a_spec = pl.BlockSpec((bm, k), lambda i, j: (i, 0))
    b_spec = pl.BlockSpec((bn, k), lambda i, j: (j, 0)) if tb else pl.BlockSpec((k, bn), lambda i, j: (0, j))
    dims = (((1,), (1 if tb else 0,)), ((), ()))
    n_ex = len(extras)
    for e in extras:
        assert e.shape == (m, n), (e.shape, m, n)

    def body(a_ref, b_ref, *rest):
        outs = rest[n_ex:]
        acc = lax.dot_general(a_ref[...], b_ref[...], dims, preferred_element_type=F32)
        res = (acc,) if epilogue is None else epilogue(acc, *[e[...] for e in rest[:n_ex]])
        for o_ref, r in zip(outs, res, strict=True):
            o_ref[...] = r.astype(o_ref.dtype)

    tile = pl.BlockSpec((bm, bn), lambda i, j: (i, j))
    out = pl.pallas_call(
        body,
        name=name,
        grid=(m // bm, n // bn),
        in_specs=[a_spec, b_spec] + [tile] * n_ex,
        out_specs=[tile] * len(out_dtypes),
        out_shape=[jax.ShapeDtypeStruct((m, n), d) for d in out_dtypes],
        compiler_params=_params("parallel", "parallel"),
    )(a, b, *extras)
    return out[0] if len(out_dtypes) == 1 else out


_WEIGHT_GRAD_ACC_ELEMS = 1024 * 1024


def _weight_grad(x, dys, *, name, column_shards=False):
    t, m = x.shape
    n = sum(dy.shape[1] for dy in dys)
    bm = m if m <= 2 * LANES else min(m // 2, max(LANES, _WEIGHT_GRAD_ACC_ELEMS // n // LANES * LANES))
    assert m % bm == 0
    ns = n // N_DEV

    def body(x_ref, *rest):
        o_ref = rest[-1]
        xb = x_ref[...]
        acc = jnp.concatenate([lax.dot_general(xb, dy_ref[...], _TN, preferred_element_type=F32) for dy_ref in rest[:-1]],
                              axis=1)
        if column_shards:
            for s in range(N_DEV):
                o_ref[s] = acc[:, s * ns:(s + 1) * ns].astype(o_ref.dtype)
        else:
            o_ref[...] = acc.astype(o_ref.dtype)

    if column_shards:
        out_spec, out_dims = pl.BlockSpec((N_DEV, bm, ns), lambda i: (0, i, 0)), (N_DEV, m, ns)
    else:
        out_spec, out_dims = pl.BlockSpec((bm, n), lambda i: (i, 0)), (m, n)
    return pl.pallas_call(
        body,
        name=name,
        grid=(m // bm,),
        in_specs=[pl.BlockSpec((t, bm), lambda i: (0, i))] + [pl.BlockSpec(dy.shape, lambda i: (0, 0)) for dy in dys],
        out_specs=out_spec,
        out_shape=jax.ShapeDtypeStruct(out_dims, BF16),
        compiler_params=_params("parallel"),
    )(x, *dys)


def _rms_fwd(h, g, *, name, br=512):
    t, d = h.shape
    br = min(br, t)

    def body(h_ref, g_ref, o_ref):
        x = h_ref[...]
        r = lax.rsqrt(jnp.mean(x * x, axis=-1, keepdims=True) + EPS)
        o_ref[...] = (x * r * g_ref[...]).astype(o_ref.dtype)

    return pl.pallas_call(
        body,
        name=name,
        grid=(t // br,),
        in_specs=[pl.BlockSpec((br, d), lambda i: (i, 0)), pl.BlockSpec((1, d), lambda i: (0, 0))],
        out_specs=pl.BlockSpec((br, d), lambda i: (i, 0)),
        out_shape=jax.ShapeDtypeStruct((t, d), BF16),
        compiler_params=_params("parallel"),
    )(h, g)


def _matmul_rms_bwd(dzs, w, h, g, dres, *, name, w_is_k_by_d=False):
    t, d = h.shape
    widths = [dz.shape[1] for dz in dzs]
    k = sum(widths)
    assert w.shape == ((k, d) if w_is_k_by_d else (d, k))
    br = min(t, 512 if k <= d else 256)
    n_dz = len(dzs)

    def body(*refs):
        w_ref, h_ref, g_ref, dres_ref, dh_ref, dh16_ref, dg_ref = refs[n_dz:]
        x = h_ref[...]
        dyv, at = None, 0
        for dz_ref, width in zip(refs[:n_dz], widths):
            if w_is_k_by_d:
                part = jnp.dot(dz_ref[...], w_ref[at:at + width, :], preferred_element_type=F32)
            else:
                part = lax.dot_general(dz_ref[...], w_ref[:, at:at + width], _NT, preferred_element_type=F32)
            dyv = part if dyv is None else dyv + part
            at += width
        r = lax.rsqrt(jnp.mean(x * x, axis=-1, keepdims=True) + EPS)
        xhat = x * r
        dxhat = dyv * g_ref[...]
        dh = dres_ref[...] + r * (dxhat - xhat * jnp.mean(dxhat * xhat, axis=-1, keepdims=True))
        dh_ref[...] = dh
        dh16_ref[...] = dh.astype(dh16_ref.dtype)

        @pl.when(pl.program_id(0) == 0)
        def _():
            dg_ref[...] = jnp.zeros_like(dg_ref)

        dg_ref[...] += jnp.sum(dyv * xhat, axis=0, keepdims=True)

    row = pl.BlockSpec((br, d), lambda i: (i, 0))
    vec = pl.BlockSpec((1, d), lambda i: (0, 0))
    return pl.pallas_call(
        body,
        name=name,
        grid=(t // br,),
        in_specs=[pl.BlockSpec((br, width), lambda i: (i, 0)) for width in widths]
        + [pl.BlockSpec(w.shape, lambda i: (0, 0)), row, vec, row],
        out_specs=[row, row, vec],
        out_shape=[jax.ShapeDtypeStruct((t, d), F32), jax.ShapeDtypeStruct((t, d), BF16),
                   jax.ShapeDtypeStruct((1, d), F32)],
        compiler_params=_params("arbitrary"),
    )(*dzs, w, h, g, dres)


def _group_mean(x, width):
    grp = lax.broadcasted_iota(jnp.int32, x.shape, 1) // HEAD_DIM
    out = jnp.zeros_like(x)
    for gi in range(width // HEAD_DIM):
        m = grp == gi
        s = jnp.sum(jnp.where(m, x, 0.0), axis=1, keepdims=True)
        out = jnp.where(m, s, out)
    return out * (1.0 / HEAD_DIM)


def _gelu(x):
    return 0.5 * x * (1.0 + lax.erf(x * (2.0 ** -0.5)))


def _gelu_grad(x):
    cdf = 0.5 * (1.0 + lax.erf(x * (2.0 ** -0.5)))
    pdf = jnp.exp(-0.5 * x * x) * (1.0 / math.sqrt(2.0 * math.pi))
    return cdf + x * pdf


def _shift_down(z, s, row):
    return jnp.where(row >= s, pltpu.roll(z, s, 0), 0.0)


def _shift_up(z, s, row, t):
    return jnp.where(row < t - s, pltpu.roll(z, t - s, 0), 0.0)


def _conv_fwd(proj, conv_w, *, name):
    t = proj.shape[0]
    nb = CONV_W // LANES

    def body(b_ref, c_ref, h_ref, w_ref, o_ref):
        row = lax.broadcasted_iota(jnp.int32, (t, LANES), 0)
        z = c_ref[...] * h_ref[...]
        w = w_ref[...]
        conv = w[2:3, :] * z + w[1:2, :] * _shift_down(z, 1, row) + w[0:1, :] * _shift_down(z, 2, row)
        o_ref[...] = (b_ref[...] * conv).astype(o_ref.dtype)

    return pl.pallas_call(
        body,
        name=name,
        grid=(nb,),
        in_specs=[
            pl.BlockSpec((t, LANES), lambda j: (0, j)),
            pl.BlockSpec((t, LANES), lambda j: (0, nb + j)),
            pl.BlockSpec((t, LANES), lambda j: (0, 2 * nb + j)),
            pl.BlockSpec((CONV_TAPS, LANES), lambda j: (0, j)),
        ],
        out_specs=pl.BlockSpec((t, LANES), lambda j: (0, j)),
        out_shape=jax.ShapeDtypeStruct((t, CONV_W), BF16),
        compiler_params=_params("parallel"),
    )(proj, proj, proj, conv_w)


def _conv_bwd(dmix, proj, conv_w, *, name):
    t = proj.shape[0]

    def body(dy_ref, b_ref, c_ref, h_ref, w_ref, o_ref, dw_ref):
        row = lax.broadcasted_iota(jnp.int32, (t, CONV_W), 0)
        ac, ah = c_ref[...], h_ref[...]
        z = ac * ah
        w = w_ref[...]
        z1 = _shift_down(z, 1, row)
        z2 = _shift_down(z, 2, row)
        conv = w[2:3, :] * z + w[1:2, :] * z1 + w[0:1, :] * z2
        dy = dy_ref[...]
        o_ref[:, 0:CONV_W] = (dy * conv).astype(o_ref.dtype)
        dconv = dy * b_ref[...]
        dz = w[2:3, :] * dconv + w[1:2, :] * _shift_up(dconv, 1, row, t) + w[0:1, :] * _shift_up(dconv, 2, row, t)
        o_ref[:, CONV_W:2 * CONV_W] = (dz * ah).astype(o_ref.dtype)
        o_ref[:, 2 * CONV_W:3 * CONV_W] = (dz * ac).astype(o_ref.dtype)
        dw_ref[...] = jnp.zeros_like(dw_ref)
        dw_ref[0:1, :] = jnp.sum(dconv * z2, axis=0, keepdims=True)
        dw_ref[1:2, :] = jnp.sum(dconv * z1, axis=0, keepdims=True)
        dw_ref[2:3, :] = jnp.sum(dconv * z, axis=0, keepdims=True)

    col = lambda j: pl.BlockSpec((t, CONV_W), lambda i: (0, j))
    return pl.pallas_call(
        body,
        name=name,
        grid=(1,),
        in_specs=[col(0), col(0), col(1), col(2), pl.BlockSpec((CONV_TAPS, CONV_W), lambda i: (0, 0))],
        out_specs=[pl.BlockSpec((t, 3 * CONV_W), lambda i: (0, 0)), pl.BlockSpec((8, CONV_W), lambda i: (0, 0))],
        out_shape=[jax.ShapeDtypeStruct((t, 3 * CONV_W), BF16), jax.ShapeDtypeStruct((8, CONV_W), F32)],
        compiler_params=_params("arbitrary"),
    )(dmix, proj, proj, proj, conv_w)


_QK_BLOCK = 256


def _qk_prep(proj, gq, gk, *, name, br=512):
    t = proj.shape[0]
    br = min(br, t)
    nb = ATTN_W // _QK_BLOCK
    q0 = (3 * CONV_W) // _QK_BLOCK

    def body(q_ref, k_ref, v_ref, gq_ref, gk_ref, qo_ref, ko_ref, vo_ref):
        q = q_ref[...]
        k = k_ref[...]
        rq = lax.rsqrt(_group_mean(q * q, _QK_BLOCK) + EPS)
        rk = lax.rsqrt(_group_mean(k * k, _QK_BLOCK) + EPS)
        qo_ref[...] = ((q * rq * gq_ref[...]).astype(BF16) * QK_SCALE).astype(qo_ref.dtype)
        ko_ref[...] = (k * rk * gk_ref[...]).astype(ko_ref.dtype)
        vo_ref[...] = v_ref[...].astype(vo_ref.dtype)

    col = lambda off: pl.BlockSpec((br, _QK_BLOCK), lambda i, j: (i, off + j))
    vec = pl.BlockSpec((1, _QK_BLOCK), lambda i, j: (0, 0))
    return pl.pallas_call(
        body,
        name=name,
        grid=(t // br, nb),
        in_specs=[col(q0), col(q0 + nb), col(q0 + 2 * nb), vec, vec],
        out_specs=[col(0)] * 3,
        out_shape=[jax.ShapeDtypeStruct((t, ATTN_W), BF16)] * 3,
        compiler_params=_params("parallel", "parallel"),
    )(proj, proj, proj, gq, gk)


def _qk_prep_bwd(dqs, dkn, dv, proj, gq, gk, *, name, br=256):
    t = proj.shape[0]
    br = min(br, t)
    nb = ATTN_W // _QK_BLOCK
    q0 = (3 * CONV_W) // _QK_BLOCK

    def norm_bwd(dy, x, g):
        r = lax.rsqrt(_group_mean(x * x, ATTN_W) + EPS)
        xhat = x * r
        dxhat = dy * g
        dx = r * (dxhat - xhat * _group_mean(dxhat * xhat, ATTN_W))
        return dx, jnp.sum(dy * xhat, axis=0, keepdims=True)

    def body(dq_ref, dk_ref, dv_ref, *rest):
        x_refs, (gq_ref, gk_ref, o_ref, dgq_ref, dgk_ref) = rest[:2 * nb], rest[2 * nb:]
        whole = lambda refs: jnp.concatenate([r[...] for r in refs], axis=1)
        dq, dgq = norm_bwd(dq_ref[...] * QK_SCALE, whole(x_refs[:nb]), whole([gq_ref] * nb))
        dk, dgk = norm_bwd(dk_ref[...], whole(x_refs[nb:]), whole([gk_ref] * nb))
        o_ref[:, 0:ATTN_W] = dq.astype(o_ref.dtype)
        o_ref[:, ATTN_W:2 * ATTN_W] = dk.astype(o_ref.dtype)
        o_ref[:, 2 * ATTN_W:3 * ATTN_W] = dv_ref[...].astype(o_ref.dtype)

        @pl.when(pl.program_id(0) == 0)
        def _():
            dgq_ref[...] = jnp.zeros_like(dgq_ref)
            dgk_ref[...] = jnp.zeros_like(dgk_ref)

        dgq_ref[...] += dgq
        dgk_ref[...] += dgk

    rows = pl.BlockSpec((br, ATTN_W), lambda i: (i, 0))
    col = lambda j: pl.BlockSpec((br, _QK_BLOCK), lambda i: (i, j))
    gain = pl.BlockSpec((1, _QK_BLOCK), lambda i: (0, 0))
    total = pl.BlockSpec((1, ATTN_W), lambda i: (0, 0))
    return pl.pallas_call(
        body,
        name=name,
        grid=(t // br,),
        in_specs=[rows, rows, rows] + [col(q0 + j) for j in range(2 * nb)] + [gain, gain],
        out_specs=[pl.BlockSpec((br, 3 * ATTN_W), lambda i: (i, 0)), total, total],
        out_shape=[jax.ShapeDtypeStruct((t, 3 * ATTN_W), BF16)] + [jax.ShapeDtypeStruct((1, ATTN_W), F32)] * 2,
        compiler_params=_params("arbitrary"),
    )(dqs, dkn, dv, *[proj] * (2 * nb), gq, gk)


def _key_order_matrix(tb, relation):
    jj = lax.broadcasted_iota(jnp.int32, (tb, tb), 0)
    ss = lax.broadcasted_iota(jnp.int32, (tb, tb), 1)
    return relation(jj, ss).astype(BF16)


def _log_sigmoids(z):
    lb = jnp.minimum(z, 0.0) - jnp.log(1.0 + jnp.exp(-jnp.abs(z)))
    return lb, lb - z


def _below_diagonal(tb):
    return lax.broadcasted_iota(jnp.int32, (tb, tb), 1) < lax.broadcasted_iota(jnp.int32, (tb, tb), 0)


_NT = (((1,), (1,)), ((), ()))
_TN = (((0,), (0,)), ((), ()))
_ATTN_BLOCK = 256
_ATTN_FWD_UNROLL = 2
_ATTN_BWD_UNROLL = 3


def _attn_fwd(qs, kn, v, *, name, tb=_ATTN_BLOCK, unroll=_ATTN_FWD_UNROLL):
    t = qs.shape[0]
    tb = min(tb, t)
    assert t % tb == 0
    n_pairs = ATTN_W // LANES

    def body(q_ref, k_ref, v_ref, o_ref, lt_ref, acc_ref, carry_ref):
        qb = pl.program_id(1)
        half = lax.broadcasted_iota(jnp.int32, (1, LANES), 1) // HEAD_DIM
        later = _key_order_matrix(tb, lambda j, s: j > s)
        acc_ref[...] = jnp.zeros_like(acc_ref)
        carry_ref[...] = jnp.zeros_like(carry_ref)
        q = q_ref[...]
        qh = [jnp.where(half == h, q, jnp.zeros_like(q)) for h in range(2)]

        def tiles(kbs, diagonal):
            blk = []
            for kb in kbs:
                start = pl.multiple_of(kb * tb, tb)
                blk.append((k_ref[pl.ds(start, tb), :], v_ref[pl.ds(start, tb), :]))
            chains = [(h, j) for j in range(len(kbs)) for h in range(2)]
            z = [lax.dot_general(qh[h], blk[j][0], _NT, preferred_element_type=F32) for h, j in chains]
            causal = _below_diagonal(tb) if diagonal else None
            lb, lr = [], []
            for zi in z:
                b, r = _log_sigmoids(zi)
                lb.append(b)
                lr.append(jnp.where(causal, r, 0.0) if diagonal else r)
            suffix = [jnp.dot(r.astype(BF16), later, preferred_element_type=F32) for r in lr]
            carry = [carry_ref[0], carry_ref[1]]
            w = []
            for i, (h, j) in enumerate(chains):
                wi = jnp.exp(lb[i] + (suffix[i] + carry[h][:, 0:1]))
                w.append((jnp.where(causal, wi, 0.0) if diagonal else wi).astype(BF16))
                carry[h] = carry[h] + jnp.sum(lr[i], axis=1, keepdims=True)
            for i, (h, j) in enumerate(chains):
                vh = jnp.where(half == h, blk[j][1], jnp.zeros_like(blk[j][1]))
                acc_ref[h] += jnp.dot(w[i], vh, preferred_element_type=F32)
            carry_ref[0] = carry[0]
            carry_ref[1] = carry[1]

        tiles([qb], True)

        def step(i, _):
            kb = qb - 1 - unroll * i
            tiles([kb - u for u in range(unroll)], False)
            return 0

        lax.fori_loop(0, qb // unroll, step, 0)
        for left in range(1, unroll):

            @pl.when(qb % unroll == left)
            def _(left=left):
                tiles([left - 1 - u for u in range(left)], False)

        o_ref[...] = (acc_ref[0] + acc_ref[1]).astype(o_ref.dtype)
        lt_ref[...] = jnp.where(half == 0, carry_ref[0], carry_ref[1])

    return pl.pallas_call(
        body,
        name=name,
        grid=(n_pairs, t // tb),
        in_specs=[
            pl.BlockSpec((tb, LANES), lambda p, i: (i, p)),
            pl.BlockSpec((t, LANES), lambda p, i: (0, p)),
            pl.BlockSpec((t, LANES), lambda p, i: (0, p)),
        ],
        out_specs=[pl.BlockSpec((tb, LANES), lambda p, i: (i, p))] * 2,
        out_shape=[jax.ShapeDtypeStruct((t, ATTN_W), BF16), jax.ShapeDtypeStruct((t, ATTN_W), F32)],
        scratch_shapes=[pltpu.VMEM((2, tb, LANES), F32), pltpu.VMEM((2, tb, LANES), F32)],
        compiler_params=_params("parallel", "parallel"),
    )(qs, kn, v)


def _attn_bwd(dmix, qs, kn, v, lt, order_after, *, name, tb=_ATTN_BLOCK, unroll=_ATTN_BWD_UNROLL):
    t = qs.shape[0]
    tb = min(tb, t)
    assert t % tb == 0
    n_pairs = ATTN_W // LANES
    dy0 = CONV_W // LANES

    def body(do_ref, q_ref, k_ref, v_ref, lt_ref, order_ref, dq_ref, dk_ref, dv_ref, dqacc_ref, cc_ref, cg_ref):
        qb = pl.program_id(1)
        half = lax.broadcasted_iota(jnp.int32, (1, LANES), 1) // HEAD_DIM
        lane = lax.broadcasted_iota(jnp.int32, (tb, LANES), 1)
        later = _key_order_matrix(tb, lambda j, s: j > s)
        before = _key_order_matrix(tb, lambda j, s: j < s)
        q = q_ref[...]
        do = do_ref[...].astype(BF16)
        lt = lt_ref[...]
        qh = [jnp.where(half == h, q, jnp.zeros_like(q)) for h in range(2)]
        doh = [jnp.where(half == h, do, jnp.zeros_like(do)) for h in range(2)]
        lth = [jnp.sum(jnp.where(lane == h * HEAD_DIM, lt, 0.0), axis=1, keepdims=True) for h in range(2)]

        @pl.when(qb == 0)
        def _():
            dk_ref[...] = jnp.zeros_like(dk_ref)
            dv_ref[...] = jnp.zeros_like(dv_ref)

        dqacc_ref[...] = jnp.zeros_like(dqacc_ref)
        cc_ref[...] = jnp.zeros_like(cc_ref)
        cg_ref[...] = jnp.zeros_like(cg_ref)

        def tiles(kbs, diagonal):
            starts = [pl.multiple_of(kb * tb, tb) for kb in kbs]
            blk = [(k_ref[pl.ds(s, tb), :], v_ref[pl.ds(s, tb), :]) for s in starts]
            chains = [(h, j) for j in range(len(kbs)) for h in range(2)]
            z = [lax.dot_general(qh[h], blk[j][0], _NT, preferred_element_type=F32) for h, j in chains]
            da = [lax.dot_general(doh[h], jnp.where(half == h, blk[j][1], jnp.zeros_like(blk[j][1])), _NT,
                                  preferred_element_type=F32) for h, j in chains]
            causal = _below_diagonal(tb) if diagonal else None
            lb, lr = [], []
            for zi in z:
                b, r = _log_sigmoids(zi)
                lb.append(b)
                lr.append(jnp.where(causal, r, 0.0) if diagonal else r)
            suffix = [jnp.dot(r.astype(BF16), later, preferred_element_type=F32) for r in lr]
            cc = [cc_ref[0], cc_ref[1]]
            cg = [cg_ref[0], cg_ref[1]]
            a16, g = [], []
            for i, (h, j) in enumerate(chains):
                cc[h] = cc[h] + jnp.sum(lr[i], axis=1, keepdims=True)
                a = jnp.exp(lb[i] + suffix[i] + (lth[h] - cc[h][:, 0:1]))
                if diagonal:
                    a = jnp.where(causal, a, 0.0)
                a16.append(a.astype(BF16))
                g.append(da[i] * a)
            g_before = [jnp.dot(gi.astype(BF16), before, preferred_element_type=F32) for gi in g]
            dz = []
            for i, (h, j) in enumerate(chains):
                dzi = g[i] - jnp.exp(lb[i]) * (g[i] + (g_before[i] + cg[h][:, 0:1]))
                dz.append((jnp.where(causal, dzi, 0.0) if diagonal else dzi).astype(BF16))
                cg[h] = cg[h] + jnp.sum(g[i], axis=1, keepdims=True)
            for i, (h, j) in enumerate(chains):
                kh = jnp.where(half == h, blk[j][0], jnp.zeros_like(blk[j][0]))
                dqacc_ref[h] += jnp.dot(dz[i], kh, preferred_element_type=F32)
                dk_ref[pl.ds(starts[j], tb), :] += lax.dot_general(dz[i], qh[h], _TN, preferred_element_type=F32)
                dv_ref[pl.ds(starts[j], tb), :] += lax.dot_general(a16[i], doh[h], _TN, preferred_element_type=F32)
            for h in range(2):
                cc_ref[h] = cc[h]
                cg_ref[h] = cg[h]

        def step(i, _):
            kb = unroll * i
            tiles([kb + u for u in range(unroll)], False)
            return 0

        lax.fori_loop(0, qb // unroll, step, 0)
        for left in range(1, unroll):

            @pl.when(qb % unroll == left)
            def _(left=left):
                tiles([qb - left + u for u in range(left)], False)

        tiles([qb], True)
        dq_ref[...] = dqacc_ref[0] + dqacc_ref[1]

    qblk = pl.BlockSpec((tb, LANES), lambda p, i: (i, p))
    whole = pl.BlockSpec((t, LANES), lambda p, i: (0, p))
    return pl.pallas_call(
        body,
        name=name,
        grid=(n_pairs, t // tb),
        in_specs=[pl.BlockSpec((tb, LANES), lambda p, i: (i, dy0 + p)), qblk, whole, whole, qblk,
                  pl.BlockSpec(order_after.shape, lambda p, i: (0, 0))],
        out_specs=[qblk, whole, whole],
        out_shape=[jax.ShapeDtypeStruct((t, ATTN_W), F32)] * 3,
        scratch_shapes=[pltpu.VMEM((2, tb, LANES), F32)] * 3,
        compiler_params=_params("parallel", "arbitrary"),
    )(dmix, qs, kn, v, lt, order_after)


_SGU_CHUNKS_PER_STEP = 4


def _sgu_rows(t):
    return CHUNK * math.gcd(_SGU_CHUNKS_PER_STEP, t // CHUNK)


def _sgu_weights(w_ref):
    tt = lax.broadcasted_iota(jnp.int32, (CHUNK, CHUNK), 0)
    ss = lax.broadcasted_iota(jnp.int32, (CHUNK, CHUNK), 1)
    tril = ss <= tt
    return [jnp.where(tril, w_ref[gi], 0.0).astype(BF16) for gi in range(SGU_HEADS)], tril


def _sgu_fwd(proj, g_v, w_s, b_exp, *, name):
    t = proj.shape[0]
    u0 = (3 * CONV_W + 3 * ATTN_W) // SGU_W
    rows = _sgu_rows(t)

    def body(u_ref, v_ref, g_ref, w_ref, b_ref, o_ref):
        grp = lax.broadcasted_iota(jnp.int32, (1, SGU_W), 1) // HEAD_DIM
        wm, _ = _sgu_weights(w_ref)
        gain, bias = g_ref[...], b_ref[...]
        for c in range(rows // CHUNK):
            chunk = pl.ds(c * CHUNK, CHUNK)
            u = _gelu(u_ref[chunk, :])
            vv = _gelu(v_ref[chunk, :])
            vn = (vv * lax.rsqrt(_group_mean(vv * vv, SGU_W) + EPS) * gain).astype(BF16)
            sv = bias
            for gi in range(SGU_HEADS):
                sv = sv + jnp.dot(wm[gi], jnp.where(grp == gi, vn, jnp.zeros_like(vn)), preferred_element_type=F32)
            o_ref[chunk, :] = (u * sv).astype(o_ref.dtype)

    return pl.pallas_call(
        body,
        name=name,
        grid=(t // rows,),
        in_specs=[
            pl.BlockSpec((rows, SGU_W), lambda i: (i, u0)),
            pl.BlockSpec((rows, SGU_W), lambda i: (i, u0 + 1)),
            pl.BlockSpec((1, SGU_W), lambda i: (0, 0)),
            pl.BlockSpec((SGU_HEADS, CHUNK, CHUNK), lambda i: (0, 0, 0)),
            pl.BlockSpec((CHUNK, SGU_W), lambda i: (0, 0)),
        ],
        out_specs=pl.BlockSpec((rows, SGU_W), lambda i: (i, 0)),
        out_shape=jax.ShapeDtypeStruct((t, SGU_W), BF16),
        compiler_params=_params("parallel"),
    )(proj, proj, g_v, w_s, b_exp)


def _sgu_bwd(dmix, proj, g_v, w_s, b_exp, *, name):
    t = proj.shape[0]
    u0 = (3 * CONV_W + 3 * ATTN_W) // SGU_W
    dy0 = (CONV_W + ATTN_W) // SGU_W
    rows = _sgu_rows(t)

    def body(dy_ref, u_ref, v_ref, g_ref, w_ref, b_ref, o_ref, dg_ref, dw_ref, db_ref):
        grp = lax.broadcasted_iota(jnp.int32, (1, SGU_W), 1) // HEAD_DIM
        gain, bias = g_ref[...], b_ref[...]
        wm, tril = _sgu_weights(w_ref)

        @pl.when(pl.program_id(0) == 0)
        def _():
            dg_ref[...] = jnp.zeros_like(dg_ref)
            dw_ref[...] = jnp.zeros_like(dw_ref)
            db_ref[...] = jnp.zeros_like(db_ref)

        dg = jnp.zeros_like(gain)
        db = jnp.zeros_like(bias)
        dw = [jnp.zeros((CHUNK, CHUNK), F32) for _ in range(SGU_HEADS)]
        for c in range(rows // CHUNK):
            chunk = pl.ds(c * CHUNK, CHUNK)
            cu, cv = u_ref[chunk, :], v_ref[chunk, :]
            u = _gelu(cu)
            vv = _gelu(cv)
            r = lax.rsqrt(_group_mean(vv * vv, SGU_W) + EPS)
            xhat = vv * r
            vn = (xhat * gain).astype(BF16)
            vng = [jnp.where(grp == gi, vn, jnp.zeros_like(vn)) for gi in range(SGU_HEADS)]
            sv = bias
            for gi in range(SGU_HEADS):
                sv = sv + jnp.dot(wm[gi], vng[gi], preferred_element_type=F32)
            dy = dy_ref[chunk, :]
            o_ref[chunk, 0:SGU_W] = (dy * sv * _gelu_grad(cu)).astype(o_ref.dtype)
            dsv = dy * u
            dsv16 = dsv.astype(BF16)
            db = db + dsv
            dvn = jnp.zeros_like(dsv)
            for gi in range(SGU_HEADS):
                dw[gi] = dw[gi] + lax.dot_general(dsv16, vng[gi], _NT, preferred_element_type=F32)
                dvn_g = lax.dot_general(wm[gi], dsv16, _TN, preferred_element_type=F32)
                dvn = jnp.where(grp == gi, dvn_g, dvn)
            dg = dg + jnp.sum(dvn * xhat, axis=0, keepdims=True)
            dxhat = dvn * gain
            dvv = r * (dxhat - xhat * _group_mean(dxhat * xhat, SGU_W))
            o_ref[chunk, SGU_W:2 * SGU_W] = (dvv * _gelu_grad(cv)).astype(o_ref.dtype)
        dg_ref[...] += dg
        db_ref[...] += db
        for gi in range(SGU_HEADS):
            dw_ref[gi] += jnp.where(tril, dw[gi], 0.0)

    return pl.pallas_call(
        body,
        name=name,
        grid=(t // rows,),
        in_specs=[
            pl.BlockSpec((rows, SGU_W), lambda i: (i, dy0)),
            pl.BlockSpec((rows, SGU_W), lambda i: (i, u0)),
            pl.BlockSpec((rows, SGU_W), lambda i: (i, u0 + 1)),
            pl.BlockSpec((1, SGU_W), lambda i: (0, 0)),
            pl.BlockSpec((SGU_HEADS, CHUNK, CHUNK), lambda i: (0, 0, 0)),
            pl.BlockSpec((CHUNK, SGU_W), lambda i: (0, 0)),
        ],
        out_specs=[
            pl.BlockSpec((rows, 2 * SGU_W), lambda i: (i, 0)),
            pl.BlockSpec((1, SGU_W), lambda i: (0, 0)),
            pl.BlockSpec((SGU_HEADS, CHUNK, CHUNK), lambda i: (0, 0, 0)),
            pl.BlockSpec((CHUNK, SGU_W), lambda i: (0, 0)),
        ],
        out_shape=[
            jax.ShapeDtypeStruct((t, 2 * SGU_W), BF16),
            jax.ShapeDtypeStruct((1, SGU_W), F32),
            jax.ShapeDtypeStruct((SGU_HEADS, CHUNK, CHUNK), F32),
            jax.ShapeDtypeStruct((CHUNK, SGU_W), F32),
        ],
        compiler_params=_params("arbitrary"),
    )(dmix, proj, proj, g_v, w_s, b_exp)


def _ple_bwd(dh, gate, pp, order_after, *, name, br=512):
    t, d = dh.shape
    br = min(br, t)

    def body(dh_ref, g_ref, p_ref, order_ref, dpre_ref, dpp_ref):
        dhv, g = dh_ref[...], g_ref[...]
        dpre_ref[...] = (dhv * p_ref[...] * g * (1.0 - g)).astype(dpre_ref.dtype)
        dpp_ref[...] = (dhv * g).astype(dpp_ref.dtype)

    row = pl.BlockSpec((br, d), lambda i: (i, 0))
    return pl.pallas_call(
        body,
        name=name,
        grid=(t // br,),
        in_specs=[row] * 3 + [pl.BlockSpec(order_after.shape, lambda i: (0, 0))],
        out_specs=[row] * 2,
        out_shape=[jax.ShapeDtypeStruct((t, d), BF16)] * 2,
        compiler_params=_params("parallel"),
    )(dh, gate, pp, order_after)


def _loss_head(y, target, *, name, br=512):
    t, d = y.shape
    br = min(br, t)

    def body(y_ref, t_ref, dy_ref, loss_ref):
        err = y_ref[...] - t_ref[...]
        dy_ref[...] = err * (1.0 / d)

        @pl.when(pl.program_id(0) == 0)
        def _():
            loss_ref[...] = jnp.zeros_like(loss_ref)

        loss_ref[...] += 0.5 * jnp.sum(jnp.sum(err * err, axis=1, keepdims=True) * (1.0 / d), axis=0, keepdims=True)

    row = pl.BlockSpec((br, d), lambda i: (i, 0))
    return pl.pallas_call(
        body,
        name=name,
        grid=(t // br,),
        in_specs=[row, row],
        out_specs=[row, pl.BlockSpec((8, LANES), lambda i: (0, 0))],
        out_shape=[jax.ShapeDtypeStruct((t, d), F32), jax.ShapeDtypeStruct((8, LANES), F32)],
        compiler_params=_params("arbitrary"),
    )(y, target)


def _adamw_update(w, g, m, v):
    nm = ADAM_B1 * m + (1.0 - ADAM_B1) * g
    nv = ADAM_B2 * v + (1.0 - ADAM_B2) * (g * g)
    m_hat = nm / (1.0 - ADAM_B1 ** ADAM_STEP)
    v_hat = nv / (1.0 - ADAM_B2 ** ADAM_STEP)
    return -ADAM_LR * (m_hat / (jnp.sqrt(v_hat) + ADAM_EPS) + ADAM_WD * w), nm, nv


def _adamw(w, g, m, v, *, name, br=512):
    r, c = w.shape
    br = _row_block(r, br)

    def body(w_ref, g_ref, m_ref, v_ref, d_ref, nm_ref, nv_ref):
        d_ref[...], nm_ref[...], nv_ref[...] = _adamw_update(w_ref[...], g_ref[...], m_ref[...], v_ref[...])

    row = pl.BlockSpec((br, c), lambda i: (i, 0))
    return pl.pallas_call(
        body,
        name=name,
        grid=(r // br,),
        in_specs=[row] * 4,
        out_specs=[row] * 3,
        out_shape=[jax.ShapeDtypeStruct((r, c), F32)] * 3,
        compiler_params=_params("parallel"),
    )(w, g, m, v)


def _sum_slots(per_layer, *, name):
    counts = [len(arrays) for arrays in per_layer]
    flat = [a for arrays in per_layer for a in arrays]

    def body(*refs):
        ins, outs = refs[:len(flat)], refs[len(flat):]
        at = 0
        for o_ref, count in zip(outs, counts, strict=True):
            for li in range(count):
                acc = ins[at + li][0]
                for j in range(1, N_DEV):
                    acc = acc + ins[at + li][j]
                o_ref[li] = acc
            at += count

    return pl.pallas_call(
        body,
        name=name,
        out_shape=[jax.ShapeDtypeStruct((len(arrays), *arrays[0].shape[1:]), F32) for arrays in per_layer],
        compiler_params=pltpu.CompilerParams(vmem_limit_bytes=VMEM_LIMIT_BYTES),
    )(*flat)


_ADAMW_BLOCK_ELEMS = 192 * 1024


def _adamw_reduce(w, arrived, m, v, *, name):
    depth, r, c = w.shape
    br = _row_block(r, max(BF16_TILE_ROWS, _ADAMW_BLOCK_ELEMS // (-(-c // LANES) * LANES)))

    def body(w_ref, m_ref, v_ref, *rest):
        parts, (g_ref, d_ref, nm_ref, nv_ref) = rest[:depth], rest[depth:]
        for li in range(depth):

            @pl.when(pl.program_id(0) == li)
            def _(li=li):
                g = parts[li][0].astype(F32)
                for j in range(1, N_DEV):
                    g = g + parts[li][j].astype(F32)
                g_ref[...] = g
                d_ref[...], nm_ref[...], nv_ref[...] = _adamw_update(w_ref[...], g, m_ref[...], v_ref[...])

    cur = pl.BlockSpec((None, br, c), lambda l, i: (l, i, 0))
    slots = [pl.BlockSpec((N_DEV, br, c), lambda l, i, li=li: (0, jnp.where(l == li, i, 0), 0)) for li in range(depth)]
    return pl.pallas_call(
        body,
        name=name,
        grid=(depth, r // br),
        in_specs=[cur, cur, cur] + slots,
        out_specs=[cur] * 4,
        out_shape=[jax.ShapeDtypeStruct((depth, r, c), F32)] * 4,
        compiler_params=_params("arbitrary", "arbitrary"),
    )(w, m, v, *arrived)


def _my_place():
    return lax.axis_index("x"), lax.axis_index("y"), lax.axis_index("c")


def _flip(v, bit):
    return 1 - v if bit else v


def _slot_of(px, py, pc):
    return 4 * px + 2 * py + pc


_ANY = pl.BlockSpec(memory_space=pl.ANY)


_HBM = pl.BlockSpec(memory_space=pltpu.HBM)
_SEM = pl.BlockSpec(memory_space=pltpu.SEMAPHORE)
_DATAFLOW = pltpu.SideEffectType.DATAFLOW_SIDE_EFFECTING


_GATHER, _GATHER_COLUMNS, _SCATTER = "gather", "gather_columns", "scatter"


def _landing_shape(a, mode):
    if mode == _SCATTER:
        return a.shape
    if mode == _GATHER_COLUMNS:
        return (a.shape[0], N_DEV * a.shape[1])
    return (N_DEV, *a.shape)


_DIRECT, _NEAR, _RELAY = "direct", "near", "relay"
_OTHER_CHIPS = (2, 4, 6)
_SIBLING = 1


def _exchange_copies(src_refs, land_refs, send_sem, recv_sem, modes, hops=_DIRECT):
    mx, my, mc = _my_place()
    peer_of = lambda k: (_flip(mx, k & 4), _flip(my, k & 2), _flip(mc, k & 1))
    mine = _slot_of(mx, my, mc)

    def block(land, mode, slot):
        if mode == _GATHER_COLUMNS:
            n = land.shape[1] // N_DEV
            return land.at[:, pl.ds(pl.multiple_of(slot * n, LANES), n)]
        return land.at[slot]

    def remote_copy(src, dst, to):
        return pltpu.make_async_remote_copy(src_ref=src, dst_ref=dst, send_sem=send_sem, recv_sem=recv_sem,
                                            device_id=to, device_id_type=MESH)

    remote, local = [], []
    for src, land, mode in zip(src_refs, land_refs, modes, strict=True):
        if hops == _RELAY:
            assert mode != _SCATTER
            for k in _OTHER_CHIPS:
                came = block(land, mode, _slot_of(*peer_of(k)))
                remote.append(remote_copy(came, came, peer_of(_SIBLING)))
            continue
        dst = block(land, mode, mine)
        for k in ((_SIBLING,) + _OTHER_CHIPS if hops == _NEAR else range(1, N_DEV)):
            remote.append(remote_copy(src.at[_slot_of(*peer_of(k))] if mode == _SCATTER else src, dst, peer_of(k)))
        local.append(pltpu.make_async_copy(src.at[mine] if mode == _SCATTER else src, dst, recv_sem))
    return remote, local


def _wait_copies(remote, local):
    for cp in remote:
        cp.wait_send()
        cp.wait_recv()
    for cp in local:
        cp.wait()


def _exchange_start(groups, after, *, name, hops=_DIRECT):
    sizes = [len(srcs) for srcs, _ in groups]
    n, n_sems = sum(sizes), 2 * len(groups)
    srcs = [a for arrays, _ in groups for a in arrays]
    lands = [lax.empty(_landing_shape(a, mode), a.dtype)
             for arrays, modes in groups for a, mode in zip(arrays, modes, strict=True)]
    offsets = [sum(sizes[:g]) for g in range(len(groups))]

    def body(*refs):
        sems = refs[2 * n + 1:2 * n + 1 + n_sems]
        for g, (off, size, (_, modes)) in enumerate(zip(offsets, sizes, groups)):
            remote, local = _exchange_copies(refs[off:off + size], refs[n + off:n + off + size], sems[2 * g],
                                             sems[2 * g + 1], modes, hops)
            for cp in remote + local:
                cp.start()
        refs[-1][...] = jnp.zeros_like(refs[-1])

    thru = [pltpu.HBM(a.shape, a.dtype) for a in (*srcs, *lands)]
    out = pl.pallas_call(
        body,
        name=name,
        in_specs=[_HBM] * (2 * n) + [_ANY],
        out_specs=(*[_SEM] * n_sems, *[_HBM] * (2 * n), pl.BlockSpec(memory_space=pltpu.VMEM)),
        out_shape=(*[pltpu.SemaphoreType.DMA(())] * n_sems, *thru, jax.ShapeDtypeStruct((8, LANES), F32)),
        input_output_aliases={i: n_sems + i for i in range(2 * n)},
        compiler_params=pltpu.CompilerParams(has_side_effects=_DATAFLOW),
    )(*[pltpu.with_memory_space_constraint(a, pltpu.HBM) for a in (*srcs, *lands)], after)
    sems, arrays = out[:n_sems], out[n_sems:-1]
    started = [(sems[2 * g], sems[2 * g + 1], *arrays[off:off + size], *arrays[n + off:n + off + size])
               for g, (off, size) in enumerate(zip(offsets, sizes))]
    return started, out[-1]


def _exchange_relay(started, after, *, modes, regroup, name):
    send_sem, recv_sem, *thru = started
    n, n_sems = len(thru) // 2, 2 * len(regroup)

    def body(*refs):
        srcs, lands = refs[:n], refs[n:2 * n]
        _wait_copies(*_exchange_copies(srcs, lands, refs[2 * n], refs[2 * n + 1], modes, _NEAR))
        sems = refs[2 * n + 3:2 * n + 3 + n_sems]
        for g, members in enumerate(regroup):
            remote, _ = _exchange_copies([srcs[i] for i in members], [lands[i] for i in members], sems[2 * g],
                                         sems[2 * g + 1], [modes[i] for i in members], _RELAY)
            for cp in remote:
                cp.start()
        refs[-1][...] = jnp.zeros_like(refs[-1])

    out = pl.pallas_call(
        body,
        name=name,
        in_specs=[_HBM] * (2 * n) + [_SEM, _SEM, _ANY],
        out_specs=(*[_SEM] * n_sems, *[_HBM] * (2 * n), pl.BlockSpec(memory_space=pltpu.VMEM)),
        out_shape=(*[pltpu.SemaphoreType.DMA(())] * n_sems, *[pltpu.HBM(a.shape, a.dtype) for a in thru],
                   jax.ShapeDtypeStruct((8, LANES), F32)),
        input_output_aliases={i: n_sems + i for i in range(2 * n)},
        compiler_params=pltpu.CompilerParams(has_side_effects=_DATAFLOW),
    )(*thru, send_sem, recv_sem, after)
    sems, arrays = out[:n_sems], out[n_sems:-1]
    groups = [(sems[2 * g], sems[2 * g + 1], *[arrays[i] for i in members], *[arrays[n + i] for i in members])
              for g, members in enumerate(regroup)]
    return groups, out[-1]


def _exchange_wait(started, after, *, modes, name, hops=_DIRECT):
    send_sem, recv_sem, *thru = started
    n = len(thru) // 2

    def body(*refs):
        _wait_copies(*_exchange_copies(refs[:n], refs[n:2 * n], refs[2 * n], refs[2 * n + 1], modes, hops))

    out = pl.pallas_call(
        body,
        name=name,
        in_specs=[_HBM] * (2 * n) + [_SEM, _SEM, _ANY],
        out_specs=[_HBM] * (2 * n),
        out_shape=[pltpu.HBM(a.shape, a.dtype) for a in thru],
        input_output_aliases={i: i for i in range(2 * n)},
        compiler_params=pltpu.CompilerParams(has_side_effects=_DATAFLOW),
    )(*thru, send_sem, recv_sem, after)
    return out[n:]


def _gather_columns(g):
    return jnp.moveaxis(g, 0, 1).reshape(g.shape[1], -1)


def _split_rows(w):
    return w.reshape(N_DEV, w.shape[0] // N_DEV, w.shape[1])


_FIRST = ("w_in", "conv_w")
_REST = ("w_out", "w_ff1", "w_ff2", "w_ple_gate", "w_ple_proj")
_REST_GROUPS = (("w_out",), ("w_ff1",), ("w_ff2",), ("w_ple_gate", "w_ple_proj"))
_BIG = ("w_in",) + _REST
_GATHER_MODE = dict(w_in=_GATHER, conv_w=_GATHER, w_out=_GATHER, w_ff1=_GATHER_COLUMNS, w_ff2=_GATHER,
                    w_ple_gate=_GATHER, w_ple_proj=_GATHER_COLUMNS)
_RELAYOUT_AFTER_GATHER = ("conv_w",)
_SMALL = ("norm1_g", "q_norm_g", "k_norm_g", "sgu_norm_g", "sgu_w", "sgu_b", "norm2_g", "norm3_g")
_ORDER = ("norm1_g", "w_in", "conv_w", "q_norm_g", "k_norm_g", "sgu_norm_g", "sgu_w", "sgu_b", "w_out", "norm2_g",
          "w_ff1", "w_ff2", "norm3_g", "w_ple_gate", "w_ple_proj")


def _whole_matrices(names, landed):
    return {k: _gather_columns(g) if k in _RELAYOUT_AFTER_GATHER else g.reshape(-1, g.shape[-1])
            for k, g in zip(names, landed, strict=True)}


def _layer_forward(h0, hn1, p16, s, li, w_first, gathered):
    nm = lambda k: f"{k}_l{li}"
    t = h0.shape[0]
    w = dict(w_first)
    proj = _matmul(hn1, w["w_in"], name=nm("proj"), tb=True, bm=t, bn=256)
    y_a = _conv_fwd(proj, w["conv_w"], name=nm("conv"))
    qs, kn, v = _qk_prep(proj, s["gq"], s["gk"], name=nm("qkprep"))
    y_b, lt = _attn_fwd(qs, kn, v, name=nm("attn"))
    gathered["relay_rest"](y_b)
    y_c = _sgu_fwd(proj, s["sgu_norm_g"], s["sgu_w"], s["b_exp"], name=nm("sgu"))
    mix = jnp.concatenate([y_a, y_b, y_c], axis=1)
    w.update(gathered["fetch"](0, mix))
    h1 = _matmul(mix, w["w_out"], name=nm("out"), bm=t, bn=256, extras=(h0,), epilogue=lambda acc, r: (r + acc,))
    hn2 = _rms_fwd(h1, s["norm2_g"], name=nm("rms2"))
    w.update(gathered["fetch"](1, hn2))
    f = _matmul(hn2, w["w_ff1"], name=nm("ff1"), bm=t, bn=512, out_dtypes=(BF16,),
                epilogue=lambda acc: (jnp.square(jnp.maximum(acc, 0.0)),))
    w.update(gathered["fetch"](2, f))
    h2 = _matmul(f, w["w_ff2"], name=nm("ff2"), bm=1024, bn=512, extras=(h1,), epilogue=lambda acc, r: (r + acc,))
    gathered["relay_next"](h2)
    hn3 = _rms_fwd(h2, s["norm3_g"], name=nm("rms3"))
    w.update(gathered["fetch"](3, hn3))
    w_next = gathered["fetch_next"](hn3)
    pp = _matmul(p16, w["w_ple_proj"], name=nm("pleproj"), bm=t, bn=512)

    def gate_epilogue(acc, pp_blk, h_blk):
        gate = jax.nn.sigmoid(acc)
        return h_blk + gate * pp_blk, gate

    h3, gate = _matmul(hn3, w["w_ple_gate"], name=nm("plegate"), bm=t, bn=256, out_dtypes=(F32, F32),
                       extras=(pp, h2), epilogue=gate_epilogue)
    saved = dict(h0=h0, hn1=hn1, proj=proj, qs=qs, kn=kn, v=v, lt=lt, mix=mix, h1=h1, hn2=hn2, f=f, h2=h2,
                 hn3=hn3, pp=pp, gate=gate, p16=p16)
    return h3, w, w_next, saved


def _layer_backward(dh3, a, w, s, li, order_after, start_rest):
    nm = lambda k: f"{k}_bwd_l{li}"
    t = dh3.shape[0]
    dpre, dpp = _ple_bwd(dh3, a["gate"], a["pp"], order_after, name=nm("ple"))
    g_gate = _weight_grad(a["hn3"], [dpre], name=nm("dwgate"))
    g_proj = _weight_grad(a["p16"], [dpp], name=nm("dwproj"), column_shards=True)
    dh2, dh2_16, g_n3 = _matmul_rms_bwd([dpre], w["w_ple_gate"], a["h2"], s["norm3_g"], dh3, name=nm("dh2"))
    du = _matmul(dh2_16, w["w_ff2"], name=nm("du"), tb=True, bm=t, bn=512, out_dtypes=(BF16,), extras=(a["f"],),
                 epilogue=lambda acc, f: (acc * (2.0 * jnp.sqrt(f.astype(F32))),))
    g_ff2 = _weight_grad(a["f"], [dh2_16], name=nm("dwff2"))
    g_ff1 = _weight_grad(a["hn2"], [du], name=nm("dwff1"), column_shards=True)
    dh1, dh1_16, g_n2 = _matmul_rms_bwd([du], w["w_ff1"], a["h1"], s["norm2_g"], dh2, name=nm("dh1"))
    dmix = _matmul(dh1_16, w["w_out"], name=nm("dmix"), tb=True, bm=t, bn=256)
    g_out = _weight_grad(a["mix"], [dh1_16], name=nm("dwout"))
    started = start_rest(dict(w_out=_split_rows(g_out), w_ff1=g_ff1, w_ff2=_split_rows(g_ff2),
                              w_ple_gate=_split_rows(g_gate), w_ple_proj=g_proj), dmix)
    d_conv, g_conv = _conv_bwd(dmix, a["proj"], w["conv_w"], name=nm("conv"))
    dqs, dkn, dv = _attn_bwd(dmix, a["qs"], a["kn"], a["v"], a["lt"], started, name=nm("attn"))
    d_qkv, g_q, g_k = _qk_prep_bwd(dqs, dkn, dv, a["proj"], s["gq"], s["gk"], name=nm("qkprep"))
    d_sgu, g_sn, g_sw, g_sb = _sgu_bwd(dmix, a["proj"], s["sgu_norm_g"], s["sgu_w"], s["b_exp"], name=nm("sgu"))
    dproj = [d_conv, d_qkv, d_sgu]
    g_in = jnp.concatenate([_weight_grad(piece, [a["hn1"]], name=nm(f"dwin{i}")) for i, piece in enumerate(dproj)])
    dh0, _, g_n1 = _matmul_rms_bwd(dproj, w["w_in"], a["h0"], s["norm1_g"], dh1, name=nm("dh0"), w_is_k_by_d=True)
    small = dict(norm1_g=g_n1, norm2_g=g_n2, norm3_g=g_n3, q_norm_g=g_q, k_norm_g=g_k, sgu_norm_g=g_sn, sgu_w=g_sw,
                 sgu_b=g_sb, conv_w=g_conv)
    return dh0, _split_rows(g_in), small


def _small_gradients(raw, depth):
    return dict(
        norm1_g=raw["norm1_g"].reshape(depth, -1), norm2_g=raw["norm2_g"].reshape(depth, -1),
        norm3_g=raw["norm3_g"].reshape(depth, -1),
        q_norm_g=raw["q_norm_g"].reshape(depth, -1, HEAD_DIM).sum(1),
        k_norm_g=raw["k_norm_g"].reshape(depth, -1, HEAD_DIM).sum(1),
        sgu_norm_g=raw["sgu_norm_g"].reshape(depth, -1), sgu_w=raw["sgu_w"],
        sgu_b=jnp.swapaxes(raw["sgu_b"].reshape(depth, CHUNK, SGU_HEADS, HEAD_DIM).sum(-1), 1, 2),
        conv_w=raw["conv_w"][:, :CONV_TAPS],
    )


def kernel(x, p, norm1_g, w_in, conv_w, q_norm_g, k_norm_g, sgu_norm_g, sgu_w, sgu_b, w_out, norm2_g, w_ff1, w_ff2, norm3_g, w_ple_gate, w_ple_proj, loss_target, m_norm1_g, m_w_in, m_conv_w, m_q_norm_g, m_k_norm_g, m_sgu_norm_g, m_sgu_w, m_sgu_b, m_w_out, m_norm2_g, m_w_ff1, m_w_ff2, m_norm3_g, m_w_ple_gate, m_w_ple_proj, v_norm1_g, v_w_in, v_conv_w, v_q_norm_g, v_k_norm_g, v_sgu_norm_g, v_sgu_w, v_sgu_b, v_w_out, v_norm2_g, v_w_ff1, v_w_ff2, v_norm3_g, v_w_ple_gate, v_w_ple_proj):
    weights = dict(norm1_g=norm1_g, w_in=w_in, conv_w=conv_w, q_norm_g=q_norm_g, k_norm_g=k_norm_g,
                   sgu_norm_g=sgu_norm_g, sgu_w=sgu_w, sgu_b=sgu_b, w_out=w_out, norm2_g=norm2_g, w_ff1=w_ff1,
                   w_ff2=w_ff2, norm3_g=norm3_g, w_ple_gate=w_ple_gate, w_ple_proj=w_ple_proj)
    mom = dict(norm1_g=m_norm1_g, w_in=m_w_in, conv_w=m_conv_w, q_norm_g=m_q_norm_g, k_norm_g=m_k_norm_g,
               sgu_norm_g=m_sgu_norm_g, sgu_w=m_sgu_w, sgu_b=m_sgu_b, w_out=m_w_out, norm2_g=m_norm2_g, w_ff1=m_w_ff1,
               w_ff2=m_w_ff2, norm3_g=m_norm3_g, w_ple_gate=m_w_ple_gate, w_ple_proj=m_w_ple_proj)
    var = dict(norm1_g=v_norm1_g, w_in=v_w_in, conv_w=v_conv_w, q_norm_g=v_q_norm_g, k_norm_g=v_k_norm_g,
               sgu_norm_g=v_sgu_norm_g, sgu_w=v_sgu_w, sgu_b=v_sgu_b, w_out=v_w_out, norm2_g=v_norm2_g, w_ff1=v_w_ff1,
               w_ff2=v_w_ff2, norm3_g=v_norm3_g, w_ple_gate=v_w_ple_gate, w_ple_proj=v_w_ple_proj)
    for params in (weights, mom, var):
        params["w_in"] = jnp.swapaxes(params["w_in"], 1, 2)
    depth = norm1_g.shape[0]
    mx, my, mc = _my_place()
    me = _slot_of(mx, my, mc)

    gathers = []
    modes_of = lambda names: tuple(_GATHER_MODE[k] for k in names)
    token = x[0, :8, :LANES]
    for li in range(depth):
        groups = [([weights[k][li] if k == "conv_w" else weights[k][li].astype(BF16) for k in names], modes_of(names))
                  for names in (_FIRST, _REST)]
        started, token = _exchange_start(groups, token, name=f"gather_start_l{li}", hops=_NEAR)
        gathers.append(started)

    small = []
    for li in range(depth):
        small.append(dict(
            norm1_g=norm1_g[li][None], norm2_g=norm2_g[li][None], norm3_g=norm3_g[li][None],
            gq=jnp.tile(q_norm_g[li], _QK_BLOCK // HEAD_DIM)[None], gk=jnp.tile(k_norm_g[li], _QK_BLOCK // HEAD_DIM)[None],
            sgu_norm_g=sgu_norm_g[li][None], sgu_w=sgu_w[li], b_exp=jnp.repeat(sgu_b[li].T, HEAD_DIM, axis=1),
        ))
    small[0]["norm1_g"] = small[0]["norm1_g"] + token[0, 0]

    h = x[0]
    saved, full = [], []
    relayed_first, relayed_rest = [None] * depth, [None] * depth
    rest_members = [[_REST.index(k) for k in names] for names in _REST_GROUPS]

    def relay_first(li, after):
        if li < depth:
            (relayed_first[li],), _ = _exchange_relay(gathers[li][0], after, modes=modes_of(_FIRST),
                                                      regroup=[list(range(len(_FIRST)))], name=f"gather_first_relay_l{li}")

    def fetch_first(li, after):
        if li == depth:
            return None
        landed = _exchange_wait(relayed_first[li], after, modes=modes_of(_FIRST), hops=_RELAY,
                                name=f"gather_first_wait_l{li}")
        return _whole_matrices(_FIRST, landed)

    hn1 = _rms_fwd(h, small[0]["norm1_g"], name="rms1_l0")
    relay_first(0, hn1)
    w_first = fetch_first(0, hn1)
    for li in range(depth):

        def relay_rest(after, li=li):
            relayed_rest[li], _ = _exchange_relay(gathers[li][1], after, modes=modes_of(_REST), regroup=rest_members,
                                                  name=f"gather_rest_relay_l{li}")

        def fetch(g, after, li=li):
            landed = _exchange_wait(relayed_rest[li][g], after, modes=modes_of(_REST_GROUPS[g]), hops=_RELAY,
                                    name=f"gather_{_REST_GROUPS[g][0]}_wait_l{li}")
            return _whole_matrices(_REST_GROUPS[g], landed)

        gathered = dict(relay_rest=relay_rest, fetch=fetch, relay_next=functools.partial(relay_first, li + 1),
                        fetch_next=functools.partial(fetch_first, li + 1))
        if li > 0:
            hn1 = _rms_fwd(h, small[li]["norm1_g"], name=f"rms1_l{li}")
        h, w, w_first, acts = _layer_forward(h, hn1, p[li, 0].astype(BF16), small[li], li, w_first, gathered)
        full.append(w)
        saved.append(acts)
    dh, loss_tile = _loss_head(h, loss_target[0], name="loss_head")
    loss = lax.psum(loss_tile[0, 0], ("x", "y", "c"))

    small_names = _SMALL + ("conv_w",)
    scatter_first, scatter_rest = [None] * depth, [None] * depth
    first_modes, rest_modes = (_SCATTER,) + (_GATHER,) * len(small_names), (_SCATTER,) * len(_REST)
    token = loss_tile
    for li in reversed(range(depth)):

        def start_rest(parts, after, li=li):
            (scatter_rest[li],), started = _exchange_start([([parts[k] for k in _REST], rest_modes)], after,
                                                           name=f"scatter_rest_start_l{li}")
            return started

        dh, g_in, small_grads = _layer_backward(dh, saved[li], full[li], small[li], li, token, start_rest)
        (scatter_first[li],), token = _exchange_start(
            [([g_in] + [small_grads[k] for k in small_names], first_modes)], dh, name=f"scatter_first_start_l{li}")
    grad_x = dh[None]

    grads, delta, new_m, new_v = {}, {}, {}, {}
    arrived = {k: [None] * depth for k in _BIG}
    for li in reversed(range(depth)):
        landed = _exchange_wait(scatter_rest[li], token, modes=rest_modes, name=f"scatter_rest_wait_l{li}")
        for k, g in zip(_REST, landed, strict=True):
            arrived[k][li] = g
    for k in _REST:
        grads[k], delta[k], new_m[k], new_v[k] = _adamw_reduce(weights[k], arrived[k], mom[k], var[k], name=f"adamw_{k}")
    small_parts = {k: [None] * depth for k in small_names}
    updated = jnp.stack([delta[k][0, 0, :1] for k in _REST])
    for li in reversed(range(depth)):
        arrived["w_in"][li], *parts = _exchange_wait(scatter_first[li], updated, modes=first_modes,
                                                     name=f"scatter_first_wait_l{li}")
        for k, part in zip(small_names, parts, strict=True):
            small_parts[k][li] = part
    grads["w_in"], delta["w_in"], new_m["w_in"], new_v["w_in"] = _adamw_reduce(
        weights["w_in"], arrived["w_in"], mom["w_in"], var["w_in"], name="adamw_w_in")
    for results in (grads, delta, new_m, new_v):
        results["w_in"] = jnp.swapaxes(results["w_in"], 1, 2)
    sums = _sum_slots([small_parts[k] for k in small_names], name="sum_small_grads")
    grads.update(_small_gradients(dict(zip(small_names, sums)), depth))
    n_conv = conv_w.shape[2]
    grads["conv_w"] = lax.dynamic_slice_in_dim(grads["conv_w"], me * n_conv, n_conv, axis=2)
    for k in small_names:
        as_rows = lambda a: a.reshape(-1, a.shape[-1])
        outs = _adamw(as_rows(weights[k]), as_rows(grads[k]), as_rows(mom[k]), as_rows(var[k]), name=f"adamw_{k}")
        delta[k], new_m[k], new_v[k] = (o.reshape(weights[k].shape) for o in outs)

    return (loss, grad_x, *[grads[k] for k in _ORDER], *[delta[k] for k in _ORDER],
            *[new_m[k] for k in _ORDER], *[new_v[k] for k in _ORDER])
```

```python
import functools
import math

import jax
import jax.numpy as jnp
from jax import lax
from jax.experimental import pallas as pl
from jax.experimental.pallas import tpu as pltpu

F32 = jnp.float32
BF16 = jnp.bfloat16

N_DEV = 8
HEAD_DIM = 64
CONV_W = 256
ATTN_W = 512
SGU_W = 256
SGU_HEADS = 4
CHUNK = 128
CONV_TAPS = 3
EPS = 1e-6
QK_SCALE = HEAD_DIM ** -0.5

ADAM_LR = 0.001
ADAM_B1 = 0.9
ADAM_B2 = 0.999
ADAM_EPS = 1e-08
ADAM_WD = 0.01
ADAM_STEP = 10

LANES = 128
BF16_TILE_ROWS = 16
VMEM_LIMIT_BYTES = 56 * 1024 * 1024
MESH = pl.DeviceIdType.MESH


def _params(*sem):
    return pltpu.CompilerParams(dimension_semantics=sem, vmem_limit_bytes=VMEM_LIMIT_BYTES)


def _row_block(rows, cap):
    if rows <= cap:
        return rows
    return max(b for b in range(BF16_TILE_ROWS, cap + 1, BF16_TILE_ROWS) if rows % b == 0)


def _matmul(a, b, *, name, tb=False, bm=512, bn=512, out_dtypes=(F32,), epilogue=None, extras=(), row_vectors=()):
    m, k = a.shape
    n = b.shape[0] if tb else b.shape[1]
    assert k == (b.shape[1] if tb else b.shape[0])
    bm, bn = min(bm, m), min(bn, n)
    assert m % bm == 0 and n % bn == 0
    a_spec = pl.BlockSpec((bm, k), lambda i, j: (i, 0))
    b_spec = pl.BlockSpec((bn, k), lambda i, j: (j, 0)) if tb else pl.BlockSpec((k, bn), lambda i, j: (0, j))
    dims = (((1,), (1 if tb else 0,)), ((), ()))
    n_in = len(extras) + len(row_vectors)
    for e in extras:
        assert e.shape == (m, n), (e.shape, m, n)
    for e in row_vectors:
        assert e.shape == (1, n), (e.shape, n)

    def body(a_ref, b_ref, *rest):
        outs = rest[n_in:]
        acc = lax.dot_general(a_ref[...], b_ref[...], dims, preferred_element_type=F32)
        res = (acc,) if epilogue is None else epilogue(acc, *[e[...] for e in rest[:n_in]])
        for o_ref, r in zip(outs, res, strict=True):
            o_ref[...] = r.astype(o_ref.dtype)

    tile = pl.BlockSpec((bm, bn), lambda i, j: (i, j))
    vec = pl.BlockSpec((1, bn), lambda i, j: (0, j))
    out = pl.pallas_call(
        body,
        name=name,
        grid=(m // bm, n // bn),
        in_specs=[a_spec, b_spec] + [tile] * len(extras) + [vec] * len(row_vectors),
        out_specs=[tile] * len(out_dtypes),
        out_shape=[jax.ShapeDtypeStruct((m, n), d) for d in out_dtypes],
        compiler_params=_params("parallel", "parallel"),
    )(a, b, *extras, *row_vectors)
    return out[0] if len(out_dtypes) == 1 else out


def _rms_rows(x, g):
    return x * lax.rsqrt(jnp.mean(x * x, axis=-1, keepdims=True) + EPS) * g


_WEIGHT_GRAD_ACC_ELEMS = 1024 * 1024


def _weight_grad(x, dys, *, name, column_shards=False):
    t, m = x.shape
    n = sum(dy.shape[1] for dy in dys)
    bm = m if m <= 2 * LANES else min(m // 2, max(LANES, _WEIGHT_GRAD_ACC_ELEMS // n // LANES * LANES))
    assert m % bm == 0
    ns = n // N_DEV

    def body(x_ref, *rest):
        o_ref = rest[-1]
        xb = x_ref[...]
        acc = jnp.concatenate([lax.dot_general(xb, dy_ref[...], _TN, preferred_element_type=F32) for dy_ref in rest[:-1]],
                              axis=1)
        if column_shards:
            for s in range(N_DEV):
                o_ref[s] = acc[:, s * ns:(s + 1) * ns].astype(o_ref.dtype)
        else:
            o_ref[...] = acc.astype(o_ref.dtype)

    if column_shards:
        out_spec, out_dims = pl.BlockSpec((N_DEV, bm, ns), lambda i: (0, i, 0)), (N_DEV, m, ns)
    else:
        out_spec, out_dims = pl.BlockSpec((bm, n), lambda i: (i, 0)), (m, n)
    return pl.pallas_call(
        body,
        name=name,
        grid=(m // bm,),
        in_specs=[pl.BlockSpec((t, bm), lambda i: (0, i))] + [pl.BlockSpec(dy.shape, lambda i: (0, 0)) for dy in dys],
        out_specs=out_spec,
        out_shape=jax.ShapeDtypeStruct(out_dims, BF16),
        compiler_params=_params("parallel"),
    )(x, *dys)


def _rms_fwd(h, g, *, name, br=512):
    t, d = h.shape
    br = min(br, t)

    def body(h_ref, g_ref, o_ref):
        o_ref[...] = _rms_rows(h_ref[...], g_ref[...]).astype(o_ref.dtype)

    return pl.pallas_call(
        body,
        name=name,
        grid=(t // br,),
        in_specs=[pl.BlockSpec((br, d), lambda i: (i, 0)), pl.BlockSpec((1, d), lambda i: (0, 0))],
        out_specs=pl.BlockSpec((br, d), lambda i: (i, 0)),
        out_shape=jax.ShapeDtypeStruct((t, d), BF16),
        compiler_params=_params("parallel"),
    )(h, g)


def _matmul_rms_bwd(dzs, w, h, g, dres, *, name, w_is_k_by_d=False):
    t, d = h.shape
    widths = [dz.shape[1] for dz in dzs]
    k = sum(widths)
    assert w.shape == ((k, d) if w_is_k_by_d else (d, k))
    br = min(t, 512 if k <= d else 256)
    n_dz = len(dzs)

    def body(*refs):
        w_ref, h_ref, g_ref, dres_ref, dh_ref, dh16_ref, dg_ref = refs[n_dz:]
        x = h_ref[...]
        dyv, at = None, 0
        for dz_ref, width in zip(refs[:n_dz], widths):
            if w_is_k_by_d:
                part = jnp.dot(dz_ref[...], w_ref[at:at + width, :], preferred_element_type=F32)
            else:
                part = lax.dot_general(dz_ref[...], w_ref[:, at:at + width], _NT, preferred_element_type=F32)
            dyv = part if dyv is None else dyv + part
            at += width
        r = lax.rsqrt(jnp.mean(x * x, axis=-1, keepdims=True) + EPS)
        xhat = x * r
        dxhat = dyv * g_ref[...]
        dh = dres_ref[...] + r * (dxhat - xhat * jnp.mean(dxhat * xhat, axis=-1, keepdims=True))
        dh_ref[...] = dh
        dh16_ref[...] = dh.astype(dh16_ref.dtype)

        @pl.when(pl.program_id(0) == 0)
        def _():
            dg_ref[...] = jnp.zeros_like(dg_ref)

        dg_ref[...] += jnp.sum(dyv * xhat, axis=0, keepdims=True)

    row = pl.BlockSpec((br, d), lambda i: (i, 0))
    vec = pl.BlockSpec((1, d), lambda i: (0, 0))
    return pl.pallas_call(
        body,
        name=name,
        grid=(t // br,),
        in_specs=[pl.BlockSpec((br, width), lambda i: (i, 0)) for width in widths]
        + [pl.BlockSpec(w.shape, lambda i: (0, 0)), row, vec, row],
        out_specs=[row, row, vec],
        out_shape=[jax.ShapeDtypeStruct((t, d), F32), jax.ShapeDtypeStruct((t, d), BF16),
                   jax.ShapeDtypeStruct((1, d), F32)],
        compiler_params=_params("arbitrary"),
    )(*dzs, w, h, g, dres)


def _group_mean(x, width):
    grp = lax.broadcasted_iota(jnp.int32, x.shape, 1) // HEAD_DIM
    out = jnp.zeros_like(x)
    for gi in range(width // HEAD_DIM):
        m = grp == gi
        s = jnp.sum(jnp.where(m, x, 0.0), axis=1, keepdims=True)
        out = jnp.where(m, s, out)
    return out * (1.0 / HEAD_DIM)


def _gelu(x):
    return 0.5 * x * (1.0 + lax.erf(x * (2.0 ** -0.5)))


def _gelu_grad(x):
    cdf = 0.5 * (1.0 + lax.erf(x * (2.0 ** -0.5)))
    pdf = jnp.exp(-0.5 * x * x) * (1.0 / math.sqrt(2.0 * math.pi))
    return cdf + x * pdf


def _shift_down(z, s, row):
    return jnp.where(row >= s, pltpu.roll(z, s, 0), 0.0)


def _shift_up(z, s, row, t):
    return jnp.where(row < t - s, pltpu.roll(z, t - s, 0), 0.0)


def _conv_fwd(proj, conv_w, *, name):
    t = proj.shape[0]
    nb = CONV_W // LANES

    def body(b_ref, c_ref, h_ref, w_ref, o_ref):
        row = lax.broadcasted_iota(jnp.int32, (t, LANES), 0)
        z = c_ref[...] * h_ref[...]
        w = w_ref[...]
        conv = w[2:3, :] * z + w[1:2, :] * _shift_down(z, 1, row) + w[0:1, :] * _shift_down(z, 2, row)
        o_ref[...] = (b_ref[...] * conv).astype(o_ref.dtype)

    return pl.pallas_call(
        body,
        name=name,
        grid=(nb,),
        in_specs=[
            pl.BlockSpec((t, LANES), lambda j: (0, j)),
            pl.BlockSpec((t, LANES), lambda j: (0, nb + j)),
            pl.BlockSpec((t, LANES), lambda j: (0, 2 * nb + j)),
            pl.BlockSpec((CONV_TAPS, LANES), lambda j: (0, j)),
        ],
        out_specs=pl.BlockSpec((t, LANES), lambda j: (0, j)),
        out_shape=jax.ShapeDtypeStruct((t, CONV_W), BF16),
        compiler_params=_params("parallel"),
    )(proj, proj, proj, conv_w)


def _conv_bwd(dmix, proj, conv_w, *, name):
    t = proj.shape[0]

    def body(dy_ref, b_ref, c_ref, h_ref, w_ref, o_ref, dw_ref):
        row = lax.broadcasted_iota(jnp.int32, (t, CONV_W), 0)
        ac, ah = c_ref[...], h_ref[...]
        z = ac * ah
        w = w_ref[...]
        z1 = _shift_down(z, 1, row)
        z2 = _shift_down(z, 2, row)
        conv = w[2:3, :] * z + w[1:2, :] * z1 + w[0:1, :] * z2
        dy = dy_ref[...]
        o_ref[:, 0:CONV_W] = (dy * conv).astype(o_ref.dtype)
        dconv = dy * b_ref[...]
        dz = w[2:3, :] * dconv + w[1:2, :] * _shift_up(dconv, 1, row, t) + w[0:1, :] * _shift_up(dconv, 2, row, t)
        o_ref[:, CONV_W:2 * CONV_W] = (dz * ah).astype(o_ref.dtype)
        o_ref[:, 2 * CONV_W:3 * CONV_W] = (dz * ac).astype(o_ref.dtype)
        dw_ref[...] = jnp.zeros_like(dw_ref)
        dw_ref[0:1, :] = jnp.sum(dconv * z2, axis=0, keepdims=True)
        dw_ref[1:2, :] = jnp.sum(dconv * z1, axis=0, keepdims=True)
        dw_ref[2:3, :] = jnp.sum(dconv * z, axis=0, keepdims=True)

    col = lambda j: pl.BlockSpec((t, CONV_W), lambda i: (0, j))
    return pl.pallas_call(
        body,
        name=name,
        grid=(1,),
        in_specs=[col(0), col(0), col(1), col(2), pl.BlockSpec((CONV_TAPS, CONV_W), lambda i: (0, 0))],
        out_specs=[pl.BlockSpec((t, 3 * CONV_W), lambda i: (0, 0)), pl.BlockSpec((8, CONV_W), lambda i: (0, 0))],
        out_shape=[jax.ShapeDtypeStruct((t, 3 * CONV_W), BF16), jax.ShapeDtypeStruct((8, CONV_W), F32)],
        compiler_params=_params("arbitrary"),
    )(dmix, proj, proj, proj, conv_w)


_QK_BLOCK = 256


def _qk_prep(proj, gq, gk, *, name, br=512):
    t = proj.shape[0]
    br = min(br, t)
    nb = ATTN_W // _QK_BLOCK
    q0 = (3 * CONV_W) // _QK_BLOCK

    def body(q_ref, k_ref, v_ref, gq_ref, gk_ref, qo_ref, ko_ref, vo_ref):
        q = q_ref[...]
        k = k_ref[...]
        rq = lax.rsqrt(_group_mean(q * q, _QK_BLOCK) + EPS)
        rk = lax.rsqrt(_group_mean(k * k, _QK_BLOCK) + EPS)
        qo_ref[...] = ((q * rq * gq_ref[...]).astype(BF16) * QK_SCALE).astype(qo_ref.dtype)
        ko_ref[...] = (k * rk * gk_ref[...]).astype(ko_ref.dtype)
        vo_ref[...] = v_ref[...].astype(vo_ref.dtype)

    col = lambda off: pl.BlockSpec((br, _QK_BLOCK), lambda i, j: (i, off + j))
    vec = pl.BlockSpec((1, _QK_BLOCK), lambda i, j: (0, 0))
    return pl.pallas_call(
        body,
        name=name,
        grid=(t // br, nb),
        in_specs=[col(q0), col(q0 + nb), col(q0 + 2 * nb), vec, vec],
        out_specs=[col(0)] * 3,
        out_shape=[jax.ShapeDtypeStruct((t, ATTN_W), BF16)] * 3,
        compiler_params=_params("parallel", "parallel"),
    )(proj, proj, proj, gq, gk)


def _qk_prep_bwd(dqs, dkn, dv, proj, gq, gk, *, name, br=256):
    t = proj.shape[0]
    br = min(br, t)
    nb = ATTN_W // _QK_BLOCK
    q0 = (3 * CONV_W) // _QK_BLOCK

    def norm_bwd(dy, x, g):
        r = lax.rsqrt(_group_mean(x * x, ATTN_W) + EPS)
        xhat = x * r
        dxhat = dy * g
        dx = r * (dxhat - xhat * _group_mean(dxhat * xhat, ATTN_W))
        return dx, jnp.sum(dy * xhat, axis=0, keepdims=True)

    def body(dq_ref, dk_ref, dv_ref, *rest):
        x_refs, (gq_ref, gk_ref, o_ref, dgq_ref, dgk_ref) = rest[:2 * nb], rest[2 * nb:]
        whole = lambda refs: jnp.concatenate([r[...] for r in refs], axis=1)
        dq, dgq = norm_bwd(dq_ref[...] * QK_SCALE, whole(x_refs[:nb]), whole([gq_ref] * nb))
        dk, dgk = norm_bwd(dk_ref[...], whole(x_refs[nb:]), whole([gk_ref] * nb))
        o_ref[:, 0:ATTN_W] = dq.astype(o_ref.dtype)
        o_ref[:, ATTN_W:2 * ATTN_W] = dk.astype(o_ref.dtype)
        o_ref[:, 2 * ATTN_W:3 * ATTN_W] = dv_ref[...].astype(o_ref.dtype)

        @pl.when(pl.program_id(0) == 0)
        def _():
            dgq_ref[...] = jnp.zeros_like(dgq_ref)
            dgk_ref[...] = jnp.zeros_like(dgk_ref)

        dgq_ref[...] += dgq
        dgk_ref[...] += dgk

    rows = pl.BlockSpec((br, ATTN_W), lambda i: (i, 0))
    col = lambda j: pl.BlockSpec((br, _QK_BLOCK), lambda i: (i, j))
    gain = pl.BlockSpec((1, _QK_BLOCK), lambda i: (0, 0))
    total = pl.BlockSpec((1, ATTN_W), lambda i: (0, 0))
    return pl.pallas_call(
        body,
        name=name,
        grid=(t // br,),
        in_specs=[rows, rows, rows] + [col(q0 + j) for j in range(2 * nb)] + [gain, gain],
        out_specs=[pl.BlockSpec((br, 3 * ATTN_W), lambda i: (i, 0)), total, total],
        out_shape=[jax.ShapeDtypeStruct((t, 3 * ATTN_W), BF16)] + [jax.ShapeDtypeStruct((1, ATTN_W), F32)] * 2,
        compiler_params=_params("arbitrary"),
    )(dqs, dkn, dv, *[proj] * (2 * nb), gq, gk)


def _key_order_matrix(tb, relation):
    jj = lax.broadcasted_iota(jnp.int32, (tb, tb), 0)
    ss = lax.broadcasted_iota(jnp.int32, (tb, tb), 1)
    return relation(jj, ss).astype(BF16)


def _log_sigmoids(z):
    lb = jnp.minimum(z, 0.0) - jnp.log(1.0 + jnp.exp(-jnp.abs(z)))
    return lb, lb - z


def _below_diagonal(tb):
    return lax.broadcasted_iota(jnp.int32, (tb, tb), 1) < lax.broadcasted_iota(jnp.int32, (tb, tb), 0)


_NT = (((1,), (1,)), ((), ()))
_TN = (((0,), (0,)), ((), ()))
_ATTN_BLOCK = 256
_ATTN_FWD_UNROLL = 2
_ATTN_BWD_UNROLL = 3


def _attn_fwd(qs, kn, v, *, name, tb=_ATTN_BLOCK, unroll=_ATTN_FWD_UNROLL):
    t = qs.shape[0]
    tb = min(tb, t)
    assert t % tb == 0
    n_pairs = ATTN_W // LANES

    def body(q_ref, k_ref, v_ref, o_ref, lt_ref, acc_ref, carry_ref):
        qb = pl.program_id(1)
        half = lax.broadcasted_iota(jnp.int32, (1, LANES), 1) // HEAD_DIM
        later = _key_order_matrix(tb, lambda j, s: j > s)
        acc_ref[...] = jnp.zeros_like(acc_ref)
        carry_ref[...] = jnp.zeros_like(carry_ref)
        q = q_ref[...]
        qh = [jnp.where(half == h, q, jnp.zeros_like(q)) for h in range(2)]

        def tiles(kbs, diagonal):
            blk = []
            for kb in kbs:
                start = pl.multiple_of(kb * tb, tb)
                blk.append((k_ref[pl.ds(start, tb), :], v_ref[pl.ds(start, tb), :]))
            chains = [(h, j) for j in range(len(kbs)) for h in range(2)]
            z = [lax.dot_general(qh[h], blk[j][0], _NT, preferred_element_type=F32) for h, j in chains]
            causal = _below_diagonal(tb) if diagonal else None
            lb, lr = [], []
            for zi in z:
                b, r = _log_sigmoids(zi)
                lb.append(b)
                lr.append(jnp.where(causal, r, 0.0) if diagonal else r)
            suffix = [jnp.dot(r.astype(BF16), later, preferred_element_type=F32) for r in lr]
            carry = [carry_ref[0], carry_ref[1]]
            w = []
            for i, (h, j) in enumerate(chains):
                wi = jnp.exp(lb[i] + (suffix[i] + carry[h][:, 0:1]))
                w.append((jnp.where(causal, wi, 0.0) if diagonal else wi).astype(BF16))
                carry[h] = carry[h] + jnp.sum(lr[i], axis=1, keepdims=True)
            for i, (h, j) in enumerate(chains):
                vh = jnp.where(half == h, blk[j][1], jnp.zeros_like(blk[j][1]))
                acc_ref[h] += jnp.dot(w[i], vh, preferred_element_type=F32)
            carry_ref[0] = carry[0]
            carry_ref[1] = carry[1]

        tiles([qb], True)

        def step(i, _):
            kb = qb - 1 - unroll * i
            tiles([kb - u for u in range(unroll)], False)
            return 0

        lax.fori_loop(0, qb // unroll, step, 0)
        for left in range(1, unroll):

            @pl.when(qb % unroll == left)
            def _(left=left):
                tiles([left - 1 - u for u in range(left)], False)

        o_ref[...] = (acc_ref[0] + acc_ref[1]).astype(o_ref.dtype)
        lt_ref[...] = jnp.where(half == 0, carry_ref[0], carry_ref[1])

    return pl.pallas_call(
        body,
        name=name,
        grid=(n_pairs, t // tb),
        in_specs=[
            pl.BlockSpec((tb, LANES), lambda p, i: (i, p)),
            pl.BlockSpec((t, LANES), lambda p, i: (0, p)),
            pl.BlockSpec((t, LANES), lambda p, i: (0, p)),
        ],
        out_specs=[pl.BlockSpec((tb, LANES), lambda p, i: (i, p))] * 2,
        out_shape=[jax.ShapeDtypeStruct((t, ATTN_W), BF16), jax.ShapeDtypeStruct((t, ATTN_W), F32)],
        scratch_shapes=[pltpu.VMEM((2, tb, LANES), F32), pltpu.VMEM((2, tb, LANES), F32)],
        compiler_params=_params("parallel", "parallel"),
    )(qs, kn, v)


def _attn_bwd(dmix, qs, kn, v, lt, order_after, *, name, tb=_ATTN_BLOCK, unroll=_ATTN_BWD_UNROLL):
    t = qs.shape[0]
    tb = min(tb, t)
    assert t % tb == 0
    n_pairs = ATTN_W // LANES
    dy0 = CONV_W // LANES

    def body(do_ref, q_ref, k_ref, v_ref, lt_ref, order_ref, dq_ref, dk_ref, dv_ref, dqacc_ref, cc_ref, cg_ref):
        qb = pl.program_id(1)
        half = lax.broadcasted_iota(jnp.int32, (1, LANES), 1) // HEAD_DIM
        lane = lax.broadcasted_iota(jnp.int32, (tb, LANES), 1)
        later = _key_order_matrix(tb, lambda j, s: j > s)
        before = _key_order_matrix(tb, lambda j, s: j < s)
        q = q_ref[...]
        do = do_ref[...].astype(BF16)
        lt = lt_ref[...]
        qh = [jnp.where(half == h, q, jnp.zeros_like(q)) for h in range(2)]
        doh = [jnp.where(half == h, do, jnp.zeros_like(do)) for h in range(2)]
        lth = [jnp.sum(jnp.where(lane == h * HEAD_DIM, lt, 0.0), axis=1, keepdims=True) for h in range(2)]

        @pl.when(qb == 0)
        def _():
            dk_ref[...] = jnp.zeros_like(dk_ref)
            dv_ref[...] = jnp.zeros_like(dv_ref)

        dqacc_ref[...] = jnp.zeros_like(dqacc_ref)
        cc_ref[...] = jnp.zeros_like(cc_ref)
        cg_ref[...] = jnp.zeros_like(cg_ref)

        def tiles(kbs, diagonal):
            starts = [pl.multiple_of(kb * tb, tb) for kb in kbs]
            blk = [(k_ref[pl.ds(s, tb), :], v_ref[pl.ds(s, tb), :]) for s in starts]
            chains = [(h, j) for j in range(len(kbs)) for h in range(2)]
            z = [lax.dot_general(qh[h], blk[j][0], _NT, preferred_element_type=F32) for h, j in chains]
            da = [lax.dot_general(doh[h], jnp.where(half == h, blk[j][1], jnp.zeros_like(blk[j][1])), _NT,
                                  preferred_element_type=F32) for h, j in chains]
            causal = _below_diagonal(tb) if diagonal else None
            lb, lr = [], []
            for zi in z:
                b, r = _log_sigmoids(zi)
                lb.append(b)
                lr.append(jnp.where(causal, r, 0.0) if diagonal else r)
            suffix = [jnp.dot(r.astype(BF16), later, preferred_element_type=F32) for r in lr]
            cc = [cc_ref[0], cc_ref[1]]
            cg = [cg_ref[0], cg_ref[1]]
            a16, g = [], []
            for i, (h, j) in enumerate(chains):
                cc[h] = cc[h] + jnp.sum(lr[i], axis=1, keepdims=True)
                a = jnp.exp(lb[i] + suffix[i] + (lth[h] - cc[h][:, 0:1]))
                if diagonal:
                    a = jnp.where(causal, a, 0.0)
                a16.append(a.astype(BF16))
                g.append(da[i] * a)
            g_before = [jnp.dot(gi.astype(BF16), before, preferred_element_type=F32) for gi in g]
            dz = []
            for i, (h, j) in enumerate(chains):
                dzi = g[i] - jnp.exp(lb[i]) * (g[i] + (g_before[i] + cg[h][:, 0:1]))
                dz.append((jnp.where(causal, dzi, 0.0) if diagonal else dzi).astype(BF16))
                cg[h] = cg[h] + jnp.sum(g[i], axis=1, keepdims=True)
            for i, (h, j) in enumerate(chains):
                kh = jnp.where(half == h, blk[j][0], jnp.zeros_like(blk[j][0]))
                dqacc_ref[h] += jnp.dot(dz[i], kh, preferred_element_type=F32)
                dk_ref[pl.ds(starts[j], tb), :] += lax.dot_general(dz[i], qh[h], _TN, preferred_element_type=F32)
                dv_ref[pl.ds(starts[j], tb), :] += lax.dot_general(a16[i], doh[h], _TN, preferred_element_type=F32)
            for h in range(2):
                cc_ref[h] = cc[h]
                cg_ref[h] = cg[h]

        def step(i, _):
            kb = unroll * i
            tiles([kb + u for u in range(unroll)], False)
            return 0

        lax.fori_loop(0, qb // unroll, step, 0)
        for left in range(1, unroll):

            @pl.when(qb % unroll == left)
            def _(left=left):
                tiles([qb - left + u for u in range(left)], False)

        tiles([qb], True)
        dq_ref[...] = dqacc_ref[0] + dqacc_ref[1]

    qblk = pl.BlockSpec((tb, LANES), lambda p, i: (i, p))
    whole = pl.BlockSpec((t, LANES), lambda p, i: (0, p))
    return pl.pallas_call(
        body,
        name=name,
        grid=(n_pairs, t // tb),
        in_specs=[pl.BlockSpec((tb, LANES), lambda p, i: (i, dy0 + p)), qblk, whole, whole, qblk,
                  pl.BlockSpec(order_after.shape, lambda p, i: (0, 0))],
        out_specs=[qblk, whole, whole],
        out_shape=[jax.ShapeDtypeStruct((t, ATTN_W), F32)] * 3,
        scratch_shapes=[pltpu.VMEM((2, tb, LANES), F32)] * 3,
        compiler_params=_params("parallel", "arbitrary"),
    )(dmix, qs, kn, v, lt, order_after)


_SGU_CHUNKS_PER_STEP = 4


def _sgu_rows(t):
    return CHUNK * math.gcd(_SGU_CHUNKS_PER_STEP, t // CHUNK)


def _sgu_weights(w_ref):
    tt = lax.broadcasted_iota(jnp.int32, (CHUNK, CHUNK), 0)
    ss = lax.broadcasted_iota(jnp.int32, (CHUNK, CHUNK), 1)
    tril = ss <= tt
    return [jnp.where(tril, w_ref[gi], 0.0).astype(BF16) for gi in range(SGU_HEADS)], tril


def _sgu_fwd(proj, g_v, w_s, b_exp, *, name):
    t = proj.shape[0]
    u0 = (3 * CONV_W + 3 * ATTN_W) // SGU_W
    rows = _sgu_rows(t)

    def body(u_ref, v_ref, g_ref, w_ref, b_ref, o_ref):
        grp = lax.broadcasted_iota(jnp.int32, (1, SGU_W), 1) // HEAD_DIM
        wm, _ = _sgu_weights(w_ref)
        gain, bias = g_ref[...], b_ref[...]
        for c in range(rows // CHUNK):
            chunk = pl.ds(c * CHUNK, CHUNK)
            u = _gelu(u_ref[chunk, :])
            vv = _gelu(v_ref[chunk, :])
            vn = (vv * lax.rsqrt(_group_mean(vv * vv, SGU_W) + EPS) * gain).astype(BF16)
            sv = bias
            for gi in range(SGU_HEADS):
                sv = sv + jnp.dot(wm[gi], jnp.where(grp == gi, vn, jnp.zeros_like(vn)), preferred_element_type=F32)
            o_ref[chunk, :] = (u * sv).astype(o_ref.dtype)

    return pl.pallas_call(
        body,
        name=name,
        grid=(t // rows,),
        in_specs=[
            pl.BlockSpec((rows, SGU_W), lambda i: (i, u0)),
            pl.BlockSpec((rows, SGU_W), lambda i: (i, u0 + 1)),
            pl.BlockSpec((1, SGU_W), lambda i: (0, 0)),
            pl.BlockSpec((SGU_HEADS, CHUNK, CHUNK), lambda i: (0, 0, 0)),
            pl.BlockSpec((CHUNK, SGU_W), lambda i: (0, 0)),
        ],
        out_specs=pl.BlockSpec((rows, SGU_W), lambda i: (i, 0)),
        out_shape=jax.ShapeDtypeStruct((t, SGU_W), BF16),
        compiler_params=_params("parallel"),
    )(proj, proj, g_v, w_s, b_exp)


def _sgu_bwd(dmix, proj, g_v, w_s, b_exp, *, name):
    t = proj.shape[0]
    u0 = (3 * CONV_W + 3 * ATTN_W) // SGU_W
    dy0 = (CONV_W + ATTN_W) // SGU_W
    rows = _sgu_rows(t)

    def body(dy_ref, u_ref, v_ref, g_ref, w_ref, b_ref, o_ref, dg_ref, dw_ref, db_ref):
        grp = lax.broadcasted_iota(jnp.int32, (1, SGU_W), 1) // HEAD_DIM
        gain, bias = g_ref[...], b_ref[...]
        wm, tril = _sgu_weights(w_ref)

        @pl.when(pl.program_id(0) == 0)
        def _():
            dg_ref[...] = jnp.zeros_like(dg_ref)
            dw_ref[...] = jnp.zeros_like(dw_ref)
            db_ref[...] = jnp.zeros_like(db_ref)

        dg = jnp.zeros_like(gain)
        db = jnp.zeros_like(bias)
        dw = [jnp.zeros((CHUNK, CHUNK), F32) for _ in range(SGU_HEADS)]
        for c in range(rows // CHUNK):
            chunk = pl.ds(c * CHUNK, CHUNK)
            cu, cv = u_ref[chunk, :], v_ref[chunk, :]
            u = _gelu(cu)
            vv = _gelu(cv)
            r = lax.rsqrt(_group_mean(vv * vv, SGU_W) + EPS)
            xhat = vv * r
            vn = (xhat * gain).astype(BF16)
            vng = [jnp.where(grp == gi, vn, jnp.zeros_like(vn)) for gi in range(SGU_HEADS)]
            sv = bias
            for gi in range(SGU_HEADS):
                sv = sv + jnp.dot(wm[gi], vng[gi], preferred_element_type=F32)
            dy = dy_ref[chunk, :]
            o_ref[chunk, 0:SGU_W] = (dy * sv * _gelu_grad(cu)).astype(o_ref.dtype)
            dsv = dy * u
            dsv16 = dsv.astype(BF16)
            db = db + dsv
            dvn = jnp.zeros_like(dsv)
            for gi in range(SGU_HEADS):
                dw[gi] = dw[gi] + lax.dot_general(dsv16, vng[gi], _NT, preferred_element_type=F32)
                dvn_g = lax.dot_general(wm[gi], dsv16, _TN, preferred_element_type=F32)
                dvn = jnp.where(grp == gi, dvn_g, dvn)
            dg = dg + jnp.sum(dvn * xhat, axis=0, keepdims=True)
            dxhat = dvn * gain
            dvv = r * (dxhat - xhat * _group_mean(dxhat * xhat, SGU_W))
            o_ref[chunk, SGU_W:2 * SGU_W] = (dvv * _gelu_grad(cv)).astype(o_ref.dtype)
        dg_ref[...] += dg
        db_ref[...] += db
        for gi in range(SGU_HEADS):
            dw_ref[gi] += jnp.where(tril, dw[gi], 0.0)

    return pl.pallas_call(
        body,
        name=name,
        grid=(t // rows,),
        in_specs=[
            pl.BlockSpec((rows, SGU_W), lambda i: (i, dy0)),
            pl.BlockSpec((rows, SGU_W), lambda i: (i, u0)),
            pl.BlockSpec((rows, SGU_W), lambda i: (i, u0 + 1)),
            pl.BlockSpec((1, SGU_W), lambda i: (0, 0)),
            pl.BlockSpec((SGU_HEADS, CHUNK, CHUNK), lambda i: (0, 0, 0)),
            pl.BlockSpec((CHUNK, SGU_W), lambda i: (0, 0)),
        ],
        out_specs=[
            pl.BlockSpec((rows, 2 * SGU_W), lambda i: (i, 0)),
            pl.BlockSpec((1, SGU_W), lambda i: (0, 0)),
            pl.BlockSpec((SGU_HEADS, CHUNK, CHUNK), lambda i: (0, 0, 0)),
            pl.BlockSpec((CHUNK, SGU_W), lambda i: (0, 0)),
        ],
        out_shape=[
            jax.ShapeDtypeStruct((t, 2 * SGU_W), BF16),
            jax.ShapeDtypeStruct((1, SGU_W), F32),
            jax.ShapeDtypeStruct((SGU_HEADS, CHUNK, CHUNK), F32),
            jax.ShapeDtypeStruct((CHUNK, SGU_W), F32),
        ],
        compiler_params=_params("arbitrary"),
    )(dmix, proj, proj, g_v, w_s, b_exp)


def _ple_bwd(dh, gate, pp, order_after, *, name, br=512):
    t, d = dh.shape
    br = min(br, t)

    def body(dh_ref, g_ref, p_ref, order_ref, dpre_ref, dpp_ref):
        dhv, g = dh_ref[...], g_ref[...]
        dpre_ref[...] = (dhv * p_ref[...] * g * (1.0 - g)).astype(dpre_ref.dtype)
        dpp_ref[...] = (dhv * g).astype(dpp_ref.dtype)

    row = pl.BlockSpec((br, d), lambda i: (i, 0))
    return pl.pallas_call(
        body,
        name=name,
        grid=(t // br,),
        in_specs=[row] * 3 + [pl.BlockSpec(order_after.shape, lambda i: (0, 0))],
        out_specs=[row] * 2,
        out_shape=[jax.ShapeDtypeStruct((t, d), BF16)] * 2,
        compiler_params=_params("parallel"),
    )(dh, gate, pp, order_after)


def _loss_head(y, target, *, name, br=512):
    t, d = y.shape
    br = min(br, t)

    def body(y_ref, t_ref, dy_ref, loss_ref):
        err = y_ref[...] - t_ref[...]
        dy_ref[...] = err * (1.0 / d)

        @pl.when(pl.program_id(0) == 0)
        def _():
            loss_ref[...] = jnp.zeros_like(loss_ref)

        loss_ref[...] += 0.5 * jnp.sum(jnp.sum(err * err, axis=1, keepdims=True) * (1.0 / d), axis=0, keepdims=True)

    row = pl.BlockSpec((br, d), lambda i: (i, 0))
    return pl.pallas_call(
        body,
        name=name,
        grid=(t // br,),
        in_specs=[row, row],
        out_specs=[row, pl.BlockSpec((8, LANES), lambda i: (0, 0))],
        out_shape=[jax.ShapeDtypeStruct((t, d), F32), jax.ShapeDtypeStruct((8, LANES), F32)],
        compiler_params=_params("arbitrary"),
    )(y, target)


def _adamw_update(w, g, m, v):
    nm = ADAM_B1 * m + (1.0 - ADAM_B1) * g
    nv = ADAM_B2 * v + (1.0 - ADAM_B2) * (g * g)
    m_hat = nm / (1.0 - ADAM_B1 ** ADAM_STEP)
    v_hat = nv / (1.0 - ADAM_B2 ** ADAM_STEP)
    return -ADAM_LR * (m_hat / (jnp.sqrt(v_hat) + ADAM_EPS) + ADAM_WD * w), nm, nv


def _adamw(w, g, m, v, *, name, br=512):
    r, c = w.shape
    br = _row_block(r, br)

    def body(w_ref, g_ref, m_ref, v_ref, d_ref, nm_ref, nv_ref):
        d_ref[...], nm_ref[...], nv_ref[...] = _adamw_update(w_ref[...], g_ref[...], m_ref[...], v_ref[...])

    row = pl.BlockSpec((br, c), lambda i: (i, 0))
    return pl.pallas_call(
        body,
        name=name,
        grid=(r // br,),
        in_specs=[row] * 4,
        out_specs=[row] * 3,
        out_shape=[jax.ShapeDtypeStruct((r, c), F32)] * 3,
        compiler_params=_params("parallel"),
    )(w, g, m, v)


def _sum_slots(per_layer, *, name):
    counts = [len(arrays) for arrays in per_layer]
    flat = [a for arrays in per_layer for a in arrays]

    def body(*refs):
        ins, outs = refs[:len(flat)], refs[len(flat):]
        at = 0
        for o_ref, count in zip(outs, counts, strict=True):
            for li in range(count):
                acc = ins[at + li][0]
                for j in range(1, N_DEV):
                    acc = acc + ins[at + li][j]
                o_ref[li] = acc
            at += count

    return pl.pallas_call(
        body,
        name=name,
        out_shape=[jax.ShapeDtypeStruct((len(arrays), *arrays[0].shape[1:]), F32) for arrays in per_layer],
        compiler_params=pltpu.CompilerParams(vmem_limit_bytes=VMEM_LIMIT_BYTES),
    )(*flat)


_ADAMW_BLOCK_ELEMS = 192 * 1024


def _adamw_reduce(w, arrived, m, v, *, name):
    depth, r, c = w.shape
    br = _row_block(r, max(BF16_TILE_ROWS, _ADAMW_BLOCK_ELEMS // (-(-c // LANES) * LANES)))

    def body(w_ref, m_ref, v_ref, *rest):
        parts, (g_ref, d_ref, nm_ref, nv_ref) = rest[:depth], rest[depth:]
        for li in range(depth):

            @pl.when(pl.program_id(0) == li)
            def _(li=li):
                g = parts[li][0].astype(F32)
                for j in range(1, N_DEV):
                    g = g + parts[li][j].astype(F32)
                g_ref[...] = g
                d_ref[...], nm_ref[...], nv_ref[...] = _adamw_update(w_ref[...], g, m_ref[...], v_ref[...])

    cur = pl.BlockSpec((None, br, c), lambda l, i: (l, i, 0))
    slots = [pl.BlockSpec((N_DEV, br, c), lambda l, i, li=li: (0, jnp.where(l == li, i, 0), 0)) for li in range(depth)]
    return pl.pallas_call(
        body,
        name=name,
        grid=(depth, r // br),
        in_specs=[cur, cur, cur] + slots,
        out_specs=[cur] * 4,
        out_shape=[jax.ShapeDtypeStruct((depth, r, c), F32)] * 4,
        compiler_params=_params("arbitrary", "arbitrary"),
    )(w, m, v, *arrived)


def _my_place():
    return lax.axis_index("x"), lax.axis_index("y"), lax.axis_index("c")


def _flip(v, bit):
    return 1 - v if bit else v


def _slot_of(px, py, pc):
    return 4 * px + 2 * py + pc


_ANY = pl.BlockSpec(memory_space=pl.ANY)


_HBM = pl.BlockSpec(memory_space=pltpu.HBM)
_SEM = pl.BlockSpec(memory_space=pltpu.SEMAPHORE)
_DATAFLOW = pltpu.SideEffectType.DATAFLOW_SIDE_EFFECTING


_GATHER, _GATHER_COLUMNS, _SCATTER = "gather", "gather_columns", "scatter"


def _landing_shape(a, mode):
    if mode == _SCATTER:
        return a.shape
    if mode == _GATHER_COLUMNS:
        return (a.shape[0], N_DEV * a.shape[1])
    return (N_DEV, *a.shape)


_DIRECT, _NEAR, _RELAY = "direct", "near", "relay"
_OTHER_CHIPS = (2, 4, 6)
_SIBLING = 1


def _exchange_copies(src_refs, land_refs, send_sem, recv_sem, modes, hops=_DIRECT):
    mx, my, mc = _my_place()
    peer_of = lambda k: (_flip(mx, k & 4), _flip(my, k & 2), _flip(mc, k & 1))
    mine = _slot_of(mx, my, mc)

    def block(land, mode, slot):
        if mode == _GATHER_COLUMNS:
            n = land.shape[1] // N_DEV
            return land.at[:, pl.ds(pl.multiple_of(slot * n, LANES), n)]
        return land.at[slot]

    def remote_copy(src, dst, to):
        return pltpu.make_async_remote_copy(src_ref=src, dst_ref=dst, send_sem=send_sem, recv_sem=recv_sem,
                                            device_id=to, device_id_type=MESH)

    remote, local = [], []
    for src, land, mode in zip(src_refs, land_refs, modes, strict=True):
        if hops == _RELAY:
            assert mode != _SCATTER
            for k in _OTHER_CHIPS:
                came = block(land, mode, _slot_of(*peer_of(k)))
                remote.append(remote_copy(came, came, peer_of(_SIBLING)))
            continue
        dst = block(land, mode, mine)
        for k in ((_SIBLING,) + _OTHER_CHIPS if hops == _NEAR else range(1, N_DEV)):
            remote.append(remote_copy(src.at[_slot_of(*peer_of(k))] if mode == _SCATTER else src, dst, peer_of(k)))
        local.append(pltpu.make_async_copy(src.at[mine] if mode == _SCATTER else src, dst, recv_sem))
    return remote, local


def _wait_copies(remote, local):
    for cp in remote:
        cp.wait_send()
        cp.wait_recv()
    for cp in local:
        cp.wait()


def _exchange_start(groups, after, *, name, hops=_DIRECT):
    sizes = [len(srcs) for srcs, _ in groups]
    n, n_sems = sum(sizes), 2 * len(groups)
    srcs = [a for arrays, _ in groups for a in arrays]
    lands = [lax.empty(_landing_shape(a, mode), a.dtype)
             for arrays, modes in groups for a, mode in zip(arrays, modes, strict=True)]
    offsets = [sum(sizes[:g]) for g in range(len(groups))]

    def body(*refs):
        sems = refs[2 * n + 1:2 * n + 1 + n_sems]
        for g, (off, size, (_, modes)) in enumerate(zip(offsets, sizes, groups)):
            remote, local = _exchange_copies(refs[off:off + size], refs[n + off:n + off + size], sems[2 * g],
                                             sems[2 * g + 1], modes, hops)
            for cp in remote + local:
                cp.start()
        refs[-1][...] = jnp.zeros_like(refs[-1])

    thru = [pltpu.HBM(a.shape, a.dtype) for a in (*srcs, *lands)]
    out = pl.pallas_call(
        body,
        name=name,
        in_specs=[_HBM] * (2 * n) + [_ANY],
        out_specs=(*[_SEM] * n_sems, *[_HBM] * (2 * n), pl.BlockSpec(memory_space=pltpu.VMEM)),
        out_shape=(*[pltpu.SemaphoreType.DMA(())] * n_sems, *thru, jax.ShapeDtypeStruct((8, LANES), F32)),
        input_output_aliases={i: n_sems + i for i in range(2 * n)},
        compiler_params=pltpu.CompilerParams(has_side_effects=_DATAFLOW),
    )(*[pltpu.with_memory_space_constraint(a, pltpu.HBM) for a in (*srcs, *lands)], after)
    sems, arrays = out[:n_sems], out[n_sems:-1]
    started = [(sems[2 * g], sems[2 * g + 1], *arrays[off:off + size], *arrays[n + off:n + off + size])
               for g, (off, size) in enumerate(zip(offsets, sizes))]
    return started, out[-1]


def _exchange_relay(started, after, *, modes, regroup, name):
    send_sem, recv_sem, *thru = started
    n, n_sems = len(thru) // 2, 2 * len(regroup)

    def body(*refs):
        srcs, lands = refs[:n], refs[n:2 * n]
        _wait_copies(*_exchange_copies(srcs, lands, refs[2 * n], refs[2 * n + 1], modes, _NEAR))
        sems = refs[2 * n + 3:2 * n + 3 + n_sems]
        for g, members in enumerate(regroup):
            remote, _ = _exchange_copies([srcs[i] for i in members], [lands[i] for i in members], sems[2 * g],
                                         sems[2 * g + 1], [modes[i] for i in members], _RELAY)
            for cp in remote:
                cp.start()
        refs[-1][...] = jnp.zeros_like(refs[-1])

    out = pl.pallas_call(
        body,
        name=name,
        in_specs=[_HBM] * (2 * n) + [_SEM, _SEM, _ANY],
        out_specs=(*[_SEM] * n_sems, *[_HBM] * (2 * n), pl.BlockSpec(memory_space=pltpu.VMEM)),
        out_shape=(*[pltpu.SemaphoreType.DMA(())] * n_sems, *[pltpu.HBM(a.shape, a.dtype) for a in thru],
                   jax.ShapeDtypeStruct((8, LANES), F32)),
        input_output_aliases={i: n_sems + i for i in range(2 * n)},
        compiler_params=pltpu.CompilerParams(has_side_effects=_DATAFLOW),
    )(*thru, send_sem, recv_sem, after)
    sems, arrays = out[:n_sems], out[n_sems:-1]
    groups = [(sems[2 * g], sems[2 * g + 1], *[arrays[i] for i in members], *[arrays[n + i] for i in members])
              for g, members in enumerate(regroup)]
    return groups, out[-1]


def _exchange_wait(started, after, *, modes, name, hops=_DIRECT):
    send_sem, recv_sem, *thru = started
    n = len(thru) // 2

    def body(*refs):
        _wait_copies(*_exchange_copies(refs[:n], refs[n:2 * n], refs[2 * n], refs[2 * n + 1], modes, hops))

    out = pl.pallas_call(
        body,
        name=name,
        in_specs=[_HBM] * (2 * n) + [_SEM, _SEM, _ANY],
        out_specs=[_HBM] * (2 * n),
        out_shape=[pltpu.HBM(a.shape, a.dtype) for a in thru],
        input_output_aliases={i: i for i in range(2 * n)},
        compiler_params=pltpu.CompilerParams(has_side_effects=_DATAFLOW),
    )(*thru, send_sem, recv_sem, after)
    return out[n:]


def _gather_columns(g):
    return jnp.moveaxis(g, 0, 1).reshape(g.shape[1], -1)


def _split_rows(w):
    return w.reshape(N_DEV, w.shape[0] // N_DEV, w.shape[1])


_FIRST = ("w_in", "conv_w")
_REST = ("w_out", "w_ff1", "w_ff2", "w_ple_gate", "w_ple_proj")
_REST_GROUPS = (("w_out",), ("w_ff1",), ("w_ff2",), ("w_ple_gate", "w_ple_proj"))
_BIG = ("w_in",) + _REST
_GATHER_MODE = dict(w_in=_GATHER, conv_w=_GATHER, w_out=_GATHER, w_ff1=_GATHER_COLUMNS, w_ff2=_GATHER,
                    w_ple_gate=_GATHER, w_ple_proj=_GATHER_COLUMNS)
_RELAYOUT_AFTER_GATHER = ("conv_w",)
_SMALL = ("norm1_g", "q_norm_g", "k_norm_g", "sgu_norm_g", "sgu_w", "sgu_b", "norm2_g", "norm3_g")
_ORDER = ("norm1_g", "w_in", "conv_w", "q_norm_g", "k_norm_g", "sgu_norm_g", "sgu_w", "sgu_b", "w_out", "norm2_g",
          "w_ff1", "w_ff2", "norm3_g", "w_ple_gate", "w_ple_proj")


def _whole_matrices(names, landed):
    return {k: _gather_columns(g) if k in _RELAYOUT_AFTER_GATHER else g.reshape(-1, g.shape[-1])
            for k, g in zip(names, landed, strict=True)}


_NORM_FUSED_ROWS = 256


def _layer_forward(h0, hn1, p16, s, li, w_first, gathered, next_norm_g):
    nm = lambda k: f"{k}_l{li}"
    t, d = h0.shape
    w = dict(w_first)

    def add_and_norm(acc, res, gain):
        h = res + acc
        return h, _rms_rows(h, gain)

    proj = _matmul(hn1, w["w_in"], name=nm("proj"), tb=True, bm=t, bn=256)
    y_a = _conv_fwd(proj, w["conv_w"], name=nm("conv"))
    qs, kn, v = _qk_prep(proj, s["gq"], s["gk"], name=nm("qkprep"))
    y_b, lt = _attn_fwd(qs, kn, v, name=nm("attn"))
    gathered["relay_rest"](y_b)
    y_c = _sgu_fwd(proj, s["sgu_norm_g"], s["sgu_w"], s["b_exp"], name=nm("sgu"))
    mix = jnp.concatenate([y_a, y_b, y_c], axis=1)
    w.update(gathered["fetch"](0, mix))
    h1, hn2 = _matmul(mix, w["w_out"], name=nm("out"), bm=_NORM_FUSED_ROWS, bn=d, out_dtypes=(F32, BF16),
                      extras=(h0,), row_vectors=(s["norm2_g"],), epilogue=add_and_norm)
    w.update(gathered["fetch"](1, hn2))
    f = _matmul(hn2, w["w_ff1"], name=nm("ff1"), bm=t, bn=512, out_dtypes=(BF16,),
                epilogue=lambda acc: (jnp.square(jnp.maximum(acc, 0.0)),))
    w.update(gathered["fetch"](2, f))
    gathered["relay_next"](f)
    h2, hn3 = _matmul(f, w["w_ff2"], name=nm("ff2"), bm=_NORM_FUSED_ROWS, bn=d, out_dtypes=(F32, BF16),
                      extras=(h1,), row_vectors=(s["norm3_g"],), epilogue=add_and_norm)
    w.update(gathered["fetch"](3, hn3))
    w_next = gathered["fetch_next"](hn3)
    pp = _matmul(p16, w["w_ple_proj"], name=nm("pleproj"), bm=t, bn=512)

    def gate_epilogue(acc, pp_blk, h_blk, *gain):
        gate = jax.nn.sigmoid(acc)
        h = h_blk + gate * pp_blk
        return (h, gate) + tuple(_rms_rows(h, g) for g in gain)

    fused_norm = () if next_norm_g is None else (next_norm_g,)
    h3, gate, *hn1_next = _matmul(hn3, w["w_ple_gate"], name=nm("plegate"), bm=_NORM_FUSED_ROWS, bn=d,
                                  out_dtypes=(F32, F32) + (BF16,) * len(fused_norm), extras=(pp, h2),
                                  row_vectors=fused_norm, epilogue=gate_epilogue)
    saved = dict(h0=h0, hn1=hn1, proj=proj, qs=qs, kn=kn, v=v, lt=lt, mix=mix, h1=h1, hn2=hn2, f=f, h2=h2,
                 hn3=hn3, pp=pp, gate=gate, p16=p16)
    return h3, (hn1_next[0] if hn1_next else None), w, w_next, saved


def _layer_backward(dh3, a, w, s, li, order_after, start_rest):
    nm = lambda k: f"{k}_bwd_l{li}"
    t = dh3.shape[0]
    dpre, dpp = _ple_bwd(dh3, a["gate"], a["pp"], order_after, name=nm("ple"))
    g_gate = _weight_grad(a["hn3"], [dpre], name=nm("dwgate"))
    g_proj = _weight_grad(a["p16"], [dpp], name=nm("dwproj"), column_shards=True)
    dh2, dh2_16, g_n3 = _matmul_rms_bwd([dpre], w["w_ple_gate"], a["h2"], s["norm3_g"], dh3, name=nm("dh2"))
    du = _matmul(dh2_16, w["w_ff2"], name=nm("du"), tb=True, bm=t, bn=512, out_dtypes=(BF16,), extras=(a["f"],),
                 epilogue=lambda acc, f: (acc * (2.0 * jnp.sqrt(f.astype(F32))),))
    g_ff2 = _weight_grad(a["f"], [dh2_16], name=nm("dwff2"))
    g_ff1 = _weight_grad(a["hn2"], [du], name=nm("dwff1"), column_shards=True)
    dh1, dh1_16, g_n2 = _matmul_rms_bwd([du], w["w_ff1"], a["h1"], s["norm2_g"], dh2, name=nm("dh1"))
    dmix = _matmul(dh1_16, w["w_out"], name=nm("dmix"), tb=True, bm=t, bn=256)
    g_out = _weight_grad(a["mix"], [dh1_16], name=nm("dwout"))
    started = start_rest(dict(w_out=_split_rows(g_out), w_ff1=g_ff1, w_ff2=_split_rows(g_ff2),
                              w_ple_gate=_split_rows(g_gate), w_ple_proj=g_proj), dmix)
    d_conv, g_conv = _conv_bwd(dmix, a["proj"], w["conv_w"], name=nm("conv"))
    dqs, dkn, dv = _attn_bwd(dmix, a["qs"], a["kn"], a["v"], a["lt"], started, name=nm("attn"))
    d_qkv, g_q, g_k = _qk_prep_bwd(dqs, dkn, dv, a["proj"], s["gq"], s["gk"], name=nm("qkprep"))
    d_sgu, g_sn, g_sw, g_sb = _sgu_bwd(dmix, a["proj"], s["sgu_norm_g"], s["sgu_w"], s["b_exp"], name=nm("sgu"))
    dproj = [d_conv, d_qkv, d_sgu]
    g_in = jnp.concatenate([_weight_grad(piece, [a["hn1"]], name=nm(f"dwin{i}")) for i, piece in enumerate(dproj)])
    dh0, _, g_n1 = _matmul_rms_bwd(dproj, w["w_in"], a["h0"], s["norm1_g"], dh1, name=nm("dh0"), w_is_k_by_d=True)
    small = dict(norm1_g=g_n1, norm2_g=g_n2, norm3_g=g_n3, q_norm_g=g_q, k_norm_g=g_k, sgu_norm_g=g_sn, sgu_w=g_sw,
                 sgu_b=g_sb, conv_w=g_conv)
    return dh0, _split_rows(g_in), small


def _small_gradients(raw, depth):
    return dict(
        norm1_g=raw["norm1_g"].reshape(depth, -1), norm2_g=raw["norm2_g"].reshape(depth, -1),
        norm3_g=raw["norm3_g"].reshape(depth, -1),
        q_norm_g=raw["q_norm_g"].reshape(depth, -1, HEAD_DIM).sum(1),
        k_norm_g=raw["k_norm_g"].reshape(depth, -1, HEAD_DIM).sum(1),
        sgu_norm_g=raw["sgu_norm_g"].reshape(depth, -1), sgu_w=raw["sgu_w"],
        sgu_b=jnp.swapaxes(raw["sgu_b"].reshape(depth, CHUNK, SGU_HEADS, HEAD_DIM).sum(-1), 1, 2),
        conv_w=raw["conv_w"][:, :CONV_TAPS],
    )


def kernel(x, p, norm1_g, w_in, conv_w, q_norm_g, k_norm_g, sgu_norm_g, sgu_w, sgu_b, w_out, norm2_g, w_ff1, w_ff2, norm3_g, w_ple_gate, w_ple_proj, loss_target, m_norm1_g, m_w_in, m_conv_w, m_q_norm_g, m_k_norm_g, m_sgu_norm_g, m_sgu_w, m_sgu_b, m_w_out, m_norm2_g, m_w_ff1, m_w_ff2, m_norm3_g, m_w_ple_gate, m_w_ple_proj, v_norm1_g, v_w_in, v_conv_w, v_q_norm_g, v_k_norm_g, v_sgu_norm_g, v_sgu_w, v_sgu_b, v_w_out, v_norm2_g, v_w_ff1, v_w_ff2, v_norm3_g, v_w_ple_gate, v_w_ple_proj):
    weights = dict(norm1_g=norm1_g, w_in=w_in, conv_w=conv_w, q_norm_g=q_norm_g, k_norm_g=k_norm_g,
                   sgu_norm_g=sgu_norm_g, sgu_w=sgu_w, sgu_b=sgu_b, w_out=w_out, norm2_g=norm2_g, w_ff1=w_ff1,
                   w_ff2=w_ff2, norm3_g=norm3_g, w_ple_gate=w_ple_gate, w_ple_proj=w_ple_proj)
    mom = dict(norm1_g=m_norm1_g, w_in=m_w_in, conv_w=m_conv_w, q_norm_g=m_q_norm_g, k_norm_g=m_k_norm_g,
               sgu_norm_g=m_sgu_norm_g, sgu_w=m_sgu_w, sgu_b=m_sgu_b, w_out=m_w_out, norm2_g=m_norm2_g, w_ff1=m_w_ff1,
               w_ff2=m_w_ff2, norm3_g=m_norm3_g, w_ple_gate=m_w_ple_gate, w_ple_proj=m_w_ple_proj)
    var = dict(norm1_g=v_norm1_g, w_in=v_w_in, conv_w=v_conv_w, q_norm_g=v_q_norm_g, k_norm_g=v_k_norm_g,
               sgu_norm_g=v_sgu_norm_g, sgu_w=v_sgu_w, sgu_b=v_sgu_b, w_out=v_w_out, norm2_g=v_norm2_g, w_ff1=v_w_ff1,
               w_ff2=v_w_ff2, norm3_g=v_norm3_g, w_ple_gate=v_w_ple_gate, w_ple_proj=v_w_ple_proj)
    for params in (weights, mom, var):
        params["w_in"] = jnp.swapaxes(params["w_in"], 1, 2)
    depth = norm1_g.shape[0]
    mx, my, mc = _my_place()
    me = _slot_of(mx, my, mc)

    gathers = []
    modes_of = lambda names: tuple(_GATHER_MODE[k] for k in names)
    token = x[0, :8, :LANES]
    for li in range(depth):
        groups = [([weights[k][li] if k == "conv_w" else weights[k][li].astype(BF16) for k in names], modes_of(names))
                  for names in (_FIRST, _REST)]
        started, token = _exchange_start(groups, token, name=f"gather_start_l{li}", hops=_NEAR)
        gathers.append(started)

    small = []
    for li in range(depth):
        small.append(dict(
            norm1_g=norm1_g[li][None], norm2_g=norm2_g[li][None], norm3_g=norm3_g[li][None],
            gq=jnp.tile(q_norm_g[li], _QK_BLOCK // HEAD_DIM)[None], gk=jnp.tile(k_norm_g[li], _QK_BLOCK // HEAD_DIM)[None],
            sgu_norm_g=sgu_norm_g[li][None], sgu_w=sgu_w[li], b_exp=jnp.repeat(sgu_b[li].T, HEAD_DIM, axis=1),
        ))
    small[0]["norm1_g"] = small[0]["norm1_g"] + token[0, 0]

    h = x[0]
    saved, full = [], []
    relayed_first, relayed_rest = [None] * depth, [None] * depth
    rest_members = [[_REST.index(k) for k in names] for names in _REST_GROUPS]

    def relay_first(li, after):
        if li < depth:
            (relayed_first[li],), _ = _exchange_relay(gathers[li][0], after, modes=modes_of(_FIRST),
                                                      regroup=[list(range(len(_FIRST)))], name=f"gather_first_relay_l{li}")

    def fetch_first(li, after):
        if li == depth:
            return None
        landed = _exchange_wait(relayed_first[li], after, modes=modes_of(_FIRST), hops=_RELAY,
                                name=f"gather_first_wait_l{li}")
        return _whole_matrices(_FIRST, landed)

    hn1 = _rms_fwd(h, small[0]["norm1_g"], name="rms1_l0")
    relay_first(0, hn1)
    w_first = fetch_first(0, hn1)
    for li in range(depth):

        def relay_rest(after, li=li):
            relayed_rest[li], _ = _exchange_relay(gathers[li][1], after, modes=modes_of(_REST), regroup=rest_members,
                                                  name=f"gather_rest_relay_l{li}")

        def fetch(g, after, li=li):
            landed = _exchange_wait(relayed_rest[li][g], after, modes=modes_of(_REST_GROUPS[g]), hops=_RELAY,
                                    name=f"gather_{_REST_GROUPS[g][0]}_wait_l{li}")
            return _whole_matrices(_REST_GROUPS[g], landed)

        gathered = dict(relay_rest=relay_rest, fetch=fetch, relay_next=functools.partial(relay_first, li + 1),
                        fetch_next=functools.partial(fetch_first, li + 1))
        next_norm_g = small[li + 1]["norm1_g"] if li + 1 < depth else None
        h, hn1, w, w_first, acts = _layer_forward(h, hn1, p[li, 0].astype(BF16), small[li], li, w_first, gathered,
                                                  next_norm_g)
        full.append(w)
        saved.append(acts)
    dh, loss_tile = _loss_head(h, loss_target[0], name="loss_head")
    loss = lax.psum(loss_tile[0, 0], ("x", "y", "c"))

    small_names = _SMALL + ("conv_w",)
    scatter_first, scatter_rest = [None] * depth, [None] * depth
    first_modes, rest_modes = (_SCATTER,) + (_GATHER,) * len(small_names), (_SCATTER,) * len(_REST)
    token = loss_tile
    for li in reversed(range(depth)):

        def start_rest(parts, after, li=li):
            (scatter_rest[li],), started = _exchange_start([([parts[k] for k in _REST], rest_modes)], after,
                                                           name=f"scatter_rest_start_l{li}")
            return started

        dh, g_in, small_grads = _layer_backward(dh, saved[li], full[li], small[li], li, token, start_rest)
        (scatter_first[li],), token = _exchange_start(
            [([g_in] + [small_grads[k] for k in small_names], first_modes)], dh, name=f"scatter_first_start_l{li}")
    grad_x = dh[None]

    grads, delta, new_m, new_v = {}, {}, {}, {}
    arrived = {k: [None] * depth for k in _BIG}
    for li in reversed(range(depth)):
        landed = _exchange_wait(scatter_rest[li], token, modes=rest_modes, name=f"scatter_rest_wait_l{li}")
        for k, g in zip(_REST, landed, strict=True):
            arrived[k][li] = g
    for k in _REST:
        grads[k], delta[k], new_m[k], new_v[k] = _adamw_reduce(weights[k], arrived[k], mom[k], var[k], name=f"adamw_{k}")
    small_parts = {k: [None] * depth for k in small_names}
    updated = jnp.stack([delta[k][0, 0, :1] for k in _REST])
    for li in reversed(range(depth)):
        arrived["w_in"][li], *parts = _exchange_wait(scatter_first[li], updated, modes=first_modes,
                                                     name=f"scatter_first_wait_l{li}")
        for k, part in zip(small_names, parts, strict=True):
            small_parts[k][li] = part
    grads["w_in"], delta["w_in"], new_m["w_in"], new_v["w_in"] = _adamw_reduce(
        weights["w_in"], arrived["w_in"], mom["w_in"], var["w_in"], name="adamw_w_in")
    for results in (grads, delta, new_m, new_v):
        results["w_in"] = jnp.swapaxes(results["w_in"], 1, 2)
    sums = _sum_slots([small_parts[k] for k in small_names], name="sum_small_grads")
    grads.update(_small_gradients(dict(zip(small_names, sums)), depth))
    n_conv = conv_w.shape[2]
    grads["conv_w"] = lax.dynamic_slice_in_dim(grads["conv_w"], me * n_conv, n_conv, axis=2)
    for k in small_names:
        as_rows = lambda a: a.reshape(-1, a.shape[-1])
        outs = _adamw(as_rows(weights[k]), as_rows(grads[k]), as_rows(mom[k]), as_rows(var[k]), name=f"adamw_{k}")
        delta[k], new_m[k], new_v[k] = (o.reshape(weights[k].shape) for o in outs)

    return (loss, grad_x, *[grads[k] for k in _ORDER], *[delta[k] for k in _ORDER],
            *[new_m[k] for k in _ORDER], *[new_v[k] for k in _ORDER])
```

```python
import functools
import math

import jax
import jax.numpy as jnp
from jax import lax
from jax.experimental import pallas as pl
from jax.experimental.pallas import tpu as pltpu

F32 = jnp.float32
BF16 = jnp.bfloat16

N_DEV = 8
HEAD_DIM = 64
CONV_W = 256
ATTN_W = 512
SGU_W = 256
SGU_HEADS = 4
CHUNK = 128
CONV_TAPS = 3
EPS = 1e-6
QK_SCALE = HEAD_DIM ** -0.5

ADAM_LR = 0.001
ADAM_B1 = 0.9
ADAM_B2 = 0.999
ADAM_EPS = 1e-08
ADAM_WD = 0.01
ADAM_STEP = 10

LANES = 128
BF16_TILE_ROWS = 16
VMEM_LIMIT_BYTES = 56 * 1024 * 1024
MESH = pl.DeviceIdType.MESH


def _params(*sem):
    return pltpu.CompilerParams(dimension_semantics=sem, vmem_limit_bytes=VMEM_LIMIT_BYTES)


def _row_block(rows, cap):
    if rows <= cap:
        return rows
    return max(b for b in range(BF16_TILE_ROWS, cap + 1, BF16_TILE_ROWS) if rows % b == 0)


def _matmul(a, b, *, name, tb=False, bm=512, bn=512, out_dtypes=(F32,), epilogue=None, extras=(), row_vectors=()):
    m, k = a.shape
    n = b.shape[0] if tb else b.shape[1]
    assert k == (b.shape[1] if tb else b.shape[0])
    bm, bn = min(bm, m), min(bn, n)
    assert m % bm == 0 and n % bn == 0
    a_spec = pl.BlockSpec((bm, k), lambda i, j: (i, 0))
    b_spec = pl.BlockSpec((bn, k), lambda i, j: (j, 0)) if tb else pl.BlockSpec((k, bn), lambda i, j: (0, j))
    dims = (((1,), (1 if tb else 0,)), ((), ()))
    n_in = len(extras) + len(row_vectors)
    for e in extras:
        assert e.shape == (m, n), (e.shape, m, n)
    for e in row_vectors:
        assert e.shape == (1, n), (e.shape, n)

    def body(a_ref, b_ref, *rest):
        outs = rest[n_in:]
        acc = lax.dot_general(a_ref[...], b_ref[...], dims, preferred_element_type=F32)
        res = (acc,) if epilogue is None else epilogue(acc, *[e[...] for e in rest[:n_in]])
        for o_ref, r in zip(outs, res, strict=True):
            o_ref[...] = r.astype(o_ref.dtype)

    tile = pl.BlockSpec((bm, bn), lambda i, j: (i, j))
    vec = pl.BlockSpec((1, bn), lambda i, j: (0, j))
    out = pl.pallas_call(
        body,
        name=name,
        grid=(m // bm, n // bn),
        in_specs=[a_spec, b_spec] + [tile] * len(extras) + [vec] * len(row_vectors),
        out_specs=[tile] * len(out_dtypes),
        out_shape=[jax.ShapeDtypeStruct((m, n), d) for d in out_dtypes],
        compiler_params=_params("parallel", "parallel"),
    )(a, b, *extras, *row_vectors)
    return out[0] if len(out_dtypes) == 1 else out


def _rms_rows(x, g):
    return x * lax.rsqrt(jnp.mean(x * x, axis=-1, keepdims=True) + EPS) * g


_WEIGHT_GRAD_ACC_ELEMS = 1024 * 1024


def _weight_grad(x, dys, *, name, column_shards=False):
    t, m = x.shape
    n = sum(dy.shape[1] for dy in dys)
    bm = m if m <= 2 * LANES else min(m // 2, max(LANES, _WEIGHT_GRAD_ACC_ELEMS // n // LANES * LANES))
    assert m % bm == 0
    ns = n // N_DEV

    def body(x_ref, *rest):
        o_ref = rest[-1]
        xb = x_ref[...]
        acc = jnp.concatenate([lax.dot_general(xb, dy_ref[...], _TN, preferred_element_type=F32) for dy_ref in rest[:-1]],
                              axis=1)
        if column_shards:
            for s in range(N_DEV):
                o_ref[s] = acc[:, s * ns:(s + 1) * ns].astype(o_ref.dtype)
        else:
            o_ref[...] = acc.astype(o_ref.dtype)

    if column_shards:
        out_spec, out_dims = pl.BlockSpec((N_DEV, bm, ns), lambda i: (0, i, 0)), (N_DEV, m, ns)
    else:
        out_spec, out_dims = pl.BlockSpec((bm, n), lambda i: (i, 0)), (m, n)
    return pl.pallas_call(
        body,
        name=name,
        grid=(m // bm,),
        in_specs=[pl.BlockSpec((t, bm), lambda i: (0, i))] + [pl.BlockSpec(dy.shape, lambda i: (0, 0)) for dy in dys],
        out_specs=out_spec,
        out_shape=jax.ShapeDtypeStruct(out_dims, BF16),
        compiler_params=_params("parallel"),
    )(x, *dys)


def _rms_fwd(h, g, *, name, br=512):
    t, d = h.shape
    br = min(br, t)

    def body(h_ref, g_ref, o_ref):
        o_ref[...] = _rms_rows(h_ref[...], g_ref[...]).astype(o_ref.dtype)

    return pl.pallas_call(
        body,
        name=name,
        grid=(t // br,),
        in_specs=[pl.BlockSpec((br, d), lambda i: (i, 0)), pl.BlockSpec((1, d), lambda i: (0, 0))],
        out_specs=pl.BlockSpec((br, d), lambda i: (i, 0)),
        out_shape=jax.ShapeDtypeStruct((t, d), BF16),
        compiler_params=_params("parallel"),
    )(h, g)


def _matmul_rms_bwd(dzs, w, h, g, dres, *, name, w_is_k_by_d=False):
    t, d = h.shape
    widths = [dz.shape[1] for dz in dzs]
    k = sum(widths)
    assert w.shape == ((k, d) if w_is_k_by_d else (d, k))
    br = min(t, 512 if k <= d else 256)
    n_dz = len(dzs)

    def body(*refs):
        w_ref, h_ref, g_ref, dres_ref, dh_ref, dh16_ref, dg_ref = refs[n_dz:]
        x = h_ref[...]
        dyv, at = None, 0
        for dz_ref, width in zip(refs[:n_dz], widths):
            if w_is_k_by_d:
                part = jnp.dot(dz_ref[...], w_ref[at:at + width, :], preferred_element_type=F32)
            else:
                part = lax.dot_general(dz_ref[...], w_ref[:, at:at + width], _NT, preferred_element_type=F32)
            dyv = part if dyv is None else dyv + part
            at += width
        r = lax.rsqrt(jnp.mean(x * x, axis=-1, keepdims=True) + EPS)
        xhat = x * r
        dxhat = dyv * g_ref[...]
        dh = dres_ref[...] + r * (dxhat - xhat * jnp.mean(dxhat * xhat, axis=-1, keepdims=True))
        dh_ref[...] = dh
        dh16_ref[...] = dh.astype(dh16_ref.dtype)

        @pl.when(pl.program_id(0) == 0)
        def _():
            dg_ref[...] = jnp.zeros_like(dg_ref)

        dg_ref[...] += jnp.sum(dyv * xhat, axis=0, keepdims=True)

    row = pl.BlockSpec((br, d), lambda i: (i, 0))
    vec = pl.BlockSpec((1, d), lambda i: (0, 0))
    return pl.pallas_call(
        body,
        name=name,
        grid=(t // br,),
        in_specs=[pl.BlockSpec((br, width), lambda i: (i, 0)) for width in widths]
        + [pl.BlockSpec(w.shape, lambda i: (0, 0)), row, vec, row],
        out_specs=[row, row, vec],
        out_shape=[jax.ShapeDtypeStruct((t, d), F32), jax.ShapeDtypeStruct((t, d), BF16),
                   jax.ShapeDtypeStruct((1, d), F32)],
        compiler_params=_params("arbitrary"),
    )(*dzs, w, h, g, dres)


def _group_mean(x, width):
    grp = lax.broadcasted_iota(jnp.int32, x.shape, 1) // HEAD_DIM
    out = jnp.zeros_like(x)
    for gi in range(width // HEAD_DIM):
        m = grp == gi
        s = jnp.sum(jnp.where(m, x, 0.0), axis=1, keepdims=True)
        out = jnp.where(m, s, out)
    return out * (1.0 / HEAD_DIM)


def _gelu(x):
    return 0.5 * x * (1.0 + lax.erf(x * (2.0 ** -0.5)))


def _gelu_grad(x):
    cdf = 0.5 * (1.0 + lax.erf(x * (2.0 ** -0.5)))
    pdf = jnp.exp(-0.5 * x * x) * (1.0 / math.sqrt(2.0 * math.pi))
    return cdf + x * pdf


def _shift_down(z, s, row):
    return jnp.where(row >= s, pltpu.roll(z, s, 0), 0.0)


def _shift_up(z, s, row, t):
    return jnp.where(row < t - s, pltpu.roll(z, t - s, 0), 0.0)


def _conv_fwd(proj, conv_w, *, name):
    t = proj.shape[0]
    nb = CONV_W // LANES

    def body(b_ref, c_ref, h_ref, w_ref, o_ref):
        row = lax.broadcasted_iota(jnp.int32, (t, LANES), 0)
        z = c_ref[...] * h_ref[...]
        w = w_ref[...]
        conv = w[2:3, :] * z + w[1:2, :] * _shift_down(z, 1, row) + w[0:1, :] * _shift_down(z, 2, row)
        o_ref[...] = (b_ref[...] * conv).astype(o_ref.dtype)

    return pl.pallas_call(
        body,
        name=name,
        grid=(nb,),
        in_specs=[
            pl.BlockSpec((t, LANES), lambda j: (0, j)),
            pl.BlockSpec((t, LANES), lambda j: (0, nb + j)),
            pl.BlockSpec((t, LANES), lambda j: (0, 2 * nb + j)),
            pl.BlockSpec((CONV_TAPS, LANES), lambda j: (0, j)),
        ],
        out_specs=pl.BlockSpec((t, LANES), lambda j: (0, j)),
        out_shape=jax.ShapeDtypeStruct((t, CONV_W), BF16),
        compiler_params=_params("parallel"),
    )(proj, proj, proj, conv_w)


def _conv_bwd(dmix, proj, conv_w, *, name):
    t = proj.shape[0]

    def body(dy_ref, b_ref, c_ref, h_ref, w_ref, o_ref, dw_ref):
        row = lax.broadcasted_iota(jnp.int32, (t, CONV_W), 0)
        ac, ah = c_ref[...], h_ref[...]
        z = ac * ah
        w = w_ref[...]
        z1 = _shift_down(z, 1, row)
        z2 = _shift_down(z, 2, row)
        conv = w[2:3, :] * z + w[1:2, :] * z1 + w[0:1, :] * z2
        dy = dy_ref[...]
        o_ref[:, 0:CONV_W] = (dy * conv).astype(o_ref.dtype)
        dconv = dy * b_ref[...]
        dz = w[2:3, :] * dconv + w[1:2, :] * _shift_up(dconv, 1, row, t) + w[0:1, :] * _shift_up(dconv, 2, row, t)
        o_ref[:, CONV_W:2 * CONV_W] = (dz * ah).astype(o_ref.dtype)
        o_ref[:, 2 * CONV_W:3 * CONV_W] = (dz * ac).astype(o_ref.dtype)
        dw_ref[...] = jnp.zeros_like(dw_ref)
        dw_ref[0:1, :] = jnp.sum(dconv * z2, axis=0, keepdims=True)
        dw_ref[1:2, :] = jnp.sum(dconv * z1, axis=0, keepdims=True)
        dw_ref[2:3, :] = jnp.sum(dconv * z, axis=0, keepdims=True)

    col = lambda j: pl.BlockSpec((t, CONV_W), lambda i: (0, j))
    return pl.pallas_call(
        body,
        name=name,
        grid=(1,),
        in_specs=[col(0), col(0), col(1), col(2), pl.BlockSpec((CONV_TAPS, CONV_W), lambda i: (0, 0))],
        out_specs=[pl.BlockSpec((t, 3 * CONV_W), lambda i: (0, 0)), pl.BlockSpec((8, CONV_W), lambda i: (0, 0))],
        out_shape=[jax.ShapeDtypeStruct((t, 3 * CONV_W), BF16), jax.ShapeDtypeStruct((8, CONV_W), F32)],
        compiler_params=_params("arbitrary"),
    )(dmix, proj, proj, proj, conv_w)


_QK_BLOCK = 256


def _qk_prep(proj, gq, gk, *, name, br=512):
    t = proj.shape[0]
    br = min(br, t)
    nb = ATTN_W // _QK_BLOCK
    q0 = (3 * CONV_W) // _QK_BLOCK

    def body(q_ref, k_ref, v_ref, gq_ref, gk_ref, qo_ref, ko_ref, vo_ref):
        q = q_ref[...]
        k = k_ref[...]
        rq = lax.rsqrt(_group_mean(q * q, _QK_BLOCK) + EPS)
        rk = lax.rsqrt(_group_mean(k * k, _QK_BLOCK) + EPS)
        qo_ref[...] = ((q * rq * gq_ref[...]).astype(BF16) * QK_SCALE).astype(qo_ref.dtype)
        ko_ref[...] = (k * rk * gk_ref[...]).astype(ko_ref.dtype)
        vo_ref[...] = v_ref[...].astype(vo_ref.dtype)

    col = lambda off: pl.BlockSpec((br, _QK_BLOCK), lambda i, j: (i, off + j))
    vec = pl.BlockSpec((1, _QK_BLOCK), lambda i, j: (0, 0))
    return pl.pallas_call(
        body,
        name=name,
        grid=(t // br, nb),
        in_specs=[col(q0), col(q0 + nb), col(q0 + 2 * nb), vec, vec],
        out_specs=[col(0)] * 3,
        out_shape=[jax.ShapeDtypeStruct((t, ATTN_W), BF16)] * 3,
        compiler_params=_params("parallel", "parallel"),
    )(proj, proj, proj, gq, gk)


def _qk_prep_bwd(dqs, dkn, dv, proj, gq, gk, *, name, br=256):
    t = proj.shape[0]
    br = min(br, t)
    nb = ATTN_W // _QK_BLOCK
    q0 = (3 * CONV_W) // _QK_BLOCK

    def norm_bwd(dy, x, g):
        r = lax.rsqrt(_group_mean(x * x, ATTN_W) + EPS)
        xhat = x * r
        dxhat = dy * g
        dx = r * (dxhat - xhat * _group_mean(dxhat * xhat, ATTN_W))
        return dx, jnp.sum(dy * xhat, axis=0, keepdims=True)

    def body(dq_ref, dk_ref, dv_ref, *rest):
        x_refs, (gq_ref, gk_ref, o_ref, dgq_ref, dgk_ref) = rest[:2 * nb], rest[2 * nb:]
        whole = lambda refs: jnp.concatenate([r[...] for r in refs], axis=1)
        dq, dgq = norm_bwd(dq_ref[...] * QK_SCALE, whole(x_refs[:nb]), whole([gq_ref] * nb))
        dk, dgk = norm_bwd(dk_ref[...], whole(x_refs[nb:]), whole([gk_ref] * nb))
        o_ref[:, 0:ATTN_W] = dq.astype(o_ref.dtype)
        o_ref[:, ATTN_W:2 * ATTN_W] = dk.astype(o_ref.dtype)
        o_ref[:, 2 * ATTN_W:3 * ATTN_W] = dv_ref[...].astype(o_ref.dtype)

        @pl.when(pl.program_id(0) == 0)
        def _():
            dgq_ref[...] = jnp.zeros_like(dgq_ref)
            dgk_ref[...] = jnp.zeros_like(dgk_ref)

        dgq_ref[...] += dgq
        dgk_ref[...] += dgk

    rows = pl.BlockSpec((br, ATTN_W), lambda i: (i, 0))
    col = lambda j: pl.BlockSpec((br, _QK_BLOCK), lambda i: (i, j))
    gain = pl.BlockSpec((1, _QK_BLOCK), lambda i: (0, 0))
    total = pl.BlockSpec((1, ATTN_W), lambda i: (0, 0))
    return pl.pallas_call(
        body,
        name=name,
        grid=(t // br,),
        in_specs=[rows, rows, rows] + [col(q0 + j) for j in range(2 * nb)] + [gain, gain],
        out_specs=[pl.BlockSpec((br, 3 * ATTN_W), lambda i: (i, 0)), total, total],
        out_shape=[jax.ShapeDtypeStruct((t, 3 * ATTN_W), BF16)] + [jax.ShapeDtypeStruct((1, ATTN_W), F32)] * 2,
        compiler_params=_params("arbitrary"),
    )(dqs, dkn, dv, *[proj] * (2 * nb), gq, gk)


def _key_order_matrix(tb, relation):
    jj = lax.broadcasted_iota(jnp.int32, (tb, tb), 0)
    ss = lax.broadcasted_iota(jnp.int32, (tb, tb), 1)
    return relation(jj, ss).astype(BF16)


def _log_sigmoids(z):
    lb = jnp.minimum(z, 0.0) - jnp.log(1.0 + jnp.exp(-jnp.abs(z)))
    return lb, lb - z


def _below_diagonal(tb):
    return lax.broadcasted_iota(jnp.int32, (tb, tb), 1) < lax.broadcasted_iota(jnp.int32, (tb, tb), 0)


_NT = (((1,), (1,)), ((), ()))
_TN = (((0,), (0,)), ((), ()))
_ATTN_BLOCK = 256
_ATTN_FWD_UNROLL = 4
_ATTN_BWD_UNROLL = 3


def _attn_fwd(qs, kn, v, *, name, tb=_ATTN_BLOCK, unroll=_ATTN_FWD_UNROLL):
    t = qs.shape[0]
    tb = min(tb, t)
    assert t % tb == 0
    n_pairs = ATTN_W // LANES

    def body(q_ref, k_ref, v_ref, o_ref, lt_ref, acc_ref, carry_ref):
        qb = pl.program_id(1)
        half = lax.broadcasted_iota(jnp.int32, (1, LANES), 1) // HEAD_DIM
        later = _key_order_matrix(tb, lambda j, s: j > s)
        acc_ref[...] = jnp.zeros_like(acc_ref)
        carry_ref[...] = jnp.zeros_like(carry_ref)
        q = q_ref[...]
        qh = [jnp.where(half == h, q, jnp.zeros_like(q)) for h in range(2)]

        def tiles(kbs, diagonal):
            blk = []
            for kb in kbs:
                start = pl.multiple_of(kb * tb, tb)
                blk.append((k_ref[pl.ds(start, tb), :], v_ref[pl.ds(start, tb), :]))
            chains = [(h, j) for j in range(len(kbs)) for h in range(2)]
            z = [lax.dot_general(qh[h], blk[j][0], _NT, preferred_element_type=F32) for h, j in chains]
            causal = _below_diagonal(tb) if diagonal else None
            lb, lr = [], []
            for zi in z:
                b, r = _log_sigmoids(zi)
                lb.append(b)
                lr.append(jnp.where(causal, r, 0.0) if diagonal else r)
            suffix = [jnp.dot(r.astype(BF16), later, preferred_element_type=F32) for r in lr]
            carry = [carry_ref[0], carry_ref[1]]
            w = []
            for i, (h, j) in enumerate(chains):
                wi = jnp.exp(lb[i] + (suffix[i] + carry[h][:, 0:1]))
                w.append((jnp.where(causal, wi, 0.0) if diagonal else wi).astype(BF16))
                carry[h] = carry[h] + jnp.sum(lr[i], axis=1, keepdims=True)
            for i, (h, j) in enumerate(chains):
                vh = jnp.where(half == h, blk[j][1], jnp.zeros_like(blk[j][1]))
                acc_ref[h] += jnp.dot(w[i], vh, preferred_element_type=F32)
            carry_ref[0] = carry[0]
            carry_ref[1] = carry[1]

        tiles([qb], True)

        def step(i, _):
            kb = qb - 1 - unroll * i
            tiles([kb - u for u in range(unroll)], False)
            return 0

        lax.fori_loop(0, qb // unroll, step, 0)
        for left in range(1, unroll):

            @pl.when(qb % unroll == left)
            def _(left=left):
                tiles([left - 1 - u for u in range(left)], False)

        o_ref[...] = (acc_ref[0] + acc_ref[1]).astype(o_ref.dtype)
        lt_ref[...] = jnp.where(half == 0, carry_ref[0], carry_ref[1])

    return pl.pallas_call(
        body,
        name=name,
        grid=(n_pairs, t // tb),
        in_specs=[
            pl.BlockSpec((tb, LANES), lambda p, i: (i, p)),
            pl.BlockSpec((t, LANES), lambda p, i: (0, p)),
            pl.BlockSpec((t, LANES), lambda p, i: (0, p)),
        ],
        out_specs=[pl.BlockSpec((tb, LANES), lambda p, i: (i, p))] * 2,
        out_shape=[jax.ShapeDtypeStruct((t, ATTN_W), BF16), jax.ShapeDtypeStruct((t, ATTN_W), F32)],
        scratch_shapes=[pltpu.VMEM((2, tb, LANES), F32), pltpu.VMEM((2, tb, LANES), F32)],
        compiler_params=_params("parallel", "parallel"),
    )(qs, kn, v)


def _attn_bwd(dmix, qs, kn, v, lt, order_after, *, name, tb=_ATTN_BLOCK, unroll=_ATTN_BWD_UNROLL):
    t = qs.shape[0]
    tb = min(tb, t)
    assert t % tb == 0
    n_pairs = ATTN_W // LANES
    dy0 = CONV_W // LANES

    def body(do_ref, q_ref, k_ref, v_ref, lt_ref, order_ref, dq_ref, dk_ref, dv_ref, dqacc_ref, cc_ref, cg_ref):
        qb = pl.program_id(1)
        half = lax.broadcasted_iota(jnp.int32, (1, LANES), 1) // HEAD_DIM
        lane = lax.broadcasted_iota(jnp.int32, (tb, LANES), 1)
        later = _key_order_matrix(tb, lambda j, s: j > s)
        before = _key_order_matrix(tb, lambda j, s: j < s)
        q = q_ref[...]
        do = do_ref[...].astype(BF16)
        lt = lt_ref[...]
        qh = [jnp.where(half == h, q, jnp.zeros_like(q)) for h in range(2)]
        doh = [jnp.where(half == h, do, jnp.zeros_like(do)) for h in range(2)]
        lth = [jnp.sum(jnp.where(lane == h * HEAD_DIM, lt, 0.0), axis=1, keepdims=True) for h in range(2)]

        @pl.when(qb == 0)
        def _():
            dk_ref[...] = jnp.zeros_like(dk_ref)
            dv_ref[...] = jnp.zeros_like(dv_ref)

        dqacc_ref[...] = jnp.zeros_like(dqacc_ref)
        cc_ref[...] = jnp.zeros_like(cc_ref)
        cg_ref[...] = jnp.zeros_like(cg_ref)

        def tiles(kbs, diagonal):
            starts = [pl.multiple_of(kb * tb, tb) for kb in kbs]
            blk = [(k_ref[pl.ds(s, tb), :], v_ref[pl.ds(s, tb), :]) for s in starts]
            chains = [(h, j) for j in range(len(kbs)) for h in range(2)]
            z = [lax.dot_general(qh[h], blk[j][0], _NT, preferred_element_type=F32) for h, j in chains]
            da = [lax.dot_general(doh[h], jnp.where(half == h, blk[j][1], jnp.zeros_like(blk[j][1])), _NT,
                                  preferred_element_type=F32) for h, j in chains]
            causal = _below_diagonal(tb) if diagonal else None
            lb, lr = [], []
            for zi in z:
                b, r = _log_sigmoids(zi)
                lb.append(b)
                lr.append(jnp.where(causal, r, 0.0) if diagonal else r)
            suffix = [jnp.dot(r.astype(BF16), later, preferred_element_type=F32) for r in lr]
            cc = [cc_ref[0], cc_ref[1]]
            cg = [cg_ref[0], cg_ref[1]]
            a16, g = [], []
            for i, (h, j) in enumerate(chains):
                cc[h] = cc[h] + jnp.sum(lr[i], axis=1, keepdims=True)
                a = jnp.exp(lb[i] + suffix[i] + (lth[h] - cc[h][:, 0:1]))
                if diagonal:
                    a = jnp.where(causal, a, 0.0)
                a16.append(a.astype(BF16))
                g.append(da[i] * a)
            g_before = [jnp.dot(gi.astype(BF16), before, preferred_element_type=F32) for gi in g]
            dz = []
            for i, (h, j) in enumerate(chains):
                dzi = g[i] - jnp.exp(lb[i]) * (g[i] + (g_before[i] + cg[h][:, 0:1]))
                dz.append((jnp.where(causal, dzi, 0.0) if diagonal else dzi).astype(BF16))
                cg[h] = cg[h] + jnp.sum(g[i], axis=1, keepdims=True)
            for i, (h, j) in enumerate(chains):
                kh = jnp.where(half == h, blk[j][0], jnp.zeros_like(blk[j][0]))
                dqacc_ref[h] += jnp.dot(dz[i], kh, preferred_element_type=F32)
                dk_ref[pl.ds(starts[j], tb), :] += lax.dot_general(dz[i], qh[h], _TN, preferred_element_type=F32)
                dv_ref[pl.ds(starts[j], tb), :] += lax.dot_general(a16[i], doh[h], _TN, preferred_element_type=F32)
            for h in range(2):
                cc_ref[h] = cc[h]
                cg_ref[h] = cg[h]

        def step(i, _):
            kb = unroll * i
            tiles([kb + u for u in range(unroll)], False)
            return 0

        lax.fori_loop(0, qb // unroll, step, 0)
        for left in range(1, unroll):

            @pl.when(qb % unroll == left)
            def _(left=left):
                tiles([qb - left + u for u in range(left)], False)

        tiles([qb], True)
        dq_ref[...] = dqacc_ref[0] + dqacc_ref[1]

    qblk = pl.BlockSpec((tb, LANES), lambda p, i: (i, p))
    whole = pl.BlockSpec((t, LANES), lambda p, i: (0, p))
    return pl.pallas_call(
        body,
        name=name,
        grid=(n_pairs, t // tb),
        in_specs=[pl.BlockSpec((tb, LANES), lambda p, i: (i, dy0 + p)), qblk, whole, whole, qblk,
                  pl.BlockSpec(order_after.shape, lambda p, i: (0, 0))],
        out_specs=[qblk, whole, whole],
        out_shape=[jax.ShapeDtypeStruct((t, ATTN_W), F32)] * 3,
        scratch_shapes=[pltpu.VMEM((2, tb, LANES), F32)] * 3,
        compiler_params=_params("parallel", "arbitrary"),
    )(dmix, qs, kn, v, lt, order_after)


_SGU_CHUNKS_PER_STEP = 4


def _sgu_rows(t):
    return CHUNK * math.gcd(_SGU_CHUNKS_PER_STEP, t // CHUNK)


def _sgu_weights(w_ref):
    tt = lax.broadcasted_iota(jnp.int32, (CHUNK, CHUNK), 0)
    ss = lax.broadcasted_iota(jnp.int32, (CHUNK, CHUNK), 1)
    tril = ss <= tt
    return [jnp.where(tril, w_ref[gi], 0.0).astype(BF16) for gi in range(SGU_HEADS)], tril


def _sgu_fwd(proj, g_v, w_s, b_exp, *, name):
    t = proj.shape[0]
    u0 = (3 * CONV_W + 3 * ATTN_W) // SGU_W
    rows = _sgu_rows(t)

    def body(u_ref, v_ref, g_ref, w_ref, b_ref, o_ref):
        grp = lax.broadcasted_iota(jnp.int32, (1, SGU_W), 1) // HEAD_DIM
        wm, _ = _sgu_weights(w_ref)
        gain, bias = g_ref[...], b_ref[...]
        for c in range(rows // CHUNK):
            chunk = pl.ds(c * CHUNK, CHUNK)
            u = _gelu(u_ref[chunk, :])
            vv = _gelu(v_ref[chunk, :])
            vn = (vv * lax.rsqrt(_group_mean(vv * vv, SGU_W) + EPS) * gain).astype(BF16)
            sv = bias
            for gi in range(SGU_HEADS):
                sv = sv + jnp.dot(wm[gi], jnp.where(grp == gi, vn, jnp.zeros_like(vn)), preferred_element_type=F32)
            o_ref[chunk, :] = (u * sv).astype(o_ref.dtype)

    return pl.pallas_call(
        body,
        name=name,
        grid=(t // rows,),
        in_specs=[
            pl.BlockSpec((rows, SGU_W), lambda i: (i, u0)),
            pl.BlockSpec((rows, SGU_W), lambda i: (i, u0 + 1)),
            pl.BlockSpec((1, SGU_W), lambda i: (0, 0)),
            pl.BlockSpec((SGU_HEADS, CHUNK, CHUNK), lambda i: (0, 0, 0)),
            pl.BlockSpec((CHUNK, SGU_W), lambda i: (0, 0)),
        ],
        out_specs=pl.BlockSpec((rows, SGU_W), lambda i: (i, 0)),
        out_shape=jax.ShapeDtypeStruct((t, SGU_W), BF16),
        compiler_params=_params("parallel"),
    )(proj, proj, g_v, w_s, b_exp)


def _sgu_bwd(dmix, proj, g_v, w_s, b_exp, *, name):
    t = proj.shape[0]
    u0 = (3 * CONV_W + 3 * ATTN_W) // SGU_W
    dy0 = (CONV_W + ATTN_W) // SGU_W
    rows = _sgu_rows(t)

    def body(dy_ref, u_ref, v_ref, g_ref, w_ref, b_ref, o_ref, dg_ref, dw_ref, db_ref):
        grp = lax.broadcasted_iota(jnp.int32, (1, SGU_W), 1) // HEAD_DIM
        gain, bias = g_ref[...], b_ref[...]
        wm, tril = _sgu_weights(w_ref)

        @pl.when(pl.program_id(0) == 0)
        def _():
            dg_ref[...] = jnp.zeros_like(dg_ref)
            dw_ref[...] = jnp.zeros_like(dw_ref)
            db_ref[...] = jnp.zeros_like(db_ref)

        dg = jnp.zeros_like(gain)
        db = jnp.zeros_like(bias)
        dw = [jnp.zeros((CHUNK, CHUNK), F32) for _ in range(SGU_HEADS)]
        for c in range(rows // CHUNK):
            chunk = pl.ds(c * CHUNK, CHUNK)
            cu, cv = u_ref[chunk, :], v_ref[chunk, :]
            u = _gelu(cu)
            vv = _gelu(cv)
            r = lax.rsqrt(_group_mean(vv * vv, SGU_W) + EPS)
            xhat = vv * r
            vn = (xhat * gain).astype(BF16)
            vng = [jnp.where(grp == gi, vn, jnp.zeros_like(vn)) for gi in range(SGU_HEADS)]
            sv = bias
            for gi in range(SGU_HEADS):
                sv = sv + jnp.dot(wm[gi], vng[gi], preferred_element_type=F32)
            dy = dy_ref[chunk, :]
            o_ref[chunk, 0:SGU_W] = (dy * sv * _gelu_grad(cu)).astype(o_ref.dtype)
            dsv = dy * u
            dsv16 = dsv.astype(BF16)
            db = db + dsv
            dvn = jnp.zeros_like(dsv)
            for gi in range(SGU_HEADS):
                dw[gi] = dw[gi] + lax.dot_general(dsv16, vng[gi], _NT, preferred_element_type=F32)
                dvn_g = lax.dot_general(wm[gi], dsv16, _TN, preferred_element_type=F32)
                dvn = jnp.where(grp == gi, dvn_g, dvn)
            dg = dg + jnp.sum(dvn * xhat, axis=0, keepdims=True)
            dxhat = dvn * gain
            dvv = r * (dxhat - xhat * _group_mean(dxhat * xhat, SGU_W))
            o_ref[chunk, SGU_W:2 * SGU_W] = (dvv * _gelu_grad(cv)).astype(o_ref.dtype)
        dg_ref[...] += dg
        db_ref[...] += db
        for gi in range(SGU_HEADS):
            dw_ref[gi] += jnp.where(tril, dw[gi], 0.0)

    return pl.pallas_call(
        body,
        name=name,
        grid=(t // rows,),
        in_specs=[
            pl.BlockSpec((rows, SGU_W), lambda i: (i, dy0)),
            pl.BlockSpec((rows, SGU_W), lambda i: (i, u0)),
            pl.BlockSpec((rows, SGU_W), lambda i: (i, u0 + 1)),
            pl.BlockSpec((1, SGU_W), lambda i: (0, 0)),
            pl.BlockSpec((SGU_HEADS, CHUNK, CHUNK), lambda i: (0, 0, 0)),
            pl.BlockSpec((CHUNK, SGU_W), lambda i: (0, 0)),
        ],
        out_specs=[
            pl.BlockSpec((rows, 2 * SGU_W), lambda i: (i, 0)),
            pl.BlockSpec((1, SGU_W), lambda i: (0, 0)),
            pl.BlockSpec((SGU_HEADS, CHUNK, CHUNK), lambda i: (0, 0, 0)),
            pl.BlockSpec((CHUNK, SGU_W), lambda i: (0, 0)),
        ],
        out_shape=[
            jax.ShapeDtypeStruct((t, 2 * SGU_W), BF16),
            jax.ShapeDtypeStruct((1, SGU_W), F32),
            jax.ShapeDtypeStruct((SGU_HEADS, CHUNK, CHUNK), F32),
            jax.ShapeDtypeStruct((CHUNK, SGU_W), F32),
        ],
        compiler_params=_params("arbitrary"),
    )(dmix, proj, proj, g_v, w_s, b_exp)


def _ple_bwd(dh, gate, pp, order_after, *, name, br=512):
    t, d = dh.shape
    br = min(br, t)

    def body(dh_ref, g_ref, p_ref, order_ref, dpre_ref, dpp_ref):
        dhv, g = dh_ref[...], g_ref[...]
        dpre_ref[...] = (dhv * p_ref[...] * g * (1.0 - g)).astype(dpre_ref.dtype)
        dpp_ref[...] = (dhv * g).astype(dpp_ref.dtype)

    row = pl.BlockSpec((br, d), lambda i: (i, 0))
    return pl.pallas_call(
        body,
        name=name,
        grid=(t // br,),
        in_specs=[row] * 3 + [pl.BlockSpec(order_after.shape, lambda i: (0, 0))],
        out_specs=[row] * 2,
        out_shape=[jax.ShapeDtypeStruct((t, d), BF16)] * 2,
        compiler_params=_params("parallel"),
    )(dh, gate, pp, order_after)


def _loss_head(y, target, *, name, br=512):
    t, d = y.shape
    br = min(br, t)

    def body(y_ref, t_ref, dy_ref, loss_ref):
        err = y_ref[...] - t_ref[...]
        dy_ref[...] = err * (1.0 / d)

        @pl.when(pl.program_id(0) == 0)
        def _():
            loss_ref[...] = jnp.zeros_like(loss_ref)

        loss_ref[...] += 0.5 * jnp.sum(jnp.sum(err * err, axis=1, keepdims=True) * (1.0 / d), axis=0, keepdims=True)

    row = pl.BlockSpec((br, d), lambda i: (i, 0))
    return pl.pallas_call(
        body,
        name=name,
        grid=(t // br,),
        in_specs=[row, row],
        out_specs=[row, pl.BlockSpec((8, LANES), lambda i: (0, 0))],
        out_shape=[jax.ShapeDtypeStruct((t, d), F32), jax.ShapeDtypeStruct((8, LANES), F32)],
        compiler_params=_params("arbitrary"),
    )(y, target)


def _adamw_update(w, g, m, v):
    nm = ADAM_B1 * m + (1.0 - ADAM_B1) * g
    nv = ADAM_B2 * v + (1.0 - ADAM_B2) * (g * g)
    m_hat = nm / (1.0 - ADAM_B1 ** ADAM_STEP)
    v_hat = nv / (1.0 - ADAM_B2 ** ADAM_STEP)
    return -ADAM_LR * (m_hat / (jnp.sqrt(v_hat) + ADAM_EPS) + ADAM_WD * w), nm, nv


def _adamw(w, g, m, v, *, name, br=512):
    r, c = w.shape
    br = _row_block(r, br)

    def body(w_ref, g_ref, m_ref, v_ref, d_ref, nm_ref, nv_ref):
        d_ref[...], nm_ref[...], nv_ref[...] = _adamw_update(w_ref[...], g_ref[...], m_ref[...], v_ref[...])

    row = pl.BlockSpec((br, c), lambda i: (i, 0))
    return pl.pallas_call(
        body,
        name=name,
        grid=(r // br,),
        in_specs=[row] * 4,
        out_specs=[row] * 3,
        out_shape=[jax.ShapeDtypeStruct((r, c), F32)] * 3,
        compiler_params=_params("parallel"),
    )(w, g, m, v)


def _sum_slots(per_layer, *, name):
    counts = [len(arrays) for arrays in per_layer]
    flat = [a for arrays in per_layer for a in arrays]

    def body(*refs):
        ins, outs = refs[:len(flat)], refs[len(flat):]
        at = 0
        for o_ref, count in zip(outs, counts, strict=True):
            for li in range(count):
                acc = ins[at + li][0]
                for j in range(1, N_DEV):
                    acc = acc + ins[at + li][j]
                o_ref[li] = acc
            at += count

    return pl.pallas_call(
        body,
        name=name,
        out_shape=[jax.ShapeDtypeStruct((len(arrays), *arrays[0].shape[1:]), F32) for arrays in per_layer],
        compiler_params=pltpu.CompilerParams(vmem_limit_bytes=VMEM_LIMIT_BYTES),
    )(*flat)


_ADAMW_BLOCK_ELEMS = 192 * 1024


def _adamw_reduce(w, arrived, m, v, *, name):
    depth, r, c = w.shape
    br = _row_block(r, max(BF16_TILE_ROWS, _ADAMW_BLOCK_ELEMS // (-(-c // LANES) * LANES)))

    def body(w_ref, m_ref, v_ref, *rest):
        parts, (g_ref, d_ref, nm_ref, nv_ref) = rest[:depth], rest[depth:]
        for li in range(depth):

            @pl.when(pl.program_id(0) == li)
            def _(li=li):
                g = parts[li][0].astype(F32)
                for j in range(1, N_DEV):
                    g = g + parts[li][j].astype(F32)
                g_ref[...] = g
                d_ref[...], nm_ref[...], nv_ref[...] = _adamw_update(w_ref[...], g, m_ref[...], v_ref[...])

    cur = pl.BlockSpec((None, br, c), lambda l, i: (l, i, 0))
    slots = [pl.BlockSpec((N_DEV, br, c), lambda l, i, li=li: (0, jnp.where(l == li, i, 0), 0)) for li in range(depth)]
    return pl.pallas_call(
        body,
        name=name,
        grid=(depth, r // br),
        in_specs=[cur, cur, cur] + slots,
        out_specs=[cur] * 4,
        out_shape=[jax.ShapeDtypeStruct((depth, r, c), F32)] * 4,
        compiler_params=_params("arbitrary", "arbitrary"),
    )(w, m, v, *arrived)


def _my_place():
    return lax.axis_index("x"), lax.axis_index("y"), lax.axis_index("c")


def _flip(v, bit):
    return 1 - v if bit else v


def _slot_of(px, py, pc):
    return 4 * px + 2 * py + pc


_ANY = pl.BlockSpec(memory_space=pl.ANY)


_HBM = pl.BlockSpec(memory_space=pltpu.HBM)
_SEM = pl.BlockSpec(memory_space=pltpu.SEMAPHORE)
_DATAFLOW = pltpu.SideEffectType.DATAFLOW_SIDE_EFFECTING


_GATHER, _GATHER_COLUMNS, _SCATTER = "gather", "gather_columns", "scatter"


def _landing_shape(a, mode):
    if mode == _SCATTER:
        return a.shape
    if mode == _GATHER_COLUMNS:
        return (a.shape[0], N_DEV * a.shape[1])
    return (N_DEV, *a.shape)


_DIRECT, _NEAR, _RELAY = "direct", "near", "relay"
_OTHER_CHIPS = (2, 4, 6)
_SIBLING = 1


def _exchange_copies(src_refs, land_refs, send_sem, recv_sem, modes, hops=_DIRECT):
    mx, my, mc = _my_place()
    peer_of = lambda k: (_flip(mx, k & 4), _flip(my, k & 2), _flip(mc, k & 1))
    mine = _slot_of(mx, my, mc)

    def block(land, mode, slot):
        if mode == _GATHER_COLUMNS:
            n = land.shape[1] // N_DEV
            return land.at[:, pl.ds(pl.multiple_of(slot * n, LANES), n)]
        return land.at[slot]

    def remote_copy(src, dst, to):
        return pltpu.make_async_remote_copy(src_ref=src, dst_ref=dst, send_sem=send_sem, recv_sem=recv_sem,
                                            device_id=to, device_id_type=MESH)

    remote, local = [], []
    for src, land, mode in zip(src_refs, land_refs, modes, strict=True):
        if hops == _RELAY:
            assert mode != _SCATTER
            for k in _OTHER_CHIPS:
                came = block(land, mode, _slot_of(*peer_of(k)))
                remote.append(remote_copy(came, came, peer_of(_SIBLING)))
            continue
        dst = block(land, mode, mine)
        for k in ((_SIBLING,) + _OTHER_CHIPS if hops == _NEAR else range(1, N_DEV)):
            remote.append(remote_copy(src.at[_slot_of(*peer_of(k))] if mode == _SCATTER else src, dst, peer_of(k)))
        local.append(pltpu.make_async_copy(src.at[mine] if mode == _SCATTER else src, dst, recv_sem))
    return remote, local


def _wait_copies(remote, local):
    for cp in remote:
        cp.wait_send()
        cp.wait_recv()
    for cp in local:
        cp.wait()


def _exchange_start(groups, after, *, name, hops=_DIRECT):
    sizes = [len(srcs) for srcs, _ in groups]
    n, n_sems = sum(sizes), 2 * len(groups)
    srcs = [a for arrays, _ in groups for a in arrays]
    lands = [lax.empty(_landing_shape(a, mode), a.dtype)
             for arrays, modes in groups for a, mode in zip(arrays, modes, strict=True)]
    offsets = [sum(sizes[:g]) for g in range(len(groups))]

    def body(*refs):
        sems = refs[2 * n + 1:2 * n + 1 + n_sems]
        for g, (off, size, (_, modes)) in enumerate(zip(offsets, sizes, groups)):
            remote, local = _exchange_copies(refs[off:off + size], refs[n + off:n + off + size], sems[2 * g],
                                             sems[2 * g + 1], modes, hops)
            for cp in remote + local:
                cp.start()
        refs[-1][...] = jnp.zeros_like(refs[-1])

    thru = [pltpu.HBM(a.shape, a.dtype) for a in (*srcs, *lands)]
    out = pl.pallas_call(
        body,
        name=name,
        in_specs=[_HBM] * (2 * n) + [_ANY],
        out_specs=(*[_SEM] * n_sems, *[_HBM] * (2 * n), pl.BlockSpec(memory_space=pltpu.VMEM)),
        out_shape=(*[pltpu.SemaphoreType.DMA(())] * n_sems, *thru, jax.ShapeDtypeStruct((8, LANES), F32)),
        input_output_aliases={i: n_sems + i for i in range(2 * n)},
        compiler_params=pltpu.CompilerParams(has_side_effects=_DATAFLOW),
    )(*[pltpu.with_memory_space_constraint(a, pltpu.HBM) for a in (*srcs, *lands)], after)
    sems, arrays = out[:n_sems], out[n_sems:-1]
    started = [(sems[2 * g], sems[2 * g + 1], *arrays[off:off + size], *arrays[n + off:n + off + size])
               for g, (off, size) in enumerate(zip(offsets, sizes))]
    return started, out[-1]


def _exchange_relay(started, after, *, modes, regroup, name):
    send_sem, recv_sem, *thru = started
    n, n_sems = len(thru) // 2, 2 * len(regroup)

    def body(*refs):
        srcs, lands = refs[:n], refs[n:2 * n]
        _wait_copies(*_exchange_copies(srcs, lands, refs[2 * n], refs[2 * n + 1], modes, _NEAR))
        sems = refs[2 * n + 3:2 * n + 3 + n_sems]
        for g, members in enumerate(regroup):
            remote, _ = _exchange_copies([srcs[i] for i in members], [lands[i] for i in members], sems[2 * g],
                                         sems[2 * g + 1], [modes[i] for i in members], _RELAY)
            for cp in remote:
                cp.start()
        refs[-1][...] = jnp.zeros_like(refs[-1])

    out = pl.pallas_call(
        body,
        name=name,
        in_specs=[_HBM] * (2 * n) + [_SEM, _SEM, _ANY],
        out_specs=(*[_SEM] * n_sems, *[_HBM] * (2 * n), pl.BlockSpec(memory_space=pltpu.VMEM)),
        out_shape=(*[pltpu.SemaphoreType.DMA(())] * n_sems, *[pltpu.HBM(a.shape, a.dtype) for a in thru],
                   jax.ShapeDtypeStruct((8, LANES), F32)),
        input_output_aliases={i: n_sems + i for i in range(2 * n)},
        compiler_params=pltpu.CompilerParams(has_side_effects=_DATAFLOW),
    )(*thru, send_sem, recv_sem, after)
    sems, arrays = out[:n_sems], out[n_sems:-1]
    groups = [(sems[2 * g], sems[2 * g + 1], *[arrays[i] for i in members], *[arrays[n + i] for i in members])
              for g, members in enumerate(regroup)]
    return groups, out[-1]


def _exchange_wait(started, after, *, modes, name, hops=_DIRECT):
    send_sem, recv_sem, *thru = started
    n = len(thru) // 2

    def body(*refs):
        _wait_copies(*_exchange_copies(refs[:n], refs[n:2 * n], refs[2 * n], refs[2 * n + 1], modes, hops))

    out = pl.pallas_call(
        body,
        name=name,
        in_specs=[_HBM] * (2 * n) + [_SEM, _SEM, _ANY],
        out_specs=[_HBM] * (2 * n),
        out_shape=[pltpu.HBM(a.shape, a.dtype) for a in thru],
        input_output_aliases={i: i for i in range(2 * n)},
        compiler_params=pltpu.CompilerParams(has_side_effects=_DATAFLOW),
    )(*thru, send_sem, recv_sem, after)
    return out[n:]


def _gather_columns(g):
    return jnp.moveaxis(g, 0, 1).reshape(g.shape[1], -1)


def _split_rows(w):
    return w.reshape(N_DEV, w.shape[0] // N_DEV, w.shape[1])


_FIRST = ("w_in", "conv_w")
_REST = ("w_out", "w_ff1", "w_ff2", "w_ple_gate", "w_ple_proj")
_REST_GROUPS = (("w_out",), ("w_ff1",), ("w_ff2",), ("w_ple_gate", "w_ple_proj"))
_BIG = ("w_in",) + _REST
_GATHER_MODE = dict(w_in=_GATHER, conv_w=_GATHER, w_out=_GATHER, w_ff1=_GATHER_COLUMNS, w_ff2=_GATHER,
                    w_ple_gate=_GATHER, w_ple_proj=_GATHER_COLUMNS)
_RELAYOUT_AFTER_GATHER = ("conv_w",)
_SMALL = ("norm1_g", "q_norm_g", "k_norm_g", "sgu_norm_g", "sgu_w", "sgu_b", "norm2_g", "norm3_g")
_ORDER = ("norm1_g", "w_in", "conv_w", "q_norm_g", "k_norm_g", "sgu_norm_g", "sgu_w", "sgu_b", "w_out", "norm2_g",
          "w_ff1", "w_ff2", "norm3_g", "w_ple_gate", "w_ple_proj")


def _whole_matrices(names, landed):
    return {k: _gather_columns(g) if k in _RELAYOUT_AFTER_GATHER else g.reshape(-1, g.shape[-1])
            for k, g in zip(names, landed, strict=True)}


_NORM_FUSED_ROWS = 256


def _layer_forward(h0, hn1, p16, s, li, w_first, gathered, next_norm_g):
    nm = lambda k: f"{k}_l{li}"
    t, d = h0.shape
    w = dict(w_first)

    def add_and_norm(acc, res, gain):
        h = res + acc
        return h, _rms_rows(h, gain)

    proj = _matmul(hn1, w["w_in"], name=nm("proj"), tb=True, bm=t, bn=256)
    y_a = _conv_fwd(proj, w["conv_w"], name=nm("conv"))
    qs, kn, v = _qk_prep(proj, s["gq"], s["gk"], name=nm("qkprep"))
    y_b, lt = _attn_fwd(qs, kn, v, name=nm("attn"))
    gathered["relay_rest"](y_b)
    y_c = _sgu_fwd(proj, s["sgu_norm_g"], s["sgu_w"], s["b_exp"], name=nm("sgu"))
    mix = jnp.concatenate([y_a, y_b, y_c], axis=1)
    w.update(gathered["fetch"](0, mix))
    h1, hn2 = _matmul(mix, w["w_out"], name=nm("out"), bm=_NORM_FUSED_ROWS, bn=d, out_dtypes=(F32, BF16),
                      extras=(h0,), row_vectors=(s["norm2_g"],), epilogue=add_and_norm)
    w.update(gathered["fetch"](1, hn2))
    f = _matmul(hn2, w["w_ff1"], name=nm("ff1"), bm=t, bn=512, out_dtypes=(BF16,),
                epilogue=lambda acc: (jnp.square(jnp.maximum(acc, 0.0)),))
    w.update(gathered["fetch"](2, f))
    gathered["relay_next"](f)
    h2, hn3 = _matmul(f, w["w_ff2"], name=nm("ff2"), bm=_NORM_FUSED_ROWS, bn=d, out_dtypes=(F32, BF16),
                      extras=(h1,), row_vectors=(s["norm3_g"],), epilogue=add_and_norm)
    w.update(gathered["fetch"](3, hn3))
    w_next = gathered["fetch_next"](hn3)
    pp = _matmul(p16, w["w_ple_proj"], name=nm("pleproj"), bm=t, bn=512)

    def gate_epilogue(acc, pp_blk, h_blk, *gain):
        gate = jax.nn.sigmoid(acc)
        h = h_blk + gate * pp_blk
        return (h, gate) + tuple(_rms_rows(h, g) for g in gain)

    fused_norm = () if next_norm_g is None else (next_norm_g,)
    h3, gate, *hn1_next = _matmul(hn3, w["w_ple_gate"], name=nm("plegate"), bm=_NORM_FUSED_ROWS, bn=d,
                                  out_dtypes=(F32, F32) + (BF16,) * len(fused_norm), extras=(pp, h2),
                                  row_vectors=fused_norm, epilogue=gate_epilogue)
    saved = dict(h0=h0, hn1=hn1, proj=proj, qs=qs, kn=kn, v=v, lt=lt, mix=mix, h1=h1, hn2=hn2, f=f, h2=h2,
                 hn3=hn3, pp=pp, gate=gate, p16=p16)
    return h3, (hn1_next[0] if hn1_next else None), w, w_next, saved


def _layer_backward(dh3, a, w, s, li, order_after, start_rest):
    nm = lambda k: f"{k}_bwd_l{li}"
    t = dh3.shape[0]
    dpre, dpp = _ple_bwd(dh3, a["gate"], a["pp"], order_after, name=nm("ple"))
    g_gate = _weight_grad(a["hn3"], [dpre], name=nm("dwgate"))
    g_proj = _weight_grad(a["p16"], [dpp], name=nm("dwproj"), column_shards=True)
    dh2, dh2_16, g_n3 = _matmul_rms_bwd([dpre], w["w_ple_gate"], a["h2"], s["norm3_g"], dh3, name=nm("dh2"))
    du = _matmul(dh2_16, w["w_ff2"], name=nm("du"), tb=True, bm=t, bn=512, out_dtypes=(BF16,), extras=(a["f"],),
                 epilogue=lambda acc, f: (acc * (2.0 * jnp.sqrt(f.astype(F32))),))
    g_ff2 = _weight_grad(a["f"], [dh2_16], name=nm("dwff2"))
    g_ff1 = _weight_grad(a["hn2"], [du], name=nm("dwff1"), column_shards=True)
    dh1, dh1_16, g_n2 = _matmul_rms_bwd([du], w["w_ff1"], a["h1"], s["norm2_g"], dh2, name=nm("dh1"))
    dmix = _matmul(dh1_16, w["w_out"], name=nm("dmix"), tb=True, bm=t, bn=256)
    g_out = _weight_grad(a["mix"], [dh1_16], name=nm("dwout"))
    started = start_rest(dict(w_out=_split_rows(g_out), w_ff1=g_ff1, w_ff2=_split_rows(g_ff2),
                              w_ple_gate=_split_rows(g_gate), w_ple_proj=g_proj), dmix)
    d_conv, g_conv = _conv_bwd(dmix, a["proj"], w["conv_w"], name=nm("conv"))
    dqs, dkn, dv = _attn_bwd(dmix, a["qs"], a["kn"], a["v"], a["lt"], started, name=nm("attn"))
    d_qkv, g_q, g_k = _qk_prep_bwd(dqs, dkn, dv, a["proj"], s["gq"], s["gk"], name=nm("qkprep"))
    d_sgu, g_sn, g_sw, g_sb = _sgu_bwd(dmix, a["proj"], s["sgu_norm_g"], s["sgu_w"], s["b_exp"], name=nm("sgu"))
    dproj = [d_conv, d_qkv, d_sgu]
    g_in = jnp.concatenate([_weight_grad(piece, [a["hn1"]], name=nm(f"dwin{i}")) for i, piece in enumerate(dproj)])
    dh0, _, g_n1 = _matmul_rms_bwd(dproj, w["w_in"], a["h0"], s["norm1_g"], dh1, name=nm("dh0"), w_is_k_by_d=True)
    small = dict(norm1_g=g_n1, norm2_g=g_n2, norm3_g=g_n3, q_norm_g=g_q, k_norm_g=g_k, sgu_norm_g=g_sn, sgu_w=g_sw,
                 sgu_b=g_sb, conv_w=g_conv)
    return dh0, _split_rows(g_in), small


def _small_gradients(raw, depth):
    return dict(
        norm1_g=raw["norm1_g"].reshape(depth, -1), norm2_g=raw["norm2_g"].reshape(depth, -1),
        norm3_g=raw["norm3_g"].reshape(depth, -1),
        q_norm_g=raw["q_norm_g"].reshape(depth, -1, HEAD_DIM).sum(1),
        k_norm_g=raw["k_norm_g"].reshape(depth, -1, HEAD_DIM).sum(1),
        sgu_norm_g=raw["sgu_norm_g"].reshape(depth, -1), sgu_w=raw["sgu_w"],
        sgu_b=jnp.swapaxes(raw["sgu_b"].reshape(depth, CHUNK, SGU_HEADS, HEAD_DIM).sum(-1), 1, 2),
        conv_w=raw["conv_w"][:, :CONV_TAPS],
    )


def kernel(x, p, norm1_g, w_in, conv_w, q_norm_g, k_norm_g, sgu_norm_g, sgu_w, sgu_b, w_out, norm2_g, w_ff1, w_ff2, norm3_g, w_ple_gate, w_ple_proj, loss_target, m_norm1_g, m_w_in, m_conv_w, m_q_norm_g, m_k_norm_g, m_sgu_norm_g, m_sgu_w, m_sgu_b, m_w_out, m_norm2_g, m_w_ff1, m_w_ff2, m_norm3_g, m_w_ple_gate, m_w_ple_proj, v_norm1_g, v_w_in, v_conv_w, v_q_norm_g, v_k_norm_g, v_sgu_norm_g, v_sgu_w, v_sgu_b, v_w_out, v_norm2_g, v_w_ff1, v_w_ff2, v_norm3_g, v_w_ple_gate, v_w_ple_proj):
    weights = dict(norm1_g=norm1_g, w_in=w_in, conv_w=conv_w, q_norm_g=q_norm_g, k_norm_g=k_norm_g,
                   sgu_norm_g=sgu_norm_g, sgu_w=sgu_w, sgu_b=sgu_b, w_out=w_out, norm2_g=norm2_g, w_ff1=w_ff1,
                   w_ff2=w_ff2, norm3_g=norm3_g, w_ple_gate=w_ple_gate, w_ple_proj=w_ple_proj)
    mom = dict(norm1_g=m_norm1_g, w_in=m_w_in, conv_w=m_conv_w, q_norm_g=m_q_norm_g, k_norm_g=m_k_norm_g,
               sgu_norm_g=m_sgu_norm_g, sgu_w=m_sgu_w, sgu_b=m_sgu_b, w_out=m_w_out, norm2_g=m_norm2_g, w_ff1=m_w_ff1,
               w_ff2=m_w_ff2, norm3_g=m_norm3_g, w_ple_gate=m_w_ple_gate, w_ple_proj=m_w_ple_proj)
    var = dict(norm1_g=v_norm1_g, w_in=v_w_in, conv_w=v_conv_w, q_norm_g=v_q_norm_g, k_norm_g=v_k_norm_g,
               sgu_norm_g=v_sgu_norm_g, sgu_w=v_sgu_w, sgu_b=v_sgu_b, w_out=v_w_out, norm2_g=v_norm2_g, w_ff1=v_w_ff1,
               w_ff2=v_w_ff2, norm3_g=v_norm3_g, w_ple_gate=v_w_ple_gate, w_ple_proj=v_w_ple_proj)
    for params in (weights, mom, var):
        params["w_in"] = jnp.swapaxes(params["w_in"], 1, 2)
    depth = norm1_g.shape[0]
    mx, my, mc = _my_place()
    me = _slot_of(mx, my, mc)

    gathers = []
    modes_of = lambda names: tuple(_GATHER_MODE[k] for k in names)
    token = x[0, :8, :LANES]
    for li in range(depth):
        groups = [([weights[k][li] if k == "conv_w" else weights[k][li].astype(BF16) for k in names], modes_of(names))
                  for names in (_FIRST, _REST)]
        started, token = _exchange_start(groups, token, name=f"gather_start_l{li}", hops=_NEAR)
        gathers.append(started)

    small = []
    for li in range(depth):
        small.append(dict(
            norm1_g=norm1_g[li][None], norm2_g=norm2_g[li][None], norm3_g=norm3_g[li][None],
            gq=jnp.tile(q_norm_g[li], _QK_BLOCK // HEAD_DIM)[None], gk=jnp.tile(k_norm_g[li], _QK_BLOCK // HEAD_DIM)[None],
            sgu_norm_g=sgu_norm_g[li][None], sgu_w=sgu_w[li], b_exp=jnp.repeat(sgu_b[li].T, HEAD_DIM, axis=1),
        ))
    small[0]["norm1_g"] = small[0]["norm1_g"] + token[0, 0]

    h = x[0]
    saved, full = [], []
    relayed_first, relayed_rest = [None] * depth, [None] * depth
    rest_members = [[_REST.index(k) for k in names] for names in _REST_GROUPS]

    def relay_first(li, after):
        if li < depth:
            (relayed_first[li],), _ = _exchange_relay(gathers[li][0], after, modes=modes_of(_FIRST),
                                                      regroup=[list(range(len(_FIRST)))], name=f"gather_first_relay_l{li}")

    def fetch_first(li, after):
        if li == depth:
            return None
        landed = _exchange_wait(relayed_first[li], after, modes=modes_of(_FIRST), hops=_RELAY,
                                name=f"gather_first_wait_l{li}")
        return _whole_matrices(_FIRST, landed)

    hn1 = _rms_fwd(h, small[0]["norm1_g"], name="rms1_l0")
    relay_first(0, hn1)
    w_first = fetch_first(0, hn1)
    for li in range(depth):

        def relay_rest(after, li=li):
            relayed_rest[li], _ = _exchange_relay(gathers[li][1], after, modes=modes_of(_REST), regroup=rest_members,
                                                  name=f"gather_rest_relay_l{li}")

        def fetch(g, after, li=li):
            landed = _exchange_wait(relayed_rest[li][g], after, modes=modes_of(_REST_GROUPS[g]), hops=_RELAY,
                                    name=f"gather_{_REST_GROUPS[g][0]}_wait_l{li}")
            return _whole_matrices(_REST_GROUPS[g], landed)

        gathered = dict(relay_rest=relay_rest, fetch=fetch, relay_next=functools.partial(relay_first, li + 1),
                        fetch_next=functools.partial(fetch_first, li + 1))
        next_norm_g = small[li + 1]["norm1_g"] if li + 1 < depth else None
        h, hn1, w, w_first, acts = _layer_forward(h, hn1, p[li, 0].astype(BF16), small[li], li, w_first, gathered,
                                                  next_norm_g)
        full.append(w)
        saved.append(acts)
    dh, loss_tile = _loss_head(h, loss_target[0], name="loss_head")
    loss = lax.psum(loss_tile[0, 0], ("x", "y", "c"))

    small_names = _SMALL + ("conv_w",)
    scatter_first, scatter_rest = [None] * depth, [None] * depth
    first_modes, rest_modes = (_SCATTER,) + (_GATHER,) * len(small_names), (_SCATTER,) * len(_REST)
    token = loss_tile
    for li in reversed(range(depth)):

        def start_rest(parts, after, li=li):
            (scatter_rest[li],), started = _exchange_start([([parts[k] for k in _REST], rest_modes)], after,
                                                           name=f"scatter_rest_start_l{li}")
            return started

        dh, g_in, small_grads = _layer_backward(dh, saved[li], full[li], small[li], li, token, start_rest)
        (scatter_first[li],), token = _exchange_start(
            [([g_in] + [small_grads[k] for k in small_names], first_modes)], dh, name=f"scatter_first_start_l{li}")
    grad_x = dh[None]

    grads, delta, new_m, new_v = {}, {}, {}, {}
    arrived = {k: [None] * depth for k in _BIG}
    for li in reversed(range(depth)):
        landed = _exchange_wait(scatter_rest[li], token, modes=rest_modes, name=f"scatter_rest_wait_l{li}")
        for k, g in zip(_REST, landed, strict=True):
            arrived[k][li] = g
    for k in _REST:
        grads[k], delta[k], new_m[k], new_v[k] = _adamw_reduce(weights[k], arrived[k], mom[k], var[k], name=f"adamw_{k}")
    small_parts = {k: [None] * depth for k in small_names}
    updated = jnp.stack([delta[k][0, 0, :1] for k in _REST])
    for li in reversed(range(depth)):
        arrived["w_in"][li], *parts = _exchange_wait(scatter_first[li], updated, modes=first_modes,
                                                     name=f"scatter_first_wait_l{li}")
        for k, part in zip(small_names, parts, strict=True):
            small_parts[k][li] = part
    grads["w_in"], delta["w_in"], new_m["w_in"], new_v["w_in"] = _adamw_reduce(
        weights["w_in"], arrived["w_in"], mom["w_in"], var["w_in"], name="adamw_w_in")
    for results in (grads, delta, new_m, new_v):
        results["w_in"] = jnp.swapaxes(results["w_in"], 1, 2)
    sums = _sum_slots([small_parts[k] for k in small_names], name="sum_small_grads")
    grads.update(_small_gradients(dict(zip(small_names, sums)), depth))
    n_conv = conv_w.shape[2]
    grads["conv_w"] = lax.dynamic_slice_in_dim(grads["conv_w"], me * n_conv, n_conv, axis=2)
    for k in small_names:
        as_rows = lambda a: a.reshape(-1, a.shape[-1])
        outs = _adamw(as_rows(weights[k]), as_rows(grads[k]), as_rows(mom[k]), as_rows(var[k]), name=f"adamw_{k}")
        delta[k], new_m[k], new_v[k] = (o.reshape(weights[k].shape) for o in outs)

    return (loss, grad_x, *[grads[k] for k in _ORDER], *[delta[k] for k in _ORDER],
            *[new_m[k] for k in _ORDER], *[new_v[k] for k in _ORDER])
```

```python
import functools
import math

import jax
import jax.numpy as jnp
from jax import lax
from jax.experimental import pallas as pl
from jax.experimental.pallas import tpu as pltpu

F32 = jnp.float32
BF16 = jnp.bfloat16

N_DEV = 8
HEAD_DIM = 64
CONV_W = 256
ATTN_W = 512
SGU_W = 256
SGU_HEADS = 4
CHUNK = 128
CONV_TAPS = 3
EPS = 1e-6
QK_SCALE = HEAD_DIM ** -0.5

ADAM_LR = 0.001
ADAM_B1 = 0.9
ADAM_B2 = 0.999
ADAM_EPS = 1e-08
ADAM_WD = 0.01
ADAM_STEP = 10

LANES = 128
BF16_TILE_ROWS = 16
VMEM_LIMIT_BYTES = 56 * 1024 * 1024
MESH = pl.DeviceIdType.MESH


def _params(*sem):
    return pltpu.CompilerParams(dimension_semantics=sem, vmem_limit_bytes=VMEM_LIMIT_BYTES)


def _row_block(rows, cap):
    if rows <= cap:
        return rows
    return max(b for b in range(BF16_TILE_ROWS, cap + 1, BF16_TILE_ROWS) if rows % b == 0)


def _matmul(a, b, *, name, tb=False, bm=512, bn=512, out_dtypes=(F32,), epilogue=None, extras=(), row_vectors=()):
    m, k = a.shape
    n = b.shape[0] if tb else b.shape[1]
    assert k == (b.shape[1] if tb else b.shape[0])
    bm, bn = min(bm, m), min(bn, n)
    assert m % bm == 0 and n % bn == 0
    a_spec = pl.BlockSpec((bm, k), lambda i, j: (i, 0))
    b_spec = pl.BlockSpec((bn, k), lambda i, j: (j, 0)) if tb else pl.BlockSpec((k, bn), lambda i, j: (0, j))
    dims = (((1,), (1 if tb else 0,)), ((), ()))
    n_in = len(extras) + len(row_vectors)
    for e in extras:
        assert e.shape == (m, n), (e.shape, m, n)
    for e in row_vectors:
        assert e.shape == (1, n), (e.shape, n)

    def body(a_ref, b_ref, *rest):
        outs = rest[n_in:]
        acc = lax.dot_general(a_ref[...], b_ref[...], dims, preferred_element_type=F32)
        res = (acc,) if epilogue is None else epilogue(acc, *[e[...] for e in rest[:n_in]])
        for o_ref, r in zip(outs, res, strict=True):
            o_ref[...] = r.astype(o_ref.dtype)

    tile = pl.BlockSpec((bm, bn), lambda i, j: (i, j))
    vec = pl.BlockSpec((1, bn), lambda i, j: (0, j))
    out = pl.pallas_call(
        body,
        name=name,
        grid=(m // bm, n // bn),
        in_specs=[a_spec, b_spec] + [tile] * len(extras) + [vec] * len(row_vectors),
        out_specs=[tile] * len(out_dtypes),
        out_shape=[jax.ShapeDtypeStruct((m, n), d) for d in out_dtypes],
        compiler_params=_params("parallel", "parallel"),
    )(a, b, *extras, *row_vectors)
    return out[0] if len(out_dtypes) == 1 else out


def _rms_rows(x, g):
    return x * lax.rsqrt(jnp.mean(x * x, axis=-1, keepdims=True) + EPS) * g


_WEIGHT_GRAD_ACC_ELEMS = 1024 * 1024


def _weight_grad(x, dys, *, name, column_shards=False):
    t, m = x.shape
    n = sum(dy.shape[1] for dy in dys)
    bm = m if m <= 2 * LANES else min(m // 2, max(LANES, _WEIGHT_GRAD_ACC_ELEMS // n // LANES * LANES))
    assert m % bm == 0
    ns = n // N_DEV

    def body(x_ref, *rest):
        o_ref = rest[-1]
        xb = x_ref[...]
        acc = jnp.concatenate([lax.dot_general(xb, dy_ref[...], _TN, preferred_element_type=F32) for dy_ref in rest[:-1]],
                              axis=1)
        if column_shards:
            for s in range(N_DEV):
                o_ref[s] = acc[:, s * ns:(s + 1) * ns].astype(o_ref.dtype)
        else:
            o_ref[...] = acc.astype(o_ref.dtype)

    if column_shards:
        out_spec, out_dims = pl.BlockSpec((N_DEV, bm, ns), lambda i: (0, i, 0)), (N_DEV, m, ns)
    else:
        out_spec, out_dims = pl.BlockSpec((bm, n), lambda i: (i, 0)), (m, n)
    return pl.pallas_call(
        body,
        name=name,
        grid=(m // bm,),
        in_specs=[pl.BlockSpec((t, bm), lambda i: (0, i))] + [pl.BlockSpec(dy.shape, lambda i: (0, 0)) for dy in dys],
        out_specs=out_spec,
        out_shape=jax.ShapeDtypeStruct(out_dims, BF16),
        compiler_params=_params("parallel"),
    )(x, *dys)


def _rms_fwd(h, g, *, name, br=512):
    t, d = h.shape
    br = min(br, t)

    def body(h_ref, g_ref, o_ref):
        o_ref[...] = _rms_rows(h_ref[...], g_ref[...]).astype(o_ref.dtype)

    return pl.pallas_call(
        body,
        name=name,
        grid=(t // br,),
        in_specs=[pl.BlockSpec((br, d), lambda i: (i, 0)), pl.BlockSpec((1, d), lambda i: (0, 0))],
        out_specs=pl.BlockSpec((br, d), lambda i: (i, 0)),
        out_shape=jax.ShapeDtypeStruct((t, d), BF16),
        compiler_params=_params("parallel"),
    )(h, g)


def _matmul_rms_bwd(dzs, w, h, g, dres, *, name, w_is_k_by_d=False):
    t, d = h.shape
    widths = [dz.shape[1] for dz in dzs]
    k = sum(widths)
    assert w.shape == ((k, d) if w_is_k_by_d else (d, k))
    br = min(t, 512 if k <= d else 256)
    n_dz = len(dzs)

    def body(*refs):
        w_ref, h_ref, g_ref, dres_ref, dh_ref, dh16_ref, dg_ref = refs[n_dz:]
        x = h_ref[...]
        dyv, at = None, 0
        for dz_ref, width in zip(refs[:n_dz], widths):
            if w_is_k_by_d:
                part = jnp.dot(dz_ref[...], w_ref[at:at + width, :], preferred_element_type=F32)
            else:
                part = lax.dot_general(dz_ref[...], w_ref[:, at:at + width], _NT, preferred_element_type=F32)
            dyv = part if dyv is None else dyv + part
            at += width
        r = lax.rsqrt(jnp.mean(x * x, axis=-1, keepdims=True) + EPS)
        xhat = x * r
        dxhat = dyv * g_ref[...]
        dh = dres_ref[...] + r * (dxhat - xhat * jnp.mean(dxhat * xhat, axis=-1, keepdims=True))
        dh_ref[...] = dh
        dh16_ref[...] = dh.astype(dh16_ref.dtype)

        @pl.when(pl.program_id(0) == 0)
        def _():
            dg_ref[...] = jnp.zeros_like(dg_ref)

        dg_ref[...] += jnp.sum(dyv * xhat, axis=0, keepdims=True)

    row = pl.BlockSpec((br, d), lambda i: (i, 0))
    vec = pl.BlockSpec((1, d), lambda i: (0, 0))
    return pl.pallas_call(
        body,
        name=name,
        grid=(t // br,),
        in_specs=[pl.BlockSpec((br, width), lambda i: (i, 0)) for width in widths]
        + [pl.BlockSpec(w.shape, lambda i: (0, 0)), row, vec, row],
        out_specs=[row, row, vec],
        out_shape=[jax.ShapeDtypeStruct((t, d), F32), jax.ShapeDtypeStruct((t, d), BF16),
                   jax.ShapeDtypeStruct((1, d), F32)],
        compiler_params=_params("arbitrary"),
    )(*dzs, w, h, g, dres)


def _group_mean(x, width):
    grp = lax.broadcasted_iota(jnp.int32, x.shape, 1) // HEAD_DIM
    out = jnp.zeros_like(x)
    for gi in range(width // HEAD_DIM):
        m = grp == gi
        s = jnp.sum(jnp.where(m, x, 0.0), axis=1, keepdims=True)
        out = jnp.where(m, s, out)
    return out * (1.0 / HEAD_DIM)


def _gelu(x):
    return 0.5 * x * (1.0 + lax.erf(x * (2.0 ** -0.5)))


def _gelu_grad(x):
    cdf = 0.5 * (1.0 + lax.erf(x * (2.0 ** -0.5)))
    pdf = jnp.exp(-0.5 * x * x) * (1.0 / math.sqrt(2.0 * math.pi))
    return cdf + x * pdf


def _shift_down(z, s, row):
    return jnp.where(row >= s, pltpu.roll(z, s, 0), 0.0)


def _shift_up(z, s, row, t):
    return jnp.where(row < t - s, pltpu.roll(z, t - s, 0), 0.0)


def _conv_fwd(proj, conv_w, *, name):
    t = proj.shape[0]
    nb = CONV_W // LANES

    def body(b_ref, c_ref, h_ref, w_ref, o_ref):
        row = lax.broadcasted_iota(jnp.int32, (t, LANES), 0)
        z = c_ref[...] * h_ref[...]
        w = w_ref[...]
        conv = w[2:3, :] * z + w[1:2, :] * _shift_down(z, 1, row) + w[0:1, :] * _shift_down(z, 2, row)
        o_ref[...] = (b_ref[...] * conv).astype(o_ref.dtype)

    return pl.pallas_call(
        body,
        name=name,
        grid=(nb,),
        in_specs=[
            pl.BlockSpec((t, LANES), lambda j: (0, j)),
            pl.BlockSpec((t, LANES), lambda j: (0, nb + j)),
            pl.BlockSpec((t, LANES), lambda j: (0, 2 * nb + j)),
            pl.BlockSpec((CONV_TAPS, LANES), lambda j: (0, j)),
        ],
        out_specs=pl.BlockSpec((t, LANES), lambda j: (0, j)),
        out_shape=jax.ShapeDtypeStruct((t, CONV_W), BF16),
        compiler_params=_params("parallel"),
    )(proj, proj, proj, conv_w)


def _conv_bwd(dmix, proj, conv_w, *, name):
    t = proj.shape[0]

    def body(dy_ref, b_ref, c_ref, h_ref, w_ref, o_ref, dw_ref):
        row = lax.broadcasted_iota(jnp.int32, (t, CONV_W), 0)
        ac, ah = c_ref[...], h_ref[...]
        z = ac * ah
        w = w_ref[...]
        z1 = _shift_down(z, 1, row)
        z2 = _shift_down(z, 2, row)
        conv = w[2:3, :] * z + w[1:2, :] * z1 + w[0:1, :] * z2
        dy = dy_ref[...]
        o_ref[:, 0:CONV_W] = (dy * conv).astype(o_ref.dtype)
        dconv = dy * b_ref[...]
        dz = w[2:3, :] * dconv + w[1:2, :] * _shift_up(dconv, 1, row, t) + w[0:1, :] * _shift_up(dconv, 2, row, t)
        o_ref[:, CONV_W:2 * CONV_W] = (dz * ah).astype(o_ref.dtype)
        o_ref[:, 2 * CONV_W:3 * CONV_W] = (dz * ac).astype(o_ref.dtype)
        dw_ref[...] = jnp.zeros_like(dw_ref)
        dw_ref[0:1, :] = jnp.sum(dconv * z2, axis=0, keepdims=True)
        dw_ref[1:2, :] = jnp.sum(dconv * z1, axis=0, keepdims=True)
        dw_ref[2:3, :] = jnp.sum(dconv * z, axis=0, keepdims=True)

    col = lambda j: pl.BlockSpec((t, CONV_W), lambda i: (0, j))
    return pl.pallas_call(
        body,
        name=name,
        grid=(1,),
        in_specs=[col(0), col(0), col(1), col(2), pl.BlockSpec((CONV_TAPS, CONV_W), lambda i: (0, 0))],
        out_specs=[pl.BlockSpec((t, 3 * CONV_W), lambda i: (0, 0)), pl.BlockSpec((8, CONV_W), lambda i: (0, 0))],
        out_shape=[jax.ShapeDtypeStruct((t, 3 * CONV_W), BF16), jax.ShapeDtypeStruct((8, CONV_W), F32)],
        compiler_params=_params("arbitrary"),
    )(dmix, proj, proj, proj, conv_w)


_QK_BLOCK = 256


def _qk_prep(proj, gq, gk, *, name, br=512):
    t = proj.shape[0]
    br = min(br, t)
    nb = ATTN_W // _QK_BLOCK
    q0 = (3 * CONV_W) // _QK_BLOCK

    def body(q_ref, k_ref, v_ref, gq_ref, gk_ref, qo_ref, ko_ref, vo_ref):
        q = q_ref[...]
        k = k_ref[...]
        rq = lax.rsqrt(_group_mean(q * q, _QK_BLOCK) + EPS)
        rk = lax.rsqrt(_group_mean(k * k, _QK_BLOCK) + EPS)
        qo_ref[...] = ((q * rq * gq_ref[...]).astype(BF16) * QK_SCALE).astype(qo_ref.dtype)
        ko_ref[...] = (k * rk * gk_ref[...]).astype(ko_ref.dtype)
        vo_ref[...] = v_ref[...].astype(vo_ref.dtype)

    col = lambda off: pl.BlockSpec((br, _QK_BLOCK), lambda i, j: (i, off + j))
    vec = pl.BlockSpec((1, _QK_BLOCK), lambda i, j: (0, 0))
    return pl.pallas_call(
        body,
        name=name,
        grid=(t // br, nb),
        in_specs=[col(q0), col(q0 + nb), col(q0 + 2 * nb), vec, vec],
        out_specs=[col(0)] * 3,
        out_shape=[jax.ShapeDtypeStruct((t, ATTN_W), BF16)] * 3,
        compiler_params=_params("parallel", "parallel"),
    )(proj, proj, proj, gq, gk)


def _qk_prep_bwd(dqs, dkn, dv, proj, gq, gk, *, name, br=256):
    t = proj.shape[0]
    br = min(br, t)
    nb = ATTN_W // _QK_BLOCK
    q0 = (3 * CONV_W) // _QK_BLOCK

    def norm_bwd(dy, x, g):
        r = lax.rsqrt(_group_mean(x * x, ATTN_W) + EPS)
        xhat = x * r
        dxhat = dy * g
        dx = r * (dxhat - xhat * _group_mean(dxhat * xhat, ATTN_W))
        return dx, jnp.sum(dy * xhat, axis=0, keepdims=True)

    def body(dq_ref, dk_ref, dv_ref, *rest):
        x_refs, (gq_ref, gk_ref, o_ref, dgq_ref, dgk_ref) = rest[:2 * nb], rest[2 * nb:]
        whole = lambda refs: jnp.concatenate([r[...] for r in refs], axis=1)
        dq, dgq = norm_bwd(dq_ref[...] * QK_SCALE, whole(x_refs[:nb]), whole([gq_ref] * nb))
        dk, dgk = norm_bwd(dk_ref[...], whole(x_refs[nb:]), whole([gk_ref] * nb))
        o_ref[:, 0:ATTN_W] = dq.astype(o_ref.dtype)
        o_ref[:, ATTN_W:2 * ATTN_W] = dk.astype(o_ref.dtype)
        o_ref[:, 2 * ATTN_W:3 * ATTN_W] = dv_ref[...].astype(o_ref.dtype)

        @pl.when(pl.program_id(0) == 0)
        def _():
            dgq_ref[...] = jnp.zeros_like(dgq_ref)
            dgk_ref[...] = jnp.zeros_like(dgk_ref)

        dgq_ref[...] += dgq
        dgk_ref[...] += dgk

    rows = pl.BlockSpec((br, ATTN_W), lambda i: (i, 0))
    col = lambda j: pl.BlockSpec((br, _QK_BLOCK), lambda i: (i, j))
    gain = pl.BlockSpec((1, _QK_BLOCK), lambda i: (0, 0))
    total = pl.BlockSpec((1, ATTN_W), lambda i: (0, 0))
    return pl.pallas_call(
        body,
        name=name,
        grid=(t // br,),
        in_specs=[rows, rows, rows] + [col(q0 + j) for j in range(2 * nb)] + [gain, gain],
        out_specs=[pl.BlockSpec((br, 3 * ATTN_W), lambda i: (i, 0)), total, total],
        out_shape=[jax.ShapeDtypeStruct((t, 3 * ATTN_W), BF16)] + [jax.ShapeDtypeStruct((1, ATTN_W), F32)] * 2,
        compiler_params=_params("arbitrary"),
    )(dqs, dkn, dv, *[proj] * (2 * nb), gq, gk)


def _key_order_matrix(tb, relation):
    jj = lax.broadcasted_iota(jnp.int32, (tb, tb), 0)
    ss = lax.broadcasted_iota(jnp.int32, (tb, tb), 1)
    return relation(jj, ss).astype(BF16)


def _log_sigmoids(z):
    lb = jnp.minimum(z, 0.0) - jnp.log(1.0 + jnp.exp(-jnp.abs(z)))
    return lb, lb - z


def _below_diagonal(tb):
    return lax.broadcasted_iota(jnp.int32, (tb, tb), 1) < lax.broadcasted_iota(jnp.int32, (tb, tb), 0)


_NT = (((1,), (1,)), ((), ()))
_TN = (((0,), (0,)), ((), ()))
_ATTN_BLOCK = 256
_ATTN_FWD_UNROLL = 4
_ATTN_BWD_UNROLL = 3


def _attn_fwd(qs, kn, v, *, name, tb=_ATTN_BLOCK, unroll=_ATTN_FWD_UNROLL):
    t = qs.shape[0]
    tb = min(tb, t)
    assert t % tb == 0
    n_pairs = ATTN_W // LANES

    def body(q_ref, k_ref, v_ref, o_ref, lt_ref, acc_ref, carry_ref):
        qb = pl.program_id(1)
        half = lax.broadcasted_iota(jnp.int32, (1, LANES), 1) // HEAD_DIM
        later = _key_order_matrix(tb, lambda j, s: j > s)
        acc_ref[...] = jnp.zeros_like(acc_ref)
        carry_ref[...] = jnp.zeros_like(carry_ref)
        q = q_ref[...]
        qh = [jnp.where(half == h, q, jnp.zeros_like(q)) for h in range(2)]

        def tiles(kbs, first_is_diagonal):
            blk = []
            for kb in kbs:
                start = pl.multiple_of(kb * tb, tb)
                blk.append((k_ref[pl.ds(start, tb), :], v_ref[pl.ds(start, tb), :]))
            chains = [(h, j) for j in range(len(kbs)) for h in range(2)]
            masked = [first_is_diagonal and j == 0 for _, j in chains]
            z = [lax.dot_general(qh[h], blk[j][0], _NT, preferred_element_type=F32) for h, j in chains]
            causal = _below_diagonal(tb) if first_is_diagonal else None
            lb, lr = [], []
            for zi, mask in zip(z, masked):
                b, r = _log_sigmoids(zi)
                lb.append(b)
                lr.append(jnp.where(causal, r, 0.0) if mask else r)
            suffix = [jnp.dot(r.astype(BF16), later, preferred_element_type=F32) for r in lr]
            carry = [carry_ref[0], carry_ref[1]]
            w = []
            for i, (h, j) in enumerate(chains):
                wi = jnp.exp(lb[i] + (suffix[i] + carry[h][:, 0:1]))
                w.append((jnp.where(causal, wi, 0.0) if masked[i] else wi).astype(BF16))
                carry[h] = carry[h] + jnp.sum(lr[i], axis=1, keepdims=True)
            for i, (h, j) in enumerate(chains):
                vh = jnp.where(half == h, blk[j][1], jnp.zeros_like(blk[j][1]))
                acc_ref[h] += jnp.dot(w[i], vh, preferred_element_type=F32)
            carry_ref[0] = carry[0]
            carry_ref[1] = carry[1]

        @pl.when(qb == 0)
        def _():
            tiles([qb], True)

        @pl.when(qb > 0)
        def _():
            tiles([qb, qb - 1], True)
            rest = qb - 1

            def step(i, _):
                kb = rest - 1 - unroll * i
                tiles([kb - u for u in range(unroll)], False)
                return 0

            lax.fori_loop(0, rest // unroll, step, 0)
            for left in range(1, unroll):

                @pl.when(rest % unroll == left)
                def _(left=left):
                    tiles([left - 1 - u for u in range(left)], False)

        o_ref[...] = (acc_ref[0] + acc_ref[1]).astype(o_ref.dtype)
        lt_ref[...] = jnp.where(half == 0, carry_ref[0], carry_ref[1])

    return pl.pallas_call(
        body,
        name=name,
        grid=(n_pairs, t // tb),
        in_specs=[
            pl.BlockSpec((tb, LANES), lambda p, i: (i, p)),
            pl.BlockSpec((t, LANES), lambda p, i: (0, p)),
            pl.BlockSpec((t, LANES), lambda p, i: (0, p)),
        ],
        out_specs=[pl.BlockSpec((tb, LANES), lambda p, i: (i, p))] * 2,
        out_shape=[jax.ShapeDtypeStruct((t, ATTN_W), BF16), jax.ShapeDtypeStruct((t, ATTN_W), F32)],
        scratch_shapes=[pltpu.VMEM((2, tb, LANES), F32), pltpu.VMEM((2, tb, LANES), F32)],
        compiler_params=_params("parallel", "parallel"),
    )(qs, kn, v)


def _attn_bwd(dmix, qs, kn, v, lt, order_after, *, name, tb=_ATTN_BLOCK, unroll=_ATTN_BWD_UNROLL):
    t = qs.shape[0]
    tb = min(tb, t)
    assert t % tb == 0
    n_pairs = ATTN_W // LANES
    dy0 = CONV_W // LANES

    def body(do_ref, q_ref, k_ref, v_ref, lt_ref, order_ref, dq_ref, dk_ref, dv_ref, dqacc_ref, cc_ref, cg_ref):
        qb = pl.program_id(1)
        half = lax.broadcasted_iota(jnp.int32, (1, LANES), 1) // HEAD_DIM
        lane = lax.broadcasted_iota(jnp.int32, (tb, LANES), 1)
        later = _key_order_matrix(tb, lambda j, s: j > s)
        before = _key_order_matrix(tb, lambda j, s: j < s)
        q = q_ref[...]
        do = do_ref[...].astype(BF16)
        lt = lt_ref[...]
        qh = [jnp.where(half == h, q, jnp.zeros_like(q)) for h in range(2)]
        doh = [jnp.where(half == h, do, jnp.zeros_like(do)) for h in range(2)]
        lth = [jnp.sum(jnp.where(lane == h * HEAD_DIM, lt, 0.0), axis=1, keepdims=True) for h in range(2)]

        @pl.when(qb == 0)
        def _():
            dk_ref[...] = jnp.zeros_like(dk_ref)
            dv_ref[...] = jnp.zeros_like(dv_ref)

        dqacc_ref[...] = jnp.zeros_like(dqacc_ref)
        cc_ref[...] = jnp.zeros_like(cc_ref)
        cg_ref[...] = jnp.zeros_like(cg_ref)

        def tiles(kbs, last_is_diagonal):
            starts = [pl.multiple_of(kb * tb, tb) for kb in kbs]
            blk = [(k_ref[pl.ds(s, tb), :], v_ref[pl.ds(s, tb), :]) for s in starts]
            chains = [(h, j) for j in range(len(kbs)) for h in range(2)]
            masked = [last_is_diagonal and j == len(kbs) - 1 for _, j in chains]
            z = [lax.dot_general(qh[h], blk[j][0], _NT, preferred_element_type=F32) for h, j in chains]
            da = [lax.dot_general(doh[h], jnp.where(half == h, blk[j][1], jnp.zeros_like(blk[j][1])), _NT,
                                  preferred_element_type=F32) for h, j in chains]
            causal = _below_diagonal(tb) if last_is_diagonal else None
            lb, lr = [], []
            for zi, mask in zip(z, masked):
                b, r = _log_sigmoids(zi)
                lb.append(b)
                lr.append(jnp.where(causal, r, 0.0) if mask else r)
            suffix = [jnp.dot(r.astype(BF16), later, preferred_element_type=F32) for r in lr]
            cc = [cc_ref[0], cc_ref[1]]
            cg = [cg_ref[0], cg_ref[1]]
            a16, g = [], []
            for i, (h, j) in enumerate(chains):
                cc[h] = cc[h] + jnp.sum(lr[i], axis=1, keepdims=True)
                a = jnp.exp(lb[i] + suffix[i] + (lth[h] - cc[h][:, 0:1]))
                if masked[i]:
                    a = jnp.where(causal, a, 0.0)
                a16.append(a.astype(BF16))
                g.append(da[i] * a)
            g_before = [jnp.dot(gi.astype(BF16), before, preferred_element_type=F32) for gi in g]
            dz = []
            for i, (h, j) in enumerate(chains):
                dzi = g[i] - jnp.exp(lb[i]) * (g[i] + (g_before[i] + cg[h][:, 0:1]))
                dz.append((jnp.where(causal, dzi, 0.0) if masked[i] else dzi).astype(BF16))
                cg[h] = cg[h] + jnp.sum(g[i], axis=1, keepdims=True)
            for i, (h, j) in enumerate(chains):
                kh = jnp.where(half == h, blk[j][0], jnp.zeros_like(blk[j][0]))
                dqacc_ref[h] += jnp.dot(dz[i], kh, preferred_element_type=F32)
                dk_ref[pl.ds(starts[j], tb), :] += lax.dot_general(dz[i], qh[h], _TN, preferred_element_type=F32)
                dv_ref[pl.ds(starts[j], tb), :] += lax.dot_general(a16[i], doh[h], _TN, preferred_element_type=F32)
            for h in range(2):
                cc_ref[h] = cc[h]
                cg_ref[h] = cg[h]

        @pl.when(qb == 0)
        def _():
            tiles([qb], True)

        @pl.when(qb > 0)
        def _():
            rest = qb - 1

            def step(i, _):
                kb = unroll * i
                tiles([kb + u for u in range(unroll)], False)
                return 0

            lax.fori_loop(0, rest // unroll, step, 0)
            for left in range(1, unroll):

                @pl.when(rest % unroll == left)
                def _(left=left):
                    tiles([rest - left + u for u in range(left)], False)

            tiles([qb - 1, qb], True)

        dq_ref[...] = dqacc_ref[0] + dqacc_ref[1]

    qblk = pl.BlockSpec((tb, LANES), lambda p, i: (i, p))
    whole = pl.BlockSpec((t, LANES), lambda p, i: (0, p))
    return pl.pallas_call(
        body,
        name=name,
        grid=(n_pairs, t // tb),
        in_specs=[pl.BlockSpec((tb, LANES), lambda p, i: (i, dy0 + p)), qblk, whole, whole, qblk,
                  pl.BlockSpec(order_after.shape, lambda p, i: (0, 0))],
        out_specs=[qblk, whole, whole],
        out_shape=[jax.ShapeDtypeStruct((t, ATTN_W), F32)] * 3,
        scratch_shapes=[pltpu.VMEM((2, tb, LANES), F32)] * 3,
        compiler_params=_params("parallel", "arbitrary"),
    )(dmix, qs, kn, v, lt, order_after)


_SGU_CHUNKS_PER_STEP = 4


def _sgu_rows(t):
    return CHUNK * math.gcd(_SGU_CHUNKS_PER_STEP, t // CHUNK)


def _sgu_weights(w_ref):
    tt = lax.broadcasted_iota(jnp.int32, (CHUNK, CHUNK), 0)
    ss = lax.broadcasted_iota(jnp.int32, (CHUNK, CHUNK), 1)
    tril = ss <= tt
    return [jnp.where(tril, w_ref[gi], 0.0).astype(BF16) for gi in range(SGU_HEADS)], tril


def _sgu_fwd(proj, g_v, w_s, b_exp, *, name):
    t = proj.shape[0]
    u0 = (3 * CONV_W + 3 * ATTN_W) // SGU_W
    rows = _sgu_rows(t)

    def body(u_ref, v_ref, g_ref, w_ref, b_ref, o_ref):
        grp = lax.broadcasted_iota(jnp.int32, (1, SGU_W), 1) // HEAD_DIM
        wm, _ = _sgu_weights(w_ref)
        gain, bias = g_ref[...], b_ref[...]
        for c in range(rows // CHUNK):
            chunk = pl.ds(c * CHUNK, CHUNK)
            u = _gelu(u_ref[chunk, :])
            vv = _gelu(v_ref[chunk, :])
            vn = (vv * lax.rsqrt(_group_mean(vv * vv, SGU_W) + EPS) * gain).astype(BF16)
            sv = bias
            for gi in range(SGU_HEADS):
                sv = sv + jnp.dot(wm[gi], jnp.where(grp == gi, vn, jnp.zeros_like(vn)), preferred_element_type=F32)
            o_ref[chunk, :] = (u * sv).astype(o_ref.dtype)

    return pl.pallas_call(
        body,
        name=name,
        grid=(t // rows,),
        in_specs=[
            pl.BlockSpec((rows, SGU_W), lambda i: (i, u0)),
            pl.BlockSpec((rows, SGU_W), lambda i: (i, u0 + 1)),
            pl.BlockSpec((1, SGU_W), lambda i: (0, 0)),
            pl.BlockSpec((SGU_HEADS, CHUNK, CHUNK), lambda i: (0, 0, 0)),
            pl.BlockSpec((CHUNK, SGU_W), lambda i: (0, 0)),
        ],
        out_specs=pl.BlockSpec((rows, SGU_W), lambda i: (i, 0)),
        out_shape=jax.ShapeDtypeStruct((t, SGU_W), BF16),
        compiler_params=_params("parallel"),
    )(proj, proj, g_v, w_s, b_exp)


def _sgu_bwd(dmix, proj, g_v, w_s, b_exp, *, name):
    t = proj.shape[0]
    u0 = (3 * CONV_W + 3 * ATTN_W) // SGU_W
    dy0 = (CONV_W + ATTN_W) // SGU_W
    rows = _sgu_rows(t)

    def body(dy_ref, u_ref, v_ref, g_ref, w_ref, b_ref, o_ref, dg_ref, dw_ref, db_ref):
        grp = lax.broadcasted_iota(jnp.int32, (1, SGU_W), 1) // HEAD_DIM
        gain, bias = g_ref[...], b_ref[...]
        wm, tril = _sgu_weights(w_ref)

        @pl.when(pl.program_id(0) == 0)
        def _():
            dg_ref[...] = jnp.zeros_like(dg_ref)
            dw_ref[...] = jnp.zeros_like(dw_ref)
            db_ref[...] = jnp.zeros_like(db_ref)

        dg = jnp.zeros_like(gain)
        db = jnp.zeros_like(bias)
        dw = [jnp.zeros((CHUNK, CHUNK), F32) for _ in range(SGU_HEADS)]
        for c in range(rows // CHUNK):
            chunk = pl.ds(c * CHUNK, CHUNK)
            cu, cv = u_ref[chunk, :], v_ref[chunk, :]
            u = _gelu(cu)
            vv = _gelu(cv)
            r = lax.rsqrt(_group_mean(vv * vv, SGU_W) + EPS)
            xhat = vv * r
            vn = (xhat * gain).astype(BF16)
            vng = [jnp.where(grp == gi, vn, jnp.zeros_like(vn)) for gi in range(SGU_HEADS)]
            sv = bias
            for gi in range(SGU_HEADS):
                sv = sv + jnp.dot(wm[gi], vng[gi], preferred_element_type=F32)
            dy = dy_ref[chunk, :]
            o_ref[chunk, 0:SGU_W] = (dy * sv * _gelu_grad(cu)).astype(o_ref.dtype)
            dsv = dy * u
            dsv16 = dsv.astype(BF16)
            db = db + dsv
            dvn = jnp.zeros_like(dsv)
            for gi in range(SGU_HEADS):
                dw[gi] = dw[gi] + lax.dot_general(dsv16, vng[gi], _NT, preferred_element_type=F32)
                dvn_g = lax.dot_general(wm[gi], dsv16, _TN, preferred_element_type=F32)
                dvn = jnp.where(grp == gi, dvn_g, dvn)
            dg = dg + jnp.sum(dvn * xhat, axis=0, keepdims=True)
            dxhat = dvn * gain
            dvv = r * (dxhat - xhat * _group_mean(dxhat * xhat, SGU_W))
            o_ref[chunk, SGU_W:2 * SGU_W] = (dvv * _gelu_grad(cv)).astype(o_ref.dtype)
        dg_ref[...] += dg
        db_ref[...] += db
        for gi in range(SGU_HEADS):
            dw_ref[gi] += jnp.where(tril, dw[gi], 0.0)

    return pl.pallas_call(
        body,
        name=name,
        grid=(t // rows,),
        in_specs=[
            pl.BlockSpec((rows, SGU_W), lambda i: (i, dy0)),
            pl.BlockSpec((rows, SGU_W), lambda i: (i, u0)),
            pl.BlockSpec((rows, SGU_W), lambda i: (i, u0 + 1)),
            pl.BlockSpec((1, SGU_W), lambda i: (0, 0)),
            pl.BlockSpec((SGU_HEADS, CHUNK, CHUNK), lambda i: (0, 0, 0)),
            pl.BlockSpec((CHUNK, SGU_W), lambda i: (0, 0)),
        ],
        out_specs=[
            pl.BlockSpec((rows, 2 * SGU_W), lambda i: (i, 0)),
            pl.BlockSpec((1, SGU_W), lambda i: (0, 0)),
            pl.BlockSpec((SGU_HEADS, CHUNK, CHUNK), lambda i: (0, 0, 0)),
            pl.BlockSpec((CHUNK, SGU_W), lambda i: (0, 0)),
        ],
        out_shape=[
            jax.ShapeDtypeStruct((t, 2 * SGU_W), BF16),
            jax.ShapeDtypeStruct((1, SGU_W), F32),
            jax.ShapeDtypeStruct((SGU_HEADS, CHUNK, CHUNK), F32),
            jax.ShapeDtypeStruct((CHUNK, SGU_W), F32),
        ],
        compiler_params=_params("arbitrary"),
    )(dmix, proj, proj, g_v, w_s, b_exp)


def _ple_bwd(dh, gate, pp, order_after, *, name, br=512):
    t, d = dh.shape
    br = min(br, t)

    def body(dh_ref, g_ref, p_ref, order_ref, dpre_ref, dpp_ref):
        dhv, g = dh_ref[...], g_ref[...]
        dpre_ref[...] = (dhv * p_ref[...] * g * (1.0 - g)).astype(dpre_ref.dtype)
        dpp_ref[...] = (dhv * g).astype(dpp_ref.dtype)

    row = pl.BlockSpec((br, d), lambda i: (i, 0))
    return pl.pallas_call(
        body,
        name=name,
        grid=(t // br,),
        in_specs=[row] * 3 + [pl.BlockSpec(order_after.shape, lambda i: (0, 0))],
        out_specs=[row] * 2,
        out_shape=[jax.ShapeDtypeStruct((t, d), BF16)] * 2,
        compiler_params=_params("parallel"),
    )(dh, gate, pp, order_after)


def _loss_head(y, target, *, name, br=512):
    t, d = y.shape
    br = min(br, t)

    def body(y_ref, t_ref, dy_ref, loss_ref):
        err = y_ref[...] - t_ref[...]
        dy_ref[...] = err * (1.0 / d)

        @pl.when(pl.program_id(0) == 0)
        def _():
            loss_ref[...] = jnp.zeros_like(loss_ref)

        loss_ref[...] += 0.5 * jnp.sum(jnp.sum(err * err, axis=1, keepdims=True) * (1.0 / d), axis=0, keepdims=True)

    row = pl.BlockSpec((br, d), lambda i: (i, 0))
    return pl.pallas_call(
        body,
        name=name,
        grid=(t // br,),
        in_specs=[row, row],
        out_specs=[row, pl.BlockSpec((8, LANES), lambda i: (0, 0))],
        out_shape=[jax.ShapeDtypeStruct((t, d), F32), jax.ShapeDtypeStruct((8, LANES), F32)],
        compiler_params=_params("arbitrary"),
    )(y, target)


def _adamw_update(w, g, m, v):
    nm = ADAM_B1 * m + (1.0 - ADAM_B1) * g
    nv = ADAM_B2 * v + (1.0 - ADAM_B2) * (g * g)
    m_hat = nm / (1.0 - ADAM_B1 ** ADAM_STEP)
    v_hat = nv / (1.0 - ADAM_B2 ** ADAM_STEP)
    return -ADAM_LR * (m_hat / (jnp.sqrt(v_hat) + ADAM_EPS) + ADAM_WD * w), nm, nv


def _adamw(w, g, m, v, *, name, br=512):
    r, c = w.shape
    br = _row_block(r, br)

    def body(w_ref, g_ref, m_ref, v_ref, d_ref, nm_ref, nv_ref):
        d_ref[...], nm_ref[...], nv_ref[...] = _adamw_update(w_ref[...], g_ref[...], m_ref[...], v_ref[...])

    row = pl.BlockSpec((br, c), lambda i: (i, 0))
    return pl.pallas_call(
        body,
        name=name,
        grid=(r // br,),
        in_specs=[row] * 4,
        out_specs=[row] * 3,
        out_shape=[jax.ShapeDtypeStruct((r, c), F32)] * 3,
        compiler_params=_params("parallel"),
    )(w, g, m, v)


def _sum_slots(per_layer, *, name):
    counts = [len(arrays) for arrays in per_layer]
    flat = [a for arrays in per_layer for a in arrays]

    def body(*refs):
        ins, outs = refs[:len(flat)], refs[len(flat):]
        at = 0
        for o_ref, count in zip(outs, counts, strict=True):
            for li in range(count):
                acc = ins[at + li][0]
                for j in range(1, N_DEV):
                    acc = acc + ins[at + li][j]
                o_ref[li] = acc
            at += count

    return pl.pallas_call(
        body,
        name=name,
        out_shape=[jax.ShapeDtypeStruct((len(arrays), *arrays[0].shape[1:]), F32) for arrays in per_layer],
        compiler_params=pltpu.CompilerParams(vmem_limit_bytes=VMEM_LIMIT_BYTES),
    )(*flat)


_ADAMW_BLOCK_ELEMS = 192 * 1024


def _adamw_reduce(w, arrived, m, v, *, name):
    depth, r, c = w.shape
    br = _row_block(r, max(BF16_TILE_ROWS, _ADAMW_BLOCK_ELEMS // (-(-c // LANES) * LANES)))

    def body(w_ref, m_ref, v_ref, *rest):
        parts, (g_ref, d_ref, nm_ref, nv_ref) = rest[:depth], rest[depth:]
        for li in range(depth):

            @pl.when(pl.program_id(0) == li)
            def _(li=li):
                g = parts[li][0].astype(F32)
                for j in range(1, N_DEV):
                    g = g + parts[li][j].astype(F32)
                g_ref[...] = g
                d_ref[...], nm_ref[...], nv_ref[...] = _adamw_update(w_ref[...], g, m_ref[...], v_ref[...])

    cur = pl.BlockSpec((None, br, c), lambda l, i: (l, i, 0))
    slots = [pl.BlockSpec((N_DEV, br, c), lambda l, i, li=li: (0, jnp.where(l == li, i, 0), 0)) for li in range(depth)]
    return pl.pallas_call(
        body,
        name=name,
        grid=(depth, r // br),
        in_specs=[cur, cur, cur] + slots,
        out_specs=[cur] * 4,
        out_shape=[jax.ShapeDtypeStruct((depth, r, c), F32)] * 4,
        compiler_params=_params("arbitrary", "arbitrary"),
    )(w, m, v, *arrived)


def _my_place():
    return lax.axis_index("x"), lax.axis_index("y"), lax.axis_index("c")


def _flip(v, bit):
    return 1 - v if bit else v


def _slot_of(px, py, pc):
    return 4 * px + 2 * py + pc


_ANY = pl.BlockSpec(memory_space=pl.ANY)


_HBM = pl.BlockSpec(memory_space=pltpu.HBM)
_SEM = pl.BlockSpec(memory_space=pltpu.SEMAPHORE)
_DATAFLOW = pltpu.SideEffectType.DATAFLOW_SIDE_EFFECTING


_GATHER, _GATHER_COLUMNS, _SCATTER = "gather", "gather_columns", "scatter"


def _landing_shape(a, mode):
    if mode == _SCATTER:
        return a.shape
    if mode == _GATHER_COLUMNS:
        return (a.shape[0], N_DEV * a.shape[1])
    return (N_DEV, *a.shape)


_DIRECT, _NEAR, _RELAY = "direct", "near", "relay"
_OTHER_CHIPS = (2, 4, 6)
_SIBLING = 1


def _exchange_copies(src_refs, land_refs, send_sem, recv_sem, modes, hops=_DIRECT):
    mx, my, mc = _my_place()
    peer_of = lambda k: (_flip(mx, k & 4), _flip(my, k & 2), _flip(mc, k & 1))
    mine = _slot_of(mx, my, mc)

    def block(land, mode, slot):
        if mode == _GATHER_COLUMNS:
            n = land.shape[1] // N_DEV
            return land.at[:, pl.ds(pl.multiple_of(slot * n, LANES), n)]
        return land.at[slot]

    def remote_copy(src, dst, to):
        return pltpu.make_async_remote_copy(src_ref=src, dst_ref=dst, send_sem=send_sem, recv_sem=recv_sem,
                                            device_id=to, device_id_type=MESH)

    remote, local = [], []
    for src, land, mode in zip(src_refs, land_refs, modes, strict=True):
        if hops == _RELAY:
            assert mode != _SCATTER
            for k in _OTHER_CHIPS:
                came = block(land, mode, _slot_of(*peer_of(k)))
                remote.append(remote_copy(came, came, peer_of(_SIBLING)))
            continue
        dst = block(land, mode, mine)
        for k in ((_SIBLING,) + _OTHER_CHIPS if hops == _NEAR else range(1, N_DEV)):
            remote.append(remote_copy(src.at[_slot_of(*peer_of(k))] if mode == _SCATTER else src, dst, peer_of(k)))
        local.append(pltpu.make_async_copy(src.at[mine] if mode == _SCATTER else src, dst, recv_sem))
    return remote, local


def _wait_copies(remote, local):
    for cp in remote:
        cp.wait_send()
        cp.wait_recv()
    for cp in local:
        cp.wait()


def _exchange_start(groups, after, *, name, hops=_DIRECT):
    sizes = [len(srcs) for srcs, _ in groups]
    n, n_sems = sum(sizes), 2 * len(groups)
    srcs = [a for arrays, _ in groups for a in arrays]
    lands = [lax.empty(_landing_shape(a, mode), a.dtype)
             for arrays, modes in groups for a, mode in zip(arrays, modes, strict=True)]
    offsets = [sum(sizes[:g]) for g in range(len(groups))]

    def body(*refs):
        sems = refs[2 * n + 1:2 * n + 1 + n_sems]
        for g, (off, size, (_, modes)) in enumerate(zip(offsets, sizes, groups)):
            remote, local = _exchange_copies(refs[off:off + size], refs[n + off:n + off + size], sems[2 * g],
                                             sems[2 * g + 1], modes, hops)
            for cp in remote + local:
                cp.start()
        refs[-1][...] = jnp.zeros_like(refs[-1])

    thru = [pltpu.HBM(a.shape, a.dtype) for a in (*srcs, *lands)]
    out = pl.pallas_call(
        body,
        name=name,
        in_specs=[_HBM] * (2 * n) + [_ANY],
        out_specs=(*[_SEM] * n_sems, *[_HBM] * (2 * n), pl.BlockSpec(memory_space=pltpu.VMEM)),
        out_shape=(*[pltpu.SemaphoreType.DMA(())] * n_sems, *thru, jax.ShapeDtypeStruct((8, LANES), F32)),
        input_output_aliases={i: n_sems + i for i in range(2 * n)},
        compiler_params=pltpu.CompilerParams(has_side_effects=_DATAFLOW),
    )(*[pltpu.with_memory_space_constraint(a, pltpu.HBM) for a in (*srcs, *lands)], after)
    sems, arrays = out[:n_sems], out[n_sems:-1]
    started = [(sems[2 * g], sems[2 * g + 1], *arrays[off:off + size], *arrays[n + off:n + off + size])
               for g, (off, size) in enumerate(zip(offsets, sizes))]
    return started, out[-1]


def _exchange_relay(started, after, *, modes, regroup, name):
    send_sem, recv_sem, *thru = started
    n, n_sems = len(thru) // 2, 2 * len(regroup)

    def body(*refs):
        srcs, lands = refs[:n], refs[n:2 * n]
        _wait_copies(*_exchange_copies(srcs, lands, refs[2 * n], refs[2 * n + 1], modes, _NEAR))
        sems = refs[2 * n + 3:2 * n + 3 + n_sems]
        for g, members in enumerate(regroup):
            remote, _ = _exchange_copies([srcs[i] for i in members], [lands[i] for i in members], sems[2 * g],
                                         sems[2 * g + 1], [modes[i] for i in members], _RELAY)
            for cp in remote:
                cp.start()
        refs[-1][...] = jnp.zeros_like(refs[-1])

    out = pl.pallas_call(
        body,
        name=name,
        in_specs=[_HBM] * (2 * n) + [_SEM, _SEM, _ANY],
        out_specs=(*[_SEM] * n_sems, *[_HBM] * (2 * n), pl.BlockSpec(memory_space=pltpu.VMEM)),
        out_shape=(*[pltpu.SemaphoreType.DMA(())] * n_sems, *[pltpu.HBM(a.shape, a.dtype) for a in thru],
                   jax.ShapeDtypeStruct((8, LANES), F32)),
        input_output_aliases={i: n_sems + i for i in range(2 * n)},
        compiler_params=pltpu.CompilerParams(has_side_effects=_DATAFLOW),
    )(*thru, send_sem, recv_sem, after)
    sems, arrays = out[:n_sems], out[n_sems:-1]
    groups = [(sems[2 * g], sems[2 * g + 1], *[arrays[i] for i in members], *[arrays[n + i] for i in members])
              for g, members in enumerate(regroup)]
    return groups, out[-1]


def _exchange_wait(started, after, *, modes, name, hops=_DIRECT):
    send_sem, recv_sem, *thru = started
    n = len(thru) // 2

    def body(*refs):
        _wait_copies(*_exchange_copies(refs[:n], refs[n:2 * n], refs[2 * n], refs[2 * n + 1], modes, hops))

    out = pl.pallas_call(
        body,
        name=name,
        in_specs=[_HBM] * (2 * n) + [_SEM, _SEM, _ANY],
        out_specs=[_HBM] * (2 * n),
        out_shape=[pltpu.HBM(a.shape, a.dtype) for a in thru],
        input_output_aliases={i: i for i in range(2 * n)},
        compiler_params=pltpu.CompilerParams(has_side_effects=_DATAFLOW),
    )(*thru, send_sem, recv_sem, after)
    return out[n:]


def _gather_columns(g):
    return jnp.moveaxis(g, 0, 1).reshape(g.shape[1], -1)


def _split_rows(w):
    return w.reshape(N_DEV, w.shape[0] // N_DEV, w.shape[1])


_FIRST = ("w_in", "conv_w")
_REST = ("w_out", "w_ff1", "w_ff2", "w_ple_gate", "w_ple_proj")
_REST_GROUPS = (("w_out",), ("w_ff1",), ("w_ff2",), ("w_ple_gate", "w_ple_proj"))
_BIG = ("w_in",) + _REST
_GATHER_MODE = dict(w_in=_GATHER, conv_w=_GATHER, w_out=_GATHER, w_ff1=_GATHER_COLUMNS, w_ff2=_GATHER,
                    w_ple_gate=_GATHER, w_ple_proj=_GATHER_COLUMNS)
_RELAYOUT_AFTER_GATHER = ("conv_w",)
_SMALL = ("norm1_g", "q_norm_g", "k_norm_g", "sgu_norm_g", "sgu_w", "sgu_b", "norm2_g", "norm3_g")
_ORDER = ("norm1_g", "w_in", "conv_w", "q_norm_g", "k_norm_g", "sgu_norm_g", "sgu_w", "sgu_b", "w_out", "norm2_g",
          "w_ff1", "w_ff2", "norm3_g", "w_ple_gate", "w_ple_proj")


def _whole_matrices(names, landed):
    return {k: _gather_columns(g) if k in _RELAYOUT_AFTER_GATHER else g.reshape(-1, g.shape[-1])
            for k, g in zip(names, landed, strict=True)}


_NORM_FUSED_ROWS = 256


def _layer_forward(h0, hn1, p16, s, li, w_first, gathered, next_norm_g):
    nm = lambda k: f"{k}_l{li}"
    t, d = h0.shape
    w = dict(w_first)

    def add_and_norm(acc, res, gain):
        h = res + acc
        return h, _rms_rows(h, gain)

    proj = _matmul(hn1, w["w_in"], name=nm("proj"), tb=True, bm=t, bn=256)
    y_a = _conv_fwd(proj, w["conv_w"], name=nm("conv"))
    qs, kn, v = _qk_prep(proj, s["gq"], s["gk"], name=nm("qkprep"))
    y_b, lt = _attn_fwd(qs, kn, v, name=nm("attn"))
    gathered["relay_rest"](y_b)
    y_c = _sgu_fwd(proj, s["sgu_norm_g"], s["sgu_w"], s["b_exp"], name=nm("sgu"))
    mix = jnp.concatenate([y_a, y_b, y_c], axis=1)
    w.update(gathered["fetch"](0, mix))
    h1, hn2 = _matmul(mix, w["w_out"], name=nm("out"), bm=_NORM_FUSED_ROWS, bn=d, out_dtypes=(F32, BF16),
                      extras=(h0,), row_vectors=(s["norm2_g"],), epilogue=add_and_norm)
    w.update(gathered["fetch"](1, hn2))
    f = _matmul(hn2, w["w_ff1"], name=nm("ff1"), bm=t, bn=512, out_dtypes=(BF16,),
                epilogue=lambda acc: (jnp.square(jnp.maximum(acc, 0.0)),))
    w.update(gathered["fetch"](2, f))
    gathered["relay_next"](f)
    h2, hn3 = _matmul(f, w["w_ff2"], name=nm("ff2"), bm=_NORM_FUSED_ROWS, bn=d, out_dtypes=(F32, BF16),
                      extras=(h1,), row_vectors=(s["norm3_g"],), epilogue=add_and_norm)
    w.update(gathered["fetch"](3, hn3))
    w_next = gathered["fetch_next"](hn3)
    pp = _matmul(p16, w["w_ple_proj"], name=nm("pleproj"), bm=t, bn=512)

    def gate_epilogue(acc, pp_blk, h_blk, *gain):
        gate = jax.nn.sigmoid(acc)
        h = h_blk + gate * pp_blk
        return (h, gate) + tuple(_rms_rows(h, g) for g in gain)

    fused_norm = () if next_norm_g is None else (next_norm_g,)
    h3, gate, *hn1_next = _matmul(hn3, w["w_ple_gate"], name=nm("plegate"), bm=_NORM_FUSED_ROWS, bn=d,
                                  out_dtypes=(F32, F32) + (BF16,) * len(fused_norm), extras=(pp, h2),
                                  row_vectors=fused_norm, epilogue=gate_epilogue)
    saved = dict(h0=h0, hn1=hn1, proj=proj, qs=qs, kn=kn, v=v, lt=lt, mix=mix, h1=h1, hn2=hn2, f=f, h2=h2,
                 hn3=hn3, pp=pp, gate=gate, p16=p16)
    return h3, (hn1_next[0] if hn1_next else None), w, w_next, saved


def _layer_backward(dh3, a, w, s, li, order_after, start_rest):
    nm = lambda k: f"{k}_bwd_l{li}"
    t = dh3.shape[0]
    dpre, dpp = _ple_bwd(dh3, a["gate"], a["pp"], order_after, name=nm("ple"))
    g_gate = _weight_grad(a["hn3"], [dpre], name=nm("dwgate"))
    g_proj = _weight_grad(a["p16"], [dpp], name=nm("dwproj"), column_shards=True)
    dh2, dh2_16, g_n3 = _matmul_rms_bwd([dpre], w["w_ple_gate"], a["h2"], s["norm3_g"], dh3, name=nm("dh2"))
    du = _matmul(dh2_16, w["w_ff2"], name=nm("du"), tb=True, bm=t, bn=512, out_dtypes=(BF16,), extras=(a["f"],),
                 epilogue=lambda acc, f: (acc * (2.0 * jnp.sqrt(f.astype(F32))),))
    g_ff2 = _weight_grad(a["f"], [dh2_16], name=nm("dwff2"))
    g_ff1 = _weight_grad(a["hn2"], [du], name=nm("dwff1"), column_shards=True)
    dh1, dh1_16, g_n2 = _matmul_rms_bwd([du], w["w_ff1"], a["h1"], s["norm2_g"], dh2, name=nm("dh1"))
    dmix = _matmul(dh1_16, w["w_out"], name=nm("dmix"), tb=True, bm=t, bn=256)
    g_out = _weight_grad(a["mix"], [dh1_16], name=nm("dwout"))
    started = start_rest(dict(w_out=_split_rows(g_out), w_ff1=g_ff1, w_ff2=_split_rows(g_ff2),
                              w_ple_gate=_split_rows(g_gate), w_ple_proj=g_proj), dmix)
    d_conv, g_conv = _conv_bwd(dmix, a["proj"], w["conv_w"], name=nm("conv"))
    dqs, dkn, dv = _attn_bwd(dmix, a["qs"], a["kn"], a["v"], a["lt"], started, name=nm("attn"))
    d_qkv, g_q, g_k = _qk_prep_bwd(dqs, dkn, dv, a["proj"], s["gq"], s["gk"], name=nm("qkprep"))
    d_sgu, g_sn, g_sw, g_sb = _sgu_bwd(dmix, a["proj"], s["sgu_norm_g"], s["sgu_w"], s["b_exp"], name=nm("sgu"))
    dproj = [d_conv, d_qkv, d_sgu]
    g_in = jnp.concatenate([_weight_grad(piece, [a["hn1"]], name=nm(f"dwin{i}")) for i, piece in enumerate(dproj)])
    dh0, _, g_n1 = _matmul_rms_bwd(dproj, w["w_in"], a["h0"], s["norm1_g"], dh1, name=nm("dh0"), w_is_k_by_d=True)
    small = dict(norm1_g=g_n1, norm2_g=g_n2, norm3_g=g_n3, q_norm_g=g_q, k_norm_g=g_k, sgu_norm_g=g_sn, sgu_w=g_sw,
                 sgu_b=g_sb, conv_w=g_conv)
    return dh0, _split_rows(g_in), small


def _small_gradients(raw, depth):
    return dict(
        norm1_g=raw["norm1_g"].reshape(depth, -1), norm2_g=raw["norm2_g"].reshape(depth, -1),
        norm3_g=raw["norm3_g"].reshape(depth, -1),
        q_norm_g=raw["q_norm_g"].reshape(depth, -1, HEAD_DIM).sum(1),
        k_norm_g=raw["k_norm_g"].reshape(depth, -1, HEAD_DIM).sum(1),
        sgu_norm_g=raw["sgu_norm_g"].reshape(depth, -1), sgu_w=raw["sgu_w"],
        sgu_b=jnp.swapaxes(raw["sgu_b"].reshape(depth, CHUNK, SGU_HEADS, HEAD_DIM).sum(-1), 1, 2),
        conv_w=raw["conv_w"][:, :CONV_TAPS],
    )


def kernel(x, p, norm1_g, w_in, conv_w, q_norm_g, k_norm_g, sgu_norm_g, sgu_w, sgu_b, w_out, norm2_g, w_ff1, w_ff2, norm3_g, w_ple_gate, w_ple_proj, loss_target, m_norm1_g, m_w_in, m_conv_w, m_q_norm_g, m_k_norm_g, m_sgu_norm_g, m_sgu_w, m_sgu_b, m_w_out, m_norm2_g, m_w_ff1, m_w_ff2, m_norm3_g, m_w_ple_gate, m_w_ple_proj, v_norm1_g, v_w_in, v_conv_w, v_q_norm_g, v_k_norm_g, v_sgu_norm_g, v_sgu_w, v_sgu_b, v_w_out, v_norm2_g, v_w_ff1, v_w_ff2, v_norm3_g, v_w_ple_gate, v_w_ple_proj):
    weights = dict(norm1_g=norm1_g, w_in=w_in, conv_w=conv_w, q_norm_g=q_norm_g, k_norm_g=k_norm_g,
                   sgu_norm_g=sgu_norm_g, sgu_w=sgu_w, sgu_b=sgu_b, w_out=w_out, norm2_g=norm2_g, w_ff1=w_ff1,
                   w_ff2=w_ff2, norm3_g=norm3_g, w_ple_gate=w_ple_gate, w_ple_proj=w_ple_proj)
    mom = dict(norm1_g=m_norm1_g, w_in=m_w_in, conv_w=m_conv_w, q_norm_g=m_q_norm_g, k_norm_g=m_k_norm_g,
               sgu_norm_g=m_sgu_norm_g, sgu_w=m_sgu_w, sgu_b=m_sgu_b, w_out=m_w_out, norm2_g=m_norm2_g, w_ff1=m_w_ff1,
               w_ff2=m_w_ff2, norm3_g=m_norm3_g, w_ple_gate=m_w_ple_gate, w_ple_proj=m_w_ple_proj)
    var = dict(norm1_g=v_norm1_g, w_in=v_w_in, conv_w=v_conv_w, q_norm_g=v_q_norm_g, k_norm_g=v_k_norm_g,
               sgu_norm_g=v_sgu_norm_g, sgu_w=v_sgu_w, sgu_b=v_sgu_b, w_out=v_w_out, norm2_g=v_norm2_g, w_ff1=v_w_ff1,
               w_ff2=v_w_ff2, norm3_g=v_norm3_g, w_ple_gate=v_w_ple_gate, w_ple_proj=v_w_ple_proj)
    for params in (weights, mom, var):
        params["w_in"] = jnp.swapaxes(params["w_in"], 1, 2)
    depth = norm1_g.shape[0]
    mx, my, mc = _my_place()
    me = _slot_of(mx, my, mc)

    gathers = []
    modes_of = lambda names: tuple(_GATHER_MODE[k] for k in names)
    token = x[0, :8, :LANES]
    for li in range(depth):
        groups = [([weights[k][li] if k == "conv_w" else weights[k][li].astype(BF16) for k in names], modes_of(names))
                  for names in (_FIRST, _REST)]
        started, token = _exchange_start(groups, token, name=f"gather_start_l{li}", hops=_NEAR)
        gathers.append(started)

    small = []
    for li in range(depth):
        small.append(dict(
            norm1_g=norm1_g[li][None], norm2_g=norm2_g[li][None], norm3_g=norm3_g[li][None],
            gq=jnp.tile(q_norm_g[li], _QK_BLOCK // HEAD_DIM)[None], gk=jnp.tile(k_norm_g[li], _QK_BLOCK // HEAD_DIM)[None],
            sgu_norm_g=sgu_norm_g[li][None], sgu_w=sgu_w[li], b_exp=jnp.repeat(sgu_b[li].T, HEAD_DIM, axis=1),
        ))
    small[0]["norm1_g"] = small[0]["norm1_g"] + token[0, 0]

    h = x[0]
    saved, full = [], []
    relayed_first, relayed_rest = [None] * depth, [None] * depth
    rest_members = [[_REST.index(k) for k in names] for names in _REST_GROUPS]

    def relay_first(li, after):
        if li < depth:
            (relayed_first[li],), _ = _exchange_relay(gathers[li][0], after, modes=modes_of(_FIRST),
                                                      regroup=[list(range(len(_FIRST)))], name=f"gather_first_relay_l{li}")

    def fetch_first(li, after):
        if li == depth:
            return None
        landed = _exchange_wait(relayed_first[li], after, modes=modes_of(_FIRST), hops=_RELAY,
                                name=f"gather_first_wait_l{li}")
        return _whole_matrices(_FIRST, landed)

    hn1 = _rms_fwd(h, small[0]["norm1_g"], name="rms1_l0")
    relay_first(0, hn1)
    w_first = fetch_first(0, hn1)
    for li in range(depth):

        def relay_rest(after, li=li):
            relayed_rest[li], _ = _exchange_relay(gathers[li][1], after, modes=modes_of(_REST), regroup=rest_members,
                                                  name=f"gather_rest_relay_l{li}")

        def fetch(g, after, li=li):
            landed = _exchange_wait(relayed_rest[li][g], after, modes=modes_of(_REST_GROUPS[g]), hops=_RELAY,
                                    name=f"gather_{_REST_GROUPS[g][0]}_wait_l{li}")
            return _whole_matrices(_REST_GROUPS[g], landed)

        gathered = dict(relay_rest=relay_rest, fetch=fetch, relay_next=functools.partial(relay_first, li + 1),
                        fetch_next=functools.partial(fetch_first, li + 1))
        next_norm_g = small[li + 1]["norm1_g"] if li + 1 < depth else None
        h, hn1, w, w_first, acts = _layer_forward(h, hn1, p[li, 0].astype(BF16), small[li], li, w_first, gathered,
                                                  next_norm_g)
        full.append(w)
        saved.append(acts)
    dh, loss_tile = _loss_head(h, loss_target[0], name="loss_head")
    loss = lax.psum(loss_tile[0, 0], ("x", "y", "c"))

    small_names = _SMALL + ("conv_w",)
    scatter_first, scatter_rest = [None] * depth, [None] * depth
    first_modes, rest_modes = (_SCATTER,) + (_GATHER,) * len(small_names), (_SCATTER,) * len(_REST)
    token = loss_tile
    for li in reversed(range(depth)):

        def start_rest(parts, after, li=li):
            (scatter_rest[li],), started = _exchange_start([([parts[k] for k in _REST], rest_modes)], after,
                                                           name=f"scatter_rest_start_l{li}")
            return started

        dh, g_in, small_grads = _layer_backward(dh, saved[li], full[li], small[li], li, token, start_rest)
        (scatter_first[li],), token = _exchange_start(
            [([g_in] + [small_grads[k] for k in small_names], first_modes)], dh, name=f"scatter_first_start_l{li}")
    grad_x = dh[None]

    grads, delta, new_m, new_v = {}, {}, {}, {}
    arrived = {k: [None] * depth for k in _BIG}
    for li in reversed(range(depth)):
        landed = _exchange_wait(scatter_rest[li], token, modes=rest_modes, name=f"scatter_rest_wait_l{li}")
        for k, g in zip(_REST, landed, strict=True):
            arrived[k][li] = g
    for k in _REST:
        grads[k], delta[k], new_m[k], new_v[k] = _adamw_reduce(weights[k], arrived[k], mom[k], var[k], name=f"adamw_{k}")
    small_parts = {k: [None] * depth for k in small_names}
    updated = jnp.stack([delta[k][0, 0, :1] for k in _REST])
    for li in reversed(range(depth)):
        arrived["w_in"][li], *parts = _exchange_wait(scatter_first[li], updated, modes=first_modes,
                                                     name=f"scatter_first_wait_l{li}")
        for k, part in zip(small_names, parts, strict=True):
            small_parts[k][li] = part
    grads["w_in"], delta["w_in"], new_m["w_in"], new_v["w_in"] = _adamw_reduce(
        weights["w_in"], arrived["w_in"], mom["w_in"], var["w_in"], name="adamw_w_in")
    for results in (grads, delta, new_m, new_v):
        results["w_in"] = jnp.swapaxes(results["w_in"], 1, 2)
    sums = _sum_slots([small_parts[k] for k in small_names], name="sum_small_grads")
    grads.update(_small_gradients(dict(zip(small_names, sums)), depth))
    n_conv = conv_w.shape[2]
    grads["conv_w"] = lax.dynamic_slice_in_dim(grads["conv_w"], me * n_conv, n_conv, axis=2)
    for k in small_names:
        as_rows = lambda a: a.reshape(-1, a.shape[-1])
        outs = _adamw(as_rows(weights[k]), as_rows(grads[k]), as_rows(mom[k]), as_rows(var[k]), name=f"adamw_{k}")
        delta[k], new_m[k], new_v[k] = (o.reshape(weights[k].shape) for o in outs)

    return (loss, grad_x, *[grads[k] for k in _ORDER], *[delta[k] for k in _ORDER],
            *[new_m[k] for k in _ORDER], *[new_v[k] for k in _ORDER])
```

```python
import functools
import math

import jax
import jax.numpy as jnp
from jax import lax
from jax.experimental import pallas as pl
from jax.experimental.pallas import tpu as pltpu

F32 = jnp.float32
BF16 = jnp.bfloat16

N_DEV = 8
HEAD_DIM = 64
CONV_W = 256
ATTN_W = 512
SGU_W = 256
SGU_HEADS = 4
CHUNK = 128
CONV_TAPS = 3
EPS = 1e-6
QK_SCALE = HEAD_DIM ** -0.5

ADAM_LR = 0.001
ADAM_B1 = 0.9
ADAM_B2 = 0.999
ADAM_EPS = 1e-08
ADAM_WD = 0.01
ADAM_STEP = 10

LANES = 128
BF16_TILE_ROWS = 16
VMEM_LIMIT_BYTES = 56 * 1024 * 1024
MESH = pl.DeviceIdType.MESH


def _params(*sem):
    return pltpu.CompilerParams(dimension_semantics=sem, vmem_limit_bytes=VMEM_LIMIT_BYTES)


def _row_block(rows, cap):
    if rows <= cap:
        return rows
    return max(b for b in range(BF16_TILE_ROWS, cap + 1, BF16_TILE_ROWS) if rows % b == 0)


def _matmul(a, b, *, name, tb=False, bm=512, bn=512, out_dtypes=(F32,), epilogue=None, extras=(), row_vectors=()):
    m, k = a.shape
    n = b.shape[0] if tb else b.shape[1]
    assert k == (b.shape[1] if tb else b.shape[0])
    bm, bn = min(bm, m), min(bn, n)
    assert m % bm == 0 and n % bn == 0
    a_spec = pl.BlockSpec((bm, k), lambda i, j: (i, 0))
    b_spec = pl.BlockSpec((bn, k), lambda i, j: (j, 0)) if tb else pl.BlockSpec((k, bn), lambda i, j: (0, j))
    dims = (((1,), (1 if tb else 0,)), ((), ()))
    n_in = len(extras) + len(row_vectors)
    for e in extras:
        assert e.shape == (m, n), (e.shape, m, n)
    for e in row_vectors:
        assert e.shape == (1, n), (e.shape, n)

    def body(a_ref, b_ref, *rest):
        outs = rest[n_in:]
        acc = lax.dot_general(a_ref[...], b_ref[...], dims, preferred_element_type=F32)
        res = (acc,) if epilogue is None else epilogue(acc, *[e[...] for e in rest[:n_in]])
        for o_ref, r in zip(outs, res, strict=True):
            o_ref[...] = r.astype(o_ref.dtype)

    tile = pl.BlockSpec((bm, bn), lambda i, j: (i, j))
    vec = pl.BlockSpec((1, bn), lambda i, j: (0, j))
    out = pl.pallas_call(
        body,
        name=name,
        grid=(m // bm, n // bn),
        in_specs=[a_spec, b_spec] + [tile] * len(extras) + [vec] * len(row_vectors),
        out_specs=[tile] * len(out_dtypes),
        out_shape=[jax.ShapeDtypeStruct((m, n), d) for d in out_dtypes],
        compiler_params=_params("parallel", "parallel"),
    )(a, b, *extras, *row_vectors)
    return out[0] if len(out_dtypes) == 1 else out


def _rms_rows(x, g):
    return x * lax.rsqrt(jnp.mean(x * x, axis=-1, keepdims=True) + EPS) * g


_WEIGHT_GRAD_ACC_ELEMS = 1024 * 1024


def _weight_grad(x, dys, *, name, column_shards=False):
    t, m = x.shape
    n = sum(dy.shape[1] for dy in dys)
    bm = m if m <= 2 * LANES else min(m // 2, max(LANES, _WEIGHT_GRAD_ACC_ELEMS // n // LANES * LANES))
    assert m % bm == 0
    ns = n // N_DEV

    def body(x_ref, *rest):
        o_ref = rest[-1]
        xb = x_ref[...]
        acc = jnp.concatenate([lax.dot_general(xb, dy_ref[...], _TN, preferred_element_type=F32) for dy_ref in rest[:-1]],
                              axis=1)
        if column_shards:
            for s in range(N_DEV):
                o_ref[s] = acc[:, s * ns:(s + 1) * ns].astype(o_ref.dtype)
        else:
            o_ref[...] = acc.astype(o_ref.dtype)

    if column_shards:
        out_spec, out_dims = pl.BlockSpec((N_DEV, bm, ns), lambda i: (0, i, 0)), (N_DEV, m, ns)
    else:
        out_spec, out_dims = pl.BlockSpec((bm, n), lambda i: (i, 0)), (m, n)
    return pl.pallas_call(
        body,
        name=name,
        grid=(m // bm,),
        in_specs=[pl.BlockSpec((t, bm), lambda i: (0, i))] + [pl.BlockSpec(dy.shape, lambda i: (0, 0)) for dy in dys],
        out_specs=out_spec,
        out_shape=jax.ShapeDtypeStruct(out_dims, BF16),
        compiler_params=_params("parallel"),
    )(x, *dys)


def _rms_fwd(h, g, *, name, br=512):
    t, d = h.shape
    br = min(br, t)

    def body(h_ref, g_ref, o_ref):
        o_ref[...] = _rms_rows(h_ref[...], g_ref[...]).astype(o_ref.dtype)

    return pl.pallas_call(
        body,
        name=name,
        grid=(t // br,),
        in_specs=[pl.BlockSpec((br, d), lambda i: (i, 0)), pl.BlockSpec((1, d), lambda i: (0, 0))],
        out_specs=pl.BlockSpec((br, d), lambda i: (i, 0)),
        out_shape=jax.ShapeDtypeStruct((t, d), BF16),
        compiler_params=_params("parallel"),
    )(h, g)


def _matmul_rms_bwd(dzs, w, h, g, dres, *, name, w_is_k_by_d=False):
    t, d = h.shape
    widths = [dz.shape[1] for dz in dzs]
    k = sum(widths)
    assert w.shape == ((k, d) if w_is_k_by_d else (d, k))
    br = min(t, 512 if k <= d else 256)
    n_dz = len(dzs)

    def body(*refs):
        w_ref, h_ref, g_ref, dres_ref, dh_ref, dh16_ref, dg_ref = refs[n_dz:]
        x = h_ref[...]
        dyv, at = None, 0
        for dz_ref, width in zip(refs[:n_dz], widths):
            if w_is_k_by_d:
                part = jnp.dot(dz_ref[...], w_ref[at:at + width, :], preferred_element_type=F32)
            else:
                part = lax.dot_general(dz_ref[...], w_ref[:, at:at + width], _NT, preferred_element_type=F32)
            dyv = part if dyv is None else dyv + part
            at += width
        r = lax.rsqrt(jnp.mean(x * x, axis=-1, keepdims=True) + EPS)
        xhat = x * r
        dxhat = dyv * g_ref[...]
        dh = dres_ref[...] + r * (dxhat - xhat * jnp.mean(dxhat * xhat, axis=-1, keepdims=True))
        dh_ref[...] = dh
        dh16_ref[...] = dh.astype(dh16_ref.dtype)

        @pl.when(pl.program_id(0) == 0)
        def _():
            dg_ref[...] = jnp.zeros_like(dg_ref)

        dg_ref[...] += jnp.sum(dyv * xhat, axis=0, keepdims=True)

    row = pl.BlockSpec((br, d), lambda i: (i, 0))
    vec = pl.BlockSpec((1, d), lambda i: (0, 0))
    return pl.pallas_call(
        body,
        name=name,
        grid=(t // br,),
        in_specs=[pl.BlockSpec((br, width), lambda i: (i, 0)) for width in widths]
        + [pl.BlockSpec(w.shape, lambda i: (0, 0)), row, vec, row],
        out_specs=[row, row, vec],
        out_shape=[jax.ShapeDtypeStruct((t, d), F32), jax.ShapeDtypeStruct((t, d), BF16),
                   jax.ShapeDtypeStruct((1, d), F32)],
        compiler_params=_params("arbitrary"),
    )(*dzs, w, h, g, dres)


def _group_mean(x, width):
    grp = lax.broadcasted_iota(jnp.int32, x.shape, 1) // HEAD_DIM
    out = jnp.zeros_like(x)
    for gi in range(width // HEAD_DIM):
        m = grp == gi
        s = jnp.sum(jnp.where(m, x, 0.0), axis=1, keepdims=True)
        out = jnp.where(m, s, out)
    return out * (1.0 / HEAD_DIM)


def _gelu(x):
    return 0.5 * x * (1.0 + lax.erf(x * (2.0 ** -0.5)))


def _gelu_grad(x):
    cdf = 0.5 * (1.0 + lax.erf(x * (2.0 ** -0.5)))
    pdf = jnp.exp(-0.5 * x * x) * (1.0 / math.sqrt(2.0 * math.pi))
    return cdf + x * pdf


def _shift_down(z, s, row):
    return jnp.where(row >= s, pltpu.roll(z, s, 0), 0.0)


def _shift_up(z, s, row, t):
    return jnp.where(row < t - s, pltpu.roll(z, t - s, 0), 0.0)


def _conv_fwd(proj, conv_w, *, name):
    t = proj.shape[0]
    nb = CONV_W // LANES

    def body(b_ref, c_ref, h_ref, w_ref, o_ref):
        row = lax.broadcasted_iota(jnp.int32, (t, LANES), 0)
        z = c_ref[...] * h_ref[...]
        w = w_ref[...]
        conv = w[2:3, :] * z + w[1:2, :] * _shift_down(z, 1, row) + w[0:1, :] * _shift_down(z, 2, row)
        o_ref[...] = (b_ref[...] * conv).astype(o_ref.dtype)

    return pl.pallas_call(
        body,
        name=name,
        grid=(nb,),
        in_specs=[
            pl.BlockSpec((t, LANES), lambda j: (0, j)),
            pl.BlockSpec((t, LANES), lambda j: (0, nb + j)),
            pl.BlockSpec((t, LANES), lambda j: (0, 2 * nb + j)),
            pl.BlockSpec((CONV_TAPS, LANES), lambda j: (0, j)),
        ],
        out_specs=pl.BlockSpec((t, LANES), lambda j: (0, j)),
        out_shape=jax.ShapeDtypeStruct((t, CONV_W), BF16),
        compiler_params=_params("parallel"),
    )(proj, proj, proj, conv_w)


def _conv_bwd(dmix, proj, conv_w, *, name):
    t = proj.shape[0]

    def body(dy_ref, b_ref, c_ref, h_ref, w_ref, o_ref, dw_ref):
        row = lax.broadcasted_iota(jnp.int32, (t, CONV_W), 0)
        ac, ah = c_ref[...], h_ref[...]
        z = ac * ah
        w = w_ref[...]
        z1 = _shift_down(z, 1, row)
        z2 = _shift_down(z, 2, row)
        conv = w[2:3, :] * z + w[1:2, :] * z1 + w[0:1, :] * z2
        dy = dy_ref[...]
        o_ref[:, 0:CONV_W] = (dy * conv).astype(o_ref.dtype)
        dconv = dy * b_ref[...]
        dz = w[2:3, :] * dconv + w[1:2, :] * _shift_up(dconv, 1, row, t) + w[0:1, :] * _shift_up(dconv, 2, row, t)
        o_ref[:, CONV_W:2 * CONV_W] = (dz * ah).astype(o_ref.dtype)
        o_ref[:, 2 * CONV_W:3 * CONV_W] = (dz * ac).astype(o_ref.dtype)
        dw_ref[...] = jnp.zeros_like(dw_ref)
        dw_ref[0:1, :] = jnp.sum(dconv * z2, axis=0, keepdims=True)
        dw_ref[1:2, :] = jnp.sum(dconv * z1, axis=0, keepdims=True)
        dw_ref[2:3, :] = jnp.sum(dconv * z, axis=0, keepdims=True)

    col = lambda j: pl.BlockSpec((t, CONV_W), lambda i: (0, j))
    return pl.pallas_call(
        body,
        name=name,
        grid=(1,),
        in_specs=[col(0), col(0), col(1), col(2), pl.BlockSpec((CONV_TAPS, CONV_W), lambda i: (0, 0))],
        out_specs=[pl.BlockSpec((t, 3 * CONV_W), lambda i: (0, 0)), pl.BlockSpec((8, CONV_W), lambda i: (0, 0))],
        out_shape=[jax.ShapeDtypeStruct((t, 3 * CONV_W), BF16), jax.ShapeDtypeStruct((8, CONV_W), F32)],
        compiler_params=_params("arbitrary"),
    )(dmix, proj, proj, proj, conv_w)


_QK_BLOCK = 256


def _qk_prep(proj, gq, gk, *, name, br=512):
    t = proj.shape[0]
    br = min(br, t)
    nb = ATTN_W // _QK_BLOCK
    q0 = (3 * CONV_W) // _QK_BLOCK

    def body(q_ref, k_ref, v_ref, gq_ref, gk_ref, qo_ref, ko_ref, vo_ref):
        q = q_ref[...]
        k = k_ref[...]
        rq = lax.rsqrt(_group_mean(q * q, _QK_BLOCK) + EPS)
        rk = lax.rsqrt(_group_mean(k * k, _QK_BLOCK) + EPS)
        qo_ref[...] = ((q * rq * gq_ref[...]).astype(BF16) * QK_SCALE).astype(qo_ref.dtype)
        ko_ref[...] = (k * rk * gk_ref[...]).astype(ko_ref.dtype)
        vo_ref[...] = v_ref[...].astype(vo_ref.dtype)

    col = lambda off: pl.BlockSpec((br, _QK_BLOCK), lambda i, j: (i, off + j))
    vec = pl.BlockSpec((1, _QK_BLOCK), lambda i, j: (0, 0))
    return pl.pallas_call(
        body,
        name=name,
        grid=(t // br, nb),
        in_specs=[col(q0), col(q0 + nb), col(q0 + 2 * nb), vec, vec],
        out_specs=[col(0)] * 3,
        out_shape=[jax.ShapeDtypeStruct((t, ATTN_W), BF16)] * 3,
        compiler_params=_params("parallel", "parallel"),
    )(proj, proj, proj, gq, gk)


def _qk_prep_bwd(dqs, dkn, dv, proj, gq, gk, *, name, br=256):
    t = proj.shape[0]
    br = min(br, t)
    nb = ATTN_W // _QK_BLOCK
    q0 = (3 * CONV_W) // _QK_BLOCK

    def norm_bwd(dy, x, g):
        r = lax.rsqrt(_group_mean(x * x, ATTN_W) + EPS)
        xhat = x * r
        dxhat = dy * g
        dx = r * (dxhat - xhat * _group_mean(dxhat * xhat, ATTN_W))
        return dx, jnp.sum(dy * xhat, axis=0, keepdims=True)

    def body(dq_ref, dk_ref, dv_ref, *rest):
        x_refs, (gq_ref, gk_ref, o_ref, dgq_ref, dgk_ref) = rest[:2 * nb], rest[2 * nb:]
        whole = lambda refs: jnp.concatenate([r[...] for r in refs], axis=1)
        dq, dgq = norm_bwd(dq_ref[...] * QK_SCALE, whole(x_refs[:nb]), whole([gq_ref] * nb))
        dk, dgk = norm_bwd(dk_ref[...], whole(x_refs[nb:]), whole([gk_ref] * nb))
        o_ref[:, 0:ATTN_W] = dq.astype(o_ref.dtype)
        o_ref[:, ATTN_W:2 * ATTN_W] = dk.astype(o_ref.dtype)
        o_ref[:, 2 * ATTN_W:3 * ATTN_W] = dv_ref[...].astype(o_ref.dtype)

        @pl.when(pl.program_id(0) == 0)
        def _():
            dgq_ref[...] = jnp.zeros_like(dgq_ref)
            dgk_ref[...] = jnp.zeros_like(dgk_ref)

        dgq_ref[...] += dgq
        dgk_ref[...] += dgk

    rows = pl.BlockSpec((br, ATTN_W), lambda i: (i, 0))
    col = lambda j: pl.BlockSpec((br, _QK_BLOCK), lambda i: (i, j))
    gain = pl.BlockSpec((1, _QK_BLOCK), lambda i: (0, 0))
    total = pl.BlockSpec((1, ATTN_W), lambda i: (0, 0))
    return pl.pallas_call(
        body,
        name=name,
        grid=(t // br,),
        in_specs=[rows, rows, rows] + [col(q0 + j) for j in range(2 * nb)] + [gain, gain],
        out_specs=[pl.BlockSpec((br, 3 * ATTN_W), lambda i: (i, 0)), total, total],
        out_shape=[jax.ShapeDtypeStruct((t, 3 * ATTN_W), BF16)] + [jax.ShapeDtypeStruct((1, ATTN_W), F32)] * 2,
        compiler_params=_params("arbitrary"),
    )(dqs, dkn, dv, *[proj] * (2 * nb), gq, gk)


def _key_order_matrix(tb, relation):
    jj = lax.broadcasted_iota(jnp.int32, (tb, tb), 0)
    ss = lax.broadcasted_iota(jnp.int32, (tb, tb), 1)
    return relation(jj, ss).astype(BF16)


def _log_sigmoids(z):
    lb = jnp.minimum(z, 0.0) - jnp.log(1.0 + jnp.exp(-jnp.abs(z)))
    return lb, lb - z


def _below_diagonal(tb):
    return lax.broadcasted_iota(jnp.int32, (tb, tb), 1) < lax.broadcasted_iota(jnp.int32, (tb, tb), 0)


_NT = (((1,), (1,)), ((), ()))
_TN = (((0,), (0,)), ((), ()))
_ATTN_BLOCK = 256
_ATTN_FWD_UNROLL = 4
_ATTN_BWD_UNROLL = 3


def _attn_fwd(qs, kn, v, *, name, tb=_ATTN_BLOCK, unroll=_ATTN_FWD_UNROLL):
    t = qs.shape[0]
    tb = min(tb, t)
    assert t % tb == 0
    n_pairs = ATTN_W // LANES

    def body(q_ref, k_ref, v_ref, o_ref, lt_ref, acc_ref, carry_ref):
        qb = pl.program_id(1)
        half = lax.broadcasted_iota(jnp.int32, (1, LANES), 1) // HEAD_DIM
        later = _key_order_matrix(tb, lambda j, s: j > s)
        acc_ref[...] = jnp.zeros_like(acc_ref)
        carry_ref[...] = jnp.zeros_like(carry_ref)
        q = q_ref[...]
        qh = [jnp.where(half == h, q, jnp.zeros_like(q)) for h in range(2)]

        def tiles(kbs, first_is_diagonal):
            blk = []
            for kb in kbs:
                start = pl.multiple_of(kb * tb, tb)
                blk.append((k_ref[pl.ds(start, tb), :], v_ref[pl.ds(start, tb), :]))
            chains = [(h, j) for j in range(len(kbs)) for h in range(2)]
            masked = [first_is_diagonal and j == 0 for _, j in chains]
            z = [lax.dot_general(qh[h], blk[j][0], _NT, preferred_element_type=F32) for h, j in chains]
            causal = _below_diagonal(tb) if first_is_diagonal else None
            lb, lr = [], []
            for zi, mask in zip(z, masked):
                b, r = _log_sigmoids(zi)
                lb.append(b)
                lr.append(jnp.where(causal, r, 0.0) if mask else r)
            suffix = [jnp.dot(r.astype(BF16), later, preferred_element_type=F32) for r in lr]
            carry = [carry_ref[0], carry_ref[1]]
            w = []
            for i, (h, j) in enumerate(chains):
                wi = jnp.exp(lb[i] + (suffix[i] + carry[h][:, 0:1]))
                w.append((jnp.where(causal, wi, 0.0) if masked[i] else wi).astype(BF16))
                carry[h] = carry[h] + jnp.sum(lr[i], axis=1, keepdims=True)
            for i, (h, j) in enumerate(chains):
                vh = jnp.where(half == h, blk[j][1], jnp.zeros_like(blk[j][1]))
                acc_ref[h] += jnp.dot(w[i], vh, preferred_element_type=F32)
            carry_ref[0] = carry[0]
            carry_ref[1] = carry[1]

        @pl.when(qb == 0)
        def _():
            tiles([qb], True)

        @pl.when(qb > 0)
        def _():
            tiles([qb, qb - 1], True)
            rest = qb - 1

            def step(i, _):
                kb = rest - 1 - unroll * i
                tiles([kb - u for u in range(unroll)], False)
                return 0

            lax.fori_loop(0, rest // unroll, step, 0)
            for left in range(1, unroll):

                @pl.when(rest % unroll == left)
                def _(left=left):
                    tiles([left - 1 - u for u in range(left)], False)

        o_ref[...] = (acc_ref[0] + acc_ref[1]).astype(o_ref.dtype)
        lt_ref[...] = jnp.where(half == 0, carry_ref[0], carry_ref[1])

    return pl.pallas_call(
        body,
        name=name,
        grid=(n_pairs, t // tb),
        in_specs=[
            pl.BlockSpec((tb, LANES), lambda p, i: (i, p)),
            pl.BlockSpec((t, LANES), lambda p, i: (0, p)),
            pl.BlockSpec((t, LANES), lambda p, i: (0, p)),
        ],
        out_specs=[pl.BlockSpec((tb, LANES), lambda p, i: (i, p))] * 2,
        out_shape=[jax.ShapeDtypeStruct((t, ATTN_W), BF16), jax.ShapeDtypeStruct((t, ATTN_W), F32)],
        scratch_shapes=[pltpu.VMEM((2, tb, LANES), F32), pltpu.VMEM((2, tb, LANES), F32)],
        compiler_params=_params("parallel", "parallel"),
    )(qs, kn, v)


def _attn_bwd(dmix, qs, kn, v, lt, order_after, *, name, tb=_ATTN_BLOCK, unroll=_ATTN_BWD_UNROLL):
    t = qs.shape[0]
    tb = min(tb, t)
    assert t % tb == 0
    n_pairs = ATTN_W // LANES
    dy0 = CONV_W // LANES

    def body(do_ref, q_ref, k_ref, v_ref, lt_ref, order_ref, dq_ref, dk_ref, dv_ref, dqacc_ref, cc_ref, cg_ref):
        qb = pl.program_id(1)
        half = lax.broadcasted_iota(jnp.int32, (1, LANES), 1) // HEAD_DIM
        lane = lax.broadcasted_iota(jnp.int32, (tb, LANES), 1)
        later = _key_order_matrix(tb, lambda j, s: j > s)
        before = _key_order_matrix(tb, lambda j, s: j < s)
        q = q_ref[...]
        do = do_ref[...].astype(BF16)
        lt = lt_ref[...]
        qh = [jnp.where(half == h, q, jnp.zeros_like(q)) for h in range(2)]
        doh = [jnp.where(half == h, do, jnp.zeros_like(do)) for h in range(2)]
        lth = [jnp.sum(jnp.where(lane == h * HEAD_DIM, lt, 0.0), axis=1, keepdims=True) for h in range(2)]

        @pl.when(qb == 0)
        def _():
            dk_ref[...] = jnp.zeros_like(dk_ref)
            dv_ref[...] = jnp.zeros_like(dv_ref)

        dqacc_ref[...] = jnp.zeros_like(dqacc_ref)
        cc_ref[...] = jnp.zeros_like(cc_ref)
        cg_ref[...] = jnp.zeros_like(cg_ref)

        def tiles(kbs, last_is_diagonal):
            starts = [pl.multiple_of(kb * tb, tb) for kb in kbs]
            blk = [(k_ref[pl.ds(s, tb), :], v_ref[pl.ds(s, tb), :]) for s in starts]
            chains = [(h, j) for j in range(len(kbs)) for h in range(2)]
            masked = [last_is_diagonal and j == len(kbs) - 1 for _, j in chains]
            z = [lax.dot_general(qh[h], blk[j][0], _NT, preferred_element_type=F32) for h, j in chains]
            da = [lax.dot_general(doh[h], jnp.where(half == h, blk[j][1], jnp.zeros_like(blk[j][1])), _NT,
                                  preferred_element_type=F32) for h, j in chains]
            causal = _below_diagonal(tb) if last_is_diagonal else None
            lb, lr = [], []
            for zi, mask in zip(z, masked):
                b, r = _log_sigmoids(zi)
                lb.append(b)
                lr.append(jnp.where(causal, r, 0.0) if mask else r)
            suffix = [jnp.dot(r.astype(BF16), later, preferred_element_type=F32) for r in lr]
            cc = [cc_ref[0], cc_ref[1]]
            cg = [cg_ref[0], cg_ref[1]]
            a16, g = [], []
            for i, (h, j) in enumerate(chains):
                cc[h] = cc[h] + jnp.sum(lr[i], axis=1, keepdims=True)
                a = jnp.exp(lb[i] + suffix[i] + (lth[h] - cc[h][:, 0:1]))
                if masked[i]:
                    a = jnp.where(causal, a, 0.0)
                a16.append(a.astype(BF16))
                g.append(da[i] * a)
            g_before = [jnp.dot(gi.astype(BF16), before, preferred_element_type=F32) for gi in g]
            dz = []
            for i, (h, j) in enumerate(chains):
                dzi = g[i] - jnp.exp(lb[i]) * (g[i] + (g_before[i] + cg[h][:, 0:1]))
                dz.append((jnp.where(causal, dzi, 0.0) if masked[i] else dzi).astype(BF16))
                cg[h] = cg[h] + jnp.sum(g[i], axis=1, keepdims=True)
            for i, (h, j) in enumerate(chains):
                kh = jnp.where(half == h, blk[j][0], jnp.zeros_like(blk[j][0]))
                dqacc_ref[h] += jnp.dot(dz[i], kh, preferred_element_type=F32)
                dk_ref[pl.ds(starts[j], tb), :] += lax.dot_general(dz[i], qh[h], _TN, preferred_element_type=F32)
                dv_ref[pl.ds(starts[j], tb), :] += lax.dot_general(a16[i], doh[h], _TN, preferred_element_type=F32)
            for h in range(2):
                cc_ref[h] = cc[h]
                cg_ref[h] = cg[h]

        def step(i, _):
            kb = unroll * i
            tiles([kb + u for u in range(unroll)], False)
            return 0

        lax.fori_loop(0, qb // unroll, step, 0)
        for left in range(1, unroll):

            @pl.when(qb % unroll == left)
            def _(left=left):
                tiles([qb - left + u for u in range(left)], False)

        tiles([qb], True)
        dq_ref[...] = dqacc_ref[0] + dqacc_ref[1]

    qblk = pl.BlockSpec((tb, LANES), lambda p, i: (i, p))
    whole = pl.BlockSpec((t, LANES), lambda p, i: (0, p))
    return pl.pallas_call(
        body,
        name=name,
        grid=(n_pairs, t // tb),
        in_specs=[pl.BlockSpec((tb, LANES), lambda p, i: (i, dy0 + p)), qblk, whole, whole, qblk,
                  pl.BlockSpec(order_after.shape, lambda p, i: (0, 0))],
        out_specs=[qblk, whole, whole],
        out_shape=[jax.ShapeDtypeStruct((t, ATTN_W), F32)] * 3,
        scratch_shapes=[pltpu.VMEM((2, tb, LANES), F32)] * 3,
        compiler_params=_params("parallel", "arbitrary"),
    )(dmix, qs, kn, v, lt, order_after)


_SGU_CHUNKS_PER_STEP = 4


def _sgu_rows(t):
    return CHUNK * math.gcd(_SGU_CHUNKS_PER_STEP, t // CHUNK)


def _sgu_weights(w_ref):
    tt = lax.broadcasted_iota(jnp.int32, (CHUNK, CHUNK), 0)
    ss = lax.broadcasted_iota(jnp.int32, (CHUNK, CHUNK), 1)
    tril = ss <= tt
    return [jnp.where(tril, w_ref[gi], 0.0).astype(BF16) for gi in range(SGU_HEADS)], tril


def _sgu_fwd(proj, g_v, w_s, b_exp, *, name):
    t = proj.shape[0]
    u0 = (3 * CONV_W + 3 * ATTN_W) // SGU_W
    rows = _sgu_rows(t)

    def body(u_ref, v_ref, g_ref, w_ref, b_ref, o_ref):
        grp = lax.broadcasted_iota(jnp.int32, (1, SGU_W), 1) // HEAD_DIM
        wm, _ = _sgu_weights(w_ref)
        gain, bias = g_ref[...], b_ref[...]
        for c in range(rows // CHUNK):
            chunk = pl.ds(c * CHUNK, CHUNK)
            u = _gelu(u_ref[chunk, :])
            vv = _gelu(v_ref[chunk, :])
            vn = (vv * lax.rsqrt(_group_mean(vv * vv, SGU_W) + EPS) * gain).astype(BF16)
            sv = bias
            for gi in range(SGU_HEADS):
                sv = sv + jnp.dot(wm[gi], jnp.where(grp == gi, vn, jnp.zeros_like(vn)), preferred_element_type=F32)
            o_ref[chunk, :] = (u * sv).astype(o_ref.dtype)

    return pl.pallas_call(
        body,
        name=name,
        grid=(t // rows,),
        in_specs=[
            pl.BlockSpec((rows, SGU_W), lambda i: (i, u0)),
            pl.BlockSpec((rows, SGU_W), lambda i: (i, u0 + 1)),
            pl.BlockSpec((1, SGU_W), lambda i: (0, 0)),
            pl.BlockSpec((SGU_HEADS, CHUNK, CHUNK), lambda i: (0, 0, 0)),
            pl.BlockSpec((CHUNK, SGU_W), lambda i: (0, 0)),
        ],
        out_specs=pl.BlockSpec((rows, SGU_W), lambda i: (i, 0)),
        out_shape=jax.ShapeDtypeStruct((t, SGU_W), BF16),
        compiler_params=_params("parallel"),
    )(proj, proj, g_v, w_s, b_exp)


def _sgu_bwd(dmix, proj, g_v, w_s, b_exp, *, name):
    t = proj.shape[0]
    u0 = (3 * CONV_W + 3 * ATTN_W) // SGU_W
    dy0 = (CONV_W + ATTN_W) // SGU_W
    rows = _sgu_rows(t)

    def body(dy_ref, u_ref, v_ref, g_ref, w_ref, b_ref, o_ref, dg_ref, dw_ref, db_ref):
        grp = lax.broadcasted_iota(jnp.int32, (1, SGU_W), 1) // HEAD_DIM
        gain, bias = g_ref[...], b_ref[...]
        wm, tril = _sgu_weights(w_ref)

        @pl.when(pl.program_id(0) == 0)
        def _():
            dg_ref[...] = jnp.zeros_like(dg_ref)
            dw_ref[...] = jnp.zeros_like(dw_ref)
            db_ref[...] = jnp.zeros_like(db_ref)

        dg = jnp.zeros_like(gain)
        db = jnp.zeros_like(bias)
        dw = [jnp.zeros((CHUNK, CHUNK), F32) for _ in range(SGU_HEADS)]
        for c in range(rows // CHUNK):
            chunk = pl.ds(c * CHUNK, CHUNK)
            cu, cv = u_ref[chunk, :], v_ref[chunk, :]
            u = _gelu(cu)
            vv = _gelu(cv)
            r = lax.rsqrt(_group_mean(vv * vv, SGU_W) + EPS)
            xhat = vv * r
            vn = (xhat * gain).astype(BF16)
            vng = [jnp.where(grp == gi, vn, jnp.zeros_like(vn)) for gi in range(SGU_HEADS)]
            sv = bias
            for gi in range(SGU_HEADS):
                sv = sv + jnp.dot(wm[gi], vng[gi], preferred_element_type=F32)
            dy = dy_ref[chunk, :]
            o_ref[chunk, 0:SGU_W] = (dy * sv * _gelu_grad(cu)).astype(o_ref.dtype)
            dsv = dy * u
            dsv16 = dsv.astype(BF16)
            db = db + dsv
            dvn = jnp.zeros_like(dsv)
            for gi in range(SGU_HEADS):
                dw[gi] = dw[gi] + lax.dot_general(dsv16, vng[gi], _NT, preferred_element_type=F32)
                dvn_g = lax.dot_general(wm[gi], dsv16, _TN, preferred_element_type=F32)
                dvn = jnp.where(grp == gi, dvn_g, dvn)
            dg = dg + jnp.sum(dvn * xhat, axis=0, keepdims=True)
            dxhat = dvn * gain
            dvv = r * (dxhat - xhat * _group_mean(dxhat * xhat, SGU_W))
            o_ref[chunk, SGU_W:2 * SGU_W] = (dvv * _gelu_grad(cv)).astype(o_ref.dtype)
        dg_ref[...] += dg
        db_ref[...] += db
        for gi in range(SGU_HEADS):
            dw_ref[gi] += jnp.where(tril, dw[gi], 0.0)

    return pl.pallas_call(
        body,
        name=name,
        grid=(t // rows,),
        in_specs=[
            pl.BlockSpec((rows, SGU_W), lambda i: (i, dy0)),
            pl.BlockSpec((rows, SGU_W), lambda i: (i, u0)),
            pl.BlockSpec((rows, SGU_W), lambda i: (i, u0 + 1)),
            pl.BlockSpec((1, SGU_W), lambda i: (0, 0)),
            pl.BlockSpec((SGU_HEADS, CHUNK, CHUNK), lambda i: (0, 0, 0)),
            pl.BlockSpec((CHUNK, SGU_W), lambda i: (0, 0)),
        ],
        out_specs=[
            pl.BlockSpec((rows, 2 * SGU_W), lambda i: (i, 0)),
            pl.BlockSpec((1, SGU_W), lambda i: (0, 0)),
            pl.BlockSpec((SGU_HEADS, CHUNK, CHUNK), lambda i: (0, 0, 0)),
            pl.BlockSpec((CHUNK, SGU_W), lambda i: (0, 0)),
        ],
        out_shape=[
            jax.ShapeDtypeStruct((t, 2 * SGU_W), BF16),
            jax.ShapeDtypeStruct((1, SGU_W), F32),
            jax.ShapeDtypeStruct((SGU_HEADS, CHUNK, CHUNK), F32),
            jax.ShapeDtypeStruct((CHUNK, SGU_W), F32),
        ],
        compiler_params=_params("arbitrary"),
    )(dmix, proj, proj, g_v, w_s, b_exp)


def _ple_bwd(dh, gate, pp, order_after, *, name, br=512):
    t, d = dh.shape
    br = min(br, t)

    def body(dh_ref, g_ref, p_ref, order_ref, dpre_ref, dpp_ref):
        dhv, g = dh_ref[...], g_ref[...]
        dpre_ref[...] = (dhv * p_ref[...] * g * (1.0 - g)).astype(dpre_ref.dtype)
        dpp_ref[...] = (dhv * g).astype(dpp_ref.dtype)

    row = pl.BlockSpec((br, d), lambda i: (i, 0))
    return pl.pallas_call(
        body,
        name=name,
        grid=(t // br,),
        in_specs=[row] * 3 + [pl.BlockSpec(order_after.shape, lambda i: (0, 0))],
        out_specs=[row] * 2,
        out_shape=[jax.ShapeDtypeStruct((t, d), BF16)] * 2,
        compiler_params=_params("parallel"),
    )(dh, gate, pp, order_after)


def _loss_head(y, target, *, name, br=512):
    t, d = y.shape
    br = min(br, t)

    def body(y_ref, t_ref, dy_ref, loss_ref):
        err = y_ref[...] - t_ref[...]
        dy_ref[...] = err * (1.0 / d)

        @pl.when(pl.program_id(0) == 0)
        def _():
            loss_ref[...] = jnp.zeros_like(loss_ref)

        loss_ref[...] += 0.5 * jnp.sum(jnp.sum(err * err, axis=1, keepdims=True) * (1.0 / d), axis=0, keepdims=True)

    row = pl.BlockSpec((br, d), lambda i: (i, 0))
    return pl.pallas_call(
        body,
        name=name,
        grid=(t // br,),
        in_specs=[row, row],
        out_specs=[row, pl.BlockSpec((8, LANES), lambda i: (0, 0))],
        out_shape=[jax.ShapeDtypeStruct((t, d), F32), jax.ShapeDtypeStruct((8, LANES), F32)],
        compiler_params=_params("arbitrary"),
    )(y, target)


def _adamw_update(w, g, m, v):
    nm = ADAM_B1 * m + (1.0 - ADAM_B1) * g
    nv = ADAM_B2 * v + (1.0 - ADAM_B2) * (g * g)
    m_hat = nm / (1.0 - ADAM_B1 ** ADAM_STEP)
    v_hat = nv / (1.0 - ADAM_B2 ** ADAM_STEP)
    return -ADAM_LR * (m_hat / (jnp.sqrt(v_hat) + ADAM_EPS) + ADAM_WD * w), nm, nv


def _adamw(w, g, m, v, *, name, br=512):
    r, c = w.shape
    br = _row_block(r, br)

    def body(w_ref, g_ref, m_ref, v_ref, d_ref, nm_ref, nv_ref):
        d_ref[...], nm_ref[...], nv_ref[...] = _adamw_update(w_ref[...], g_ref[...], m_ref[...], v_ref[...])

    row = pl.BlockSpec((br, c), lambda i: (i, 0))
    return pl.pallas_call(
        body,
        name=name,
        grid=(r // br,),
        in_specs=[row] * 4,
        out_specs=[row] * 3,
        out_shape=[jax.ShapeDtypeStruct((r, c), F32)] * 3,
        compiler_params=_params("parallel"),
    )(w, g, m, v)


def _sum_slots(per_layer, *, name):
    counts = [len(arrays) for arrays in per_layer]
    flat = [a for arrays in per_layer for a in arrays]

    def body(*refs):
        ins, outs = refs[:len(flat)], refs[len(flat):]
        at = 0
        for o_ref, count in zip(outs, counts, strict=True):
            for li in range(count):
                acc = ins[at + li][0]
                for j in range(1, N_DEV):
                    acc = acc + ins[at + li][j]
                o_ref[li] = acc
            at += count

    return pl.pallas_call(
        body,
        name=name,
        out_shape=[jax.ShapeDtypeStruct((len(arrays), *arrays[0].shape[1:]), F32) for arrays in per_layer],
        compiler_params=pltpu.CompilerParams(vmem_limit_bytes=VMEM_LIMIT_BYTES),
    )(*flat)


_ADAMW_BLOCK_ELEMS = 192 * 1024


def _adamw_reduce(w, arrived, m, v, *, name):
    depth, r, c = w.shape
    br = _row_block(r, max(BF16_TILE_ROWS, _ADAMW_BLOCK_ELEMS // (-(-c // LANES) * LANES)))

    def body(w_ref, m_ref, v_ref, *rest):
        parts, (g_ref, d_ref, nm_ref, nv_ref) = rest[:depth], rest[depth:]
        for li in range(depth):

            @pl.when(pl.program_id(0) == li)
            def _(li=li):
                g = parts[li][0].astype(F32)
                for j in range(1, N_DEV):
                    g = g + parts[li][j].astype(F32)
                g_ref[...] = g
                d_ref[...], nm_ref[...], nv_ref[...] = _adamw_update(w_ref[...], g, m_ref[...], v_ref[...])

    cur = pl.BlockSpec((None, br, c), lambda l, i: (l, i, 0))
    slots = [pl.BlockSpec((N_DEV, br, c), lambda l, i, li=li: (0, jnp.where(l == li, i, 0), 0)) for li in range(depth)]
    return pl.pallas_call(
        body,
        name=name,
        grid=(depth, r // br),
        in_specs=[cur, cur, cur] + slots,
        out_specs=[cur] * 4,
        out_shape=[jax.ShapeDtypeStruct((depth, r, c), F32)] * 4,
        compiler_params=_params("arbitrary", "arbitrary"),
    )(w, m, v, *arrived)


def _my_place():
    return lax.axis_index("x"), lax.axis_index("y"), lax.axis_index("c")


def _flip(v, bit):
    return 1 - v if bit else v


def _slot_of(px, py, pc):
    return 4 * px + 2 * py + pc


_ANY = pl.BlockSpec(memory_space=pl.ANY)


_HBM = pl.BlockSpec(memory_space=pltpu.HBM)
_SEM = pl.BlockSpec(memory_space=pltpu.SEMAPHORE)
_DATAFLOW = pltpu.SideEffectType.DATAFLOW_SIDE_EFFECTING


_GATHER, _GATHER_COLUMNS, _SCATTER = "gather", "gather_columns", "scatter"


def _landing_shape(a, mode):
    if mode == _SCATTER:
        return a.shape
    if mode == _GATHER_COLUMNS:
        return (a.shape[0], N_DEV * a.shape[1])
    return (N_DEV, *a.shape)


_DIRECT, _NEAR, _RELAY = "direct", "near", "relay"
_OTHER_CHIPS = (2, 4, 6)
_SIBLING = 1


def _exchange_copies(src_refs, land_refs, send_sem, recv_sem, modes, hops=_DIRECT):
    mx, my, mc = _my_place()
    peer_of = lambda k: (_flip(mx, k & 4), _flip(my, k & 2), _flip(mc, k & 1))
    mine = _slot_of(mx, my, mc)

    def block(land, mode, slot):
        if mode == _GATHER_COLUMNS:
            n = land.shape[1] // N_DEV
            return land.at[:, pl.ds(pl.multiple_of(slot * n, LANES), n)]
        return land.at[slot]

    def remote_copy(src, dst, to):
        return pltpu.make_async_remote_copy(src_ref=src, dst_ref=dst, send_sem=send_sem, recv_sem=recv_sem,
                                            device_id=to, device_id_type=MESH)

    remote, local = [], []
    for src, land, mode in zip(src_refs, land_refs, modes, strict=True):
        if hops == _RELAY:
            assert mode != _SCATTER
            for k in _OTHER_CHIPS:
                came = block(land, mode, _slot_of(*peer_of(k)))
                remote.append(remote_copy(came, came, peer_of(_SIBLING)))
            continue
        dst = block(land, mode, mine)
        for k in ((_SIBLING,) + _OTHER_CHIPS if hops == _NEAR else range(1, N_DEV)):
            remote.append(remote_copy(src.at[_slot_of(*peer_of(k))] if mode == _SCATTER else src, dst, peer_of(k)))
        local.append(pltpu.make_async_copy(src.at[mine] if mode == _SCATTER else src, dst, recv_sem))
    return remote, local


def _wait_copies(remote, local):
    for cp in remote:
        cp.wait_send()
        cp.wait_recv()
    for cp in local:
        cp.wait()


def _exchange_start(groups, after, *, name, hops=_DIRECT):
    sizes = [len(srcs) for srcs, _ in groups]
    n, n_sems = sum(sizes), 2 * len(groups)
    srcs = [a for arrays, _ in groups for a in arrays]
    lands = [lax.empty(_landing_shape(a, mode), a.dtype)
             for arrays, modes in groups for a, mode in zip(arrays, modes, strict=True)]
    offsets = [sum(sizes[:g]) for g in range(len(groups))]

    def body(*refs):
        sems = refs[2 * n + 1:2 * n + 1 + n_sems]
        for g, (off, size, (_, modes)) in enumerate(zip(offsets, sizes, groups)):
            remote, local = _exchange_copies(refs[off:off + size], refs[n + off:n + off + size], sems[2 * g],
                                             sems[2 * g + 1], modes, hops)
            for cp in remote + local:
                cp.start()
        refs[-1][...] = jnp.zeros_like(refs[-1])

    thru = [pltpu.HBM(a.shape, a.dtype) for a in (*srcs, *lands)]
    out = pl.pallas_call(
        body,
        name=name,
        in_specs=[_HBM] * (2 * n) + [_ANY],
        out_specs=(*[_SEM] * n_sems, *[_HBM] * (2 * n), pl.BlockSpec(memory_space=pltpu.VMEM)),
        out_shape=(*[pltpu.SemaphoreType.DMA(())] * n_sems, *thru, jax.ShapeDtypeStruct((8, LANES), F32)),
        input_output_aliases={i: n_sems + i for i in range(2 * n)},
        compiler_params=pltpu.CompilerParams(has_side_effects=_DATAFLOW),
    )(*[pltpu.with_memory_space_constraint(a, pltpu.HBM) for a in (*srcs, *lands)], after)
    sems, arrays = out[:n_sems], out[n_sems:-1]
    started = [(sems[2 * g], sems[2 * g + 1], *arrays[off:off + size], *arrays[n + off:n + off + size])
               for g, (off, size) in enumerate(zip(offsets, sizes))]
    return started, out[-1]


def _exchange_relay(started, after, *, modes, regroup, name):
    send_sem, recv_sem, *thru = started
    n, n_sems = len(thru) // 2, 2 * len(regroup)

    def body(*refs):
        srcs, lands = refs[:n], refs[n:2 * n]
        _wait_copies(*_exchange_copies(srcs, lands, refs[2 * n], refs[2 * n + 1], modes, _NEAR))
        sems = refs[2 * n + 3:2 * n + 3 + n_sems]
        for g, members in enumerate(regroup):
            remote, _ = _exchange_copies([srcs[i] for i in members], [lands[i] for i in members], sems[2 * g],
                                         sems[2 * g + 1], [modes[i] for i in members], _RELAY)
            for cp in remote:
                cp.start()
        refs[-1][...] = jnp.zeros_like(refs[-1])

    out = pl.pallas_call(
        body,
        name=name,
        in_specs=[_HBM] * (2 * n) + [_SEM, _SEM, _ANY],
        out_specs=(*[_SEM] * n_sems, *[_HBM] * (2 * n), pl.BlockSpec(memory_space=pltpu.VMEM)),
        out_shape=(*[pltpu.SemaphoreType.DMA(())] * n_sems, *[pltpu.HBM(a.shape, a.dtype) for a in thru],
                   jax.ShapeDtypeStruct((8, LANES), F32)),
        input_output_aliases={i: n_sems + i for i in range(2 * n)},
        compiler_params=pltpu.CompilerParams(has_side_effects=_DATAFLOW),
    )(*thru, send_sem, recv_sem, after)
    sems, arrays = out[:n_sems], out[n_sems:-1]
    groups = [(sems[2 * g], sems[2 * g + 1], *[arrays[i] for i in members], *[arrays[n + i] for i in members])
              for g, members in enumerate(regroup)]
    return groups, out[-1]


def _exchange_wait(started, after, *, modes, name, hops=_DIRECT):
    send_sem, recv_sem, *thru = started
    n = len(thru) // 2

    def body(*refs):
        _wait_copies(*_exchange_copies(refs[:n], refs[n:2 * n], refs[2 * n], refs[2 * n + 1], modes, hops))

    out = pl.pallas_call(
        body,
        name=name,
        in_specs=[_HBM] * (2 * n) + [_SEM, _SEM, _ANY],
        out_specs=[_HBM] * (2 * n),
        out_shape=[pltpu.HBM(a.shape, a.dtype) for a in thru],
        input_output_aliases={i: i for i in range(2 * n)},
        compiler_params=pltpu.CompilerParams(has_side_effects=_DATAFLOW),
    )(*thru, send_sem, recv_sem, after)
    return out[n:]


def _gather_columns(g):
    return jnp.moveaxis(g, 0, 1).reshape(g.shape[1], -1)


def _split_rows(w):
    return w.reshape(N_DEV, w.shape[0] // N_DEV, w.shape[1])


_FIRST = ("w_in", "conv_w")
_REST = ("w_out", "w_ff1", "w_ff2", "w_ple_gate", "w_ple_proj")
_REST_GROUPS = (("w_out",), ("w_ff1",), ("w_ff2",), ("w_ple_gate", "w_ple_proj"))
_BIG = ("w_in",) + _REST
_GATHER_MODE = dict(w_in=_GATHER, conv_w=_GATHER, w_out=_GATHER, w_ff1=_GATHER_COLUMNS, w_ff2=_GATHER,
                    w_ple_gate=_GATHER, w_ple_proj=_GATHER_COLUMNS)
_RELAYOUT_AFTER_GATHER = ("conv_w",)
_SMALL = ("norm1_g", "q_norm_g", "k_norm_g", "sgu_norm_g", "sgu_w", "sgu_b", "norm2_g", "norm3_g")
_ORDER = ("norm1_g", "w_in", "conv_w", "q_norm_g", "k_norm_g", "sgu_norm_g", "sgu_w", "sgu_b", "w_out", "norm2_g",
          "w_ff1", "w_ff2", "norm3_g", "w_ple_gate", "w_ple_proj")


def _whole_matrices(names, landed):
    return {k: _gather_columns(g) if k in _RELAYOUT_AFTER_GATHER else g.reshape(-1, g.shape[-1])
            for k, g in zip(names, landed, strict=True)}


_NORM_FUSED_ROWS = 256


def _layer_forward(h0, hn1, p16, s, li, w_first, gathered, next_norm_g):
    nm = lambda k: f"{k}_l{li}"
    t, d = h0.shape
    w = dict(w_first)

    def add_and_norm(acc, res, gain):
        h = res + acc
        return h, _rms_rows(h, gain)

    proj = _matmul(hn1, w["w_in"], name=nm("proj"), tb=True, bm=t, bn=256)
    y_a = _conv_fwd(proj, w["conv_w"], name=nm("conv"))
    qs, kn, v = _qk_prep(proj, s["gq"], s["gk"], name=nm("qkprep"))
    y_b, lt = _attn_fwd(qs, kn, v, name=nm("attn"))
    gathered["relay_rest"](y_b)
    y_c = _sgu_fwd(proj, s["sgu_norm_g"], s["sgu_w"], s["b_exp"], name=nm("sgu"))
    mix = jnp.concatenate([y_a, y_b, y_c], axis=1)
    w.update(gathered["fetch"](0, mix))
    h1, hn2 = _matmul(mix, w["w_out"], name=nm("out"), bm=_NORM_FUSED_ROWS, bn=d, out_dtypes=(F32, BF16),
                      extras=(h0,), row_vectors=(s["norm2_g"],), epilogue=add_and_norm)
    w.update(gathered["fetch"](1, hn2))
    f = _matmul(hn2, w["w_ff1"], name=nm("ff1"), bm=t, bn=512, out_dtypes=(BF16,),
                epilogue=lambda acc: (jnp.square(jnp.maximum(acc, 0.0)),))
    w.update(gathered["fetch"](2, f))
    gathered["relay_next"](f)
    h2, hn3 = _matmul(f, w["w_ff2"], name=nm("ff2"), bm=_NORM_FUSED_ROWS, bn=d, out_dtypes=(F32, BF16),
                      extras=(h1,), row_vectors=(s["norm3_g"],), epilogue=add_and_norm)
    w.update(gathered["fetch"](3, hn3))
    w_next = gathered["fetch_next"](hn3)
    pp = _matmul(p16, w["w_ple_proj"], name=nm("pleproj"), bm=t, bn=512)

    def gate_epilogue(acc, pp_blk, h_blk, *gain):
        gate = jax.nn.sigmoid(acc)
        h = h_blk + gate * pp_blk
        return (h, gate) + tuple(_rms_rows(h, g) for g in gain)

    fused_norm = () if next_norm_g is None else (next_norm_g,)
    h3, gate, *hn1_next = _matmul(hn3, w["w_ple_gate"], name=nm("plegate"), bm=_NORM_FUSED_ROWS, bn=d,
                                  out_dtypes=(F32, F32) + (BF16,) * len(fused_norm), extras=(pp, h2),
                                  row_vectors=fused_norm, epilogue=gate_epilogue)
    saved = dict(h0=h0, hn1=hn1, proj=proj, qs=qs, kn=kn, v=v, lt=lt, mix=mix, h1=h1, hn2=hn2, f=f, h2=h2,
                 hn3=hn3, pp=pp, gate=gate, p16=p16)
    return h3, (hn1_next[0] if hn1_next else None), w, w_next, saved


def _layer_backward(dh3, a, w, s, li, order_after, start_rest):
    nm = lambda k: f"{k}_bwd_l{li}"
    t = dh3.shape[0]
    dpre, dpp = _ple_bwd(dh3, a["gate"], a["pp"], order_after, name=nm("ple"))
    g_gate = _weight_grad(a["hn3"], [dpre], name=nm("dwgate"))
    g_proj = _weight_grad(a["p16"], [dpp], name=nm("dwproj"), column_shards=True)
    dh2, dh2_16, g_n3 = _matmul_rms_bwd([dpre], w["w_ple_gate"], a["h2"], s["norm3_g"], dh3, name=nm("dh2"))
    du = _matmul(dh2_16, w["w_ff2"], name=nm("du"), tb=True, bm=t, bn=512, out_dtypes=(BF16,), extras=(a["f"],),
                 epilogue=lambda acc, f: (acc * (2.0 * jnp.sqrt(f.astype(F32))),))
    g_ff2 = _weight_grad(a["f"], [dh2_16], name=nm("dwff2"))
    g_ff1 = _weight_grad(a["hn2"], [du], name=nm("dwff1"), column_shards=True)
    dh1, dh1_16, g_n2 = _matmul_rms_bwd([du], w["w_ff1"], a["h1"], s["norm2_g"], dh2, name=nm("dh1"))
    dmix = _matmul(dh1_16, w["w_out"], name=nm("dmix"), tb=True, bm=t, bn=256)
    g_out = _weight_grad(a["mix"], [dh1_16], name=nm("dwout"))
    started = start_rest(dict(w_out=_split_rows(g_out), w_ff1=g_ff1, w_ff2=_split_rows(g_ff2),
                              w_ple_gate=_split_rows(g_gate), w_ple_proj=g_proj), dmix)
    d_conv, g_conv = _conv_bwd(dmix, a["proj"], w["conv_w"], name=nm("conv"))
    dqs, dkn, dv = _attn_bwd(dmix, a["qs"], a["kn"], a["v"], a["lt"], started, name=nm("attn"))
    d_qkv, g_q, g_k = _qk_prep_bwd(dqs, dkn, dv, a["proj"], s["gq"], s["gk"], name=nm("qkprep"))
    d_sgu, g_sn, g_sw, g_sb = _sgu_bwd(dmix, a["proj"], s["sgu_norm_g"], s["sgu_w"], s["b_exp"], name=nm("sgu"))
    dproj = [d_conv, d_qkv, d_sgu]
    g_in = jnp.concatenate([_weight_grad(piece, [a["hn1"]], name=nm(f"dwin{i}")) for i, piece in enumerate(dproj)])
    dh0, _, g_n1 = _matmul_rms_bwd(dproj, w["w_in"], a["h0"], s["norm1_g"], dh1, name=nm("dh0"), w_is_k_by_d=True)
    small = dict(norm1_g=g_n1, norm2_g=g_n2, norm3_g=g_n3, q_norm_g=g_q, k_norm_g=g_k, sgu_norm_g=g_sn, sgu_w=g_sw,
                 sgu_b=g_sb, conv_w=g_conv)
    return dh0, _split_rows(g_in), small


def _small_gradients(raw, depth):
    return dict(
        norm1_g=raw["norm1_g"].reshape(depth, -1), norm2_g=raw["norm2_g"].reshape(depth, -1),
        norm3_g=raw["norm3_g"].reshape(depth, -1),
        q_norm_g=raw["q_norm_g"].reshape(depth, -1, HEAD_DIM).sum(1),
        k_norm_g=raw["k_norm_g"].reshape(depth, -1, HEAD_DIM).sum(1),
        sgu_norm_g=raw["sgu_norm_g"].reshape(depth, -1), sgu_w=raw["sgu_w"],
        sgu_b=jnp.swapaxes(raw["sgu_b"].reshape(depth, CHUNK, SGU_HEADS, HEAD_DIM).sum(-1), 1, 2),
        conv_w=raw["conv_w"][:, :CONV_TAPS],
    )


def kernel(x, p, norm1_g, w_in, conv_w, q_norm_g, k_norm_g, sgu_norm_g, sgu_w, sgu_b, w_out, norm2_g, w_ff1, w_ff2, norm3_g, w_ple_gate, w_ple_proj, loss_target, m_norm1_g, m_w_in, m_conv_w, m_q_norm_g, m_k_norm_g, m_sgu_norm_g, m_sgu_w, m_sgu_b, m_w_out, m_norm2_g, m_w_ff1, m_w_ff2, m_norm3_g, m_w_ple_gate, m_w_ple_proj, v_norm1_g, v_w_in, v_conv_w, v_q_norm_g, v_k_norm_g, v_sgu_norm_g, v_sgu_w, v_sgu_b, v_w_out, v_norm2_g, v_w_ff1, v_w_ff2, v_norm3_g, v_w_ple_gate, v_w_ple_proj):
    weights = dict(norm1_g=norm1_g, w_in=w_in, conv_w=conv_w, q_norm_g=q_norm_g, k_norm_g=k_norm_g,
                   sgu_norm_g=sgu_norm_g, sgu_w=sgu_w, sgu_b=sgu_b, w_out=w_out, norm2_g=norm2_g, w_ff1=w_ff1,
                   w_ff2=w_ff2, norm3_g=norm3_g, w_ple_gate=w_ple_gate, w_ple_proj=w_ple_proj)
    mom = dict(norm1_g=m_norm1_g, w_in=m_w_in, conv_w=m_conv_w, q_norm_g=m_q_norm_g, k_norm_g=m_k_norm_g,
               sgu_norm_g=m_sgu_norm_g, sgu_w=m_sgu_w, sgu_b=m_sgu_b, w_out=m_w_out, norm2_g=m_norm2_g, w_ff1=m_w_ff1,
               w_ff2=m_w_ff2, norm3_g=m_norm3_g, w_ple_gate=m_w_ple_gate, w_ple_proj=m_w_ple_proj)
    var = dict(norm1_g=v_norm1_g, w_in=v_w_in, conv_w=v_conv_w, q_norm_g=v_q_norm_g, k_norm_g=v_k_norm_g,
               sgu_norm_g=v_sgu_norm_g, sgu_w=v_sgu_w, sgu_b=v_sgu_b, w_out=v_w_out, norm2_g=v_norm2_g, w_ff1=v_w_ff1,
               w_ff2=v_w_ff2, norm3_g=v_norm3_g, w_ple_gate=v_w_ple_gate, w_ple_proj=v_w_ple_proj)
    for params in (weights, mom, var):
        params["w_in"] = jnp.swapaxes(params["w_in"], 1, 2)
    depth = norm1_g.shape[0]
    mx, my, mc = _my_place()
    me = _slot_of(mx, my, mc)

    gathers = []
    modes_of = lambda names: tuple(_GATHER_MODE[k] for k in names)
    token = x[0, :8, :LANES]
    for li in range(depth):
        groups = [([weights[k][li] if k == "conv_w" else weights[k][li].astype(BF16) for k in names], modes_of(names))
                  for names in (_FIRST, _REST)]
        started, token = _exchange_start(groups, token, name=f"gather_start_l{li}", hops=_NEAR)
        gathers.append(started)

    small = []
    for li in range(depth):
        small.append(dict(
            norm1_g=norm1_g[li][None], norm2_g=norm2_g[li][None], norm3_g=norm3_g[li][None],
            gq=jnp.tile(q_norm_g[li], _QK_BLOCK // HEAD_DIM)[None], gk=jnp.tile(k_norm_g[li], _QK_BLOCK // HEAD_DIM)[None],
            sgu_norm_g=sgu_norm_g[li][None], sgu_w=sgu_w[li], b_exp=jnp.repeat(sgu_b[li].T, HEAD_DIM, axis=1),
        ))
    small[0]["norm1_g"] = small[0]["norm1_g"] + token[0, 0]

    h = x[0]
    saved, full = [], []
    relayed_first, relayed_rest = [None] * depth, [None] * depth
    rest_members = [[_REST.index(k) for k in names] for names in _REST_GROUPS]

    def relay_first(li, after):
        if li < depth:
            (relayed_first[li],), _ = _exchange_relay(gathers[li][0], after, modes=modes_of(_FIRST),
                                                      regroup=[list(range(len(_FIRST)))], name=f"gather_first_relay_l{li}")

    def fetch_first(li, after):
        if li == depth:
            return None
        landed = _exchange_wait(relayed_first[li], after, modes=modes_of(_FIRST), hops=_RELAY,
                                name=f"gather_first_wait_l{li}")
        return _whole_matrices(_FIRST, landed)

    hn1 = _rms_fwd(h, small[0]["norm1_g"], name="rms1_l0")
    relay_first(0, hn1)
    w_first = fetch_first(0, hn1)
    for li in range(depth):

        def relay_rest(after, li=li):
            relayed_rest[li], _ = _exchange_relay(gathers[li][1], after, modes=modes_of(_REST), regroup=rest_members,
                                                  name=f"gather_rest_relay_l{li}")

        def fetch(g, after, li=li):
            landed = _exchange_wait(relayed_rest[li][g], after, modes=modes_of(_REST_GROUPS[g]), hops=_RELAY,
                                    name=f"gather_{_REST_GROUPS[g][0]}_wait_l{li}")
            return _whole_matrices(_REST_GROUPS[g], landed)

        gathered = dict(relay_rest=relay_rest, fetch=fetch, relay_next=functools.partial(relay_first, li + 1),
                        fetch_next=functools.partial(fetch_first, li + 1))
        next_norm_g = small[li + 1]["norm1_g"] if li + 1 < depth else None
        h, hn1, w, w_first, acts = _layer_forward(h, hn1, p[li, 0].astype(BF16), small[li], li, w_first, gathered,
                                                  next_norm_g)
        full.append(w)
        saved.append(acts)
    dh, loss_tile = _loss_head(h, loss_target[0], name="loss_head")
    loss = lax.psum(loss_tile[0, 0], ("x", "y", "c"))

    small_names = _SMALL + ("conv_w",)
    scatter_first, scatter_rest = [None] * depth, [None] * depth
    first_modes, rest_modes = (_SCATTER,) + (_GATHER,) * len(small_names), (_SCATTER,) * len(_REST)
    token = loss_tile
    for li in reversed(range(depth)):

        def start_rest(parts, after, li=li):
            (scatter_rest[li],), started = _exchange_start([([parts[k] for k in _REST], rest_modes)], after,
                                                           name=f"scatter_rest_start_l{li}")
            return started

        dh, g_in, small_grads = _layer_backward(dh, saved[li], full[li], small[li], li, token, start_rest)
        (scatter_first[li],), token = _exchange_start(
            [([g_in] + [small_grads[k] for k in small_names], first_modes)], dh, name=f"scatter_first_start_l{li}")
    grad_x = dh[None]

    grads, delta, new_m, new_v = {}, {}, {}, {}
    arrived = {k: [None] * depth for k in _BIG}
    for li in reversed(range(depth)):
        landed = _exchange_wait(scatter_rest[li], token, modes=rest_modes, name=f"scatter_rest_wait_l{li}")
        for k, g in zip(_REST, landed, strict=True):
            arrived[k][li] = g
    for k in _REST:
        grads[k], delta[k], new_m[k], new_v[k] = _adamw_reduce(weights[k], arrived[k], mom[k], var[k], name=f"adamw_{k}")
    small_parts = {k: [None] * depth for k in small_names}
    updated = jnp.stack([delta[k][0, 0, :1] for k in _REST])
    for li in reversed(range(depth)):
        arrived["w_in"][li], *parts = _exchange_wait(scatter_first[li], updated, modes=first_modes,
                                                     name=f"scatter_first_wait_l{li}")
        for k, part in zip(small_names, parts, strict=True):
            small_parts[k][li] = part
    grads["w_in"], delta["w_in"], new_m["w_in"], new_v["w_in"] = _adamw_reduce(
        weights["w_in"], arrived["w_in"], mom["w_in"], var["w_in"], name="adamw_w_in")
    for results in (grads, delta, new_m, new_v):
        results["w_in"] = jnp.swapaxes(results["w_in"], 1, 2)
    sums = _sum_slots([small_parts[k] for k in small_names], name="sum_small_grads")
    grads.update(_small_gradients(dict(zip(small_names, sums)), depth))
    n_conv = conv_w.shape[2]
    grads["conv_w"] = lax.dynamic_slice_in_dim(grads["conv_w"], me * n_conv, n_conv, axis=2)
    for k in small_names:
        as_rows = lambda a: a.reshape(-1, a.shape[-1])
        outs = _adamw(as_rows(weights[k]), as_rows(grads[k]), as_rows(mom[k]), as_rows(var[k]), name=f"adamw_{k}")
        delta[k], new_m[k], new_v[k] = (o.reshape(weights[k].shape) for o in outs)

    return (loss, grad_x, *[grads[k] for k in _ORDER], *[delta[k] for k in _ORDER],
            *[new_m[k] for k in _ORDER], *[new_v[k] for k in _ORDER])
```

```python
import functools
import math

import jax
import jax.numpy as jnp
from jax import lax
from jax.experimental import pallas as pl
from jax.experimental.pallas import tpu as pltpu

F32 = jnp.float32
BF16 = jnp.bfloat16

N_DEV = 8
HEAD_DIM = 64
CONV_W = 256
ATTN_W = 512
SGU_W = 256
SGU_HEADS = 4
CHUNK = 128
CONV_TAPS = 3
EPS = 1e-6
QK_SCALE = HEAD_DIM ** -0.5

ADAM_LR = 0.001
ADAM_B1 = 0.9
ADAM_B2 = 0.999
ADAM_EPS = 1e-08
ADAM_WD = 0.01
ADAM_STEP = 10

LANES = 128
BF16_TILE_ROWS = 16
VMEM_LIMIT_BYTES = 56 * 1024 * 1024
MESH = pl.DeviceIdType.MESH


def _params(*sem):
    return pltpu.CompilerParams(dimension_semantics=sem, vmem_limit_bytes=VMEM_LIMIT_BYTES)


def _row_block(rows, cap):
    if rows <= cap:
        return rows
    return max(b for b in range(BF16_TILE_ROWS, cap + 1, BF16_TILE_ROWS) if rows % b == 0)


def _matmul(a, b, *, name, tb=False, bm=512, bn=512, out_dtypes=(F32,), epilogue=None, extras=(), row_vectors=()):
    m, k = a.shape
    n = b.shape[0] if tb else b.shape[1]
    assert k == (b.shape[1] if tb else b.shape[0])
    bm, bn = min(bm, m), min(bn, n)
    assert m % bm == 0 and n % bn == 0
    a_spec = pl.BlockSpec((bm, k), lambda i, j: (i, 0))
    b_spec = pl.BlockSpec((bn, k), lambda i, j: (j, 0)) if tb else pl.BlockSpec((k, bn), lambda i, j: (0, j))
    dims = (((1,), (1 if tb else 0,)), ((), ()))
    n_in = len(extras) + len(row_vectors)
    for e in extras:
        assert e.shape == (m, n), (e.shape, m, n)
    for e in row_vectors:
        assert e.shape == (1, n), (e.shape, n)

    def body(a_ref, b_ref, *rest):
        outs = rest[n_in:]
        acc = lax.dot_general(a_ref[...], b_ref[...], dims, preferred_element_type=F32)
        res = (acc,) if epilogue is None else epilogue(acc, *[e[...] for e in rest[:n_in]])
        for o_ref, r in zip(outs, res, strict=True):
            o_ref[...] = r.astype(o_ref.dtype)

    tile = pl.BlockSpec((bm, bn), lambda i, j: (i, j))
    vec = pl.BlockSpec((1, bn), lambda i, j: (0, j))
    out = pl.pallas_call(
        body,
        name=name,
        grid=(m // bm, n // bn),
        in_specs=[a_spec, b_spec] + [tile] * len(extras) + [vec] * len(row_vectors),
        out_specs=[tile] * len(out_dtypes),
        out_shape=[jax.ShapeDtypeStruct((m, n), d) for d in out_dtypes],
        compiler_params=_params("parallel", "parallel"),
    )(a, b, *extras, *row_vectors)
    return out[0] if len(out_dtypes) == 1 else out


def _rms_rows(x, g):
    return x * lax.rsqrt(jnp.mean(x * x, axis=-1, keepdims=True) + EPS) * g


_WEIGHT_GRAD_ACC_ELEMS = 1024 * 1024


def _weight_grad(x, dys, *, name, column_shards=False):
    t, m = x.shape
    n = sum(dy.shape[1] for dy in dys)
    bm = m if m <= 2 * LANES else min(m // 2, max(LANES, _WEIGHT_GRAD_ACC_ELEMS // n // LANES * LANES))
    assert m % bm == 0
    ns = n // N_DEV

    def body(x_ref, *rest):
        o_ref = rest[-1]
        xb = x_ref[...]
        acc = jnp.concatenate([lax.dot_general(xb, dy_ref[...], _TN, preferred_element_type=F32) for dy_ref in rest[:-1]],
                              axis=1)
        if column_shards:
            for s in range(N_DEV):
                o_ref[s] = acc[:, s * ns:(s + 1) * ns].astype(o_ref.dtype)
        else:
            o_ref[...] = acc.astype(o_ref.dtype)

    if column_shards:
        out_spec, out_dims = pl.BlockSpec((N_DEV, bm, ns), lambda i: (0, i, 0)), (N_DEV, m, ns)
    else:
        out_spec, out_dims = pl.BlockSpec((bm, n), lambda i: (i, 0)), (m, n)
    return pl.pallas_call(
        body,
        name=name,
        grid=(m // bm,),
        in_specs=[pl.BlockSpec((t, bm), lambda i: (0, i))] + [pl.BlockSpec(dy.shape, lambda i: (0, 0)) for dy in dys],
        out_specs=out_spec,
        out_shape=jax.ShapeDtypeStruct(out_dims, BF16),
        compiler_params=_params("parallel"),
    )(x, *dys)


def _rms_fwd(h, g, *, name, br=512):
    t, d = h.shape
    br = min(br, t)

    def body(h_ref, g_ref, o_ref):
        o_ref[...] = _rms_rows(h_ref[...], g_ref[...]).astype(o_ref.dtype)

    return pl.pallas_call(
        body,
        name=name,
        grid=(t // br,),
        in_specs=[pl.BlockSpec((br, d), lambda i: (i, 0)), pl.BlockSpec((1, d), lambda i: (0, 0))],
        out_specs=pl.BlockSpec((br, d), lambda i: (i, 0)),
        out_shape=jax.ShapeDtypeStruct((t, d), BF16),
        compiler_params=_params("parallel"),
    )(h, g)


def _matmul_rms_bwd(dzs, w, h, g, dres, *, name, w_is_k_by_d=False):
    t, d = h.shape
    widths = [dz.shape[1] for dz in dzs]
    k = sum(widths)
    assert w.shape == ((k, d) if w_is_k_by_d else (d, k))
    br = min(t, _NORM_FUSED_ROWS)
    n_dz = len(dzs)

    def body(*refs):
        w_ref, h_ref, g_ref, dres_ref, dh_ref, dh16_ref, dg_ref = refs[n_dz:]
        x = h_ref[...]
        dyv, at = None, 0
        for dz_ref, width in zip(refs[:n_dz], widths):
            if w_is_k_by_d:
                part = jnp.dot(dz_ref[...], w_ref[at:at + width, :], preferred_element_type=F32)
            else:
                part = lax.dot_general(dz_ref[...], w_ref[:, at:at + width], _NT, preferred_element_type=F32)
            dyv = part if dyv is None else dyv + part
            at += width
        r = lax.rsqrt(jnp.mean(x * x, axis=-1, keepdims=True) + EPS)
        xhat = x * r
        dxhat = dyv * g_ref[...]
        dh = dres_ref[...] + r * (dxhat - xhat * jnp.mean(dxhat * xhat, axis=-1, keepdims=True))
        dh_ref[...] = dh
        dh16_ref[...] = dh.astype(dh16_ref.dtype)

        @pl.when(pl.program_id(0) == 0)
        def _():
            dg_ref[...] = jnp.zeros_like(dg_ref)

        dg_ref[...] += jnp.sum(dyv * xhat, axis=0, keepdims=True)

    row = pl.BlockSpec((br, d), lambda i: (i, 0))
    vec = pl.BlockSpec((1, d), lambda i: (0, 0))
    return pl.pallas_call(
        body,
        name=name,
        grid=(t // br,),
        in_specs=[pl.BlockSpec((br, width), lambda i: (i, 0)) for width in widths]
        + [pl.BlockSpec(w.shape, lambda i: (0, 0)), row, vec, row],
        out_specs=[row, row, vec],
        out_shape=[jax.ShapeDtypeStruct((t, d), F32), jax.ShapeDtypeStruct((t, d), BF16),
                   jax.ShapeDtypeStruct((1, d), F32)],
        compiler_params=_params("arbitrary"),
    )(*dzs, w, h, g, dres)


def _group_mean(x, width):
    grp = lax.broadcasted_iota(jnp.int32, x.shape, 1) // HEAD_DIM
    out = jnp.zeros_like(x)
    for gi in range(width // HEAD_DIM):
        m = grp == gi
        s = jnp.sum(jnp.where(m, x, 0.0), axis=1, keepdims=True)
        out = jnp.where(m, s, out)
    return out * (1.0 / HEAD_DIM)


def _gelu(x):
    return 0.5 * x * (1.0 + lax.erf(x * (2.0 ** -0.5)))


def _gelu_grad(x):
    cdf = 0.5 * (1.0 + lax.erf(x * (2.0 ** -0.5)))
    pdf = jnp.exp(-0.5 * x * x) * (1.0 / math.sqrt(2.0 * math.pi))
    return cdf + x * pdf


def _shift_down(z, s, row):
    return jnp.where(row >= s, pltpu.roll(z, s, 0), 0.0)


def _shift_up(z, s, row, t):
    return jnp.where(row < t - s, pltpu.roll(z, t - s, 0), 0.0)


def _conv_fwd(proj, conv_w, *, name):
    t = proj.shape[0]
    nb = CONV_W // LANES

    def body(b_ref, c_ref, h_ref, w_ref, o_ref):
        row = lax.broadcasted_iota(jnp.int32, (t, LANES), 0)
        z = c_ref[...] * h_ref[...]
        w = w_ref[...]
        conv = w[2:3, :] * z + w[1:2, :] * _shift_down(z, 1, row) + w[0:1, :] * _shift_down(z, 2, row)
        o_ref[...] = (b_ref[...] * conv).astype(o_ref.dtype)

    return pl.pallas_call(
        body,
        name=name,
        grid=(nb,),
        in_specs=[
            pl.BlockSpec((t, LANES), lambda j: (0, j)),
            pl.BlockSpec((t, LANES), lambda j: (0, nb + j)),
            pl.BlockSpec((t, LANES), lambda j: (0, 2 * nb + j)),
            pl.BlockSpec((CONV_TAPS, LANES), lambda j: (0, j)),
        ],
        out_specs=pl.BlockSpec((t, LANES), lambda j: (0, j)),
        out_shape=jax.ShapeDtypeStruct((t, CONV_W), BF16),
        compiler_params=_params("parallel"),
    )(proj, proj, proj, conv_w)


def _conv_bwd(dmix, proj, conv_w, *, name):
    t = proj.shape[0]

    def body(dy_ref, b_ref, c_ref, h_ref, w_ref, o_ref, dw_ref):
        row = lax.broadcasted_iota(jnp.int32, (t, CONV_W), 0)
        ac, ah = c_ref[...], h_ref[...]
        z = ac * ah
        w = w_ref[...]
        z1 = _shift_down(z, 1, row)
        z2 = _shift_down(z, 2, row)
        conv = w[2:3, :] * z + w[1:2, :] * z1 + w[0:1, :] * z2
        dy = dy_ref[...]
        o_ref[:, 0:CONV_W] = (dy * conv).astype(o_ref.dtype)
        dconv = dy * b_ref[...]
        dz = w[2:3, :] * dconv + w[1:2, :] * _shift_up(dconv, 1, row, t) + w[0:1, :] * _shift_up(dconv, 2, row, t)
        o_ref[:, CONV_W:2 * CONV_W] = (dz * ah).astype(o_ref.dtype)
        o_ref[:, 2 * CONV_W:3 * CONV_W] = (dz * ac).astype(o_ref.dtype)
        dw_ref[...] = jnp.zeros_like(dw_ref)
        dw_ref[0:1, :] = jnp.sum(dconv * z2, axis=0, keepdims=True)
        dw_ref[1:2, :] = jnp.sum(dconv * z1, axis=0, keepdims=True)
        dw_ref[2:3, :] = jnp.sum(dconv * z, axis=0, keepdims=True)

    col = lambda j: pl.BlockSpec((t, CONV_W), lambda i: (0, j))
    return pl.pallas_call(
        body,
        name=name,
        grid=(1,),
        in_specs=[col(0), col(0), col(1), col(2), pl.BlockSpec((CONV_TAPS, CONV_W), lambda i: (0, 0))],
        out_specs=[pl.BlockSpec((t, 3 * CONV_W), lambda i: (0, 0)), pl.BlockSpec((8, CONV_W), lambda i: (0, 0))],
        out_shape=[jax.ShapeDtypeStruct((t, 3 * CONV_W), BF16), jax.ShapeDtypeStruct((8, CONV_W), F32)],
        compiler_params=_params("arbitrary"),
    )(dmix, proj, proj, proj, conv_w)


_QK_BLOCK = 256


def _qk_prep(proj, gq, gk, *, name, br=512):
    t = proj.shape[0]
    br = min(br, t)
    nb = ATTN_W // _QK_BLOCK
    q0 = (3 * CONV_W) // _QK_BLOCK

    def body(q_ref, k_ref, v_ref, gq_ref, gk_ref, qo_ref, ko_ref, vo_ref):
        q = q_ref[...]
        k = k_ref[...]
        rq = lax.rsqrt(_group_mean(q * q, _QK_BLOCK) + EPS)
        rk = lax.rsqrt(_group_mean(k * k, _QK_BLOCK) + EPS)
        qo_ref[...] = ((q * rq * gq_ref[...]).astype(BF16) * QK_SCALE).astype(qo_ref.dtype)
        ko_ref[...] = (k * rk * gk_ref[...]).astype(ko_ref.dtype)
        vo_ref[...] = v_ref[...].astype(vo_ref.dtype)

    col = lambda off: pl.BlockSpec((br, _QK_BLOCK), lambda i, j: (i, off + j))
    vec = pl.BlockSpec((1, _QK_BLOCK), lambda i, j: (0, 0))
    return pl.pallas_call(
        body,
        name=name,
        grid=(t // br, nb),
        in_specs=[col(q0), col(q0 + nb), col(q0 + 2 * nb), vec, vec],
        out_specs=[col(0)] * 3,
        out_shape=[jax.ShapeDtypeStruct((t, ATTN_W), BF16)] * 3,
        compiler_params=_params("parallel", "parallel"),
    )(proj, proj, proj, gq, gk)


def _qk_prep_bwd(dqs, dkn, dv, proj, gq, gk, *, name, br=256):
    t = proj.shape[0]
    br = min(br, t)
    nb = ATTN_W // _QK_BLOCK
    q0 = (3 * CONV_W) // _QK_BLOCK

    def norm_bwd(dy, x, g):
        r = lax.rsqrt(_group_mean(x * x, ATTN_W) + EPS)
        xhat = x * r
        dxhat = dy * g
        dx = r * (dxhat - xhat * _group_mean(dxhat * xhat, ATTN_W))
        return dx, jnp.sum(dy * xhat, axis=0, keepdims=True)

    def body(dq_ref, dk_ref, dv_ref, *rest):
        x_refs, (gq_ref, gk_ref, o_ref, dgq_ref, dgk_ref) = rest[:2 * nb], rest[2 * nb:]
        whole = lambda refs: jnp.concatenate([r[...] for r in refs], axis=1)
        dq, dgq = norm_bwd(dq_ref[...] * QK_SCALE, whole(x_refs[:nb]), whole([gq_ref] * nb))
        dk, dgk = norm_bwd(dk_ref[...], whole(x_refs[nb:]), whole([gk_ref] * nb))
        o_ref[:, 0:ATTN_W] = dq.astype(o_ref.dtype)
        o_ref[:, ATTN_W:2 * ATTN_W] = dk.astype(o_ref.dtype)
        o_ref[:, 2 * ATTN_W:3 * ATTN_W] = dv_ref[...].astype(o_ref.dtype)

        @pl.when(pl.program_id(0) == 0)
        def _():
            dgq_ref[...] = jnp.zeros_like(dgq_ref)
            dgk_ref[...] = jnp.zeros_like(dgk_ref)

        dgq_ref[...] += dgq
        dgk_ref[...] += dgk

    rows = pl.BlockSpec((br, ATTN_W), lambda i: (i, 0))
    col = lambda j: pl.BlockSpec((br, _QK_BLOCK), lambda i: (i, j))
    gain = pl.BlockSpec((1, _QK_BLOCK), lambda i: (0, 0))
    total = pl.BlockSpec((1, ATTN_W), lambda i: (0, 0))
    return pl.pallas_call(
        body,
        name=name,
        grid=(t // br,),
        in_specs=[rows, rows, rows] + [col(q0 + j) for j in range(2 * nb)] + [gain, gain],
        out_specs=[pl.BlockSpec((br, 3 * ATTN_W), lambda i: (i, 0)), total, total],
        out_shape=[jax.ShapeDtypeStruct((t, 3 * ATTN_W), BF16)] + [jax.ShapeDtypeStruct((1, ATTN_W), F32)] * 2,
        compiler_params=_params("arbitrary"),
    )(dqs, dkn, dv, *[proj] * (2 * nb), gq, gk)


def _key_order_matrix(tb, relation):
    jj = lax.broadcasted_iota(jnp.int32, (tb, tb), 0)
    ss = lax.broadcasted_iota(jnp.int32, (tb, tb), 1)
    return relation(jj, ss).astype(BF16)


def _log_sigmoids(z):
    lb = jnp.minimum(z, 0.0) - jnp.log(1.0 + jnp.exp(-jnp.abs(z)))
    return lb, lb - z


def _below_diagonal(tb):
    return lax.broadcasted_iota(jnp.int32, (tb, tb), 1) < lax.broadcasted_iota(jnp.int32, (tb, tb), 0)


_NT = (((1,), (1,)), ((), ()))
_TN = (((0,), (0,)), ((), ()))
_ATTN_BLOCK = 256
_ATTN_FWD_UNROLL = 4
_ATTN_BWD_UNROLL = 3


def _attn_fwd(qs, kn, v, *, name, tb=_ATTN_BLOCK, unroll=_ATTN_FWD_UNROLL):
    t = qs.shape[0]
    tb = min(tb, t)
    assert t % tb == 0
    n_pairs = ATTN_W // LANES

    def body(q_ref, k_ref, v_ref, o_ref, lt_ref, acc_ref, carry_ref):
        qb = pl.program_id(1)
        half = lax.broadcasted_iota(jnp.int32, (1, LANES), 1) // HEAD_DIM
        later = _key_order_matrix(tb, lambda j, s: j > s)
        acc_ref[...] = jnp.zeros_like(acc_ref)
        carry_ref[...] = jnp.zeros_like(carry_ref)
        q = q_ref[...]
        qh = [jnp.where(half == h, q, jnp.zeros_like(q)) for h in range(2)]

        def tiles(kbs, first_is_diagonal):
            blk = []
            for kb in kbs:
                start = pl.multiple_of(kb * tb, tb)
                blk.append((k_ref[pl.ds(start, tb), :], v_ref[pl.ds(start, tb), :]))
            chains = [(h, j) for j in range(len(kbs)) for h in range(2)]
            masked = [first_is_diagonal and j == 0 for _, j in chains]
            z = [lax.dot_general(qh[h], blk[j][0], _NT, preferred_element_type=F32) for h, j in chains]
            causal = _below_diagonal(tb) if first_is_diagonal else None
            lb, lr = [], []
            for zi, mask in zip(z, masked):
                b, r = _log_sigmoids(zi)
                lb.append(b)
                lr.append(jnp.where(causal, r, 0.0) if mask else r)
            suffix = [jnp.dot(r.astype(BF16), later, preferred_element_type=F32) for r in lr]
            carry = [carry_ref[0], carry_ref[1]]
            w = []
            for i, (h, j) in enumerate(chains):
                wi = jnp.exp(lb[i] + (suffix[i] + carry[h][:, 0:1]))
                w.append((jnp.where(causal, wi, 0.0) if masked[i] else wi).astype(BF16))
                carry[h] = carry[h] + jnp.sum(lr[i], axis=1, keepdims=True)
            for i, (h, j) in enumerate(chains):
                vh = jnp.where(half == h, blk[j][1], jnp.zeros_like(blk[j][1]))
                acc_ref[h] += jnp.dot(w[i], vh, preferred_element_type=F32)
            carry_ref[0] = carry[0]
            carry_ref[1] = carry[1]

        @pl.when(qb == 0)
        def _():
            tiles([qb], True)

        @pl.when(qb > 0)
        def _():
            tiles([qb, qb - 1], True)
            rest = qb - 1

            def step(i, _):
                kb = rest - 1 - unroll * i
                tiles([kb - u for u in range(unroll)], False)
                return 0

            lax.fori_loop(0, rest // unroll, step, 0)
            for left in range(1, unroll):

                @pl.when(rest % unroll == left)
                def _(left=left):
                    tiles([left - 1 - u for u in range(left)], False)

        o_ref[...] = (acc_ref[0] + acc_ref[1]).astype(o_ref.dtype)
        lt_ref[...] = jnp.where(half == 0, carry_ref[0], carry_ref[1])

    return pl.pallas_call(
        body,
        name=name,
        grid=(n_pairs, t // tb),
        in_specs=[
            pl.BlockSpec((tb, LANES), lambda p, i: (i, p)),
            pl.BlockSpec((t, LANES), lambda p, i: (0, p)),
            pl.BlockSpec((t, LANES), lambda p, i: (0, p)),
        ],
        out_specs=[pl.BlockSpec((tb, LANES), lambda p, i: (i, p))] * 2,
        out_shape=[jax.ShapeDtypeStruct((t, ATTN_W), BF16), jax.ShapeDtypeStruct((t, ATTN_W), F32)],
        scratch_shapes=[pltpu.VMEM((2, tb, LANES), F32), pltpu.VMEM((2, tb, LANES), F32)],
        compiler_params=_params("parallel", "parallel"),
    )(qs, kn, v)


def _attn_bwd(dmix, qs, kn, v, lt, order_after, *, name, tb=_ATTN_BLOCK, unroll=_ATTN_BWD_UNROLL):
    t = qs.shape[0]
    tb = min(tb, t)
    assert t % tb == 0
    n_pairs = ATTN_W // LANES
    dy0 = CONV_W // LANES

    def body(do_ref, q_ref, k_ref, v_ref, lt_ref, order_ref, dq_ref, dk_ref, dv_ref, dqacc_ref, cc_ref, cg_ref):
        qb = pl.program_id(1)
        half = lax.broadcasted_iota(jnp.int32, (1, LANES), 1) // HEAD_DIM
        lane = lax.broadcasted_iota(jnp.int32, (tb, LANES), 1)
        later = _key_order_matrix(tb, lambda j, s: j > s)
        before = _key_order_matrix(tb, lambda j, s: j < s)
        q = q_ref[...]
        do = do_ref[...].astype(BF16)
        lt = lt_ref[...]
        qh = [jnp.where(half == h, q, jnp.zeros_like(q)) for h in range(2)]
        doh = [jnp.where(half == h, do, jnp.zeros_like(do)) for h in range(2)]
        lth = [jnp.sum(jnp.where(lane == h * HEAD_DIM, lt, 0.0), axis=1, keepdims=True) for h in range(2)]

        @pl.when(qb == 0)
        def _():
            dk_ref[...] = jnp.zeros_like(dk_ref)
            dv_ref[...] = jnp.zeros_like(dv_ref)

        dqacc_ref[...] = jnp.zeros_like(dqacc_ref)
        cc_ref[...] = jnp.zeros_like(cc_ref)
        cg_ref[...] = jnp.zeros_like(cg_ref)

        def tiles(kbs, last_is_diagonal):
            starts = [pl.multiple_of(kb * tb, tb) for kb in kbs]
            blk = [(k_ref[pl.ds(s, tb), :], v_ref[pl.ds(s, tb), :]) for s in starts]
            chains = [(h, j) for j in range(len(kbs)) for h in range(2)]
            masked = [last_is_diagonal and j == len(kbs) - 1 for _, j in chains]
            z = [lax.dot_general(qh[h], blk[j][0], _NT, preferred_element_type=F32) for h, j in chains]
            da = [lax.dot_general(doh[h], jnp.where(half == h, blk[j][1], jnp.zeros_like(blk[j][1])), _NT,
                                  preferred_element_type=F32) for h, j in chains]
            causal = _below_diagonal(tb) if last_is_diagonal else None
            lb, lr = [], []
            for zi, mask in zip(z, masked):
                b, r = _log_sigmoids(zi)
                lb.append(b)
                lr.append(jnp.where(causal, r, 0.0) if mask else r)
            suffix = [jnp.dot(r.astype(BF16), later, preferred_element_type=F32) for r in lr]
            cc = [cc_ref[0], cc_ref[1]]
            cg = [cg_ref[0], cg_ref[1]]
            a16, g = [], []
            for i, (h, j) in enumerate(chains):
                cc[h] = cc[h] + jnp.sum(lr[i], axis=1, keepdims=True)
                a = jnp.exp(lb[i] + suffix[i] + (lth[h] - cc[h][:, 0:1]))
                if masked[i]:
                    a = jnp.where(causal, a, 0.0)
                a16.append(a.astype(BF16))
                g.append(da[i] * a)
            g_before = [jnp.dot(gi.astype(BF16), before, preferred_element_type=F32) for gi in g]
            dz = []
            for i, (h, j) in enumerate(chains):
                dzi = g[i] - jnp.exp(lb[i]) * (g[i] + (g_before[i] + cg[h][:, 0:1]))
                dz.append((jnp.where(causal, dzi, 0.0) if masked[i] else dzi).astype(BF16))
                cg[h] = cg[h] + jnp.sum(g[i], axis=1, keepdims=True)
            for i, (h, j) in enumerate(chains):
                kh = jnp.where(half == h, blk[j][0], jnp.zeros_like(blk[j][0]))
                dqacc_ref[h] += jnp.dot(dz[i], kh, preferred_element_type=F32)
                dk_ref[pl.ds(starts[j], tb), :] += lax.dot_general(dz[i], qh[h], _TN, preferred_element_type=F32)
                dv_ref[pl.ds(starts[j], tb), :] += lax.dot_general(a16[i], doh[h], _TN, preferred_element_type=F32)
            for h in range(2):
                cc_ref[h] = cc[h]
                cg_ref[h] = cg[h]

        def step(i, _):
            kb = unroll * i
            tiles([kb + u for u in range(unroll)], False)
            return 0

        lax.fori_loop(0, qb // unroll, step, 0)
        for left in range(1, unroll):

            @pl.when(qb % unroll == left)
            def _(left=left):
                tiles([qb - left + u for u in range(left)], False)

        tiles([qb], True)
        dq_ref[...] = dqacc_ref[0] + dqacc_ref[1]

    qblk = pl.BlockSpec((tb, LANES), lambda p, i: (i, p))
    whole = pl.BlockSpec((t, LANES), lambda p, i: (0, p))
    return pl.pallas_call(
        body,
        name=name,
        grid=(n_pairs, t // tb),
        in_specs=[pl.BlockSpec((tb, LANES), lambda p, i: (i, dy0 + p)), qblk, whole, whole, qblk,
                  pl.BlockSpec(order_after.shape, lambda p, i: (0, 0))],
        out_specs=[qblk, whole, whole],
        out_shape=[jax.ShapeDtypeStruct((t, ATTN_W), F32)] * 3,
        scratch_shapes=[pltpu.VMEM((2, tb, LANES), F32)] * 3,
        compiler_params=_params("parallel", "arbitrary"),
    )(dmix, qs, kn, v, lt, order_after)


_SGU_CHUNKS_PER_STEP = 4


def _sgu_rows(t):
    return CHUNK * math.gcd(_SGU_CHUNKS_PER_STEP, t // CHUNK)


def _sgu_weights(w_ref):
    tt = lax.broadcasted_iota(jnp.int32, (CHUNK, CHUNK), 0)
    ss = lax.broadcasted_iota(jnp.int32, (CHUNK, CHUNK), 1)
    tril = ss <= tt
    return [jnp.where(tril, w_ref[gi], 0.0).astype(BF16) for gi in range(SGU_HEADS)], tril


def _sgu_fwd(proj, g_v, w_s, b_exp, *, name):
    t = proj.shape[0]
    u0 = (3 * CONV_W + 3 * ATTN_W) // SGU_W
    rows = _sgu_rows(t)

    def body(u_ref, v_ref, g_ref, w_ref, b_ref, o_ref):
        grp = lax.broadcasted_iota(jnp.int32, (1, SGU_W), 1) // HEAD_DIM
        wm, _ = _sgu_weights(w_ref)
        gain, bias = g_ref[...], b_ref[...]
        for c in range(rows // CHUNK):
            chunk = pl.ds(c * CHUNK, CHUNK)
            u = _gelu(u_ref[chunk, :])
            vv = _gelu(v_ref[chunk, :])
            vn = (vv * lax.rsqrt(_group_mean(vv * vv, SGU_W) + EPS) * gain).astype(BF16)
            sv = bias
            for gi in range(SGU_HEADS):
                sv = sv + jnp.dot(wm[gi], jnp.where(grp == gi, vn, jnp.zeros_like(vn)), preferred_element_type=F32)
            o_ref[chunk, :] = (u * sv).astype(o_ref.dtype)

    return pl.pallas_call(
        body,
        name=name,
        grid=(t // rows,),
        in_specs=[
            pl.BlockSpec((rows, SGU_W), lambda i: (i, u0)),
            pl.BlockSpec((rows, SGU_W), lambda i: (i, u0 + 1)),
            pl.BlockSpec((1, SGU_W), lambda i: (0, 0)),
            pl.BlockSpec((SGU_HEADS, CHUNK, CHUNK), lambda i: (0, 0, 0)),
            pl.BlockSpec((CHUNK, SGU_W), lambda i: (0, 0)),
        ],
        out_specs=pl.BlockSpec((rows, SGU_W), lambda i: (i, 0)),
        out_shape=jax.ShapeDtypeStruct((t, SGU_W), BF16),
        compiler_params=_params("parallel"),
    )(proj, proj, g_v, w_s, b_exp)


def _sgu_bwd(dmix, proj, g_v, w_s, b_exp, *, name):
    t = proj.shape[0]
    u0 = (3 * CONV_W + 3 * ATTN_W) // SGU_W
    dy0 = (CONV_W + ATTN_W) // SGU_W
    rows = _sgu_rows(t)

    def body(dy_ref, u_ref, v_ref, g_ref, w_ref, b_ref, o_ref, dg_ref, dw_ref, db_ref):
        grp = lax.broadcasted_iota(jnp.int32, (1, SGU_W), 1) // HEAD_DIM
        gain, bias = g_ref[...], b_ref[...]
        wm, tril = _sgu_weights(w_ref)

        @pl.when(pl.program_id(0) == 0)
        def _():
            dg_ref[...] = jnp.zeros_like(dg_ref)
            dw_ref[...] = jnp.zeros_like(dw_ref)
            db_ref[...] = jnp.zeros_like(db_ref)

        dg = jnp.zeros_like(gain)
        db = jnp.zeros_like(bias)
        dw = [jnp.zeros((CHUNK, CHUNK), F32) for _ in range(SGU_HEADS)]
        for c in range(rows // CHUNK):
            chunk = pl.ds(c * CHUNK, CHUNK)
            cu, cv = u_ref[chunk, :], v_ref[chunk, :]
            u = _gelu(cu)
            vv = _gelu(cv)
            r = lax.rsqrt(_group_mean(vv * vv, SGU_W) + EPS)
            xhat = vv * r
            vn = (xhat * gain).astype(BF16)
            vng = [jnp.where(grp == gi, vn, jnp.zeros_like(vn)) for gi in range(SGU_HEADS)]
            sv = bias
            for gi in range(SGU_HEADS):
                sv = sv + jnp.dot(wm[gi], vng[gi], preferred_element_type=F32)
            dy = dy_ref[chunk, :]
            o_ref[chunk, 0:SGU_W] = (dy * sv * _gelu_grad(cu)).astype(o_ref.dtype)
            dsv = dy * u
            dsv16 = dsv.astype(BF16)
            db = db + dsv
            dvn = jnp.zeros_like(dsv)
            for gi in range(SGU_HEADS):
                dw[gi] = dw[gi] + lax.dot_general(dsv16, vng[gi], _NT, preferred_element_type=F32)
                dvn_g = lax.dot_general(wm[gi], dsv16, _TN, preferred_element_type=F32)
                dvn = jnp.where(grp == gi, dvn_g, dvn)
            dg = dg + jnp.sum(dvn * xhat, axis=0, keepdims=True)
            dxhat = dvn * gain
            dvv = r * (dxhat - xhat * _group_mean(dxhat * xhat, SGU_W))
            o_ref[chunk, SGU_W:2 * SGU_W] = (dvv * _gelu_grad(cv)).astype(o_ref.dtype)
        dg_ref[...] += dg
        db_ref[...] += db
        for gi in range(SGU_HEADS):
            dw_ref[gi] += jnp.where(tril, dw[gi], 0.0)

    return pl.pallas_call(
        body,
        name=name,
        grid=(t // rows,),
        in_specs=[
            pl.BlockSpec((rows, SGU_W), lambda i: (i, dy0)),
            pl.BlockSpec((rows, SGU_W), lambda i: (i, u0)),
            pl.BlockSpec((rows, SGU_W), lambda i: (i, u0 + 1)),
            pl.BlockSpec((1, SGU_W), lambda i: (0, 0)),
            pl.BlockSpec((SGU_HEADS, CHUNK, CHUNK), lambda i: (0, 0, 0)),
            pl.BlockSpec((CHUNK, SGU_W), lambda i: (0, 0)),
        ],
        out_specs=[
            pl.BlockSpec((rows, 2 * SGU_W), lambda i: (i, 0)),
            pl.BlockSpec((1, SGU_W), lambda i: (0, 0)),
            pl.BlockSpec((SGU_HEADS, CHUNK, CHUNK), lambda i: (0, 0, 0)),
            pl.BlockSpec((CHUNK, SGU_W), lambda i: (0, 0)),
        ],
        out_shape=[
            jax.ShapeDtypeStruct((t, 2 * SGU_W), BF16),
            jax.ShapeDtypeStruct((1, SGU_W), F32),
            jax.ShapeDtypeStruct((SGU_HEADS, CHUNK, CHUNK), F32),
            jax.ShapeDtypeStruct((CHUNK, SGU_W), F32),
        ],
        compiler_params=_params("arbitrary"),
    )(dmix, proj, proj, g_v, w_s, b_exp)


def _ple_bwd(dh, gate, pp, order_after, *, name, br=512):
    t, d = dh.shape
    br = min(br, t)

    def body(dh_ref, g_ref, p_ref, order_ref, dpre_ref, dpp_ref):
        dhv, g = dh_ref[...], g_ref[...]
        dpre_ref[...] = (dhv * p_ref[...] * g * (1.0 - g)).astype(dpre_ref.dtype)
        dpp_ref[...] = (dhv * g).astype(dpp_ref.dtype)

    row = pl.BlockSpec((br, d), lambda i: (i, 0))
    return pl.pallas_call(
        body,
        name=name,
        grid=(t // br,),
        in_specs=[row] * 3 + [pl.BlockSpec(order_after.shape, lambda i: (0, 0))],
        out_specs=[row] * 2,
        out_shape=[jax.ShapeDtypeStruct((t, d), BF16)] * 2,
        compiler_params=_params("parallel"),
    )(dh, gate, pp, order_after)


def _loss_head(y, target, *, name, br=512):
    t, d = y.shape
    br = min(br, t)

    def body(y_ref, t_ref, dy_ref, loss_ref):
        err = y_ref[...] - t_ref[...]
        dy_ref[...] = err * (1.0 / d)

        @pl.when(pl.program_id(0) == 0)
        def _():
            loss_ref[...] = jnp.zeros_like(loss_ref)

        loss_ref[...] += 0.5 * jnp.sum(jnp.sum(err * err, axis=1, keepdims=True) * (1.0 / d), axis=0, keepdims=True)

    row = pl.BlockSpec((br, d), lambda i: (i, 0))
    return pl.pallas_call(
        body,
        name=name,
        grid=(t // br,),
        in_specs=[row, row],
        out_specs=[row, pl.BlockSpec((8, LANES), lambda i: (0, 0))],
        out_shape=[jax.ShapeDtypeStruct((t, d), F32), jax.ShapeDtypeStruct((8, LANES), F32)],
        compiler_params=_params("arbitrary"),
    )(y, target)


def _adamw_update(w, g, m, v):
    nm = ADAM_B1 * m + (1.0 - ADAM_B1) * g
    nv = ADAM_B2 * v + (1.0 - ADAM_B2) * (g * g)
    m_hat = nm / (1.0 - ADAM_B1 ** ADAM_STEP)
    v_hat = nv / (1.0 - ADAM_B2 ** ADAM_STEP)
    return -ADAM_LR * (m_hat / (jnp.sqrt(v_hat) + ADAM_EPS) + ADAM_WD * w), nm, nv


def _adamw(w, g, m, v, *, name, br=512):
    r, c = w.shape
    br = _row_block(r, br)

    def body(w_ref, g_ref, m_ref, v_ref, d_ref, nm_ref, nv_ref):
        d_ref[...], nm_ref[...], nv_ref[...] = _adamw_update(w_ref[...], g_ref[...], m_ref[...], v_ref[...])

    row = pl.BlockSpec((br, c), lambda i: (i, 0))
    return pl.pallas_call(
        body,
        name=name,
        grid=(r // br,),
        in_specs=[row] * 4,
        out_specs=[row] * 3,
        out_shape=[jax.ShapeDtypeStruct((r, c), F32)] * 3,
        compiler_params=_params("parallel"),
    )(w, g, m, v)


def _sum_slots(per_layer, *, name):
    counts = [len(arrays) for arrays in per_layer]
    flat = [a for arrays in per_layer for a in arrays]

    def body(*refs):
        ins, outs = refs[:len(flat)], refs[len(flat):]
        at = 0
        for o_ref, count in zip(outs, counts, strict=True):
            for li in range(count):
                acc = ins[at + li][0]
                for j in range(1, N_DEV):
                    acc = acc + ins[at + li][j]
                o_ref[li] = acc
            at += count

    return pl.pallas_call(
        body,
        name=name,
        out_shape=[jax.ShapeDtypeStruct((len(arrays), *arrays[0].shape[1:]), F32) for arrays in per_layer],
        compiler_params=pltpu.CompilerParams(vmem_limit_bytes=VMEM_LIMIT_BYTES),
    )(*flat)


_ADAMW_BLOCK_ELEMS = 192 * 1024


def _adamw_reduce(w, arrived, m, v, *, name):
    depth, r, c = w.shape
    br = _row_block(r, max(BF16_TILE_ROWS, _ADAMW_BLOCK_ELEMS // (-(-c // LANES) * LANES)))

    def body(w_ref, m_ref, v_ref, *rest):
        parts, (g_ref, d_ref, nm_ref, nv_ref) = rest[:depth], rest[depth:]
        for li in range(depth):

            @pl.when(pl.program_id(0) == li)
            def _(li=li):
                g = parts[li][0].astype(F32)
                for j in range(1, N_DEV):
                    g = g + parts[li][j].astype(F32)
                g_ref[...] = g
                d_ref[...], nm_ref[...], nv_ref[...] = _adamw_update(w_ref[...], g, m_ref[...], v_ref[...])

    cur = pl.BlockSpec((None, br, c), lambda l, i: (l, i, 0))
    slots = [pl.BlockSpec((N_DEV, br, c), lambda l, i, li=li: (0, jnp.where(l == li, i, 0), 0)) for li in range(depth)]
    return pl.pallas_call(
        body,
        name=name,
        grid=(depth, r // br),
        in_specs=[cur, cur, cur] + slots,
        out_specs=[cur] * 4,
        out_shape=[jax.ShapeDtypeStruct((depth, r, c), F32)] * 4,
        compiler_params=_params("arbitrary", "arbitrary"),
    )(w, m, v, *arrived)


def _my_place():
    return lax.axis_index("x"), lax.axis_index("y"), lax.axis_index("c")


def _flip(v, bit):
    return 1 - v if bit else v


def _slot_of(px, py, pc):
    return 4 * px + 2 * py + pc


_ANY = pl.BlockSpec(memory_space=pl.ANY)


_HBM = pl.BlockSpec(memory_space=pltpu.HBM)
_SEM = pl.BlockSpec(memory_space=pltpu.SEMAPHORE)
_DATAFLOW = pltpu.SideEffectType.DATAFLOW_SIDE_EFFECTING


_GATHER, _GATHER_COLUMNS, _SCATTER = "gather", "gather_columns", "scatter"


def _landing_shape(a, mode):
    if mode == _SCATTER:
        return a.shape
    if mode == _GATHER_COLUMNS:
        return (a.shape[0], N_DEV * a.shape[1])
    return (N_DEV, *a.shape)


_DIRECT, _NEAR, _RELAY = "direct", "near", "relay"
_OTHER_CHIPS = (2, 4, 6)
_SIBLING = 1


def _exchange_copies(src_refs, land_refs, send_sem, recv_sem, modes, hops=_DIRECT):
    mx, my, mc = _my_place()
    peer_of = lambda k: (_flip(mx, k & 4), _flip(my, k & 2), _flip(mc, k & 1))
    mine = _slot_of(mx, my, mc)

    def block(land, mode, slot):
        if mode == _GATHER_COLUMNS:
            n = land.shape[1] // N_DEV
            return land.at[:, pl.ds(pl.multiple_of(slot * n, LANES), n)]
        return land.at[slot]

    def remote_copy(src, dst, to):
        return pltpu.make_async_remote_copy(src_ref=src, dst_ref=dst, send_sem=send_sem, recv_sem=recv_sem,
                                            device_id=to, device_id_type=MESH)

    remote, local = [], []
    for src, land, mode in zip(src_refs, land_refs, modes, strict=True):
        if hops == _RELAY:
            assert mode != _SCATTER
            for k in _OTHER_CHIPS:
                came = block(land, mode, _slot_of(*peer_of(k)))
                remote.append(remote_copy(came, came, peer_of(_SIBLING)))
            continue
        dst = block(land, mode, mine)
        for k in ((_SIBLING,) + _OTHER_CHIPS if hops == _NEAR else range(1, N_DEV)):
            remote.append(remote_copy(src.at[_slot_of(*peer_of(k))] if mode == _SCATTER else src, dst, peer_of(k)))
        local.append(pltpu.make_async_copy(src.at[mine] if mode == _SCATTER else src, dst, recv_sem))
    return remote, local


def _wait_copies(remote, local):
    for cp in remote:
        cp.wait_send()
        cp.wait_recv()
    for cp in local:
        cp.wait()


def _exchange_start(groups, after, *, name, hops=_DIRECT):
    sizes = [len(srcs) for srcs, _ in groups]
    n, n_sems = sum(sizes), 2 * len(groups)
    srcs = [a for arrays, _ in groups for a in arrays]
    lands = [lax.empty(_landing_shape(a, mode), a.dtype)
             for arrays, modes in groups for a, mode in zip(arrays, modes, strict=True)]
    offsets = [sum(sizes[:g]) for g in range(len(groups))]

    def body(*refs):
        sems = refs[2 * n + 1:2 * n + 1 + n_sems]
        for g, (off, size, (_, modes)) in enumerate(zip(offsets, sizes, groups)):
            remote, local = _exchange_copies(refs[off:off + size], refs[n + off:n + off + size], sems[2 * g],
                                             sems[2 * g + 1], modes, hops)
            for cp in remote + local:
                cp.start()
        refs[-1][...] = jnp.zeros_like(refs[-1])

    thru = [pltpu.HBM(a.shape, a.dtype) for a in (*srcs, *lands)]
    out = pl.pallas_call(
        body,
        name=name,
        in_specs=[_HBM] * (2 * n) + [_ANY],
        out_specs=(*[_SEM] * n_sems, *[_HBM] * (2 * n), pl.BlockSpec(memory_space=pltpu.VMEM)),
        out_shape=(*[pltpu.SemaphoreType.DMA(())] * n_sems, *thru, jax.ShapeDtypeStruct((8, LANES), F32)),
        input_output_aliases={i: n_sems + i for i in range(2 * n)},
        compiler_params=pltpu.CompilerParams(has_side_effects=_DATAFLOW),
    )(*[pltpu.with_memory_space_constraint(a, pltpu.HBM) for a in (*srcs, *lands)], after)
    sems, arrays = out[:n_sems], out[n_sems:-1]
    started = [(sems[2 * g], sems[2 * g + 1], *arrays[off:off + size], *arrays[n + off:n + off + size])
               for g, (off, size) in enumerate(zip(offsets, sizes))]
    return started, out[-1]


def _exchange_relay(started, after, *, modes, regroup, name):
    send_sem, recv_sem, *thru = started
    n, n_sems = len(thru) // 2, 2 * len(regroup)

    def body(*refs):
        srcs, lands = refs[:n], refs[n:2 * n]
        _wait_copies(*_exchange_copies(srcs, lands, refs[2 * n], refs[2 * n + 1], modes, _NEAR))
        sems = refs[2 * n + 3:2 * n + 3 + n_sems]
        for g, members in enumerate(regroup):
            remote, _ = _exchange_copies([srcs[i] for i in members], [lands[i] for i in members], sems[2 * g],
                                         sems[2 * g + 1], [modes[i] for i in members], _RELAY)
            for cp in remote:
                cp.start()
        refs[-1][...] = jnp.zeros_like(refs[-1])

    out = pl.pallas_call(
        body,
        name=name,
        in_specs=[_HBM] * (2 * n) + [_SEM, _SEM, _ANY],
        out_specs=(*[_SEM] * n_sems, *[_HBM] * (2 * n), pl.BlockSpec(memory_space=pltpu.VMEM)),
        out_shape=(*[pltpu.SemaphoreType.DMA(())] * n_sems, *[pltpu.HBM(a.shape, a.dtype) for a in thru],
                   jax.ShapeDtypeStruct((8, LANES), F32)),
        input_output_aliases={i: n_sems + i for i in range(2 * n)},
        compiler_params=pltpu.CompilerParams(has_side_effects=_DATAFLOW),
    )(*thru, send_sem, recv_sem, after)
    sems, arrays = out[:n_sems], out[n_sems:-1]
    groups = [(sems[2 * g], sems[2 * g + 1], *[arrays[i] for i in members], *[arrays[n + i] for i in members])
              for g, members in enumerate(regroup)]
    return groups, out[-1]


def _exchange_wait(started, after, *, modes, name, hops=_DIRECT):
    send_sem, recv_sem, *thru = started
    n = len(thru) // 2

    def body(*refs):
        _wait_copies(*_exchange_copies(refs[:n], refs[n:2 * n], refs[2 * n], refs[2 * n + 1], modes, hops))

    out = pl.pallas_call(
        body,
        name=name,
        in_specs=[_HBM] * (2 * n) + [_SEM, _SEM, _ANY],
        out_specs=[_HBM] * (2 * n),
        out_shape=[pltpu.HBM(a.shape, a.dtype) for a in thru],
        input_output_aliases={i: i for i in range(2 * n)},
        compiler_params=pltpu.CompilerParams(has_side_effects=_DATAFLOW),
    )(*thru, send_sem, recv_sem, after)
    return out[n:]


def _gather_columns(g):
    return jnp.moveaxis(g, 0, 1).reshape(g.shape[1], -1)


def _split_rows(w):
    return w.reshape(N_DEV, w.shape[0] // N_DEV, w.shape[1])


_FIRST = ("w_in", "conv_w")
_REST = ("w_out", "w_ff1", "w_ff2", "w_ple_gate", "w_ple_proj")
_REST_GROUPS = (("w_out",), ("w_ff1",), ("w_ff2",), ("w_ple_gate", "w_ple_proj"))
_BIG = ("w_in",) + _REST
_GATHER_MODE = dict(w_in=_GATHER, conv_w=_GATHER, w_out=_GATHER, w_ff1=_GATHER_COLUMNS, w_ff2=_GATHER,
                    w_ple_gate=_GATHER, w_ple_proj=_GATHER_COLUMNS)
_RELAYOUT_AFTER_GATHER = ("conv_w",)
_SMALL = ("norm1_g", "q_norm_g", "k_norm_g", "sgu_norm_g", "sgu_w", "sgu_b", "norm2_g", "norm3_g")
_ORDER = ("norm1_g", "w_in", "conv_w", "q_norm_g", "k_norm_g", "sgu_norm_g", "sgu_w", "sgu_b", "w_out", "norm2_g",
          "w_ff1", "w_ff2", "norm3_g", "w_ple_gate", "w_ple_proj")


def _whole_matrices(names, landed):
    return {k: _gather_columns(g) if k in _RELAYOUT_AFTER_GATHER else g.reshape(-1, g.shape[-1])
            for k, g in zip(names, landed, strict=True)}


_NORM_FUSED_ROWS = 512


def _layer_forward(h0, hn1, p16, s, li, w_first, gathered, next_norm_g):
    nm = lambda k: f"{k}_l{li}"
    t, d = h0.shape
    w = dict(w_first)

    def add_and_norm(acc, res, gain):
        h = res + acc
        return h, _rms_rows(h, gain)

    proj = _matmul(hn1, w["w_in"], name=nm("proj"), tb=True, bm=t, bn=256)
    y_a = _conv_fwd(proj, w["conv_w"], name=nm("conv"))
    qs, kn, v = _qk_prep(proj, s["gq"], s["gk"], name=nm("qkprep"))
    y_b, lt = _attn_fwd(qs, kn, v, name=nm("attn"))
    gathered["relay_rest"](y_b)
    y_c = _sgu_fwd(proj, s["sgu_norm_g"], s["sgu_w"], s["b_exp"], name=nm("sgu"))
    mix = jnp.concatenate([y_a, y_b, y_c], axis=1)
    w.update(gathered["fetch"](0, mix))
    h1, hn2 = _matmul(mix, w["w_out"], name=nm("out"), bm=_NORM_FUSED_ROWS, bn=d, out_dtypes=(F32, BF16),
                      extras=(h0,), row_vectors=(s["norm2_g"],), epilogue=add_and_norm)
    w.update(gathered["fetch"](1, hn2))
    f = _matmul(hn2, w["w_ff1"], name=nm("ff1"), bm=t, bn=512, out_dtypes=(BF16,),
                epilogue=lambda acc: (jnp.square(jnp.maximum(acc, 0.0)),))
    w.update(gathered["fetch"](2, f))
    gathered["relay_next"](f)
    h2, hn3 = _matmul(f, w["w_ff2"], name=nm("ff2"), bm=_NORM_FUSED_ROWS, bn=d, out_dtypes=(F32, BF16),
                      extras=(h1,), row_vectors=(s["norm3_g"],), epilogue=add_and_norm)
    w.update(gathered["fetch"](3, hn3))
    w_next = gathered["fetch_next"](hn3)
    pp = _matmul(p16, w["w_ple_proj"], name=nm("pleproj"), bm=t, bn=512)

    def gate_epilogue(acc, pp_blk, h_blk, *gain):
        gate = jax.nn.sigmoid(acc)
        h = h_blk + gate * pp_blk
        return (h, gate) + tuple(_rms_rows(h, g) for g in gain)

    fused_norm = () if next_norm_g is None else (next_norm_g,)
    h3, gate, *hn1_next = _matmul(hn3, w["w_ple_gate"], name=nm("plegate"), bm=_NORM_FUSED_ROWS, bn=d,
                                  out_dtypes=(F32, F32) + (BF16,) * len(fused_norm), extras=(pp, h2),
                                  row_vectors=fused_norm, epilogue=gate_epilogue)
    saved = dict(h0=h0, hn1=hn1, proj=proj, qs=qs, kn=kn, v=v, lt=lt, mix=mix, h1=h1, hn2=hn2, f=f, h2=h2,
                 hn3=hn3, pp=pp, gate=gate, p16=p16)
    return h3, (hn1_next[0] if hn1_next else None), w, w_next, saved


def _layer_backward(dh3, a, w, s, li, order_after, start_rest):
    nm = lambda k: f"{k}_bwd_l{li}"
    t = dh3.shape[0]
    dpre, dpp = _ple_bwd(dh3, a["gate"], a["pp"], order_after, name=nm("ple"))
    g_gate = _weight_grad(a["hn3"], [dpre], name=nm("dwgate"))
    g_proj = _weight_grad(a["p16"], [dpp], name=nm("dwproj"), column_shards=True)
    dh2, dh2_16, g_n3 = _matmul_rms_bwd([dpre], w["w_ple_gate"], a["h2"], s["norm3_g"], dh3, name=nm("dh2"))
    du = _matmul(dh2_16, w["w_ff2"], name=nm("du"), tb=True, bm=t, bn=512, out_dtypes=(BF16,), extras=(a["f"],),
                 epilogue=lambda acc, f: (acc * (2.0 * jnp.sqrt(f.astype(F32))),))
    g_ff2 = _weight_grad(a["f"], [dh2_16], name=nm("dwff2"))
    g_ff1 = _weight_grad(a["hn2"], [du], name=nm("dwff1"), column_shards=True)
    dh1, dh1_16, g_n2 = _matmul_rms_bwd([du], w["w_ff1"], a["h1"], s["norm2_g"], dh2, name=nm("dh1"))
    dmix = _matmul(dh1_16, w["w_out"], name=nm("dmix"), tb=True, bm=t, bn=256)
    g_out = _weight_grad(a["mix"], [dh1_16], name=nm("dwout"))
    started = start_rest(dict(w_out=_split_rows(g_out), w_ff1=g_ff1, w_ff2=_split_rows(g_ff2),
                              w_ple_gate=_split_rows(g_gate), w_ple_proj=g_proj), dmix)
    d_conv, g_conv = _conv_bwd(dmix, a["proj"], w["conv_w"], name=nm("conv"))
    dqs, dkn, dv = _attn_bwd(dmix, a["qs"], a["kn"], a["v"], a["lt"], started, name=nm("attn"))
    d_qkv, g_q, g_k = _qk_prep_bwd(dqs, dkn, dv, a["proj"], s["gq"], s["gk"], name=nm("qkprep"))
    d_sgu, g_sn, g_sw, g_sb = _sgu_bwd(dmix, a["proj"], s["sgu_norm_g"], s["sgu_w"], s["b_exp"], name=nm("sgu"))
    dproj = [d_conv, d_qkv, d_sgu]
    g_in = jnp.concatenate([_weight_grad(piece, [a["hn1"]], name=nm(f"dwin{i}")) for i, piece in enumerate(dproj)])
    dh0, _, g_n1 = _matmul_rms_bwd(dproj, w["w_in"], a["h0"], s["norm1_g"], dh1, name=nm("dh0"), w_is_k_by_d=True)
    small = dict(norm1_g=g_n1, norm2_g=g_n2, norm3_g=g_n3, q_norm_g=g_q, k_norm_g=g_k, sgu_norm_g=g_sn, sgu_w=g_sw,
                 sgu_b=g_sb, conv_w=g_conv)
    return dh0, _split_rows(g_in), small


def _small_gradients(raw, depth):
    return dict(
        norm1_g=raw["norm1_g"].reshape(depth, -1), norm2_g=raw["norm2_g"].reshape(depth, -1),
        norm3_g=raw["norm3_g"].reshape(depth, -1),
        q_norm_g=raw["q_norm_g"].reshape(depth, -1, HEAD_DIM).sum(1),
        k_norm_g=raw["k_norm_g"].reshape(depth, -1, HEAD_DIM).sum(1),
        sgu_norm_g=raw["sgu_norm_g"].reshape(depth, -1), sgu_w=raw["sgu_w"],
        sgu_b=jnp.swapaxes(raw["sgu_b"].reshape(depth, CHUNK, SGU_HEADS, HEAD_DIM).sum(-1), 1, 2),
        conv_w=raw["conv_w"][:, :CONV_TAPS],
    )


def kernel(x, p, norm1_g, w_in, conv_w, q_norm_g, k_norm_g, sgu_norm_g, sgu_w, sgu_b, w_out, norm2_g, w_ff1, w_ff2, norm3_g, w_ple_gate, w_ple_proj, loss_target, m_norm1_g, m_w_in, m_conv_w, m_q_norm_g, m_k_norm_g, m_sgu_norm_g, m_sgu_w, m_sgu_b, m_w_out, m_norm2_g, m_w_ff1, m_w_ff2, m_norm3_g, m_w_ple_gate, m_w_ple_proj, v_norm1_g, v_w_in, v_conv_w, v_q_norm_g, v_k_norm_g, v_sgu_norm_g, v_sgu_w, v_sgu_b, v_w_out, v_norm2_g, v_w_ff1, v_w_ff2, v_norm3_g, v_w_ple_gate, v_w_ple_proj):
    weights = dict(norm1_g=norm1_g, w_in=w_in, conv_w=conv_w, q_norm_g=q_norm_g, k_norm_g=k_norm_g,
                   sgu_norm_g=sgu_norm_g, sgu_w=sgu_w, sgu_b=sgu_b, w_out=w_out, norm2_g=norm2_g, w_ff1=w_ff1,
                   w_ff2=w_ff2, norm3_g=norm3_g, w_ple_gate=w_ple_gate, w_ple_proj=w_ple_proj)
    mom = dict(norm1_g=m_norm1_g, w_in=m_w_in, conv_w=m_conv_w, q_norm_g=m_q_norm_g, k_norm_g=m_k_norm_g,
               sgu_norm_g=m_sgu_norm_g, sgu_w=m_sgu_w, sgu_b=m_sgu_b, w_out=m_w_out, norm2_g=m_norm2_g, w_ff1=m_w_ff1,
               w_ff2=m_w_ff2, norm3_g=m_norm3_g, w_ple_gate=m_w_ple_gate, w_ple_proj=m_w_ple_proj)
    var = dict(norm1_g=v_norm1_g, w_in=v_w_in, conv_w=v_conv_w, q_norm_g=v_q_norm_g, k_norm_g=v_k_norm_g,
               sgu_norm_g=v_sgu_norm_g, sgu_w=v_sgu_w, sgu_b=v_sgu_b, w_out=v_w_out, norm2_g=v_norm2_g, w_ff1=v_w_ff1,
               w_ff2=v_w_ff2, norm3_g=v_norm3_g, w_ple_gate=v_w_ple_gate, w_ple_proj=v_w_ple_proj)
    for params in (weights, mom, var):
        params["w_in"] = jnp.swapaxes(params["w_in"], 1, 2)
    depth = norm1_g.shape[0]
    mx, my, mc = _my_place()
    me = _slot_of(mx, my, mc)

    gathers = []
    modes_of = lambda names: tuple(_GATHER_MODE[k] for k in names)
    token = x[0, :8, :LANES]
    for li in range(depth):
        groups = [([weights[k][li] if k == "conv_w" else weights[k][li].astype(BF16) for k in names], modes_of(names))
                  for names in (_FIRST, _REST)]
        started, token = _exchange_start(groups, token, name=f"gather_start_l{li}", hops=_NEAR)
        gathers.append(started)

    small = []
    for li in range(depth):
        small.append(dict(
            norm1_g=norm1_g[li][None], norm2_g=norm2_g[li][None], norm3_g=norm3_g[li][None],
            gq=jnp.tile(q_norm_g[li], _QK_BLOCK // HEAD_DIM)[None], gk=jnp.tile(k_norm_g[li], _QK_BLOCK // HEAD_DIM)[None],
            sgu_norm_g=sgu_norm_g[li][None], sgu_w=sgu_w[li], b_exp=jnp.repeat(sgu_b[li].T, HEAD_DIM, axis=1),
        ))
    small[0]["norm1_g"] = small[0]["norm1_g"] + token[0, 0]

    h = x[0]
    saved, full = [], []
    relayed_first, relayed_rest = [None] * depth, [None] * depth
    rest_members = [[_REST.index(k) for k in names] for names in _REST_GROUPS]

    def relay_first(li, after):
        if li < depth:
            (relayed_first[li],), _ = _exchange_relay(gathers[li][0], after, modes=modes_of(_FIRST),
                                                      regroup=[list(range(len(_FIRST)))], name=f"gather_first_relay_l{li}")

    def fetch_first(li, after):
        if li == depth:
            return None
        landed = _exchange_wait(relayed_first[li], after, modes=modes_of(_FIRST), hops=_RELAY,
                                name=f"gather_first_wait_l{li}")
        return _whole_matrices(_FIRST, landed)

    hn1 = _rms_fwd(h, small[0]["norm1_g"], name="rms1_l0")
    relay_first(0, hn1)
    w_first = fetch_first(0, hn1)
    for li in range(depth):

        def relay_rest(after, li=li):
            relayed_rest[li], _ = _exchange_relay(gathers[li][1], after, modes=modes_of(_REST), regroup=rest_members,
                                                  name=f"gather_rest_relay_l{li}")

        def fetch(g, after, li=li):
            landed = _exchange_wait(relayed_rest[li][g], after, modes=modes_of(_REST_GROUPS[g]), hops=_RELAY,
                                    name=f"gather_{_REST_GROUPS[g][0]}_wait_l{li}")
            return _whole_matrices(_REST_GROUPS[g], landed)

        gathered = dict(relay_rest=relay_rest, fetch=fetch, relay_next=functools.partial(relay_first, li + 1),
                        fetch_next=functools.partial(fetch_first, li + 1))
        next_norm_g = small[li + 1]["norm1_g"] if li + 1 < depth else None
        h, hn1, w, w_first, acts = _layer_forward(h, hn1, p[li, 0].astype(BF16), small[li], li, w_first, gathered,
                                                  next_norm_g)
        full.append(w)
        saved.append(acts)
    dh, loss_tile = _loss_head(h, loss_target[0], name="loss_head")
    loss = lax.psum(loss_tile[0, 0], ("x", "y", "c"))

    small_names = _SMALL + ("conv_w",)
    scatter_first, scatter_rest = [None] * depth, [None] * depth
    first_modes, rest_modes = (_SCATTER,) + (_GATHER,) * len(small_names), (_SCATTER,) * len(_REST)
    token = loss_tile
    for li in reversed(range(depth)):

        def start_rest(parts, after, li=li):
            (scatter_rest[li],), started = _exchange_start([([parts[k] for k in _REST], rest_modes)], after,
                                                           name=f"scatter_rest_start_l{li}")
            return started

        dh, g_in, small_grads = _layer_backward(dh, saved[li], full[li], small[li], li, token, start_rest)
        (scatter_first[li],), token = _exchange_start(
            [([g_in] + [small_grads[k] for k in small_names], first_modes)], dh, name=f"scatter_first_start_l{li}")
    grad_x = dh[None]

    grads, delta, new_m, new_v = {}, {}, {}, {}
    arrived = {k: [None] * depth for k in _BIG}
    for li in reversed(range(depth)):
        landed = _exchange_wait(scatter_rest[li], token, modes=rest_modes, name=f"scatter_rest_wait_l{li}")
        for k, g in zip(_REST, landed, strict=True):
            arrived[k][li] = g
    for k in _REST:
        grads[k], delta[k], new_m[k], new_v[k] = _adamw_reduce(weights[k], arrived[k], mom[k], var[k], name=f"adamw_{k}")
    small_parts = {k: [None] * depth for k in small_names}
    updated = jnp.stack([delta[k][0, 0, :1] for k in _REST])
    for li in reversed(range(depth)):
        arrived["w_in"][li], *parts = _exchange_wait(scatter_first[li], updated, modes=first_modes,
                                                     name=f"scatter_first_wait_l{li}")
        for k, part in zip(small_names, parts, strict=True):
            small_parts[k][li] = part
    grads["w_in"], delta["w_in"], new_m["w_in"], new_v["w_in"] = _adamw_reduce(
        weights["w_in"], arrived["w_in"], mom["w_in"], var["w_in"], name="adamw_w_in")
    for results in (grads, delta, new_m, new_v):
        results["w_in"] = jnp.swapaxes(results["w_in"], 1, 2)
    sums = _sum_slots([small_parts[k] for k in small_names], name="sum_small_grads")
    grads.update(_small_gradients(dict(zip(small_names, sums)), depth))
    n_conv = conv_w.shape[2]
    grads["conv_w"] = lax.dynamic_slice_in_dim(grads["conv_w"], me * n_conv, n_conv, axis=2)
    for k in small_names:
        as_rows = lambda a: a.reshape(-1, a.shape[-1])
        outs = _adamw(as_rows(weights[k]), as_rows(grads[k]), as_rows(mom[k]), as_rows(var[k]), name=f"adamw_{k}")
        delta[k], new_m[k], new_v[k] = (o.reshape(weights[k].shape) for o in outs)

    return (loss, grad_x, *[grads[k] for k in _ORDER], *[delta[k] for k in _ORDER],
            *[new_m[k] for k in _ORDER], *[new_v[k] for k in _ORDER])
```

```python
import functools
import math

import jax
import jax.numpy as jnp
from jax import lax
from jax.experimental import pallas as pl
from jax.experimental.pallas import tpu as pltpu

F32 = jnp.float32
BF16 = jnp.bfloat16

N_DEV = 8
HEAD_DIM = 64
CONV_W = 256
ATTN_W = 512
SGU_W = 256
SGU_HEADS = 4
CHUNK = 128
CONV_TAPS = 3
EPS = 1e-6
QK_SCALE = HEAD_DIM ** -0.5

ADAM_LR = 0.001
ADAM_B1 = 0.9
ADAM_B2 = 0.999
ADAM_EPS = 1e-08
ADAM_WD = 0.01
ADAM_STEP = 10

LANES = 128
BF16_TILE_ROWS = 16
VMEM_LIMIT_BYTES = 56 * 1024 * 1024
MESH = pl.DeviceIdType.MESH


def _params(*sem):
    return pltpu.CompilerParams(dimension_semantics=sem, vmem_limit_bytes=VMEM_LIMIT_BYTES)


def _row_block(rows, cap):
    if rows <= cap:
        return rows
    return max(b for b in range(BF16_TILE_ROWS, cap + 1, BF16_TILE_ROWS) if rows % b == 0)


def _matmul(a, b, *, name, tb=False, bm=512, bn=512, out_dtypes=(F32,), epilogue=None, extras=(), row_vectors=()):
    m, k = a.shape
    n = b.shape[0] if tb else b.shape[1]
    assert k == (b.shape[1] if tb else b.shape[0])
    bm, bn = min(bm, m), min(bn, n)
    assert m % bm == 0 and n % bn == 0
    a_spec = pl.BlockSpec((bm, k), lambda i, j: (i, 0))
    b_spec = pl.BlockSpec((bn, k), lambda i, j: (j, 0)) if tb else pl.BlockSpec((k, bn), lambda i, j: (0, j))
    dims = (((1,), (1 if tb else 0,)), ((), ()))
    n_in = len(extras) + len(row_vectors)
    for e in extras:
        assert e.shape == (m, n), (e.shape, m, n)
    for e in row_vectors:
        assert e.shape == (1, n), (e.shape, n)

    def body(a_ref, b_ref, *rest):
        outs = rest[n_in:]
        acc = lax.dot_general(a_ref[...], b_ref[...], dims, preferred_element_type=F32)
        res = (acc,) if epilogue is None else epilogue(acc, *[e[...] for e in rest[:n_in]])
        for o_ref, r in zip(outs, res, strict=True):
            o_ref[...] = r.astype(o_ref.dtype)

    tile = pl.BlockSpec((bm, bn), lambda i, j: (i, j))
    vec = pl.BlockSpec((1, bn), lambda i, j: (0, j))
    out = pl.pallas_call(
        body,
        name=name,
        grid=(m // bm, n // bn),
        in_specs=[a_spec, b_spec] + [tile] * len(extras) + [vec] * len(row_vectors),
        out_specs=[tile] * len(out_dtypes),
        out_shape=[jax.ShapeDtypeStruct((m, n), d) for d in out_dtypes],
        compiler_params=_params("parallel", "parallel"),
    )(a, b, *extras, *row_vectors)
    return out[0] if len(out_dtypes) == 1 else out


def _rms_rows(x, g):
    return x * lax.rsqrt(jnp.mean(x * x, axis=-1, keepdims=True) + EPS) * g


_WEIGHT_GRAD_ACC_ELEMS = 1024 * 1024


def _weight_grad(x, dys, *, name, column_shards=False):
    t, m = x.shape
    n = sum(dy.shape[1] for dy in dys)
    bm = m if m <= 2 * LANES else min(m // 2, max(LANES, _WEIGHT_GRAD_ACC_ELEMS // n // LANES * LANES))
    assert m % bm == 0
    ns = n // N_DEV

    def body(x_ref, *rest):
        o_ref = rest[-1]
        xb = x_ref[...]
        acc = jnp.concatenate([lax.dot_general(xb, dy_ref[...], _TN, preferred_element_type=F32) for dy_ref in rest[:-1]],
                              axis=1)
        if column_shards:
            for s in range(N_DEV):
                o_ref[s] = acc[:, s * ns:(s + 1) * ns].astype(o_ref.dtype)
        else:
            o_ref[...] = acc.astype(o_ref.dtype)

    if column_shards:
        out_spec, out_dims = pl.BlockSpec((N_DEV, bm, ns), lambda i: (0, i, 0)), (N_DEV, m, ns)
    else:
        out_spec, out_dims = pl.BlockSpec((bm, n), lambda i: (i, 0)), (m, n)
    return pl.pallas_call(
        body,
        name=name,
        grid=(m // bm,),
        in_specs=[pl.BlockSpec((t, bm), lambda i: (0, i))] + [pl.BlockSpec(dy.shape, lambda i: (0, 0)) for dy in dys],
        out_specs=out_spec,
        out_shape=jax.ShapeDtypeStruct(out_dims, BF16),
        compiler_params=_params("parallel"),
    )(x, *dys)


def _rms_fwd(h, g, *, name, br=512):
    t, d = h.shape
    br = min(br, t)

    def body(h_ref, g_ref, o_ref):
        o_ref[...] = _rms_rows(h_ref[...], g_ref[...]).astype(o_ref.dtype)

    return pl.pallas_call(
        body,
        name=name,
        grid=(t // br,),
        in_specs=[pl.BlockSpec((br, d), lambda i: (i, 0)), pl.BlockSpec((1, d), lambda i: (0, 0))],
        out_specs=pl.BlockSpec((br, d), lambda i: (i, 0)),
        out_shape=jax.ShapeDtypeStruct((t, d), BF16),
        compiler_params=_params("parallel"),
    )(h, g)


def _matmul_rms_bwd(dzs, w, h, g, dres, *, name, w_is_k_by_d=False):
    t, d = h.shape
    widths = [dz.shape[1] for dz in dzs]
    k = sum(widths)
    assert w.shape == ((k, d) if w_is_k_by_d else (d, k))
    br = min(t, _NORM_FUSED_ROWS)
    n_dz = len(dzs)

    def body(*refs):
        w_ref, h_ref, g_ref, dres_ref, dh_ref, dh16_ref, dg_ref = refs[n_dz:]
        x = h_ref[...]
        dyv, at = None, 0
        for dz_ref, width in zip(refs[:n_dz], widths):
            if w_is_k_by_d:
                part = jnp.dot(dz_ref[...], w_ref[at:at + width, :], preferred_element_type=F32)
            else:
                part = lax.dot_general(dz_ref[...], w_ref[:, at:at + width], _NT, preferred_element_type=F32)
            dyv = part if dyv is None else dyv + part
            at += width
        r = lax.rsqrt(jnp.mean(x * x, axis=-1, keepdims=True) + EPS)
        xhat = x * r
        dxhat = dyv * g_ref[...]
        dh = dres_ref[...] + r * (dxhat - xhat * jnp.mean(dxhat * xhat, axis=-1, keepdims=True))
        dh_ref[...] = dh
        dh16_ref[...] = dh.astype(dh16_ref.dtype)

        @pl.when(pl.program_id(0) == 0)
        def _():
            dg_ref[...] = jnp.zeros_like(dg_ref)

        dg_ref[...] += jnp.sum(dyv * xhat, axis=0, keepdims=True)

    row = pl.BlockSpec((br, d), lambda i: (i, 0))
    vec = pl.BlockSpec((1, d), lambda i: (0, 0))
    return pl.pallas_call(
        body,
        name=name,
        grid=(t // br,),
        in_specs=[pl.BlockSpec((br, width), lambda i: (i, 0)) for width in widths]
        + [pl.BlockSpec(w.shape, lambda i: (0, 0)), row, vec, row],
        out_specs=[row, row, vec],
        out_shape=[jax.ShapeDtypeStruct((t, d), F32), jax.ShapeDtypeStruct((t, d), BF16),
                   jax.ShapeDtypeStruct((1, d), F32)],
        compiler_params=_params("arbitrary"),
    )(*dzs, w, h, g, dres)


def _group_mean(x, width):
    grp = lax.broadcasted_iota(jnp.int32, x.shape, 1) // HEAD_DIM
    out = jnp.zeros_like(x)
    for gi in range(width // HEAD_DIM):
        m = grp == gi
        s = jnp.sum(jnp.where(m, x, 0.0), axis=1, keepdims=True)
        out = jnp.where(m, s, out)
    return out * (1.0 / HEAD_DIM)


def _gelu(x):
    return 0.5 * x * (1.0 + lax.erf(x * (2.0 ** -0.5)))


def _gelu_grad(x):
    cdf = 0.5 * (1.0 + lax.erf(x * (2.0 ** -0.5)))
    pdf = jnp.exp(-0.5 * x * x) * (1.0 / math.sqrt(2.0 * math.pi))
    return cdf + x * pdf


def _shift_down(z, s, row):
    return jnp.where(row >= s, pltpu.roll(z, s, 0), 0.0)


def _shift_up(z, s, row, t):
    return jnp.where(row < t - s, pltpu.roll(z, t - s, 0), 0.0)


def _conv_fwd(proj, conv_w, *, name):
    t = proj.shape[0]
    nb = CONV_W // LANES

    def body(b_ref, c_ref, h_ref, w_ref, o_ref):
        row = lax.broadcasted_iota(jnp.int32, (t, LANES), 0)
        z = c_ref[...] * h_ref[...]
        w = w_ref[...]
        conv = w[2:3, :] * z + w[1:2, :] * _shift_down(z, 1, row) + w[0:1, :] * _shift_down(z, 2, row)
        o_ref[...] = (b_ref[...] * conv).astype(o_ref.dtype)

    return pl.pallas_call(
        body,
        name=name,
        grid=(nb,),
        in_specs=[
            pl.BlockSpec((t, LANES), lambda j: (0, j)),
            pl.BlockSpec((t, LANES), lambda j: (0, nb + j)),
            pl.BlockSpec((t, LANES), lambda j: (0, 2 * nb + j)),
            pl.BlockSpec((CONV_TAPS, LANES), lambda j: (0, j)),
        ],
        out_specs=pl.BlockSpec((t, LANES), lambda j: (0, j)),
        out_shape=jax.ShapeDtypeStruct((t, CONV_W), BF16),
        compiler_params=_params("parallel"),
    )(proj, proj, proj, conv_w)


def _conv_bwd(dmix, proj, conv_w, *, name):
    t = proj.shape[0]

    def body(dy_ref, b_ref, c_ref, h_ref, w_ref, o_ref, dw_ref):
        row = lax.broadcasted_iota(jnp.int32, (t, CONV_W), 0)
        ac, ah = c_ref[...], h_ref[...]
        z = ac * ah
        w = w_ref[...]
        z1 = _shift_down(z, 1, row)
        z2 = _shift_down(z, 2, row)
        conv = w[2:3, :] * z + w[1:2, :] * z1 + w[0:1, :] * z2
        dy = dy_ref[...]
        o_ref[:, 0:CONV_W] = (dy * conv).astype(o_ref.dtype)
        dconv = dy * b_ref[...]
        dz = w[2:3, :] * dconv + w[1:2, :] * _shift_up(dconv, 1, row, t) + w[0:1, :] * _shift_up(dconv, 2, row, t)
        o_ref[:, CONV_W:2 * CONV_W] = (dz * ah).astype(o_ref.dtype)
        o_ref[:, 2 * CONV_W:3 * CONV_W] = (dz * ac).astype(o_ref.dtype)
        dw_ref[...] = jnp.zeros_like(dw_ref)
        dw_ref[0:1, :] = jnp.sum(dconv * z2, axis=0, keepdims=True)
        dw_ref[1:2, :] = jnp.sum(dconv * z1, axis=0, keepdims=True)
        dw_ref[2:3, :] = jnp.sum(dconv * z, axis=0, keepdims=True)

    col = lambda j: pl.BlockSpec((t, CONV_W), lambda i: (0, j))
    return pl.pallas_call(
        body,
        name=name,
        grid=(1,),
        in_specs=[col(0), col(0), col(1), col(2), pl.BlockSpec((CONV_TAPS, CONV_W), lambda i: (0, 0))],
        out_specs=[pl.BlockSpec((t, 3 * CONV_W), lambda i: (0, 0)), pl.BlockSpec((8, CONV_W), lambda i: (0, 0))],
        out_shape=[jax.ShapeDtypeStruct((t, 3 * CONV_W), BF16), jax.ShapeDtypeStruct((8, CONV_W), F32)],
        compiler_params=_params("arbitrary"),
    )(dmix, proj, proj, proj, conv_w)


_QK_BLOCK = 256


def _qk_prep(proj, gq, gk, *, name, br=512):
    t = proj.shape[0]
    br = min(br, t)
    nb = ATTN_W // _QK_BLOCK
    q0 = (3 * CONV_W) // _QK_BLOCK

    def body(q_ref, k_ref, v_ref, gq_ref, gk_ref, qo_ref, ko_ref, vo_ref):
        q = q_ref[...]
        k = k_ref[...]
        rq = lax.rsqrt(_group_mean(q * q, _QK_BLOCK) + EPS)
        rk = lax.rsqrt(_group_mean(k * k, _QK_BLOCK) + EPS)
        qo_ref[...] = ((q * rq * gq_ref[...]).astype(BF16) * QK_SCALE).astype(qo_ref.dtype)
        ko_ref[...] = (k * rk * gk_ref[...]).astype(ko_ref.dtype)
        vo_ref[...] = v_ref[...].astype(vo_ref.dtype)

    col = lambda off: pl.BlockSpec((br, _QK_BLOCK), lambda i, j: (i, off + j))
    vec = pl.BlockSpec((1, _QK_BLOCK), lambda i, j: (0, 0))
    return pl.pallas_call(
        body,
        name=name,
        grid=(t // br, nb),
        in_specs=[col(q0), col(q0 + nb), col(q0 + 2 * nb), vec, vec],
        out_specs=[col(0)] * 3,
        out_shape=[jax.ShapeDtypeStruct((t, ATTN_W), BF16)] * 3,
        compiler_params=_params("parallel", "parallel"),
    )(proj, proj, proj, gq, gk)


def _qk_prep_bwd(dqs, dkn, dv, proj, gq, gk, *, name, br=256):
    t = proj.shape[0]
    br = min(br, t)
    nb = ATTN_W // _QK_BLOCK
    q0 = (3 * CONV_W) // _QK_BLOCK

    def norm_bwd(dy, x, g):
        r = lax.rsqrt(_group_mean(x * x, ATTN_W) + EPS)
        xhat = x * r
        dxhat = dy * g
        dx = r * (dxhat - xhat * _group_mean(dxhat * xhat, ATTN_W))
        return dx, jnp.sum(dy * xhat, axis=0, keepdims=True)

    def body(dq_ref, dk_ref, dv_ref, *rest):
        x_refs, (gq_ref, gk_ref, o_ref, dgq_ref, dgk_ref) = rest[:2 * nb], rest[2 * nb:]
        whole = lambda refs: jnp.concatenate([r[...] for r in refs], axis=1)
        dq, dgq = norm_bwd(dq_ref[...] * QK_SCALE, whole(x_refs[:nb]), whole([gq_ref] * nb))
        dk, dgk = norm_bwd(dk_ref[...], whole(x_refs[nb:]), whole([gk_ref] * nb))
        o_ref[:, 0:ATTN_W] = dq.astype(o_ref.dtype)
        o_ref[:, ATTN_W:2 * ATTN_W] = dk.astype(o_ref.dtype)
        o_ref[:, 2 * ATTN_W:3 * ATTN_W] = dv_ref[...].astype(o_ref.dtype)

        @pl.when(pl.program_id(0) == 0)
        def _():
            dgq_ref[...] = jnp.zeros_like(dgq_ref)
            dgk_ref[...] = jnp.zeros_like(dgk_ref)

        dgq_ref[...] += dgq
        dgk_ref[...] += dgk

    rows = pl.BlockSpec((br, ATTN_W), lambda i: (i, 0))
    col = lambda j: pl.BlockSpec((br, _QK_BLOCK), lambda i: (i, j))
    gain = pl.BlockSpec((1, _QK_BLOCK), lambda i: (0, 0))
    total = pl.BlockSpec((1, ATTN_W), lambda i: (0, 0))
    return pl.pallas_call(
        body,
        name=name,
        grid=(t // br,),
        in_specs=[rows, rows, rows] + [col(q0 + j) for j in range(2 * nb)] + [gain, gain],
        out_specs=[pl.BlockSpec((br, 3 * ATTN_W), lambda i: (i, 0)), total, total],
        out_shape=[jax.ShapeDtypeStruct((t, 3 * ATTN_W), BF16)] + [jax.ShapeDtypeStruct((1, ATTN_W), F32)] * 2,
        compiler_params=_params("arbitrary"),
    )(dqs, dkn, dv, *[proj] * (2 * nb), gq, gk)


def _key_order_matrix(tb, relation):
    jj = lax.broadcasted_iota(jnp.int32, (tb, tb), 0)
    ss = lax.broadcasted_iota(jnp.int32, (tb, tb), 1)
    return relation(jj, ss).astype(BF16)


def _log_sigmoids(z):
    lb = jnp.minimum(z, 0.0) - jnp.log(1.0 + jnp.exp(-jnp.abs(z)))
    return lb, lb - z


def _below_diagonal(tb):
    return lax.broadcasted_iota(jnp.int32, (tb, tb), 1) < lax.broadcasted_iota(jnp.int32, (tb, tb), 0)


_NT = (((1,), (1,)), ((), ()))
_TN = (((0,), (0,)), ((), ()))
_ATTN_BLOCK = 256
_ATTN_FWD_UNROLL = 4
_ATTN_BWD_UNROLL = 4


def _attn_fwd(qs, kn, v, *, name, tb=_ATTN_BLOCK, unroll=_ATTN_FWD_UNROLL):
    t = qs.shape[0]
    tb = min(tb, t)
    assert t % tb == 0
    n_pairs = ATTN_W // LANES

    def body(q_ref, k_ref, v_ref, o_ref, lt_ref, acc_ref, carry_ref):
        qb = pl.program_id(1)
        half = lax.broadcasted_iota(jnp.int32, (1, LANES), 1) // HEAD_DIM
        later = _key_order_matrix(tb, lambda j, s: j > s)
        acc_ref[...] = jnp.zeros_like(acc_ref)
        carry_ref[...] = jnp.zeros_like(carry_ref)
        q = q_ref[...]
        qh = [jnp.where(half == h, q, jnp.zeros_like(q)) for h in range(2)]

        def tiles(kbs, first_is_diagonal):
            blk = []
            for kb in kbs:
                start = pl.multiple_of(kb * tb, tb)
                blk.append((k_ref[pl.ds(start, tb), :], v_ref[pl.ds(start, tb), :]))
            chains = [(h, j) for j in range(len(kbs)) for h in range(2)]
            masked = [first_is_diagonal and j == 0 for _, j in chains]
            z = [lax.dot_general(qh[h], blk[j][0], _NT, preferred_element_type=F32) for h, j in chains]
            causal = _below_diagonal(tb) if first_is_diagonal else None
            lb, lr = [], []
            for zi, mask in zip(z, masked):
                b, r = _log_sigmoids(zi)
                lb.append(b)
                lr.append(jnp.where(causal, r, 0.0) if mask else r)
            suffix = [jnp.dot(r.astype(BF16), later, preferred_element_type=F32) for r in lr]
            carry = [carry_ref[0], carry_ref[1]]
            w = []
            for i, (h, j) in enumerate(chains):
                wi = jnp.exp(lb[i] + (suffix[i] + carry[h][:, 0:1]))
                w.append((jnp.where(causal, wi, 0.0) if masked[i] else wi).astype(BF16))
                carry[h] = carry[h] + jnp.sum(lr[i], axis=1, keepdims=True)
            for i, (h, j) in enumerate(chains):
                vh = jnp.where(half == h, blk[j][1], jnp.zeros_like(blk[j][1]))
                acc_ref[h] += jnp.dot(w[i], vh, preferred_element_type=F32)
            carry_ref[0] = carry[0]
            carry_ref[1] = carry[1]

        @pl.when(qb == 0)
        def _():
            tiles([qb], True)

        @pl.when(qb > 0)
        def _():
            tiles([qb, qb - 1], True)
            rest = qb - 1

            def step(i, _):
                kb = rest - 1 - unroll * i
                tiles([kb - u for u in range(unroll)], False)
                return 0

            lax.fori_loop(0, rest // unroll, step, 0)
            for left in range(1, unroll):

                @pl.when(rest % unroll == left)
                def _(left=left):
                    tiles([left - 1 - u for u in range(left)], False)

        o_ref[...] = (acc_ref[0] + acc_ref[1]).astype(o_ref.dtype)
        lt_ref[...] = jnp.where(half == 0, carry_ref[0], carry_ref[1])

    return pl.pallas_call(
        body,
        name=name,
        grid=(n_pairs, t // tb),
        in_specs=[
            pl.BlockSpec((tb, LANES), lambda p, i: (i, p)),
            pl.BlockSpec((t, LANES), lambda p, i: (0, p)),
            pl.BlockSpec((t, LANES), lambda p, i: (0, p)),
        ],
        out_specs=[pl.BlockSpec((tb, LANES), lambda p, i: (i, p))] * 2,
        out_shape=[jax.ShapeDtypeStruct((t, ATTN_W), BF16), jax.ShapeDtypeStruct((t, ATTN_W), F32)],
        scratch_shapes=[pltpu.VMEM((2, tb, LANES), F32), pltpu.VMEM((2, tb, LANES), F32)],
        compiler_params=_params("parallel", "parallel"),
    )(qs, kn, v)


def _attn_bwd(dmix, qs, kn, v, lt, order_after, *, name, tb=_ATTN_BLOCK, unroll=_ATTN_BWD_UNROLL):
    t = qs.shape[0]
    tb = min(tb, t)
    assert t % tb == 0
    n_pairs = ATTN_W // LANES
    dy0 = CONV_W // LANES

    def body(do_ref, q_ref, k_ref, v_ref, lt_ref, order_ref, dq_ref, dk_ref, dv_ref, dqacc_ref, cc_ref, cg_ref):
        qb = pl.program_id(1)
        half = lax.broadcasted_iota(jnp.int32, (1, LANES), 1) // HEAD_DIM
        lane = lax.broadcasted_iota(jnp.int32, (tb, LANES), 1)
        later = _key_order_matrix(tb, lambda j, s: j > s)
        before = _key_order_matrix(tb, lambda j, s: j < s)
        q = q_ref[...]
        do = do_ref[...].astype(BF16)
        lt = lt_ref[...]
        qh = [jnp.where(half == h, q, jnp.zeros_like(q)) for h in range(2)]
        doh = [jnp.where(half == h, do, jnp.zeros_like(do)) for h in range(2)]
        lth = [jnp.sum(jnp.where(lane == h * HEAD_DIM, lt, 0.0), axis=1, keepdims=True) for h in range(2)]

        @pl.when(qb == 0)
        def _():
            dk_ref[...] = jnp.zeros_like(dk_ref)
            dv_ref[...] = jnp.zeros_like(dv_ref)

        dqacc_ref[...] = jnp.zeros_like(dqacc_ref)
        cc_ref[...] = jnp.zeros_like(cc_ref)
        cg_ref[...] = jnp.zeros_like(cg_ref)

        def tiles(kbs, last_is_diagonal):
            starts = [pl.multiple_of(kb * tb, tb) for kb in kbs]
            blk = [(k_ref[pl.ds(s, tb), :], v_ref[pl.ds(s, tb), :]) for s in starts]
            chains = [(h, j) for j in range(len(kbs)) for h in range(2)]
            masked = [last_is_diagonal and j == len(kbs) - 1 for _, j in chains]
            z = [lax.dot_general(qh[h], blk[j][0], _NT, preferred_element_type=F32) for h, j in chains]
            da = [lax.dot_general(doh[h], jnp.where(half == h, blk[j][1], jnp.zeros_like(blk[j][1])), _NT,
                                  preferred_element_type=F32) for h, j in chains]
            causal = _below_diagonal(tb) if last_is_diagonal else None
            lb, lr = [], []
            for zi, mask in zip(z, masked):
                b, r = _log_sigmoids(zi)
                lb.append(b)
                lr.append(jnp.where(causal, r, 0.0) if mask else r)
            suffix = [jnp.dot(r.astype(BF16), later, preferred_element_type=F32) for r in lr]
            cc = [cc_ref[0], cc_ref[1]]
            cg = [cg_ref[0], cg_ref[1]]
            a16, g = [], []
            for i, (h, j) in enumerate(chains):
                cc[h] = cc[h] + jnp.sum(lr[i], axis=1, keepdims=True)
                a = jnp.exp(lb[i] + suffix[i] + (lth[h] - cc[h][:, 0:1]))
                if masked[i]:
                    a = jnp.where(causal, a, 0.0)
                a16.append(a.astype(BF16))
                g.append(da[i] * a)
            g_before = [jnp.dot(gi.astype(BF16), before, preferred_element_type=F32) for gi in g]
            dz = []
            for i, (h, j) in enumerate(chains):
                dzi = g[i] - jnp.exp(lb[i]) * (g[i] + (g_before[i] + cg[h][:, 0:1]))
                dz.append((jnp.where(causal, dzi, 0.0) if masked[i] else dzi).astype(BF16))
                cg[h] = cg[h] + jnp.sum(g[i], axis=1, keepdims=True)
            for i, (h, j) in enumerate(chains):
                kh = jnp.where(half == h, blk[j][0], jnp.zeros_like(blk[j][0]))
                dqacc_ref[h] += jnp.dot(dz[i], kh, preferred_element_type=F32)
                dk_ref[pl.ds(starts[j], tb), :] += lax.dot_general(dz[i], qh[h], _TN, preferred_element_type=F32)
                dv_ref[pl.ds(starts[j], tb), :] += lax.dot_general(a16[i], doh[h], _TN, preferred_element_type=F32)
            for h in range(2):
                cc_ref[h] = cc[h]
                cg_ref[h] = cg[h]

        def step(i, _):
            kb = unroll * i
            tiles([kb + u for u in range(unroll)], False)
            return 0

        lax.fori_loop(0, qb // unroll, step, 0)
        for left in range(1, unroll):

            @pl.when(qb % unroll == left)
            def _(left=left):
                tiles([qb - left + u for u in range(left)], False)

        tiles([qb], True)
        dq_ref[...] = dqacc_ref[0] + dqacc_ref[1]

    qblk = pl.BlockSpec((tb, LANES), lambda p, i: (i, p))
    whole = pl.BlockSpec((t, LANES), lambda p, i: (0, p))
    return pl.pallas_call(
        body,
        name=name,
        grid=(n_pairs, t // tb),
        in_specs=[pl.BlockSpec((tb, LANES), lambda p, i: (i, dy0 + p)), qblk, whole, whole, qblk,
                  pl.BlockSpec(order_after.shape, lambda p, i: (0, 0))],
        out_specs=[qblk, whole, whole],
        out_shape=[jax.ShapeDtypeStruct((t, ATTN_W), F32)] * 3,
        scratch_shapes=[pltpu.VMEM((2, tb, LANES), F32)] * 3,
        compiler_params=_params("parallel", "arbitrary"),
    )(dmix, qs, kn, v, lt, order_after)


_SGU_CHUNKS_PER_STEP = 4


def _sgu_rows(t):
    return CHUNK * math.gcd(_SGU_CHUNKS_PER_STEP, t // CHUNK)


def _sgu_weights(w_ref):
    tt = lax.broadcasted_iota(jnp.int32, (CHUNK, CHUNK), 0)
    ss = lax.broadcasted_iota(jnp.int32, (CHUNK, CHUNK), 1)
    tril = ss <= tt
    return [jnp.where(tril, w_ref[gi], 0.0).astype(BF16) for gi in range(SGU_HEADS)], tril


def _sgu_fwd(proj, g_v, w_s, b_exp, *, name):
    t = proj.shape[0]
    u0 = (3 * CONV_W + 3 * ATTN_W) // SGU_W
    rows = _sgu_rows(t)

    def body(u_ref, v_ref, g_ref, w_ref, b_ref, o_ref):
        grp = lax.broadcasted_iota(jnp.int32, (1, SGU_W), 1) // HEAD_DIM
        wm, _ = _sgu_weights(w_ref)
        gain, bias = g_ref[...], b_ref[...]
        for c in range(rows // CHUNK):
            chunk = pl.ds(c * CHUNK, CHUNK)
            u = _gelu(u_ref[chunk, :])
            vv = _gelu(v_ref[chunk, :])
            vn = (vv * lax.rsqrt(_group_mean(vv * vv, SGU_W) + EPS) * gain).astype(BF16)
            sv = bias
            for gi in range(SGU_HEADS):
                sv = sv + jnp.dot(wm[gi], jnp.where(grp == gi, vn, jnp.zeros_like(vn)), preferred_element_type=F32)
            o_ref[chunk, :] = (u * sv).astype(o_ref.dtype)

    return pl.pallas_call(
        body,
        name=name,
        grid=(t // rows,),
        in_specs=[
            pl.BlockSpec((rows, SGU_W), lambda i: (i, u0)),
            pl.BlockSpec((rows, SGU_W), lambda i: (i, u0 + 1)),
            pl.BlockSpec((1, SGU_W), lambda i: (0, 0)),
            pl.BlockSpec((SGU_HEADS, CHUNK, CHUNK), lambda i: (0, 0, 0)),
            pl.BlockSpec((CHUNK, SGU_W), lambda i: (0, 0)),
        ],
        out_specs=pl.BlockSpec((rows, SGU_W), lambda i: (i, 0)),
        out_shape=jax.ShapeDtypeStruct((t, SGU_W), BF16),
        compiler_params=_params("parallel"),
    )(proj, proj, g_v, w_s, b_exp)


def _sgu_bwd(dmix, proj, g_v, w_s, b_exp, *, name):
    t = proj.shape[0]
    u0 = (3 * CONV_W + 3 * ATTN_W) // SGU_W
    dy0 = (CONV_W + ATTN_W) // SGU_W
    rows = _sgu_rows(t)

    def body(dy_ref, u_ref, v_ref, g_ref, w_ref, b_ref, o_ref, dg_ref, dw_ref, db_ref):
        grp = lax.broadcasted_iota(jnp.int32, (1, SGU_W), 1) // HEAD_DIM
        gain, bias = g_ref[...], b_ref[...]
        wm, tril = _sgu_weights(w_ref)

        @pl.when(pl.program_id(0) == 0)
        def _():
            dg_ref[...] = jnp.zeros_like(dg_ref)
            dw_ref[...] = jnp.zeros_like(dw_ref)
            db_ref[...] = jnp.zeros_like(db_ref)

        dg = jnp.zeros_like(gain)
        db = jnp.zeros_like(bias)
        dw = [jnp.zeros((CHUNK, CHUNK), F32) for _ in range(SGU_HEADS)]
        for c in range(rows // CHUNK):
            chunk = pl.ds(c * CHUNK, CHUNK)
            cu, cv = u_ref[chunk, :], v_ref[chunk, :]
            u = _gelu(cu)
            vv = _gelu(cv)
            r = lax.rsqrt(_group_mean(vv * vv, SGU_W) + EPS)
            xhat = vv * r
            vn = (xhat * gain).astype(BF16)
            vng = [jnp.where(grp == gi, vn, jnp.zeros_like(vn)) for gi in range(SGU_HEADS)]
            sv = bias
            for gi in range(SGU_HEADS):
                sv = sv + jnp.dot(wm[gi], vng[gi], preferred_element_type=F32)
            dy = dy_ref[chunk, :]
            o_ref[chunk, 0:SGU_W] = (dy * sv * _gelu_grad(cu)).astype(o_ref.dtype)
            dsv = dy * u
            dsv16 = dsv.astype(BF16)
            db = db + dsv
            dvn = jnp.zeros_like(dsv)
            for gi in range(SGU_HEADS):
                dw[gi] = dw[gi] + lax.dot_general(dsv16, vng[gi], _NT, preferred_element_type=F32)
                dvn_g = lax.dot_general(wm[gi], dsv16, _TN, preferred_element_type=F32)
                dvn = jnp.where(grp == gi, dvn_g, dvn)
            dg = dg + jnp.sum(dvn * xhat, axis=0, keepdims=True)
            dxhat = dvn * gain
            dvv = r * (dxhat - xhat * _group_mean(dxhat * xhat, SGU_W))
            o_ref[chunk, SGU_W:2 * SGU_W] = (dvv * _gelu_grad(cv)).astype(o_ref.dtype)
        dg_ref[...] += dg
        db_ref[...] += db
        for gi in range(SGU_HEADS):
            dw_ref[gi] += jnp.where(tril, dw[gi], 0.0)

    return pl.pallas_call(
        body,
        name=name,
        grid=(t // rows,),
        in_specs=[
            pl.BlockSpec((rows, SGU_W), lambda i: (i, dy0)),
            pl.BlockSpec((rows, SGU_W), lambda i: (i, u0)),
            pl.BlockSpec((rows, SGU_W), lambda i: (i, u0 + 1)),
            pl.BlockSpec((1, SGU_W), lambda i: (0, 0)),
            pl.BlockSpec((SGU_HEADS, CHUNK, CHUNK), lambda i: (0, 0, 0)),
            pl.BlockSpec((CHUNK, SGU_W), lambda i: (0, 0)),
        ],
        out_specs=[
            pl.BlockSpec((rows, 2 * SGU_W), lambda i: (i, 0)),
            pl.BlockSpec((1, SGU_W), lambda i: (0, 0)),
            pl.BlockSpec((SGU_HEADS, CHUNK, CHUNK), lambda i: (0, 0, 0)),
            pl.BlockSpec((CHUNK, SGU_W), lambda i: (0, 0)),
        ],
        out_shape=[
            jax.ShapeDtypeStruct((t, 2 * SGU_W), BF16),
            jax.ShapeDtypeStruct((1, SGU_W), F32),
            jax.ShapeDtypeStruct((SGU_HEADS, CHUNK, CHUNK), F32),
            jax.ShapeDtypeStruct((CHUNK, SGU_W), F32),
        ],
        compiler_params=_params("arbitrary"),
    )(dmix, proj, proj, g_v, w_s, b_exp)


def _ple_bwd(dh, gate, pp, order_after, *, name, br=512):
    t, d = dh.shape
    br = min(br, t)

    def body(dh_ref, g_ref, p_ref, order_ref, dpre_ref, dpp_ref):
        dhv, g = dh_ref[...], g_ref[...]
        dpre_ref[...] = (dhv * p_ref[...] * g * (1.0 - g)).astype(dpre_ref.dtype)
        dpp_ref[...] = (dhv * g).astype(dpp_ref.dtype)

    row = pl.BlockSpec((br, d), lambda i: (i, 0))
    return pl.pallas_call(
        body,
        name=name,
        grid=(t // br,),
        in_specs=[row] * 3 + [pl.BlockSpec(order_after.shape, lambda i: (0, 0))],
        out_specs=[row] * 2,
        out_shape=[jax.ShapeDtypeStruct((t, d), BF16)] * 2,
        compiler_params=_params("parallel"),
    )(dh, gate, pp, order_after)


def _loss_head(y, target, *, name, br=512):
    t, d = y.shape
    br = min(br, t)

    def body(y_ref, t_ref, dy_ref, loss_ref):
        err = y_ref[...] - t_ref[...]
        dy_ref[...] = err * (1.0 / d)

        @pl.when(pl.program_id(0) == 0)
        def _():
            loss_ref[...] = jnp.zeros_like(loss_ref)

        loss_ref[...] += 0.5 * jnp.sum(jnp.sum(err * err, axis=1, keepdims=True) * (1.0 / d), axis=0, keepdims=True)

    row = pl.BlockSpec((br, d), lambda i: (i, 0))
    return pl.pallas_call(
        body,
        name=name,
        grid=(t // br,),
        in_specs=[row, row],
        out_specs=[row, pl.BlockSpec((8, LANES), lambda i: (0, 0))],
        out_shape=[jax.ShapeDtypeStruct((t, d), F32), jax.ShapeDtypeStruct((8, LANES), F32)],
        compiler_params=_params("arbitrary"),
    )(y, target)


def _adamw_update(w, g, m, v):
    nm = ADAM_B1 * m + (1.0 - ADAM_B1) * g
    nv = ADAM_B2 * v + (1.0 - ADAM_B2) * (g * g)
    m_hat = nm / (1.0 - ADAM_B1 ** ADAM_STEP)
    v_hat = nv / (1.0 - ADAM_B2 ** ADAM_STEP)
    return -ADAM_LR * (m_hat / (jnp.sqrt(v_hat) + ADAM_EPS) + ADAM_WD * w), nm, nv


def _adamw(w, g, m, v, *, name, br=512):
    r, c = w.shape
    br = _row_block(r, br)

    def body(w_ref, g_ref, m_ref, v_ref, d_ref, nm_ref, nv_ref):
        d_ref[...], nm_ref[...], nv_ref[...] = _adamw_update(w_ref[...], g_ref[...], m_ref[...], v_ref[...])

    row = pl.BlockSpec((br, c), lambda i: (i, 0))
    return pl.pallas_call(
        body,
        name=name,
        grid=(r // br,),
        in_specs=[row] * 4,
        out_specs=[row] * 3,
        out_shape=[jax.ShapeDtypeStruct((r, c), F32)] * 3,
        compiler_params=_params("parallel"),
    )(w, g, m, v)


def _sum_slots(per_layer, *, name):
    counts = [len(arrays) for arrays in per_layer]
    flat = [a for arrays in per_layer for a in arrays]

    def body(*refs):
        ins, outs = refs[:len(flat)], refs[len(flat):]
        at = 0
        for o_ref, count in zip(outs, counts, strict=True):
            for li in range(count):
                acc = ins[at + li][0]
                for j in range(1, N_DEV):
                    acc = acc + ins[at + li][j]
                o_ref[li] = acc
            at += count

    return pl.pallas_call(
        body,
        name=name,
        out_shape=[jax.ShapeDtypeStruct((len(arrays), *arrays[0].shape[1:]), F32) for arrays in per_layer],
        compiler_params=pltpu.CompilerParams(vmem_limit_bytes=VMEM_LIMIT_BYTES),
    )(*flat)


_ADAMW_BLOCK_ELEMS = 192 * 1024


def _adamw_reduce(w, arrived, m, v, *, name):
    depth, r, c = w.shape
    br = _row_block(r, max(BF16_TILE_ROWS, _ADAMW_BLOCK_ELEMS // (-(-c // LANES) * LANES)))

    def body(w_ref, m_ref, v_ref, *rest):
        parts, (g_ref, d_ref, nm_ref, nv_ref) = rest[:depth], rest[depth:]
        for li in range(depth):

            @pl.when(pl.program_id(0) == li)
            def _(li=li):
                g = parts[li][0].astype(F32)
                for j in range(1, N_DEV):
                    g = g + parts[li][j].astype(F32)
                g_ref[...] = g
                d_ref[...], nm_ref[...], nv_ref[...] = _adamw_update(w_ref[...], g, m_ref[...], v_ref[...])

    cur = pl.BlockSpec((None, br, c), lambda l, i: (l, i, 0))
    slots = [pl.BlockSpec((N_DEV, br, c), lambda l, i, li=li: (0, jnp.where(l == li, i, 0), 0)) for li in range(depth)]
    return pl.pallas_call(
        body,
        name=name,
        grid=(depth, r // br),
        in_specs=[cur, cur, cur] + slots,
        out_specs=[cur] * 4,
        out_shape=[jax.ShapeDtypeStruct((depth, r, c), F32)] * 4,
        compiler_params=_params("arbitrary", "arbitrary"),
    )(w, m, v, *arrived)


def _my_place():
    return lax.axis_index("x"), lax.axis_index("y"), lax.axis_index("c")


def _flip(v, bit):
    return 1 - v if bit else v


def _slot_of(px, py, pc):
    return 4 * px + 2 * py + pc


_ANY = pl.BlockSpec(memory_space=pl.ANY)


_HBM = pl.BlockSpec(memory_space=pltpu.HBM)
_SEM = pl.BlockSpec(memory_space=pltpu.SEMAPHORE)
_DATAFLOW = pltpu.SideEffectType.DATAFLOW_SIDE_EFFECTING


_GATHER, _GATHER_COLUMNS, _SCATTER = "gather", "gather_columns", "scatter"


def _landing_shape(a, mode):
    if mode == _SCATTER:
        return a.shape
    if mode == _GATHER_COLUMNS:
        return (a.shape[0], N_DEV * a.shape[1])
    return (N_DEV, *a.shape)


_DIRECT, _NEAR, _RELAY = "direct", "near", "relay"
_OTHER_CHIPS = (2, 4, 6)
_SIBLING = 1


def _exchange_copies(src_refs, land_refs, send_sem, recv_sem, modes, hops=_DIRECT):
    mx, my, mc = _my_place()
    peer_of = lambda k: (_flip(mx, k & 4), _flip(my, k & 2), _flip(mc, k & 1))
    mine = _slot_of(mx, my, mc)

    def block(land, mode, slot):
        if mode == _GATHER_COLUMNS:
            n = land.shape[1] // N_DEV
            return land.at[:, pl.ds(pl.multiple_of(slot * n, LANES), n)]
        return land.at[slot]

    def remote_copy(src, dst, to):
        return pltpu.make_async_remote_copy(src_ref=src, dst_ref=dst, send_sem=send_sem, recv_sem=recv_sem,
                                            device_id=to, device_id_type=MESH)

    remote, local = [], []
    for src, land, mode in zip(src_refs, land_refs, modes, strict=True):
        if hops == _RELAY:
            assert mode != _SCATTER
            for k in _OTHER_CHIPS:
                came = block(land, mode, _slot_of(*peer_of(k)))
                remote.append(remote_copy(came, came, peer_of(_SIBLING)))
            continue
        dst = block(land, mode, mine)
        for k in ((_SIBLING,) + _OTHER_CHIPS if hops == _NEAR else range(1, N_DEV)):
            remote.append(remote_copy(src.at[_slot_of(*peer_of(k))] if mode == _SCATTER else src, dst, peer_of(k)))
        local.append(pltpu.make_async_copy(src.at[mine] if mode == _SCATTER else src, dst, recv_sem))
    return remote, local


def _wait_copies(remote, local):
    for cp in remote:
        cp.wait_send()
        cp.wait_recv()
    for cp in local:
        cp.wait()


def _exchange_start(groups, after, *, name, hops=_DIRECT):
    sizes = [len(srcs) for srcs, _ in groups]
    n, n_sems = sum(sizes), 2 * len(groups)
    srcs = [a for arrays, _ in groups for a in arrays]
    lands = [lax.empty(_landing_shape(a, mode), a.dtype)
             for arrays, modes in groups for a, mode in zip(arrays, modes, strict=True)]
    offsets = [sum(sizes[:g]) for g in range(len(groups))]

    def body(*refs):
        sems = refs[2 * n + 1:2 * n + 1 + n_sems]
        for g, (off, size, (_, modes)) in enumerate(zip(offsets, sizes, groups)):
            remote, local = _exchange_copies(refs[off:off + size], refs[n + off:n + off + size], sems[2 * g],
                                             sems[2 * g + 1], modes, hops)
            for cp in remote + local:
                cp.start()
        refs[-1][...] = jnp.zeros_like(refs[-1])

    thru = [pltpu.HBM(a.shape, a.dtype) for a in (*srcs, *lands)]
    out = pl.pallas_call(
        body,
        name=name,
        in_specs=[_HBM] * (2 * n) + [_ANY],
        out_specs=(*[_SEM] * n_sems, *[_HBM] * (2 * n), pl.BlockSpec(memory_space=pltpu.VMEM)),
        out_shape=(*[pltpu.SemaphoreType.DMA(())] * n_sems, *thru, jax.ShapeDtypeStruct((8, LANES), F32)),
        input_output_aliases={i: n_sems + i for i in range(2 * n)},
        compiler_params=pltpu.CompilerParams(has_side_effects=_DATAFLOW),
    )(*[pltpu.with_memory_space_constraint(a, pltpu.HBM) for a in (*srcs, *lands)], after)
    sems, arrays = out[:n_sems], out[n_sems:-1]
    started = [(sems[2 * g], sems[2 * g + 1], *arrays[off:off + size], *arrays[n + off:n + off + size])
               for g, (off, size) in enumerate(zip(offsets, sizes))]
    return started, out[-1]


def _exchange_relay(started, after, *, modes, regroup, name):
    send_sem, recv_sem, *thru = started
    n, n_sems = len(thru) // 2, 2 * len(regroup)

    def body(*refs):
        srcs, lands = refs[:n], refs[n:2 * n]
        _wait_copies(*_exchange_copies(srcs, lands, refs[2 * n], refs[2 * n + 1], modes, _NEAR))
        sems = refs[2 * n + 3:2 * n + 3 + n_sems]
        for g, members in enumerate(regroup):
            remote, _ = _exchange_copies([srcs[i] for i in members], [lands[i] for i in members], sems[2 * g],
                                         sems[2 * g + 1], [modes[i] for i in members], _RELAY)
            for cp in remote:
                cp.start()
        refs[-1][...] = jnp.zeros_like(refs[-1])

    out = pl.pallas_call(
        body,
        name=name,
        in_specs=[_HBM] * (2 * n) + [_SEM, _SEM, _ANY],
        out_specs=(*[_SEM] * n_sems, *[_HBM] * (2 * n), pl.BlockSpec(memory_space=pltpu.VMEM)),
        out_shape=(*[pltpu.SemaphoreType.DMA(())] * n_sems, *[pltpu.HBM(a.shape, a.dtype) for a in thru],
                   jax.ShapeDtypeStruct((8, LANES), F32)),
        input_output_aliases={i: n_sems + i for i in range(2 * n)},
        compiler_params=pltpu.CompilerParams(has_side_effects=_DATAFLOW),
    )(*thru, send_sem, recv_sem, after)
    sems, arrays = out[:n_sems], out[n_sems:-1]
    groups = [(sems[2 * g], sems[2 * g + 1], *[arrays[i] for i in members], *[arrays[n + i] for i in members])
              for g, members in enumerate(regroup)]
    return groups, out[-1]


def _exchange_wait(started, after, *, modes, name, hops=_DIRECT):
    send_sem, recv_sem, *thru = started
    n = len(thru) // 2

    def body(*refs):
        _wait_copies(*_exchange_copies(refs[:n], refs[n:2 * n], refs[2 * n], refs[2 * n + 1], modes, hops))

    out = pl.pallas_call(
        body,
        name=name,
        in_specs=[_HBM] * (2 * n) + [_SEM, _SEM, _ANY],
        out_specs=[_HBM] * (2 * n),
        out_shape=[pltpu.HBM(a.shape, a.dtype) for a in thru],
        input_output_aliases={i: i for i in range(2 * n)},
        compiler_params=pltpu.CompilerParams(has_side_effects=_DATAFLOW),
    )(*thru, send_sem, recv_sem, after)
    return out[n:]


def _gather_columns(g):
    return jnp.moveaxis(g, 0, 1).reshape(g.shape[1], -1)


def _split_rows(w):
    return w.reshape(N_DEV, w.shape[0] // N_DEV, w.shape[1])


_FIRST = ("w_in", "conv_w")
_REST = ("w_out", "w_ff1", "w_ff2", "w_ple_gate", "w_ple_proj")
_REST_GROUPS = (("w_out",), ("w_ff1",), ("w_ff2",), ("w_ple_gate", "w_ple_proj"))
_BIG = ("w_in",) + _REST
_GATHER_MODE = dict(w_in=_GATHER, conv_w=_GATHER, w_out=_GATHER, w_ff1=_GATHER_COLUMNS, w_ff2=_GATHER,
                    w_ple_gate=_GATHER, w_ple_proj=_GATHER_COLUMNS)
_RELAYOUT_AFTER_GATHER = ("conv_w",)
_SMALL = ("norm1_g", "q_norm_g", "k_norm_g", "sgu_norm_g", "sgu_w", "sgu_b", "norm2_g", "norm3_g")
_ORDER = ("norm1_g", "w_in", "conv_w", "q_norm_g", "k_norm_g", "sgu_norm_g", "sgu_w", "sgu_b", "w_out", "norm2_g",
          "w_ff1", "w_ff2", "norm3_g", "w_ple_gate", "w_ple_proj")


def _whole_matrices(names, landed):
    return {k: _gather_columns(g) if k in _RELAYOUT_AFTER_GATHER else g.reshape(-1, g.shape[-1])
            for k, g in zip(names, landed, strict=True)}


_NORM_FUSED_ROWS = 512


def _layer_forward(h0, hn1, p16, s, li, w_first, gathered, next_norm_g):
    nm = lambda k: f"{k}_l{li}"
    t, d = h0.shape
    w = dict(w_first)

    def add_and_norm(acc, res, gain):
        h = res + acc
        return h, _rms_rows(h, gain)

    proj = _matmul(hn1, w["w_in"], name=nm("proj"), tb=True, bm=t, bn=256)
    y_a = _conv_fwd(proj, w["conv_w"], name=nm("conv"))
    qs, kn, v = _qk_prep(proj, s["gq"], s["gk"], name=nm("qkprep"))
    y_b, lt = _attn_fwd(qs, kn, v, name=nm("attn"))
    gathered["relay_rest"](y_b)
    y_c = _sgu_fwd(proj, s["sgu_norm_g"], s["sgu_w"], s["b_exp"], name=nm("sgu"))
    mix = jnp.concatenate([y_a, y_b, y_c], axis=1)
    w.update(gathered["fetch"](0, mix))
    h1, hn2 = _matmul(mix, w["w_out"], name=nm("out"), bm=_NORM_FUSED_ROWS, bn=d, out_dtypes=(F32, BF16),
                      extras=(h0,), row_vectors=(s["norm2_g"],), epilogue=add_and_norm)
    w.update(gathered["fetch"](1, hn2))
    f = _matmul(hn2, w["w_ff1"], name=nm("ff1"), bm=t, bn=512, out_dtypes=(BF16,),
                epilogue=lambda acc: (jnp.square(jnp.maximum(acc, 0.0)),))
    w.update(gathered["fetch"](2, f))
    gathered["relay_next"](f)
    h2, hn3 = _matmul(f, w["w_ff2"], name=nm("ff2"), bm=_NORM_FUSED_ROWS, bn=d, out_dtypes=(F32, BF16),
                      extras=(h1,), row_vectors=(s["norm3_g"],), epilogue=add_and_norm)
    w.update(gathered["fetch"](3, hn3))
    w_next = gathered["fetch_next"](hn3)
    pp = _matmul(p16, w["w_ple_proj"], name=nm("pleproj"), bm=t, bn=512)

    def gate_epilogue(acc, pp_blk, h_blk, *gain):
        gate = jax.nn.sigmoid(acc)
        h = h_blk + gate * pp_blk
        return (h, gate) + tuple(_rms_rows(h, g) for g in gain)

    fused_norm = () if next_norm_g is None else (next_norm_g,)
    h3, gate, *hn1_next = _matmul(hn3, w["w_ple_gate"], name=nm("plegate"), bm=_NORM_FUSED_ROWS, bn=d,
                                  out_dtypes=(F32, F32) + (BF16,) * len(fused_norm), extras=(pp, h2),
                                  row_vectors=fused_norm, epilogue=gate_epilogue)
    saved = dict(h0=h0, hn1=hn1, proj=proj, qs=qs, kn=kn, v=v, lt=lt, mix=mix, h1=h1, hn2=hn2, f=f, h2=h2,
                 hn3=hn3, pp=pp, gate=gate, p16=p16)
    return h3, (hn1_next[0] if hn1_next else None), w, w_next, saved


def _layer_backward(dh3, a, w, s, li, order_after, start_rest):
    nm = lambda k: f"{k}_bwd_l{li}"
    t = dh3.shape[0]
    dpre, dpp = _ple_bwd(dh3, a["gate"], a["pp"], order_after, name=nm("ple"))
    g_gate = _weight_grad(a["hn3"], [dpre], name=nm("dwgate"))
    g_proj = _weight_grad(a["p16"], [dpp], name=nm("dwproj"), column_shards=True)
    dh2, dh2_16, g_n3 = _matmul_rms_bwd([dpre], w["w_ple_gate"], a["h2"], s["norm3_g"], dh3, name=nm("dh2"))
    du = _matmul(dh2_16, w["w_ff2"], name=nm("du"), tb=True, bm=t, bn=512, out_dtypes=(BF16,), extras=(a["f"],),
                 epilogue=lambda acc, f: (acc * (2.0 * jnp.sqrt(f.astype(F32))),))
    g_ff2 = _weight_grad(a["f"], [dh2_16], name=nm("dwff2"))
    g_ff1 = _weight_grad(a["hn2"], [du], name=nm("dwff1"), column_shards=True)
    dh1, dh1_16, g_n2 = _matmul_rms_bwd([du], w["w_ff1"], a["h1"], s["norm2_g"], dh2, name=nm("dh1"))
    dmix = _matmul(dh1_16, w["w_out"], name=nm("dmix"), tb=True, bm=t, bn=256)
    g_out = _weight_grad(a["mix"], [dh1_16], name=nm("dwout"))
    started = start_rest(dict(w_out=_split_rows(g_out), w_ff1=g_ff1, w_ff2=_split_rows(g_ff2),
                              w_ple_gate=_split_rows(g_gate), w_ple_proj=g_proj), dmix)
    d_conv, g_conv = _conv_bwd(dmix, a["proj"], w["conv_w"], name=nm("conv"))
    dqs, dkn, dv = _attn_bwd(dmix, a["qs"], a["kn"], a["v"], a["lt"], started, name=nm("attn"))
    d_qkv, g_q, g_k = _qk_prep_bwd(dqs, dkn, dv, a["proj"], s["gq"], s["gk"], name=nm("qkprep"))
    d_sgu, g_sn, g_sw, g_sb = _sgu_bwd(dmix, a["proj"], s["sgu_norm_g"], s["sgu_w"], s["b_exp"], name=nm("sgu"))
    dproj = [d_conv, d_qkv, d_sgu]
    g_in = jnp.concatenate([_weight_grad(piece, [a["hn1"]], name=nm(f"dwin{i}")) for i, piece in enumerate(dproj)])
    dh0, _, g_n1 = _matmul_rms_bwd(dproj, w["w_in"], a["h0"], s["norm1_g"], dh1, name=nm("dh0"), w_is_k_by_d=True)
    small = dict(norm1_g=g_n1, norm2_g=g_n2, norm3_g=g_n3, q_norm_g=g_q, k_norm_g=g_k, sgu_norm_g=g_sn, sgu_w=g_sw,
                 sgu_b=g_sb, conv_w=g_conv)
    return dh0, _split_rows(g_in), small


def _small_gradients(raw, depth):
    return dict(
        norm1_g=raw["norm1_g"].reshape(depth, -1), norm2_g=raw["norm2_g"].reshape(depth, -1),
        norm3_g=raw["norm3_g"].reshape(depth, -1),
        q_norm_g=raw["q_norm_g"].reshape(depth, -1, HEAD_DIM).sum(1),
        k_norm_g=raw["k_norm_g"].reshape(depth, -1, HEAD_DIM).sum(1),
        sgu_norm_g=raw["sgu_norm_g"].reshape(depth, -1), sgu_w=raw["sgu_w"],
        sgu_b=jnp.swapaxes(raw["sgu_b"].reshape(depth, CHUNK, SGU_HEADS, HEAD_DIM).sum(-1), 1, 2),
        conv_w=raw["conv_w"][:, :CONV_TAPS],
    )


def kernel(x, p, norm1_g, w_in, conv_w, q_norm_g, k_norm_g, sgu_norm_g, sgu_w, sgu_b, w_out, norm2_g, w_ff1, w_ff2, norm3_g, w_ple_gate, w_ple_proj, loss_target, m_norm1_g, m_w_in, m_conv_w, m_q_norm_g, m_k_norm_g, m_sgu_norm_g, m_sgu_w, m_sgu_b, m_w_out, m_norm2_g, m_w_ff1, m_w_ff2, m_norm3_g, m_w_ple_gate, m_w_ple_proj, v_norm1_g, v_w_in, v_conv_w, v_q_norm_g, v_k_norm_g, v_sgu_norm_g, v_sgu_w, v_sgu_b, v_w_out, v_norm2_g, v_w_ff1, v_w_ff2, v_norm3_g, v_w_ple_gate, v_w_ple_proj):
    weights = dict(norm1_g=norm1_g, w_in=w_in, conv_w=conv_w, q_norm_g=q_norm_g, k_norm_g=k_norm_g,
                   sgu_norm_g=sgu_norm_g, sgu_w=sgu_w, sgu_b=sgu_b, w_out=w_out, norm2_g=norm2_g, w_ff1=w_ff1,
                   w_ff2=w_ff2, norm3_g=norm3_g, w_ple_gate=w_ple_gate, w_ple_proj=w_ple_proj)
    mom = dict(norm1_g=m_norm1_g, w_in=m_w_in, conv_w=m_conv_w, q_norm_g=m_q_norm_g, k_norm_g=m_k_norm_g,
               sgu_norm_g=m_sgu_norm_g, sgu_w=m_sgu_w, sgu_b=m_sgu_b, w_out=m_w_out, norm2_g=m_norm2_g, w_ff1=m_w_ff1,
               w_ff2=m_w_ff2, norm3_g=m_norm3_g, w_ple_gate=m_w_ple_gate, w_ple_proj=m_w_ple_proj)
    var = dict(norm1_g=v_norm1_g, w_in=v_w_in, conv_w=v_conv_w, q_norm_g=v_q_norm_g, k_norm_g=v_k_norm_g,
               sgu_norm_g=v_sgu_norm_g, sgu_w=v_sgu_w, sgu_b=v_sgu_b, w_out=v_w_out, norm2_g=v_norm2_g, w_ff1=v_w_ff1,
               w_ff2=v_w_ff2, norm3_g=v_norm3_g, w_ple_gate=v_w_ple_gate, w_ple_proj=v_w_ple_proj)
    for params in (weights, mom, var):
        params["w_in"] = jnp.swapaxes(params["w_in"], 1, 2)
    depth = norm1_g.shape[0]
    mx, my, mc = _my_place()
    me = _slot_of(mx, my, mc)

    gathers = []
    modes_of = lambda names: tuple(_GATHER_MODE[k] for k in names)
    token = x[0, :8, :LANES]
    for li in range(depth):
        groups = [([weights[k][li] if k == "conv_w" else weights[k][li].astype(BF16) for k in names], modes_of(names))
                  for names in (_FIRST, _REST)]
        started, token = _exchange_start(groups, token, name=f"gather_start_l{li}", hops=_NEAR)
        gathers.append(started)

    small = []
    for li in range(depth):
        small.append(dict(
            norm1_g=norm1_g[li][None], norm2_g=norm2_g[li][None], norm3_g=norm3_g[li][None],
            gq=jnp.tile(q_norm_g[li], _QK_BLOCK // HEAD_DIM)[None], gk=jnp.tile(k_norm_g[li], _QK_BLOCK // HEAD_DIM)[None],
            sgu_norm_g=sgu_norm_g[li][None], sgu_w=sgu_w[li], b_exp=jnp.repeat(sgu_b[li].T, HEAD_DIM, axis=1),
        ))
    small[0]["norm1_g"] = small[0]["norm1_g"] + token[0, 0]

    h = x[0]
    saved, full = [], []
    relayed_first, relayed_rest = [None] * depth, [None] * depth
    rest_members = [[_REST.index(k) for k in names] for names in _REST_GROUPS]

    def relay_first(li, after):
        if li < depth:
            (relayed_first[li],), _ = _exchange_relay(gathers[li][0], after, modes=modes_of(_FIRST),
                                                      regroup=[list(range(len(_FIRST)))], name=f"gather_first_relay_l{li}")

    def fetch_first(li, after):
        if li == depth:
            return None
        landed = _exchange_wait(relayed_first[li], after, modes=modes_of(_FIRST), hops=_RELAY,
                                name=f"gather_first_wait_l{li}")
        return _whole_matrices(_FIRST, landed)

    hn1 = _rms_fwd(h, small[0]["norm1_g"], name="rms1_l0")
    relay_first(0, hn1)
    w_first = fetch_first(0, hn1)
    for li in range(depth):

        def relay_rest(after, li=li):
            relayed_rest[li], _ = _exchange_relay(gathers[li][1], after, modes=modes_of(_REST), regroup=rest_members,
                                                  name=f"gather_rest_relay_l{li}")

        def fetch(g, after, li=li):
            landed = _exchange_wait(relayed_rest[li][g], after, modes=modes_of(_REST_GROUPS[g]), hops=_RELAY,
                                    name=f"gather_{_REST_GROUPS[g][0]}_wait_l{li}")
            return _whole_matrices(_REST_GROUPS[g], landed)

        gathered = dict(relay_rest=relay_rest, fetch=fetch, relay_next=functools.partial(relay_first, li + 1),
                        fetch_next=functools.partial(fetch_first, li + 1))
        next_norm_g = small[li + 1]["norm1_g"] if li + 1 < depth else None
        h, hn1, w, w_first, acts = _layer_forward(h, hn1, p[li, 0].astype(BF16), small[li], li, w_first, gathered,
                                                  next_norm_g)
        full.append(w)
        saved.append(acts)
    dh, loss_tile = _loss_head(h, loss_target[0], name="loss_head")
    loss = lax.psum(loss_tile[0, 0], ("x", "y", "c"))

    small_names = _SMALL + ("conv_w",)
    scatter_first, scatter_rest = [None] * depth, [None] * depth
    first_modes, rest_modes = (_SCATTER,) + (_GATHER,) * len(small_names), (_SCATTER,) * len(_REST)
    token = loss_tile
    for li in reversed(range(depth)):

        def start_rest(parts, after, li=li):
            (scatter_rest[li],), started = _exchange_start([([parts[k] for k in _REST], rest_modes)], after,
                                                           name=f"scatter_rest_start_l{li}")
            return started

        dh, g_in, small_grads = _layer_backward(dh, saved[li], full[li], small[li], li, token, start_rest)
        (scatter_first[li],), token = _exchange_start(
            [([g_in] + [small_grads[k] for k in small_names], first_modes)], dh, name=f"scatter_first_start_l{li}")
    grad_x = dh[None]

    grads, delta, new_m, new_v = {}, {}, {}, {}
    arrived = {k: [None] * depth for k in _BIG}
    for li in reversed(range(depth)):
        landed = _exchange_wait(scatter_rest[li], token, modes=rest_modes, name=f"scatter_rest_wait_l{li}")
        for k, g in zip(_REST, landed, strict=True):
            arrived[k][li] = g
    for k in _REST:
        grads[k], delta[k], new_m[k], new_v[k] = _adamw_reduce(weights[k], arrived[k], mom[k], var[k], name=f"adamw_{k}")
    small_parts = {k: [None] * depth for k in small_names}
    updated = jnp.stack([delta[k][0, 0, :1] for k in _REST])
    for li in reversed(range(depth)):
        arrived["w_in"][li], *parts = _exchange_wait(scatter_first[li], updated, modes=first_modes,
                                                     name=f"scatter_first_wait_l{li}")
        for k, part in zip(small_names, parts, strict=True):
            small_parts[k][li] = part
    grads["w_in"], delta["w_in"], new_m["w_in"], new_v["w_in"] = _adamw_reduce(
        weights["w_in"], arrived["w_in"], mom["w_in"], var["w_in"], name="adamw_w_in")
    for results in (grads, delta, new_m, new_v):
        results["w_in"] = jnp.swapaxes(results["w_in"], 1, 2)
    sums = _sum_slots([small_parts[k] for k in small_names], name="sum_small_grads")
    grads.update(_small_gradients(dict(zip(small_names, sums)), depth))
    n_conv = conv_w.shape[2]
    grads["conv_w"] = lax.dynamic_slice_in_dim(grads["conv_w"], me * n_conv, n_conv, axis=2)
    for k in small_names:
        as_rows = lambda a: a.reshape(-1, a.shape[-1])
        outs = _adamw(as_rows(weights[k]), as_rows(grads[k]), as_rows(mom[k]), as_rows(var[k]), name=f"adamw_{k}")
        delta[k], new_m[k], new_v[k] = (o.reshape(weights[k].shape) for o in outs)

    return (loss, grad_x, *[grads[k] for k in _ORDER], *[delta[k] for k in _ORDER],
            *[new_m[k] for k in _ORDER], *[new_v[k] for k in _ORDER])
```
